```python
import math
import jax, jax.numpy as jnp
from jax import lax
import numpy as np

D_MODEL = 1024
BATCH = 32
SEQ = 256
DEPTH = 2
DEC_BATCH = 2
DEC_SEQ = 4096
PAST_LEN = 256

GRID_W = 64
H_RET = 4
RET_DK = 64
RET_DV = 64
RET_CHUNK = 128
H_DIFF = 4
DIFF_QK = 64
DIFF_V = 2 * DIFF_QK
Q_BLOCK = 128
ROPE_BASE = 10000.0
CONV_CH = 256
MIX_WIDTH = H_RET * RET_DV + H_DIFF * DIFF_V + CONV_CH
IN_SIZES = (H_RET * RET_DK, H_RET * RET_DK, H_RET * RET_DV, H_RET * RET_DV,
            H_DIFF * 2 * DIFF_QK, H_DIFF * 2 * DIFF_QK, H_DIFF * DIFF_V,
            CONV_CH, CONV_CH, CONV_CH)
IN_WIDTH = sum(IN_SIZES)
N_GROUPS = 4
EXPERTS_PER_GROUP = 4
N_EXPERTS = N_GROUPS * EXPERTS_PER_GROUP
TOP_K = 2
D_EXPERT = 512
ALPHA = (2 * DEPTH) ** 0.25
BETA = (8 * DEPTH) ** -0.25
EPS = 1e-5

kernel_name = 'hybrid_retention_diffattn_conv_hmoe_dit_step'


def _layer_norm(x, g, b):
    xf = x.astype(jnp.float32)
    mu = jnp.mean(xf, axis=-1, keepdims=True)
    var = jnp.mean(jnp.square(xf - mu), axis=-1, keepdims=True)
    return ((xf - mu) * lax.rsqrt(var + EPS) * g + b).astype(x.dtype)


def _head_norm(x):
    xf = x.astype(jnp.float32)
    mu = jnp.mean(xf, axis=-1, keepdims=True)
    var = jnp.mean(jnp.square(xf - mu), axis=-1, keepdims=True)
    return ((xf - mu) * lax.rsqrt(var + EPS)).astype(x.dtype)


def _rms_norm(x, g):
    xf = x.astype(jnp.float32)
    return (xf * lax.rsqrt(jnp.mean(jnp.square(xf), axis=-1, keepdims=True) + EPS) * g).astype(x.dtype)


def _split_columns(u):
    parts = []
    start = 0
    for size in IN_SIZES:
        parts.append(u[..., start:start + size])
        start += size
    return parts


def _axial_rope(x, rows, cols):
    d = x.shape[-1]
    half = d // 2
    pairs = half // 2
    inv = 1.0 / (ROPE_BASE ** (jnp.arange(pairs, dtype=jnp.float32) * 2.0 / half))
    xf = x.astype(jnp.float32)

    def rot(xp, pos):
        ang = pos.astype(jnp.float32)[:, None] * inv[None, :]
        cos = jnp.cos(ang)[:, None, None, :]
        sin = jnp.sin(ang)[:, None, None, :]
        x1, x2 = xp[..., :pairs], xp[..., pairs:]
        return jnp.concatenate([x1 * cos - x2 * sin, x1 * sin + x2 * cos], axis=-1)

    out = jnp.concatenate([rot(xf[..., :half], rows), rot(xf[..., half:], cols)], axis=-1)
    return out.astype(x.dtype)


def _retention_scan(q, k, v, log_gamma, s0):
    B, T, H, DK = q.shape
    DV = v.shape[-1]
    C = RET_CHUNK
    n = T // C
    lg = log_gamma.astype(jnp.float32)
    idx = jnp.arange(C, dtype=jnp.float32)
    diff = idx[:, None] - idx[None, :]
    dmask = jnp.exp(jnp.maximum(diff, 0.0)[None] * lg[:, None, None]) * (diff >= 0)[None]
    q_decay = jnp.exp((idx + 1.0)[None, :] * lg[:, None]).T[None, :, :, None]
    k_decay = jnp.exp((C - 1.0 - idx)[None, :] * lg[:, None])
    c_decay = jnp.exp(C * lg)[None, :, None, None]

    def to_chunks(a):
        return a.astype(jnp.float32).reshape(B, n, C, H, a.shape[-1]).transpose(1, 0, 2, 3, 4)

    def step(s, xs):
        qb, kb, vb = xs
        scores = jnp.einsum('bihd,bjhd->bhij', qb, kb) * dmask
        intra = jnp.einsum('bhij,bjhv->bihv', scores, vb)
        inter = jnp.einsum('bihd,bhdv->bihv', qb, s) * q_decay
        s_new = s * c_decay + jnp.einsum('bjhd,bjhv,hj->bhdv', kb, vb, k_decay)
        return s_new, intra + inter

    s_fin, out = lax.scan(step, s0.astype(jnp.float32), (to_chunks(q), to_chunks(k), to_chunks(v)))
    out = out.transpose(1, 0, 2, 3, 4).reshape(B, T, H, DV)
    return out.astype(q.dtype), s_fin


def _bidir_retention(q, k, v, log_gamma, s_fwd0, s_bwd0):
    o_f, s_f = _retention_scan(q, k, v, log_gamma[0], s_fwd0)
    rev = lambda a: jnp.flip(a, axis=1)
    o_b, s_b = _retention_scan(rev(q), rev(k), rev(v), log_gamma[1], s_bwd0)
    return o_f + rev(o_b), s_f, s_b


def _diff_attention(q, k, v, lam):
    B, Tq, H, _, d = q.shape
    dv = v.shape[-1]
    nb = Tq // Q_BLOCK
    qb = q.reshape(B, nb, Q_BLOCK, H, 2, d).transpose(1, 0, 2, 3, 4, 5)
    scale = d ** -0.5

    def one_block(qblk):
        s = jnp.einsum('bqhmd,bkhmd->bhmqk', qblk, k).astype(jnp.float32) * scale
        p = jax.nn.softmax(s, axis=-1)
        a = p[:, :, 0] - lam * p[:, :, 1]
        return jnp.einsum('bhqk,bkhv->bqhv', a, v.astype(jnp.float32)).astype(v.dtype)

    o = lax.map(one_block, qb)
    return o.transpose(1, 0, 2, 3, 4).reshape(B, Tq, H, dv)


def _short_conv(u, w, b):
    up = jnp.pad(u, ((0, 0), (1, 1), (0, 0)))
    return up[:, :-2] * w[0] + up[:, 1:-1] * w[1] + up[:, 2:] * w[2] + b


def _hier_moe(h, wg, bg, we, be, w_gate, w_up, w_down):
    B, T, D = h.shape
    x = h.reshape(-1, D)
    g_logits = (x @ wg + bg).astype(jnp.float32)
    g_prob = jax.nn.softmax(g_logits, axis=-1)
    g_idx = jnp.argmax(g_logits, axis=-1)
    g_w = jnp.max(g_prob, axis=-1, keepdims=True)
    e_logits = (jnp.einsum('nd,dge->nge', x, we) + be).astype(jnp.float32)
    e_sel = jnp.einsum('nge,ng->ne', e_logits, jax.nn.one_hot(g_idx, N_GROUPS, dtype=jnp.float32))
    top_v, top_i = lax.top_k(e_sel, TOP_K)
    e_w = jax.nn.softmax(top_v, axis=-1) * g_w
    flat = g_idx[:, None] * EXPERTS_PER_GROUP + top_i
    gates = jnp.sum(jax.nn.one_hot(flat, N_EXPERTS, dtype=jnp.float32) * e_w[..., None], axis=1)
    hg = jnp.einsum('nd,edf->nef', x, w_gate)
    hu = jnp.einsum('nd,edf->nef', x, w_up)
    act = jax.nn.silu(hg) * hu * gates[:, :, None].astype(x.dtype)
    y = jnp.einsum('nef,efd->nd', act, w_down)
    return y.reshape(B, T, D)


def _trunk_layer(x, mod, layer, p, ctx=None, pos=None):
    B, T, _ = x.shape
    sh1, sc1, g1, sh2, sc2, g2 = jnp.split(mod, 6, axis=-1)
    h = x * (1.0 + sc1) + sh1
    u = h @ p['w_in']
    rq, rk, rv, rg, dq, dk, dv, cb, cc, ch = _split_columns(u)
    rq = rq.reshape(B, T, H_RET, RET_DK)
    rk = rk.reshape(B, T, H_RET, RET_DK) * (RET_DK ** -0.5)
    rv = rv.reshape(B, T, H_RET, RET_DV)
    dq = dq.reshape(B, T, H_DIFF, 2, DIFF_QK)
    dk = dk.reshape(B, T, H_DIFF, 2, DIFF_QK)
    dv = dv.reshape(B, T, H_DIFF, DIFF_V)

    log_gamma = jax.nn.log_sigmoid(p['ret_decay_logit'].astype(jnp.float32))
    lam_init = 0.8 - 0.6 * math.exp(-0.3 * layer)
    lp = p['diff_lambda'].astype(jnp.float32)
    lam = jnp.exp(jnp.sum(lp[0] * lp[1])) - jnp.exp(jnp.sum(lp[2] * lp[3])) + lam_init

    if ctx is None:
        zeros = jnp.zeros((B, H_RET, RET_DK, RET_DV), jnp.float32)
        ret_o, s_f, s_b = _bidir_retention(rq, rk, rv, log_gamma, zeros, zeros)
        diff_o = _diff_attention(dq, dk, dv, lam)
        new_ctx = (dk, dv, s_f, s_b)
    else:
        k_ctx, v_ctx, s_f0, s_b0 = ctx
        ret_o, _, _ = _bidir_retention(rq, rk, rv, log_gamma, s_f0, s_b0)
        rows, cols = pos
        dq = _axial_rope(dq, rows, cols)
        dk = _axial_rope(dk, rows, cols)
        k_all = jnp.concatenate([k_ctx.astype(dk.dtype), dk], axis=1)
        v_all = jnp.concatenate([v_ctx.astype(dv.dtype), dv], axis=1)
        diff_o = _diff_attention(dq, k_all, v_all, lam)
        new_ctx = ()

    ret = _head_norm(ret_o).reshape(B, T, H_RET * RET_DV) * jax.nn.silu(rg)
    diff = (_rms_norm(diff_o, p['diff_subln_g']) * (1.0 - lam_init)).reshape(B, T, H_DIFF * DIFF_V)
    conv = cb * _short_conv(cc * ch, p['conv_w'], p['conv_b'])
    mix = jnp.concatenate([ret, diff, conv], axis=-1) @ p['w_out']
    x = _layer_norm(ALPHA * x + g1 * mix, p['ln_g'][0], p['ln_b'][0])

    h2 = x * (1.0 + sc2) + sh2
    ffn = _hier_moe(h2, p['w_router_group'], p['b_router_group'], p['w_router_expert'],
                    p['b_router_expert'], p['w_gate'], p['w_up'], p['w_down'])
    x = _layer_norm(ALPHA * x + g2 * ffn, p['ln_g'][1], p['ln_b'][1])
    return x, new_ctx


def setup_inputs(seed: int = 0) -> dict:
    key = jax.random.key(seed)
    ks = jax.random.split(key, 32)
    f32 = jnp.float32

    def nrm(k, shape, scale=1.0):
        return jax.random.normal(k, shape, f32) * scale

    base_logit = jnp.log(2.0 ** (5.0 + jnp.arange(H_RET, dtype=f32)) - 1.0)
    return {
        'x_prompt': nrm(ks[0], (BATCH, SEQ, D_MODEL)),
        'x_sample': nrm(ks[1], (DEC_BATCH, DEC_SEQ, D_MODEL)),
        'cache_diff_k': nrm(ks[2], (DEC_BATCH, DEPTH, PAST_LEN, H_DIFF, 2, DIFF_QK)),
        'cache_diff_v': nrm(ks[3], (DEC_BATCH, DEPTH, PAST_LEN, H_DIFF, DIFF_V)),
        'state_ret_fwd': nrm(ks[4], (DEC_BATCH, DEPTH, H_RET, RET_DK, RET_DV)),
        'state_ret_bwd': nrm(ks[5], (DEC_BATCH, DEPTH, H_RET, RET_DK, RET_DV)),
        'c': nrm(ks[6], (DEC_BATCH, D_MODEL)),
        'c_ctx': nrm(ks[7], (D_MODEL,)),
        'w_mod': nrm(ks[8], (DEPTH, D_MODEL, 6 * D_MODEL), 0.5 * D_MODEL ** -0.5),
        'b_mod': nrm(ks[9], (DEPTH, 6 * D_MODEL), 0.02),
        'w_in': nrm(ks[10], (DEPTH, D_MODEL, IN_WIDTH), D_MODEL ** -0.5),
        'ret_decay_logit': base_logit[None, None, :] + nrm(ks[11], (DEPTH, 2, H_RET), 0.1),
        'diff_lambda': nrm(ks[12], (DEPTH, 4, DIFF_QK), 0.1),
        'diff_subln_g': 1.0 + nrm(ks[13], (DEPTH, DIFF_V), 0.02),
        'conv_w': nrm(ks[14], (DEPTH, 3, CONV_CH), 3.0 ** -0.5),
        'conv_b': nrm(ks[15], (DEPTH, CONV_CH), 0.02),
        'w_out': nrm(ks[16], (DEPTH, MIX_WIDTH, D_MODEL), BETA * MIX_WIDTH ** -0.5),
        'ln_g': 1.0 + nrm(ks[17], (DEPTH, 2, D_MODEL), 0.02),
        'ln_b': nrm(ks[18], (DEPTH, 2, D_MODEL), 0.02),
        'w_router_group': nrm(ks[19], (DEPTH, D_MODEL, N_GROUPS), D_MODEL ** -0.5),
        'b_router_group': nrm(ks[20], (DEPTH, N_GROUPS), 0.01),
        'w_router_expert': nrm(ks[21], (DEPTH, D_MODEL, N_GROUPS, EXPERTS_PER_GROUP), D_MODEL ** -0.5),
        'b_router_expert': nrm(ks[22], (DEPTH, N_GROUPS, EXPERTS_PER_GROUP), 0.01),
        'w_gate': nrm(ks[23], (DEPTH, N_EXPERTS, D_MODEL, D_EXPERT), D_MODEL ** -0.5),
        'w_up': nrm(ks[24], (DEPTH, N_EXPERTS, D_MODEL, D_EXPERT), D_MODEL ** -0.5),
        'w_down': nrm(ks[25], (DEPTH, N_EXPERTS, D_EXPERT, D_MODEL), BETA * D_EXPERT ** -0.5),
    }


def reference(x_prompt, x_sample, cache_diff_k, cache_diff_v, state_ret_fwd, state_ret_bwd,
              c, c_ctx, w_mod, b_mod, w_in, ret_decay_logit, diff_lambda, diff_subln_g,
              conv_w, conv_b, w_out, ln_g, ln_b, w_router_group, b_router_group,
              w_router_expert, b_router_expert, w_gate, w_up, w_down):
    n_lat = x_sample.shape[1]
    n_rows = n_lat // GRID_W
    rows = jnp.repeat(jnp.arange(n_rows), GRID_W)
    cols = jnp.arange(n_rows * GRID_W) % GRID_W

    yp = x_prompt
    ys = x_sample
    ks_l, vs_l, sf_l, sb_l = [], [], [], []
    for l in range(DEPTH):
        p = {
            'w_in': w_in[l], 'ret_decay_logit': ret_decay_logit[l], 'diff_lambda': diff_lambda[l],
            'diff_subln_g': diff_subln_g[l], 'conv_w': conv_w[l], 'conv_b': conv_b[l],
            'w_out': w_out[l], 'ln_g': ln_g[l], 'ln_b': ln_b[l],
            'w_router_group': w_router_group[l], 'b_router_group': b_router_group[l],
            'w_router_expert': w_router_expert[l], 'b_router_expert': b_router_expert[l],
            'w_gate': w_gate[l], 'w_up': w_up[l], 'w_down': w_down[l],
        }
        mod_ctx = (jax.nn.silu(c_ctx)[None, :] @ w_mod[l] + b_mod[l])[:, None, :]
        mod_lat = (jax.nn.silu(c) @ w_mod[l] + b_mod[l])[:, None, :]

        yp, (k_l, v_l, s_f, s_b) = _trunk_layer(yp, mod_ctx, l, p)
        ks_l.append(k_l)
        vs_l.append(v_l)
        sf_l.append(s_f)
        sb_l.append(s_b)

        ys, _ = _trunk_layer(ys, mod_lat, l, p,
                             ctx=(cache_diff_k[:, l], cache_diff_v[:, l], state_ret_fwd[:, l], state_ret_bwd[:, l]),
                             pos=(rows, cols))

    new_diff_k = jnp.stack(ks_l, axis=1)
    new_diff_v = jnp.stack(vs_l, axis=1)
    new_ret_fwd = jnp.stack(sf_l, axis=1)
    new_ret_bwd = jnp.stack(sb_l, axis=1)
    return (yp, ys, new_diff_k, new_diff_v, new_ret_fwd, new_ret_bwd)
```

```python
import functools
import math

import numpy as np
import jax
import jax.numpy as jnp
from jax import lax
from jax.experimental import pallas as pl
from jax.experimental.pallas import tpu as pltpu

D_MODEL = 1024
DEPTH = 2
GRID_W = 64
H_RET = 4
RET_DK = 64
RET_W = H_RET * RET_DK
H_DIFF = 4
DIFF_QK = 64
DIFF_V = 2 * DIFF_QK
DIFF_W = H_DIFF * DIFF_V
CONV_CH = 256
ROPE_BASE = 10000.0
N_GROUPS = 4
EXPERTS_PER_GROUP = 4
N_EXPERTS = N_GROUPS * EXPERTS_PER_GROUP
D_EXPERT = 512
ALPHA = (2 * DEPTH) ** 0.25
EPS = 1e-5
MOD_ROWS = 8
ROUTER_LANES = 128
EXPERT_LANE0 = N_GROUPS

TILE = 256
Q_SUB = 128
MOE_TM = 1024

_O_RQ, _O_RK, _O_RV, _O_RG = 0, 256, 512, 768
_O_DQ, _O_DK, _O_DV = 1024, 1536, 2048
_O_CB, _O_CC, _O_CH = 2560, 2816, 3072
IN_WIDTH = 3328

BF = jnp.bfloat16
F32 = jnp.float32

_VMEM_LIMIT = 56 * 1024 * 1024


def _dot(a, b):
    return jnp.dot(a, b, preferred_element_type=F32)


def _dot_nt(a, b):
    return lax.dot_general(a, b, (((1,), (1,)), ((), ())), preferred_element_type=F32)


def _dot_tn(a, b):
    return lax.dot_general(a, b, (((0,), (0,)), ((), ())), preferred_element_type=F32)


def _split(x):
    hi = x.astype(BF)
    lo = (x - hi.astype(F32)).astype(BF)
    return hi, lo


def _dot_hl(x, w_bf16):
    hi, lo = _split(x)
    return _dot(hi, w_bf16) + _dot(lo, w_bf16)


def _dot3(x, w_hi, w_lo):
    hi, lo = _split(x)
    return _dot(hi, w_hi) + (_dot(lo, w_hi) + _dot(hi, w_lo))


def _iota(shape, dim):
    return lax.broadcasted_iota(jnp.int32, shape, dim)


def _log_sigmoid(x):
    return jnp.minimum(x, 0.0) - jnp.log1p(jnp.exp(-jnp.abs(x)))


def _silu(x):
    return x * jax.nn.sigmoid(x)


def _layer_norm(y, g, b):
    mu = jnp.mean(y, axis=-1, keepdims=True)
    yc = y - mu
    var = jnp.mean(yc * yc, axis=-1, keepdims=True)
    return yc * lax.rsqrt(var + EPS) * g + b


def _params(sem):
    return pltpu.CompilerParams(dimension_semantics=sem, vmem_limit_bytes=_VMEM_LIMIT)


def _mod_kernel(c_ref, w_ref, b_ref, o_ref):
    a = _silu(c_ref[...])
    w_hi, w_lo = _split(w_ref[...])
    o_ref[...] = _dot3(a, w_hi, w_lo) + b_ref[...]


def _modulation(cond, w_mod, b_mod):
    tn = 1536
    n6 = 6 * D_MODEL
    return pl.pallas_call(
        _mod_kernel,
        grid=(DEPTH, n6 // tn),
        in_specs=[
            pl.BlockSpec((MOD_ROWS, D_MODEL), lambda l, j: (0, 0)),
            pl.BlockSpec((None, D_MODEL, tn), lambda l, j: (l, 0, j)),
            pl.BlockSpec((None, 1, tn), lambda l, j: (l, 0, j)),
        ],
        out_specs=pl.BlockSpec((None, MOD_ROWS, tn), lambda l, j: (l, 0, j)),
        out_shape=jax.ShapeDtypeStruct((DEPTH, MOD_ROWS, n6), F32),
        compiler_params=_params(("parallel", "parallel")),
        name="modulation",
    )(cond, w_mod, b_mod.reshape(DEPTH, 1, n6))


def _swap16(x):
    n = x.shape[-1]
    lane = _iota(x.shape, 1)
    return jnp.where((lane & 16) == 0, pltpu.roll(x, n - 16, axis=1), pltpu.roll(x, 16, axis=1))


def _proj_kernel(*refs, rope, emit_f32_kv):
    x_ref, mod_ref, w_ref, lg_ref = refs[:4]
    pos = 4
    if rope:
        cos_ref, sin_ref = refs[pos:pos + 2]
        pos += 2
    (rq_ref, rk_ref, rv_ref, rg_ref, dq_ref, dk_ref, dv_ref, cb_ref, cch_ref,
     kvf_ref, kvb_ref) = refs[pos:pos + 11]
    pos += 11
    if emit_f32_kv:
        dk32_ref, dv32_ref = refs[pos:pos + 2]

    m = mod_ref[...]
    sh1 = m[:, 0:D_MODEL]
    sc1 = m[:, D_MODEL:2 * D_MODEL]
    h = (x_ref[...] * (1.0 + sc1) + sh1).astype(BF)

    def col(off, width):
        return _dot(h, w_ref[:, off:off + width])

    rq_ref[...] = col(_O_RQ, RET_W).astype(BF)
    rk = col(_O_RK, RET_W) * (RET_DK ** -0.5)
    rk_ref[...] = rk.astype(BF)
    rv = col(_O_RV, RET_W).astype(BF)
    rv_ref[...] = rv
    rg_ref[...] = col(_O_RG, RET_W)

    lg = _log_sigmoid(lg_ref[...])
    p = _iota((TILE, 1), 0).astype(F32)
    kf = (rk * jnp.exp((TILE - 1.0 - p) * lg[0:1])).astype(BF)
    kb = (rk * jnp.exp(p * lg[1:2])).astype(BF)
    kvf_ref[...] = _dot_tn(kf, rv)
    kvb_ref[...] = _dot_tn(kb, rv)

    dq = col(_O_DQ, DIFF_W)
    dk = col(_O_DK, DIFF_W)
    dv = col(_O_DV, DIFF_W)
    if emit_f32_kv:
        dk32_ref[...] = dk
        dv32_ref[...] = dv
    if rope:
        cos = cos_ref[...]
        sin = sin_ref[...]
        dq = dq * cos + _swap16(dq) * sin
        dk = dk * cos + _swap16(dk) * sin
    dq_ref[...] = (dq * (DIFF_QK ** -0.5)).astype(BF)
    dk_ref[...] = dk.astype(BF)
    dv_ref[...] = dv.astype(BF)

    cb_ref[...] = col(_O_CB, CONV_CH)
    cch_ref[...] = col(_O_CC, CONV_CH) * col(_O_CH, CONV_CH)


def _proj(x, mod_l, row_fn, w_in_b, lg_lanes, rope_tabs, emit_f32_kv):
    S, T, _ = x.shape
    n = T // TILE
    rope = rope_tabs is not None

    def tok(w):
        return pl.BlockSpec((None, TILE, w), lambda s, i: (s, i, 0))

    in_specs = [
        tok(D_MODEL),
        pl.BlockSpec((None, 1, 6 * D_MODEL), lambda s, i: (row_fn(s), 0, 0)),
        pl.BlockSpec((D_MODEL, IN_WIDTH), lambda s, i: (0, 0)),
        pl.BlockSpec((2, RET_W), lambda s, i: (0, 0)),
    ]
    args = [x, mod_l, w_in_b, lg_lanes]
    if rope:
        in_specs += [pl.BlockSpec((TILE, DIFF_W), lambda s, i: (i, 0))] * 2
        args += list(rope_tabs)

    def tok_shape(w, dt):
        return jax.ShapeDtypeStruct((S, T, w), dt)

    kv_spec = pl.BlockSpec((None, None, RET_W, RET_W), lambda s, i: (s, i, 0, 0))
    kv_shape = jax.ShapeDtypeStruct((S, n, RET_W, RET_W), F32)
    out_specs = [tok(RET_W)] * 4 + [tok(DIFF_W)] * 3 + [tok(CONV_CH)] * 2 + [kv_spec] * 2
    out_shape = ([tok_shape(RET_W, BF)] * 3 + [tok_shape(RET_W, F32)] + [tok_shape(DIFF_W, BF)] * 3
                 + [tok_shape(CONV_CH, F32)] * 2 + [kv_shape] * 2)
    if emit_f32_kv:
        out_specs += [tok(DIFF_W)] * 2
        out_shape += [tok_shape(DIFF_W, F32)] * 2
    return pl.pallas_call(
        functools.partial(_proj_kernel, rope=rope, emit_f32_kv=emit_f32_kv),
        grid=(S, n),
        in_specs=in_specs,
        out_specs=out_specs,
        out_shape=out_shape,
        compiler_params=_params(("parallel", "parallel")),
        name="proj_rope" if rope else "proj",
    )(*args)


def _scan_kernel(kvf_ref, kvb_ref, s0f_ref, s0b_ref, lg_ref, sf_ref, sb_ref, ff_ref, fb_ref, *, n):
    lg = _log_sigmoid(lg_ref[...])
    dec = jnp.exp(float(TILE) * lg)
    same_head = (_iota((RET_W, RET_W), 0) >> 6) == (_iota((RET_W, RET_W), 1) >> 6)

    s = jnp.where(same_head, s0f_ref[...], 0.0)
    for c in range(n):
        sf_ref[c] = s
        s = s * dec[0:1] + jnp.where(same_head, kvf_ref[c], 0.0)
    ff_ref[...] = s
    s = jnp.where(same_head, s0b_ref[...], 0.0)
    for c in reversed(range(n)):
        sb_ref[c] = s
        s = s * dec[1:2] + jnp.where(same_head, kvb_ref[c], 0.0)
    fb_ref[...] = s


def _scan(kvf, kvb, s0f, s0b, lg_lanes):
    S, n = kvf.shape[:2]
    chunks = pl.BlockSpec((None, n, RET_W, RET_W), lambda s: (s, 0, 0, 0))
    one = pl.BlockSpec((None, RET_W, RET_W), lambda s: (s, 0, 0))
    return pl.pallas_call(
        functools.partial(_scan_kernel, n=n),
        grid=(S,),
        in_specs=[chunks, chunks, one, one, pl.BlockSpec((2, RET_W), lambda s: (0, 0))],
        out_specs=[chunks, chunks, one, one],
        out_shape=[jax.ShapeDtypeStruct((S, n, RET_W, RET_W), F32)] * 2
        + [jax.ShapeDtypeStruct((S, RET_W, RET_W), F32)] * 2,
        compiler_params=_params(("parallel",)),
        name="ret_scan",
    )(kvf, kvb, s0f, s0b, lg_lanes)


def _diff_head(q_h, k_h, v_h, lam):
    lane = _iota((1, DIFF_V), 1)
    s0 = _dot_nt(jnp.where(lane < DIFF_QK, q_h, 0), k_h)
    s1 = _dot_nt(jnp.where(lane >= DIFF_QK, q_h, 0), k_h)
    p0 = jnp.exp(s0 - jnp.max(s0, axis=-1, keepdims=True))
    p1 = jnp.exp(s1 - jnp.max(s1, axis=-1, keepdims=True))
    r0 = 1.0 / jnp.sum(p0, axis=-1, keepdims=True)
    r1 = lam / jnp.sum(p1, axis=-1, keepdims=True)
    a = p0 * r0 - p1 * r1
    return _dot(a.astype(BF), v_h)


def _mixer_kernel(x_ref, mod_ref, rq_ref, rk_ref, rv_ref, rg_ref, sf_ref, sb_ref,
                  dq_ref, kall_ref, vall_ref, cb_ref, cch_ref, cprev_ref, cnext_ref,
                  wout_ref, lg_ref, lamp_ref, subg_ref, cw_ref, cbias_ref, lng_ref, lnb_ref,
                  wrh_ref, wrl_ref, br_ref,
                  x1_ref, h2_ref, gates_ref, *, n_tiles, lam_init):
    i = pl.program_id(1)
    m = mod_ref[...]
    g1 = m[:, 2 * D_MODEL:3 * D_MODEL]
    sh2 = m[:, 3 * D_MODEL:4 * D_MODEL]
    sc2 = m[:, 4 * D_MODEL:5 * D_MODEL]

    lg = _log_sigmoid(lg_ref[...])
    head_of_lane = _iota((1, RET_W), 1) >> 6
    q = rq_ref[...]
    k = rk_ref[...]
    v = rv_ref[...]
    dist = (_iota((TILE, TILE), 0) - _iota((TILE, TILE), 1)).astype(F32)
    adist = jnp.abs(dist)
    diag2 = jnp.where(dist == 0.0, 2.0, 1.0)
    ret_o = jnp.zeros((TILE, RET_W), F32)
    for hd in range(H_RET):
        in_head = head_of_lane == hd
        sc = _dot_nt(jnp.where(in_head, q, 0), k)
        lgf = lg[0:1, hd * RET_DK:hd * RET_DK + 1]
        lgb = lg[1:2, hd * RET_DK:hd * RET_DK + 1]
        decay = jnp.exp(adist * jnp.where(dist > 0.0, lgf, lgb)) * diag2
        ret_o = ret_o + jnp.where(in_head, _dot((sc * decay).astype(BF), v), 0.0)
    p = _iota((TILE, 1), 0).astype(F32)
    ret_o = ret_o + _dot(q, sf_ref[...].astype(BF)) * jnp.exp((p + 1.0) * lg[0:1])
    ret_o = ret_o + _dot(q, sb_ref[...].astype(BF)) * jnp.exp((float(TILE) - p) * lg[1:2])
    avg = jnp.where((_iota((RET_W, RET_W), 0) >> 6) == (_iota((RET_W, RET_W), 1) >> 6),
                    1.0 / RET_DK, 0.0).astype(BF)
    rc = ret_o - _dot_hl(ret_o, avg)
    ret = rc * lax.rsqrt(_dot_hl(rc * rc, avg) + EPS) * _silu(rg_ref[...])

    lp = lamp_ref[...]
    lam = (jnp.exp(jnp.sum(lp[0:1] * lp[1:2], axis=-1, keepdims=True))
           - jnp.exp(jnp.sum(lp[2:3] * lp[3:4], axis=-1, keepdims=True)) + lam_init)
    subg = subg_ref[...] * (1.0 - lam_init)
    rows = []
    for qs in range(TILE // Q_SUB):
        heads = []
        for hd in range(H_DIFF):
            lo = hd * DIFF_V
            o = _diff_head(dq_ref[qs * Q_SUB:(qs + 1) * Q_SUB, lo:lo + DIFF_V],
                           kall_ref[:, lo:lo + DIFF_V], vall_ref[:, lo:lo + DIFF_V], lam)
            o = o * lax.rsqrt(jnp.mean(o * o, axis=-1, keepdims=True) + EPS) * subg
            heads.append(o.astype(BF))
        rows.append(jnp.concatenate(heads, axis=1))
    diff = jnp.concatenate(rows, axis=0)

    cch = cch_ref[...]
    prev = jnp.where(i > 0, cprev_ref[7:8, :], 0.0)
    nxt = jnp.where(i < n_tiles - 1, cnext_ref[0:1, :], 0.0)
    r = _iota((TILE, 1), 0)
    up = jnp.where(r == 0, prev, pltpu.roll(cch, 1, axis=0))
    dn = jnp.where(r == TILE - 1, nxt, pltpu.roll(cch, TILE - 1, axis=0))
    cw = cw_ref[...]
    conv = cb_ref[...] * (up * cw[0:1] + cch * cw[1:2] + dn * cw[2:3] + cbias_ref[...])

    mix = (_dot(ret.astype(BF), wout_ref[0:RET_W, :])
           + _dot(diff, wout_ref[RET_W:RET_W + DIFF_W, :])
           + _dot(conv.astype(BF), wout_ref[RET_W + DIFF_W:, :]))
    x1 = _layer_norm(ALPHA * x_ref[...] + g1 * mix, lng_ref[...], lnb_ref[...])
    x1_ref[...] = x1

    h2 = x1 * (1.0 + sc2) + sh2
    h2_ref[...] = h2.astype(BF)
    logits = _dot3(h2, wrh_ref[...], wrl_ref[...]) + br_ref[...]
    lane = _iota((TILE, ROUTER_LANES), 1)
    neg = -jnp.inf
    gl = jnp.where(lane < N_GROUPS, logits, neg)
    gmax = jnp.max(gl, axis=-1, keepdims=True)
    g_idx = jnp.min(jnp.where(gl == gmax, lane, ROUTER_LANES), axis=-1, keepdims=True)
    g_w = 1.0 / jnp.sum(jnp.where(lane < N_GROUPS, jnp.exp(logits - gmax), 0.0), axis=-1, keepdims=True)
    e_lane = lane - EXPERT_LANE0
    in_group = (e_lane >= 0) & (e_lane < N_EXPERTS) & ((e_lane >> 2) == g_idx)
    el = jnp.where(in_group, logits, neg)
    v1 = jnp.max(el, axis=-1, keepdims=True)
    i1 = jnp.min(jnp.where(el == v1, lane, ROUTER_LANES), axis=-1, keepdims=True)
    el2 = jnp.where(lane == i1, neg, el)
    v2 = jnp.max(el2, axis=-1, keepdims=True)
    i2 = jnp.min(jnp.where(el2 == v2, lane, ROUTER_LANES), axis=-1, keepdims=True)
    t = jnp.exp(v2 - v1)
    w1 = g_w / (1.0 + t)
    w2 = g_w * t / (1.0 + t)
    gates_ref[...] = jnp.where(lane == i1, w1, 0.0) + jnp.where(lane == i2, w2, 0.0)


def _mixer(x, mod_l, row_fn, pr, sf, sb, k_all, v_all, w_out_b, lg_lanes, lamp, subg, cw, cbias,
           lng, lnb, wrh, wrl, br, lam_init):
    S, T, _ = x.shape
    n = T // TILE
    Tk = k_all.shape[1]
    rq, rk, rv, rg, dq, cb, cch = pr
    t8 = TILE // 8

    def tok(w):
        return pl.BlockSpec((None, TILE, w), lambda s, i: (s, i, 0))

    def full(shape):
        return pl.BlockSpec(shape, lambda s, i: (0,) * len(shape))

    state = pl.BlockSpec((None, None, RET_W, RET_W), lambda s, i: (s, i, 0, 0))
    seq = pl.BlockSpec((None, Tk, DIFF_W), lambda s, i: (s, 0, 0))
    halo_prev = pl.BlockSpec((None, 8, CONV_CH), lambda s, i: (s, jnp.maximum(i * t8 - 1, 0), 0))
    halo_next = pl.BlockSpec((None, 8, CONV_CH), lambda s, i: (s, jnp.minimum((i + 1) * t8, T // 8 - 1), 0))
    in_specs = [
        tok(D_MODEL),
        pl.BlockSpec((None, 1, 6 * D_MODEL), lambda s, i: (row_fn(s), 0, 0)),
        tok(RET_W), tok(RET_W), tok(RET_W), tok(RET_W), state, state,
        tok(DIFF_W), seq, seq,
        tok(CONV_CH), tok(CONV_CH), halo_prev, halo_next,
        full((D_MODEL, D_MODEL)), full((2, RET_W)), full((4, DIFF_QK)), full((1, DIFF_V)),
        full((3, CONV_CH)), full((1, CONV_CH)), full((1, D_MODEL)), full((1, D_MODEL)),
        full((D_MODEL, ROUTER_LANES)), full((D_MODEL, ROUTER_LANES)), full((1, ROUTER_LANES)),
    ]
    return pl.pallas_call(
        functools.partial(_mixer_kernel, n_tiles=n, lam_init=lam_init),
        grid=(S, n),
        in_specs=in_specs,
        out_specs=[tok(D_MODEL), tok(D_MODEL), tok(ROUTER_LANES)],
        out_shape=[jax.ShapeDtypeStruct((S, T, D_MODEL), F32),
                   jax.ShapeDtypeStruct((S, T, D_MODEL), BF),
                   jax.ShapeDtypeStruct((S, T, ROUTER_LANES), F32)],
        compiler_params=_params(("parallel", "parallel")),
        name="mixer_lat" if Tk > T else "mixer_ctx",
    )(x, mod_l, rq, rk, rv, rg, sf, sb, dq, k_all, v_all, cb, cch, cch, cch,
      w_out_b, lg_lanes, lamp, subg, cw, cbias, lng, lnb, wrh, wrl, br)


def _moe_kernel(h2_ref, gates_ref, x1_ref, mod_ref, wg_ref, wu_ref, wd_ref, lng_ref, lnb_ref,
                o_ref, acc_ref):
    e = pl.program_id(1)

    @pl.when(e == 0)
    def _():
        acc_ref[...] = jnp.zeros_like(acc_ref)

    h = h2_ref[...]
    hg = _dot(h, wg_ref[...])
    hu = _dot(h, wu_ref[...])
    lane = _iota((MOE_TM, ROUTER_LANES), 1)
    gate = jnp.sum(jnp.where(lane == e + EXPERT_LANE0, gates_ref[...], 0.0), axis=-1, keepdims=True)
    act = _silu(hg) * hu * gate
    acc_ref[...] += _dot(act.astype(BF), wd_ref[...])

    @pl.when(e == N_EXPERTS - 1)
    def _():
        g2 = mod_ref[:, 5 * D_MODEL:6 * D_MODEL]
        o_ref[...] = _layer_norm(ALPHA * x1_ref[...] + g2 * acc_ref[...], lng_ref[...], lnb_ref[...])


def _moe(h2, gates, x1, mod_l, row_fn, wg_b, wu_b, wd_b, lng, lnb):
    N = h2.shape[0]

    def tok(w):
        return pl.BlockSpec((MOE_TM, w), lambda i, e: (i, 0))

    return pl.pallas_call(
        _moe_kernel,
        grid=(N // MOE_TM, N_EXPERTS),
        in_specs=[
            tok(D_MODEL), tok(ROUTER_LANES), tok(D_MODEL),
            pl.BlockSpec((None, 1, 6 * D_MODEL), lambda i, e: (row_fn(i), 0, 0)),
            pl.BlockSpec((None, D_MODEL, D_EXPERT), lambda i, e: (e, 0, 0)),
            pl.BlockSpec((None, D_MODEL, D_EXPERT), lambda i, e: (e, 0, 0)),
            pl.BlockSpec((None, D_EXPERT, D_MODEL), lambda i, e: (e, 0, 0)),
            pl.BlockSpec((1, D_MODEL), lambda i, e: (0, 0)),
            pl.BlockSpec((1, D_MODEL), lambda i, e: (0, 0)),
        ],
        out_specs=tok(D_MODEL),
        out_shape=jax.ShapeDtypeStruct((N, D_MODEL), F32),
        scratch_shapes=[pltpu.VMEM((MOE_TM, D_MODEL), F32)],
        compiler_params=_params(("parallel", "arbitrary")),
        name="moe_dense",
    )(h2, gates, x1, mod_l, wg_b, wu_b, wd_b, lng, lnb)


def _rope_tables(n_lat):
    half = DIFF_QK // 2
    pairs = half // 2
    inv = (1.0 / (ROPE_BASE ** (np.arange(pairs, dtype=np.float32) * 2.0 / half))).astype(np.float32)
    t = np.arange(n_lat)
    ang_r = ((t // GRID_W).astype(np.float32)[:, None] * inv[None, :]).astype(np.float64)
    ang_c = ((t % GRID_W).astype(np.float32)[:, None] * inv[None, :]).astype(np.float64)
    cos = np.concatenate([np.cos(ang_r)] * 2 + [np.cos(ang_c)] * 2, axis=1)
    sin = np.concatenate([-np.sin(ang_r), np.sin(ang_r), -np.sin(ang_c), np.sin(ang_c)], axis=1)
    reps = DIFF_W // DIFF_QK
    return (jnp.asarray(np.tile(cos, (1, reps)), F32), jnp.asarray(np.tile(sin, (1, reps)), F32))


def _block_diag(s):
    S = s.shape[0]
    eye = jnp.eye(H_RET, dtype=s.dtype)
    return jnp.einsum('shdv,hg->shdgv', s, eye).reshape(S, RET_W, RET_W)


def _diag_blocks(s):
    S = s.shape[0]
    s5 = s.reshape(S, H_RET, RET_DK, H_RET, RET_DK)
    return jnp.stack([s5[:, hd, :, hd, :] for hd in range(H_RET)], axis=1)


def kernel(x_prompt, x_sample, cache_diff_k, cache_diff_v, state_ret_fwd, state_ret_bwd, c, c_ctx, w_mod, b_mod, w_in, ret_decay_logit, diff_lambda, diff_subln_g, conv_w, conv_b, w_out, ln_g, ln_b, w_router_group, b_router_group, w_router_expert, b_router_expert, w_gate, w_up, w_down):
    B, T_ctx, _ = x_prompt.shape
    Bd, T_lat, _ = x_sample.shape
    assert T_ctx == TILE and T_lat % TILE == 0 and T_lat % GRID_W == 0
    assert 1 + Bd <= MOD_ROWS

    cond = jnp.concatenate([c_ctx[None, :], c, jnp.zeros((MOD_ROWS - 1 - Bd, D_MODEL), F32)], axis=0)
    mod = _modulation(cond, w_mod, b_mod).reshape(DEPTH, MOD_ROWS, 1, 6 * D_MODEL)
    rope_tabs = _rope_tables(T_lat)

    w_in_b = w_in.astype(BF)
    w_out_b = w_out.astype(BF)
    wg_b = w_gate.astype(BF)
    wu_b = w_up.astype(BF)
    wd_b = w_down.astype(BF)

    ctx_row = lambda s: 0
    lat_row = lambda s: s + 1
    ctx_tile_row = lambda i: 0
    lat_tiles = T_lat // MOE_TM
    lat_tile_row = lambda i: i // lat_tiles + 1

    yp, ys = x_prompt, x_sample
    ks_l, vs_l, sf_l, sb_l = [], [], [], []
    for l in range(DEPTH):
        lam_init = 0.8 - 0.6 * math.exp(-0.3 * l)
        lg_lanes = jnp.repeat(ret_decay_logit[l], RET_DK, axis=1)
        w_r = jnp.concatenate([w_router_group[l], w_router_expert[l].reshape(D_MODEL, N_EXPERTS)], axis=1)
        w_r = jnp.pad(w_r, ((0, 0), (0, ROUTER_LANES - N_GROUPS - N_EXPERTS)))
        wrh = w_r.astype(BF)
        wrl = (w_r - wrh.astype(F32)).astype(BF)
        br = jnp.pad(jnp.concatenate([b_router_group[l], b_router_expert[l].reshape(N_EXPERTS)]),
                     (0, ROUTER_LANES - N_GROUPS - N_EXPERTS)).reshape(1, ROUTER_LANES)
        shared = (w_out_b[l], lg_lanes, diff_lambda[l], diff_subln_g[l].reshape(1, DIFF_V), conv_w[l],
                  conv_b[l].reshape(1, CONV_CH), ln_g[l, 0].reshape(1, D_MODEL), ln_b[l, 0].reshape(1, D_MODEL),
                  wrh, wrl, br, lam_init)
        ln2 = (ln_g[l, 1].reshape(1, D_MODEL), ln_b[l, 1].reshape(1, D_MODEL))
        experts = (wg_b[l], wu_b[l], wd_b[l])

        pr = _proj(yp, mod[l], ctx_row, w_in_b[l], lg_lanes, None, True)
        rq, rk, rv, rg, dq, dk, dv, cb, cch, kvf, kvb, dk32, dv32 = pr
        zeros = jnp.zeros((B, RET_W, RET_W), F32)
        sf, sb, ff, fb = _scan(kvf, kvb, zeros, zeros, lg_lanes)
        x1, h2, gates = _mixer(yp, mod[l], ctx_row, (rq, rk, rv, rg, dq, cb, cch), sf, sb, dk, dv, *shared)
        yp = _moe(h2.reshape(B * T_ctx, D_MODEL), gates.reshape(B * T_ctx, ROUTER_LANES),
                  x1.reshape(B * T_ctx, D_MODEL), mod[l], ctx_tile_row, *experts, *ln2).reshape(B, T_ctx, D_MODEL)
        ks_l.append(dk32.reshape(B, T_ctx, H_DIFF, 2, DIFF_QK))
        vs_l.append(dv32.reshape(B, T_ctx, H_DIFF, DIFF_V))
        sf_l.append(_diag_blocks(ff))
        sb_l.append(_diag_blocks(fb))

        pr = _proj(ys, mod[l], lat_row, w_in_b[l], lg_lanes, rope_tabs, False)
        rq, rk, rv, rg, dq, dk, dv, cb, cch, kvf, kvb = pr
        sf, sb, _, _ = _scan(kvf, kvb, _block_diag(state_ret_fwd[:, l]), _block_diag(state_ret_bwd[:, l]), lg_lanes)
        k_all = jnp.concatenate([cache_diff_k[:, l].reshape(Bd, -1, DIFF_W).astype(BF), dk], axis=1)
        v_all = jnp.concatenate([cache_diff_v[:, l].reshape(Bd, -1, DIFF_W).astype(BF), dv], axis=1)
        x1, h2, gates = _mixer(ys, mod[l], lat_row, (rq, rk, rv, rg, dq, cb, cch), sf, sb, k_all, v_all, *shared)
        ys = _moe(h2.reshape(Bd * T_lat, D_MODEL), gates.reshape(Bd * T_lat, ROUTER_LANES),
                  x1.reshape(Bd * T_lat, D_MODEL), mod[l], lat_tile_row, *experts, *ln2).reshape(Bd, T_lat, D_MODEL)

    return (yp, ys, jnp.stack(ks_l, axis=1), jnp.stack(vs_l, axis=1),
            jnp.stack(sf_l, axis=1), jnp.stack(sb_l, axis=1))
```

```python
import functools
import math

import numpy as np
import jax
import jax.numpy as jnp
from jax import lax
from jax.experimental import pallas as pl
from jax.experimental.pallas import tpu as pltpu

D_MODEL = 1024
DEPTH = 2
GRID_W = 64
H_RET = 4
RET_DK = 64
RET_W = H_RET * RET_DK
H_DIFF = 4
DIFF_QK = 64
DIFF_V = 2 * DIFF_QK
DIFF_W = H_DIFF * DIFF_V
CONV_CH = 256
ROPE_BASE = 10000.0
N_GROUPS = 4
EXPERTS_PER_GROUP = 4
N_EXPERTS = N_GROUPS * EXPERTS_PER_GROUP
D_EXPERT = 512
ALPHA = (2 * DEPTH) ** 0.25
EPS = 1e-5
MOD_ROWS = 8
ROUTER_LANES = 128
EXPERT_LANE0 = N_GROUPS

TILE = 256
Q_SUB = 128
CHUNK = 16
CAP = 2 * TILE + N_EXPERTS * CHUNK
CHUNKS_PER_TILE = CAP // CHUNK
MOE_TMB = 256
CPB = MOE_TMB // CHUNK
HS_W = D_MODEL + 128

_O_RQ, _O_RK, _O_RV, _O_RG = 0, 256, 512, 768
_O_DQ, _O_DK, _O_DV = 1024, 1536, 2048
_O_CB, _O_CC, _O_CH = 2560, 2816, 3072
IN_WIDTH = 3328

BF = jnp.bfloat16
F32 = jnp.float32

_VMEM_LIMIT = 56 * 1024 * 1024


def _dot(a, b):
    return jnp.dot(a, b, preferred_element_type=F32)


def _dot_nt(a, b):
    return lax.dot_general(a, b, (((1,), (1,)), ((), ())), preferred_element_type=F32)


def _dot_tn(a, b):
    return lax.dot_general(a, b, (((0,), (0,)), ((), ())), preferred_element_type=F32)


def _split(x):
    hi = x.astype(BF)
    lo = (x - hi.astype(F32)).astype(BF)
    return hi, lo


def _dot_hl(x, w_bf16):
    hi, lo = _split(x)
    return _dot(hi, w_bf16) + _dot(lo, w_bf16)


def _dot3(x, w_hi, w_lo):
    hi, lo = _split(x)
    return _dot(hi, w_hi) + (_dot(lo, w_hi) + _dot(hi, w_lo))


def _iota(shape, dim):
    return lax.broadcasted_iota(jnp.int32, shape, dim)


def _log_sigmoid(x):
    return jnp.minimum(x, 0.0) - jnp.log1p(jnp.exp(-jnp.abs(x)))


def _silu(x):
    return x * jax.nn.sigmoid(x)


def _layer_norm(y, g, b):
    mu = jnp.mean(y, axis=-1, keepdims=True)
    yc = y - mu
    var = jnp.mean(yc * yc, axis=-1, keepdims=True)
    return yc * lax.rsqrt(var + EPS) * g + b


def _params(sem):
    return pltpu.CompilerParams(dimension_semantics=sem, vmem_limit_bytes=_VMEM_LIMIT)


def _mod_kernel(c_ref, w_ref, b_ref, o_ref):
    a = _silu(c_ref[...])
    w_hi, w_lo = _split(w_ref[...])
    o_ref[...] = _dot3(a, w_hi, w_lo) + b_ref[...]


def _modulation(cond, w_mod, b_mod):
    tn = 1536
    n6 = 6 * D_MODEL
    return pl.pallas_call(
        _mod_kernel,
        grid=(DEPTH, n6 // tn),
        in_specs=[
            pl.BlockSpec((MOD_ROWS, D_MODEL), lambda l, j: (0, 0)),
            pl.BlockSpec((None, D_MODEL, tn), lambda l, j: (l, 0, j)),
            pl.BlockSpec((None, 1, tn), lambda l, j: (l, 0, j)),
        ],
        out_specs=pl.BlockSpec((None, MOD_ROWS, tn), lambda l, j: (l, 0, j)),
        out_shape=jax.ShapeDtypeStruct((DEPTH, MOD_ROWS, n6), F32),
        compiler_params=_params(("parallel", "parallel")),
        name="modulation",
    )(cond, w_mod, b_mod.reshape(DEPTH, 1, n6))


def _swap16(x):
    n = x.shape[-1]
    lane = _iota(x.shape, 1)
    return jnp.where((lane & 16) == 0, pltpu.roll(x, n - 16, axis=1), pltpu.roll(x, 16, axis=1))


def _proj_kernel(*refs, rope, emit_f32_kv):
    x_ref, mod_ref, w_ref, lg_ref = refs[:4]
    pos = 4
    if rope:
        cos_ref, sin_ref = refs[pos:pos + 2]
        pos += 2
    (rq_ref, rk_ref, rv_ref, rg_ref, dq_ref, dk_ref, dv_ref, cb_ref, cch_ref,
     kvf_ref, kvb_ref) = refs[pos:pos + 11]
    pos += 11
    if emit_f32_kv:
        dk32_ref, dv32_ref = refs[pos:pos + 2]

    m = mod_ref[...]
    sh1 = m[:, 0:D_MODEL]
    sc1 = m[:, D_MODEL:2 * D_MODEL]
    h = (x_ref[...] * (1.0 + sc1) + sh1).astype(BF)

    def col(off, width):
        return _dot(h, w_ref[:, off:off + width])

    rq_ref[...] = col(_O_RQ, RET_W).astype(BF)
    rk = col(_O_RK, RET_W) * (RET_DK ** -0.5)
    rk_ref[...] = rk.astype(BF)
    rv = col(_O_RV, RET_W).astype(BF)
    rv_ref[...] = rv
    rg_ref[...] = col(_O_RG, RET_W)

    lg = _log_sigmoid(lg_ref[...])
    p = _iota((TILE, 1), 0).astype(F32)
    kf = (rk * jnp.exp((TILE - 1.0 - p) * lg[0:1])).astype(BF)
    kb = (rk * jnp.exp(p * lg[1:2])).astype(BF)
    kvf_ref[...] = _dot_tn(kf, rv)
    kvb_ref[...] = _dot_tn(kb, rv)

    dq = col(_O_DQ, DIFF_W)
    dk = col(_O_DK, DIFF_W)
    dv = col(_O_DV, DIFF_W)
    if emit_f32_kv:
        dk32_ref[...] = dk
        dv32_ref[...] = dv
    if rope:
        cos = cos_ref[...]
        sin = sin_ref[...]
        dq = dq * cos + _swap16(dq) * sin
        dk = dk * cos + _swap16(dk) * sin
    dq_ref[...] = (dq * (DIFF_QK ** -0.5)).astype(BF)
    dk_ref[...] = dk.astype(BF)
    dv_ref[...] = dv.astype(BF)

    cb_ref[...] = col(_O_CB, CONV_CH)
    cch_ref[...] = col(_O_CC, CONV_CH) * col(_O_CH, CONV_CH)


def _proj(x, mod_l, row_fn, w_in_b, lg_lanes, rope_tabs, emit_f32_kv):
    S, T, _ = x.shape
    n = T // TILE
    rope = rope_tabs is not None

    def tok(w):
        return pl.BlockSpec((None, TILE, w), lambda s, i: (s, i, 0))

    in_specs = [
        tok(D_MODEL),
        pl.BlockSpec((None, 1, 6 * D_MODEL), lambda s, i: (row_fn(s), 0, 0)),
        pl.BlockSpec((D_MODEL, IN_WIDTH), lambda s, i: (0, 0)),
        pl.BlockSpec((2, RET_W), lambda s, i: (0, 0)),
    ]
    args = [x, mod_l, w_in_b, lg_lanes]
    if rope:
        in_specs += [pl.BlockSpec((TILE, DIFF_W), lambda s, i: (i, 0))] * 2
        args += list(rope_tabs)

    def tok_shape(w, dt):
        return jax.ShapeDtypeStruct((S, T, w), dt)

    kv_spec = pl.BlockSpec((None, None, RET_W, RET_W), lambda s, i: (s, i, 0, 0))
    kv_shape = jax.ShapeDtypeStruct((S, n, RET_W, RET_W), F32)
    out_specs = [tok(RET_W)] * 4 + [tok(DIFF_W)] * 3 + [tok(CONV_CH)] * 2 + [kv_spec] * 2
    out_shape = ([tok_shape(RET_W, BF)] * 3 + [tok_shape(RET_W, F32)] + [tok_shape(DIFF_W, BF)] * 3
                 + [tok_shape(CONV_CH, F32)] * 2 + [kv_shape] * 2)
    if emit_f32_kv:
        out_specs += [tok(DIFF_W)] * 2
        out_shape += [tok_shape(DIFF_W, F32)] * 2
    return pl.pallas_call(
        functools.partial(_proj_kernel, rope=rope, emit_f32_kv=emit_f32_kv),
        grid=(S, n),
        in_specs=in_specs,
        out_specs=out_specs,
        out_shape=out_shape,
        compiler_params=_params(("parallel", "parallel")),
        name="proj_rope" if rope else "proj",
    )(*args)


def _scan_kernel(kvf_ref, kvb_ref, s0f_ref, s0b_ref, lg_ref, sf_ref, sb_ref, ff_ref, fb_ref, *, n):
    lg = _log_sigmoid(lg_ref[...])
    dec = jnp.exp(float(TILE) * lg)
    same_head = (_iota((RET_W, RET_W), 0) >> 6) == (_iota((RET_W, RET_W), 1) >> 6)

    s = jnp.where(same_head, s0f_ref[...], 0.0)
    for c in range(n):
        sf_ref[c] = s
        s = s * dec[0:1] + jnp.where(same_head, kvf_ref[c], 0.0)
    ff_ref[...] = s
    s = jnp.where(same_head, s0b_ref[...], 0.0)
    for c in reversed(range(n)):
        sb_ref[c] = s
        s = s * dec[1:2] + jnp.where(same_head, kvb_ref[c], 0.0)
    fb_ref[...] = s


def _scan(kvf, kvb, s0f, s0b, lg_lanes):
    S, n = kvf.shape[:2]
    chunks = pl.BlockSpec((None, n, RET_W, RET_W), lambda s: (s, 0, 0, 0))
    one = pl.BlockSpec((None, RET_W, RET_W), lambda s: (s, 0, 0))
    return pl.pallas_call(
        functools.partial(_scan_kernel, n=n),
        grid=(S,),
        in_specs=[chunks, chunks, one, one, pl.BlockSpec((2, RET_W), lambda s: (0, 0))],
        out_specs=[chunks, chunks, one, one],
        out_shape=[jax.ShapeDtypeStruct((S, n, RET_W, RET_W), F32)] * 2
        + [jax.ShapeDtypeStruct((S, RET_W, RET_W), F32)] * 2,
        compiler_params=_params(("parallel",)),
        name="ret_scan",
    )(kvf, kvb, s0f, s0b, lg_lanes)


def _diff_head(q_h, k_h, v_h, lam):
    lane = _iota((1, DIFF_V), 1)
    s0 = _dot_nt(jnp.where(lane < DIFF_QK, q_h, 0), k_h)
    s1 = _dot_nt(jnp.where(lane >= DIFF_QK, q_h, 0), k_h)
    p0 = jnp.exp(s0 - jnp.max(s0, axis=-1, keepdims=True))
    p1 = jnp.exp(s1 - jnp.max(s1, axis=-1, keepdims=True))
    r0 = 1.0 / jnp.sum(p0, axis=-1, keepdims=True)
    r1 = lam / jnp.sum(p1, axis=-1, keepdims=True)
    a = p0 * r0 - p1 * r1
    return _dot(a.astype(BF), v_h)


def _mixer_kernel(x_ref, mod_ref, rq_ref, rk_ref, rv_ref, rg_ref, sf_ref, sb_ref,
                  dq_ref, kall_ref, vall_ref, cb_ref, cch_ref, cprev_ref, cnext_ref,
                  wout_ref, lg_ref, lamp_ref, subg_ref, cw_ref, cbias_ref, lng_ref, lnb_ref,
                  wrh_ref, wrl_ref, br_ref,
                  x1_ref, hs_ref, slots_ref, cnt_ref, *, n_tiles, lam_init):
    i = pl.program_id(1)
    m = mod_ref[...]
    g1 = m[:, 2 * D_MODEL:3 * D_MODEL]
    sh2 = m[:, 3 * D_MODEL:4 * D_MODEL]
    sc2 = m[:, 4 * D_MODEL:5 * D_MODEL]

    lg = _log_sigmoid(lg_ref[...])
    head_of_lane = _iota((1, RET_W), 1) >> 6
    q = rq_ref[...]
    k = rk_ref[...]
    v = rv_ref[...]
    dist = (_iota((TILE, TILE), 0) - _iota((TILE, TILE), 1)).astype(F32)
    adist = jnp.abs(dist)
    diag2 = jnp.where(dist == 0.0, 2.0, 1.0)
    ret_o = jnp.zeros((TILE, RET_W), F32)
    for hd in range(H_RET):
        in_head = head_of_lane == hd
        sc = _dot_nt(jnp.where(in_head, q, 0), k)
        lgf = lg[0:1, hd * RET_DK:hd * RET_DK + 1]
        lgb = lg[1:2, hd * RET_DK:hd * RET_DK + 1]
        decay = jnp.exp(adist * jnp.where(dist > 0.0, lgf, lgb)) * diag2
        ret_o = ret_o + jnp.where(in_head, _dot((sc * decay).astype(BF), v), 0.0)
    p = _iota((TILE, 1), 0).astype(F32)
    ret_o = ret_o + _dot(q, sf_ref[...].astype(BF)) * jnp.exp((p + 1.0) * lg[0:1])
    ret_o = ret_o + _dot(q, sb_ref[...].astype(BF)) * jnp.exp((float(TILE) - p) * lg[1:2])
    avg = jnp.where((_iota((RET_W, RET_W), 0) >> 6) == (_iota((RET_W, RET_W), 1) >> 6),
                    1.0 / RET_DK, 0.0).astype(BF)
    rc = ret_o - _dot_hl(ret_o, avg)
    ret = rc * lax.rsqrt(_dot_hl(rc * rc, avg) + EPS) * _silu(rg_ref[...])

    lp = lamp_ref[...]
    lam = (jnp.exp(jnp.sum(lp[0:1] * lp[1:2], axis=-1, keepdims=True))
           - jnp.exp(jnp.sum(lp[2:3] * lp[3:4], axis=-1, keepdims=True)) + lam_init)
    subg = subg_ref[...] * (1.0 - lam_init)
    rows = []
    for qs in range(TILE // Q_SUB):
        heads = []
        for hd in range(H_DIFF):
            lo = hd * DIFF_V
            o = _diff_head(dq_ref[qs * Q_SUB:(qs + 1) * Q_SUB, lo:lo + DIFF_V],
                           kall_ref[:, lo:lo + DIFF_V], vall_ref[:, lo:lo + DIFF_V], lam)
            o = o * lax.rsqrt(jnp.mean(o * o, axis=-1, keepdims=True) + EPS) * subg
            heads.append(o.astype(BF))
        rows.append(jnp.concatenate(heads, axis=1))
    diff = jnp.concatenate(rows, axis=0)

    cch = cch_ref[...]
    prev = jnp.where(i > 0, cprev_ref[7:8, :], 0.0)
    nxt = jnp.where(i < n_tiles - 1, cnext_ref[0:1, :], 0.0)
    r = _iota((TILE, 1), 0)
    up = jnp.where(r == 0, prev, pltpu.roll(cch, 1, axis=0))
    dn = jnp.where(r == TILE - 1, nxt, pltpu.roll(cch, TILE - 1, axis=0))
    cw = cw_ref[...]
    conv = cb_ref[...] * (up * cw[0:1] + cch * cw[1:2] + dn * cw[2:3] + cbias_ref[...])

    mix = (_dot(ret.astype(BF), wout_ref[0:RET_W, :])
           + _dot(diff, wout_ref[RET_W:RET_W + DIFF_W, :])
           + _dot(conv.astype(BF), wout_ref[RET_W + DIFF_W:, :]))
    x1 = _layer_norm(ALPHA * x_ref[...] + g1 * mix, lng_ref[...], lnb_ref[...])
    x1_ref[...] = x1

    h2 = x1 * (1.0 + sc2) + sh2
    logits = _dot3(h2, wrh_ref[...], wrl_ref[...]) + br_ref[...]
    lane = _iota((TILE, ROUTER_LANES), 1)
    neg = -jnp.inf
    gl = jnp.where(lane < N_GROUPS, logits, neg)
    gmax = jnp.max(gl, axis=-1, keepdims=True)
    g_idx = jnp.min(jnp.where(gl == gmax, lane, ROUTER_LANES), axis=-1, keepdims=True)
    g_w = 1.0 / jnp.sum(jnp.where(lane < N_GROUPS, jnp.exp(logits - gmax), 0.0), axis=-1, keepdims=True)
    e_lane = lane - EXPERT_LANE0
    in_group = (e_lane >= 0) & (e_lane < N_EXPERTS) & ((e_lane >> 2) == g_idx)
    el = jnp.where(in_group, logits, neg)
    v1 = jnp.max(el, axis=-1, keepdims=True)
    i1 = jnp.min(jnp.where(el == v1, lane, ROUTER_LANES), axis=-1, keepdims=True)
    el2 = jnp.where(lane == i1, neg, el)
    v2 = jnp.max(el2, axis=-1, keepdims=True)
    i2 = jnp.min(jnp.where(el2 == v2, lane, ROUTER_LANES), axis=-1, keepdims=True)
    t = jnp.exp(v2 - v1)
    w1 = g_w / (1.0 + t)
    w2 = g_w * t / (1.0 + t)

    sel1 = lane + EXPERT_LANE0 == i1
    sel2 = lane + EXPERT_LANE0 == i2
    sel = jnp.where(sel1 | sel2, 1.0, 0.0)
    earlier = jnp.where(_iota((TILE, TILE), 0) > _iota((TILE, TILE), 1), 1.0, 0.0).astype(BF)
    rank = _dot(earlier, sel.astype(BF))
    cnt = jnp.sum(sel, axis=0, keepdims=True).astype(jnp.int32)
    padded = ((cnt + (CHUNK - 1)) >> 4) << 4
    before = jnp.where(_iota((ROUTER_LANES, ROUTER_LANES), 0) < _iota((ROUTER_LANES, ROUTER_LANES), 1),
                       1.0, 0.0).astype(BF)
    seg_off = _dot(jnp.broadcast_to(padded.astype(F32), (8, ROUTER_LANES)).astype(BF), before)[0:1]
    pos = seg_off + rank
    slot1 = jnp.sum(jnp.where(sel1, pos, 0.0), axis=-1, keepdims=True).astype(jnp.int32)
    slot2 = jnp.sum(jnp.where(sel2, pos, 0.0), axis=-1, keepdims=True).astype(jnp.int32)
    s_lane = _iota((TILE, CAP), 1)
    p1 = jnp.where(s_lane == slot1, 1.0, 0.0).astype(BF)
    p2 = jnp.where(s_lane == slot2, 1.0, 0.0).astype(BF)
    hs_ref[:, 0:D_MODEL] = _dot_tn(p1 + p2, h2.astype(BF)).astype(BF)
    w1h = w1.astype(BF).astype(F32)
    w2h = w2.astype(BF).astype(F32)
    gw1 = jnp.where(lane == 0, w1h, jnp.where(lane == 1, w1 - w1h, 0.0)).astype(BF)
    gw2 = jnp.where(lane == 0, w2h, jnp.where(lane == 1, w2 - w2h, 0.0)).astype(BF)
    hs_ref[:, D_MODEL:] = (_dot_tn(p1, gw1) + _dot_tn(p2, gw2)).astype(BF)
    slots_ref[...] = jnp.where(lane == 0, slot1, jnp.where(lane == 1, slot2, 0))
    cnt_ref[...] = padded


def _mixer(x, mod_l, row_fn, pr, sf, sb, k_all, v_all, w_out_b, lg_lanes, lamp, subg, cw, cbias,
           lng, lnb, wrh, wrl, br, lam_init):
    S, T, _ = x.shape
    n = T // TILE
    Tk = k_all.shape[1]
    rq, rk, rv, rg, dq, cb, cch = pr
    t8 = TILE // 8

    def tok(w):
        return pl.BlockSpec((None, TILE, w), lambda s, i: (s, i, 0))

    def full(shape):
        return pl.BlockSpec(shape, lambda s, i: (0,) * len(shape))

    state = pl.BlockSpec((None, None, RET_W, RET_W), lambda s, i: (s, i, 0, 0))
    seq = pl.BlockSpec((None, Tk, DIFF_W), lambda s, i: (s, 0, 0))
    halo_prev = pl.BlockSpec((None, 8, CONV_CH), lambda s, i: (s, jnp.maximum(i * t8 - 1, 0), 0))
    halo_next = pl.BlockSpec((None, 8, CONV_CH), lambda s, i: (s, jnp.minimum((i + 1) * t8, T // 8 - 1), 0))
    in_specs = [
        tok(D_MODEL),
        pl.BlockSpec((None, 1, 6 * D_MODEL), lambda s, i: (row_fn(s), 0, 0)),
        tok(RET_W), tok(RET_W), tok(RET_W), tok(RET_W), state, state,
        tok(DIFF_W), seq, seq,
        tok(CONV_CH), tok(CONV_CH), halo_prev, halo_next,
        full((D_MODEL, D_MODEL)), full((2, RET_W)), full((4, DIFF_QK)), full((1, DIFF_V)),
        full((3, CONV_CH)), full((1, CONV_CH)), full((1, D_MODEL)), full((1, D_MODEL)),
        full((D_MODEL, ROUTER_LANES)), full((D_MODEL, ROUTER_LANES)), full((1, ROUTER_LANES)),
    ]
    return pl.pallas_call(
        functools.partial(_mixer_kernel, n_tiles=n, lam_init=lam_init),
        grid=(S, n),
        in_specs=in_specs,
        out_specs=[tok(D_MODEL),
                   pl.BlockSpec((None, CAP, HS_W), lambda s, i: (s, i, 0)),
                   tok(ROUTER_LANES),
                   pl.BlockSpec((None, None, 1, ROUTER_LANES), lambda s, i: (s, i, 0, 0))],
        out_shape=[jax.ShapeDtypeStruct((S, T, D_MODEL), F32),
                   jax.ShapeDtypeStruct((S, n * CAP, HS_W), BF),
                   jax.ShapeDtypeStruct((S, T, ROUTER_LANES), jnp.int32),
                   jax.ShapeDtypeStruct((S, n, 1, ROUTER_LANES), jnp.int32)],
        compiler_params=_params(("parallel", "parallel")),
        name="mixer_lat" if Tk > T else "mixer_ctx",
    )(x, mod_l, rq, rk, rv, rg, sf, sb, dq, k_all, v_all, cb, cch, cch, cch,
      w_out_b, lg_lanes, lamp, subg, cw, cbias, lng, lnb, wrh, wrl, br)


def _block_schedule(padded):
    NT = padded.shape[0]
    n_blocks = NT * CHUNKS_PER_TILE // CPB + N_EXPERTS
    c16 = padded // CHUNK
    ends = jnp.cumsum(c16, axis=1)
    q = jnp.arange(CHUNKS_PER_TILE, dtype=jnp.int32)
    key = jnp.sum((ends[:, None, :] <= q[None, :, None]).astype(jnp.int32), axis=-1)
    order = jnp.argsort(key.reshape(-1), stable=True).astype(jnp.int32)
    per_expert = jnp.sum(c16, axis=0)
    c_start = jnp.cumsum(per_expert) - per_expert
    nb = (per_expert + (CPB - 1)) // CPB
    b_end = jnp.cumsum(nb)
    b = jnp.arange(n_blocks, dtype=jnp.int32)
    eb = jnp.sum((b_end[None, :] <= b[:, None]).astype(jnp.int32), axis=-1)
    live = eb < N_EXPERTS
    ebc = jnp.minimum(eb, N_EXPERTS - 1)
    local = (b - (b_end - nb)[ebc])[:, None] * CPB + jnp.arange(CPB, dtype=jnp.int32)[None, :]
    valid = live[:, None] & (local < per_expert[ebc][:, None])
    nv = jnp.sum(valid.astype(jnp.int32), axis=-1)
    idx = jnp.clip(c_start[ebc][:, None] + local, 0, NT * CHUNKS_PER_TILE - 1)
    cid = jnp.where(valid, order[idx], 0)
    return ebc.astype(jnp.int32), nv.astype(jnp.int32), cid.reshape(-1).astype(jnp.int32)


def _moe_kernel(eb_ref, nv_ref, cid_ref, hs_hbm, ys_init_hbm, wg_ref, wu_ref, wd_ref, ys_hbm,
                xbuf, ybuf, in_sem, out_sem, *, n_blocks):
    del eb_ref, ys_init_hbm
    b = pl.program_id(0)

    def chunk_rows(c):
        return pl.ds(pl.multiple_of(c * CHUNK, CHUNK), CHUNK)

    def gather_copy(blk, j, slot):
        c = cid_ref[blk * CPB + j]
        return pltpu.make_async_copy(hs_hbm.at[chunk_rows(c)], xbuf.at[slot, chunk_rows(j)], in_sem.at[slot])

    def scatter_copy(blk, j, slot):
        c = cid_ref[blk * CPB + j]
        return pltpu.make_async_copy(ybuf.at[slot, chunk_rows(j)], ys_hbm.at[chunk_rows(c)], out_sem.at[slot])

    def for_chunks(blk, fn):
        def body(j, carry):
            fn(j)
            return carry
        lax.fori_loop(0, nv_ref[blk], body, 0)

    slot = lax.rem(b, 2)
    other = 1 - slot

    @pl.when(b == 0)
    def _():
        xbuf[...] = jnp.zeros_like(xbuf)
        for_chunks(0, lambda j: gather_copy(0, j, 0).start())

    @pl.when(b + 1 < n_blocks)
    def _():
        for_chunks(b + 1, lambda j: gather_copy(b + 1, j, other).start())

    for_chunks(b, lambda j: gather_copy(b, j, slot).wait())

    @pl.when(b >= 2)
    def _():
        for_chunks(b - 2, lambda j: scatter_copy(b - 2, j, slot).wait())

    @pl.when(nv_ref[b] > 0)
    def _():
        x = xbuf[slot, :, 0:D_MODEL]
        gate = (xbuf[slot, :, D_MODEL:D_MODEL + 1].astype(F32)
                + xbuf[slot, :, D_MODEL + 1:D_MODEL + 2].astype(F32))
        hg = _dot(x, wg_ref[...])
        hu = _dot(x, wu_ref[...])
        act = _silu(hg) * hu * gate
        ybuf[slot] = _dot(act.astype(BF), wd_ref[...]).astype(BF)

    for_chunks(b, lambda j: scatter_copy(b, j, slot).start())

    @pl.when(b == n_blocks - 1)
    def _():
        if n_blocks >= 2:
            for_chunks(b - 1, lambda j: scatter_copy(b - 1, j, other).wait())
        for_chunks(b, lambda j: scatter_copy(b, j, slot).wait())


def _moe(hs, eb, nv, cid, wg_b, wu_b, wd_b):
    rows = hs.shape[0]
    n_blocks = eb.shape[0]
    ys_init = jnp.zeros((rows, D_MODEL), BF)
    hbm = pl.BlockSpec(memory_space=pltpu.HBM)
    grid_spec = pltpu.PrefetchScalarGridSpec(
        num_scalar_prefetch=3,
        grid=(n_blocks,),
        in_specs=[
            hbm, hbm,
            pl.BlockSpec((None, D_MODEL, D_EXPERT), lambda b, eb, nv, cid: (eb[b], 0, 0)),
            pl.BlockSpec((None, D_MODEL, D_EXPERT), lambda b, eb, nv, cid: (eb[b], 0, 0)),
            pl.BlockSpec((None, D_EXPERT, D_MODEL), lambda b, eb, nv, cid: (eb[b], 0, 0)),
        ],
        out_specs=hbm,
        scratch_shapes=[pltpu.VMEM((2, MOE_TMB, HS_W), BF), pltpu.VMEM((2, MOE_TMB, D_MODEL), BF),
                        pltpu.SemaphoreType.DMA((2,)), pltpu.SemaphoreType.DMA((2,))],
    )
    return pl.pallas_call(
        functools.partial(_moe_kernel, n_blocks=n_blocks),
        grid_spec=grid_spec,
        out_shape=jax.ShapeDtypeStruct((rows, D_MODEL), BF),
        input_output_aliases={4: 0},
        compiler_params=_params(("arbitrary",)),
        name="moe_sorted",
    )(eb, nv, cid, hs, ys_init, wg_b, wu_b, wd_b)


def _combine_kernel(ys_ref, slots_ref, x1_ref, mod_ref, lng_ref, lnb_ref, o_ref):
    sl = slots_ref[...]
    s_lane = _iota((TILE, CAP), 1)
    pick = jnp.where((s_lane == sl[:, 0:1]) | (s_lane == sl[:, 1:2]), 1.0, 0.0).astype(BF)
    ffn = _dot(pick, ys_ref[...])
    g2 = mod_ref[:, 5 * D_MODEL:6 * D_MODEL]
    o_ref[...] = _layer_norm(ALPHA * x1_ref[...] + g2 * ffn, lng_ref[...], lnb_ref[...])


def _combine(ys, slots, x1, mod_l, row_fn, lng, lnb):
    S, T, _ = x1.shape
    n = T // TILE

    def tok(w):
        return pl.BlockSpec((None, TILE, w), lambda s, i: (s, i, 0))

    return pl.pallas_call(
        _combine_kernel,
        grid=(S, n),
        in_specs=[
            pl.BlockSpec((CAP, D_MODEL), lambda s, i: (s * n + i, 0)),
            tok(ROUTER_LANES), tok(D_MODEL),
            pl.BlockSpec((None, 1, 6 * D_MODEL), lambda s, i: (row_fn(s), 0, 0)),
            pl.BlockSpec((1, D_MODEL), lambda s, i: (0, 0)),
            pl.BlockSpec((1, D_MODEL), lambda s, i: (0, 0)),
        ],
        out_specs=tok(D_MODEL),
        out_shape=jax.ShapeDtypeStruct((S, T, D_MODEL), F32),
        compiler_params=_params(("parallel", "parallel")),
        name="moe_combine",
    )(ys, slots, x1, mod_l, lng, lnb)


def _ffn(x1, hs, slots, cnt, mod_l, row_fn, experts, ln2):
    S = x1.shape[0]
    eb, nv, cid = _block_schedule(cnt.reshape(-1, ROUTER_LANES)[:, :N_EXPERTS])
    ys = _moe(hs.reshape(-1, HS_W), eb, nv, cid, *experts)
    return _combine(ys, slots, x1, mod_l, row_fn, *ln2)


def _rope_tables(n_lat):
    half = DIFF_QK // 2
    pairs = half // 2
    inv = (1.0 / (ROPE_BASE ** (np.arange(pairs, dtype=np.float32) * 2.0 / half))).astype(np.float32)
    t = np.arange(n_lat)
    ang_r = ((t // GRID_W).astype(np.float32)[:, None] * inv[None, :]).astype(np.float64)
    ang_c = ((t % GRID_W).astype(np.float32)[:, None] * inv[None, :]).astype(np.float64)
    cos = np.concatenate([np.cos(ang_r)] * 2 + [np.cos(ang_c)] * 2, axis=1)
    sin = np.concatenate([-np.sin(ang_r), np.sin(ang_r), -np.sin(ang_c), np.sin(ang_c)], axis=1)
    reps = DIFF_W // DIFF_QK
    return (jnp.asarray(np.tile(cos, (1, reps)), F32), jnp.asarray(np.tile(sin, (1, reps)), F32))


def _block_diag(s):
    S = s.shape[0]
    eye = jnp.eye(H_RET, dtype=s.dtype)
    return jnp.einsum('shdv,hg->shdgv', s, eye).reshape(S, RET_W, RET_W)


def _diag_blocks(s):
    S = s.shape[0]
    s5 = s.reshape(S, H_RET, RET_DK, H_RET, RET_DK)
    return jnp.stack([s5[:, hd, :, hd, :] for hd in range(H_RET)], axis=1)


def kernel(x_prompt, x_sample, cache_diff_k, cache_diff_v, state_ret_fwd, state_ret_bwd, c, c_ctx, w_mod, b_mod, w_in, ret_decay_logit, diff_lambda, diff_subln_g, conv_w, conv_b, w_out, ln_g, ln_b, w_router_group, b_router_group, w_router_expert, b_router_expert, w_gate, w_up, w_down):
    B, T_ctx, _ = x_prompt.shape
    Bd, T_lat, _ = x_sample.shape
    assert T_ctx == TILE and T_lat % TILE == 0 and T_lat % GRID_W == 0
    assert 1 + Bd <= MOD_ROWS

    cond = jnp.concatenate([c_ctx[None, :], c, jnp.zeros((MOD_ROWS - 1 - Bd, D_MODEL), F32)], axis=0)
    mod = _modulation(cond, w_mod, b_mod).reshape(DEPTH, MOD_ROWS, 1, 6 * D_MODEL)
    rope_tabs = _rope_tables(T_lat)

    w_in_b = w_in.astype(BF)
    w_out_b = w_out.astype(BF)
    wg_b = w_gate.astype(BF)
    wu_b = w_up.astype(BF)
    wd_b = w_down.astype(BF)

    ctx_row = lambda s: 0
    lat_row = lambda s: s + 1

    yp, ys = x_prompt, x_sample
    ks_l, vs_l, sf_l, sb_l = [], [], [], []
    for l in range(DEPTH):
        lam_init = 0.8 - 0.6 * math.exp(-0.3 * l)
        lg_lanes = jnp.repeat(ret_decay_logit[l], RET_DK, axis=1)
        w_r = jnp.concatenate([w_router_group[l], w_router_expert[l].reshape(D_MODEL, N_EXPERTS)], axis=1)
        w_r = jnp.pad(w_r, ((0, 0), (0, ROUTER_LANES - N_GROUPS - N_EXPERTS)))
        wrh = w_r.astype(BF)
        wrl = (w_r - wrh.astype(F32)).astype(BF)
        br = jnp.pad(jnp.concatenate([b_router_group[l], b_router_expert[l].reshape(N_EXPERTS)]),
                     (0, ROUTER_LANES - N_GROUPS - N_EXPERTS)).reshape(1, ROUTER_LANES)
        shared = (w_out_b[l], lg_lanes, diff_lambda[l], diff_subln_g[l].reshape(1, DIFF_V), conv_w[l],
                  conv_b[l].reshape(1, CONV_CH), ln_g[l, 0].reshape(1, D_MODEL), ln_b[l, 0].reshape(1, D_MODEL),
                  wrh, wrl, br, lam_init)
        ln2 = (ln_g[l, 1].reshape(1, D_MODEL), ln_b[l, 1].reshape(1, D_MODEL))
        experts = (wg_b[l], wu_b[l], wd_b[l])

        pr = _proj(yp, mod[l], ctx_row, w_in_b[l], lg_lanes, None, True)
        rq, rk, rv, rg, dq, dk, dv, cb, cch, kvf, kvb, dk32, dv32 = pr
        zeros = jnp.zeros((B, RET_W, RET_W), F32)
        sf, sb, ff, fb = _scan(kvf, kvb, zeros, zeros, lg_lanes)
        x1, hs, slots, cnt = _mixer(yp, mod[l], ctx_row, (rq, rk, rv, rg, dq, cb, cch), sf, sb, dk, dv, *shared)
        yp = _ffn(x1, hs, slots, cnt, mod[l], ctx_row, experts, ln2)
        ks_l.append(dk32.reshape(B, T_ctx, H_DIFF, 2, DIFF_QK))
        vs_l.append(dv32.reshape(B, T_ctx, H_DIFF, DIFF_V))
        sf_l.append(_diag_blocks(ff))
        sb_l.append(_diag_blocks(fb))

        pr = _proj(ys, mod[l], lat_row, w_in_b[l], lg_lanes, rope_tabs, False)
        rq, rk, rv, rg, dq, dk, dv, cb, cch, kvf, kvb = pr
        sf, sb, _, _ = _scan(kvf, kvb, _block_diag(state_ret_fwd[:, l]), _block_diag(state_ret_bwd[:, l]), lg_lanes)
        k_all = jnp.concatenate([cache_diff_k[:, l].reshape(Bd, -1, DIFF_W).astype(BF), dk], axis=1)
        v_all = jnp.concatenate([cache_diff_v[:, l].reshape(Bd, -1, DIFF_W).astype(BF), dv], axis=1)
        x1, hs, slots, cnt = _mixer(ys, mod[l], lat_row, (rq, rk, rv, rg, dq, cb, cch), sf, sb, k_all, v_all, *shared)
        ys = _ffn(x1, hs, slots, cnt, mod[l], lat_row, experts, ln2)

    return (yp, ys, jnp.stack(ks_l, axis=1), jnp.stack(vs_l, axis=1),
            jnp.stack(sf_l, axis=1), jnp.stack(sb_l, axis=1))
```

```python
import functools
import math

import numpy as np
import jax
import jax.numpy as jnp
from jax import lax
from jax.experimental import pallas as pl
from jax.experimental.pallas import tpu as pltpu

D_MODEL = 1024
DEPTH = 2
GRID_W = 64
H_RET = 4
RET_DK = 64
RET_W = H_RET * RET_DK
H_DIFF = 4
DIFF_QK = 64
DIFF_V = 2 * DIFF_QK
DIFF_W = H_DIFF * DIFF_V
CONV_CH = 256
ROPE_BASE = 10000.0
N_GROUPS = 4
EXPERTS_PER_GROUP = 4
N_EXPERTS = N_GROUPS * EXPERTS_PER_GROUP
D_EXPERT = 512
ALPHA = (2 * DEPTH) ** 0.25
EPS = 1e-5
MOD_ROWS = 8
LANES = 128
ROUTER_ROWS = 32
EXPERT_ROW0 = 8

TILE = 256
Q_SUB = 128
CHUNK = 16
CAP = 2 * TILE + N_EXPERTS * CHUNK
CHUNKS_PER_TILE = CAP // CHUNK
MOE_TMB = 256
CPB = MOE_TMB // CHUNK
HS_W = D_MODEL + LANES
SLOT_RADIX = 32

_O_RQ, _O_RK, _O_RV, _O_RG = 0, 256, 512, 768
_O_DQ, _O_DK, _O_DV = 1024, 1536, 2048
_O_CB, _O_CC, _O_CH = 2560, 2816, 3072
IN_WIDTH = 3328

BF = jnp.bfloat16
F32 = jnp.float32
I32 = jnp.int32

_VMEM_LIMIT = 56 * 1024 * 1024


def _dot(a, b):
    return jnp.dot(a, b, preferred_element_type=F32)


def _dot_nt(a, b):
    return lax.dot_general(a, b, (((1,), (1,)), ((), ())), preferred_element_type=F32)


def _dot_tn(a, b):
    return lax.dot_general(a, b, (((0,), (0,)), ((), ())), preferred_element_type=F32)


def _split(x):
    hi = x.astype(BF)
    lo = (x - hi.astype(F32)).astype(BF)
    return hi, lo


def _dot_hl(x, w_bf16):
    hi, lo = _split(x)
    return _dot(hi, w_bf16) + _dot(lo, w_bf16)


def _dot3(x, w_hi, w_lo):
    hi, lo = _split(x)
    return _dot(hi, w_hi) + (_dot(lo, w_hi) + _dot(hi, w_lo))


def _iota(shape, dim):
    return lax.broadcasted_iota(I32, shape, dim)


def _ones_where(cond):
    return jnp.where(cond, 1.0, 0.0).astype(BF)


def _log_sigmoid(x):
    return jnp.minimum(x, 0.0) - jnp.log1p(jnp.exp(-jnp.abs(x)))


def _silu(x):
    return x * jax.nn.sigmoid(x)


def _layer_norm(y, g, b):
    mu = jnp.mean(y, axis=-1, keepdims=True)
    yc = y - mu
    var = jnp.mean(yc * yc, axis=-1, keepdims=True)
    return yc * lax.rsqrt(var + EPS) * g + b


def _params(sem):
    return pltpu.CompilerParams(dimension_semantics=sem, vmem_limit_bytes=_VMEM_LIMIT)


_HBM = pl.BlockSpec(memory_space=pltpu.HBM)


def _mod_kernel(c_ref, w_ref, b_ref, o_ref):
    a = _silu(c_ref[...])
    w_hi, w_lo = _split(w_ref[...])
    o_ref[...] = _dot3(a, w_hi, w_lo) + b_ref[...]


def _modulation(cond, w_mod, b_mod):
    tn = 1536
    n6 = 6 * D_MODEL
    return pl.pallas_call(
        _mod_kernel,
        grid=(DEPTH, n6 // tn),
        in_specs=[
            pl.BlockSpec((MOD_ROWS, D_MODEL), lambda l, j: (0, 0)),
            pl.BlockSpec((None, D_MODEL, tn), lambda l, j: (l, 0, j)),
            pl.BlockSpec((None, 1, tn), lambda l, j: (l, 0, j)),
        ],
        out_specs=pl.BlockSpec((None, MOD_ROWS, tn), lambda l, j: (l, 0, j)),
        out_shape=jax.ShapeDtypeStruct((DEPTH, MOD_ROWS, n6), F32),
        compiler_params=_params(("parallel", "parallel")),
        name="modulation",
    )(cond, w_mod, b_mod.reshape(DEPTH, 1, n6))


def _mod_spec(layer, row_fn):
    return pl.BlockSpec((None, None, 1, 6 * D_MODEL), lambda s, i: (layer, row_fn(s), 0, 0))


def _swap16(x):
    n = x.shape[-1]
    lane = _iota(x.shape, 1)
    return jnp.where((lane & 16) == 0, pltpu.roll(x, n - 16, axis=1), pltpu.roll(x, 16, axis=1))


def _proj_kernel(*refs, latent, n_aliased, layer):
    x_ref, mod_ref, w_ref, lg_ref = refs[:4]
    pos = 4
    if latent:
        cos_ref, sin_ref = refs[pos:pos + 2]
        pos += 2
    pos += n_aliased
    rq_ref, rk_ref, rv_ref, rg_ref, dq_ref, dk_ref, dv_ref, cb_ref, cch_ref = refs[pos:pos + 9]
    pos += 9
    if latent:
        kvf_ref, kvb_ref = refs[pos:pos + 2]
    else:
        dk32_ref, dv32_ref, stf_ref, stb_ref = refs[pos:pos + 4]

    def put(ref, idx, val):
        if n_aliased:
            ref[idx if idx else ...] = val
        else:
            for other in range(DEPTH):
                ref[(other,) + idx] = val if other == layer else jnp.zeros_like(val)

    m = mod_ref[...]
    sh1 = m[:, 0:D_MODEL]
    sc1 = m[:, D_MODEL:2 * D_MODEL]
    h = (x_ref[...] * (1.0 + sc1) + sh1).astype(BF)

    def col(off, width):
        return _dot(h, w_ref[:, off:off + width])

    rq_ref[...] = col(_O_RQ, RET_W).astype(BF)
    rk = col(_O_RK, RET_W) * (RET_DK ** -0.5)
    rk_ref[...] = rk.astype(BF)
    rv = col(_O_RV, RET_W).astype(BF)
    rv_ref[...] = rv
    rg_ref[...] = col(_O_RG, RET_W)

    lg = _log_sigmoid(lg_ref[...])
    p = _iota((TILE, 1), 0).astype(F32)
    kf = (rk * jnp.exp((TILE - 1.0 - p) * lg[0:1])).astype(BF)
    kb = (rk * jnp.exp(p * lg[1:2])).astype(BF)
    kvf = _dot_tn(kf, rv)
    kvb = _dot_tn(kb, rv)
    if latent:
        kvf_ref[...] = kvf
        kvb_ref[...] = kvb
    else:
        for hd in range(H_RET):
            lo = hd * RET_DK
            put(stf_ref, (hd,), kvf[lo:lo + RET_DK, lo:lo + RET_DK])
            put(stb_ref, (hd,), kvb[lo:lo + RET_DK, lo:lo + RET_DK])

    dq = col(_O_DQ, DIFF_W)
    dk = col(_O_DK, DIFF_W)
    dv = col(_O_DV, DIFF_W)
    if latent:
        cos = cos_ref[...]
        sin = sin_ref[...]
        dq = dq * cos + _swap16(dq) * sin
        dk = dk * cos + _swap16(dk) * sin
    else:
        put(dk32_ref, (), dk)
        put(dv32_ref, (), dv)
    dq_ref[...] = (dq * (DIFF_QK ** -0.5)).astype(BF)
    dk_ref[...] = dk.astype(BF)
    dv_ref[...] = dv.astype(BF)

    cb_ref[...] = col(_O_CB, CONV_CH)
    cch_ref[...] = col(_O_CC, CONV_CH) * col(_O_CH, CONV_CH)


def _proj(x, mod, layer, row_fn, w_in_b, lg_lanes, rope_tabs=None, stacked=None):
    S, T, _ = x.shape
    n = T // TILE
    latent = rope_tabs is not None

    def tok(w):
        return pl.BlockSpec((None, TILE, w), lambda s, i: (s, i, 0))

    def tok_shape(w, dt):
        return jax.ShapeDtypeStruct((S, T, w), dt)

    in_specs = [
        tok(D_MODEL), _mod_spec(layer, row_fn),
        pl.BlockSpec((None, D_MODEL, IN_WIDTH), lambda s, i: (layer, 0, 0)),
        pl.BlockSpec((2, RET_W), lambda s, i: (0, 0)),
    ]
    args = [x, mod, w_in_b, lg_lanes]
    out_specs = [tok(RET_W)] * 4 + [tok(DIFF_W)] * 3 + [tok(CONV_CH)] * 2
    out_shape = ([tok_shape(RET_W, BF)] * 3 + [tok_shape(RET_W, F32)] + [tok_shape(DIFF_W, BF)] * 3
                 + [tok_shape(CONV_CH, F32)] * 2)
    aliases = {}
    if latent:
        in_specs += [pl.BlockSpec((TILE, DIFF_W), lambda s, i: (i, 0))] * 2
        args += list(rope_tabs)
        out_specs += [pl.BlockSpec((None, None, RET_W, RET_W), lambda s, i: (s, i, 0, 0))] * 2
        out_shape += [jax.ShapeDtypeStruct((S, n, RET_W, RET_W), F32)] * 2
    else:
        assert n == 1
        kv_shape = jax.ShapeDtypeStruct((S, DEPTH, T, DIFF_W), F32)
        st_shape = jax.ShapeDtypeStruct((S, DEPTH, H_RET, RET_DK, RET_DK), F32)
        if stacked is not None:
            in_specs += [_HBM] * 4
            args += list(stacked)
            aliases = {len(args) - 4 + j: len(out_shape) + j for j in range(4)}
            out_specs += [pl.BlockSpec((None, None, TILE, DIFF_W), lambda s, i: (s, layer, i, 0))] * 2
            out_specs += [pl.BlockSpec((None, None, H_RET, RET_DK, RET_DK), lambda s, i: (s, layer, 0, 0, 0))] * 2
        else:
            out_specs += [pl.BlockSpec((None, DEPTH, TILE, DIFF_W), lambda s, i: (s, 0, i, 0))] * 2
            out_specs += [pl.BlockSpec((None, DEPTH, H_RET, RET_DK, RET_DK), lambda s, i: (s, 0, 0, 0, 0))] * 2
        out_shape += [kv_shape, kv_shape, st_shape, st_shape]
    return pl.pallas_call(
        functools.partial(_proj_kernel, latent=latent, n_aliased=len(aliases), layer=layer),
        grid=(S, n),
        in_specs=in_specs,
        out_specs=out_specs,
        out_shape=out_shape,
        input_output_aliases=aliases,
        compiler_params=_params(("parallel", "parallel")),
        name="proj_lat" if latent else "proj_ctx",
    )(*args)


def _scan_kernel(kvf_ref, kvb_ref, s0f_ref, s0b_ref, lg_ref, sf_ref, sb_ref, *, n):
    lg = _log_sigmoid(lg_ref[...])
    dec = jnp.exp(float(TILE) * lg)
    same_head = (_iota((RET_W, RET_W), 0) >> 6) == (_iota((RET_W, RET_W), 1) >> 6)

    s = jnp.where(same_head, s0f_ref[...], 0.0)
    for c in range(n):
        sf_ref[c] = s
        s = s * dec[0:1] + jnp.where(same_head, kvf_ref[c], 0.0)
    s = jnp.where(same_head, s0b_ref[...], 0.0)
    for c in reversed(range(n)):
        sb_ref[c] = s
        s = s * dec[1:2] + jnp.where(same_head, kvb_ref[c], 0.0)


def _scan(kvf, kvb, s0f, s0b, lg_lanes):
    S, n = kvf.shape[:2]
    chunks = pl.BlockSpec((None, n, RET_W, RET_W), lambda s: (s, 0, 0, 0))
    one = pl.BlockSpec((None, RET_W, RET_W), lambda s: (s, 0, 0))
    return pl.pallas_call(
        functools.partial(_scan_kernel, n=n),
        grid=(S,),
        in_specs=[chunks, chunks, one, one, pl.BlockSpec((2, RET_W), lambda s: (0, 0))],
        out_specs=[chunks, chunks],
        out_shape=[jax.ShapeDtypeStruct((S, n, RET_W, RET_W), F32)] * 2,
        compiler_params=_params(("parallel",)),
        name="ret_scan",
    )(kvf, kvb, s0f, s0b, lg_lanes)


def _diff_head(q_h, kv_refs, lo, lam):
    lane = _iota((1, DIFF_V), 1)
    q0 = jnp.where(lane < DIFF_QK, q_h, 0)
    q1 = jnp.where(lane >= DIFF_QK, q_h, 0)
    s0 = jnp.concatenate([_dot_nt(q0, k[:, lo:lo + DIFF_V]) for k, _ in kv_refs], axis=1)
    s1 = jnp.concatenate([_dot_nt(q1, k[:, lo:lo + DIFF_V]) for k, _ in kv_refs], axis=1)
    p0 = jnp.exp(s0 - jnp.max(s0, axis=-1, keepdims=True))
    p1 = jnp.exp(s1 - jnp.max(s1, axis=-1, keepdims=True))
    r0 = 1.0 / jnp.sum(p0, axis=-1, keepdims=True)
    r1 = lam / jnp.sum(p1, axis=-1, keepdims=True)
    a = (p0 * r0 - p1 * r1).astype(BF)
    o = None
    start = 0
    for k, v in kv_refs:
        part = _dot(a[:, start:start + k.shape[0]], v[:, lo:lo + DIFF_V])
        o = part if o is None else o + part
        start += k.shape[0]
    return o


def _mixer_kernel(*refs, n_tiles, lam_init, cached):
    (x_ref, mod_ref, rq_ref, rk_ref, rv_ref, rg_ref, sf_ref, sb_ref, dq_ref) = refs[:9]
    pos = 9
    kv_refs = []
    if cached:
        kv_refs.append((refs[pos], refs[pos + 1]))
        pos += 2
    kv_refs.append((refs[pos], refs[pos + 1]))
    pos += 2
    (cb_ref, cch_ref, cprev_ref, cnext_ref, wout_ref, lg_ref, lamp_ref, subg_ref, cw_ref, cbias_ref,
     lng_ref, lnb_ref, wrh_ref, wrl_ref, br_ref) = refs[pos:pos + 15]
    pos += 15
    x1_ref, hs_ref, slots_ref, cnt_ref = refs[pos:pos + 4]

    i = pl.program_id(1)
    m = mod_ref[...]
    g1 = m[:, 2 * D_MODEL:3 * D_MODEL]
    sh2 = m[:, 3 * D_MODEL:4 * D_MODEL]
    sc2 = m[:, 4 * D_MODEL:5 * D_MODEL]

    lg = _log_sigmoid(lg_ref[...])
    head_of_lane = _iota((1, RET_W), 1) >> 6
    q = rq_ref[...]
    k = rk_ref[...]
    v = rv_ref[...]
    dist = (_iota((TILE, TILE), 0) - _iota((TILE, TILE), 1)).astype(F32)
    adist = jnp.abs(dist)
    diag2 = jnp.where(dist == 0.0, 2.0, 1.0)
    ret_o = jnp.zeros((TILE, RET_W), F32)
    for hd in range(H_RET):
        in_head = head_of_lane == hd
        sc = _dot_nt(jnp.where(in_head, q, 0), k)
        lgf = lg[0:1, hd * RET_DK:hd * RET_DK + 1]
        lgb = lg[1:2, hd * RET_DK:hd * RET_DK + 1]
        decay = jnp.exp(adist * jnp.where(dist > 0.0, lgf, lgb)) * diag2
        ret_o = ret_o + jnp.where(in_head, _dot((sc * decay).astype(BF), v), 0.0)
    p = _iota((TILE, 1), 0).astype(F32)
    ret_o = ret_o + _dot(q, sf_ref[...].astype(BF)) * jnp.exp((p + 1.0) * lg[0:1])
    ret_o = ret_o + _dot(q, sb_ref[...].astype(BF)) * jnp.exp((float(TILE) - p) * lg[1:2])
    avg = jnp.where((_iota((RET_W, RET_W), 0) >> 6) == (_iota((RET_W, RET_W), 1) >> 6),
                    1.0 / RET_DK, 0.0).astype(BF)
    rc = ret_o - _dot_hl(ret_o, avg)
    ret = rc * lax.rsqrt(_dot_hl(rc * rc, avg) + EPS) * _silu(rg_ref[...])

    lp = lamp_ref[...]
    lam = (jnp.exp(jnp.sum(lp[0:1] * lp[1:2], axis=-1, keepdims=True))
           - jnp.exp(jnp.sum(lp[2:3] * lp[3:4], axis=-1, keepdims=True)) + lam_init)
    subg = subg_ref[...] * (1.0 - lam_init)
    rows = []
    for qs in range(TILE // Q_SUB):
        heads = []
        for hd in range(H_DIFF):
            lo = hd * DIFF_V
            o = _diff_head(dq_ref[qs * Q_SUB:(qs + 1) * Q_SUB, lo:lo + DIFF_V], kv_refs, lo, lam)
            o = o * lax.rsqrt(jnp.mean(o * o, axis=-1, keepdims=True) + EPS) * subg
            heads.append(o.astype(BF))
        rows.append(jnp.concatenate(heads, axis=1))
    diff = jnp.concatenate(rows, axis=0)

    cch = cch_ref[...]
    prev = jnp.where(i > 0, cprev_ref[7:8, :], 0.0)
    nxt = jnp.where(i < n_tiles - 1, cnext_ref[0:1, :], 0.0)
    r = _iota((TILE, 1), 0)
    up = jnp.where(r == 0, prev, pltpu.roll(cch, 1, axis=0))
    dn = jnp.where(r == TILE - 1, nxt, pltpu.roll(cch, TILE - 1, axis=0))
    cw = cw_ref[...]
    conv = cb_ref[...] * (up * cw[0:1] + cch * cw[1:2] + dn * cw[2:3] + cbias_ref[...])

    mix = (_dot(ret.astype(BF), wout_ref[0:RET_W, :])
           + _dot(diff, wout_ref[RET_W:RET_W + DIFF_W, :])
           + _dot(conv.astype(BF), wout_ref[RET_W + DIFF_W:, :]))
    x1 = _layer_norm(ALPHA * x_ref[...] + g1 * mix, lng_ref[...], lnb_ref[...])
    x1_ref[...] = x1

    h2 = x1 * (1.0 + sc2) + sh2
    h2h, h2l = _split(h2)
    wrh = wrh_ref[...]
    logits = (_dot_nt(wrh, h2h) + (_dot_nt(wrh, h2l) + _dot_nt(wrl_ref[...], h2h))) + br_ref[...]
    neg = -jnp.inf
    g_id = _iota((EXPERT_ROW0, TILE), 0)
    gl = jnp.where(g_id < N_GROUPS, logits[0:EXPERT_ROW0], neg)
    gmax = jnp.max(gl, axis=0, keepdims=True)
    g_idx = jnp.min(jnp.where(gl == gmax, g_id, N_GROUPS), axis=0, keepdims=True)
    g_w = 1.0 / jnp.sum(jnp.exp(gl - gmax), axis=0, keepdims=True)
    e_id = _iota((N_EXPERTS, TILE), 0)
    el = jnp.where((e_id >> 2) == g_idx, logits[EXPERT_ROW0:EXPERT_ROW0 + N_EXPERTS], neg)
    v1 = jnp.max(el, axis=0, keepdims=True)
    i1 = jnp.min(jnp.where(el == v1, e_id, N_EXPERTS), axis=0, keepdims=True)
    el2 = jnp.where(e_id == i1, neg, el)
    v2 = jnp.max(el2, axis=0, keepdims=True)
    i2 = jnp.min(jnp.where(el2 == v2, e_id, N_EXPERTS), axis=0, keepdims=True)
    t = jnp.exp(v2 - v1)
    w1 = g_w / (1.0 + t)
    w2 = g_w * t / (1.0 + t)

    sel1 = e_id == i1
    sel2 = e_id == i2
    sel = jnp.where(sel1 | sel2, 1.0, 0.0)
    earlier = _ones_where(_iota((TILE, TILE), 0) < _iota((TILE, TILE), 1))
    rank = _dot(sel.astype(BF), earlier)
    cnt = jnp.sum(sel, axis=1, keepdims=True).astype(I32)
    padded = ((cnt + (CHUNK - 1)) >> 4) << 4
    incl = jnp.broadcast_to(padded, (N_EXPERTS, LANES))
    e_row = _iota((N_EXPERTS, LANES), 0)
    for step in (1, 2, 4, 8):
        incl = incl + jnp.where(e_row >= step, pltpu.roll(incl, step, axis=0), 0)
    seg_off = (incl[:, 0:1] - padded).astype(F32)
    spos = seg_off + rank
    slot1 = jnp.sum(jnp.where(sel1, spos, 0.0), axis=0, keepdims=True).astype(I32)
    slot2 = jnp.sum(jnp.where(sel2, spos, 0.0), axis=0, keepdims=True).astype(I32)
    s_id = _iota((CAP, TILE), 0)
    at1 = s_id == slot1
    at2 = s_id == slot2
    hs_ref[:, 0:D_MODEL] = _dot(_ones_where(at1 | at2), h2h).astype(BF)
    w1h = w1.astype(BF).astype(F32)
    w2h = w2.astype(BF).astype(F32)
    ones = jnp.ones((TILE, LANES), BF)
    g_hi = _dot((jnp.where(at1, w1h, 0.0) + jnp.where(at2, w2h, 0.0)).astype(BF), ones)
    g_lo = _dot((jnp.where(at1, w1 - w1h, 0.0) + jnp.where(at2, w2 - w2h, 0.0)).astype(BF), ones)
    lane = _iota((CAP, LANES), 1)
    hs_ref[:, D_MODEL:] = jnp.where(lane == 0, g_hi, jnp.where(lane == 1, g_lo, 0.0)).astype(BF)
    cnt_ref[...] = jnp.broadcast_to(padded, (N_EXPERTS, LANES))

    dr = _iota((LANES, TILE), 0)
    digits = jnp.where(dr == 0, slot1 & (SLOT_RADIX - 1), jnp.where(dr == 1, slot1 >> 5,
             jnp.where(dr == 2, slot2 & (SLOT_RADIX - 1), jnp.where(dr == 3, slot2 >> 5, 0))))
    eye = _ones_where(_iota((TILE, TILE), 0) == _iota((TILE, TILE), 1))
    cols = _dot_nt(eye, digits.astype(F32).astype(BF))
    s1c = (cols[:, 0:1] + SLOT_RADIX * cols[:, 1:2]).astype(I32)
    s2c = (cols[:, 2:3] + SLOT_RADIX * cols[:, 3:4]).astype(I32)
    tl = _iota((TILE, LANES), 1)
    slots_ref[...] = jnp.where(tl == 0, s1c, jnp.where(tl == 1, s2c, 0))


def _mixer(x, mod, layer, row_fn, pr, states, cache_kv, new_kv, w_out_b, lg_lanes, lamp, subg, cw, cbias,
           lng, lnb, wrh, wrl, br, lam_init):
    S, T, _ = x.shape
    n = T // TILE
    rq, rk, rv, rg, dq, cb, cch = pr
    t8 = TILE // 8
    cached = cache_kv is not None

    def tok(w):
        return pl.BlockSpec((None, TILE, w), lambda s, i: (s, i, 0))

    def full(shape):
        return pl.BlockSpec(shape, lambda s, i: (0,) * len(shape))

    def seq(a):
        return pl.BlockSpec((None,) + a.shape[1:], lambda s, i: (s, 0, 0))

    if states is None:
        zero_state = jnp.zeros((1, 1, RET_W, RET_W), F32)
        states = (zero_state, zero_state)
        state = pl.BlockSpec((None, None, RET_W, RET_W), lambda s, i: (0, 0, 0, 0))
    else:
        state = pl.BlockSpec((None, None, RET_W, RET_W), lambda s, i: (s, i, 0, 0))
    halo_prev = pl.BlockSpec((None, 8, CONV_CH), lambda s, i: (s, jnp.maximum(i * t8 - 1, 0), 0))
    halo_next = pl.BlockSpec((None, 8, CONV_CH), lambda s, i: (s, jnp.minimum((i + 1) * t8, T // 8 - 1), 0))
    in_specs = [tok(D_MODEL), _mod_spec(layer, row_fn), tok(RET_W), tok(RET_W), tok(RET_W), tok(RET_W),
                state, state, tok(DIFF_W)]
    args = [x, mod, rq, rk, rv, rg, states[0], states[1], dq]
    if cached:
        in_specs += [seq(cache_kv[0]), seq(cache_kv[1])]
        args += list(cache_kv)
    in_specs += [seq(new_kv[0]), seq(new_kv[1])]
    args += list(new_kv)
    in_specs += [
        tok(CONV_CH), tok(CONV_CH), halo_prev, halo_next,
        pl.BlockSpec((None, D_MODEL, D_MODEL), lambda s, i: (layer, 0, 0)),
        full((2, RET_W)), full((4, DIFF_QK)), full((1, DIFF_V)),
        full((3, CONV_CH)), full((1, CONV_CH)), full((1, D_MODEL)), full((1, D_MODEL)),
        full((ROUTER_ROWS, D_MODEL)), full((ROUTER_ROWS, D_MODEL)), full((ROUTER_ROWS, 1)),
    ]
    args += [cb, cch, cch, cch, w_out_b, lg_lanes, lamp, subg, cw, cbias, lng, lnb, wrh, wrl, br]
    return pl.pallas_call(
        functools.partial(_mixer_kernel, n_tiles=n, lam_init=lam_init, cached=cached),
        grid=(S, n),
        in_specs=in_specs,
        out_specs=[tok(D_MODEL),
                   pl.BlockSpec((CAP, HS_W), lambda s, i: (s * n + i, 0)),
                   tok(LANES),
                   pl.BlockSpec((None, None, N_EXPERTS, LANES), lambda s, i: (s, i, 0, 0))],
        out_shape=[jax.ShapeDtypeStruct((S, T, D_MODEL), F32),
                   jax.ShapeDtypeStruct((S * n * CAP, HS_W), BF),
                   jax.ShapeDtypeStruct((S, T, LANES), I32),
                   jax.ShapeDtypeStruct((S, n, N_EXPERTS, LANES), I32)],
        compiler_params=_params(("parallel", "parallel")),
        name="mixer_lat" if cached else "mixer_ctx",
    )(*args)


def _block_schedule(padded):
    NT = padded.shape[0]
    n_chunks = NT * CHUNKS_PER_TILE
    n_blocks = n_chunks // CPB + N_EXPERTS
    c16 = padded // CHUNK
    ends = jnp.cumsum(c16, axis=1)
    before = jnp.cumsum(c16, axis=0) - c16
    per_expert = jnp.sum(c16, axis=0)
    nb = (per_expert + (CPB - 1)) // CPB
    b_end = jnp.cumsum(nb)
    q = jnp.arange(CHUNKS_PER_TILE, dtype=I32)
    e = jnp.arange(N_EXPERTS, dtype=I32)
    key = jnp.sum((ends[:, None, :] <= q[None, :, None]).astype(I32), axis=-1)
    pos_e = ((b_end - nb) * CPB + before)[:, None, :] + (q[None, :, None] - (ends - c16)[:, None, :])
    pos = jnp.sum(jnp.where(key[..., None] == e, pos_e, 0), axis=-1)
    pos = jnp.where(key < N_EXPERTS, pos, -1).reshape(-1)
    match = pos[None, :] == jnp.arange(n_blocks * CPB, dtype=I32)[:, None]
    cid = jnp.sum(jnp.where(match, jnp.arange(n_chunks, dtype=I32)[None, :], 0), axis=-1)
    nv = jnp.sum(match.astype(I32).reshape(n_blocks, -1), axis=-1)
    b = jnp.arange(n_blocks, dtype=I32)
    eb = jnp.minimum(jnp.sum((b_end[None, :] <= b[:, None]).astype(I32), axis=-1), N_EXPERTS - 1)
    return eb.astype(I32), nv.astype(I32), cid.astype(I32)


def _moe_kernel(eb_ref, nv_ref, cid_ref, hs_a, hs_b, wg_ref, wu_ref, wd_ref, ys_a, ys_b,
                xbuf, ybuf, wg_b, wu_b, wd_b, in_sem, out_sem, *, n_blocks, chunks_a):
    b = pl.program_id(0)

    def chunk_rows(c):
        return pl.ds(pl.multiple_of(c * CHUNK, CHUNK), CHUNK)

    def gather_copy(which, c, j, slot):
        src = (hs_a, hs_b)[which]
        return pltpu.make_async_copy(src.at[chunk_rows(c)], xbuf.at[slot, chunk_rows(j)], in_sem.at[slot])

    def scatter_copy(which, c, j, slot):
        dst = (ys_a, ys_b)[which]
        return pltpu.make_async_copy(ybuf.at[slot, chunk_rows(j)],
                                     dst.at[chunk_rows(c), pl.ds(0, D_MODEL)], out_sem.at[slot])

    def for_chunks(blk, fn):
        def body(j, carry):
            c = cid_ref[blk * CPB + j]

            @pl.when(c < chunks_a)
            def _():
                fn(j, 0, c)

            @pl.when(c >= chunks_a)
            def _():
                fn(j, 1, c - chunks_a)
            return carry
        lax.fori_loop(0, nv_ref[blk], body, 0)

    slot = lax.rem(b, 2)
    other = 1 - slot

    @pl.when(b == 0)
    def _():
        xbuf[...] = jnp.zeros_like(xbuf)
        for_chunks(0, lambda j, w, c: gather_copy(w, c, j, 0).start())

    @pl.when(b + 1 < n_blocks)
    def _():
        for_chunks(b + 1, lambda j, w, c: gather_copy(w, c, j, other).start())

    @pl.when((b == 0) | (eb_ref[b] != eb_ref[jnp.maximum(b - 1, 0)]))
    def _():
        wg_b[...] = wg_ref[...].astype(BF)
        wu_b[...] = wu_ref[...].astype(BF)
        wd_b[...] = wd_ref[...].astype(BF)

    for_chunks(b, lambda j, w, c: gather_copy(w, c, j, slot).wait())

    @pl.when(b >= 2)
    def _():
        for_chunks(b - 2, lambda j, w, c: scatter_copy(w, c, j, slot).wait())

    @pl.when(nv_ref[b] > 0)
    def _():
        x = xbuf[slot, :, 0:D_MODEL]
        gate = (xbuf[slot, :, D_MODEL:D_MODEL + 1].astype(F32)
                + xbuf[slot, :, D_MODEL + 1:D_MODEL + 2].astype(F32))
        hg = _dot(x, wg_b[...])
        hu = _dot(x, wu_b[...])
        act = _silu(hg) * hu * gate
        ybuf[slot] = _dot(act.astype(BF), wd_b[...]).astype(BF)

    for_chunks(b, lambda j, w, c: scatter_copy(w, c, j, slot).start())

    @pl.when(b == n_blocks - 1)
    def _():
        if n_blocks >= 2:
            for_chunks(b - 1, lambda j, w, c: scatter_copy(w, c, j, other).wait())
        for_chunks(b, lambda j, w, c: scatter_copy(w, c, j, slot).wait())


def _moe(hs_a, hs_b, eb, nv, cid, layer, w_gate, w_up, w_down):
    n_blocks = eb.shape[0]

    def w_spec(a, c):
        return pl.BlockSpec((None, None, a, c), lambda b, eb, nv, cid: (layer, eb[b], 0, 0))

    grid_spec = pltpu.PrefetchScalarGridSpec(
        num_scalar_prefetch=3,
        grid=(n_blocks,),
        in_specs=[_HBM, _HBM, w_spec(D_MODEL, D_EXPERT), w_spec(D_MODEL, D_EXPERT), w_spec(D_EXPERT, D_MODEL)],
        out_specs=[_HBM, _HBM],
        scratch_shapes=[pltpu.VMEM((2, MOE_TMB, HS_W), BF), pltpu.VMEM((2, MOE_TMB, D_MODEL), BF),
                        pltpu.VMEM((D_MODEL, D_EXPERT), BF), pltpu.VMEM((D_MODEL, D_EXPERT), BF),
                        pltpu.VMEM((D_EXPERT, D_MODEL), BF),
                        pltpu.SemaphoreType.DMA((2,)), pltpu.SemaphoreType.DMA((2,))],
    )
    return pl.pallas_call(
        functools.partial(_moe_kernel, n_blocks=n_blocks, chunks_a=hs_a.shape[0] // CHUNK),
        grid_spec=grid_spec,
        out_shape=[jax.ShapeDtypeStruct(hs_a.shape, BF), jax.ShapeDtypeStruct(hs_b.shape, BF)],
        input_output_aliases={3: 0, 4: 1},
        compiler_params=_params(("arbitrary",)),
        name="moe_sorted",
    )(eb, nv, cid, hs_a, hs_b, w_gate, w_up, w_down)


def _combine_kernel(ys_ref, slots_ref, x1_ref, mod_ref, lng_ref, lnb_ref, o_ref):
    sl = slots_ref[...]
    s_lane = _iota((TILE, CAP), 1)
    pick = _ones_where((s_lane == sl[:, 0:1]) | (s_lane == sl[:, 1:2]))
    ffn = _dot(pick, ys_ref[:, 0:D_MODEL])
    g2 = mod_ref[:, 5 * D_MODEL:6 * D_MODEL]
    o_ref[...] = _layer_norm(ALPHA * x1_ref[...] + g2 * ffn, lng_ref[...], lnb_ref[...])


def _combine(ys, slots, x1, mod, layer, row_fn, lng, lnb):
    S, T, _ = x1.shape
    n = T // TILE

    def tok(w):
        return pl.BlockSpec((None, TILE, w), lambda s, i: (s, i, 0))

    return pl.pallas_call(
        _combine_kernel,
        grid=(S, n),
        in_specs=[
            pl.BlockSpec((CAP, HS_W), lambda s, i: (s * n + i, 0)),
            tok(LANES), tok(D_MODEL), _mod_spec(layer, row_fn),
            pl.BlockSpec((1, D_MODEL), lambda s, i: (0, 0)),
            pl.BlockSpec((1, D_MODEL), lambda s, i: (0, 0)),
        ],
        out_specs=tok(D_MODEL),
        out_shape=jax.ShapeDtypeStruct((S, T, D_MODEL), F32),
        compiler_params=_params(("parallel", "parallel")),
        name="moe_combine",
    )(ys, slots, x1, mod, lng, lnb)


def _rope_tables(n_lat):
    half = DIFF_QK // 2
    pairs = half // 2
    inv = (1.0 / (ROPE_BASE ** (np.arange(pairs, dtype=np.float32) * 2.0 / half))).astype(np.float32)
    t = np.arange(n_lat)
    ang_r = ((t // GRID_W).astype(np.float32)[:, None] * inv[None, :]).astype(np.float64)
    ang_c = ((t % GRID_W).astype(np.float32)[:, None] * inv[None, :]).astype(np.float64)
    cos = np.concatenate([np.cos(ang_r)] * 2 + [np.cos(ang_c)] * 2, axis=1)
    sin = np.concatenate([-np.sin(ang_r), np.sin(ang_r), -np.sin(ang_c), np.sin(ang_c)], axis=1)
    reps = DIFF_W // DIFF_QK
    return (jnp.asarray(np.tile(cos, (1, reps)), F32), jnp.asarray(np.tile(sin, (1, reps)), F32))


def _block_diag(s):
    S = s.shape[0]
    eye = jnp.eye(H_RET, dtype=s.dtype)
    return jnp.einsum('shdv,hg->shdgv', s, eye).reshape(S, RET_W, RET_W)


def _router_rows(w_group, b_group, w_expert, b_expert):
    pad = EXPERT_ROW0 - N_GROUPS
    tail = ROUTER_ROWS - EXPERT_ROW0 - N_EXPERTS
    w = jnp.concatenate([w_group.T, jnp.zeros((pad, D_MODEL), F32),
                         w_expert.reshape(D_MODEL, N_EXPERTS).T, jnp.zeros((tail, D_MODEL), F32)], axis=0)
    bias = jnp.concatenate([b_group, jnp.zeros((pad,), F32), b_expert.reshape(N_EXPERTS), jnp.zeros((tail,), F32)])
    hi = w.astype(BF)
    return hi, (w - hi.astype(F32)).astype(BF), bias.reshape(ROUTER_ROWS, 1)


def kernel(x_prompt, x_sample, cache_diff_k, cache_diff_v, state_ret_fwd, state_ret_bwd, c, c_ctx, w_mod, b_mod, w_in, ret_decay_logit, diff_lambda, diff_subln_g, conv_w, conv_b, w_out, ln_g, ln_b, w_router_group, b_router_group, w_router_expert, b_router_expert, w_gate, w_up, w_down):
    B, T_ctx, _ = x_prompt.shape
    Bd, T_lat, _ = x_sample.shape
    assert T_ctx == TILE and T_lat % TILE == 0 and T_lat % GRID_W == 0
    assert 1 + Bd <= MOD_ROWS

    cond = jnp.concatenate([c_ctx[None, :], c, jnp.zeros((MOD_ROWS - 1 - Bd, D_MODEL), F32)], axis=0)
    mod = _modulation(cond, w_mod, b_mod).reshape(DEPTH, MOD_ROWS, 1, 6 * D_MODEL)
    rope_tabs = _rope_tables(T_lat)
    w_in_b = w_in.astype(BF)
    w_out_b = w_out.astype(BF)

    ctx_row = lambda s: 0
    lat_row = lambda s: s + 1
    ctx_tiles = B * (T_ctx // TILE)
    lat_tiles = Bd * (T_lat // TILE)

    yp, ys = x_prompt, x_sample
    stacked = None
    for l in range(DEPTH):
        lam_init = 0.8 - 0.6 * math.exp(-0.3 * l)
        lg_lanes = jnp.repeat(ret_decay_logit[l], RET_DK, axis=1)
        wrh, wrl, br = _router_rows(w_router_group[l], b_router_group[l], w_router_expert[l], b_router_expert[l])
        shared = (w_out_b, lg_lanes, diff_lambda[l], diff_subln_g[l].reshape(1, DIFF_V), conv_w[l],
                  conv_b[l].reshape(1, CONV_CH), ln_g[l, 0].reshape(1, D_MODEL), ln_b[l, 0].reshape(1, D_MODEL),
                  wrh, wrl, br, lam_init)
        ln2 = (ln_g[l, 1].reshape(1, D_MODEL), ln_b[l, 1].reshape(1, D_MODEL))

        pr = _proj(yp, mod, l, ctx_row, w_in_b, lg_lanes, stacked=stacked)
        rq, rk, rv, rg, dq, dk, dv, cb, cch = pr[:9]
        stacked = pr[9:]
        x1_c, hs_c, slots_c, cnt_c = _mixer(yp, mod, l, ctx_row, (rq, rk, rv, rg, dq, cb, cch), None, None,
                                            (dk, dv), *shared)

        pr = _proj(ys, mod, l, lat_row, w_in_b, lg_lanes, rope_tabs=rope_tabs)
        rq, rk, rv, rg, dq, dk, dv, cb, cch, kvf, kvb = pr
        states = _scan(kvf, kvb, _block_diag(state_ret_fwd[:, l]), _block_diag(state_ret_bwd[:, l]), lg_lanes)
        cache_kv = (cache_diff_k[:, l].reshape(Bd, -1, DIFF_W).astype(BF),
                    cache_diff_v[:, l].reshape(Bd, -1, DIFF_W).astype(BF))
        x1_l, hs_l, slots_l, cnt_l = _mixer(ys, mod, l, lat_row, (rq, rk, rv, rg, dq, cb, cch), states, cache_kv,
                                            (dk, dv), *shared)

        padded = jnp.concatenate([cnt_c.reshape(ctx_tiles, N_EXPERTS, LANES)[:, :, 0],
                                  cnt_l.reshape(lat_tiles, N_EXPERTS, LANES)[:, :, 0]], axis=0)
        eb, nv, cid = _block_schedule(padded)
        out_c, out_l = _moe(hs_c, hs_l, eb, nv, cid, l, w_gate, w_up, w_down)
        yp = _combine(out_c, slots_c, x1_c, mod, l, ctx_row, *ln2)
        ys = _combine(out_l, slots_l, x1_l, mod, l, lat_row, *ln2)

    new_k, new_v, st_f, st_b = stacked
    return (yp, ys, new_k.reshape(B, DEPTH, T_ctx, H_DIFF, 2, DIFF_QK),
            new_v.reshape(B, DEPTH, T_ctx, H_DIFF, DIFF_V), st_f, st_b)
```

```python
import functools
import math

import numpy as np
import jax
import jax.numpy as jnp
from jax import lax
from jax.experimental import pallas as pl
from jax.experimental.pallas import tpu as pltpu

D_MODEL = 1024
DEPTH = 2
GRID_W = 64
H_RET = 4
RET_DK = 64
RET_W = H_RET * RET_DK
H_DIFF = 4
DIFF_QK = 64
DIFF_V = 2 * DIFF_QK
DIFF_W = H_DIFF * DIFF_V
CONV_CH = 256
ROPE_BASE = 10000.0
N_GROUPS = 4
EXPERTS_PER_GROUP = 4
N_EXPERTS = N_GROUPS * EXPERTS_PER_GROUP
D_EXPERT = 512
ALPHA = (2 * DEPTH) ** 0.25
EPS = 1e-5
MOD_ROWS = 8
LANES = 128
ROUTER_ROWS = 32
EXPERT_ROW0 = 8

TILE = 256
CHUNK = 16
CAP = 2 * TILE + N_EXPERTS * CHUNK
CHUNKS_PER_TILE = CAP // CHUNK
MOE_TMB = 256
CPB = MOE_TMB // CHUNK
HS_W = D_MODEL + LANES
V_AUG_W = 2 * DIFF_W
QK_SCALE_LOG2 = (DIFF_QK ** -0.5) * math.log2(math.e)
SLOT_RADIX = 32

_O_RQ, _O_RK, _O_RV, _O_RG = 0, 256, 512, 768
_O_DQ, _O_DK, _O_DV = 1024, 1536, 2048
_O_CB, _O_CC, _O_CH = 2560, 2816, 3072
IN_WIDTH = 3328

BF = jnp.bfloat16
F32 = jnp.float32
I32 = jnp.int32

_VMEM_LIMIT = 56 * 1024 * 1024


def _dot(a, b):
    return jnp.dot(a, b, preferred_element_type=F32)


def _dot_nt(a, b):
    return lax.dot_general(a, b, (((1,), (1,)), ((), ())), preferred_element_type=F32)


def _dot_tn(a, b):
    return lax.dot_general(a, b, (((0,), (0,)), ((), ())), preferred_element_type=F32)


def _split(x):
    hi = x.astype(BF)
    lo = (x - hi.astype(F32)).astype(BF)
    return hi, lo


def _dot_hl(x, w_bf16):
    hi, lo = _split(x)
    return _dot(hi, w_bf16) + _dot(lo, w_bf16)


def _dot3(x, w_hi, w_lo):
    hi, lo = _split(x)
    return _dot(hi, w_hi) + (_dot(lo, w_hi) + _dot(hi, w_lo))


def _iota(shape, dim):
    return lax.broadcasted_iota(I32, shape, dim)


def _ones_where(cond):
    return jnp.where(cond, 1.0, 0.0).astype(BF)


def _log_sigmoid(x):
    return jnp.minimum(x, 0.0) - jnp.log1p(jnp.exp(-jnp.abs(x)))


def _silu(x):
    return x * jax.nn.sigmoid(x)


def _layer_norm(y, g, b):
    mu = jnp.mean(y, axis=-1, keepdims=True)
    yc = y - mu
    var = jnp.mean(yc * yc, axis=-1, keepdims=True)
    return yc * lax.rsqrt(var + EPS) * g + b


def _params(sem):
    return pltpu.CompilerParams(dimension_semantics=sem, vmem_limit_bytes=_VMEM_LIMIT)


_HBM = pl.BlockSpec(memory_space=pltpu.HBM)


def _mod_kernel(c_ref, w_ref, b_ref, o_ref):
    a = _silu(c_ref[...])
    w_hi, w_lo = _split(w_ref[...])
    o_ref[...] = _dot3(a, w_hi, w_lo) + b_ref[...]


def _modulation(cond, w_mod, b_mod):
    tn = 1536
    n6 = 6 * D_MODEL
    return pl.pallas_call(
        _mod_kernel,
        grid=(DEPTH, n6 // tn),
        in_specs=[
            pl.BlockSpec((MOD_ROWS, D_MODEL), lambda l, j: (0, 0)),
            pl.BlockSpec((None, D_MODEL, tn), lambda l, j: (l, 0, j)),
            pl.BlockSpec((None, 1, tn), lambda l, j: (l, 0, j)),
        ],
        out_specs=pl.BlockSpec((None, MOD_ROWS, tn), lambda l, j: (l, 0, j)),
        out_shape=jax.ShapeDtypeStruct((DEPTH, MOD_ROWS, n6), F32),
        compiler_params=_params(("parallel", "parallel")),
        name="modulation",
    )(cond, w_mod, b_mod.reshape(DEPTH, 1, n6))


def _mod_spec(layer, row_fn):
    return pl.BlockSpec((None, None, 1, 6 * D_MODEL), lambda s, i: (layer, row_fn(s), 0, 0))


def _augment_v(v):
    ones = jnp.ones((v.shape[0], DIFF_V), v.dtype)
    parts = []
    for hd in range(H_DIFF):
        parts += [v[:, hd * DIFF_V:(hd + 1) * DIFF_V], ones]
    return jnp.concatenate(parts, axis=1)


def _swap16(x):
    n = x.shape[-1]
    lane = _iota(x.shape, 1)
    return jnp.where((lane & 16) == 0, pltpu.roll(x, n - 16, axis=1), pltpu.roll(x, 16, axis=1))


def _proj_kernel(*refs, latent, n_aliased, layer):
    x_ref, mod_ref, w_ref, lg_ref = refs[:4]
    pos = 4
    if latent:
        cos_ref, sin_ref = refs[pos:pos + 2]
        pos += 2
    pos += n_aliased
    rq_ref, rk_ref, rv_ref, rg_ref, dq_ref, dk_ref, dv_ref, cb_ref, cch_ref = refs[pos:pos + 9]
    pos += 9
    if latent:
        kvf_ref, kvb_ref = refs[pos:pos + 2]
    else:
        dk32_ref, dv32_ref, stf_ref, stb_ref = refs[pos:pos + 4]

    def put(ref, idx, val):
        if n_aliased:
            ref[idx if idx else ...] = val
        else:
            for other in range(DEPTH):
                ref[(other,) + idx] = val if other == layer else jnp.zeros_like(val)

    m = mod_ref[...]
    sh1 = m[:, 0:D_MODEL]
    sc1 = m[:, D_MODEL:2 * D_MODEL]
    h = (x_ref[...] * (1.0 + sc1) + sh1).astype(BF)

    def col(off, width):
        return _dot(h, w_ref[:, off:off + width])

    rq_ref[...] = col(_O_RQ, RET_W).astype(BF)
    rk = col(_O_RK, RET_W) * (RET_DK ** -0.5)
    rk_ref[...] = rk.astype(BF)
    rv = col(_O_RV, RET_W).astype(BF)
    rv_ref[...] = rv
    rg_ref[...] = col(_O_RG, RET_W)

    lg = _log_sigmoid(lg_ref[...])
    p = _iota((TILE, 1), 0).astype(F32)
    kf = (rk * jnp.exp((TILE - 1.0 - p) * lg[0:1])).astype(BF)
    kb = (rk * jnp.exp(p * lg[1:2])).astype(BF)
    kvf = _dot_tn(kf, rv)
    kvb = _dot_tn(kb, rv)
    if latent:
        kvf_ref[...] = kvf
        kvb_ref[...] = kvb
    else:
        for hd in range(H_RET):
            lo = hd * RET_DK
            put(stf_ref, (hd,), kvf[lo:lo + RET_DK, lo:lo + RET_DK])
            put(stb_ref, (hd,), kvb[lo:lo + RET_DK, lo:lo + RET_DK])

    dq = col(_O_DQ, DIFF_W)
    dk = col(_O_DK, DIFF_W)
    dv = col(_O_DV, DIFF_W)
    if latent:
        cos = cos_ref[...]
        sin = sin_ref[...]
        dq = dq * cos + _swap16(dq) * sin
        dk = dk * cos + _swap16(dk) * sin
    else:
        put(dk32_ref, (), dk)
        put(dv32_ref, (), dv)
    dq_ref[...] = (dq * QK_SCALE_LOG2).astype(BF)
    dk_ref[...] = dk.astype(BF)
    dv_ref[...] = _augment_v(dv.astype(BF))

    cb_ref[...] = col(_O_CB, CONV_CH)
    cch_ref[...] = col(_O_CC, CONV_CH) * col(_O_CH, CONV_CH)


def _proj(x, mod, layer, row_fn, w_in_b, lg_lanes, rope_tabs=None, stacked=None):
    S, T, _ = x.shape
    n = T // TILE
    latent = rope_tabs is not None

    def tok(w):
        return pl.BlockSpec((None, TILE, w), lambda s, i: (s, i, 0))

    def tok_shape(w, dt):
        return jax.ShapeDtypeStruct((S, T, w), dt)

    in_specs = [
        tok(D_MODEL), _mod_spec(layer, row_fn),
        pl.BlockSpec((None, D_MODEL, IN_WIDTH), lambda s, i: (layer, 0, 0)),
        pl.BlockSpec((2, RET_W), lambda s, i: (0, 0)),
    ]
    args = [x, mod, w_in_b, lg_lanes]
    out_specs = [tok(RET_W)] * 4 + [tok(DIFF_W)] * 2 + [tok(V_AUG_W)] + [tok(CONV_CH)] * 2
    out_shape = ([tok_shape(RET_W, BF)] * 3 + [tok_shape(RET_W, F32)] + [tok_shape(DIFF_W, BF)] * 2
                 + [tok_shape(V_AUG_W, BF)]
                 + [tok_shape(CONV_CH, F32)] * 2)
    aliases = {}
    if latent:
        in_specs += [pl.BlockSpec((TILE, DIFF_W), lambda s, i: (i, 0))] * 2
        args += list(rope_tabs)
        out_specs += [pl.BlockSpec((None, None, RET_W, RET_W), lambda s, i: (s, i, 0, 0))] * 2
        out_shape += [jax.ShapeDtypeStruct((S, n, RET_W, RET_W), F32)] * 2
    else:
        assert n == 1
        kv_shape = jax.ShapeDtypeStruct((S, DEPTH, T, DIFF_W), F32)
        st_shape = jax.ShapeDtypeStruct((S, DEPTH, H_RET, RET_DK, RET_DK), F32)
        if stacked is not None:
            in_specs += [_HBM] * 4
            args += list(stacked)
            aliases = {len(args) - 4 + j: len(out_shape) + j for j in range(4)}
            out_specs += [pl.BlockSpec((None, None, TILE, DIFF_W), lambda s, i: (s, layer, i, 0))] * 2
            out_specs += [pl.BlockSpec((None, None, H_RET, RET_DK, RET_DK), lambda s, i: (s, layer, 0, 0, 0))] * 2
        else:
            out_specs += [pl.BlockSpec((None, DEPTH, TILE, DIFF_W), lambda s, i: (s, 0, i, 0))] * 2
            out_specs += [pl.BlockSpec((None, DEPTH, H_RET, RET_DK, RET_DK), lambda s, i: (s, 0, 0, 0, 0))] * 2
        out_shape += [kv_shape, kv_shape, st_shape, st_shape]
    return pl.pallas_call(
        functools.partial(_proj_kernel, latent=latent, n_aliased=len(aliases), layer=layer),
        grid=(S, n),
        in_specs=in_specs,
        out_specs=out_specs,
        out_shape=out_shape,
        input_output_aliases=aliases,
        compiler_params=_params(("parallel", "parallel")),
        name="proj_lat" if latent else "proj_ctx",
    )(*args)


def _scan_kernel(kvf_ref, kvb_ref, s0f_ref, s0b_ref, lg_ref, sf_ref, sb_ref, *, n):
    lg = _log_sigmoid(lg_ref[...])
    dec = jnp.exp(float(TILE) * lg)
    same_head = (_iota((RET_W, RET_W), 0) >> 6) == (_iota((RET_W, RET_W), 1) >> 6)

    s = jnp.where(same_head, s0f_ref[...], 0.0)
    for c in range(n):
        sf_ref[c] = s
        s = s * dec[0:1] + jnp.where(same_head, kvf_ref[c], 0.0)
    s = jnp.where(same_head, s0b_ref[...], 0.0)
    for c in reversed(range(n)):
        sb_ref[c] = s
        s = s * dec[1:2] + jnp.where(same_head, kvb_ref[c], 0.0)


def _scan(kvf, kvb, s0f, s0b, lg_lanes):
    S, n = kvf.shape[:2]
    chunks = pl.BlockSpec((None, n, RET_W, RET_W), lambda s: (s, 0, 0, 0))
    one = pl.BlockSpec((None, RET_W, RET_W), lambda s: (s, 0, 0))
    return pl.pallas_call(
        functools.partial(_scan_kernel, n=n),
        grid=(S,),
        in_specs=[chunks, chunks, one, one, pl.BlockSpec((2, RET_W), lambda s: (0, 0))],
        out_specs=[chunks, chunks],
        out_shape=[jax.ShapeDtypeStruct((S, n, RET_W, RET_W), F32)] * 2,
        compiler_params=_params(("parallel",)),
        name="ret_scan",
    )(kvf, kvb, s0f, s0b, lg_lanes)


def _diff_heads(dq_ref, kv_refs, heads, lam):
    lane = _iota((1, DIFF_V), 1)
    scores = []
    for hd in heads:
        lo = hd * DIFF_V
        q_h = dq_ref[:, lo:lo + DIFF_V]
        qs = jnp.concatenate([jnp.where(lane < DIFF_QK, q_h, 0), jnp.where(lane >= DIFF_QK, q_h, 0)], axis=0)
        scores.append(jnp.concatenate([_dot_nt(qs, k[:, lo:lo + DIFF_V]) for k, _ in kv_refs], axis=1))
    s = jnp.concatenate(scores, axis=0)
    p = jnp.exp2(s - jnp.max(s, axis=-1, keepdims=True)).astype(BF)
    outs = []
    for n, hd in enumerate(heads):
        acc = None
        start = 0
        for k, v in kv_refs:
            part = _dot(p[n * 2 * TILE:(n + 1) * 2 * TILE, start:start + k.shape[0]],
                        v[:, hd * 2 * DIFF_V:(hd + 1) * 2 * DIFF_V])
            acc = part if acc is None else acc + part
            start += k.shape[0]
        o0 = acc[0:TILE, 0:DIFF_V] * (1.0 / acc[0:TILE, DIFF_V:DIFF_V + 1])
        o1 = acc[TILE:, 0:DIFF_V] * (lam / acc[TILE:, DIFF_V:DIFF_V + 1])
        outs.append(o0 - o1)
    return outs


def _mixer_kernel(*refs, n_tiles, lam_init, cached):
    (x_ref, mod_ref, rq_ref, rk_ref, rv_ref, rg_ref, sf_ref, sb_ref, dq_ref) = refs[:9]
    pos = 9
    kv_refs = []
    if cached:
        kv_refs.append((refs[pos], refs[pos + 1]))
        pos += 2
    kv_refs.append((refs[pos], refs[pos + 1]))
    pos += 2
    (cb_ref, cch_ref, cprev_ref, cnext_ref, wout_ref, lg_ref, lamp_ref, subg_ref, cw_ref, cbias_ref,
     lng_ref, lnb_ref, wrh_ref, wrl_ref, br_ref) = refs[pos:pos + 15]
    pos += 15
    x1_ref, hs_ref, slots_ref, cnt_ref = refs[pos:pos + 4]

    i = pl.program_id(1)
    m = mod_ref[...]
    g1 = m[:, 2 * D_MODEL:3 * D_MODEL]
    sh2 = m[:, 3 * D_MODEL:4 * D_MODEL]
    sc2 = m[:, 4 * D_MODEL:5 * D_MODEL]

    lg = _log_sigmoid(lg_ref[...])
    head_of_lane = _iota((1, RET_W), 1) >> 6
    q = rq_ref[...]
    k = rk_ref[...]
    v = rv_ref[...]
    dist = (_iota((TILE, TILE), 0) - _iota((TILE, TILE), 1)).astype(F32)
    adist = jnp.abs(dist)
    diag2 = jnp.where(dist == 0.0, 2.0, 1.0)
    ret_o = jnp.zeros((TILE, RET_W), F32)
    for hd in range(H_RET):
        in_head = head_of_lane == hd
        sc = _dot_nt(jnp.where(in_head, q, 0), k)
        lgf = lg[0:1, hd * RET_DK:hd * RET_DK + 1]
        lgb = lg[1:2, hd * RET_DK:hd * RET_DK + 1]
        decay = jnp.exp(adist * jnp.where(dist > 0.0, lgf, lgb)) * diag2
        ret_o = ret_o + jnp.where(in_head, _dot((sc * decay).astype(BF), v), 0.0)
    p = _iota((TILE, 1), 0).astype(F32)
    ret_o = ret_o + _dot(q, sf_ref[...].astype(BF)) * jnp.exp((p + 1.0) * lg[0:1])
    ret_o = ret_o + _dot(q, sb_ref[...].astype(BF)) * jnp.exp((float(TILE) - p) * lg[1:2])
    avg = jnp.where((_iota((RET_W, RET_W), 0) >> 6) == (_iota((RET_W, RET_W), 1) >> 6),
                    1.0 / RET_DK, 0.0).astype(BF)
    rc = ret_o - _dot_hl(ret_o, avg)
    ret = rc * lax.rsqrt(_dot_hl(rc * rc, avg) + EPS) * _silu(rg_ref[...])

    lp = lamp_ref[...]
    lam = (jnp.exp(jnp.sum(lp[0:1] * lp[1:2], axis=-1, keepdims=True))
           - jnp.exp(jnp.sum(lp[2:3] * lp[3:4], axis=-1, keepdims=True)) + lam_init)
    subg = subg_ref[...] * (1.0 - lam_init)
    head_groups = [[hd] for hd in range(H_DIFF)] if cached else [list(range(H_DIFF))]
    heads = []
    for group in head_groups:
        for o in _diff_heads(dq_ref, kv_refs, group, lam):
            o = o * lax.rsqrt(jnp.mean(o * o, axis=-1, keepdims=True) + EPS) * subg
            heads.append(o.astype(BF))
    diff = jnp.concatenate(heads, axis=1)

    cch = cch_ref[...]
    prev = jnp.where(i > 0, cprev_ref[7:8, :], 0.0)
    nxt = jnp.where(i < n_tiles - 1, cnext_ref[0:1, :], 0.0)
    r = _iota((TILE, 1), 0)
    up = jnp.where(r == 0, prev, pltpu.roll(cch, 1, axis=0))
    dn = jnp.where(r == TILE - 1, nxt, pltpu.roll(cch, TILE - 1, axis=0))
    cw = cw_ref[...]
    conv = cb_ref[...] * (up * cw[0:1] + cch * cw[1:2] + dn * cw[2:3] + cbias_ref[...])

    mix = (_dot(ret.astype(BF), wout_ref[0:RET_W, :])
           + _dot(diff, wout_ref[RET_W:RET_W + DIFF_W, :])
           + _dot(conv.astype(BF), wout_ref[RET_W + DIFF_W:, :]))
    x1 = _layer_norm(ALPHA * x_ref[...] + g1 * mix, lng_ref[...], lnb_ref[...])
    x1_ref[...] = x1

    h2 = x1 * (1.0 + sc2) + sh2
    h2h, h2l = _split(h2)
    wrh = wrh_ref[...]
    logits = (_dot_nt(wrh, h2h) + (_dot_nt(wrh, h2l) + _dot_nt(wrl_ref[...], h2h))) + br_ref[...]
    neg = -jnp.inf
    g_id = _iota((EXPERT_ROW0, TILE), 0)
    gl = jnp.where(g_id < N_GROUPS, logits[0:EXPERT_ROW0], neg)
    gmax = jnp.max(gl, axis=0, keepdims=True)
    g_idx = jnp.min(jnp.where(gl == gmax, g_id, N_GROUPS), axis=0, keepdims=True)
    g_w = 1.0 / jnp.sum(jnp.exp(gl - gmax), axis=0, keepdims=True)
    e_id = _iota((N_EXPERTS, TILE), 0)
    el = jnp.where((e_id >> 2) == g_idx, logits[EXPERT_ROW0:EXPERT_ROW0 + N_EXPERTS], neg)
    v1 = jnp.max(el, axis=0, keepdims=True)
    i1 = jnp.min(jnp.where(el == v1, e_id, N_EXPERTS), axis=0, keepdims=True)
    el2 = jnp.where(e_id == i1, neg, el)
    v2 = jnp.max(el2, axis=0, keepdims=True)
    i2 = jnp.min(jnp.where(el2 == v2, e_id, N_EXPERTS), axis=0, keepdims=True)
    t = jnp.exp(v2 - v1)
    w1 = g_w / (1.0 + t)
    w2 = g_w * t / (1.0 + t)

    sel1 = e_id == i1
    sel2 = e_id == i2
    sel = jnp.where(sel1 | sel2, 1.0, 0.0)
    earlier = _ones_where(_iota((TILE, TILE), 0) < _iota((TILE, TILE), 1))
    rank = _dot(sel.astype(BF), earlier)
    cnt = jnp.sum(sel, axis=1, keepdims=True).astype(I32)
    padded = ((cnt + (CHUNK - 1)) >> 4) << 4
    incl = jnp.broadcast_to(padded, (N_EXPERTS, LANES))
    e_row = _iota((N_EXPERTS, LANES), 0)
    for step in (1, 2, 4, 8):
        incl = incl + jnp.where(e_row >= step, pltpu.roll(incl, step, axis=0), 0)
    seg_off = (incl[:, 0:1] - padded).astype(F32)
    spos = seg_off + rank
    slot1 = jnp.sum(jnp.where(sel1, spos, 0.0), axis=0, keepdims=True).astype(I32)
    slot2 = jnp.sum(jnp.where(sel2, spos, 0.0), axis=0, keepdims=True).astype(I32)
    s_id = _iota((CAP, TILE), 0)
    at1 = s_id == slot1
    at2 = s_id == slot2
    hs_ref[:, 0:D_MODEL] = _dot(_ones_where(at1 | at2), h2h).astype(BF)
    w1h = w1.astype(BF).astype(F32)
    w2h = w2.astype(BF).astype(F32)
    ones = jnp.ones((TILE, LANES), BF)
    g_hi = _dot((jnp.where(at1, w1h, 0.0) + jnp.where(at2, w2h, 0.0)).astype(BF), ones)
    g_lo = _dot((jnp.where(at1, w1 - w1h, 0.0) + jnp.where(at2, w2 - w2h, 0.0)).astype(BF), ones)
    lane = _iota((CAP, LANES), 1)
    hs_ref[:, D_MODEL:] = jnp.where(lane == 0, g_hi, jnp.where(lane == 1, g_lo, 0.0)).astype(BF)
    cnt_ref[...] = jnp.broadcast_to(padded, (N_EXPERTS, LANES))

    dr = _iota((LANES, TILE), 0)
    digits = jnp.where(dr == 0, slot1 & (SLOT_RADIX - 1), jnp.where(dr == 1, slot1 >> 5,
             jnp.where(dr == 2, slot2 & (SLOT_RADIX - 1), jnp.where(dr == 3, slot2 >> 5, 0))))
    eye = _ones_where(_iota((TILE, TILE), 0) == _iota((TILE, TILE), 1))
    cols = _dot_nt(eye, digits.astype(F32).astype(BF))
    s1c = (cols[:, 0:1] + SLOT_RADIX * cols[:, 1:2]).astype(I32)
    s2c = (cols[:, 2:3] + SLOT_RADIX * cols[:, 3:4]).astype(I32)
    tl = _iota((TILE, LANES), 1)
    slots_ref[...] = jnp.where(tl == 0, s1c, jnp.where(tl == 1, s2c, 0))


def _mixer(x, mod, layer, row_fn, pr, states, cache_kv, new_kv, w_out_b, lg_lanes, lamp, subg, cw, cbias,
           lng, lnb, wrh, wrl, br, lam_init):
    S, T, _ = x.shape
    n = T // TILE
    rq, rk, rv, rg, dq, cb, cch = pr
    t8 = TILE // 8
    cached = cache_kv is not None

    def tok(w):
        return pl.BlockSpec((None, TILE, w), lambda s, i: (s, i, 0))

    def full(shape):
        return pl.BlockSpec(shape, lambda s, i: (0,) * len(shape))

    def seq(a):
        mode = pl.Buffered(1) if n > 1 else None
        return pl.BlockSpec((None,) + a.shape[1:], lambda s, i: (s, 0, 0), pipeline_mode=mode)

    if states is None:
        zero_state = jnp.zeros((1, 1, RET_W, RET_W), F32)
        states = (zero_state, zero_state)
        state = pl.BlockSpec((None, None, RET_W, RET_W), lambda s, i: (0, 0, 0, 0))
    else:
        state = pl.BlockSpec((None, None, RET_W, RET_W), lambda s, i: (s, i, 0, 0))
    halo_prev = pl.BlockSpec((None, 8, CONV_CH), lambda s, i: (s, jnp.maximum(i * t8 - 1, 0), 0))
    halo_next = pl.BlockSpec((None, 8, CONV_CH), lambda s, i: (s, jnp.minimum((i + 1) * t8, T // 8 - 1), 0))
    in_specs = [tok(D_MODEL), _mod_spec(layer, row_fn), tok(RET_W), tok(RET_W), tok(RET_W), tok(RET_W),
                state, state, tok(DIFF_W)]
    args = [x, mod, rq, rk, rv, rg, states[0], states[1], dq]
    if cached:
        in_specs += [seq(cache_kv[0]), seq(cache_kv[1])]
        args += list(cache_kv)
    in_specs += [seq(new_kv[0]), seq(new_kv[1])]
    args += list(new_kv)
    in_specs += [
        tok(CONV_CH), tok(CONV_CH), halo_prev, halo_next,
        pl.BlockSpec((None, D_MODEL, D_MODEL), lambda s, i: (layer, 0, 0)),
        full((2, RET_W)), full((4, DIFF_QK)), full((1, DIFF_V)),
        full((3, CONV_CH)), full((1, CONV_CH)), full((1, D_MODEL)), full((1, D_MODEL)),
        full((ROUTER_ROWS, D_MODEL)), full((ROUTER_ROWS, D_MODEL)), full((ROUTER_ROWS, 1)),
    ]
    args += [cb, cch, cch, cch, w_out_b, lg_lanes, lamp, subg, cw, cbias, lng, lnb, wrh, wrl, br]
    return pl.pallas_call(
        functools.partial(_mixer_kernel, n_tiles=n, lam_init=lam_init, cached=cached),
        grid=(S, n),
        in_specs=in_specs,
        out_specs=[tok(D_MODEL),
                   pl.BlockSpec((CAP, HS_W), lambda s, i: (s * n + i, 0)),
                   tok(LANES),
                   pl.BlockSpec((None, None, N_EXPERTS, LANES), lambda s, i: (s, i, 0, 0))],
        out_shape=[jax.ShapeDtypeStruct((S, T, D_MODEL), F32),
                   jax.ShapeDtypeStruct((S * n * CAP, HS_W), BF),
                   jax.ShapeDtypeStruct((S, T, LANES), I32),
                   jax.ShapeDtypeStruct((S, n, N_EXPERTS, LANES), I32)],
        compiler_params=_params(("parallel", "parallel")),
        name="mixer_lat" if cached else "mixer_ctx",
    )(*args)


def _block_schedule(padded):
    NT = padded.shape[0]
    n_chunks = NT * CHUNKS_PER_TILE
    n_blocks = n_chunks // CPB + N_EXPERTS
    c16 = padded // CHUNK
    ends = jnp.cumsum(c16, axis=1)
    before = jnp.cumsum(c16, axis=0) - c16
    per_expert = jnp.sum(c16, axis=0)
    nb = (per_expert + (CPB - 1)) // CPB
    b_end = jnp.cumsum(nb)
    q = jnp.arange(CHUNKS_PER_TILE, dtype=I32)
    e = jnp.arange(N_EXPERTS, dtype=I32)
    key = jnp.sum((ends[:, None, :] <= q[None, :, None]).astype(I32), axis=-1)
    pos_e = ((b_end - nb) * CPB + before)[:, None, :] + (q[None, :, None] - (ends - c16)[:, None, :])
    pos = jnp.sum(jnp.where(key[..., None] == e, pos_e, 0), axis=-1)
    pos = jnp.where(key < N_EXPERTS, pos, -1).reshape(-1)
    match = pos[None, :] == jnp.arange(n_blocks * CPB, dtype=I32)[:, None]
    cid = jnp.sum(jnp.where(match, jnp.arange(n_chunks, dtype=I32)[None, :], 0), axis=-1)
    nv = jnp.sum(match.astype(I32).reshape(n_blocks, -1), axis=-1)
    b = jnp.arange(n_blocks, dtype=I32)
    eb = jnp.minimum(jnp.sum((b_end[None, :] <= b[:, None]).astype(I32), axis=-1), N_EXPERTS - 1)
    return eb.astype(I32), nv.astype(I32), cid.astype(I32)


def _moe_kernel(eb_ref, nv_ref, cid_ref, hs_a, hs_b, wg_ref, wu_ref, wd_ref, ys_a, ys_b,
                xbuf, ybuf, wg_b, wu_b, wd_b, in_sem, out_sem, *, n_blocks, chunks_a):
    b = pl.program_id(0)

    def chunk_rows(c):
        return pl.ds(pl.multiple_of(c * CHUNK, CHUNK), CHUNK)

    def gather_copy(which, c, j, slot):
        src = (hs_a, hs_b)[which]
        return pltpu.make_async_copy(src.at[chunk_rows(c)], xbuf.at[slot, chunk_rows(j)], in_sem.at[slot])

    def scatter_copy(which, c, j, slot):
        dst = (ys_a, ys_b)[which]
        return pltpu.make_async_copy(ybuf.at[slot, chunk_rows(j)],
                                     dst.at[chunk_rows(c), pl.ds(0, D_MODEL)], out_sem.at[slot])

    def for_chunks(blk, fn):
        def body(j, carry):
            c = cid_ref[blk * CPB + j]

            @pl.when(c < chunks_a)
            def _():
                fn(j, 0, c)

            @pl.when(c >= chunks_a)
            def _():
                fn(j, 1, c - chunks_a)
            return carry
        lax.fori_loop(0, nv_ref[blk], body, 0)

    slot = lax.rem(b, 2)
    other = 1 - slot

    @pl.when(b == 0)
    def _():
        xbuf[...] = jnp.zeros_like(xbuf)
        for_chunks(0, lambda j, w, c: gather_copy(w, c, j, 0).start())

    @pl.when(b + 1 < n_blocks)
    def _():
        for_chunks(b + 1, lambda j, w, c: gather_copy(w, c, j, other).start())

    @pl.when((b == 0) | (eb_ref[b] != eb_ref[jnp.maximum(b - 1, 0)]))
    def _():
        wg_b[...] = wg_ref[...].astype(BF)
        wu_b[...] = wu_ref[...].astype(BF)
        wd_b[...] = wd_ref[...].astype(BF)

    for_chunks(b, lambda j, w, c: gather_copy(w, c, j, slot).wait())

    @pl.when(b >= 2)
    def _():
        for_chunks(b - 2, lambda j, w, c: scatter_copy(w, c, j, slot).wait())

    @pl.when(nv_ref[b] > 0)
    def _():
        x = xbuf[slot, :, 0:D_MODEL]
        gate = (xbuf[slot, :, D_MODEL:D_MODEL + 1].astype(F32)
                + xbuf[slot, :, D_MODEL + 1:D_MODEL + 2].astype(F32))
        hg = _dot(x, wg_b[...])
        hu = _dot(x, wu_b[...])
        act = _silu(hg) * hu * gate
        ybuf[slot] = _dot(act.astype(BF), wd_b[...]).astype(BF)

    for_chunks(b, lambda j, w, c: scatter_copy(w, c, j, slot).start())

    @pl.when(b == n_blocks - 1)
    def _():
        if n_blocks >= 2:
            for_chunks(b - 1, lambda j, w, c: scatter_copy(w, c, j, other).wait())
        for_chunks(b, lambda j, w, c: scatter_copy(w, c, j, slot).wait())


def _moe(hs_a, hs_b, eb, nv, cid, layer, w_gate, w_up, w_down):
    n_blocks = eb.shape[0]

    def w_spec(a, c):
        return pl.BlockSpec((None, None, a, c), lambda b, eb, nv, cid: (layer, eb[b], 0, 0))

    grid_spec = pltpu.PrefetchScalarGridSpec(
        num_scalar_prefetch=3,
        grid=(n_blocks,),
        in_specs=[_HBM, _HBM, w_spec(D_MODEL, D_EXPERT), w_spec(D_MODEL, D_EXPERT), w_spec(D_EXPERT, D_MODEL)],
        out_specs=[_HBM, _HBM],
        scratch_shapes=[pltpu.VMEM((2, MOE_TMB, HS_W), BF), pltpu.VMEM((2, MOE_TMB, D_MODEL), BF),
                        pltpu.VMEM((D_MODEL, D_EXPERT), BF), pltpu.VMEM((D_MODEL, D_EXPERT), BF),
                        pltpu.VMEM((D_EXPERT, D_MODEL), BF),
                        pltpu.SemaphoreType.DMA((2,)), pltpu.SemaphoreType.DMA((2,))],
    )
    return pl.pallas_call(
        functools.partial(_moe_kernel, n_blocks=n_blocks, chunks_a=hs_a.shape[0] // CHUNK),
        grid_spec=grid_spec,
        out_shape=[jax.ShapeDtypeStruct(hs_a.shape, BF), jax.ShapeDtypeStruct(hs_b.shape, BF)],
        input_output_aliases={3: 0, 4: 1},
        compiler_params=_params(("arbitrary",)),
        name="moe_sorted",
    )(eb, nv, cid, hs_a, hs_b, w_gate, w_up, w_down)


def _combine_kernel(ys_ref, slots_ref, x1_ref, mod_ref, lng_ref, lnb_ref, o_ref):
    sl = slots_ref[...]
    s_lane = _iota((TILE, CAP), 1)
    pick = _ones_where((s_lane == sl[:, 0:1]) | (s_lane == sl[:, 1:2]))
    ffn = _dot(pick, ys_ref[:, 0:D_MODEL])
    g2 = mod_ref[:, 5 * D_MODEL:6 * D_MODEL]
    o_ref[...] = _layer_norm(ALPHA * x1_ref[...] + g2 * ffn, lng_ref[...], lnb_ref[...])


def _combine(ys, slots, x1, mod, layer, row_fn, lng, lnb):
    S, T, _ = x1.shape
    n = T // TILE

    def tok(w):
        return pl.BlockSpec((None, TILE, w), lambda s, i: (s, i, 0))

    return pl.pallas_call(
        _combine_kernel,
        grid=(S, n),
        in_specs=[
            pl.BlockSpec((CAP, HS_W), lambda s, i: (s * n + i, 0)),
            tok(LANES), tok(D_MODEL), _mod_spec(layer, row_fn),
            pl.BlockSpec((1, D_MODEL), lambda s, i: (0, 0)),
            pl.BlockSpec((1, D_MODEL), lambda s, i: (0, 0)),
        ],
        out_specs=tok(D_MODEL),
        out_shape=jax.ShapeDtypeStruct((S, T, D_MODEL), F32),
        compiler_params=_params(("parallel", "parallel")),
        name="moe_combine",
    )(ys, slots, x1, mod, lng, lnb)


def _rope_tables(n_lat):
    half = DIFF_QK // 2
    pairs = half // 2
    inv = (1.0 / (ROPE_BASE ** (np.arange(pairs, dtype=np.float32) * 2.0 / half))).astype(np.float32)
    t = np.arange(n_lat)
    ang_r = ((t // GRID_W).astype(np.float32)[:, None] * inv[None, :]).astype(np.float64)
    ang_c = ((t % GRID_W).astype(np.float32)[:, None] * inv[None, :]).astype(np.float64)
    cos = np.concatenate([np.cos(ang_r)] * 2 + [np.cos(ang_c)] * 2, axis=1)
    sin = np.concatenate([-np.sin(ang_r), np.sin(ang_r), -np.sin(ang_c), np.sin(ang_c)], axis=1)
    reps = DIFF_W // DIFF_QK
    return (jnp.asarray(np.tile(cos, (1, reps)), F32), jnp.asarray(np.tile(sin, (1, reps)), F32))


def _block_diag(s):
    S = s.shape[0]
    eye = jnp.eye(H_RET, dtype=s.dtype)
    return jnp.einsum('shdv,hg->shdgv', s, eye).reshape(S, RET_W, RET_W)


def _router_rows(w_group, b_group, w_expert, b_expert):
    pad = EXPERT_ROW0 - N_GROUPS
    tail = ROUTER_ROWS - EXPERT_ROW0 - N_EXPERTS
    w = jnp.concatenate([w_group.T, jnp.zeros((pad, D_MODEL), F32),
                         w_expert.reshape(D_MODEL, N_EXPERTS).T, jnp.zeros((tail, D_MODEL), F32)], axis=0)
    bias = jnp.concatenate([b_group, jnp.zeros((pad,), F32), b_expert.reshape(N_EXPERTS), jnp.zeros((tail,), F32)])
    hi = w.astype(BF)
    return hi, (w - hi.astype(F32)).astype(BF), bias.reshape(ROUTER_ROWS, 1)


def kernel(x_prompt, x_sample, cache_diff_k, cache_diff_v, state_ret_fwd, state_ret_bwd, c, c_ctx, w_mod, b_mod, w_in, ret_decay_logit, diff_lambda, diff_subln_g, conv_w, conv_b, w_out, ln_g, ln_b, w_router_group, b_router_group, w_router_expert, b_router_expert, w_gate, w_up, w_down):
    B, T_ctx, _ = x_prompt.shape
    Bd, T_lat, _ = x_sample.shape
    assert T_ctx == TILE and T_lat % TILE == 0 and T_lat % GRID_W == 0
    assert 1 + Bd <= MOD_ROWS

    cond = jnp.concatenate([c_ctx[None, :], c, jnp.zeros((MOD_ROWS - 1 - Bd, D_MODEL), F32)], axis=0)
    mod = _modulation(cond, w_mod, b_mod).reshape(DEPTH, MOD_ROWS, 1, 6 * D_MODEL)
    rope_tabs = _rope_tables(T_lat)
    w_in_b = w_in.astype(BF)
    w_out_b = w_out.astype(BF)

    ctx_row = lambda s: 0
    lat_row = lambda s: s + 1
    ctx_tiles = B * (T_ctx // TILE)
    lat_tiles = Bd * (T_lat // TILE)

    yp, ys = x_prompt, x_sample
    stacked = None
    for l in range(DEPTH):
        lam_init = 0.8 - 0.6 * math.exp(-0.3 * l)
        lg_lanes = jnp.repeat(ret_decay_logit[l], RET_DK, axis=1)
        wrh, wrl, br = _router_rows(w_router_group[l], b_router_group[l], w_router_expert[l], b_router_expert[l])
        shared = (w_out_b, lg_lanes, diff_lambda[l], diff_subln_g[l].reshape(1, DIFF_V), conv_w[l],
                  conv_b[l].reshape(1, CONV_CH), ln_g[l, 0].reshape(1, D_MODEL), ln_b[l, 0].reshape(1, D_MODEL),
                  wrh, wrl, br, lam_init)
        ln2 = (ln_g[l, 1].reshape(1, D_MODEL), ln_b[l, 1].reshape(1, D_MODEL))

        pr = _proj(yp, mod, l, ctx_row, w_in_b, lg_lanes, stacked=stacked)
        rq, rk, rv, rg, dq, dk, dv, cb, cch = pr[:9]
        stacked = pr[9:]
        x1_c, hs_c, slots_c, cnt_c = _mixer(yp, mod, l, ctx_row, (rq, rk, rv, rg, dq, cb, cch), None, None,
                                            (dk, dv), *shared)

        pr = _proj(ys, mod, l, lat_row, w_in_b, lg_lanes, rope_tabs=rope_tabs)
        rq, rk, rv, rg, dq, dk, dv, cb, cch, kvf, kvb = pr
        states = _scan(kvf, kvb, _block_diag(state_ret_fwd[:, l]), _block_diag(state_ret_bwd[:, l]), lg_lanes)
        cache_v = cache_diff_v[:, l].astype(BF)
        cache_kv = (cache_diff_k[:, l].reshape(Bd, -1, DIFF_W).astype(BF),
                    jnp.concatenate([cache_v, jnp.ones_like(cache_v)], axis=-1).reshape(Bd, -1, V_AUG_W))
        x1_l, hs_l, slots_l, cnt_l = _mixer(ys, mod, l, lat_row, (rq, rk, rv, rg, dq, cb, cch), states, cache_kv,
                                            (dk, dv), *shared)

        padded = jnp.concatenate([cnt_c.reshape(ctx_tiles, N_EXPERTS, LANES)[:, :, 0],
                                  cnt_l.reshape(lat_tiles, N_EXPERTS, LANES)[:, :, 0]], axis=0)
        eb, nv, cid = _block_schedule(padded)
        out_c, out_l = _moe(hs_c, hs_l, eb, nv, cid, l, w_gate, w_up, w_down)
        yp = _combine(out_c, slots_c, x1_c, mod, l, ctx_row, *ln2)
        ys = _combine(out_l, slots_l, x1_l, mod, l, lat_row, *ln2)

    new_k, new_v, st_f, st_b = stacked
    return (yp, ys, new_k.reshape(B, DEPTH, T_ctx, H_DIFF, 2, DIFF_QK),
            new_v.reshape(B, DEPTH, T_ctx, H_DIFF, DIFF_V), st_f, st_b)
```

```python
import functools
import math

import numpy as np
import jax
import jax.numpy as jnp
from jax import lax
from jax.experimental import pallas as pl
from jax.experimental.pallas import tpu as pltpu

D_MODEL = 1024
DEPTH = 2
GRID_W = 64
H_RET = 4
RET_DK = 64
RET_W = H_RET * RET_DK
H_DIFF = 4
DIFF_QK = 64
DIFF_V = 2 * DIFF_QK
DIFF_W = H_DIFF * DIFF_V
CONV_CH = 256
ROPE_BASE = 10000.0
N_GROUPS = 4
EXPERTS_PER_GROUP = 4
N_EXPERTS = N_GROUPS * EXPERTS_PER_GROUP
D_EXPERT = 512
ALPHA = (2 * DEPTH) ** 0.25
EPS = 1e-5
MOD_ROWS = 8
LANES = 128
ROUTER_ROWS = 32
EXPERT_ROW0 = 8

TILE = 256
CHUNK = 16
CAP = 2 * TILE + N_EXPERTS * CHUNK
CHUNKS_PER_TILE = CAP // CHUNK
CTX_SEQS_PER_STEP = 2
COMBINE_TILES_PER_STEP = 2
MOE_TMB = 256
CPB = MOE_TMB // CHUNK
HS_W = D_MODEL + LANES
V_AUG_W = 2 * DIFF_W
QK_SCALE_LOG2 = (DIFF_QK ** -0.5) * math.log2(math.e)
SLOT_RADIX = 32

_O_RQ, _O_RK, _O_RV, _O_RG = 0, 256, 512, 768
_O_DQ, _O_DK, _O_DV = 1024, 1536, 2048
_O_CB, _O_CC, _O_CH = 2560, 2816, 3072
IN_WIDTH = 3328

BF = jnp.bfloat16
F32 = jnp.float32
I32 = jnp.int32

_VMEM_LIMIT = 56 * 1024 * 1024


def _dot(a, b):
    return jnp.dot(a, b, preferred_element_type=F32)


def _dot_nt(a, b):
    return lax.dot_general(a, b, (((1,), (1,)), ((), ())), preferred_element_type=F32)


def _dot_tn(a, b):
    return lax.dot_general(a, b, (((0,), (0,)), ((), ())), preferred_element_type=F32)


def _split(x):
    hi = x.astype(BF)
    lo = (x - hi.astype(F32)).astype(BF)
    return hi, lo


def _dot_hl(x, w_bf16):
    hi, lo = _split(x)
    return _dot(hi, w_bf16) + _dot(lo, w_bf16)


def _dot3(x, w_hi, w_lo):
    hi, lo = _split(x)
    return _dot(hi, w_hi) + (_dot(lo, w_hi) + _dot(hi, w_lo))


def _iota(shape, dim):
    return lax.broadcasted_iota(I32, shape, dim)


def _ones_where(cond):
    return jnp.where(cond, 1.0, 0.0).astype(BF)


def _log_sigmoid(x):
    return jnp.minimum(x, 0.0) - jnp.log1p(jnp.exp(-jnp.abs(x)))


def _silu(x):
    return x * jax.nn.sigmoid(x)


def _layer_norm(y, g, b):
    mu = jnp.mean(y, axis=-1, keepdims=True)
    yc = y - mu
    var = jnp.mean(yc * yc, axis=-1, keepdims=True)
    return yc * lax.rsqrt(var + EPS) * g + b


def _params(sem):
    return pltpu.CompilerParams(dimension_semantics=sem, vmem_limit_bytes=_VMEM_LIMIT)


_HBM = pl.BlockSpec(memory_space=pltpu.HBM)


def _mod_kernel(c_ref, w_ref, b_ref, o_ref):
    a = _silu(c_ref[...])
    w_hi, w_lo = _split(w_ref[...])
    o_ref[...] = _dot3(a, w_hi, w_lo) + b_ref[...]


def _modulation(cond, w_mod, b_mod):
    tn = 1536
    n6 = 6 * D_MODEL
    return pl.pallas_call(
        _mod_kernel,
        grid=(DEPTH, n6 // tn),
        in_specs=[
            pl.BlockSpec((MOD_ROWS, D_MODEL), lambda l, j: (0, 0)),
            pl.BlockSpec((None, D_MODEL, tn), lambda l, j: (l, 0, j)),
            pl.BlockSpec((None, 1, tn), lambda l, j: (l, 0, j)),
        ],
        out_specs=pl.BlockSpec((None, MOD_ROWS, tn), lambda l, j: (l, 0, j)),
        out_shape=jax.ShapeDtypeStruct((DEPTH, MOD_ROWS, n6), F32),
        compiler_params=_params(("parallel", "parallel")),
        name="modulation",
    )(cond, w_mod, b_mod.reshape(DEPTH, 1, n6))


def _mod_spec(layer, row_fn):
    return pl.BlockSpec((None, None, 1, 6 * D_MODEL), lambda s, i: (layer, row_fn(s), 0, 0))


def _augment_v(v):
    ones = jnp.ones((v.shape[0], DIFF_V), v.dtype)
    parts = []
    for hd in range(H_DIFF):
        parts += [v[:, hd * DIFF_V:(hd + 1) * DIFF_V], ones]
    return jnp.concatenate(parts, axis=1)


def _swap16(x):
    n = x.shape[-1]
    lane = _iota(x.shape, 1)
    return jnp.where((lane & 16) == 0, pltpu.roll(x, n - 16, axis=1), pltpu.roll(x, 16, axis=1))


def _proj_kernel(*refs, latent, n_aliased, layer):
    x_ref, mod_ref, w_ref, lg_ref = refs[:4]
    pos = 4
    if latent:
        cos_ref, sin_ref = refs[pos:pos + 2]
        pos += 2
    pos += n_aliased
    rq_ref, rk_ref, rv_ref, rg_ref, dq_ref, dk_ref, dv_ref, cb_ref, cch_ref = refs[pos:pos + 9]
    pos += 9
    if latent:
        kvf_ref, kvb_ref = refs[pos:pos + 2]
    else:
        dk32_ref, dv32_ref, stf_ref, stb_ref = refs[pos:pos + 4]

    def put(ref, idx, val):
        if n_aliased:
            ref[idx if idx else ...] = val
        else:
            for other in range(DEPTH):
                ref[(other,) + idx] = val if other == layer else jnp.zeros_like(val)

    m = mod_ref[...]
    sh1 = m[:, 0:D_MODEL]
    sc1 = m[:, D_MODEL:2 * D_MODEL]
    h = (x_ref[...] * (1.0 + sc1) + sh1).astype(BF)

    def col(off, width):
        return _dot(h, w_ref[:, off:off + width])

    rq_ref[...] = col(_O_RQ, RET_W).astype(BF)
    rk = col(_O_RK, RET_W) * (RET_DK ** -0.5)
    rk_ref[...] = rk.astype(BF)
    rv = col(_O_RV, RET_W).astype(BF)
    rv_ref[...] = rv
    rg_ref[...] = col(_O_RG, RET_W)

    lg = _log_sigmoid(lg_ref[...])
    p = _iota((TILE, 1), 0).astype(F32)
    kf = (rk * jnp.exp((TILE - 1.0 - p) * lg[0:1])).astype(BF)
    kb = (rk * jnp.exp(p * lg[1:2])).astype(BF)
    kvf = _dot_tn(kf, rv)
    kvb = _dot_tn(kb, rv)
    if latent:
        kvf_ref[...] = kvf
        kvb_ref[...] = kvb
    else:
        for hd in range(H_RET):
            lo = hd * RET_DK
            put(stf_ref, (hd,), kvf[lo:lo + RET_DK, lo:lo + RET_DK])
            put(stb_ref, (hd,), kvb[lo:lo + RET_DK, lo:lo + RET_DK])

    dq = col(_O_DQ, DIFF_W)
    dk = col(_O_DK, DIFF_W)
    dv = col(_O_DV, DIFF_W)
    if latent:
        cos = cos_ref[...]
        sin = sin_ref[...]
        dq = dq * cos + _swap16(dq) * sin
        dk = dk * cos + _swap16(dk) * sin
    else:
        put(dk32_ref, (), dk)
        put(dv32_ref, (), dv)
    dq_ref[...] = (dq * QK_SCALE_LOG2).astype(BF)
    dk_ref[...] = dk.astype(BF)
    dv_ref[...] = _augment_v(dv.astype(BF))

    cb_ref[...] = col(_O_CB, CONV_CH)
    cch_ref[...] = col(_O_CC, CONV_CH) * col(_O_CH, CONV_CH)


def _proj(x, mod, layer, row_fn, w_in_b, lg_lanes, rope_tabs=None, stacked=None):
    S, T, _ = x.shape
    n = T // TILE
    latent = rope_tabs is not None

    def tok(w):
        return pl.BlockSpec((None, TILE, w), lambda s, i: (s, i, 0))

    def tok_shape(w, dt):
        return jax.ShapeDtypeStruct((S, T, w), dt)

    in_specs = [
        tok(D_MODEL), _mod_spec(layer, row_fn),
        pl.BlockSpec((None, D_MODEL, IN_WIDTH), lambda s, i: (layer, 0, 0)),
        pl.BlockSpec((2, RET_W), lambda s, i: (0, 0)),
    ]
    args = [x, mod, w_in_b, lg_lanes]
    out_specs = [tok(RET_W)] * 4 + [tok(DIFF_W)] * 2 + [tok(V_AUG_W)] + [tok(CONV_CH)] * 2
    out_shape = ([tok_shape(RET_W, BF)] * 3 + [tok_shape(RET_W, F32)] + [tok_shape(DIFF_W, BF)] * 2
                 + [tok_shape(V_AUG_W, BF)]
                 + [tok_shape(CONV_CH, F32)] * 2)
    aliases = {}
    if latent:
        in_specs += [pl.BlockSpec((TILE, DIFF_W), lambda s, i: (i, 0))] * 2
        args += list(rope_tabs)
        out_specs += [pl.BlockSpec((None, None, RET_W, RET_W), lambda s, i: (s, i, 0, 0))] * 2
        out_shape += [jax.ShapeDtypeStruct((S, n, RET_W, RET_W), F32)] * 2
    else:
        assert n == 1
        kv_shape = jax.ShapeDtypeStruct((S, DEPTH, T, DIFF_W), F32)
        st_shape = jax.ShapeDtypeStruct((S, DEPTH, H_RET, RET_DK, RET_DK), F32)
        if stacked is not None:
            in_specs += [_HBM] * 4
            args += list(stacked)
            aliases = {len(args) - 4 + j: len(out_shape) + j for j in range(4)}
            out_specs += [pl.BlockSpec((None, None, TILE, DIFF_W), lambda s, i: (s, layer, i, 0))] * 2
            out_specs += [pl.BlockSpec((None, None, H_RET, RET_DK, RET_DK), lambda s, i: (s, layer, 0, 0, 0))] * 2
        else:
            out_specs += [pl.BlockSpec((None, DEPTH, TILE, DIFF_W), lambda s, i: (s, 0, i, 0))] * 2
            out_specs += [pl.BlockSpec((None, DEPTH, H_RET, RET_DK, RET_DK), lambda s, i: (s, 0, 0, 0, 0))] * 2
        out_shape += [kv_shape, kv_shape, st_shape, st_shape]
    return pl.pallas_call(
        functools.partial(_proj_kernel, latent=latent, n_aliased=len(aliases), layer=layer),
        grid=(S, n),
        in_specs=in_specs,
        out_specs=out_specs,
        out_shape=out_shape,
        input_output_aliases=aliases,
        compiler_params=_params(("parallel", "parallel")),
        name="proj_lat" if latent else "proj_ctx",
    )(*args)


def _scan_kernel(kvf_ref, kvb_ref, s0f_ref, s0b_ref, lg_ref, sf_ref, sb_ref, *, n):
    lg = _log_sigmoid(lg_ref[...])
    dec = jnp.exp(float(TILE) * lg)
    same_head = (_iota((RET_W, RET_W), 0) >> 6) == (_iota((RET_W, RET_W), 1) >> 6)

    s = jnp.where(same_head, s0f_ref[...], 0.0)
    for c in range(n):
        sf_ref[c] = s
        s = s * dec[0:1] + jnp.where(same_head, kvf_ref[c], 0.0)
    s = jnp.where(same_head, s0b_ref[...], 0.0)
    for c in reversed(range(n)):
        sb_ref[c] = s
        s = s * dec[1:2] + jnp.where(same_head, kvb_ref[c], 0.0)


def _scan(kvf, kvb, s0f, s0b, lg_lanes):
    S, n = kvf.shape[:2]
    chunks = pl.BlockSpec((None, n, RET_W, RET_W), lambda s: (s, 0, 0, 0))
    one = pl.BlockSpec((None, RET_W, RET_W), lambda s: (s, 0, 0))
    return pl.pallas_call(
        functools.partial(_scan_kernel, n=n),
        grid=(S,),
        in_specs=[chunks, chunks, one, one, pl.BlockSpec((2, RET_W), lambda s: (0, 0))],
        out_specs=[chunks, chunks],
        out_shape=[jax.ShapeDtypeStruct((S, n, RET_W, RET_W), F32)] * 2,
        compiler_params=_params(("parallel",)),
        name="ret_scan",
    )(kvf, kvb, s0f, s0b, lg_lanes)


def _diff_heads(dq_ref, kv_refs, heads, lam):
    lane = _iota((1, DIFF_V), 1)
    scores = []
    for hd in heads:
        lo = hd * DIFF_V
        q_h = dq_ref[:, lo:lo + DIFF_V]
        qs = jnp.concatenate([jnp.where(lane < DIFF_QK, q_h, 0), jnp.where(lane >= DIFF_QK, q_h, 0)], axis=0)
        scores.append(jnp.concatenate([_dot_nt(qs, k[:, lo:lo + DIFF_V]) for k, _ in kv_refs], axis=1))
    s = jnp.concatenate(scores, axis=0)
    p = jnp.exp2(s - jnp.max(s, axis=-1, keepdims=True)).astype(BF)
    outs = []
    for n, hd in enumerate(heads):
        acc = None
        start = 0
        for k, v in kv_refs:
            part = _dot(p[n * 2 * TILE:(n + 1) * 2 * TILE, start:start + k.shape[0]],
                        v[:, hd * 2 * DIFF_V:(hd + 1) * 2 * DIFF_V])
            acc = part if acc is None else acc + part
            start += k.shape[0]
        o0 = acc[0:TILE, 0:DIFF_V] * (1.0 / acc[0:TILE, DIFF_V:DIFF_V + 1])
        o1 = acc[TILE:, 0:DIFF_V] * (lam / acc[TILE:, DIFF_V:DIFF_V + 1])
        outs.append(o0 - o1)
    return outs


def _mixer_kernel(*refs, n_sub, split, **static):
    if n_sub == 1:
        return _mixer_tile(*refs, **static)
    for u in range(n_sub):
        views = [r.at[u] if how == 1 else (r.at[pl.ds(u * CAP, CAP)] if how == 2 else r)
                 for r, how in zip(refs, split)]
        _mixer_tile(*views, **static)


def _mixer_tile(*refs, n_tiles, lam_init, cached):
    (x_ref, mod_ref, rq_ref, rk_ref, rv_ref, rg_ref, sf_ref, sb_ref, dq_ref) = refs[:9]
    pos = 9
    kv_refs = []
    if cached:
        kv_refs.append((refs[pos], refs[pos + 1]))
        pos += 2
    kv_refs.append((refs[pos], refs[pos + 1]))
    pos += 2
    (cb_ref, cch_ref, cprev_ref, cnext_ref, wout_ref, lg_ref, lamp_ref, subg_ref, cw_ref, cbias_ref,
     lng_ref, lnb_ref, wrh_ref, wrl_ref, br_ref) = refs[pos:pos + 15]
    pos += 15
    x1_ref, hs_ref, slots_ref, cnt_ref = refs[pos:pos + 4]

    i = pl.program_id(1)
    m = mod_ref[...]
    g1 = m[:, 2 * D_MODEL:3 * D_MODEL]
    sh2 = m[:, 3 * D_MODEL:4 * D_MODEL]
    sc2 = m[:, 4 * D_MODEL:5 * D_MODEL]

    lg = _log_sigmoid(lg_ref[...])
    head_of_lane = _iota((1, RET_W), 1) >> 6
    q = rq_ref[...]
    k = rk_ref[...]
    v = rv_ref[...]
    dist = (_iota((TILE, TILE), 0) - _iota((TILE, TILE), 1)).astype(F32)
    adist = jnp.abs(dist)
    diag2 = jnp.where(dist == 0.0, 2.0, 1.0)
    ret_o = jnp.zeros((TILE, RET_W), F32)
    for hd in range(H_RET):
        in_head = head_of_lane == hd
        sc = _dot_nt(jnp.where(in_head, q, 0), k)
        lgf = lg[0:1, hd * RET_DK:hd * RET_DK + 1]
        lgb = lg[1:2, hd * RET_DK:hd * RET_DK + 1]
        decay = jnp.exp(adist * jnp.where(dist > 0.0, lgf, lgb)) * diag2
        ret_o = ret_o + jnp.where(in_head, _dot((sc * decay).astype(BF), v), 0.0)
    p = _iota((TILE, 1), 0).astype(F32)
    ret_o = ret_o + _dot(q, sf_ref[...].astype(BF)) * jnp.exp((p + 1.0) * lg[0:1])
    ret_o = ret_o + _dot(q, sb_ref[...].astype(BF)) * jnp.exp((float(TILE) - p) * lg[1:2])
    avg = jnp.where((_iota((RET_W, RET_W), 0) >> 6) == (_iota((RET_W, RET_W), 1) >> 6),
                    1.0 / RET_DK, 0.0).astype(BF)
    rc = ret_o - _dot_hl(ret_o, avg)
    ret = rc * lax.rsqrt(_dot_hl(rc * rc, avg) + EPS) * _silu(rg_ref[...])

    lp = lamp_ref[...]
    lam = (jnp.exp(jnp.sum(lp[0:1] * lp[1:2], axis=-1, keepdims=True))
           - jnp.exp(jnp.sum(lp[2:3] * lp[3:4], axis=-1, keepdims=True)) + lam_init)
    subg = subg_ref[...] * (1.0 - lam_init)
    head_groups = [[hd] for hd in range(H_DIFF)] if cached else [list(range(H_DIFF))]
    heads = []
    for group in head_groups:
        for o in _diff_heads(dq_ref, kv_refs, group, lam):
            o = o * lax.rsqrt(jnp.mean(o * o, axis=-1, keepdims=True) + EPS) * subg
            heads.append(o.astype(BF))
    diff = jnp.concatenate(heads, axis=1)

    cch = cch_ref[...]
    prev = jnp.where(i > 0, cprev_ref[7:8, :], 0.0)
    nxt = jnp.where(i < n_tiles - 1, cnext_ref[0:1, :], 0.0)
    r = _iota((TILE, 1), 0)
    up = jnp.where(r == 0, prev, pltpu.roll(cch, 1, axis=0))
    dn = jnp.where(r == TILE - 1, nxt, pltpu.roll(cch, TILE - 1, axis=0))
    cw = cw_ref[...]
    conv = cb_ref[...] * (up * cw[0:1] + cch * cw[1:2] + dn * cw[2:3] + cbias_ref[...])

    mix = (_dot(ret.astype(BF), wout_ref[0:RET_W, :])
           + _dot(diff, wout_ref[RET_W:RET_W + DIFF_W, :])
           + _dot(conv.astype(BF), wout_ref[RET_W + DIFF_W:, :]))
    x1 = _layer_norm(ALPHA * x_ref[...] + g1 * mix, lng_ref[...], lnb_ref[...])
    x1_ref[...] = x1

    h2 = x1 * (1.0 + sc2) + sh2
    h2h, h2l = _split(h2)
    wrh = wrh_ref[...]
    logits = (_dot_nt(wrh, h2h) + (_dot_nt(wrh, h2l) + _dot_nt(wrl_ref[...], h2h))) + br_ref[...]
    neg = -jnp.inf
    g_id = _iota((EXPERT_ROW0, TILE), 0)
    gl = jnp.where(g_id < N_GROUPS, logits[0:EXPERT_ROW0], neg)
    gmax = jnp.max(gl, axis=0, keepdims=True)
    g_idx = jnp.min(jnp.where(gl == gmax, g_id, N_GROUPS), axis=0, keepdims=True)
    g_w = 1.0 / jnp.sum(jnp.exp(gl - gmax), axis=0, keepdims=True)
    e_id = _iota((N_EXPERTS, TILE), 0)
    el = jnp.where((e_id >> 2) == g_idx, logits[EXPERT_ROW0:EXPERT_ROW0 + N_EXPERTS], neg)
    v1 = jnp.max(el, axis=0, keepdims=True)
    i1 = jnp.min(jnp.where(el == v1, e_id, N_EXPERTS), axis=0, keepdims=True)
    el2 = jnp.where(e_id == i1, neg, el)
    v2 = jnp.max(el2, axis=0, keepdims=True)
    i2 = jnp.min(jnp.where(el2 == v2, e_id, N_EXPERTS), axis=0, keepdims=True)
    t = jnp.exp(v2 - v1)
    w1 = g_w / (1.0 + t)
    w2 = g_w * t / (1.0 + t)

    sel1 = e_id == i1
    sel2 = e_id == i2
    sel = jnp.where(sel1 | sel2, 1.0, 0.0)
    earlier = _ones_where(_iota((TILE, TILE), 0) < _iota((TILE, TILE), 1))
    rank = _dot(sel.astype(BF), earlier)
    cnt = jnp.sum(sel, axis=1, keepdims=True).astype(I32)
    padded = ((cnt + (CHUNK - 1)) >> 4) << 4
    incl = jnp.broadcast_to(padded, (N_EXPERTS, LANES))
    e_row = _iota((N_EXPERTS, LANES), 0)
    for step in (1, 2, 4, 8):
        incl = incl + jnp.where(e_row >= step, pltpu.roll(incl, step, axis=0), 0)
    seg_off = (incl[:, 0:1] - padded).astype(F32)
    spos = seg_off + rank
    slot1 = jnp.sum(jnp.where(sel1, spos, 0.0), axis=0, keepdims=True).astype(I32)
    slot2 = jnp.sum(jnp.where(sel2, spos, 0.0), axis=0, keepdims=True).astype(I32)
    s_id = _iota((CAP, TILE), 0)
    at1 = s_id == slot1
    at2 = s_id == slot2
    hs_ref[:, 0:D_MODEL] = _dot(_ones_where(at1 | at2), h2h).astype(BF)
    w1h = w1.astype(BF).astype(F32)
    w2h = w2.astype(BF).astype(F32)
    ones = jnp.ones((TILE, LANES), BF)
    g_hi = _dot((jnp.where(at1, w1h, 0.0) + jnp.where(at2, w2h, 0.0)).astype(BF), ones)
    g_lo = _dot((jnp.where(at1, w1 - w1h, 0.0) + jnp.where(at2, w2 - w2h, 0.0)).astype(BF), ones)
    lane = _iota((CAP, LANES), 1)
    hs_ref[:, D_MODEL:] = jnp.where(lane == 0, g_hi, jnp.where(lane == 1, g_lo, 0.0)).astype(BF)
    cnt_ref[...] = jnp.broadcast_to(padded, (N_EXPERTS, LANES))

    dr = _iota((LANES, TILE), 0)
    digits = jnp.where(dr == 0, slot1 & (SLOT_RADIX - 1), jnp.where(dr == 1, slot1 >> 5,
             jnp.where(dr == 2, slot2 & (SLOT_RADIX - 1), jnp.where(dr == 3, slot2 >> 5, 0))))
    eye = _ones_where(_iota((TILE, TILE), 0) == _iota((TILE, TILE), 1))
    cols = _dot_nt(eye, digits.astype(F32).astype(BF))
    s1c = (cols[:, 0:1] + SLOT_RADIX * cols[:, 1:2]).astype(I32)
    s2c = (cols[:, 2:3] + SLOT_RADIX * cols[:, 3:4]).astype(I32)
    tl = _iota((TILE, LANES), 1)
    slots_ref[...] = jnp.where(tl == 0, s1c, jnp.where(tl == 1, s2c, 0))


def _mixer(x, mod, layer, row_fn, pr, states, cache_kv, new_kv, w_out_b, lg_lanes, lamp, subg, cw, cbias,
           lng, lnb, wrh, wrl, br, lam_init):
    S, T, _ = x.shape
    n = T // TILE
    rq, rk, rv, rg, dq, cb, cch = pr
    t8 = TILE // 8
    cached = cache_kv is not None
    n_sub = CTX_SEQS_PER_STEP if (n == 1 and S % CTX_SEQS_PER_STEP == 0) else 1
    lead = None if n_sub == 1 else n_sub
    SEQ, SHARED, ROWS = 1, 0, 2

    def tok(w):
        return pl.BlockSpec((lead, TILE, w), lambda s, i: (s, i, 0))

    def full(shape):
        return pl.BlockSpec(shape, lambda s, i: (0,) * len(shape))

    def seq(a):
        mode = pl.Buffered(1) if n > 1 else None
        return pl.BlockSpec((lead,) + a.shape[1:], lambda s, i: (s, 0, 0), pipeline_mode=mode)

    if states is None:
        zero_state = jnp.zeros((1, 1, RET_W, RET_W), F32)
        states = (zero_state, zero_state)
        state = pl.BlockSpec((None, None, RET_W, RET_W), lambda s, i: (0, 0, 0, 0))
    else:
        assert n_sub == 1
        state = pl.BlockSpec((None, None, RET_W, RET_W), lambda s, i: (s, i, 0, 0))
    halo_prev = pl.BlockSpec((lead, 8, CONV_CH), lambda s, i: (s, jnp.maximum(i * t8 - 1, 0), 0))
    halo_next = pl.BlockSpec((lead, 8, CONV_CH), lambda s, i: (s, jnp.minimum((i + 1) * t8, T // 8 - 1), 0))
    in_specs = [tok(D_MODEL), _mod_spec(layer, row_fn), tok(RET_W), tok(RET_W), tok(RET_W), tok(RET_W),
                state, state, tok(DIFF_W)]
    split = [SEQ, SHARED, SEQ, SEQ, SEQ, SEQ, SHARED, SHARED, SEQ]
    args = [x, mod, rq, rk, rv, rg, states[0], states[1], dq]
    if cached:
        in_specs += [seq(cache_kv[0]), seq(cache_kv[1])]
        split += [SEQ, SEQ]
        args += list(cache_kv)
    in_specs += [seq(new_kv[0]), seq(new_kv[1])]
    split += [SEQ, SEQ]
    args += list(new_kv)
    in_specs += [
        tok(CONV_CH), tok(CONV_CH), halo_prev, halo_next,
        pl.BlockSpec((None, D_MODEL, D_MODEL), lambda s, i: (layer, 0, 0)),
        full((2, RET_W)), full((4, DIFF_QK)), full((1, DIFF_V)),
        full((3, CONV_CH)), full((1, CONV_CH)), full((1, D_MODEL)), full((1, D_MODEL)),
        full((ROUTER_ROWS, D_MODEL)), full((ROUTER_ROWS, D_MODEL)), full((ROUTER_ROWS, 1)),
    ]
    split += [SEQ] * 4 + [SHARED] * 11
    args += [cb, cch, cch, cch, w_out_b, lg_lanes, lamp, subg, cw, cbias, lng, lnb, wrh, wrl, br]
    split += [SEQ, ROWS, SEQ, SEQ]
    return pl.pallas_call(
        functools.partial(_mixer_kernel, n_sub=n_sub, split=tuple(split), n_tiles=n, lam_init=lam_init,
                          cached=cached),
        grid=(S // n_sub, n),
        in_specs=in_specs,
        out_specs=[tok(D_MODEL),
                   pl.BlockSpec((n_sub * CAP, HS_W), lambda s, i: (s * n + i, 0)),
                   tok(LANES),
                   pl.BlockSpec((lead, None, N_EXPERTS, LANES), lambda s, i: (s, i, 0, 0))],
        out_shape=[jax.ShapeDtypeStruct((S, T, D_MODEL), F32),
                   jax.ShapeDtypeStruct((S * n * CAP, HS_W), BF),
                   jax.ShapeDtypeStruct((S, T, LANES), I32),
                   jax.ShapeDtypeStruct((S, n, N_EXPERTS, LANES), I32)],
        compiler_params=_params(("parallel", "parallel")),
        name="mixer_lat" if cached else "mixer_ctx",
    )(*args)


def _block_schedule(padded, chunks_a):
    NT = padded.shape[0]
    n_chunks = NT * CHUNKS_PER_TILE
    n_blocks = n_chunks // CPB + N_EXPERTS
    c16 = padded // CHUNK
    ends = jnp.cumsum(c16, axis=1)
    before = jnp.cumsum(c16, axis=0) - c16
    per_expert = jnp.sum(c16, axis=0)
    nb = (per_expert + (CPB - 1)) // CPB
    b_end = jnp.cumsum(nb)
    q = jnp.arange(CHUNKS_PER_TILE, dtype=I32)
    e = jnp.arange(N_EXPERTS, dtype=I32)
    key = jnp.sum((ends[:, None, :] <= q[None, :, None]).astype(I32), axis=-1)
    pos_e = ((b_end - nb) * CPB + before)[:, None, :] + (q[None, :, None] - (ends - c16)[:, None, :])
    pos = jnp.sum(jnp.where(key[..., None] == e, pos_e, 0), axis=-1)
    pos = jnp.where(key < N_EXPERTS, pos, -1).reshape(-1)
    match = pos[None, :] == jnp.arange(n_blocks * CPB, dtype=I32)[:, None]
    cid = jnp.sum(jnp.where(match, jnp.arange(n_chunks, dtype=I32)[None, :], 0), axis=-1)
    nv = jnp.sum(match.astype(I32).reshape(n_blocks, -1), axis=-1)
    first = match & (jnp.arange(n_chunks, dtype=I32) < chunks_a)[None, :]
    nv_a = jnp.sum(first.astype(I32).reshape(n_blocks, -1), axis=-1)
    b = jnp.arange(n_blocks, dtype=I32)
    eb = jnp.minimum(jnp.sum((b_end[None, :] <= b[:, None]).astype(I32), axis=-1), N_EXPERTS - 1)
    return eb.astype(I32), jnp.concatenate([nv, nv_a]).astype(I32), cid.astype(I32)


def _moe_kernel(eb_ref, nv_ref, cid_ref, hs_a, hs_b, wg_ref, wu_ref, wd_ref, ys_a, ys_b,
                xbuf, ybuf, wg_b, wu_b, wd_b, in_sem, out_sem, *, n_blocks, chunks_a):
    b = pl.program_id(0)

    def chunk_rows(c):
        return pl.ds(pl.multiple_of(c * CHUNK, CHUNK), CHUNK)

    def gather_copy(which, c, j, slot):
        src = (hs_a, hs_b)[which]
        return pltpu.make_async_copy(src.at[chunk_rows(c)], xbuf.at[slot, chunk_rows(j)], in_sem.at[slot])

    def scatter_copy(which, c, j, slot):
        dst = (ys_a, ys_b)[which]
        return pltpu.make_async_copy(ybuf.at[slot, chunk_rows(j)],
                                     dst.at[chunk_rows(c), pl.ds(0, D_MODEL)], out_sem.at[slot])

    def for_chunks(blk, fn):
        n_a = nv_ref[n_blocks + blk]

        def body_a(j, carry):
            fn(j, 0, cid_ref[blk * CPB + j])
            return carry

        def body_b(j, carry):
            fn(j, 1, cid_ref[blk * CPB + j] - chunks_a)
            return carry
        lax.fori_loop(0, n_a, body_a, 0)
        lax.fori_loop(n_a, nv_ref[blk], body_b, 0)

    slot = lax.rem(b, 2)
    other = 1 - slot

    @pl.when(b == 0)
    def _():
        xbuf[...] = jnp.zeros_like(xbuf)
        for_chunks(0, lambda j, w, c: gather_copy(w, c, j, 0).start())

    @pl.when(b + 1 < n_blocks)
    def _():
        for_chunks(b + 1, lambda j, w, c: gather_copy(w, c, j, other).start())

    @pl.when((b == 0) | (eb_ref[b] != eb_ref[jnp.maximum(b - 1, 0)]))
    def _():
        wg_b[...] = wg_ref[...].astype(BF)
        wu_b[...] = wu_ref[...].astype(BF)
        wd_b[...] = wd_ref[...].astype(BF)

    for_chunks(b, lambda j, w, c: gather_copy(w, c, j, slot).wait())

    @pl.when(b >= 2)
    def _():
        for_chunks(b - 2, lambda j, w, c: scatter_copy(w, c, j, slot).wait())

    @pl.when(nv_ref[b] > 0)
    def _():
        x = xbuf[slot, :, 0:D_MODEL]
        gate = (xbuf[slot, :, D_MODEL:D_MODEL + 1].astype(F32)
                + xbuf[slot, :, D_MODEL + 1:D_MODEL + 2].astype(F32))
        hg = _dot(x, wg_b[...])
        hu = _dot(x, wu_b[...])
        act = _silu(hg) * hu * gate
        ybuf[slot] = _dot(act.astype(BF), wd_b[...]).astype(BF)

    for_chunks(b, lambda j, w, c: scatter_copy(w, c, j, slot).start())

    @pl.when(b == n_blocks - 1)
    def _():
        if n_blocks >= 2:
            for_chunks(b - 1, lambda j, w, c: scatter_copy(w, c, j, other).wait())
        for_chunks(b, lambda j, w, c: scatter_copy(w, c, j, slot).wait())


def _moe(hs_a, hs_b, eb, nv, cid, layer, w_gate, w_up, w_down):
    n_blocks = eb.shape[0]

    def w_spec(a, c):
        return pl.BlockSpec((None, None, a, c), lambda b, eb, nv, cid: (layer, eb[b], 0, 0))

    grid_spec = pltpu.PrefetchScalarGridSpec(
        num_scalar_prefetch=3,
        grid=(n_blocks,),
        in_specs=[_HBM, _HBM, w_spec(D_MODEL, D_EXPERT), w_spec(D_MODEL, D_EXPERT), w_spec(D_EXPERT, D_MODEL)],
        out_specs=[_HBM, _HBM],
        scratch_shapes=[pltpu.VMEM((2, MOE_TMB, HS_W), BF), pltpu.VMEM((2, MOE_TMB, D_MODEL), BF),
                        pltpu.VMEM((D_MODEL, D_EXPERT), BF), pltpu.VMEM((D_MODEL, D_EXPERT), BF),
                        pltpu.VMEM((D_EXPERT, D_MODEL), BF),
                        pltpu.SemaphoreType.DMA((2,)), pltpu.SemaphoreType.DMA((2,))],
    )
    return pl.pallas_call(
        functools.partial(_moe_kernel, n_blocks=n_blocks, chunks_a=hs_a.shape[0] // CHUNK),
        grid_spec=grid_spec,
        out_shape=[jax.ShapeDtypeStruct(hs_a.shape, BF), jax.ShapeDtypeStruct(hs_b.shape, BF)],
        input_output_aliases={3: 0, 4: 1},
        compiler_params=_params(("arbitrary",)),
        name="moe_sorted",
    )(eb, nv, cid, hs_a, hs_b, w_gate, w_up, w_down)


def _combine_kernel(ys_ref, slots_ref, x1_ref, mod_ref, lng_ref, lnb_ref, o_ref):
    g2 = mod_ref[:, 5 * D_MODEL:6 * D_MODEL]
    s_lane = _iota((TILE, CAP), 1)
    for u in range(COMBINE_TILES_PER_STEP):
        sl = slots_ref[u]
        pick = _ones_where((s_lane == sl[:, 0:1]) | (s_lane == sl[:, 1:2]))
        ffn = _dot(pick, ys_ref[u * CAP:(u + 1) * CAP, 0:D_MODEL])
        o_ref[u] = _layer_norm(ALPHA * x1_ref[u] + g2 * ffn, lng_ref[...], lnb_ref[...])


def _combine(ys, slots, x1, mod, layer, row_fn, lng, lnb):
    S, T, _ = x1.shape
    n = T // TILE
    U = COMBINE_TILES_PER_STEP
    assert (S * n) % U == 0 and (n % U == 0 or n == 1)

    def tok(w):
        return pl.BlockSpec((U, TILE, w), lambda t: (t, 0, 0))

    out = pl.pallas_call(
        _combine_kernel,
        grid=(S * n // U,),
        in_specs=[
            pl.BlockSpec((U * CAP, HS_W), lambda t: (t, 0)),
            tok(LANES), tok(D_MODEL),
            pl.BlockSpec((None, None, 1, 6 * D_MODEL), lambda t: (layer, row_fn(t * U // n), 0, 0)),
            pl.BlockSpec((1, D_MODEL), lambda t: (0, 0)),
            pl.BlockSpec((1, D_MODEL), lambda t: (0, 0)),
        ],
        out_specs=tok(D_MODEL),
        out_shape=jax.ShapeDtypeStruct((S * n, TILE, D_MODEL), F32),
        compiler_params=_params(("parallel",)),
        name="moe_combine",
    )(ys, slots.reshape(S * n, TILE, LANES), x1.reshape(S * n, TILE, D_MODEL), mod, lng, lnb)
    return out.reshape(S, T, D_MODEL)


def _rope_tables(n_lat):
    half = DIFF_QK // 2
    pairs = half // 2
    inv = (1.0 / (ROPE_BASE ** (np.arange(pairs, dtype=np.float32) * 2.0 / half))).astype(np.float32)
    t = np.arange(n_lat)
    ang_r = ((t // GRID_W).astype(np.float32)[:, None] * inv[None, :]).astype(np.float64)
    ang_c = ((t % GRID_W).astype(np.float32)[:, None] * inv[None, :]).astype(np.float64)
    cos = np.concatenate([np.cos(ang_r)] * 2 + [np.cos(ang_c)] * 2, axis=1)
    sin = np.concatenate([-np.sin(ang_r), np.sin(ang_r), -np.sin(ang_c), np.sin(ang_c)], axis=1)
    reps = DIFF_W // DIFF_QK
    return (jnp.asarray(np.tile(cos, (1, reps)), F32), jnp.asarray(np.tile(sin, (1, reps)), F32))


def _block_diag(s):
    S = s.shape[0]
    eye = jnp.eye(H_RET, dtype=s.dtype)
    return jnp.einsum('shdv,hg->shdgv', s, eye).reshape(S, RET_W, RET_W)


def _router_rows(w_group, b_group, w_expert, b_expert):
    pad = EXPERT_ROW0 - N_GROUPS
    tail = ROUTER_ROWS - EXPERT_ROW0 - N_EXPERTS
    w = jnp.concatenate([w_group.T, jnp.zeros((pad, D_MODEL), F32),
                         w_expert.reshape(D_MODEL, N_EXPERTS).T, jnp.zeros((tail, D_MODEL), F32)], axis=0)
    bias = jnp.concatenate([b_group, jnp.zeros((pad,), F32), b_expert.reshape(N_EXPERTS), jnp.zeros((tail,), F32)])
    hi = w.astype(BF)
    return hi, (w - hi.astype(F32)).astype(BF), bias.reshape(ROUTER_ROWS, 1)


def kernel(x_prompt, x_sample, cache_diff_k, cache_diff_v, state_ret_fwd, state_ret_bwd, c, c_ctx, w_mod, b_mod, w_in, ret_decay_logit, diff_lambda, diff_subln_g, conv_w, conv_b, w_out, ln_g, ln_b, w_router_group, b_router_group, w_router_expert, b_router_expert, w_gate, w_up, w_down):
    B, T_ctx, _ = x_prompt.shape
    Bd, T_lat, _ = x_sample.shape
    assert T_ctx == TILE and T_lat % TILE == 0 and T_lat % GRID_W == 0
    assert 1 + Bd <= MOD_ROWS

    cond = jnp.concatenate([c_ctx[None, :], c, jnp.zeros((MOD_ROWS - 1 - Bd, D_MODEL), F32)], axis=0)
    mod = _modulation(cond, w_mod, b_mod).reshape(DEPTH, MOD_ROWS, 1, 6 * D_MODEL)
    rope_tabs = _rope_tables(T_lat)
    w_in_b = w_in.astype(BF)
    w_out_b = w_out.astype(BF)

    ctx_row = lambda s: 0
    lat_row = lambda s: s + 1
    ctx_tiles = B * (T_ctx // TILE)
    lat_tiles = Bd * (T_lat // TILE)

    yp, ys = x_prompt, x_sample
    stacked = None
    for l in range(DEPTH):
        lam_init = 0.8 - 0.6 * math.exp(-0.3 * l)
        lg_lanes = jnp.repeat(ret_decay_logit[l], RET_DK, axis=1)
        wrh, wrl, br = _router_rows(w_router_group[l], b_router_group[l], w_router_expert[l], b_router_expert[l])
        shared = (w_out_b, lg_lanes, diff_lambda[l], diff_subln_g[l].reshape(1, DIFF_V), conv_w[l],
                  conv_b[l].reshape(1, CONV_CH), ln_g[l, 0].reshape(1, D_MODEL), ln_b[l, 0].reshape(1, D_MODEL),
                  wrh, wrl, br, lam_init)
        ln2 = (ln_g[l, 1].reshape(1, D_MODEL), ln_b[l, 1].reshape(1, D_MODEL))

        pr = _proj(yp, mod, l, ctx_row, w_in_b, lg_lanes, stacked=stacked)
        rq, rk, rv, rg, dq, dk, dv, cb, cch = pr[:9]
        stacked = pr[9:]
        x1_c, hs_c, slots_c, cnt_c = _mixer(yp, mod, l, ctx_row, (rq, rk, rv, rg, dq, cb, cch), None, None,
                                            (dk, dv), *shared)

        pr = _proj(ys, mod, l, lat_row, w_in_b, lg_lanes, rope_tabs=rope_tabs)
        rq, rk, rv, rg, dq, dk, dv, cb, cch, kvf, kvb = pr
        states = _scan(kvf, kvb, _block_diag(state_ret_fwd[:, l]), _block_diag(state_ret_bwd[:, l]), lg_lanes)
        cache_v = cache_diff_v[:, l].astype(BF)
        cache_kv = (cache_diff_k[:, l].reshape(Bd, -1, DIFF_W).astype(BF),
                    jnp.concatenate([cache_v, jnp.ones_like(cache_v)], axis=-1).reshape(Bd, -1, V_AUG_W))
        x1_l, hs_l, slots_l, cnt_l = _mixer(ys, mod, l, lat_row, (rq, rk, rv, rg, dq, cb, cch), states, cache_kv,
                                            (dk, dv), *shared)

        padded = jnp.concatenate([cnt_c.reshape(ctx_tiles, N_EXPERTS, LANES)[:, :, 0],
                                  cnt_l.reshape(lat_tiles, N_EXPERTS, LANES)[:, :, 0]], axis=0)
        eb, nv, cid = _block_schedule(padded, ctx_tiles * CHUNKS_PER_TILE)
        out_c, out_l = _moe(hs_c, hs_l, eb, nv, cid, l, w_gate, w_up, w_down)
        yp = _combine(out_c, slots_c, x1_c, mod, l, ctx_row, *ln2)
        ys = _combine(out_l, slots_l, x1_l, mod, l, lat_row, *ln2)

    new_k, new_v, st_f, st_b = stacked
    return (yp, ys, new_k.reshape(B, DEPTH, T_ctx, H_DIFF, 2, DIFF_QK),
            new_v.reshape(B, DEPTH, T_ctx, H_DIFF, DIFF_V), st_f, st_b)
```

```python
import functools
import math

import numpy as np
import jax
import jax.numpy as jnp
from jax import lax
from jax.experimental import pallas as pl
from jax.experimental.pallas import tpu as pltpu

D_MODEL = 1024
DEPTH = 2
GRID_W = 64
H_RET = 4
RET_DK = 64
RET_W = H_RET * RET_DK
H_DIFF = 4
DIFF_QK = 64
DIFF_V = 2 * DIFF_QK
DIFF_W = H_DIFF * DIFF_V
CONV_CH = 256
ROPE_BASE = 10000.0
N_GROUPS = 4
EXPERTS_PER_GROUP = 4
N_EXPERTS = N_GROUPS * EXPERTS_PER_GROUP
D_EXPERT = 512
ALPHA = (2 * DEPTH) ** 0.25
EPS = 1e-5
MOD_ROWS = 8
LANES = 128
ROUTER_ROWS = 32
EXPERT_ROW0 = 8

TILE = 256
CHUNK = 16
CAP = 2 * TILE + N_EXPERTS * CHUNK
CHUNKS_PER_TILE = CAP // CHUNK
CTX_SEQS_PER_STEP = 2
COMBINE_TILES_PER_STEP = 2
MOE_TMB = 256
CPB = MOE_TMB // CHUNK
HS_W = D_MODEL + LANES
V_AUG_W = 2 * DIFF_W
QK_SCALE_LOG2 = (DIFF_QK ** -0.5) * math.log2(math.e)
SLOT_RADIX = 32

_O_RQ, _O_RK, _O_RV, _O_RG = 0, 256, 512, 768
_O_DQ, _O_DK, _O_DV = 1024, 1536, 2048
_O_CB, _O_CC, _O_CH = 2560, 2816, 3072
IN_WIDTH = 3328

BF = jnp.bfloat16
F32 = jnp.float32
I32 = jnp.int32

_VMEM_LIMIT = 56 * 1024 * 1024


def _dot(a, b):
    return jnp.dot(a, b, preferred_element_type=F32)


def _dot_nt(a, b):
    return lax.dot_general(a, b, (((1,), (1,)), ((), ())), preferred_element_type=F32)


def _dot_tn(a, b):
    return lax.dot_general(a, b, (((0,), (0,)), ((), ())), preferred_element_type=F32)


def _split(x):
    hi = x.astype(BF)
    lo = (x - hi.astype(F32)).astype(BF)
    return hi, lo


def _dot_hl(x, w_bf16):
    hi, lo = _split(x)
    return _dot(hi, w_bf16) + _dot(lo, w_bf16)


def _dot3(x, w_hi, w_lo):
    hi, lo = _split(x)
    return _dot(hi, w_hi) + (_dot(lo, w_hi) + _dot(hi, w_lo))


def _iota(shape, dim):
    return lax.broadcasted_iota(I32, shape, dim)


def _ones_where(cond):
    return jnp.where(cond, 1.0, 0.0).astype(BF)


def _log_sigmoid(x):
    return jnp.minimum(x, 0.0) - jnp.log1p(jnp.exp(-jnp.abs(x)))


def _silu(x):
    return x * jax.nn.sigmoid(x)


def _layer_norm(y, g, b):
    mu = jnp.mean(y, axis=-1, keepdims=True)
    yc = y - mu
    var = jnp.mean(yc * yc, axis=-1, keepdims=True)
    return yc * lax.rsqrt(var + EPS) * g + b


def _params(sem):
    return pltpu.CompilerParams(dimension_semantics=sem, vmem_limit_bytes=_VMEM_LIMIT)


_HBM = pl.BlockSpec(memory_space=pltpu.HBM)


def _mod_kernel(c_ref, w_ref, b_ref, o_ref):
    a = _silu(c_ref[...])
    w_hi, w_lo = _split(w_ref[...])
    o_ref[...] = _dot3(a, w_hi, w_lo) + b_ref[...]


def _modulation(cond, w_mod, b_mod):
    tn = 1536
    n6 = 6 * D_MODEL
    return pl.pallas_call(
        _mod_kernel,
        grid=(DEPTH, n6 // tn),
        in_specs=[
            pl.BlockSpec((MOD_ROWS, D_MODEL), lambda l, j: (0, 0)),
            pl.BlockSpec((None, D_MODEL, tn), lambda l, j: (l, 0, j)),
            pl.BlockSpec((None, 1, tn), lambda l, j: (l, 0, j)),
        ],
        out_specs=pl.BlockSpec((None, MOD_ROWS, tn), lambda l, j: (l, 0, j)),
        out_shape=jax.ShapeDtypeStruct((DEPTH, MOD_ROWS, n6), F32),
        compiler_params=_params(("parallel", "parallel")),
        name="modulation",
    )(cond, w_mod, b_mod.reshape(DEPTH, 1, n6))


def _mod_spec(layer, row_fn):
    return pl.BlockSpec((None, None, 1, 6 * D_MODEL), lambda s, i: (layer, row_fn(s), 0, 0))


def _augment_v(v):
    ones = jnp.ones((v.shape[0], DIFF_V), v.dtype)
    parts = []
    for hd in range(H_DIFF):
        parts += [v[:, hd * DIFF_V:(hd + 1) * DIFF_V], ones]
    return jnp.concatenate(parts, axis=1)


def _swap16(x):
    n = x.shape[-1]
    lane = _iota(x.shape, 1)
    return jnp.where((lane & 16) == 0, pltpu.roll(x, n - 16, axis=1), pltpu.roll(x, 16, axis=1))


def _proj_kernel(*refs, latent, n_aliased, layer):
    x_ref, mod_ref, w_ref, lg_ref = refs[:4]
    pos = 4
    if latent:
        cos_ref, sin_ref = refs[pos:pos + 2]
        pos += 2
    pos += n_aliased
    rq_ref, rk_ref, rv_ref, rg_ref, dq_ref, dk_ref, dv_ref, cb_ref, cch_ref = refs[pos:pos + 9]
    pos += 9
    if latent:
        kvf_ref, kvb_ref = refs[pos:pos + 2]
    else:
        dk32_ref, dv32_ref, stf_ref, stb_ref = refs[pos:pos + 4]

    def put(ref, idx, val):
        if n_aliased:
            ref[idx if idx else ...] = val
        else:
            for other in range(DEPTH):
                ref[(other,) + idx] = val if other == layer else jnp.zeros_like(val)

    m = mod_ref[...]
    sh1 = m[:, 0:D_MODEL]
    sc1 = m[:, D_MODEL:2 * D_MODEL]
    h = (x_ref[...] * (1.0 + sc1) + sh1).astype(BF)

    def col(off, width):
        return _dot(h, w_ref[:, off:off + width])

    rq_ref[...] = col(_O_RQ, RET_W).astype(BF)
    rk = col(_O_RK, RET_W) * (RET_DK ** -0.5)
    rk_ref[...] = rk.astype(BF)
    rv = col(_O_RV, RET_W).astype(BF)
    rv_ref[...] = rv
    rg_ref[...] = col(_O_RG, RET_W)

    lg = _log_sigmoid(lg_ref[...])
    p = _iota((TILE, 1), 0).astype(F32)
    kf = (rk * jnp.exp((TILE - 1.0 - p) * lg[0:1])).astype(BF)
    kb = (rk * jnp.exp(p * lg[1:2])).astype(BF)
    kvf = _dot_tn(kf, rv)
    kvb = _dot_tn(kb, rv)
    if latent:
        kvf_ref[...] = kvf
        kvb_ref[...] = kvb
    else:
        for hd in range(H_RET):
            lo = hd * RET_DK
            put(stf_ref, (hd,), kvf[lo:lo + RET_DK, lo:lo + RET_DK])
            put(stb_ref, (hd,), kvb[lo:lo + RET_DK, lo:lo + RET_DK])

    dq = col(_O_DQ, DIFF_W)
    dk = col(_O_DK, DIFF_W)
    dv = col(_O_DV, DIFF_W)
    if latent:
        cos = cos_ref[...]
        sin = sin_ref[...]
        dq = dq * cos + _swap16(dq) * sin
        dk = dk * cos + _swap16(dk) * sin
    else:
        put(dk32_ref, (), dk)
        put(dv32_ref, (), dv)
    dq_ref[...] = (dq * QK_SCALE_LOG2).astype(BF)
    dk_ref[...] = dk.astype(BF)
    dv_ref[...] = _augment_v(dv.astype(BF))

    cb_ref[...] = col(_O_CB, CONV_CH)
    cch_ref[...] = col(_O_CC, CONV_CH) * col(_O_CH, CONV_CH)


def _proj(x, mod, layer, row_fn, w_in_b, lg_lanes, rope_tabs=None, stacked=None):
    S, T, _ = x.shape
    n = T // TILE
    latent = rope_tabs is not None

    def tok(w):
        return pl.BlockSpec((None, TILE, w), lambda s, i: (s, i, 0))

    def tok_shape(w, dt):
        return jax.ShapeDtypeStruct((S, T, w), dt)

    in_specs = [
        tok(D_MODEL), _mod_spec(layer, row_fn),
        pl.BlockSpec((None, D_MODEL, IN_WIDTH), lambda s, i: (layer, 0, 0)),
        pl.BlockSpec((2, RET_W), lambda s, i: (0, 0)),
    ]
    args = [x, mod, w_in_b, lg_lanes]
    out_specs = [tok(RET_W)] * 4 + [tok(DIFF_W)] * 2 + [tok(V_AUG_W)] + [tok(CONV_CH)] * 2
    out_shape = ([tok_shape(RET_W, BF)] * 3 + [tok_shape(RET_W, F32)] + [tok_shape(DIFF_W, BF)] * 2
                 + [tok_shape(V_AUG_W, BF)]
                 + [tok_shape(CONV_CH, F32)] * 2)
    aliases = {}
    if latent:
        in_specs += [pl.BlockSpec((TILE, DIFF_W), lambda s, i: (i, 0))] * 2
        args += list(rope_tabs)
        out_specs += [pl.BlockSpec((None, None, RET_W, RET_W), lambda s, i: (s, i, 0, 0))] * 2
        out_shape += [jax.ShapeDtypeStruct((S, n, RET_W, RET_W), F32)] * 2
    else:
        assert n == 1
        kv_shape = jax.ShapeDtypeStruct((S, DEPTH, T, DIFF_W), F32)
        st_shape = jax.ShapeDtypeStruct((S, DEPTH, H_RET, RET_DK, RET_DK), F32)
        if stacked is not None:
            in_specs += [_HBM] * 4
            args += list(stacked)
            aliases = {len(args) - 4 + j: len(out_shape) + j for j in range(4)}
            out_specs += [pl.BlockSpec((None, None, TILE, DIFF_W), lambda s, i: (s, layer, i, 0))] * 2
            out_specs += [pl.BlockSpec((None, None, H_RET, RET_DK, RET_DK), lambda s, i: (s, layer, 0, 0, 0))] * 2
        else:
            out_specs += [pl.BlockSpec((None, DEPTH, TILE, DIFF_W), lambda s, i: (s, 0, i, 0))] * 2
            out_specs += [pl.BlockSpec((None, DEPTH, H_RET, RET_DK, RET_DK), lambda s, i: (s, 0, 0, 0, 0))] * 2
        out_shape += [kv_shape, kv_shape, st_shape, st_shape]
    return pl.pallas_call(
        functools.partial(_proj_kernel, latent=latent, n_aliased=len(aliases), layer=layer),
        grid=(S, n),
        in_specs=in_specs,
        out_specs=out_specs,
        out_shape=out_shape,
        input_output_aliases=aliases,
        compiler_params=_params(("parallel", "parallel")),
        name="proj_lat" if latent else "proj_ctx",
    )(*args)


def _scan_kernel(kvf_ref, kvb_ref, s0f_ref, s0b_ref, lg_ref, sf_ref, sb_ref, *, n):
    lg = _log_sigmoid(lg_ref[...])
    dec = jnp.exp(float(TILE) * lg)
    same_head = (_iota((RET_W, RET_W), 0) >> 6) == (_iota((RET_W, RET_W), 1) >> 6)

    s = jnp.where(same_head, s0f_ref[...], 0.0)
    for c in range(n):
        sf_ref[c] = s
        s = s * dec[0:1] + jnp.where(same_head, kvf_ref[c], 0.0)
    s = jnp.where(same_head, s0b_ref[...], 0.0)
    for c in reversed(range(n)):
        sb_ref[c] = s
        s = s * dec[1:2] + jnp.where(same_head, kvb_ref[c], 0.0)


def _scan(kvf, kvb, s0f, s0b, lg_lanes):
    S, n = kvf.shape[:2]
    chunks = pl.BlockSpec((None, n, RET_W, RET_W), lambda s: (s, 0, 0, 0))
    one = pl.BlockSpec((None, RET_W, RET_W), lambda s: (s, 0, 0))
    return pl.pallas_call(
        functools.partial(_scan_kernel, n=n),
        grid=(S,),
        in_specs=[chunks, chunks, one, one, pl.BlockSpec((2, RET_W), lambda s: (0, 0))],
        out_specs=[chunks, chunks],
        out_shape=[jax.ShapeDtypeStruct((S, n, RET_W, RET_W), F32)] * 2,
        compiler_params=_params(("parallel",)),
        name="ret_scan",
    )(kvf, kvb, s0f, s0b, lg_lanes)


def _diff_heads(dq_ref, kv_refs, heads, lam):
    lane = _iota((1, DIFF_V), 1)
    scores = []
    for hd in heads:
        lo = hd * DIFF_V
        q_h = dq_ref[:, lo:lo + DIFF_V]
        qs = jnp.concatenate([jnp.where(lane < DIFF_QK, q_h, 0), jnp.where(lane >= DIFF_QK, q_h, 0)], axis=0)
        scores.append(jnp.concatenate([_dot_nt(qs, k[:, lo:lo + DIFF_V]) for k, _ in kv_refs], axis=1))
    s = jnp.concatenate(scores, axis=0)
    p = jnp.exp2(s - jnp.max(s, axis=-1, keepdims=True)).astype(BF)
    outs = []
    for n, hd in enumerate(heads):
        acc = None
        start = 0
        for k, v in kv_refs:
            part = _dot(p[n * 2 * TILE:(n + 1) * 2 * TILE, start:start + k.shape[0]],
                        v[:, hd * 2 * DIFF_V:(hd + 1) * 2 * DIFF_V])
            acc = part if acc is None else acc + part
            start += k.shape[0]
        o0 = acc[0:TILE, 0:DIFF_V] * (1.0 / acc[0:TILE, DIFF_V:DIFF_V + 1])
        o1 = acc[TILE:, 0:DIFF_V] * (lam / acc[TILE:, DIFF_V:DIFF_V + 1])
        outs.append(o0 - o1)
    return outs


def _mixer_kernel(*refs, n_sub, split, **static):
    if n_sub == 1:
        return _mixer_tile(*refs, **static)
    for u in range(n_sub):
        views = [r.at[u] if how == 1 else (r.at[pl.ds(u * CAP, CAP)] if how == 2 else r)
                 for r, how in zip(refs, split)]
        _mixer_tile(*views, **static)


def _mixer_tile(*refs, n_tiles, lam_init, cached):
    (x_ref, mod_ref, rq_ref, rk_ref, rv_ref, rg_ref, sf_ref, sb_ref, dq_ref) = refs[:9]
    pos = 9
    kv_refs = []
    if cached:
        kv_refs.append((refs[pos], refs[pos + 1]))
        pos += 2
    kv_refs.append((refs[pos], refs[pos + 1]))
    pos += 2
    (cb_ref, cch_ref, cprev_ref, cnext_ref, wout_ref, lg_ref, lamp_ref, subg_ref, cw_ref, cbias_ref,
     lng_ref, lnb_ref, wrh_ref, wrl_ref, br_ref) = refs[pos:pos + 15]
    pos += 15
    x1_ref, hs_ref, slots_ref, cnt_ref = refs[pos:pos + 4]

    i = pl.program_id(1)
    m = mod_ref[...]
    g1 = m[:, 2 * D_MODEL:3 * D_MODEL]
    sh2 = m[:, 3 * D_MODEL:4 * D_MODEL]
    sc2 = m[:, 4 * D_MODEL:5 * D_MODEL]

    lg = _log_sigmoid(lg_ref[...])
    head_of_lane = _iota((1, RET_W), 1) >> 6
    q = rq_ref[...]
    k = rk_ref[...]
    v = rv_ref[...]
    dist = (_iota((TILE, TILE), 0) - _iota((TILE, TILE), 1)).astype(F32)
    adist = jnp.abs(dist)
    diag2 = jnp.where(dist == 0.0, 2.0, 1.0)
    ret_o = jnp.zeros((TILE, RET_W), F32)
    for hd in range(H_RET):
        in_head = head_of_lane == hd
        sc = _dot_nt(jnp.where(in_head, q, 0), k)
        lgf = lg[0:1, hd * RET_DK:hd * RET_DK + 1]
        lgb = lg[1:2, hd * RET_DK:hd * RET_DK + 1]
        decay = jnp.exp(adist * jnp.where(dist > 0.0, lgf, lgb)) * diag2
        ret_o = ret_o + jnp.where(in_head, _dot((sc * decay).astype(BF), v), 0.0)
    p = _iota((TILE, 1), 0).astype(F32)
    ret_o = ret_o + _dot(q, sf_ref[...].astype(BF)) * jnp.exp((p + 1.0) * lg[0:1])
    ret_o = ret_o + _dot(q, sb_ref[...].astype(BF)) * jnp.exp((float(TILE) - p) * lg[1:2])
    avg = jnp.where((_iota((RET_W, RET_W), 0) >> 6) == (_iota((RET_W, RET_W), 1) >> 6),
                    1.0 / RET_DK, 0.0).astype(BF)
    rc = ret_o - _dot_hl(ret_o, avg)
    ret = rc * lax.rsqrt(_dot_hl(rc * rc, avg) + EPS) * _silu(rg_ref[...])

    lp = lamp_ref[...]
    lam = (jnp.exp(jnp.sum(lp[0:1] * lp[1:2], axis=-1, keepdims=True))
           - jnp.exp(jnp.sum(lp[2:3] * lp[3:4], axis=-1, keepdims=True)) + lam_init)
    subg = subg_ref[...] * (1.0 - lam_init)
    head_groups = [[hd] for hd in range(H_DIFF)] if cached else [list(range(H_DIFF))]
    heads = []
    for group in head_groups:
        for o in _diff_heads(dq_ref, kv_refs, group, lam):
            o = o * lax.rsqrt(jnp.mean(o * o, axis=-1, keepdims=True) + EPS) * subg
            heads.append(o.astype(BF))
    diff = jnp.concatenate(heads, axis=1)

    cch = cch_ref[...]
    prev = jnp.where(i > 0, cprev_ref[7:8, :], 0.0)
    nxt = jnp.where(i < n_tiles - 1, cnext_ref[0:1, :], 0.0)
    r = _iota((TILE, 1), 0)
    up = jnp.where(r == 0, prev, pltpu.roll(cch, 1, axis=0))
    dn = jnp.where(r == TILE - 1, nxt, pltpu.roll(cch, TILE - 1, axis=0))
    cw = cw_ref[...]
    conv = cb_ref[...] * (up * cw[0:1] + cch * cw[1:2] + dn * cw[2:3] + cbias_ref[...])

    mix = (_dot(ret.astype(BF), wout_ref[0:RET_W, :])
           + _dot(diff, wout_ref[RET_W:RET_W + DIFF_W, :])
           + _dot(conv.astype(BF), wout_ref[RET_W + DIFF_W:, :]))
    x1 = _layer_norm(ALPHA * x_ref[...] + g1 * mix, lng_ref[...], lnb_ref[...])
    x1_ref[...] = x1

    h2 = x1 * (1.0 + sc2) + sh2
    h2h, h2l = _split(h2)
    wrh = wrh_ref[...]
    logits = (_dot_nt(wrh, h2h) + (_dot_nt(wrh, h2l) + _dot_nt(wrl_ref[...], h2h))) + br_ref[...]
    neg = -jnp.inf
    g_id = _iota((EXPERT_ROW0, TILE), 0)
    gl = jnp.where(g_id < N_GROUPS, logits[0:EXPERT_ROW0], neg)
    gmax = jnp.max(gl, axis=0, keepdims=True)
    g_idx = jnp.min(jnp.where(gl == gmax, g_id, N_GROUPS), axis=0, keepdims=True)
    g_w = 1.0 / jnp.sum(jnp.exp(gl - gmax), axis=0, keepdims=True)
    e_id = _iota((N_EXPERTS, TILE), 0)
    el = jnp.where((e_id >> 2) == g_idx, logits[EXPERT_ROW0:EXPERT_ROW0 + N_EXPERTS], neg)
    v1 = jnp.max(el, axis=0, keepdims=True)
    i1 = jnp.min(jnp.where(el == v1, e_id, N_EXPERTS), axis=0, keepdims=True)
    el2 = jnp.where(e_id == i1, neg, el)
    v2 = jnp.max(el2, axis=0, keepdims=True)
    i2 = jnp.min(jnp.where(el2 == v2, e_id, N_EXPERTS), axis=0, keepdims=True)
    t = jnp.exp(v2 - v1)
    w1 = g_w / (1.0 + t)
    w2 = g_w * t / (1.0 + t)

    sel1 = e_id == i1
    sel2 = e_id == i2
    sel = jnp.where(sel1 | sel2, 1.0, 0.0)
    earlier = _ones_where(_iota((TILE, TILE), 0) < _iota((TILE, TILE), 1))
    rank = _dot(sel.astype(BF), earlier)
    cnt = jnp.sum(sel, axis=1, keepdims=True).astype(I32)
    padded = ((cnt + (CHUNK - 1)) >> 4) << 4
    incl = jnp.broadcast_to(padded, (N_EXPERTS, LANES))
    e_row = _iota((N_EXPERTS, LANES), 0)
    for step in (1, 2, 4, 8):
        incl = incl + jnp.where(e_row >= step, pltpu.roll(incl, step, axis=0), 0)
    seg_off = (incl[:, 0:1] - padded).astype(F32)
    spos = seg_off + rank
    slot1 = jnp.sum(jnp.where(sel1, spos, 0.0), axis=0, keepdims=True).astype(I32)
    slot2 = jnp.sum(jnp.where(sel2, spos, 0.0), axis=0, keepdims=True).astype(I32)
    s_id = _iota((CAP, TILE), 0)
    at1 = s_id == slot1
    at2 = s_id == slot2
    hs_ref[:, 0:D_MODEL] = _dot(_ones_where(at1 | at2), h2h).astype(BF)
    w1h = w1.astype(BF).astype(F32)
    w2h = w2.astype(BF).astype(F32)
    ones = jnp.ones((TILE, LANES), BF)
    g_hi = _dot((jnp.where(at1, w1h, 0.0) + jnp.where(at2, w2h, 0.0)).astype(BF), ones)
    g_lo = _dot((jnp.where(at1, w1 - w1h, 0.0) + jnp.where(at2, w2 - w2h, 0.0)).astype(BF), ones)
    lane = _iota((CAP, LANES), 1)
    hs_ref[:, D_MODEL:] = jnp.where(lane == 0, g_hi, jnp.where(lane == 1, g_lo, 0.0)).astype(BF)
    cnt_ref[...] = jnp.broadcast_to(padded, (N_EXPERTS, LANES))

    dr = _iota((LANES, TILE), 0)
    digits = jnp.where(dr == 0, slot1 & (SLOT_RADIX - 1), jnp.where(dr == 1, slot1 >> 5,
             jnp.where(dr == 2, slot2 & (SLOT_RADIX - 1), jnp.where(dr == 3, slot2 >> 5, 0))))
    eye = _ones_where(_iota((TILE, TILE), 0) == _iota((TILE, TILE), 1))
    cols = _dot_nt(eye, digits.astype(F32).astype(BF))
    s1c = (cols[:, 0:1] + SLOT_RADIX * cols[:, 1:2]).astype(I32)
    s2c = (cols[:, 2:3] + SLOT_RADIX * cols[:, 3:4]).astype(I32)
    tl = _iota((TILE, LANES), 1)
    slots_ref[...] = jnp.where(tl == 0, s1c, jnp.where(tl == 1, s2c, 0))


def _mixer(x, mod, layer, row_fn, pr, states, cache_kv, new_kv, w_out_b, lg_lanes, lamp, subg, cw, cbias,
           lng, lnb, wrh, wrl, br, lam_init):
    S, T, _ = x.shape
    n = T // TILE
    rq, rk, rv, rg, dq, cb, cch = pr
    t8 = TILE // 8
    cached = cache_kv is not None
    n_sub = CTX_SEQS_PER_STEP if (n == 1 and S % CTX_SEQS_PER_STEP == 0) else 1
    lead = None if n_sub == 1 else n_sub
    SEQ, SHARED, ROWS = 1, 0, 2

    def tok(w):
        return pl.BlockSpec((lead, TILE, w), lambda s, i: (s, i, 0))

    def full(shape):
        return pl.BlockSpec(shape, lambda s, i: (0,) * len(shape))

    def seq(a):
        mode = pl.Buffered(1) if n > 1 else None
        return pl.BlockSpec((lead,) + a.shape[1:], lambda s, i: (s, 0, 0), pipeline_mode=mode)

    if states is None:
        zero_state = jnp.zeros((1, 1, RET_W, RET_W), F32)
        states = (zero_state, zero_state)
        state = pl.BlockSpec((None, None, RET_W, RET_W), lambda s, i: (0, 0, 0, 0))
    else:
        assert n_sub == 1
        state = pl.BlockSpec((None, None, RET_W, RET_W), lambda s, i: (s, i, 0, 0))
    halo_prev = pl.BlockSpec((lead, 8, CONV_CH), lambda s, i: (s, jnp.maximum(i * t8 - 1, 0), 0))
    halo_next = pl.BlockSpec((lead, 8, CONV_CH), lambda s, i: (s, jnp.minimum((i + 1) * t8, T // 8 - 1), 0))
    in_specs = [tok(D_MODEL), _mod_spec(layer, row_fn), tok(RET_W), tok(RET_W), tok(RET_W), tok(RET_W),
                state, state, tok(DIFF_W)]
    split = [SEQ, SHARED, SEQ, SEQ, SEQ, SEQ, SHARED, SHARED, SEQ]
    args = [x, mod, rq, rk, rv, rg, states[0], states[1], dq]
    if cached:
        in_specs += [seq(cache_kv[0]), seq(cache_kv[1])]
        split += [SEQ, SEQ]
        args += list(cache_kv)
    in_specs += [seq(new_kv[0]), seq(new_kv[1])]
    split += [SEQ, SEQ]
    args += list(new_kv)
    in_specs += [
        tok(CONV_CH), tok(CONV_CH), halo_prev, halo_next,
        pl.BlockSpec((None, D_MODEL, D_MODEL), lambda s, i: (layer, 0, 0)),
        full((2, RET_W)), full((4, DIFF_QK)), full((1, DIFF_V)),
        full((3, CONV_CH)), full((1, CONV_CH)), full((1, D_MODEL)), full((1, D_MODEL)),
        full((ROUTER_ROWS, D_MODEL)), full((ROUTER_ROWS, D_MODEL)), full((ROUTER_ROWS, 1)),
    ]
    split += [SEQ] * 4 + [SHARED] * 11
    args += [cb, cch, cch, cch, w_out_b, lg_lanes, lamp, subg, cw, cbias, lng, lnb, wrh, wrl, br]
    split += [SEQ, ROWS, SEQ, SEQ]
    return pl.pallas_call(
        functools.partial(_mixer_kernel, n_sub=n_sub, split=tuple(split), n_tiles=n, lam_init=lam_init,
                          cached=cached),
        grid=(S // n_sub, n),
        in_specs=in_specs,
        out_specs=[tok(D_MODEL),
                   pl.BlockSpec((n_sub * CAP, HS_W), lambda s, i: (s * n + i, 0)),
                   tok(LANES),
                   pl.BlockSpec((lead, None, N_EXPERTS, LANES), lambda s, i: (s, i, 0, 0))],
        out_shape=[jax.ShapeDtypeStruct((S, T, D_MODEL), F32),
                   jax.ShapeDtypeStruct((S * n * CAP, HS_W), BF),
                   jax.ShapeDtypeStruct((S, T, LANES), I32),
                   jax.ShapeDtypeStruct((S, n, N_EXPERTS, LANES), I32)],
        compiler_params=_params(("parallel", "parallel")),
        name="mixer_lat" if cached else "mixer_ctx",
    )(*args)


def _block_schedule(padded, chunks_a):
    NT = padded.shape[0]
    n_chunks = NT * CHUNKS_PER_TILE
    n_blocks = n_chunks // CPB + N_EXPERTS
    c16 = padded // CHUNK
    ends = jnp.cumsum(c16, axis=1)
    before = jnp.cumsum(c16, axis=0) - c16
    per_expert = jnp.sum(c16, axis=0)
    nb = (per_expert + (CPB - 1)) // CPB
    b_end = jnp.cumsum(nb)
    q = jnp.arange(CHUNKS_PER_TILE, dtype=I32)
    e = jnp.arange(N_EXPERTS, dtype=I32)
    key = jnp.sum((ends[:, None, :] <= q[None, :, None]).astype(I32), axis=-1)
    pos_e = ((b_end - nb) * CPB + before)[:, None, :] + (q[None, :, None] - (ends - c16)[:, None, :])
    pos = jnp.sum(jnp.where(key[..., None] == e, pos_e, 0), axis=-1)
    pos = jnp.where(key < N_EXPERTS, pos, -1).reshape(-1)
    match = pos[None, :] == jnp.arange(n_blocks * CPB, dtype=I32)[:, None]
    cid = jnp.sum(jnp.where(match, jnp.arange(n_chunks, dtype=I32)[None, :], 0), axis=-1)
    nv = jnp.sum(match.astype(I32).reshape(n_blocks, -1), axis=-1)
    first = match & (jnp.arange(n_chunks, dtype=I32) < chunks_a)[None, :]
    nv_a = jnp.sum(first.astype(I32).reshape(n_blocks, -1), axis=-1)
    b = jnp.arange(n_blocks, dtype=I32)
    eb = jnp.minimum(jnp.sum((b_end[None, :] <= b[:, None]).astype(I32), axis=-1), N_EXPERTS - 1)
    return eb.astype(I32), jnp.concatenate([nv, nv_a]).astype(I32), cid.astype(I32)


def _moe_kernel(eb_ref, nv_ref, cid_ref, hs_a, hs_b, wg_ref, wu_ref, wd_ref, ys_a, ys_b,
                xbuf, ybuf, wg_b, wu_b, wd_b, in_sem, out_sem, *, n_blocks, chunks_a):
    b = pl.program_id(0)

    def chunk_rows(c):
        return pl.ds(pl.multiple_of(c * CHUNK, CHUNK), CHUNK)

    def gather_copy(which, c, j, slot):
        src = (hs_a, hs_b)[which]
        return pltpu.make_async_copy(src.at[chunk_rows(c)], xbuf.at[slot, chunk_rows(j)], in_sem.at[slot])

    def scatter_copy(which, c, j, slot):
        dst = (ys_a, ys_b)[which]
        return pltpu.make_async_copy(ybuf.at[slot, chunk_rows(j)],
                                     dst.at[chunk_rows(c), pl.ds(0, D_MODEL)], out_sem.at[slot])

    def for_chunks(blk, fn):
        n_a = nv_ref[n_blocks + blk]

        def body_a(j, carry):
            fn(j, 0, cid_ref[blk * CPB + j])
            return carry

        def body_b(j, carry):
            fn(j, 1, cid_ref[blk * CPB + j] - chunks_a)
            return carry
        lax.fori_loop(0, n_a, body_a, 0)
        lax.fori_loop(n_a, nv_ref[blk], body_b, 0)

    def wait_gathers(blk, slot):
        rows = pl.ds(0, nv_ref[blk] * CHUNK)

        @pl.when(nv_ref[blk] > 0)
        def _():
            pltpu.make_async_copy(hs_a.at[rows], xbuf.at[slot, rows], in_sem.at[slot]).wait()

    def wait_scatters(blk, slot):
        rows = pl.ds(0, nv_ref[blk] * CHUNK)

        @pl.when(nv_ref[blk] > 0)
        def _():
            pltpu.make_async_copy(ybuf.at[slot, rows], ys_a.at[rows, pl.ds(0, D_MODEL)], out_sem.at[slot]).wait()

    slot = lax.rem(b, 2)
    other = 1 - slot

    @pl.when(b == 0)
    def _():
        xbuf[...] = jnp.zeros_like(xbuf)
        for_chunks(0, lambda j, w, c: gather_copy(w, c, j, 0).start())

    @pl.when(b + 1 < n_blocks)
    def _():
        for_chunks(b + 1, lambda j, w, c: gather_copy(w, c, j, other).start())

    @pl.when((b == 0) | (eb_ref[b] != eb_ref[jnp.maximum(b - 1, 0)]))
    def _():
        wg_b[...] = wg_ref[...].astype(BF)
        wu_b[...] = wu_ref[...].astype(BF)
        wd_b[...] = wd_ref[...].astype(BF)

    wait_gathers(b, slot)

    @pl.when(b >= 2)
    def _():
        wait_scatters(b - 2, slot)

    @pl.when(nv_ref[b] > 0)
    def _():
        x = xbuf[slot, :, 0:D_MODEL]
        gate = (xbuf[slot, :, D_MODEL:D_MODEL + 1].astype(F32)
                + xbuf[slot, :, D_MODEL + 1:D_MODEL + 2].astype(F32))
        hg = _dot(x, wg_b[...])
        hu = _dot(x, wu_b[...])
        act = _silu(hg) * hu * gate
        ybuf[slot] = _dot(act.astype(BF), wd_b[...]).astype(BF)

    for_chunks(b, lambda j, w, c: scatter_copy(w, c, j, slot).start())

    @pl.when(b == n_blocks - 1)
    def _():
        if n_blocks >= 2:
            wait_scatters(b - 1, other)
        wait_scatters(b, slot)


def _moe(hs_a, hs_b, eb, nv, cid, layer, w_gate, w_up, w_down):
    n_blocks = eb.shape[0]

    def w_spec(a, c):
        return pl.BlockSpec((None, None, a, c), lambda b, eb, nv, cid: (layer, eb[b], 0, 0))

    grid_spec = pltpu.PrefetchScalarGridSpec(
        num_scalar_prefetch=3,
        grid=(n_blocks,),
        in_specs=[_HBM, _HBM, w_spec(D_MODEL, D_EXPERT), w_spec(D_MODEL, D_EXPERT), w_spec(D_EXPERT, D_MODEL)],
        out_specs=[_HBM, _HBM],
        scratch_shapes=[pltpu.VMEM((2, MOE_TMB, HS_W), BF), pltpu.VMEM((2, MOE_TMB, D_MODEL), BF),
                        pltpu.VMEM((D_MODEL, D_EXPERT), BF), pltpu.VMEM((D_MODEL, D_EXPERT), BF),
                        pltpu.VMEM((D_EXPERT, D_MODEL), BF),
                        pltpu.SemaphoreType.DMA((2,)), pltpu.SemaphoreType.DMA((2,))],
    )
    return pl.pallas_call(
        functools.partial(_moe_kernel, n_blocks=n_blocks, chunks_a=hs_a.shape[0] // CHUNK),
        grid_spec=grid_spec,
        out_shape=[jax.ShapeDtypeStruct(hs_a.shape, BF), jax.ShapeDtypeStruct(hs_b.shape, BF)],
        input_output_aliases={3: 0, 4: 1},
        compiler_params=_params(("arbitrary",)),
        name="moe_sorted",
    )(eb, nv, cid, hs_a, hs_b, w_gate, w_up, w_down)


def _combine_kernel(ys_ref, slots_ref, x1_ref, mod_ref, lng_ref, lnb_ref, o_ref):
    g2 = mod_ref[:, 5 * D_MODEL:6 * D_MODEL]
    s_lane = _iota((TILE, CAP), 1)
    for u in range(COMBINE_TILES_PER_STEP):
        sl = slots_ref[u]
        pick = _ones_where((s_lane == sl[:, 0:1]) | (s_lane == sl[:, 1:2]))
        ffn = _dot(pick, ys_ref[u * CAP:(u + 1) * CAP, 0:D_MODEL])
        o_ref[u] = _layer_norm(ALPHA * x1_ref[u] + g2 * ffn, lng_ref[...], lnb_ref[...])


def _combine(ys, slots, x1, mod, layer, row_fn, lng, lnb):
    S, T, _ = x1.shape
    n = T // TILE
    U = COMBINE_TILES_PER_STEP
    assert (S * n) % U == 0 and (n % U == 0 or n == 1)

    def tok(w):
        return pl.BlockSpec((U, TILE, w), lambda t: (t, 0, 0))

    out = pl.pallas_call(
        _combine_kernel,
        grid=(S * n // U,),
        in_specs=[
            pl.BlockSpec((U * CAP, HS_W), lambda t: (t, 0)),
            tok(LANES), tok(D_MODEL),
            pl.BlockSpec((None, None, 1, 6 * D_MODEL), lambda t: (layer, row_fn(t * U // n), 0, 0)),
            pl.BlockSpec((1, D_MODEL), lambda t: (0, 0)),
            pl.BlockSpec((1, D_MODEL), lambda t: (0, 0)),
        ],
        out_specs=tok(D_MODEL),
        out_shape=jax.ShapeDtypeStruct((S * n, TILE, D_MODEL), F32),
        compiler_params=_params(("parallel",)),
        name="moe_combine",
    )(ys, slots.reshape(S * n, TILE, LANES), x1.reshape(S * n, TILE, D_MODEL), mod, lng, lnb)
    return out.reshape(S, T, D_MODEL)


def _rope_tables(n_lat):
    half = DIFF_QK // 2
    pairs = half // 2
    inv = (1.0 / (ROPE_BASE ** (np.arange(pairs, dtype=np.float32) * 2.0 / half))).astype(np.float32)
    t = np.arange(n_lat)
    ang_r = ((t // GRID_W).astype(np.float32)[:, None] * inv[None, :]).astype(np.float64)
    ang_c = ((t % GRID_W).astype(np.float32)[:, None] * inv[None, :]).astype(np.float64)
    cos = np.concatenate([np.cos(ang_r)] * 2 + [np.cos(ang_c)] * 2, axis=1)
    sin = np.concatenate([-np.sin(ang_r), np.sin(ang_r), -np.sin(ang_c), np.sin(ang_c)], axis=1)
    reps = DIFF_W // DIFF_QK
    return (jnp.asarray(np.tile(cos, (1, reps)), F32), jnp.asarray(np.tile(sin, (1, reps)), F32))


def _block_diag(s):
    S = s.shape[0]
    eye = jnp.eye(H_RET, dtype=s.dtype)
    return jnp.einsum('shdv,hg->shdgv', s, eye).reshape(S, RET_W, RET_W)


def _router_rows(w_group, b_group, w_expert, b_expert):
    pad = EXPERT_ROW0 - N_GROUPS
    tail = ROUTER_ROWS - EXPERT_ROW0 - N_EXPERTS
    w = jnp.concatenate([w_group.T, jnp.zeros((pad, D_MODEL), F32),
                         w_expert.reshape(D_MODEL, N_EXPERTS).T, jnp.zeros((tail, D_MODEL), F32)], axis=0)
    bias = jnp.concatenate([b_group, jnp.zeros((pad,), F32), b_expert.reshape(N_EXPERTS), jnp.zeros((tail,), F32)])
    hi = w.astype(BF)
    return hi, (w - hi.astype(F32)).astype(BF), bias.reshape(ROUTER_ROWS, 1)


def kernel(x_prompt, x_sample, cache_diff_k, cache_diff_v, state_ret_fwd, state_ret_bwd, c, c_ctx, w_mod, b_mod, w_in, ret_decay_logit, diff_lambda, diff_subln_g, conv_w, conv_b, w_out, ln_g, ln_b, w_router_group, b_router_group, w_router_expert, b_router_expert, w_gate, w_up, w_down):
    B, T_ctx, _ = x_prompt.shape
    Bd, T_lat, _ = x_sample.shape
    assert T_ctx == TILE and T_lat % TILE == 0 and T_lat % GRID_W == 0
    assert 1 + Bd <= MOD_ROWS

    cond = jnp.concatenate([c_ctx[None, :], c, jnp.zeros((MOD_ROWS - 1 - Bd, D_MODEL), F32)], axis=0)
    mod = _modulation(cond, w_mod, b_mod).reshape(DEPTH, MOD_ROWS, 1, 6 * D_MODEL)
    rope_tabs = _rope_tables(T_lat)
    w_in_b = w_in.astype(BF)
    w_out_b = w_out.astype(BF)

    ctx_row = lambda s: 0
    lat_row = lambda s: s + 1
    ctx_tiles = B * (T_ctx // TILE)
    lat_tiles = Bd * (T_lat // TILE)

    yp, ys = x_prompt, x_sample
    stacked = None
    for l in range(DEPTH):
        lam_init = 0.8 - 0.6 * math.exp(-0.3 * l)
        lg_lanes = jnp.repeat(ret_decay_logit[l], RET_DK, axis=1)
        wrh, wrl, br = _router_rows(w_router_group[l], b_router_group[l], w_router_expert[l], b_router_expert[l])
        shared = (w_out_b, lg_lanes, diff_lambda[l], diff_subln_g[l].reshape(1, DIFF_V), conv_w[l],
                  conv_b[l].reshape(1, CONV_CH), ln_g[l, 0].reshape(1, D_MODEL), ln_b[l, 0].reshape(1, D_MODEL),
                  wrh, wrl, br, lam_init)
        ln2 = (ln_g[l, 1].reshape(1, D_MODEL), ln_b[l, 1].reshape(1, D_MODEL))

        pr = _proj(yp, mod, l, ctx_row, w_in_b, lg_lanes, stacked=stacked)
        rq, rk, rv, rg, dq, dk, dv, cb, cch = pr[:9]
        stacked = pr[9:]
        x1_c, hs_c, slots_c, cnt_c = _mixer(yp, mod, l, ctx_row, (rq, rk, rv, rg, dq, cb, cch), None, None,
                                            (dk, dv), *shared)

        pr = _proj(ys, mod, l, lat_row, w_in_b, lg_lanes, rope_tabs=rope_tabs)
        rq, rk, rv, rg, dq, dk, dv, cb, cch, kvf, kvb = pr
        states = _scan(kvf, kvb, _block_diag(state_ret_fwd[:, l]), _block_diag(state_ret_bwd[:, l]), lg_lanes)
        cache_v = cache_diff_v[:, l].astype(BF)
        cache_kv = (cache_diff_k[:, l].reshape(Bd, -1, DIFF_W).astype(BF),
                    jnp.concatenate([cache_v, jnp.ones_like(cache_v)], axis=-1).reshape(Bd, -1, V_AUG_W))
        x1_l, hs_l, slots_l, cnt_l = _mixer(ys, mod, l, lat_row, (rq, rk, rv, rg, dq, cb, cch), states, cache_kv,
                                            (dk, dv), *shared)

        padded = jnp.concatenate([cnt_c.reshape(ctx_tiles, N_EXPERTS, LANES)[:, :, 0],
                                  cnt_l.reshape(lat_tiles, N_EXPERTS, LANES)[:, :, 0]], axis=0)
        eb, nv, cid = _block_schedule(padded, ctx_tiles * CHUNKS_PER_TILE)
        out_c, out_l = _moe(hs_c, hs_l, eb, nv, cid, l, w_gate, w_up, w_down)
        yp = _combine(out_c, slots_c, x1_c, mod, l, ctx_row, *ln2)
        ys = _combine(out_l, slots_l, x1_l, mod, l, lat_row, *ln2)

    new_k, new_v, st_f, st_b = stacked
    return (yp, ys, new_k.reshape(B, DEPTH, T_ctx, H_DIFF, 2, DIFF_QK),
            new_v.reshape(B, DEPTH, T_ctx, H_DIFF, DIFF_V), st_f, st_b)
```

```python
import functools
import math

import numpy as np
import jax
import jax.numpy as jnp
from jax import lax
from jax.experimental import pallas as pl
from jax.experimental.pallas import tpu as pltpu

D_MODEL = 1024
DEPTH = 2
GRID_W = 64
H_RET = 4
RET_DK = 64
RET_W = H_RET * RET_DK
H_DIFF = 4
DIFF_QK = 64
DIFF_V = 2 * DIFF_QK
DIFF_W = H_DIFF * DIFF_V
CONV_CH = 256
ROPE_BASE = 10000.0
N_GROUPS = 4
EXPERTS_PER_GROUP = 4
N_EXPERTS = N_GROUPS * EXPERTS_PER_GROUP
D_EXPERT = 512
ALPHA = (2 * DEPTH) ** 0.25
EPS = 1e-5
MOD_ROWS = 8
LANES = 128
ROUTER_ROWS = 32
EXPERT_ROW0 = 8

TILE = 256
CHUNK = 16
CAP = 2 * TILE + N_EXPERTS * CHUNK
CHUNKS_PER_TILE = CAP // CHUNK
CTX_SEQS_PER_STEP = 2
COMBINE_TILES_PER_STEP = 2
MOE_TMB = 256
CPB = MOE_TMB // CHUNK
HS_W = D_MODEL + LANES
V_AUG_W = 2 * DIFF_W
QK_SCALE_LOG2 = (DIFF_QK ** -0.5) * math.log2(math.e)
SLOT_RADIX = 32

_O_RQ, _O_RK, _O_RV, _O_RG = 0, 256, 512, 768
_O_DQ, _O_DK, _O_DV = 1024, 1536, 2048
_O_CB, _O_CC, _O_CH = 2560, 2816, 3072
IN_WIDTH = 3328

BF = jnp.bfloat16
F32 = jnp.float32
I32 = jnp.int32

_VMEM_LIMIT = 56 * 1024 * 1024


def _dot(a, b):
    return jnp.dot(a, b, preferred_element_type=F32)


def _dot_nt(a, b):
    return lax.dot_general(a, b, (((1,), (1,)), ((), ())), preferred_element_type=F32)


def _dot_tn(a, b):
    return lax.dot_general(a, b, (((0,), (0,)), ((), ())), preferred_element_type=F32)


def _split(x):
    hi = x.astype(BF)
    lo = (x - hi.astype(F32)).astype(BF)
    return hi, lo


def _dot_hl(x, w_bf16):
    hi, lo = _split(x)
    return _dot(hi, w_bf16) + _dot(lo, w_bf16)


def _dot3(x, w_hi, w_lo):
    hi, lo = _split(x)
    return _dot(hi, w_hi) + (_dot(lo, w_hi) + _dot(hi, w_lo))


def _iota(shape, dim):
    return lax.broadcasted_iota(I32, shape, dim)


def _ones_where(cond):
    return jnp.where(cond, 1.0, 0.0).astype(BF)


def _log_sigmoid(x):
    return jnp.minimum(x, 0.0) - jnp.log1p(jnp.exp(-jnp.abs(x)))


def _silu(x):
    return x * jax.nn.sigmoid(x)


def _layer_norm(y, g, b):
    mu = jnp.mean(y, axis=-1, keepdims=True)
    yc = y - mu
    var = jnp.mean(yc * yc, axis=-1, keepdims=True)
    return yc * lax.rsqrt(var + EPS) * g + b


def _params(sem):
    return pltpu.CompilerParams(dimension_semantics=sem, vmem_limit_bytes=_VMEM_LIMIT)


_HBM = pl.BlockSpec(memory_space=pltpu.HBM)


def _mod_kernel(c_ref, w_ref, b_ref, o_ref):
    a = _silu(c_ref[...])
    w_hi, w_lo = _split(w_ref[...])
    o_ref[...] = _dot3(a, w_hi, w_lo) + b_ref[...]


def _modulation(cond, w_mod, b_mod):
    tn = 1536
    n6 = 6 * D_MODEL
    return pl.pallas_call(
        _mod_kernel,
        grid=(DEPTH, n6 // tn),
        in_specs=[
            pl.BlockSpec((MOD_ROWS, D_MODEL), lambda l, j: (0, 0)),
            pl.BlockSpec((None, D_MODEL, tn), lambda l, j: (l, 0, j)),
            pl.BlockSpec((None, 1, tn), lambda l, j: (l, 0, j)),
        ],
        out_specs=pl.BlockSpec((None, MOD_ROWS, tn), lambda l, j: (l, 0, j)),
        out_shape=jax.ShapeDtypeStruct((DEPTH, MOD_ROWS, n6), F32),
        compiler_params=_params(("parallel", "parallel")),
        name="modulation",
    )(cond, w_mod, b_mod.reshape(DEPTH, 1, n6))


def _mod_spec(layer, row_fn):
    return pl.BlockSpec((None, None, 1, 6 * D_MODEL), lambda s, i: (layer, row_fn(s), 0, 0))


def _augment_v(v):
    ones = jnp.ones((v.shape[0], DIFF_V), v.dtype)
    parts = []
    for hd in range(H_DIFF):
        parts += [v[:, hd * DIFF_V:(hd + 1) * DIFF_V], ones]
    return jnp.concatenate(parts, axis=1)


def _swap16(x):
    n = x.shape[-1]
    lane = _iota(x.shape, 1)
    return jnp.where((lane & 16) == 0, pltpu.roll(x, n - 16, axis=1), pltpu.roll(x, 16, axis=1))


def _proj_kernel(*refs, latent, n_aliased, layer):
    x_ref, mod_ref, w_ref, lg_ref = refs[:4]
    pos = 4
    if latent:
        cos_ref, sin_ref = refs[pos:pos + 2]
        pos += 2
    pos += n_aliased
    rq_ref, rk_ref, rv_ref, rg_ref, dq_ref, dk_ref, dv_ref, cb_ref, cch_ref = refs[pos:pos + 9]
    pos += 9
    if latent:
        kvf_ref, kvb_ref = refs[pos:pos + 2]
    else:
        dk32_ref, dv32_ref, stf_ref, stb_ref = refs[pos:pos + 4]

    def put(ref, idx, val):
        if n_aliased:
            ref[idx if idx else ...] = val
        else:
            for other in range(DEPTH):
                ref[(other,) + idx] = val if other == layer else jnp.zeros_like(val)

    m = mod_ref[...]
    sh1 = m[:, 0:D_MODEL]
    sc1 = m[:, D_MODEL:2 * D_MODEL]
    h = (x_ref[...] * (1.0 + sc1) + sh1).astype(BF)

    def col(off, width):
        return _dot(h, w_ref[:, off:off + width])

    rq_ref[...] = col(_O_RQ, RET_W).astype(BF)
    rk = col(_O_RK, RET_W) * (RET_DK ** -0.5)
    rk_ref[...] = rk.astype(BF)
    rv = col(_O_RV, RET_W).astype(BF)
    rv_ref[...] = rv
    rg_ref[...] = col(_O_RG, RET_W)

    lg = _log_sigmoid(lg_ref[...])
    p = _iota((TILE, 1), 0).astype(F32)
    kf = (rk * jnp.exp((TILE - 1.0 - p) * lg[0:1])).astype(BF)
    kb = (rk * jnp.exp(p * lg[1:2])).astype(BF)
    kvf = _dot_tn(kf, rv)
    kvb = _dot_tn(kb, rv)
    if latent:
        kvf_ref[...] = kvf
        kvb_ref[...] = kvb
    else:
        for hd in range(H_RET):
            lo = hd * RET_DK
            put(stf_ref, (hd,), kvf[lo:lo + RET_DK, lo:lo + RET_DK])
            put(stb_ref, (hd,), kvb[lo:lo + RET_DK, lo:lo + RET_DK])

    dq = col(_O_DQ, DIFF_W)
    dk = col(_O_DK, DIFF_W)
    dv = col(_O_DV, DIFF_W)
    if latent:
        cos = cos_ref[...]
        sin = sin_ref[...]
        dq = dq * cos + _swap16(dq) * sin
        dk = dk * cos + _swap16(dk) * sin
    else:
        put(dk32_ref, (), dk)
        put(dv32_ref, (), dv)
    dq_ref[...] = (dq * QK_SCALE_LOG2).astype(BF)
    dk_ref[...] = dk.astype(BF)
    dv_ref[...] = _augment_v(dv.astype(BF))

    cb_ref[...] = col(_O_CB, CONV_CH)
    cch_ref[...] = col(_O_CC, CONV_CH) * col(_O_CH, CONV_CH)


def _proj(x, mod, layer, row_fn, w_in_b, lg_lanes, rope_tabs=None, stacked=None):
    S, T, _ = x.shape
    n = T // TILE
    latent = rope_tabs is not None

    def tok(w):
        return pl.BlockSpec((None, TILE, w), lambda s, i: (s, i, 0))

    def tok_shape(w, dt):
        return jax.ShapeDtypeStruct((S, T, w), dt)

    in_specs = [
        tok(D_MODEL), _mod_spec(layer, row_fn),
        pl.BlockSpec((None, D_MODEL, IN_WIDTH), lambda s, i: (layer, 0, 0)),
        pl.BlockSpec((2, RET_W), lambda s, i: (0, 0)),
    ]
    args = [x, mod, w_in_b, lg_lanes]
    out_specs = [tok(RET_W)] * 4 + [tok(DIFF_W)] * 2 + [tok(V_AUG_W)] + [tok(CONV_CH)] * 2
    out_shape = ([tok_shape(RET_W, BF)] * 3 + [tok_shape(RET_W, F32)] + [tok_shape(DIFF_W, BF)] * 2
                 + [tok_shape(V_AUG_W, BF)]
                 + [tok_shape(CONV_CH, F32)] * 2)
    aliases = {}
    if latent:
        in_specs += [pl.BlockSpec((TILE, DIFF_W), lambda s, i: (i, 0))] * 2
        args += list(rope_tabs)
        out_specs += [pl.BlockSpec((None, None, RET_W, RET_W), lambda s, i: (s, i, 0, 0))] * 2
        out_shape += [jax.ShapeDtypeStruct((S, n, RET_W, RET_W), F32)] * 2
    else:
        assert n == 1
        kv_shape = jax.ShapeDtypeStruct((S, DEPTH, T, DIFF_W), F32)
        st_shape = jax.ShapeDtypeStruct((S, DEPTH, H_RET, RET_DK, RET_DK), F32)
        if stacked is not None:
            in_specs += [_HBM] * 4
            args += list(stacked)
            aliases = {len(args) - 4 + j: len(out_shape) + j for j in range(4)}
            out_specs += [pl.BlockSpec((None, None, TILE, DIFF_W), lambda s, i: (s, layer, i, 0))] * 2
            out_specs += [pl.BlockSpec((None, None, H_RET, RET_DK, RET_DK), lambda s, i: (s, layer, 0, 0, 0))] * 2
        else:
            out_specs += [pl.BlockSpec((None, DEPTH, TILE, DIFF_W), lambda s, i: (s, 0, i, 0))] * 2
            out_specs += [pl.BlockSpec((None, DEPTH, H_RET, RET_DK, RET_DK), lambda s, i: (s, 0, 0, 0, 0))] * 2
        out_shape += [kv_shape, kv_shape, st_shape, st_shape]
    return pl.pallas_call(
        functools.partial(_proj_kernel, latent=latent, n_aliased=len(aliases), layer=layer),
        grid=(S, n),
        in_specs=in_specs,
        out_specs=out_specs,
        out_shape=out_shape,
        input_output_aliases=aliases,
        compiler_params=_params(("parallel", "parallel")),
        name="proj_lat" if latent else "proj_ctx",
    )(*args)


def _scan_kernel(kvf_ref, kvb_ref, s0f_ref, s0b_ref, lg_ref, sf_ref, sb_ref, *, n):
    lg = _log_sigmoid(lg_ref[...])
    dec = jnp.exp(float(TILE) * lg)
    same_head = (_iota((RET_W, RET_W), 0) >> 6) == (_iota((RET_W, RET_W), 1) >> 6)

    s = jnp.where(same_head, s0f_ref[...], 0.0)
    for c in range(n):
        sf_ref[c] = s
        s = s * dec[0:1] + jnp.where(same_head, kvf_ref[c], 0.0)
    s = jnp.where(same_head, s0b_ref[...], 0.0)
    for c in reversed(range(n)):
        sb_ref[c] = s
        s = s * dec[1:2] + jnp.where(same_head, kvb_ref[c], 0.0)


def _scan(kvf, kvb, s0f, s0b, lg_lanes):
    S, n = kvf.shape[:2]
    chunks = pl.BlockSpec((None, n, RET_W, RET_W), lambda s: (s, 0, 0, 0))
    one = pl.BlockSpec((None, RET_W, RET_W), lambda s: (s, 0, 0))
    return pl.pallas_call(
        functools.partial(_scan_kernel, n=n),
        grid=(S,),
        in_specs=[chunks, chunks, one, one, pl.BlockSpec((2, RET_W), lambda s: (0, 0))],
        out_specs=[chunks, chunks],
        out_shape=[jax.ShapeDtypeStruct((S, n, RET_W, RET_W), F32)] * 2,
        compiler_params=_params(("parallel",)),
        name="ret_scan",
    )(kvf, kvb, s0f, s0b, lg_lanes)


def _diff_heads(dq_ref, kv_refs, heads, lam):
    lane = _iota((1, DIFF_V), 1)
    scores = []
    for hd in heads:
        lo = hd * DIFF_V
        q_h = dq_ref[:, lo:lo + DIFF_V]
        qs = jnp.concatenate([jnp.where(lane < DIFF_QK, q_h, 0), jnp.where(lane >= DIFF_QK, q_h, 0)], axis=0)
        scores.append(jnp.concatenate([_dot_nt(qs, k[:, lo:lo + DIFF_V]) for k, _ in kv_refs], axis=1))
    s = jnp.concatenate(scores, axis=0)
    p = jnp.exp2(s - jnp.max(s, axis=-1, keepdims=True)).astype(BF)
    outs = []
    for n, hd in enumerate(heads):
        acc = None
        start = 0
        for k, v in kv_refs:
            part = _dot(p[n * 2 * TILE:(n + 1) * 2 * TILE, start:start + k.shape[0]],
                        v[:, hd * 2 * DIFF_V:(hd + 1) * 2 * DIFF_V])
            acc = part if acc is None else acc + part
            start += k.shape[0]
        o0 = acc[0:TILE, 0:DIFF_V] * (1.0 / acc[0:TILE, DIFF_V:DIFF_V + 1])
        o1 = acc[TILE:, 0:DIFF_V] * (lam / acc[TILE:, DIFF_V:DIFF_V + 1])
        outs.append(o0 - o1)
    return outs


def _mixer_kernel(*refs, n_sub, split, **static):
    if n_sub == 1:
        return _mixer_tile(*refs, **static)
    for u in range(n_sub):
        views = [r.at[u] if how == 1 else (r.at[pl.ds(u * CAP, CAP)] if how == 2 else r)
                 for r, how in zip(refs, split)]
        _mixer_tile(*views, **static)


def _mixer_pipe_kernel(*refs, n_tiles, n_total, lam_init, cached):
    n_a = _n_token_mix_refs(cached)
    mix_even, mix_odd = refs[-2:]
    t = pl.program_id(0)
    tile_in_seq = lax.rem(jnp.minimum(t, n_total - 1), n_tiles)

    @pl.when(t == 0)
    def _():
        mix_odd[...] = jnp.zeros_like(mix_odd)

    def step(mix_write, mix_read):
        mix_write[...] = _token_mix(*refs[:n_a], tile_in_seq=tile_in_seq, n_tiles=n_tiles,
                                    lam_init=lam_init, cached=cached)
        _project_route(mix_read[...], *refs[n_a:-2])

    @pl.when(lax.rem(t, 2) == 0)
    def _():
        step(mix_even, mix_odd)

    @pl.when(lax.rem(t, 2) == 1)
    def _():
        step(mix_odd, mix_even)


def _n_token_mix_refs(cached):
    return 18 + (2 if cached else 0)


def _mixer_tile(*refs, n_tiles, lam_init, cached):
    n_a = _n_token_mix_refs(cached)
    mixed = _token_mix(*refs[:n_a], tile_in_seq=pl.program_id(1), n_tiles=n_tiles, lam_init=lam_init,
                       cached=cached)
    _project_route(mixed, *refs[n_a:])


def _token_mix(*refs, tile_in_seq, n_tiles, lam_init, cached):
    rq_ref, rk_ref, rv_ref, rg_ref, sf_ref, sb_ref, dq_ref = refs[:7]
    pos = 7
    kv_refs = []
    if cached:
        kv_refs.append((refs[pos], refs[pos + 1]))
        pos += 2
    kv_refs.append((refs[pos], refs[pos + 1]))
    pos += 2
    cb_ref, cch_ref, cprev_ref, cnext_ref, lg_ref, lamp_ref, subg_ref, cw_ref, cbias_ref = refs[pos:pos + 9]
    i = tile_in_seq

    lg = _log_sigmoid(lg_ref[...])
    head_of_lane = _iota((1, RET_W), 1) >> 6
    q = rq_ref[...]
    k = rk_ref[...]
    v = rv_ref[...]
    dist = (_iota((TILE, TILE), 0) - _iota((TILE, TILE), 1)).astype(F32)
    adist = jnp.abs(dist)
    diag2 = jnp.where(dist == 0.0, 2.0, 1.0)
    ret_o = jnp.zeros((TILE, RET_W), F32)
    for hd in range(H_RET):
        in_head = head_of_lane == hd
        sc = _dot_nt(jnp.where(in_head, q, 0), k)
        lgf = lg[0:1, hd * RET_DK:hd * RET_DK + 1]
        lgb = lg[1:2, hd * RET_DK:hd * RET_DK + 1]
        decay = jnp.exp(adist * jnp.where(dist > 0.0, lgf, lgb)) * diag2
        ret_o = ret_o + jnp.where(in_head, _dot((sc * decay).astype(BF), v), 0.0)
    p = _iota((TILE, 1), 0).astype(F32)
    ret_o = ret_o + _dot(q, sf_ref[...].astype(BF)) * jnp.exp((p + 1.0) * lg[0:1])
    ret_o = ret_o + _dot(q, sb_ref[...].astype(BF)) * jnp.exp((float(TILE) - p) * lg[1:2])
    avg = jnp.where((_iota((RET_W, RET_W), 0) >> 6) == (_iota((RET_W, RET_W), 1) >> 6),
                    1.0 / RET_DK, 0.0).astype(BF)
    rc = ret_o - _dot_hl(ret_o, avg)
    ret = rc * lax.rsqrt(_dot_hl(rc * rc, avg) + EPS) * _silu(rg_ref[...])

    lp = lamp_ref[...]
    lam = (jnp.exp(jnp.sum(lp[0:1] * lp[1:2], axis=-1, keepdims=True))
           - jnp.exp(jnp.sum(lp[2:3] * lp[3:4], axis=-1, keepdims=True)) + lam_init)
    subg = subg_ref[...] * (1.0 - lam_init)
    head_groups = [[hd] for hd in range(H_DIFF)] if cached else [list(range(H_DIFF))]
    heads = []
    for group in head_groups:
        for o in _diff_heads(dq_ref, kv_refs, group, lam):
            o = o * lax.rsqrt(jnp.mean(o * o, axis=-1, keepdims=True) + EPS) * subg
            heads.append(o.astype(BF))
    diff = jnp.concatenate(heads, axis=1)

    cch = cch_ref[...]
    prev = jnp.where(i > 0, cprev_ref[7:8, :], 0.0)
    nxt = jnp.where(i < n_tiles - 1, cnext_ref[0:1, :], 0.0)
    r = _iota((TILE, 1), 0)
    up = jnp.where(r == 0, prev, pltpu.roll(cch, 1, axis=0))
    dn = jnp.where(r == TILE - 1, nxt, pltpu.roll(cch, TILE - 1, axis=0))
    cw = cw_ref[...]
    conv = cb_ref[...] * (up * cw[0:1] + cch * cw[1:2] + dn * cw[2:3] + cbias_ref[...])
    return jnp.concatenate([ret.astype(BF), diff, conv.astype(BF)], axis=1)


def _project_route(mixed, x_ref, mod_ref, wout_ref, lng_ref, lnb_ref, wrh_ref, wrl_ref, br_ref,
                   x1_ref, hs_ref, slots_ref, cnt_ref):
    m = mod_ref[...]
    g1 = m[:, 2 * D_MODEL:3 * D_MODEL]
    sh2 = m[:, 3 * D_MODEL:4 * D_MODEL]
    sc2 = m[:, 4 * D_MODEL:5 * D_MODEL]
    x1 = _layer_norm(ALPHA * x_ref[...] + g1 * _dot(mixed, wout_ref[...]), lng_ref[...], lnb_ref[...])
    x1_ref[...] = x1

    h2 = x1 * (1.0 + sc2) + sh2
    h2h, h2l = _split(h2)
    wrh = wrh_ref[...]
    logits = (_dot_nt(wrh, h2h) + (_dot_nt(wrh, h2l) + _dot_nt(wrl_ref[...], h2h))) + br_ref[...]
    neg = -jnp.inf
    g_id = _iota((EXPERT_ROW0, TILE), 0)
    gl = jnp.where(g_id < N_GROUPS, logits[0:EXPERT_ROW0], neg)
    gmax = jnp.max(gl, axis=0, keepdims=True)
    g_idx = jnp.min(jnp.where(gl == gmax, g_id, N_GROUPS), axis=0, keepdims=True)
    g_w = 1.0 / jnp.sum(jnp.exp(gl - gmax), axis=0, keepdims=True)
    e_id = _iota((N_EXPERTS, TILE), 0)
    el = jnp.where((e_id >> 2) == g_idx, logits[EXPERT_ROW0:EXPERT_ROW0 + N_EXPERTS], neg)
    v1 = jnp.max(el, axis=0, keepdims=True)
    i1 = jnp.min(jnp.where(el == v1, e_id, N_EXPERTS), axis=0, keepdims=True)
    el2 = jnp.where(e_id == i1, neg, el)
    v2 = jnp.max(el2, axis=0, keepdims=True)
    i2 = jnp.min(jnp.where(el2 == v2, e_id, N_EXPERTS), axis=0, keepdims=True)
    t = jnp.exp(v2 - v1)
    w1 = g_w / (1.0 + t)
    w2 = g_w * t / (1.0 + t)

    sel1 = e_id == i1
    sel2 = e_id == i2
    sel = jnp.where(sel1 | sel2, 1.0, 0.0)
    earlier = _ones_where(_iota((TILE, TILE), 0) < _iota((TILE, TILE), 1))
    rank = _dot(sel.astype(BF), earlier)
    cnt = jnp.sum(sel, axis=1, keepdims=True).astype(I32)
    padded = ((cnt + (CHUNK - 1)) >> 4) << 4
    incl = jnp.broadcast_to(padded, (N_EXPERTS, LANES))
    e_row = _iota((N_EXPERTS, LANES), 0)
    for step in (1, 2, 4, 8):
        incl = incl + jnp.where(e_row >= step, pltpu.roll(incl, step, axis=0), 0)
    seg_off = (incl[:, 0:1] - padded).astype(F32)
    spos = seg_off + rank
    slot1 = jnp.sum(jnp.where(sel1, spos, 0.0), axis=0, keepdims=True).astype(I32)
    slot2 = jnp.sum(jnp.where(sel2, spos, 0.0), axis=0, keepdims=True).astype(I32)
    s_id = _iota((CAP, TILE), 0)
    at1 = s_id == slot1
    at2 = s_id == slot2
    hs_ref[:, 0:D_MODEL] = _dot(_ones_where(at1 | at2), h2h).astype(BF)
    w1h = w1.astype(BF).astype(F32)
    w2h = w2.astype(BF).astype(F32)
    ones = jnp.ones((TILE, LANES), BF)
    g_hi = _dot((jnp.where(at1, w1h, 0.0) + jnp.where(at2, w2h, 0.0)).astype(BF), ones)
    g_lo = _dot((jnp.where(at1, w1 - w1h, 0.0) + jnp.where(at2, w2 - w2h, 0.0)).astype(BF), ones)
    lane = _iota((CAP, LANES), 1)
    hs_ref[:, D_MODEL:] = jnp.where(lane == 0, g_hi, jnp.where(lane == 1, g_lo, 0.0)).astype(BF)
    cnt_ref[...] = jnp.broadcast_to(padded, (N_EXPERTS, LANES))

    dr = _iota((LANES, TILE), 0)
    digits = jnp.where(dr == 0, slot1 & (SLOT_RADIX - 1), jnp.where(dr == 1, slot1 >> 5,
             jnp.where(dr == 2, slot2 & (SLOT_RADIX - 1), jnp.where(dr == 3, slot2 >> 5, 0))))
    eye = _ones_where(_iota((TILE, TILE), 0) == _iota((TILE, TILE), 1))
    cols = _dot_nt(eye, digits.astype(F32).astype(BF))
    s1c = (cols[:, 0:1] + SLOT_RADIX * cols[:, 1:2]).astype(I32)
    s2c = (cols[:, 2:3] + SLOT_RADIX * cols[:, 3:4]).astype(I32)
    tl = _iota((TILE, LANES), 1)
    slots_ref[...] = jnp.where(tl == 0, s1c, jnp.where(tl == 1, s2c, 0))


def _mixer(x, mod, layer, row_fn, pr, states, cache_kv, new_kv, w_out_b, lg_lanes, lamp, subg, cw, cbias,
           lng, lnb, wrh, wrl, br, lam_init):
    S, T, _ = x.shape
    n = T // TILE
    rq, rk, rv, rg, dq, cb, cch = pr
    t8 = TILE // 8
    cached = cache_kv is not None
    pipelined = n > 1
    n_sub = CTX_SEQS_PER_STEP if (not pipelined and S % CTX_SEQS_PER_STEP == 0) else 1
    lead = None if n_sub == 1 else n_sub
    SEQ, SHARED, ROWS = 1, 0, 2

    if pipelined:
        last = S * n - 1
        grid = (S * n + 1,)
        at_mix = lambda t: (jnp.minimum(t, last) // n, jnp.minimum(t, last) % n)
        at_out = lambda t: (jnp.maximum(t - 1, 0) // n, jnp.maximum(t - 1, 0) % n)
    else:
        grid = (S // n_sub, n)
        at_mix = at_out = lambda s, i: (s, i)

    def tok(w, at):
        return pl.BlockSpec((lead, TILE, w), lambda *g: (*at(*g), 0))

    def full(shape):
        return pl.BlockSpec(shape, lambda *g: (0,) * len(shape))

    def seq(a):
        mode = pl.Buffered(1) if n > 1 else None
        return pl.BlockSpec((lead,) + a.shape[1:], lambda *g: (at_mix(*g)[0], 0, 0), pipeline_mode=mode)

    def halo(offset):
        def index(*g):
            s, i = at_mix(*g)
            return (s, jnp.clip(i * t8 + offset, 0, T // 8 - 1), 0)
        return pl.BlockSpec((lead, 8, CONV_CH), index)

    if states is None:
        zero_state = jnp.zeros((1, 1, RET_W, RET_W), F32)
        states = (zero_state, zero_state)
        state = pl.BlockSpec((None, None, RET_W, RET_W), lambda *g: (0, 0, 0, 0))
    else:
        assert n_sub == 1
        state = pl.BlockSpec((None, None, RET_W, RET_W), lambda *g: (*at_mix(*g), 0, 0))

    in_specs = [tok(RET_W, at_mix)] * 4 + [state, state, tok(DIFF_W, at_mix)]
    split = [SEQ] * 4 + [SHARED, SHARED, SEQ]
    args = [rq, rk, rv, rg, states[0], states[1], dq]
    for kv in ([cache_kv] if cached else []) + [new_kv]:
        in_specs += [seq(kv[0]), seq(kv[1])]
        split += [SEQ, SEQ]
        args += list(kv)
    in_specs += [tok(CONV_CH, at_mix), tok(CONV_CH, at_mix), halo(-1), halo(t8),
                 full((2, RET_W)), full((4, DIFF_QK)), full((1, DIFF_V)), full((3, CONV_CH)), full((1, CONV_CH))]
    split += [SEQ] * 4 + [SHARED] * 5
    args += [cb, cch, cch, cch, lg_lanes, lamp, subg, cw, cbias]
    assert len(args) == _n_token_mix_refs(cached)
    in_specs += [tok(D_MODEL, at_out),
                 pl.BlockSpec((None, None, 1, 6 * D_MODEL), lambda *g: (layer, row_fn(at_out(*g)[0]), 0, 0)),
                 pl.BlockSpec((None, D_MODEL, D_MODEL), lambda *g: (layer, 0, 0)),
                 full((1, D_MODEL)), full((1, D_MODEL)),
                 full((ROUTER_ROWS, D_MODEL)), full((ROUTER_ROWS, D_MODEL)), full((ROUTER_ROWS, 1))]
    split += [SEQ] + [SHARED] * 7
    args += [x, mod, w_out_b, lng, lnb, wrh, wrl, br]
    split += [SEQ, ROWS, SEQ, SEQ]

    def out_tile(*g):
        s, i = at_out(*g)
        return s * n + i

    static = dict(n_tiles=n, lam_init=lam_init, cached=cached)
    if pipelined:
        body = functools.partial(_mixer_pipe_kernel, n_total=S * n, **static)
        scratch = [pltpu.VMEM((TILE, D_MODEL), BF)] * 2
        semantics = ("arbitrary",)
    else:
        body = functools.partial(_mixer_kernel, n_sub=n_sub, split=tuple(split), **static)
        scratch = []
        semantics = ("parallel", "parallel")
    return pl.pallas_call(
        body,
        grid=grid,
        in_specs=in_specs,
        out_specs=[tok(D_MODEL, at_out),
                   pl.BlockSpec((n_sub * CAP, HS_W), lambda *g: (out_tile(*g), 0)),
                   tok(LANES, at_out),
                   pl.BlockSpec((lead, None, N_EXPERTS, LANES), lambda *g: (*at_out(*g), 0, 0))],
        out_shape=[jax.ShapeDtypeStruct((S, T, D_MODEL), F32),
                   jax.ShapeDtypeStruct((S * n * CAP, HS_W), BF),
                   jax.ShapeDtypeStruct((S, T, LANES), I32),
                   jax.ShapeDtypeStruct((S, n, N_EXPERTS, LANES), I32)],
        scratch_shapes=scratch,
        compiler_params=_params(semantics),
        name="mixer_lat" if cached else "mixer_ctx",
    )(*args)


def _block_schedule(padded, chunks_a):
    NT = padded.shape[0]
    n_chunks = NT * CHUNKS_PER_TILE
    n_blocks = n_chunks // CPB + N_EXPERTS
    c16 = padded // CHUNK
    ends = jnp.cumsum(c16, axis=1)
    before = jnp.cumsum(c16, axis=0) - c16
    per_expert = jnp.sum(c16, axis=0)
    nb = (per_expert + (CPB - 1)) // CPB
    b_end = jnp.cumsum(nb)
    q = jnp.arange(CHUNKS_PER_TILE, dtype=I32)
    e = jnp.arange(N_EXPERTS, dtype=I32)
    key = jnp.sum((ends[:, None, :] <= q[None, :, None]).astype(I32), axis=-1)
    pos_e = ((b_end - nb) * CPB + before)[:, None, :] + (q[None, :, None] - (ends - c16)[:, None, :])
    pos = jnp.sum(jnp.where(key[..., None] == e, pos_e, 0), axis=-1)
    pos = jnp.where(key < N_EXPERTS, pos, -1).reshape(-1)
    match = pos[None, :] == jnp.arange(n_blocks * CPB, dtype=I32)[:, None]
    cid = jnp.sum(jnp.where(match, jnp.arange(n_chunks, dtype=I32)[None, :], 0), axis=-1)
    nv = jnp.sum(match.astype(I32).reshape(n_blocks, -1), axis=-1)
    first = match & (jnp.arange(n_chunks, dtype=I32) < chunks_a)[None, :]
    nv_a = jnp.sum(first.astype(I32).reshape(n_blocks, -1), axis=-1)
    b = jnp.arange(n_blocks, dtype=I32)
    eb = jnp.minimum(jnp.sum((b_end[None, :] <= b[:, None]).astype(I32), axis=-1), N_EXPERTS - 1)
    return eb.astype(I32), jnp.concatenate([nv, nv_a]).astype(I32), cid.astype(I32)


def _moe_kernel(eb_ref, nv_ref, cid_ref, hs_a, hs_b, wg_ref, wu_ref, wd_ref, ys_a, ys_b,
                xbuf, ybuf, wg_b, wu_b, wd_b, in_sem, out_sem, *, n_blocks, chunks_a):
    b = pl.program_id(0)

    def chunk_rows(c):
        return pl.ds(pl.multiple_of(c * CHUNK, CHUNK), CHUNK)

    def gather_copy(which, c, j, slot):
        src = (hs_a, hs_b)[which]
        return pltpu.make_async_copy(src.at[chunk_rows(c)], xbuf.at[slot, chunk_rows(j)], in_sem.at[slot])

    def scatter_copy(which, c, j, slot):
        dst = (ys_a, ys_b)[which]
        return pltpu.make_async_copy(ybuf.at[slot, chunk_rows(j)],
                                     dst.at[chunk_rows(c), pl.ds(0, D_MODEL)], out_sem.at[slot])

    def for_chunks(blk, fn):
        n_a = nv_ref[n_blocks + blk]

        def body_a(j, carry):
            fn(j, 0, cid_ref[blk * CPB + j])
            return carry

        def body_b(j, carry):
            fn(j, 1, cid_ref[blk * CPB + j] - chunks_a)
            return carry
        lax.fori_loop(0, n_a, body_a, 0)
        lax.fori_loop(n_a, nv_ref[blk], body_b, 0)

    def wait_gathers(blk, slot):
        rows = pl.ds(0, nv_ref[blk] * CHUNK)

        @pl.when(nv_ref[blk] > 0)
        def _():
            pltpu.make_async_copy(hs_a.at[rows], xbuf.at[slot, rows], in_sem.at[slot]).wait()

    def wait_scatters(blk, slot):
        rows = pl.ds(0, nv_ref[blk] * CHUNK)

        @pl.when(nv_ref[blk] > 0)
        def _():
            pltpu.make_async_copy(ybuf.at[slot, rows], ys_a.at[rows, pl.ds(0, D_MODEL)], out_sem.at[slot]).wait()

    slot = lax.rem(b, 2)
    other = 1 - slot

    @pl.when(b == 0)
    def _():
        xbuf[...] = jnp.zeros_like(xbuf)
        for_chunks(0, lambda j, w, c: gather_copy(w, c, j, 0).start())

    @pl.when(b + 1 < n_blocks)
    def _():
        for_chunks(b + 1, lambda j, w, c: gather_copy(w, c, j, other).start())

    @pl.when((b == 0) | (eb_ref[b] != eb_ref[jnp.maximum(b - 1, 0)]))
    def _():
        wg_b[...] = wg_ref[...].astype(BF)
        wu_b[...] = wu_ref[...].astype(BF)
        wd_b[...] = wd_ref[...].astype(BF)

    wait_gathers(b, slot)

    @pl.when(b >= 2)
    def _():
        wait_scatters(b - 2, slot)

    @pl.when(nv_ref[b] > 0)
    def _():
        x = xbuf[slot, :, 0:D_MODEL]
        gate = (xbuf[slot, :, D_MODEL:D_MODEL + 1].astype(F32)
                + xbuf[slot, :, D_MODEL + 1:D_MODEL + 2].astype(F32))
        hg = _dot(x, wg_b[...])
        hu = _dot(x, wu_b[...])
        act = _silu(hg) * hu * gate
        ybuf[slot] = _dot(act.astype(BF), wd_b[...]).astype(BF)

    for_chunks(b, lambda j, w, c: scatter_copy(w, c, j, slot).start())

    @pl.when(b == n_blocks - 1)
    def _():
        if n_blocks >= 2:
            wait_scatters(b - 1, other)
        wait_scatters(b, slot)


def _moe(hs_a, hs_b, eb, nv, cid, layer, w_gate, w_up, w_down):
    n_blocks = eb.shape[0]

    def w_spec(a, c):
        return pl.BlockSpec((None, None, a, c), lambda b, eb, nv, cid: (layer, eb[b], 0, 0))

    grid_spec = pltpu.PrefetchScalarGridSpec(
        num_scalar_prefetch=3,
        grid=(n_blocks,),
        in_specs=[_HBM, _HBM, w_spec(D_MODEL, D_EXPERT), w_spec(D_MODEL, D_EXPERT), w_spec(D_EXPERT, D_MODEL)],
        out_specs=[_HBM, _HBM],
        scratch_shapes=[pltpu.VMEM((2, MOE_TMB, HS_W), BF), pltpu.VMEM((2, MOE_TMB, D_MODEL), BF),
                        pltpu.VMEM((D_MODEL, D_EXPERT), BF), pltpu.VMEM((D_MODEL, D_EXPERT), BF),
                        pltpu.VMEM((D_EXPERT, D_MODEL), BF),
                        pltpu.SemaphoreType.DMA((2,)), pltpu.SemaphoreType.DMA((2,))],
    )
    return pl.pallas_call(
        functools.partial(_moe_kernel, n_blocks=n_blocks, chunks_a=hs_a.shape[0] // CHUNK),
        grid_spec=grid_spec,
        out_shape=[jax.ShapeDtypeStruct(hs_a.shape, BF), jax.ShapeDtypeStruct(hs_b.shape, BF)],
        input_output_aliases={3: 0, 4: 1},
        compiler_params=_params(("arbitrary",)),
        name="moe_sorted",
    )(eb, nv, cid, hs_a, hs_b, w_gate, w_up, w_down)


def _combine_kernel(ys_ref, slots_ref, x1_ref, mod_ref, lng_ref, lnb_ref, o_ref):
    g2 = mod_ref[:, 5 * D_MODEL:6 * D_MODEL]
    s_lane = _iota((TILE, CAP), 1)
    for u in range(COMBINE_TILES_PER_STEP):
        sl = slots_ref[u]
        pick = _ones_where((s_lane == sl[:, 0:1]) | (s_lane == sl[:, 1:2]))
        ffn = _dot(pick, ys_ref[u * CAP:(u + 1) * CAP, 0:D_MODEL])
        o_ref[u] = _layer_norm(ALPHA * x1_ref[u] + g2 * ffn, lng_ref[...], lnb_ref[...])


def _combine(ys, slots, x1, mod, layer, row_fn, lng, lnb):
    S, T, _ = x1.shape
    n = T // TILE
    U = COMBINE_TILES_PER_STEP
    assert (S * n) % U == 0 and (n % U == 0 or n == 1)

    def tok(w):
        return pl.BlockSpec((U, TILE, w), lambda t: (t, 0, 0))

    out = pl.pallas_call(
        _combine_kernel,
        grid=(S * n // U,),
        in_specs=[
            pl.BlockSpec((U * CAP, HS_W), lambda t: (t, 0)),
            tok(LANES), tok(D_MODEL),
            pl.BlockSpec((None, None, 1, 6 * D_MODEL), lambda t: (layer, row_fn(t * U // n), 0, 0)),
            pl.BlockSpec((1, D_MODEL), lambda t: (0, 0)),
            pl.BlockSpec((1, D_MODEL), lambda t: (0, 0)),
        ],
        out_specs=tok(D_MODEL),
        out_shape=jax.ShapeDtypeStruct((S * n, TILE, D_MODEL), F32),
        compiler_params=_params(("parallel",)),
        name="moe_combine",
    )(ys, slots.reshape(S * n, TILE, LANES), x1.reshape(S * n, TILE, D_MODEL), mod, lng, lnb)
    return out.reshape(S, T, D_MODEL)


def _rope_tables(n_lat):
    half = DIFF_QK // 2
    pairs = half // 2
    inv = (1.0 / (ROPE_BASE ** (np.arange(pairs, dtype=np.float32) * 2.0 / half))).astype(np.float32)
    t = np.arange(n_lat)
    ang_r = ((t // GRID_W).astype(np.float32)[:, None] * inv[None, :]).astype(np.float64)
    ang_c = ((t % GRID_W).astype(np.float32)[:, None] * inv[None, :]).astype(np.float64)
    cos = np.concatenate([np.cos(ang_r)] * 2 + [np.cos(ang_c)] * 2, axis=1)
    sin = np.concatenate([-np.sin(ang_r), np.sin(ang_r), -np.sin(ang_c), np.sin(ang_c)], axis=1)
    reps = DIFF_W // DIFF_QK
    return (jnp.asarray(np.tile(cos, (1, reps)), F32), jnp.asarray(np.tile(sin, (1, reps)), F32))


def _block_diag(s):
    S = s.shape[0]
    eye = jnp.eye(H_RET, dtype=s.dtype)
    return jnp.einsum('shdv,hg->shdgv', s, eye).reshape(S, RET_W, RET_W)


def _router_rows(w_group, b_group, w_expert, b_expert):
    pad = EXPERT_ROW0 - N_GROUPS
    tail = ROUTER_ROWS - EXPERT_ROW0 - N_EXPERTS
    w = jnp.concatenate([w_group.T, jnp.zeros((pad, D_MODEL), F32),
                         w_expert.reshape(D_MODEL, N_EXPERTS).T, jnp.zeros((tail, D_MODEL), F32)], axis=0)
    bias = jnp.concatenate([b_group, jnp.zeros((pad,), F32), b_expert.reshape(N_EXPERTS), jnp.zeros((tail,), F32)])
    hi = w.astype(BF)
    return hi, (w - hi.astype(F32)).astype(BF), bias.reshape(ROUTER_ROWS, 1)


def kernel(x_prompt, x_sample, cache_diff_k, cache_diff_v, state_ret_fwd, state_ret_bwd, c, c_ctx, w_mod, b_mod, w_in, ret_decay_logit, diff_lambda, diff_subln_g, conv_w, conv_b, w_out, ln_g, ln_b, w_router_group, b_router_group, w_router_expert, b_router_expert, w_gate, w_up, w_down):
    B, T_ctx, _ = x_prompt.shape
    Bd, T_lat, _ = x_sample.shape
    assert T_ctx == TILE and T_lat % TILE == 0 and T_lat % GRID_W == 0
    assert 1 + Bd <= MOD_ROWS

    cond = jnp.concatenate([c_ctx[None, :], c, jnp.zeros((MOD_ROWS - 1 - Bd, D_MODEL), F32)], axis=0)
    mod = _modulation(cond, w_mod, b_mod).reshape(DEPTH, MOD_ROWS, 1, 6 * D_MODEL)
    rope_tabs = _rope_tables(T_lat)
    w_in_b = w_in.astype(BF)
    w_out_b = w_out.astype(BF)

    ctx_row = lambda s: 0
    lat_row = lambda s: s + 1
    ctx_tiles = B * (T_ctx // TILE)
    lat_tiles = Bd * (T_lat // TILE)

    yp, ys = x_prompt, x_sample
    stacked = None
    for l in range(DEPTH):
        lam_init = 0.8 - 0.6 * math.exp(-0.3 * l)
        lg_lanes = jnp.repeat(ret_decay_logit[l], RET_DK, axis=1)
        wrh, wrl, br = _router_rows(w_router_group[l], b_router_group[l], w_router_expert[l], b_router_expert[l])
        shared = (w_out_b, lg_lanes, diff_lambda[l], diff_subln_g[l].reshape(1, DIFF_V), conv_w[l],
                  conv_b[l].reshape(1, CONV_CH), ln_g[l, 0].reshape(1, D_MODEL), ln_b[l, 0].reshape(1, D_MODEL),
                  wrh, wrl, br, lam_init)
        ln2 = (ln_g[l, 1].reshape(1, D_MODEL), ln_b[l, 1].reshape(1, D_MODEL))

        pr = _proj(yp, mod, l, ctx_row, w_in_b, lg_lanes, stacked=stacked)
        rq, rk, rv, rg, dq, dk, dv, cb, cch = pr[:9]
        stacked = pr[9:]
        x1_c, hs_c, slots_c, cnt_c = _mixer(yp, mod, l, ctx_row, (rq, rk, rv, rg, dq, cb, cch), None, None,
                                            (dk, dv), *shared)

        pr = _proj(ys, mod, l, lat_row, w_in_b, lg_lanes, rope_tabs=rope_tabs)
        rq, rk, rv, rg, dq, dk, dv, cb, cch, kvf, kvb = pr
        states = _scan(kvf, kvb, _block_diag(state_ret_fwd[:, l]), _block_diag(state_ret_bwd[:, l]), lg_lanes)
        cache_v = cache_diff_v[:, l].astype(BF)
        cache_kv = (cache_diff_k[:, l].reshape(Bd, -1, DIFF_W).astype(BF),
                    jnp.concatenate([cache_v, jnp.ones_like(cache_v)], axis=-1).reshape(Bd, -1, V_AUG_W))
        x1_l, hs_l, slots_l, cnt_l = _mixer(ys, mod, l, lat_row, (rq, rk, rv, rg, dq, cb, cch), states, cache_kv,
                                            (dk, dv), *shared)

        padded = jnp.concatenate([cnt_c.reshape(ctx_tiles, N_EXPERTS, LANES)[:, :, 0],
                                  cnt_l.reshape(lat_tiles, N_EXPERTS, LANES)[:, :, 0]], axis=0)
        eb, nv, cid = _block_schedule(padded, ctx_tiles * CHUNKS_PER_TILE)
        out_c, out_l = _moe(hs_c, hs_l, eb, nv, cid, l, w_gate, w_up, w_down)
        yp = _combine(out_c, slots_c, x1_c, mod, l, ctx_row, *ln2)
        ys = _combine(out_l, slots_l, x1_l, mod, l, lat_row, *ln2)

    new_k, new_v, st_f, st_b = stacked
    return (yp, ys, new_k.reshape(B, DEPTH, T_ctx, H_DIFF, 2, DIFF_QK),
            new_v.reshape(B, DEPTH, T_ctx, H_DIFF, DIFF_V), st_f, st_b)
```

```python
import functools
import math

import numpy as np
import jax
import jax.numpy as jnp
from jax import lax
from jax.experimental import pallas as pl
from jax.experimental.pallas import tpu as pltpu

D_MODEL = 1024
DEPTH = 2
GRID_W = 64
H_RET = 4
RET_DK = 64
RET_W = H_RET * RET_DK
H_DIFF = 4
DIFF_QK = 64
DIFF_V = 2 * DIFF_QK
DIFF_W = H_DIFF * DIFF_V
CONV_CH = 256
ROPE_BASE = 10000.0
N_GROUPS = 4
EXPERTS_PER_GROUP = 4
N_EXPERTS = N_GROUPS * EXPERTS_PER_GROUP
D_EXPERT = 512
ALPHA = (2 * DEPTH) ** 0.25
EPS = 1e-5
MOD_ROWS = 8
LANES = 128
ROUTER_ROWS = 32
EXPERT_ROW0 = 8

TILE = 256
CHUNK = 16
CAP = 2 * TILE + N_EXPERTS * CHUNK
CHUNKS_PER_TILE = CAP // CHUNK
CTX_SEQS_PER_STEP = 2
COMBINE_TILES_PER_STEP = 2
MOE_TMB = 512
CPB = MOE_TMB // CHUNK
HS_W = D_MODEL + LANES
V_AUG_W = 2 * DIFF_W
QK_SCALE_LOG2 = (DIFF_QK ** -0.5) * math.log2(math.e)
SLOT_RADIX = 32

_O_RQ, _O_RK, _O_RV, _O_RG = 0, 256, 512, 768
_O_DQ, _O_DK, _O_DV = 1024, 1536, 2048
_O_CB, _O_CC, _O_CH = 2560, 2816, 3072
IN_WIDTH = 3328

BF = jnp.bfloat16
F32 = jnp.float32
I32 = jnp.int32

_VMEM_LIMIT = 56 * 1024 * 1024


def _dot(a, b):
    return jnp.dot(a, b, preferred_element_type=F32)


def _dot_nt(a, b):
    return lax.dot_general(a, b, (((1,), (1,)), ((), ())), preferred_element_type=F32)


def _dot_tn(a, b):
    return lax.dot_general(a, b, (((0,), (0,)), ((), ())), preferred_element_type=F32)


def _split(x):
    hi = x.astype(BF)
    lo = (x - hi.astype(F32)).astype(BF)
    return hi, lo


def _dot_hl(x, w_bf16):
    hi, lo = _split(x)
    return _dot(hi, w_bf16) + _dot(lo, w_bf16)


def _dot3(x, w_hi, w_lo):
    hi, lo = _split(x)
    return _dot(hi, w_hi) + (_dot(lo, w_hi) + _dot(hi, w_lo))


def _iota(shape, dim):
    return lax.broadcasted_iota(I32, shape, dim)


def _ones_where(cond):
    return jnp.where(cond, 1.0, 0.0).astype(BF)


def _log_sigmoid(x):
    return jnp.minimum(x, 0.0) - jnp.log1p(jnp.exp(-jnp.abs(x)))


def _silu(x):
    return x * jax.nn.sigmoid(x)


def _layer_norm(y, g, b):
    mu = jnp.mean(y, axis=-1, keepdims=True)
    yc = y - mu
    var = jnp.mean(yc * yc, axis=-1, keepdims=True)
    return yc * lax.rsqrt(var + EPS) * g + b


def _params(sem):
    return pltpu.CompilerParams(dimension_semantics=sem, vmem_limit_bytes=_VMEM_LIMIT)


_HBM = pl.BlockSpec(memory_space=pltpu.HBM)


def _mod_kernel(c_ref, w_ref, b_ref, o_ref):
    a = _silu(c_ref[...])
    w_hi, w_lo = _split(w_ref[...])
    o_ref[...] = _dot3(a, w_hi, w_lo) + b_ref[...]


def _modulation(cond, w_mod, b_mod):
    tn = 1536
    n6 = 6 * D_MODEL
    return pl.pallas_call(
        _mod_kernel,
        grid=(DEPTH, n6 // tn),
        in_specs=[
            pl.BlockSpec((MOD_ROWS, D_MODEL), lambda l, j: (0, 0)),
            pl.BlockSpec((None, D_MODEL, tn), lambda l, j: (l, 0, j)),
            pl.BlockSpec((None, 1, tn), lambda l, j: (l, 0, j)),
        ],
        out_specs=pl.BlockSpec((None, MOD_ROWS, tn), lambda l, j: (l, 0, j)),
        out_shape=jax.ShapeDtypeStruct((DEPTH, MOD_ROWS, n6), F32),
        compiler_params=_params(("parallel", "parallel")),
        name="modulation",
    )(cond, w_mod, b_mod.reshape(DEPTH, 1, n6))


def _mod_spec(layer, row_fn):
    return pl.BlockSpec((None, None, 1, 6 * D_MODEL), lambda s, i: (layer, row_fn(s), 0, 0))


def _augment_v(v):
    ones = jnp.ones((v.shape[0], DIFF_V), v.dtype)
    parts = []
    for hd in range(H_DIFF):
        parts += [v[:, hd * DIFF_V:(hd + 1) * DIFF_V], ones]
    return jnp.concatenate(parts, axis=1)


def _swap16(x):
    n = x.shape[-1]
    lane = _iota(x.shape, 1)
    return jnp.where((lane & 16) == 0, pltpu.roll(x, n - 16, axis=1), pltpu.roll(x, 16, axis=1))


def _proj_kernel(*refs, latent, n_aliased, layer):
    x_ref, mod_ref, w_ref, lg_ref = refs[:4]
    pos = 4
    if latent:
        cos_ref, sin_ref = refs[pos:pos + 2]
        pos += 2
    pos += n_aliased
    rq_ref, rk_ref, rv_ref, rg_ref, dq_ref, dk_ref, dv_ref, cb_ref, cch_ref = refs[pos:pos + 9]
    pos += 9
    if latent:
        kvf_ref, kvb_ref = refs[pos:pos + 2]
    else:
        dk32_ref, dv32_ref, stf_ref, stb_ref = refs[pos:pos + 4]

    def put(ref, idx, val):
        if n_aliased:
            ref[idx if idx else ...] = val
        else:
            for other in range(DEPTH):
                ref[(other,) + idx] = val if other == layer else jnp.zeros_like(val)

    m = mod_ref[...]
    sh1 = m[:, 0:D_MODEL]
    sc1 = m[:, D_MODEL:2 * D_MODEL]
    h = (x_ref[...] * (1.0 + sc1) + sh1).astype(BF)

    def col(off, width):
        return _dot(h, w_ref[:, off:off + width])

    rq_ref[...] = col(_O_RQ, RET_W).astype(BF)
    rk = col(_O_RK, RET_W) * (RET_DK ** -0.5)
    rk_ref[...] = rk.astype(BF)
    rv = col(_O_RV, RET_W).astype(BF)
    rv_ref[...] = rv
    rg_ref[...] = col(_O_RG, RET_W)

    lg = _log_sigmoid(lg_ref[...])
    p = _iota((TILE, 1), 0).astype(F32)
    kf = (rk * jnp.exp((TILE - 1.0 - p) * lg[0:1])).astype(BF)
    kb = (rk * jnp.exp(p * lg[1:2])).astype(BF)
    kvf = _dot_tn(kf, rv)
    kvb = _dot_tn(kb, rv)
    if latent:
        kvf_ref[...] = kvf
        kvb_ref[...] = kvb
    else:
        for hd in range(H_RET):
            lo = hd * RET_DK
            put(stf_ref, (hd,), kvf[lo:lo + RET_DK, lo:lo + RET_DK])
            put(stb_ref, (hd,), kvb[lo:lo + RET_DK, lo:lo + RET_DK])

    dq = col(_O_DQ, DIFF_W)
    dk = col(_O_DK, DIFF_W)
    dv = col(_O_DV, DIFF_W)
    if latent:
        cos = cos_ref[...]
        sin = sin_ref[...]
        dq = dq * cos + _swap16(dq) * sin
        dk = dk * cos + _swap16(dk) * sin
    else:
        put(dk32_ref, (), dk)
        put(dv32_ref, (), dv)
    dq_ref[...] = (dq * QK_SCALE_LOG2).astype(BF)
    dk_ref[...] = dk.astype(BF)
    dv_ref[...] = _augment_v(dv.astype(BF))

    cb_ref[...] = col(_O_CB, CONV_CH)
    cch_ref[...] = col(_O_CC, CONV_CH) * col(_O_CH, CONV_CH)


def _proj(x, mod, layer, row_fn, w_in_b, lg_lanes, rope_tabs=None, stacked=None):
    S, T, _ = x.shape
    n = T // TILE
    latent = rope_tabs is not None

    def tok(w):
        return pl.BlockSpec((None, TILE, w), lambda s, i: (s, i, 0))

    def tok_shape(w, dt):
        return jax.ShapeDtypeStruct((S, T, w), dt)

    in_specs = [
        tok(D_MODEL), _mod_spec(layer, row_fn),
        pl.BlockSpec((None, D_MODEL, IN_WIDTH), lambda s, i: (layer, 0, 0)),
        pl.BlockSpec((2, RET_W), lambda s, i: (0, 0)),
    ]
    args = [x, mod, w_in_b, lg_lanes]
    out_specs = [tok(RET_W)] * 4 + [tok(DIFF_W)] * 2 + [tok(V_AUG_W)] + [tok(CONV_CH)] * 2
    out_shape = ([tok_shape(RET_W, BF)] * 3 + [tok_shape(RET_W, F32)] + [tok_shape(DIFF_W, BF)] * 2
                 + [tok_shape(V_AUG_W, BF)]
                 + [tok_shape(CONV_CH, F32)] * 2)
    aliases = {}
    if latent:
        in_specs += [pl.BlockSpec((TILE, DIFF_W), lambda s, i: (i, 0))] * 2
        args += list(rope_tabs)
        out_specs += [pl.BlockSpec((None, None, RET_W, RET_W), lambda s, i: (s, i, 0, 0))] * 2
        out_shape += [jax.ShapeDtypeStruct((S, n, RET_W, RET_W), F32)] * 2
    else:
        assert n == 1
        kv_shape = jax.ShapeDtypeStruct((S, DEPTH, T, DIFF_W), F32)
        st_shape = jax.ShapeDtypeStruct((S, DEPTH, H_RET, RET_DK, RET_DK), F32)
        if stacked is not None:
            in_specs += [_HBM] * 4
            args += list(stacked)
            aliases = {len(args) - 4 + j: len(out_shape) + j for j in range(4)}
            out_specs += [pl.BlockSpec((None, None, TILE, DIFF_W), lambda s, i: (s, layer, i, 0))] * 2
            out_specs += [pl.BlockSpec((None, None, H_RET, RET_DK, RET_DK), lambda s, i: (s, layer, 0, 0, 0))] * 2
        else:
            out_specs += [pl.BlockSpec((None, DEPTH, TILE, DIFF_W), lambda s, i: (s, 0, i, 0))] * 2
            out_specs += [pl.BlockSpec((None, DEPTH, H_RET, RET_DK, RET_DK), lambda s, i: (s, 0, 0, 0, 0))] * 2
        out_shape += [kv_shape, kv_shape, st_shape, st_shape]
    return pl.pallas_call(
        functools.partial(_proj_kernel, latent=latent, n_aliased=len(aliases), layer=layer),
        grid=(S, n),
        in_specs=in_specs,
        out_specs=out_specs,
        out_shape=out_shape,
        input_output_aliases=aliases,
        compiler_params=_params(("parallel", "parallel")),
        name="proj_lat" if latent else "proj_ctx",
    )(*args)


def _scan_kernel(kvf_ref, kvb_ref, s0f_ref, s0b_ref, lg_ref, sf_ref, sb_ref, *, n):
    lg = _log_sigmoid(lg_ref[...])
    dec = jnp.exp(float(TILE) * lg)
    same_head = (_iota((RET_W, RET_W), 0) >> 6) == (_iota((RET_W, RET_W), 1) >> 6)

    s = jnp.where(same_head, s0f_ref[...], 0.0)
    for c in range(n):
        sf_ref[c] = s
        s = s * dec[0:1] + jnp.where(same_head, kvf_ref[c], 0.0)
    s = jnp.where(same_head, s0b_ref[...], 0.0)
    for c in reversed(range(n)):
        sb_ref[c] = s
        s = s * dec[1:2] + jnp.where(same_head, kvb_ref[c], 0.0)


def _scan(kvf, kvb, s0f, s0b, lg_lanes):
    S, n = kvf.shape[:2]
    chunks = pl.BlockSpec((None, n, RET_W, RET_W), lambda s: (s, 0, 0, 0))
    one = pl.BlockSpec((None, RET_W, RET_W), lambda s: (s, 0, 0))
    return pl.pallas_call(
        functools.partial(_scan_kernel, n=n),
        grid=(S,),
        in_specs=[chunks, chunks, one, one, pl.BlockSpec((2, RET_W), lambda s: (0, 0))],
        out_specs=[chunks, chunks],
        out_shape=[jax.ShapeDtypeStruct((S, n, RET_W, RET_W), F32)] * 2,
        compiler_params=_params(("parallel",)),
        name="ret_scan",
    )(kvf, kvb, s0f, s0b, lg_lanes)


def _diff_heads(dq_ref, kv_refs, heads, lam):
    lane = _iota((1, DIFF_V), 1)
    scores = []
    for hd in heads:
        lo = hd * DIFF_V
        q_h = dq_ref[:, lo:lo + DIFF_V]
        qs = jnp.concatenate([jnp.where(lane < DIFF_QK, q_h, 0), jnp.where(lane >= DIFF_QK, q_h, 0)], axis=0)
        scores.append(jnp.concatenate([_dot_nt(qs, k[:, lo:lo + DIFF_V]) for k, _ in kv_refs], axis=1))
    s = jnp.concatenate(scores, axis=0)
    p = jnp.exp2(s - jnp.max(s, axis=-1, keepdims=True)).astype(BF)
    outs = []
    for n, hd in enumerate(heads):
        acc = None
        start = 0
        for k, v in kv_refs:
            part = _dot(p[n * 2 * TILE:(n + 1) * 2 * TILE, start:start + k.shape[0]],
                        v[:, hd * 2 * DIFF_V:(hd + 1) * 2 * DIFF_V])
            acc = part if acc is None else acc + part
            start += k.shape[0]
        o0 = acc[0:TILE, 0:DIFF_V] * (1.0 / acc[0:TILE, DIFF_V:DIFF_V + 1])
        o1 = acc[TILE:, 0:DIFF_V] * (lam / acc[TILE:, DIFF_V:DIFF_V + 1])
        outs.append(o0 - o1)
    return outs


def _mixer_kernel(*refs, n_sub, split, **static):
    if n_sub == 1:
        return _mixer_tile(*refs, **static)
    for u in range(n_sub):
        views = [r.at[u] if how == 1 else (r.at[pl.ds(u * CAP, CAP)] if how == 2 else r)
                 for r, how in zip(refs, split)]
        _mixer_tile(*views, **static)


def _mixer_pipe_kernel(*refs, n_tiles, n_total, lam_init, cached):
    n_a = _n_token_mix_refs(cached)
    mix_even, mix_odd = refs[-2:]
    t = pl.program_id(0)
    tile_in_seq = lax.rem(jnp.minimum(t, n_total - 1), n_tiles)

    @pl.when(t == 0)
    def _():
        mix_odd[...] = jnp.zeros_like(mix_odd)

    def step(mix_write, mix_read):
        mix_write[...] = _token_mix(*refs[:n_a], tile_in_seq=tile_in_seq, n_tiles=n_tiles,
                                    lam_init=lam_init, cached=cached)
        _project_route(mix_read[...], *refs[n_a:-2])

    @pl.when(lax.rem(t, 2) == 0)
    def _():
        step(mix_even, mix_odd)

    @pl.when(lax.rem(t, 2) == 1)
    def _():
        step(mix_odd, mix_even)


def _n_token_mix_refs(cached):
    return 18 + (2 if cached else 0)


def _mixer_tile(*refs, n_tiles, lam_init, cached):
    n_a = _n_token_mix_refs(cached)
    mixed = _token_mix(*refs[:n_a], tile_in_seq=pl.program_id(1), n_tiles=n_tiles, lam_init=lam_init,
                       cached=cached)
    _project_route(mixed, *refs[n_a:])


def _token_mix(*refs, tile_in_seq, n_tiles, lam_init, cached):
    rq_ref, rk_ref, rv_ref, rg_ref, sf_ref, sb_ref, dq_ref = refs[:7]
    pos = 7
    kv_refs = []
    if cached:
        kv_refs.append((refs[pos], refs[pos + 1]))
        pos += 2
    kv_refs.append((refs[pos], refs[pos + 1]))
    pos += 2
    cb_ref, cch_ref, cprev_ref, cnext_ref, lg_ref, lamp_ref, subg_ref, cw_ref, cbias_ref = refs[pos:pos + 9]
    i = tile_in_seq

    lg = _log_sigmoid(lg_ref[...])
    head_of_lane = _iota((1, RET_W), 1) >> 6
    q = rq_ref[...]
    k = rk_ref[...]
    v = rv_ref[...]
    dist = (_iota((TILE, TILE), 0) - _iota((TILE, TILE), 1)).astype(F32)
    adist = jnp.abs(dist)
    diag2 = jnp.where(dist == 0.0, 2.0, 1.0)
    ret_o = jnp.zeros((TILE, RET_W), F32)
    for hd in range(H_RET):
        in_head = head_of_lane == hd
        sc = _dot_nt(jnp.where(in_head, q, 0), k)
        lgf = lg[0:1, hd * RET_DK:hd * RET_DK + 1]
        lgb = lg[1:2, hd * RET_DK:hd * RET_DK + 1]
        decay = jnp.exp(adist * jnp.where(dist > 0.0, lgf, lgb)) * diag2
        ret_o = ret_o + jnp.where(in_head, _dot((sc * decay).astype(BF), v), 0.0)
    p = _iota((TILE, 1), 0).astype(F32)
    ret_o = ret_o + _dot(q, sf_ref[...].astype(BF)) * jnp.exp((p + 1.0) * lg[0:1])
    ret_o = ret_o + _dot(q, sb_ref[...].astype(BF)) * jnp.exp((float(TILE) - p) * lg[1:2])
    avg = jnp.where((_iota((RET_W, RET_W), 0) >> 6) == (_iota((RET_W, RET_W), 1) >> 6),
                    1.0 / RET_DK, 0.0).astype(BF)
    rc = ret_o - _dot_hl(ret_o, avg)
    ret = rc * lax.rsqrt(_dot_hl(rc * rc, avg) + EPS) * _silu(rg_ref[...])

    lp = lamp_ref[...]
    lam = (jnp.exp(jnp.sum(lp[0:1] * lp[1:2], axis=-1, keepdims=True))
           - jnp.exp(jnp.sum(lp[2:3] * lp[3:4], axis=-1, keepdims=True)) + lam_init)
    subg = subg_ref[...] * (1.0 - lam_init)
    head_groups = [[hd] for hd in range(H_DIFF)] if cached else [list(range(H_DIFF))]
    heads = []
    for group in head_groups:
        for o in _diff_heads(dq_ref, kv_refs, group, lam):
            o = o * lax.rsqrt(jnp.mean(o * o, axis=-1, keepdims=True) + EPS) * subg
            heads.append(o.astype(BF))
    diff = jnp.concatenate(heads, axis=1)

    cch = cch_ref[...]
    prev = jnp.where(i > 0, cprev_ref[7:8, :], 0.0)
    nxt = jnp.where(i < n_tiles - 1, cnext_ref[0:1, :], 0.0)
    r = _iota((TILE, 1), 0)
    up = jnp.where(r == 0, prev, pltpu.roll(cch, 1, axis=0))
    dn = jnp.where(r == TILE - 1, nxt, pltpu.roll(cch, TILE - 1, axis=0))
    cw = cw_ref[...]
    conv = cb_ref[...] * (up * cw[0:1] + cch * cw[1:2] + dn * cw[2:3] + cbias_ref[...])
    return jnp.concatenate([ret.astype(BF), diff, conv.astype(BF)], axis=1)


def _project_route(mixed, x_ref, mod_ref, wout_ref, lng_ref, lnb_ref, wrh_ref, wrl_ref, br_ref,
                   x1_ref, hs_ref, slots_ref, cnt_ref):
    m = mod_ref[...]
    g1 = m[:, 2 * D_MODEL:3 * D_MODEL]
    sh2 = m[:, 3 * D_MODEL:4 * D_MODEL]
    sc2 = m[:, 4 * D_MODEL:5 * D_MODEL]
    x1 = _layer_norm(ALPHA * x_ref[...] + g1 * _dot(mixed, wout_ref[...]), lng_ref[...], lnb_ref[...])
    x1_ref[...] = x1

    h2 = x1 * (1.0 + sc2) + sh2
    h2h, h2l = _split(h2)
    wrh = wrh_ref[...]
    logits = (_dot_nt(wrh, h2h) + (_dot_nt(wrh, h2l) + _dot_nt(wrl_ref[...], h2h))) + br_ref[...]
    neg = -jnp.inf
    g_id = _iota((EXPERT_ROW0, TILE), 0)
    gl = jnp.where(g_id < N_GROUPS, logits[0:EXPERT_ROW0], neg)
    gmax = jnp.max(gl, axis=0, keepdims=True)
    g_idx = jnp.min(jnp.where(gl == gmax, g_id, N_GROUPS), axis=0, keepdims=True)
    g_w = 1.0 / jnp.sum(jnp.exp(gl - gmax), axis=0, keepdims=True)
    e_id = _iota((N_EXPERTS, TILE), 0)
    el = jnp.where((e_id >> 2) == g_idx, logits[EXPERT_ROW0:EXPERT_ROW0 + N_EXPERTS], neg)
    v1 = jnp.max(el, axis=0, keepdims=True)
    i1 = jnp.min(jnp.where(el == v1, e_id, N_EXPERTS), axis=0, keepdims=True)
    el2 = jnp.where(e_id == i1, neg, el)
    v2 = jnp.max(el2, axis=0, keepdims=True)
    i2 = jnp.min(jnp.where(el2 == v2, e_id, N_EXPERTS), axis=0, keepdims=True)
    t = jnp.exp(v2 - v1)
    w1 = g_w / (1.0 + t)
    w2 = g_w * t / (1.0 + t)

    sel1 = e_id == i1
    sel2 = e_id == i2
    sel = jnp.where(sel1 | sel2, 1.0, 0.0)
    earlier = _ones_where(_iota((TILE, TILE), 0) < _iota((TILE, TILE), 1))
    rank = _dot(sel.astype(BF), earlier)
    cnt = jnp.sum(sel, axis=1, keepdims=True).astype(I32)
    padded = ((cnt + (CHUNK - 1)) >> 4) << 4
    incl = jnp.broadcast_to(padded, (N_EXPERTS, LANES))
    e_row = _iota((N_EXPERTS, LANES), 0)
    for step in (1, 2, 4, 8):
        incl = incl + jnp.where(e_row >= step, pltpu.roll(incl, step, axis=0), 0)
    seg_off = (incl[:, 0:1] - padded).astype(F32)
    spos = seg_off + rank
    slot1 = jnp.sum(jnp.where(sel1, spos, 0.0), axis=0, keepdims=True).astype(I32)
    slot2 = jnp.sum(jnp.where(sel2, spos, 0.0), axis=0, keepdims=True).astype(I32)
    s_id = _iota((CAP, TILE), 0)
    at1 = s_id == slot1
    at2 = s_id == slot2
    hs_ref[:, 0:D_MODEL] = _dot(_ones_where(at1 | at2), h2h).astype(BF)
    w1h = w1.astype(BF).astype(F32)
    w2h = w2.astype(BF).astype(F32)
    ones = jnp.ones((TILE, LANES), BF)
    g_hi = _dot((jnp.where(at1, w1h, 0.0) + jnp.where(at2, w2h, 0.0)).astype(BF), ones)
    g_lo = _dot((jnp.where(at1, w1 - w1h, 0.0) + jnp.where(at2, w2 - w2h, 0.0)).astype(BF), ones)
    lane = _iota((CAP, LANES), 1)
    hs_ref[:, D_MODEL:] = jnp.where(lane == 0, g_hi, jnp.where(lane == 1, g_lo, 0.0)).astype(BF)
    cnt_ref[...] = jnp.broadcast_to(padded, (N_EXPERTS, LANES))

    dr = _iota((LANES, TILE), 0)
    digits = jnp.where(dr == 0, slot1 & (SLOT_RADIX - 1), jnp.where(dr == 1, slot1 >> 5,
             jnp.where(dr == 2, slot2 & (SLOT_RADIX - 1), jnp.where(dr == 3, slot2 >> 5, 0))))
    eye = _ones_where(_iota((TILE, TILE), 0) == _iota((TILE, TILE), 1))
    cols = _dot_nt(eye, digits.astype(F32).astype(BF))
    s1c = (cols[:, 0:1] + SLOT_RADIX * cols[:, 1:2]).astype(I32)
    s2c = (cols[:, 2:3] + SLOT_RADIX * cols[:, 3:4]).astype(I32)
    tl = _iota((TILE, LANES), 1)
    slots_ref[...] = jnp.where(tl == 0, s1c, jnp.where(tl == 1, s2c, 0))


def _mixer(x, mod, layer, row_fn, pr, states, cache_kv, new_kv, w_out_b, lg_lanes, lamp, subg, cw, cbias,
           lng, lnb, wrh, wrl, br, lam_init):
    S, T, _ = x.shape
    n = T // TILE
    rq, rk, rv, rg, dq, cb, cch = pr
    t8 = TILE // 8
    cached = cache_kv is not None
    pipelined = True
    n_sub = CTX_SEQS_PER_STEP if (not pipelined and S % CTX_SEQS_PER_STEP == 0) else 1
    lead = None if n_sub == 1 else n_sub
    SEQ, SHARED, ROWS = 1, 0, 2

    if pipelined:
        last = S * n - 1
        grid = (S * n + 1,)
        at_mix = lambda t: (jnp.minimum(t, last) // n, jnp.minimum(t, last) % n)
        at_out = lambda t: (jnp.maximum(t - 1, 0) // n, jnp.maximum(t - 1, 0) % n)
    else:
        grid = (S // n_sub, n)
        at_mix = at_out = lambda s, i: (s, i)

    def tok(w, at):
        return pl.BlockSpec((lead, TILE, w), lambda *g: (*at(*g), 0))

    def full(shape):
        return pl.BlockSpec(shape, lambda *g: (0,) * len(shape))

    def seq(a):
        mode = pl.Buffered(1) if n > 1 else None
        return pl.BlockSpec((lead,) + a.shape[1:], lambda *g: (at_mix(*g)[0], 0, 0), pipeline_mode=mode)

    def halo(offset):
        def index(*g):
            s, i = at_mix(*g)
            return (s, jnp.clip(i * t8 + offset, 0, T // 8 - 1), 0)
        return pl.BlockSpec((lead, 8, CONV_CH), index)

    if states is None:
        zero_state = jnp.zeros((1, 1, RET_W, RET_W), F32)
        states = (zero_state, zero_state)
        state = pl.BlockSpec((None, None, RET_W, RET_W), lambda *g: (0, 0, 0, 0))
    else:
        assert n_sub == 1
        state = pl.BlockSpec((None, None, RET_W, RET_W), lambda *g: (*at_mix(*g), 0, 0))

    in_specs = [tok(RET_W, at_mix)] * 4 + [state, state, tok(DIFF_W, at_mix)]
    split = [SEQ] * 4 + [SHARED, SHARED, SEQ]
    args = [rq, rk, rv, rg, states[0], states[1], dq]
    for kv in ([cache_kv] if cached else []) + [new_kv]:
        in_specs += [seq(kv[0]), seq(kv[1])]
        split += [SEQ, SEQ]
        args += list(kv)
    in_specs += [tok(CONV_CH, at_mix), tok(CONV_CH, at_mix), halo(-1), halo(t8),
                 full((2, RET_W)), full((4, DIFF_QK)), full((1, DIFF_V)), full((3, CONV_CH)), full((1, CONV_CH))]
    split += [SEQ] * 4 + [SHARED] * 5
    args += [cb, cch, cch, cch, lg_lanes, lamp, subg, cw, cbias]
    assert len(args) == _n_token_mix_refs(cached)
    in_specs += [tok(D_MODEL, at_out),
                 pl.BlockSpec((None, None, 1, 6 * D_MODEL), lambda *g: (layer, row_fn(at_out(*g)[0]), 0, 0)),
                 pl.BlockSpec((None, D_MODEL, D_MODEL), lambda *g: (layer, 0, 0)),
                 full((1, D_MODEL)), full((1, D_MODEL)),
                 full((ROUTER_ROWS, D_MODEL)), full((ROUTER_ROWS, D_MODEL)), full((ROUTER_ROWS, 1))]
    split += [SEQ] + [SHARED] * 7
    args += [x, mod, w_out_b, lng, lnb, wrh, wrl, br]
    split += [SEQ, ROWS, SEQ, SEQ]

    def out_tile(*g):
        s, i = at_out(*g)
        return s * n + i

    static = dict(n_tiles=n, lam_init=lam_init, cached=cached)
    if pipelined:
        body = functools.partial(_mixer_pipe_kernel, n_total=S * n, **static)
        scratch = [pltpu.VMEM((TILE, D_MODEL), BF)] * 2
        semantics = ("arbitrary",)
    else:
        body = functools.partial(_mixer_kernel, n_sub=n_sub, split=tuple(split), **static)
        scratch = []
        semantics = ("parallel", "parallel")
    return pl.pallas_call(
        body,
        grid=grid,
        in_specs=in_specs,
        out_specs=[tok(D_MODEL, at_out),
                   pl.BlockSpec((n_sub * CAP, HS_W), lambda *g: (out_tile(*g), 0)),
                   tok(LANES, at_out),
                   pl.BlockSpec((lead, None, N_EXPERTS, LANES), lambda *g: (*at_out(*g), 0, 0))],
        out_shape=[jax.ShapeDtypeStruct((S, T, D_MODEL), F32),
                   jax.ShapeDtypeStruct((S * n * CAP, HS_W), BF),
                   jax.ShapeDtypeStruct((S, T, LANES), I32),
                   jax.ShapeDtypeStruct((S, n, N_EXPERTS, LANES), I32)],
        scratch_shapes=scratch,
        compiler_params=_params(semantics),
        name="mixer_lat" if cached else "mixer_ctx",
    )(*args)


def _block_schedule(padded, chunks_a):
    NT = padded.shape[0]
    n_chunks = NT * CHUNKS_PER_TILE
    n_blocks = n_chunks // CPB + N_EXPERTS
    c16 = padded // CHUNK
    ends = jnp.cumsum(c16, axis=1)
    before = jnp.cumsum(c16, axis=0) - c16
    per_expert = jnp.sum(c16, axis=0)
    nb = (per_expert + (CPB - 1)) // CPB
    b_end = jnp.cumsum(nb)
    q = jnp.arange(CHUNKS_PER_TILE, dtype=I32)
    e = jnp.arange(N_EXPERTS, dtype=I32)
    key = jnp.sum((ends[:, None, :] <= q[None, :, None]).astype(I32), axis=-1)
    pos_e = ((b_end - nb) * CPB + before)[:, None, :] + (q[None, :, None] - (ends - c16)[:, None, :])
    pos = jnp.sum(jnp.where(key[..., None] == e, pos_e, 0), axis=-1)
    pos = jnp.where(key < N_EXPERTS, pos, -1).reshape(-1)
    match = pos[None, :] == jnp.arange(n_blocks * CPB, dtype=I32)[:, None]
    cid = jnp.sum(jnp.where(match, jnp.arange(n_chunks, dtype=I32)[None, :], 0), axis=-1)
    nv = jnp.sum(match.astype(I32).reshape(n_blocks, -1), axis=-1)
    first = match & (jnp.arange(n_chunks, dtype=I32) < chunks_a)[None, :]
    nv_a = jnp.sum(first.astype(I32).reshape(n_blocks, -1), axis=-1)
    b = jnp.arange(n_blocks, dtype=I32)
    eb = jnp.minimum(jnp.sum((b_end[None, :] <= b[:, None]).astype(I32), axis=-1), N_EXPERTS - 1)
    return eb.astype(I32), jnp.concatenate([nv, nv_a]).astype(I32), cid.astype(I32)


def _moe_kernel(eb_ref, nv_ref, cid_ref, hs_a, hs_b, wg_ref, wu_ref, wd_ref, ys_a, ys_b,
                xbuf, ybuf, wg_b, wu_b, wd_b, in_sem, out_sem, *, n_blocks, chunks_a):
    b = pl.program_id(0)

    def chunk_rows(c):
        return pl.ds(pl.multiple_of(c * CHUNK, CHUNK), CHUNK)

    def gather_copy(which, c, j, slot):
        src = (hs_a, hs_b)[which]
        return pltpu.make_async_copy(src.at[chunk_rows(c)], xbuf.at[slot, chunk_rows(j)], in_sem.at[slot])

    def scatter_copy(which, c, j, slot):
        dst = (ys_a, ys_b)[which]
        return pltpu.make_async_copy(ybuf.at[slot, chunk_rows(j)],
                                     dst.at[chunk_rows(c), pl.ds(0, D_MODEL)], out_sem.at[slot])

    def for_chunks(blk, fn):
        n_a = nv_ref[n_blocks + blk]

        def body_a(j, carry):
            fn(j, 0, cid_ref[blk * CPB + j])
            return carry

        def body_b(j, carry):
            fn(j, 1, cid_ref[blk * CPB + j] - chunks_a)
            return carry
        lax.fori_loop(0, n_a, body_a, 0)
        lax.fori_loop(n_a, nv_ref[blk], body_b, 0)

    def wait_gathers(blk, slot):
        rows = pl.ds(0, nv_ref[blk] * CHUNK)

        @pl.when(nv_ref[blk] > 0)
        def _():
            pltpu.make_async_copy(hs_a.at[rows], xbuf.at[slot, rows], in_sem.at[slot]).wait()

    def wait_scatters(blk, slot):
        rows = pl.ds(0, nv_ref[blk] * CHUNK)

        @pl.when(nv_ref[blk] > 0)
        def _():
            pltpu.make_async_copy(ybuf.at[slot, rows], ys_a.at[rows, pl.ds(0, D_MODEL)], out_sem.at[slot]).wait()

    slot = lax.rem(b, 2)
    other = 1 - slot

    @pl.when(b == 0)
    def _():
        xbuf[...] = jnp.zeros_like(xbuf)
        for_chunks(0, lambda j, w, c: gather_copy(w, c, j, 0).start())

    @pl.when(b + 1 < n_blocks)
    def _():
        for_chunks(b + 1, lambda j, w, c: gather_copy(w, c, j, other).start())

    @pl.when((b == 0) | (eb_ref[b] != eb_ref[jnp.maximum(b - 1, 0)]))
    def _():
        wg_b[...] = wg_ref[...].astype(BF)
        wu_b[...] = wu_ref[...].astype(BF)
        wd_b[...] = wd_ref[...].astype(BF)

    wait_gathers(b, slot)

    @pl.when(b >= 2)
    def _():
        wait_scatters(b - 2, slot)

    @pl.when(nv_ref[b] > 0)
    def _():
        x = xbuf[slot, :, 0:D_MODEL]
        gate = (xbuf[slot, :, D_MODEL:D_MODEL + 1].astype(F32)
                + xbuf[slot, :, D_MODEL + 1:D_MODEL + 2].astype(F32))
        hg = _dot(x, wg_b[...])
        hu = _dot(x, wu_b[...])
        act = _silu(hg) * hu * gate
        ybuf[slot] = _dot(act.astype(BF), wd_b[...]).astype(BF)

    for_chunks(b, lambda j, w, c: scatter_copy(w, c, j, slot).start())

    @pl.when(b == n_blocks - 1)
    def _():
        if n_blocks >= 2:
            wait_scatters(b - 1, other)
        wait_scatters(b, slot)


def _moe(hs_a, hs_b, eb, nv, cid, layer, w_gate, w_up, w_down):
    n_blocks = eb.shape[0]

    def w_spec(a, c):
        return pl.BlockSpec((None, None, a, c), lambda b, eb, nv, cid: (layer, eb[b], 0, 0))

    grid_spec = pltpu.PrefetchScalarGridSpec(
        num_scalar_prefetch=3,
        grid=(n_blocks,),
        in_specs=[_HBM, _HBM, w_spec(D_MODEL, D_EXPERT), w_spec(D_MODEL, D_EXPERT), w_spec(D_EXPERT, D_MODEL)],
        out_specs=[_HBM, _HBM],
        scratch_shapes=[pltpu.VMEM((2, MOE_TMB, HS_W), BF), pltpu.VMEM((2, MOE_TMB, D_MODEL), BF),
                        pltpu.VMEM((D_MODEL, D_EXPERT), BF), pltpu.VMEM((D_MODEL, D_EXPERT), BF),
                        pltpu.VMEM((D_EXPERT, D_MODEL), BF),
                        pltpu.SemaphoreType.DMA((2,)), pltpu.SemaphoreType.DMA((2,))],
    )
    return pl.pallas_call(
        functools.partial(_moe_kernel, n_blocks=n_blocks, chunks_a=hs_a.shape[0] // CHUNK),
        grid_spec=grid_spec,
        out_shape=[jax.ShapeDtypeStruct(hs_a.shape, BF), jax.ShapeDtypeStruct(hs_b.shape, BF)],
        input_output_aliases={3: 0, 4: 1},
        compiler_params=_params(("arbitrary",)),
        name="moe_sorted",
    )(eb, nv, cid, hs_a, hs_b, w_gate, w_up, w_down)


def _combine_kernel(ys_ref, slots_ref, x1_ref, mod_ref, lng_ref, lnb_ref, o_ref):
    g2 = mod_ref[:, 5 * D_MODEL:6 * D_MODEL]
    s_lane = _iota((TILE, CAP), 1)
    for u in range(COMBINE_TILES_PER_STEP):
        sl = slots_ref[u]
        pick = _ones_where((s_lane == sl[:, 0:1]) | (s_lane == sl[:, 1:2]))
        ffn = _dot(pick, ys_ref[u * CAP:(u + 1) * CAP, 0:D_MODEL])
        o_ref[u] = _layer_norm(ALPHA * x1_ref[u] + g2 * ffn, lng_ref[...], lnb_ref[...])


def _combine(ys, slots, x1, mod, layer, row_fn, lng, lnb):
    S, T, _ = x1.shape
    n = T // TILE
    U = COMBINE_TILES_PER_STEP
    assert (S * n) % U == 0 and (n % U == 0 or n == 1)

    def tok(w):
        return pl.BlockSpec((U, TILE, w), lambda t: (t, 0, 0))

    out = pl.pallas_call(
        _combine_kernel,
        grid=(S * n // U,),
        in_specs=[
            pl.BlockSpec((U * CAP, HS_W), lambda t: (t, 0)),
            tok(LANES), tok(D_MODEL),
            pl.BlockSpec((None, None, 1, 6 * D_MODEL), lambda t: (layer, row_fn(t * U // n), 0, 0)),
            pl.BlockSpec((1, D_MODEL), lambda t: (0, 0)),
            pl.BlockSpec((1, D_MODEL), lambda t: (0, 0)),
        ],
        out_specs=tok(D_MODEL),
        out_shape=jax.ShapeDtypeStruct((S * n, TILE, D_MODEL), F32),
        compiler_params=_params(("parallel",)),
        name="moe_combine",
    )(ys, slots.reshape(S * n, TILE, LANES), x1.reshape(S * n, TILE, D_MODEL), mod, lng, lnb)
    return out.reshape(S, T, D_MODEL)


def _rope_tables(n_lat):
    half = DIFF_QK // 2
    pairs = half // 2
    inv = (1.0 / (ROPE_BASE ** (np.arange(pairs, dtype=np.float32) * 2.0 / half))).astype(np.float32)
    t = np.arange(n_lat)
    ang_r = ((t // GRID_W).astype(np.float32)[:, None] * inv[None, :]).astype(np.float64)
    ang_c = ((t % GRID_W).astype(np.float32)[:, None] * inv[None, :]).astype(np.float64)
    cos = np.concatenate([np.cos(ang_r)] * 2 + [np.cos(ang_c)] * 2, axis=1)
    sin = np.concatenate([-np.sin(ang_r), np.sin(ang_r), -np.sin(ang_c), np.sin(ang_c)], axis=1)
    reps = DIFF_W // DIFF_QK
    return (jnp.asarray(np.tile(cos, (1, reps)), F32), jnp.asarray(np.tile(sin, (1, reps)), F32))


def _block_diag(s):
    S = s.shape[0]
    eye = jnp.eye(H_RET, dtype=s.dtype)
    return jnp.einsum('shdv,hg->shdgv', s, eye).reshape(S, RET_W, RET_W)


def _router_rows(w_group, b_group, w_expert, b_expert):
    pad = EXPERT_ROW0 - N_GROUPS
    tail = ROUTER_ROWS - EXPERT_ROW0 - N_EXPERTS
    w = jnp.concatenate([w_group.T, jnp.zeros((pad, D_MODEL), F32),
                         w_expert.reshape(D_MODEL, N_EXPERTS).T, jnp.zeros((tail, D_MODEL), F32)], axis=0)
    bias = jnp.concatenate([b_group, jnp.zeros((pad,), F32), b_expert.reshape(N_EXPERTS), jnp.zeros((tail,), F32)])
    hi = w.astype(BF)
    return hi, (w - hi.astype(F32)).astype(BF), bias.reshape(ROUTER_ROWS, 1)


def kernel(x_prompt, x_sample, cache_diff_k, cache_diff_v, state_ret_fwd, state_ret_bwd, c, c_ctx, w_mod, b_mod, w_in, ret_decay_logit, diff_lambda, diff_subln_g, conv_w, conv_b, w_out, ln_g, ln_b, w_router_group, b_router_group, w_router_expert, b_router_expert, w_gate, w_up, w_down):
    B, T_ctx, _ = x_prompt.shape
    Bd, T_lat, _ = x_sample.shape
    assert T_ctx == TILE and T_lat % TILE == 0 and T_lat % GRID_W == 0
    assert 1 + Bd <= MOD_ROWS

    cond = jnp.concatenate([c_ctx[None, :], c, jnp.zeros((MOD_ROWS - 1 - Bd, D_MODEL), F32)], axis=0)
    mod = _modulation(cond, w_mod, b_mod).reshape(DEPTH, MOD_ROWS, 1, 6 * D_MODEL)
    rope_tabs = _rope_tables(T_lat)
    w_in_b = w_in.astype(BF)
    w_out_b = w_out.astype(BF)

    ctx_row = lambda s: 0
    lat_row = lambda s: s + 1
    ctx_tiles = B * (T_ctx // TILE)
    lat_tiles = Bd * (T_lat // TILE)

    yp, ys = x_prompt, x_sample
    stacked = None
    for l in range(DEPTH):
        lam_init = 0.8 - 0.6 * math.exp(-0.3 * l)
        lg_lanes = jnp.repeat(ret_decay_logit[l], RET_DK, axis=1)
        wrh, wrl, br = _router_rows(w_router_group[l], b_router_group[l], w_router_expert[l], b_router_expert[l])
        shared = (w_out_b, lg_lanes, diff_lambda[l], diff_subln_g[l].reshape(1, DIFF_V), conv_w[l],
                  conv_b[l].reshape(1, CONV_CH), ln_g[l, 0].reshape(1, D_MODEL), ln_b[l, 0].reshape(1, D_MODEL),
                  wrh, wrl, br, lam_init)
        ln2 = (ln_g[l, 1].reshape(1, D_MODEL), ln_b[l, 1].reshape(1, D_MODEL))

        pr = _proj(yp, mod, l, ctx_row, w_in_b, lg_lanes, stacked=stacked)
        rq, rk, rv, rg, dq, dk, dv, cb, cch = pr[:9]
        stacked = pr[9:]
        x1_c, hs_c, slots_c, cnt_c = _mixer(yp, mod, l, ctx_row, (rq, rk, rv, rg, dq, cb, cch), None, None,
                                            (dk, dv), *shared)

        pr = _proj(ys, mod, l, lat_row, w_in_b, lg_lanes, rope_tabs=rope_tabs)
        rq, rk, rv, rg, dq, dk, dv, cb, cch, kvf, kvb = pr
        states = _scan(kvf, kvb, _block_diag(state_ret_fwd[:, l]), _block_diag(state_ret_bwd[:, l]), lg_lanes)
        cache_v = cache_diff_v[:, l].astype(BF)
        cache_kv = (cache_diff_k[:, l].reshape(Bd, -1, DIFF_W).astype(BF),
                    jnp.concatenate([cache_v, jnp.ones_like(cache_v)], axis=-1).reshape(Bd, -1, V_AUG_W))
        x1_l, hs_l, slots_l, cnt_l = _mixer(ys, mod, l, lat_row, (rq, rk, rv, rg, dq, cb, cch), states, cache_kv,
                                            (dk, dv), *shared)

        padded = jnp.concatenate([cnt_c.reshape(ctx_tiles, N_EXPERTS, LANES)[:, :, 0],
                                  cnt_l.reshape(lat_tiles, N_EXPERTS, LANES)[:, :, 0]], axis=0)
        eb, nv, cid = _block_schedule(padded, ctx_tiles * CHUNKS_PER_TILE)
        out_c, out_l = _moe(hs_c, hs_l, eb, nv, cid, l, w_gate, w_up, w_down)
        yp = _combine(out_c, slots_c, x1_c, mod, l, ctx_row, *ln2)
        ys = _combine(out_l, slots_l, x1_l, mod, l, lat_row, *ln2)

    new_k, new_v, st_f, st_b = stacked
    return (yp, ys, new_k.reshape(B, DEPTH, T_ctx, H_DIFF, 2, DIFF_QK),
            new_v.reshape(B, DEPTH, T_ctx, H_DIFF, DIFF_V), st_f, st_b)
```

```python
import functools
import math

import numpy as np
import jax
import jax.numpy as jnp
from jax import lax
from jax.experimental import pallas as pl
from jax.experimental.pallas import tpu as pltpu

D_MODEL = 1024
DEPTH = 2
GRID_W = 64
H_RET = 4
RET_DK = 64
RET_W = H_RET * RET_DK
H_DIFF = 4
DIFF_QK = 64
DIFF_V = 2 * DIFF_QK
DIFF_W = H_DIFF * DIFF_V
CONV_CH = 256
ROPE_BASE = 10000.0
N_GROUPS = 4
EXPERTS_PER_GROUP = 4
N_EXPERTS = N_GROUPS * EXPERTS_PER_GROUP
D_EXPERT = 512
ALPHA = (2 * DEPTH) ** 0.25
EPS = 1e-5
MOD_ROWS = 8
LANES = 128
ROUTER_ROWS = 32
EXPERT_ROW0 = 8

TILE = 256
CHUNK = 16
CAP = 2 * TILE + N_EXPERTS * CHUNK
CHUNKS_PER_TILE = CAP // CHUNK
CTX_SEQS_PER_STEP = 2
COMBINE_TILES_PER_STEP = 2
MOE_TMB = 512
CPB = MOE_TMB // CHUNK
HS_W = D_MODEL + LANES
V_AUG_W = 2 * DIFF_W
QK_SCALE_LOG2 = (DIFF_QK ** -0.5) * math.log2(math.e)
SLOT_RADIX = 32

_O_RQ, _O_RK, _O_RV, _O_RG = 0, 256, 512, 768
_O_DQ, _O_DK, _O_DV = 1024, 1536, 2048
_O_CB, _O_CC, _O_CH = 2560, 2816, 3072
IN_WIDTH = 3328

BF = jnp.bfloat16
F32 = jnp.float32
I32 = jnp.int32

_VMEM_LIMIT = 56 * 1024 * 1024


def _dot(a, b):
    return jnp.dot(a, b, preferred_element_type=F32)


def _dot_nt(a, b):
    return lax.dot_general(a, b, (((1,), (1,)), ((), ())), preferred_element_type=F32)


def _dot_tn(a, b):
    return lax.dot_general(a, b, (((0,), (0,)), ((), ())), preferred_element_type=F32)


def _split(x):
    hi = x.astype(BF)
    lo = (x - hi.astype(F32)).astype(BF)
    return hi, lo


def _dot_hl(x, w_bf16):
    hi, lo = _split(x)
    return _dot(hi, w_bf16) + _dot(lo, w_bf16)


def _dot3(x, w_hi, w_lo):
    hi, lo = _split(x)
    return _dot(hi, w_hi) + (_dot(lo, w_hi) + _dot(hi, w_lo))


def _iota(shape, dim):
    return lax.broadcasted_iota(I32, shape, dim)


def _ones_where(cond):
    return jnp.where(cond, 1.0, 0.0).astype(BF)


def _log_sigmoid(x):
    return jnp.minimum(x, 0.0) - jnp.log1p(jnp.exp(-jnp.abs(x)))


def _silu(x):
    return x * jax.nn.sigmoid(x)


def _layer_norm(y, g, b):
    mu = jnp.mean(y, axis=-1, keepdims=True)
    yc = y - mu
    var = jnp.mean(yc * yc, axis=-1, keepdims=True)
    return yc * lax.rsqrt(var + EPS) * g + b


def _params(sem):
    return pltpu.CompilerParams(dimension_semantics=sem, vmem_limit_bytes=_VMEM_LIMIT)


_HBM = pl.BlockSpec(memory_space=pltpu.HBM)


def _mod_kernel(c_ref, w_ref, b_ref, o_ref):
    a = _silu(c_ref[...])
    w_hi, w_lo = _split(w_ref[...])
    o_ref[...] = _dot3(a, w_hi, w_lo) + b_ref[...]


def _modulation(cond, w_mod, b_mod):
    tn = 1536
    n6 = 6 * D_MODEL
    return pl.pallas_call(
        _mod_kernel,
        grid=(DEPTH, n6 // tn),
        in_specs=[
            pl.BlockSpec((MOD_ROWS, D_MODEL), lambda l, j: (0, 0)),
            pl.BlockSpec((None, D_MODEL, tn), lambda l, j: (l, 0, j)),
            pl.BlockSpec((None, 1, tn), lambda l, j: (l, 0, j)),
        ],
        out_specs=pl.BlockSpec((None, MOD_ROWS, tn), lambda l, j: (l, 0, j)),
        out_shape=jax.ShapeDtypeStruct((DEPTH, MOD_ROWS, n6), F32),
        compiler_params=_params(("parallel", "parallel")),
        name="modulation",
    )(cond, w_mod, b_mod.reshape(DEPTH, 1, n6))


def _mod_spec(layer, row_fn):
    return pl.BlockSpec((None, None, 1, 6 * D_MODEL), lambda s, i: (layer, row_fn(s), 0, 0))


def _augment_v(v):
    ones = jnp.ones((v.shape[0], DIFF_V), v.dtype)
    parts = []
    for hd in range(H_DIFF):
        parts += [v[:, hd * DIFF_V:(hd + 1) * DIFF_V], ones]
    return jnp.concatenate(parts, axis=1)


def _swap16(x):
    n = x.shape[-1]
    lane = _iota(x.shape, 1)
    return jnp.where((lane & 16) == 0, pltpu.roll(x, n - 16, axis=1), pltpu.roll(x, 16, axis=1))


def _proj_kernel(*refs, latent, n_aliased, layer):
    x_ref, mod_ref, w_ref, lg_ref = refs[:4]
    pos = 4
    if latent:
        cos_ref, sin_ref = refs[pos:pos + 2]
        pos += 2
    pos += n_aliased
    rq_ref, rk_ref, rv_ref, rg_ref, dq_ref, dk_ref, dv_ref, cb_ref, cch_ref = refs[pos:pos + 9]
    pos += 9
    if latent:
        kvf_ref, kvb_ref = refs[pos:pos + 2]
    else:
        dk32_ref, dv32_ref, stf_ref, stb_ref = refs[pos:pos + 4]

    def put(ref, idx, val):
        if n_aliased:
            ref[idx if idx else ...] = val
        else:
            for other in range(DEPTH):
                ref[(other,) + idx] = val if other == layer else jnp.zeros_like(val)

    m = mod_ref[...]
    sh1 = m[:, 0:D_MODEL]
    sc1 = m[:, D_MODEL:2 * D_MODEL]
    h = (x_ref[...] * (1.0 + sc1) + sh1).astype(BF)

    def col(off, width):
        return _dot(h, w_ref[:, off:off + width])

    rq_ref[...] = col(_O_RQ, RET_W).astype(BF)
    rk = col(_O_RK, RET_W) * (RET_DK ** -0.5)
    rk_ref[...] = rk.astype(BF)
    rv = col(_O_RV, RET_W).astype(BF)
    rv_ref[...] = rv
    rg_ref[...] = col(_O_RG, RET_W)

    lg = _log_sigmoid(lg_ref[...])
    p = _iota((TILE, 1), 0).astype(F32)
    kf = (rk * jnp.exp((TILE - 1.0 - p) * lg[0:1])).astype(BF)
    kb = (rk * jnp.exp(p * lg[1:2])).astype(BF)
    kvf = _dot_tn(kf, rv)
    kvb = _dot_tn(kb, rv)
    if latent:
        kvf_ref[...] = kvf
        kvb_ref[...] = kvb
    else:
        for hd in range(H_RET):
            lo = hd * RET_DK
            put(stf_ref, (hd,), kvf[lo:lo + RET_DK, lo:lo + RET_DK])
            put(stb_ref, (hd,), kvb[lo:lo + RET_DK, lo:lo + RET_DK])

    dq = col(_O_DQ, DIFF_W)
    dk = col(_O_DK, DIFF_W)
    dv = col(_O_DV, DIFF_W)
    if latent:
        cos = cos_ref[...]
        sin = sin_ref[...]
        dq = dq * cos + _swap16(dq) * sin
        dk = dk * cos + _swap16(dk) * sin
    else:
        put(dk32_ref, (), dk)
        put(dv32_ref, (), dv)
    dq_ref[...] = (dq * QK_SCALE_LOG2).astype(BF)
    dk_ref[...] = dk.astype(BF)
    dv_ref[...] = _augment_v(dv.astype(BF))

    cb_ref[...] = col(_O_CB, CONV_CH)
    cch_ref[...] = col(_O_CC, CONV_CH) * col(_O_CH, CONV_CH)


def _proj(x, mod, layer, row_fn, w_in_b, lg_lanes, rope_tabs=None, stacked=None):
    S, T, _ = x.shape
    n = T // TILE
    latent = rope_tabs is not None

    def tok(w):
        return pl.BlockSpec((None, TILE, w), lambda s, i: (s, i, 0))

    def tok_shape(w, dt):
        return jax.ShapeDtypeStruct((S, T, w), dt)

    in_specs = [
        tok(D_MODEL), _mod_spec(layer, row_fn),
        pl.BlockSpec((None, D_MODEL, IN_WIDTH), lambda s, i: (layer, 0, 0)),
        pl.BlockSpec((2, RET_W), lambda s, i: (0, 0)),
    ]
    args = [x, mod, w_in_b, lg_lanes]
    out_specs = [tok(RET_W)] * 4 + [tok(DIFF_W)] * 2 + [tok(V_AUG_W)] + [tok(CONV_CH)] * 2
    out_shape = ([tok_shape(RET_W, BF)] * 3 + [tok_shape(RET_W, F32)] + [tok_shape(DIFF_W, BF)] * 2
                 + [tok_shape(V_AUG_W, BF)]
                 + [tok_shape(CONV_CH, F32)] * 2)
    aliases = {}
    if latent:
        in_specs += [pl.BlockSpec((TILE, DIFF_W), lambda s, i: (i, 0))] * 2
        args += list(rope_tabs)
        out_specs += [pl.BlockSpec((None, None, RET_W, RET_W), lambda s, i: (s, i, 0, 0))] * 2
        out_shape += [jax.ShapeDtypeStruct((S, n, RET_W, RET_W), F32)] * 2
    else:
        assert n == 1
        kv_shape = jax.ShapeDtypeStruct((S, DEPTH, T, DIFF_W), F32)
        st_shape = jax.ShapeDtypeStruct((S, DEPTH, H_RET, RET_DK, RET_DK), F32)
        if stacked is not None:
            in_specs += [_HBM] * 4
            args += list(stacked)
            aliases = {len(args) - 4 + j: len(out_shape) + j for j in range(4)}
            out_specs += [pl.BlockSpec((None, None, TILE, DIFF_W), lambda s, i: (s, layer, i, 0))] * 2
            out_specs += [pl.BlockSpec((None, None, H_RET, RET_DK, RET_DK), lambda s, i: (s, layer, 0, 0, 0))] * 2
        else:
            out_specs += [pl.BlockSpec((None, DEPTH, TILE, DIFF_W), lambda s, i: (s, 0, i, 0))] * 2
            out_specs += [pl.BlockSpec((None, DEPTH, H_RET, RET_DK, RET_DK), lambda s, i: (s, 0, 0, 0, 0))] * 2
        out_shape += [kv_shape, kv_shape, st_shape, st_shape]
    return pl.pallas_call(
        functools.partial(_proj_kernel, latent=latent, n_aliased=len(aliases), layer=layer),
        grid=(S, n),
        in_specs=in_specs,
        out_specs=out_specs,
        out_shape=out_shape,
        input_output_aliases=aliases,
        compiler_params=_params(("parallel", "parallel")),
        name="proj_lat" if latent else "proj_ctx",
    )(*args)


def _scan_kernel(kvf_ref, kvb_ref, s0f_ref, s0b_ref, lg_ref, sf_ref, sb_ref, *, n):
    lg = _log_sigmoid(lg_ref[...])
    dec = jnp.exp(float(TILE) * lg)
    same_head = (_iota((RET_W, RET_W), 0) >> 6) == (_iota((RET_W, RET_W), 1) >> 6)

    s = jnp.where(same_head, s0f_ref[...], 0.0)
    for c in range(n):
        sf_ref[c] = s
        s = s * dec[0:1] + jnp.where(same_head, kvf_ref[c], 0.0)
    s = jnp.where(same_head, s0b_ref[...], 0.0)
    for c in reversed(range(n)):
        sb_ref[c] = s
        s = s * dec[1:2] + jnp.where(same_head, kvb_ref[c], 0.0)


def _scan(kvf, kvb, s0f, s0b, lg_lanes):
    S, n = kvf.shape[:2]
    chunks = pl.BlockSpec((None, n, RET_W, RET_W), lambda s: (s, 0, 0, 0))
    one = pl.BlockSpec((None, RET_W, RET_W), lambda s: (s, 0, 0))
    return pl.pallas_call(
        functools.partial(_scan_kernel, n=n),
        grid=(S,),
        in_specs=[chunks, chunks, one, one, pl.BlockSpec((2, RET_W), lambda s: (0, 0))],
        out_specs=[chunks, chunks],
        out_shape=[jax.ShapeDtypeStruct((S, n, RET_W, RET_W), F32)] * 2,
        compiler_params=_params(("parallel",)),
        name="ret_scan",
    )(kvf, kvb, s0f, s0b, lg_lanes)


def _diff_heads(dq_ref, kv_refs, heads, lam, combine_first):
    lane = _iota((1, DIFF_V), 1)
    scores = []
    for hd in heads:
        lo = hd * DIFF_V
        q_h = dq_ref[:, lo:lo + DIFF_V]
        qs = jnp.concatenate([jnp.where(lane < DIFF_QK, q_h, 0), jnp.where(lane >= DIFF_QK, q_h, 0)], axis=0)
        scores.append(jnp.concatenate([_dot_nt(qs, k[:, lo:lo + DIFF_V]) for k, _ in kv_refs], axis=1))
    s = jnp.concatenate(scores, axis=0)
    p = jnp.exp2(s - jnp.max(s, axis=-1, keepdims=True))

    def times_v(w, hd, width):
        acc = None
        start = 0
        for k, v in kv_refs:
            part = _dot(w[:, start:start + k.shape[0]], v[:, hd * 2 * DIFF_V:hd * 2 * DIFF_V + width])
            acc = part if acc is None else acc + part
            start += k.shape[0]
        return acc

    outs = []
    if combine_first:
        scale = jnp.where((_iota((p.shape[0], 1), 0) & (2 * TILE - 1)) < TILE, 1.0, lam)
        pn = p * (scale / jnp.sum(p, axis=-1, keepdims=True))
        for n, hd in enumerate(heads):
            a = pn[2 * n * TILE:(2 * n + 1) * TILE] - pn[(2 * n + 1) * TILE:(2 * n + 2) * TILE]
            outs.append(times_v(a.astype(BF), hd, DIFF_V))
        return outs
    p = p.astype(BF)
    for n, hd in enumerate(heads):
        acc = times_v(p[n * 2 * TILE:(n + 1) * 2 * TILE], hd, 2 * DIFF_V)
        o0 = acc[0:TILE, 0:DIFF_V] * (1.0 / acc[0:TILE, DIFF_V:DIFF_V + 1])
        o1 = acc[TILE:, 0:DIFF_V] * (lam / acc[TILE:, DIFF_V:DIFF_V + 1])
        outs.append(o0 - o1)
    return outs


def _mixer_kernel(*refs, n_sub, split, **static):
    if n_sub == 1:
        return _mixer_tile(*refs, **static)
    for u in range(n_sub):
        views = [r.at[u] if how == 1 else (r.at[pl.ds(u * CAP, CAP)] if how == 2 else r)
                 for r, how in zip(refs, split)]
        _mixer_tile(*views, **static)


def _mixer_pipe_kernel(*refs, n_tiles, n_total, lam_init, cached):
    n_a = _n_token_mix_refs(cached)
    mix_even, mix_odd = refs[-2:]
    t = pl.program_id(0)
    tile_in_seq = lax.rem(jnp.minimum(t, n_total - 1), n_tiles)

    @pl.when(t == 0)
    def _():
        mix_odd[...] = jnp.zeros_like(mix_odd)

    def step(mix_write, mix_read):
        mix_write[...] = _token_mix(*refs[:n_a], tile_in_seq=tile_in_seq, n_tiles=n_tiles,
                                    lam_init=lam_init, cached=cached)
        _project_route(mix_read[...], *refs[n_a:-2])

    @pl.when(lax.rem(t, 2) == 0)
    def _():
        step(mix_even, mix_odd)

    @pl.when(lax.rem(t, 2) == 1)
    def _():
        step(mix_odd, mix_even)


def _n_token_mix_refs(cached):
    return 18 + (2 if cached else 0)


def _mixer_tile(*refs, n_tiles, lam_init, cached):
    n_a = _n_token_mix_refs(cached)
    mixed = _token_mix(*refs[:n_a], tile_in_seq=pl.program_id(1), n_tiles=n_tiles, lam_init=lam_init,
                       cached=cached)
    _project_route(mixed, *refs[n_a:])


def _token_mix(*refs, tile_in_seq, n_tiles, lam_init, cached):
    rq_ref, rk_ref, rv_ref, rg_ref, sf_ref, sb_ref, dq_ref = refs[:7]
    pos = 7
    kv_refs = []
    if cached:
        kv_refs.append((refs[pos], refs[pos + 1]))
        pos += 2
    kv_refs.append((refs[pos], refs[pos + 1]))
    pos += 2
    cb_ref, cch_ref, cprev_ref, cnext_ref, lg_ref, lamp_ref, subg_ref, cw_ref, cbias_ref = refs[pos:pos + 9]
    i = tile_in_seq

    lg = _log_sigmoid(lg_ref[...])
    head_of_lane = _iota((1, RET_W), 1) >> 6
    q = rq_ref[...]
    k = rk_ref[...]
    v = rv_ref[...]
    dist = (_iota((TILE, TILE), 0) - _iota((TILE, TILE), 1)).astype(F32)
    adist = jnp.abs(dist)
    diag2 = jnp.where(dist == 0.0, 2.0, 1.0)
    ret_o = jnp.zeros((TILE, RET_W), F32)
    for hd in range(H_RET):
        in_head = head_of_lane == hd
        sc = _dot_nt(jnp.where(in_head, q, 0), k)
        lgf = lg[0:1, hd * RET_DK:hd * RET_DK + 1]
        lgb = lg[1:2, hd * RET_DK:hd * RET_DK + 1]
        decay = jnp.exp(adist * jnp.where(dist > 0.0, lgf, lgb)) * diag2
        ret_o = ret_o + jnp.where(in_head, _dot((sc * decay).astype(BF), v), 0.0)
    p = _iota((TILE, 1), 0).astype(F32)
    ret_o = ret_o + _dot(q, sf_ref[...].astype(BF)) * jnp.exp((p + 1.0) * lg[0:1])
    ret_o = ret_o + _dot(q, sb_ref[...].astype(BF)) * jnp.exp((float(TILE) - p) * lg[1:2])
    avg = jnp.where((_iota((RET_W, RET_W), 0) >> 6) == (_iota((RET_W, RET_W), 1) >> 6),
                    1.0 / RET_DK, 0.0).astype(BF)
    rc = ret_o - _dot_hl(ret_o, avg)
    ret = rc * lax.rsqrt(_dot_hl(rc * rc, avg) + EPS) * _silu(rg_ref[...])

    lp = lamp_ref[...]
    lam = (jnp.exp(jnp.sum(lp[0:1] * lp[1:2], axis=-1, keepdims=True))
           - jnp.exp(jnp.sum(lp[2:3] * lp[3:4], axis=-1, keepdims=True)) + lam_init)
    subg = subg_ref[...] * (1.0 - lam_init)
    head_groups = [[hd] for hd in range(H_DIFF)] if cached else [list(range(H_DIFF))]
    heads = []
    for group in head_groups:
        for o in _diff_heads(dq_ref, kv_refs, group, lam, combine_first=cached):
            o = o * lax.rsqrt(jnp.mean(o * o, axis=-1, keepdims=True) + EPS) * subg
            heads.append(o.astype(BF))
    diff = jnp.concatenate(heads, axis=1)

    cch = cch_ref[...]
    prev = jnp.where(i > 0, cprev_ref[7:8, :], 0.0)
    nxt = jnp.where(i < n_tiles - 1, cnext_ref[0:1, :], 0.0)
    r = _iota((TILE, 1), 0)
    up = jnp.where(r == 0, prev, pltpu.roll(cch, 1, axis=0))
    dn = jnp.where(r == TILE - 1, nxt, pltpu.roll(cch, TILE - 1, axis=0))
    cw = cw_ref[...]
    conv = cb_ref[...] * (up * cw[0:1] + cch * cw[1:2] + dn * cw[2:3] + cbias_ref[...])
    return jnp.concatenate([ret.astype(BF), diff, conv.astype(BF)], axis=1)


def _project_route(mixed, x_ref, mod_ref, wout_ref, lng_ref, lnb_ref, wrh_ref, wrl_ref, br_ref,
                   x1_ref, hs_ref, slots_ref, cnt_ref):
    m = mod_ref[...]
    g1 = m[:, 2 * D_MODEL:3 * D_MODEL]
    sh2 = m[:, 3 * D_MODEL:4 * D_MODEL]
    sc2 = m[:, 4 * D_MODEL:5 * D_MODEL]
    x1 = _layer_norm(ALPHA * x_ref[...] + g1 * _dot(mixed, wout_ref[...]), lng_ref[...], lnb_ref[...])
    x1_ref[...] = x1

    h2 = x1 * (1.0 + sc2) + sh2
    h2h, h2l = _split(h2)
    wrh = wrh_ref[...]
    logits = (_dot_nt(wrh, h2h) + (_dot_nt(wrh, h2l) + _dot_nt(wrl_ref[...], h2h))) + br_ref[...]
    neg = -jnp.inf
    g_id = _iota((EXPERT_ROW0, TILE), 0)
    gl = jnp.where(g_id < N_GROUPS, logits[0:EXPERT_ROW0], neg)
    gmax = jnp.max(gl, axis=0, keepdims=True)
    g_idx = jnp.min(jnp.where(gl == gmax, g_id, N_GROUPS), axis=0, keepdims=True)
    g_w = 1.0 / jnp.sum(jnp.exp(gl - gmax), axis=0, keepdims=True)
    e_id = _iota((N_EXPERTS, TILE), 0)
    el = jnp.where((e_id >> 2) == g_idx, logits[EXPERT_ROW0:EXPERT_ROW0 + N_EXPERTS], neg)
    v1 = jnp.max(el, axis=0, keepdims=True)
    i1 = jnp.min(jnp.where(el == v1, e_id, N_EXPERTS), axis=0, keepdims=True)
    el2 = jnp.where(e_id == i1, neg, el)
    v2 = jnp.max(el2, axis=0, keepdims=True)
    i2 = jnp.min(jnp.where(el2 == v2, e_id, N_EXPERTS), axis=0, keepdims=True)
    t = jnp.exp(v2 - v1)
    w1 = g_w / (1.0 + t)
    w2 = g_w * t / (1.0 + t)

    sel1 = e_id == i1
    sel2 = e_id == i2
    sel = jnp.where(sel1 | sel2, 1.0, 0.0)
    earlier = _ones_where(_iota((TILE, TILE), 0) < _iota((TILE, TILE), 1))
    rank = _dot(sel.astype(BF), earlier)
    cnt = jnp.sum(sel, axis=1, keepdims=True).astype(I32)
    padded = ((cnt + (CHUNK - 1)) >> 4) << 4
    incl = jnp.broadcast_to(padded, (N_EXPERTS, LANES))
    e_row = _iota((N_EXPERTS, LANES), 0)
    for step in (1, 2, 4, 8):
        incl = incl + jnp.where(e_row >= step, pltpu.roll(incl, step, axis=0), 0)
    seg_off = (incl[:, 0:1] - padded).astype(F32)
    spos = seg_off + rank
    slot1 = jnp.sum(jnp.where(sel1, spos, 0.0), axis=0, keepdims=True).astype(I32)
    slot2 = jnp.sum(jnp.where(sel2, spos, 0.0), axis=0, keepdims=True).astype(I32)
    s_id = _iota((CAP, TILE), 0)
    at1 = s_id == slot1
    at2 = s_id == slot2
    hs_ref[:, 0:D_MODEL] = _dot(_ones_where(at1 | at2), h2h).astype(BF)
    w1h = w1.astype(BF).astype(F32)
    w2h = w2.astype(BF).astype(F32)
    ones = jnp.ones((TILE, LANES), BF)
    g_hi = _dot((jnp.where(at1, w1h, 0.0) + jnp.where(at2, w2h, 0.0)).astype(BF), ones)
    g_lo = _dot((jnp.where(at1, w1 - w1h, 0.0) + jnp.where(at2, w2 - w2h, 0.0)).astype(BF), ones)
    lane = _iota((CAP, LANES), 1)
    hs_ref[:, D_MODEL:] = jnp.where(lane == 0, g_hi, jnp.where(lane == 1, g_lo, 0.0)).astype(BF)
    cnt_ref[...] = jnp.broadcast_to(padded, (N_EXPERTS, LANES))

    dr = _iota((LANES, TILE), 0)
    digits = jnp.where(dr == 0, slot1 & (SLOT_RADIX - 1), jnp.where(dr == 1, slot1 >> 5,
             jnp.where(dr == 2, slot2 & (SLOT_RADIX - 1), jnp.where(dr == 3, slot2 >> 5, 0))))
    eye = _ones_where(_iota((TILE, TILE), 0) == _iota((TILE, TILE), 1))
    cols = _dot_nt(eye, digits.astype(F32).astype(BF))
    s1c = (cols[:, 0:1] + SLOT_RADIX * cols[:, 1:2]).astype(I32)
    s2c = (cols[:, 2:3] + SLOT_RADIX * cols[:, 3:4]).astype(I32)
    tl = _iota((TILE, LANES), 1)
    slots_ref[...] = jnp.where(tl == 0, s1c, jnp.where(tl == 1, s2c, 0))


def _mixer(x, mod, layer, row_fn, pr, states, cache_kv, new_kv, w_out_b, lg_lanes, lamp, subg, cw, cbias,
           lng, lnb, wrh, wrl, br, lam_init):
    S, T, _ = x.shape
    n = T // TILE
    rq, rk, rv, rg, dq, cb, cch = pr
    t8 = TILE // 8
    cached = cache_kv is not None
    pipelined = True
    n_sub = CTX_SEQS_PER_STEP if (not pipelined and S % CTX_SEQS_PER_STEP == 0) else 1
    lead = None if n_sub == 1 else n_sub
    SEQ, SHARED, ROWS = 1, 0, 2

    if pipelined:
        last = S * n - 1
        grid = (S * n + 1,)
        at_mix = lambda t: (jnp.minimum(t, last) // n, jnp.minimum(t, last) % n)
        at_out = lambda t: (jnp.maximum(t - 1, 0) // n, jnp.maximum(t - 1, 0) % n)
    else:
        grid = (S // n_sub, n)
        at_mix = at_out = lambda s, i: (s, i)

    def tok(w, at):
        return pl.BlockSpec((lead, TILE, w), lambda *g: (*at(*g), 0))

    def full(shape):
        return pl.BlockSpec(shape, lambda *g: (0,) * len(shape))

    def seq(a):
        mode = pl.Buffered(1) if n > 1 else None
        return pl.BlockSpec((lead,) + a.shape[1:], lambda *g: (at_mix(*g)[0], 0, 0), pipeline_mode=mode)

    def halo(offset):
        def index(*g):
            s, i = at_mix(*g)
            return (s, jnp.clip(i * t8 + offset, 0, T // 8 - 1), 0)
        return pl.BlockSpec((lead, 8, CONV_CH), index)

    if states is None:
        zero_state = jnp.zeros((1, 1, RET_W, RET_W), F32)
        states = (zero_state, zero_state)
        state = pl.BlockSpec((None, None, RET_W, RET_W), lambda *g: (0, 0, 0, 0))
    else:
        assert n_sub == 1
        state = pl.BlockSpec((None, None, RET_W, RET_W), lambda *g: (*at_mix(*g), 0, 0))

    in_specs = [tok(RET_W, at_mix)] * 4 + [state, state, tok(DIFF_W, at_mix)]
    split = [SEQ] * 4 + [SHARED, SHARED, SEQ]
    args = [rq, rk, rv, rg, states[0], states[1], dq]
    for kv in ([cache_kv] if cached else []) + [new_kv]:
        in_specs += [seq(kv[0]), seq(kv[1])]
        split += [SEQ, SEQ]
        args += list(kv)
    in_specs += [tok(CONV_CH, at_mix), tok(CONV_CH, at_mix), halo(-1), halo(t8),
                 full((2, RET_W)), full((4, DIFF_QK)), full((1, DIFF_V)), full((3, CONV_CH)), full((1, CONV_CH))]
    split += [SEQ] * 4 + [SHARED] * 5
    args += [cb, cch, cch, cch, lg_lanes, lamp, subg, cw, cbias]
    assert len(args) == _n_token_mix_refs(cached)
    in_specs += [tok(D_MODEL, at_out),
                 pl.BlockSpec((None, None, 1, 6 * D_MODEL), lambda *g: (layer, row_fn(at_out(*g)[0]), 0, 0)),
                 pl.BlockSpec((None, D_MODEL, D_MODEL), lambda *g: (layer, 0, 0)),
                 full((1, D_MODEL)), full((1, D_MODEL)),
                 full((ROUTER_ROWS, D_MODEL)), full((ROUTER_ROWS, D_MODEL)), full((ROUTER_ROWS, 1))]
    split += [SEQ] + [SHARED] * 7
    args += [x, mod, w_out_b, lng, lnb, wrh, wrl, br]
    split += [SEQ, ROWS, SEQ, SEQ]

    def out_tile(*g):
        s, i = at_out(*g)
        return s * n + i

    static = dict(n_tiles=n, lam_init=lam_init, cached=cached)
    if pipelined:
        body = functools.partial(_mixer_pipe_kernel, n_total=S * n, **static)
        scratch = [pltpu.VMEM((TILE, D_MODEL), BF)] * 2
        semantics = ("arbitrary",)
    else:
        body = functools.partial(_mixer_kernel, n_sub=n_sub, split=tuple(split), **static)
        scratch = []
        semantics = ("parallel", "parallel")
    return pl.pallas_call(
        body,
        grid=grid,
        in_specs=in_specs,
        out_specs=[tok(D_MODEL, at_out),
                   pl.BlockSpec((n_sub * CAP, HS_W), lambda *g: (out_tile(*g), 0)),
                   tok(LANES, at_out),
                   pl.BlockSpec((lead, None, N_EXPERTS, LANES), lambda *g: (*at_out(*g), 0, 0))],
        out_shape=[jax.ShapeDtypeStruct((S, T, D_MODEL), F32),
                   jax.ShapeDtypeStruct((S * n * CAP, HS_W), BF),
                   jax.ShapeDtypeStruct((S, T, LANES), I32),
                   jax.ShapeDtypeStruct((S, n, N_EXPERTS, LANES), I32)],
        scratch_shapes=scratch,
        compiler_params=_params(semantics),
        name="mixer_lat" if cached else "mixer_ctx",
    )(*args)


def _block_schedule(padded, chunks_a):
    NT = padded.shape[0]
    n_chunks = NT * CHUNKS_PER_TILE
    n_blocks = n_chunks // CPB + N_EXPERTS
    c16 = padded // CHUNK
    ends = jnp.cumsum(c16, axis=1)
    before = jnp.cumsum(c16, axis=0) - c16
    per_expert = jnp.sum(c16, axis=0)
    nb = (per_expert + (CPB - 1)) // CPB
    b_end = jnp.cumsum(nb)
    q = jnp.arange(CHUNKS_PER_TILE, dtype=I32)
    e = jnp.arange(N_EXPERTS, dtype=I32)
    key = jnp.sum((ends[:, None, :] <= q[None, :, None]).astype(I32), axis=-1)
    pos_e = ((b_end - nb) * CPB + before)[:, None, :] + (q[None, :, None] - (ends - c16)[:, None, :])
    pos = jnp.sum(jnp.where(key[..., None] == e, pos_e, 0), axis=-1)
    pos = jnp.where(key < N_EXPERTS, pos, -1).reshape(-1)
    match = pos[None, :] == jnp.arange(n_blocks * CPB, dtype=I32)[:, None]
    cid = jnp.sum(jnp.where(match, jnp.arange(n_chunks, dtype=I32)[None, :], 0), axis=-1)
    nv = jnp.sum(match.astype(I32).reshape(n_blocks, -1), axis=-1)
    first = match & (jnp.arange(n_chunks, dtype=I32) < chunks_a)[None, :]
    nv_a = jnp.sum(first.astype(I32).reshape(n_blocks, -1), axis=-1)
    b = jnp.arange(n_blocks, dtype=I32)
    eb = jnp.minimum(jnp.sum((b_end[None, :] <= b[:, None]).astype(I32), axis=-1), N_EXPERTS - 1)
    return eb.astype(I32), jnp.concatenate([nv, nv_a]).astype(I32), cid.astype(I32)


def _moe_kernel(eb_ref, nv_ref, cid_ref, hs_a, hs_b, wg_ref, wu_ref, wd_ref, ys_a, ys_b,
                xbuf, ybuf, wg_b, wu_b, wd_b, in_sem, out_sem, *, n_blocks, chunks_a):
    b = pl.program_id(0)

    def chunk_rows(c):
        return pl.ds(pl.multiple_of(c * CHUNK, CHUNK), CHUNK)

    def gather_copy(which, c, j, slot):
        src = (hs_a, hs_b)[which]
        return pltpu.make_async_copy(src.at[chunk_rows(c)], xbuf.at[slot, chunk_rows(j)], in_sem.at[slot])

    def scatter_copy(which, c, j, slot):
        dst = (ys_a, ys_b)[which]
        return pltpu.make_async_copy(ybuf.at[slot, chunk_rows(j)],
                                     dst.at[chunk_rows(c), pl.ds(0, D_MODEL)], out_sem.at[slot])

    def for_chunks(blk, fn):
        n_a = nv_ref[n_blocks + blk]

        def body_a(j, carry):
            fn(j, 0, cid_ref[blk * CPB + j])
            return carry

        def body_b(j, carry):
            fn(j, 1, cid_ref[blk * CPB + j] - chunks_a)
            return carry
        lax.fori_loop(0, n_a, body_a, 0)
        lax.fori_loop(n_a, nv_ref[blk], body_b, 0)

    def wait_gathers(blk, slot):
        rows = pl.ds(0, nv_ref[blk] * CHUNK)

        @pl.when(nv_ref[blk] > 0)
        def _():
            pltpu.make_async_copy(hs_a.at[rows], xbuf.at[slot, rows], in_sem.at[slot]).wait()

    def wait_scatters(blk, slot):
        rows = pl.ds(0, nv_ref[blk] * CHUNK)

        @pl.when(nv_ref[blk] > 0)
        def _():
            pltpu.make_async_copy(ybuf.at[slot, rows], ys_a.at[rows, pl.ds(0, D_MODEL)], out_sem.at[slot]).wait()

    slot = lax.rem(b, 2)
    other = 1 - slot

    @pl.when(b == 0)
    def _():
        xbuf[...] = jnp.zeros_like(xbuf)
        for_chunks(0, lambda j, w, c: gather_copy(w, c, j, 0).start())

    @pl.when(b + 1 < n_blocks)
    def _():
        for_chunks(b + 1, lambda j, w, c: gather_copy(w, c, j, other).start())

    @pl.when((b == 0) | (eb_ref[b] != eb_ref[jnp.maximum(b - 1, 0)]))
    def _():
        wg_b[...] = wg_ref[...].astype(BF)
        wu_b[...] = wu_ref[...].astype(BF)
        wd_b[...] = wd_ref[...].astype(BF)

    wait_gathers(b, slot)

    @pl.when(b >= 2)
    def _():
        wait_scatters(b - 2, slot)

    @pl.when(nv_ref[b] > 0)
    def _():
        x = xbuf[slot, :, 0:D_MODEL]
        gate = (xbuf[slot, :, D_MODEL:D_MODEL + 1].astype(F32)
                + xbuf[slot, :, D_MODEL + 1:D_MODEL + 2].astype(F32))
        hg = _dot(x, wg_b[...])
        hu = _dot(x, wu_b[...])
        act = _silu(hg) * hu * gate
        ybuf[slot] = _dot(act.astype(BF), wd_b[...]).astype(BF)

    for_chunks(b, lambda j, w, c: scatter_copy(w, c, j, slot).start())

    @pl.when(b == n_blocks - 1)
    def _():
        if n_blocks >= 2:
            wait_scatters(b - 1, other)
        wait_scatters(b, slot)


def _moe(hs_a, hs_b, eb, nv, cid, layer, w_gate, w_up, w_down):
    n_blocks = eb.shape[0]

    def w_spec(a, c):
        return pl.BlockSpec((None, None, a, c), lambda b, eb, nv, cid: (layer, eb[b], 0, 0))

    grid_spec = pltpu.PrefetchScalarGridSpec(
        num_scalar_prefetch=3,
        grid=(n_blocks,),
        in_specs=[_HBM, _HBM, w_spec(D_MODEL, D_EXPERT), w_spec(D_MODEL, D_EXPERT), w_spec(D_EXPERT, D_MODEL)],
        out_specs=[_HBM, _HBM],
        scratch_shapes=[pltpu.VMEM((2, MOE_TMB, HS_W), BF), pltpu.VMEM((2, MOE_TMB, D_MODEL), BF),
                        pltpu.VMEM((D_MODEL, D_EXPERT), BF), pltpu.VMEM((D_MODEL, D_EXPERT), BF),
                        pltpu.VMEM((D_EXPERT, D_MODEL), BF),
                        pltpu.SemaphoreType.DMA((2,)), pltpu.SemaphoreType.DMA((2,))],
    )
    return pl.pallas_call(
        functools.partial(_moe_kernel, n_blocks=n_blocks, chunks_a=hs_a.shape[0] // CHUNK),
        grid_spec=grid_spec,
        out_shape=[jax.ShapeDtypeStruct(hs_a.shape, BF), jax.ShapeDtypeStruct(hs_b.shape, BF)],
        input_output_aliases={3: 0, 4: 1},
        compiler_params=_params(("arbitrary",)),
        name="moe_sorted",
    )(eb, nv, cid, hs_a, hs_b, w_gate, w_up, w_down)


def _combine_kernel(ys_ref, slots_ref, x1_ref, mod_ref, lng_ref, lnb_ref, o_ref):
    g2 = mod_ref[:, 5 * D_MODEL:6 * D_MODEL]
    s_lane = _iota((TILE, CAP), 1)
    for u in range(COMBINE_TILES_PER_STEP):
        sl = slots_ref[u]
        pick = _ones_where((s_lane == sl[:, 0:1]) | (s_lane == sl[:, 1:2]))
        ffn = _dot(pick, ys_ref[u * CAP:(u + 1) * CAP, 0:D_MODEL])
        o_ref[u] = _layer_norm(ALPHA * x1_ref[u] + g2 * ffn, lng_ref[...], lnb_ref[...])


def _combine(ys, slots, x1, mod, layer, row_fn, lng, lnb):
    S, T, _ = x1.shape
    n = T // TILE
    U = COMBINE_TILES_PER_STEP
    assert (S * n) % U == 0 and (n % U == 0 or n == 1)

    def tok(w):
        return pl.BlockSpec((U, TILE, w), lambda t: (t, 0, 0))

    out = pl.pallas_call(
        _combine_kernel,
        grid=(S * n // U,),
        in_specs=[
            pl.BlockSpec((U * CAP, HS_W), lambda t: (t, 0)),
            tok(LANES), tok(D_MODEL),
            pl.BlockSpec((None, None, 1, 6 * D_MODEL), lambda t: (layer, row_fn(t * U // n), 0, 0)),
            pl.BlockSpec((1, D_MODEL), lambda t: (0, 0)),
            pl.BlockSpec((1, D_MODEL), lambda t: (0, 0)),
        ],
        out_specs=tok(D_MODEL),
        out_shape=jax.ShapeDtypeStruct((S * n, TILE, D_MODEL), F32),
        compiler_params=_params(("parallel",)),
        name="moe_combine",
    )(ys, slots.reshape(S * n, TILE, LANES), x1.reshape(S * n, TILE, D_MODEL), mod, lng, lnb)
    return out.reshape(S, T, D_MODEL)


def _rope_tables(n_lat):
    half = DIFF_QK // 2
    pairs = half // 2
    inv = (1.0 / (ROPE_BASE ** (np.arange(pairs, dtype=np.float32) * 2.0 / half))).astype(np.float32)
    t = np.arange(n_lat)
    ang_r = ((t // GRID_W).astype(np.float32)[:, None] * inv[None, :]).astype(np.float64)
    ang_c = ((t % GRID_W).astype(np.float32)[:, None] * inv[None, :]).astype(np.float64)
    cos = np.concatenate([np.cos(ang_r)] * 2 + [np.cos(ang_c)] * 2, axis=1)
    sin = np.concatenate([-np.sin(ang_r), np.sin(ang_r), -np.sin(ang_c), np.sin(ang_c)], axis=1)
    reps = DIFF_W // DIFF_QK
    return (jnp.asarray(np.tile(cos, (1, reps)), F32), jnp.asarray(np.tile(sin, (1, reps)), F32))


def _block_diag(s):
    S = s.shape[0]
    eye = jnp.eye(H_RET, dtype=s.dtype)
    return jnp.einsum('shdv,hg->shdgv', s, eye).reshape(S, RET_W, RET_W)


def _router_rows(w_group, b_group, w_expert, b_expert):
    pad = EXPERT_ROW0 - N_GROUPS
    tail = ROUTER_ROWS - EXPERT_ROW0 - N_EXPERTS
    w = jnp.concatenate([w_group.T, jnp.zeros((pad, D_MODEL), F32),
                         w_expert.reshape(D_MODEL, N_EXPERTS).T, jnp.zeros((tail, D_MODEL), F32)], axis=0)
    bias = jnp.concatenate([b_group, jnp.zeros((pad,), F32), b_expert.reshape(N_EXPERTS), jnp.zeros((tail,), F32)])
    hi = w.astype(BF)
    return hi, (w - hi.astype(F32)).astype(BF), bias.reshape(ROUTER_ROWS, 1)


def kernel(x_prompt, x_sample, cache_diff_k, cache_diff_v, state_ret_fwd, state_ret_bwd, c, c_ctx, w_mod, b_mod, w_in, ret_decay_logit, diff_lambda, diff_subln_g, conv_w, conv_b, w_out, ln_g, ln_b, w_router_group, b_router_group, w_router_expert, b_router_expert, w_gate, w_up, w_down):
    B, T_ctx, _ = x_prompt.shape
    Bd, T_lat, _ = x_sample.shape
    assert T_ctx == TILE and T_lat % TILE == 0 and T_lat % GRID_W == 0
    assert 1 + Bd <= MOD_ROWS

    cond = jnp.concatenate([c_ctx[None, :], c, jnp.zeros((MOD_ROWS - 1 - Bd, D_MODEL), F32)], axis=0)
    mod = _modulation(cond, w_mod, b_mod).reshape(DEPTH, MOD_ROWS, 1, 6 * D_MODEL)
    rope_tabs = _rope_tables(T_lat)
    w_in_b = w_in.astype(BF)
    w_out_b = w_out.astype(BF)

    ctx_row = lambda s: 0
    lat_row = lambda s: s + 1
    ctx_tiles = B * (T_ctx // TILE)
    lat_tiles = Bd * (T_lat // TILE)

    yp, ys = x_prompt, x_sample
    stacked = None
    for l in range(DEPTH):
        lam_init = 0.8 - 0.6 * math.exp(-0.3 * l)
        lg_lanes = jnp.repeat(ret_decay_logit[l], RET_DK, axis=1)
        wrh, wrl, br = _router_rows(w_router_group[l], b_router_group[l], w_router_expert[l], b_router_expert[l])
        shared = (w_out_b, lg_lanes, diff_lambda[l], diff_subln_g[l].reshape(1, DIFF_V), conv_w[l],
                  conv_b[l].reshape(1, CONV_CH), ln_g[l, 0].reshape(1, D_MODEL), ln_b[l, 0].reshape(1, D_MODEL),
                  wrh, wrl, br, lam_init)
        ln2 = (ln_g[l, 1].reshape(1, D_MODEL), ln_b[l, 1].reshape(1, D_MODEL))

        pr = _proj(yp, mod, l, ctx_row, w_in_b, lg_lanes, stacked=stacked)
        rq, rk, rv, rg, dq, dk, dv, cb, cch = pr[:9]
        stacked = pr[9:]
        x1_c, hs_c, slots_c, cnt_c = _mixer(yp, mod, l, ctx_row, (rq, rk, rv, rg, dq, cb, cch), None, None,
                                            (dk, dv), *shared)

        pr = _proj(ys, mod, l, lat_row, w_in_b, lg_lanes, rope_tabs=rope_tabs)
        rq, rk, rv, rg, dq, dk, dv, cb, cch, kvf, kvb = pr
        states = _scan(kvf, kvb, _block_diag(state_ret_fwd[:, l]), _block_diag(state_ret_bwd[:, l]), lg_lanes)
        cache_v = cache_diff_v[:, l].astype(BF)
        cache_kv = (cache_diff_k[:, l].reshape(Bd, -1, DIFF_W).astype(BF),
                    jnp.concatenate([cache_v, jnp.ones_like(cache_v)], axis=-1).reshape(Bd, -1, V_AUG_W))
        x1_l, hs_l, slots_l, cnt_l = _mixer(ys, mod, l, lat_row, (rq, rk, rv, rg, dq, cb, cch), states, cache_kv,
                                            (dk, dv), *shared)

        padded = jnp.concatenate([cnt_c.reshape(ctx_tiles, N_EXPERTS, LANES)[:, :, 0],
                                  cnt_l.reshape(lat_tiles, N_EXPERTS, LANES)[:, :, 0]], axis=0)
        eb, nv, cid = _block_schedule(padded, ctx_tiles * CHUNKS_PER_TILE)
        out_c, out_l = _moe(hs_c, hs_l, eb, nv, cid, l, w_gate, w_up, w_down)
        yp = _combine(out_c, slots_c, x1_c, mod, l, ctx_row, *ln2)
        ys = _combine(out_l, slots_l, x1_l, mod, l, lat_row, *ln2)

    new_k, new_v, st_f, st_b = stacked
    return (yp, ys, new_k.reshape(B, DEPTH, T_ctx, H_DIFF, 2, DIFF_QK),
            new_v.reshape(B, DEPTH, T_ctx, H_DIFF, DIFF_V), st_f, st_b)
```

```python
import functools
import math

import numpy as np
import jax
import jax.numpy as jnp
from jax import lax
from jax.experimental import pallas as pl
from jax.experimental.pallas import tpu as pltpu

D_MODEL = 1024
DEPTH = 2
GRID_W = 64
H_RET = 4
RET_DK = 64
RET_W = H_RET * RET_DK
H_DIFF = 4
DIFF_QK = 64
DIFF_V = 2 * DIFF_QK
DIFF_W = H_DIFF * DIFF_V
CONV_CH = 256
ROPE_BASE = 10000.0
N_GROUPS = 4
EXPERTS_PER_GROUP = 4
N_EXPERTS = N_GROUPS * EXPERTS_PER_GROUP
D_EXPERT = 512
ALPHA = (2 * DEPTH) ** 0.25
EPS = 1e-5
MOD_ROWS = 8
LANES = 128
ROUTER_ROWS = 32
EXPERT_ROW0 = 8

TILE = 256
CHUNK = 16
CAP = 2 * TILE + N_EXPERTS * CHUNK
CHUNKS_PER_TILE = CAP // CHUNK
COMBINE_TILES_PER_STEP = 2
MOE_TMB = 512
CPB = MOE_TMB // CHUNK
HS_W = D_MODEL + LANES
V_AUG_W = 2 * DIFF_W
QK_SCALE_LOG2 = (DIFF_QK ** -0.5) * math.log2(math.e)
SLOT_RADIX = 32

_O_RQ, _O_RK, _O_RV, _O_RG = 0, 256, 512, 768
_O_DQ, _O_DK, _O_DV = 1024, 1536, 2048
_O_CB, _O_CC, _O_CH = 2560, 2816, 3072
IN_WIDTH = 3328

BF = jnp.bfloat16
F32 = jnp.float32
I32 = jnp.int32

_VMEM_LIMIT = 56 * 1024 * 1024


def _dot(a, b):
    return jnp.dot(a, b, preferred_element_type=F32)


def _dot_nt(a, b):
    return lax.dot_general(a, b, (((1,), (1,)), ((), ())), preferred_element_type=F32)


def _dot_tn(a, b):
    return lax.dot_general(a, b, (((0,), (0,)), ((), ())), preferred_element_type=F32)


def _split(x):
    hi = x.astype(BF)
    lo = (x - hi.astype(F32)).astype(BF)
    return hi, lo


def _dot_hl(x, w_bf16):
    hi, lo = _split(x)
    return _dot(hi, w_bf16) + _dot(lo, w_bf16)


def _dot3(x, w_hi, w_lo):
    hi, lo = _split(x)
    return _dot(hi, w_hi) + (_dot(lo, w_hi) + _dot(hi, w_lo))


def _iota(shape, dim):
    return lax.broadcasted_iota(I32, shape, dim)


def _ones_where(cond):
    return jnp.where(cond, 1.0, 0.0).astype(BF)


def _log_sigmoid(x):
    return jnp.minimum(x, 0.0) - jnp.log1p(jnp.exp(-jnp.abs(x)))


def _silu(x):
    return x * jax.nn.sigmoid(x)


def _layer_norm(y, g, b):
    mu = jnp.mean(y, axis=-1, keepdims=True)
    yc = y - mu
    var = jnp.mean(yc * yc, axis=-1, keepdims=True)
    return yc * lax.rsqrt(var + EPS) * g + b


def _params(sem):
    return pltpu.CompilerParams(dimension_semantics=sem, vmem_limit_bytes=_VMEM_LIMIT)


_HBM = pl.BlockSpec(memory_space=pltpu.HBM)


def _mod_kernel(c_ref, w_ref, b_ref, o_ref):
    a = _silu(c_ref[...])
    w_hi, w_lo = _split(w_ref[...])
    o_ref[...] = _dot3(a, w_hi, w_lo) + b_ref[...]


def _modulation(cond, w_mod, b_mod):
    tn = 1536
    n6 = 6 * D_MODEL
    return pl.pallas_call(
        _mod_kernel,
        grid=(DEPTH, n6 // tn),
        in_specs=[
            pl.BlockSpec((MOD_ROWS, D_MODEL), lambda l, j: (0, 0)),
            pl.BlockSpec((None, D_MODEL, tn), lambda l, j: (l, 0, j)),
            pl.BlockSpec((None, 1, tn), lambda l, j: (l, 0, j)),
        ],
        out_specs=pl.BlockSpec((None, MOD_ROWS, tn), lambda l, j: (l, 0, j)),
        out_shape=jax.ShapeDtypeStruct((DEPTH, MOD_ROWS, n6), F32),
        compiler_params=_params(("parallel", "parallel")),
        name="modulation",
    )(cond, w_mod, b_mod.reshape(DEPTH, 1, n6))


def _mod_spec(layer, row_fn):
    return pl.BlockSpec((None, None, 1, 6 * D_MODEL), lambda s, i: (layer, row_fn(s), 0, 0))


def _augment_v(v):
    ones = jnp.ones((v.shape[0], DIFF_V), v.dtype)
    parts = []
    for hd in range(H_DIFF):
        parts += [v[:, hd * DIFF_V:(hd + 1) * DIFF_V], ones]
    return jnp.concatenate(parts, axis=1)


def _swap16(x):
    n = x.shape[-1]
    lane = _iota(x.shape, 1)
    return jnp.where((lane & 16) == 0, pltpu.roll(x, n - 16, axis=1), pltpu.roll(x, 16, axis=1))


def _proj_kernel(*refs, latent, n_aliased, layer):
    x_ref, mod_ref, w_ref, lg_ref = refs[:4]
    pos = 4
    if latent:
        cos_ref, sin_ref = refs[pos:pos + 2]
        pos += 2
    pos += n_aliased
    rq_ref, rk_ref, rv_ref, rg_ref, dq_ref, dk_ref, dv_ref, cb_ref, cch_ref = refs[pos:pos + 9]
    pos += 9
    if latent:
        kvf_ref, kvb_ref = refs[pos:pos + 2]
    else:
        dk32_ref, dv32_ref, stf_ref, stb_ref = refs[pos:pos + 4]

    def put(ref, idx, val):
        if n_aliased:
            ref[idx if idx else ...] = val
        else:
            for other in range(DEPTH):
                ref[(other,) + idx] = val if other == layer else jnp.zeros_like(val)

    m = mod_ref[...]
    sh1 = m[:, 0:D_MODEL]
    sc1 = m[:, D_MODEL:2 * D_MODEL]
    h = (x_ref[...] * (1.0 + sc1) + sh1).astype(BF)

    def col(off, width):
        return _dot(h, w_ref[:, off:off + width])

    rq_ref[...] = col(_O_RQ, RET_W).astype(BF)
    rk = col(_O_RK, RET_W) * (RET_DK ** -0.5)
    rk_ref[...] = rk.astype(BF)
    rv = col(_O_RV, RET_W).astype(BF)
    rv_ref[...] = rv
    rg_ref[...] = col(_O_RG, RET_W)

    lg = _log_sigmoid(lg_ref[...])
    p = _iota((TILE, 1), 0).astype(F32)
    kf = (rk * jnp.exp((TILE - 1.0 - p) * lg[0:1])).astype(BF)
    kb = (rk * jnp.exp(p * lg[1:2])).astype(BF)
    kvf = _dot_tn(kf, rv)
    kvb = _dot_tn(kb, rv)
    if latent:
        kvf_ref[...] = kvf
        kvb_ref[...] = kvb
    else:
        for hd in range(H_RET):
            lo = hd * RET_DK
            put(stf_ref, (hd,), kvf[lo:lo + RET_DK, lo:lo + RET_DK])
            put(stb_ref, (hd,), kvb[lo:lo + RET_DK, lo:lo + RET_DK])

    dq = col(_O_DQ, DIFF_W)
    dk = col(_O_DK, DIFF_W)
    dv = col(_O_DV, DIFF_W)
    if latent:
        cos = cos_ref[...]
        sin = sin_ref[...]
        dq = dq * cos + _swap16(dq) * sin
        dk = dk * cos + _swap16(dk) * sin
    else:
        put(dk32_ref, (), dk)
        put(dv32_ref, (), dv)
    dq_ref[...] = (dq * QK_SCALE_LOG2).astype(BF)
    dk_ref[...] = dk.astype(BF)
    dv_ref[...] = _augment_v(dv.astype(BF))

    cb_ref[...] = col(_O_CB, CONV_CH)
    cch_ref[...] = col(_O_CC, CONV_CH) * col(_O_CH, CONV_CH)


def _proj(x, mod, layer, row_fn, w_in_b, lg_lanes, rope_tabs=None, stacked=None):
    S, T, _ = x.shape
    n = T // TILE
    latent = rope_tabs is not None

    def tok(w):
        return pl.BlockSpec((None, TILE, w), lambda s, i: (s, i, 0))

    def tok_shape(w, dt):
        return jax.ShapeDtypeStruct((S, T, w), dt)

    in_specs = [
        tok(D_MODEL), _mod_spec(layer, row_fn),
        pl.BlockSpec((None, D_MODEL, IN_WIDTH), lambda s, i: (layer, 0, 0)),
        pl.BlockSpec((2, RET_W), lambda s, i: (0, 0)),
    ]
    args = [x, mod, w_in_b, lg_lanes]
    out_specs = [tok(RET_W)] * 4 + [tok(DIFF_W)] * 2 + [tok(V_AUG_W)] + [tok(CONV_CH)] * 2
    out_shape = ([tok_shape(RET_W, BF)] * 3 + [tok_shape(RET_W, F32)] + [tok_shape(DIFF_W, BF)] * 2
                 + [tok_shape(V_AUG_W, BF)]
                 + [tok_shape(CONV_CH, F32)] * 2)
    aliases = {}
    if latent:
        in_specs += [pl.BlockSpec((TILE, DIFF_W), lambda s, i: (i, 0))] * 2
        args += list(rope_tabs)
        out_specs += [pl.BlockSpec((None, None, RET_W, RET_W), lambda s, i: (s, i, 0, 0))] * 2
        out_shape += [jax.ShapeDtypeStruct((S, n, RET_W, RET_W), F32)] * 2
    else:
        assert n == 1
        kv_shape = jax.ShapeDtypeStruct((S, DEPTH, T, DIFF_W), F32)
        st_shape = jax.ShapeDtypeStruct((S, DEPTH, H_RET, RET_DK, RET_DK), F32)
        if stacked is not None:
            in_specs += [_HBM] * 4
            args += list(stacked)
            aliases = {len(args) - 4 + j: len(out_shape) + j for j in range(4)}
            out_specs += [pl.BlockSpec((None, None, TILE, DIFF_W), lambda s, i: (s, layer, i, 0))] * 2
            out_specs += [pl.BlockSpec((None, None, H_RET, RET_DK, RET_DK), lambda s, i: (s, layer, 0, 0, 0))] * 2
        else:
            out_specs += [pl.BlockSpec((None, DEPTH, TILE, DIFF_W), lambda s, i: (s, 0, i, 0))] * 2
            out_specs += [pl.BlockSpec((None, DEPTH, H_RET, RET_DK, RET_DK), lambda s, i: (s, 0, 0, 0, 0))] * 2
        out_shape += [kv_shape, kv_shape, st_shape, st_shape]
    return pl.pallas_call(
        functools.partial(_proj_kernel, latent=latent, n_aliased=len(aliases), layer=layer),
        grid=(S, n),
        in_specs=in_specs,
        out_specs=out_specs,
        out_shape=out_shape,
        input_output_aliases=aliases,
        compiler_params=_params(("parallel", "parallel")),
        name="proj_lat" if latent else "proj_ctx",
    )(*args)


def _scan_kernel(kvf_ref, kvb_ref, s0f_ref, s0b_ref, lg_ref, sf_ref, sb_ref, *, n):
    lg = _log_sigmoid(lg_ref[...])
    dec = jnp.exp(float(TILE) * lg)
    same_head = (_iota((RET_W, RET_W), 0) >> 6) == (_iota((RET_W, RET_W), 1) >> 6)

    s = jnp.where(same_head, s0f_ref[...], 0.0)
    for c in range(n):
        sf_ref[c] = s
        s = s * dec[0:1] + jnp.where(same_head, kvf_ref[c], 0.0)
    s = jnp.where(same_head, s0b_ref[...], 0.0)
    for c in reversed(range(n)):
        sb_ref[c] = s
        s = s * dec[1:2] + jnp.where(same_head, kvb_ref[c], 0.0)


def _scan(kvf, kvb, s0f, s0b, lg_lanes):
    S, n = kvf.shape[:2]
    chunks = pl.BlockSpec((None, n, RET_W, RET_W), lambda s: (s, 0, 0, 0))
    one = pl.BlockSpec((None, RET_W, RET_W), lambda s: (s, 0, 0))
    return pl.pallas_call(
        functools.partial(_scan_kernel, n=n),
        grid=(S,),
        in_specs=[chunks, chunks, one, one, pl.BlockSpec((2, RET_W), lambda s: (0, 0))],
        out_specs=[chunks, chunks],
        out_shape=[jax.ShapeDtypeStruct((S, n, RET_W, RET_W), F32)] * 2,
        compiler_params=_params(("parallel",)),
        name="ret_scan",
    )(kvf, kvb, s0f, s0b, lg_lanes)


def _diff_heads(dq_ref, kv_refs, heads, lam):
    lane = _iota((1, DIFF_V), 1)
    scores = []
    for hd in heads:
        lo = hd * DIFF_V
        q_h = dq_ref[:, lo:lo + DIFF_V]
        qs = jnp.concatenate([jnp.where(lane < DIFF_QK, q_h, 0), jnp.where(lane >= DIFF_QK, q_h, 0)], axis=0)
        scores.append(jnp.concatenate([_dot_nt(qs, k[:, lo:lo + DIFF_V]) for k, _ in kv_refs], axis=1))
    s = jnp.concatenate(scores, axis=0)
    p = jnp.exp2(s - jnp.max(s, axis=-1, keepdims=True))

    def times_v(w, hd, width):
        acc = None
        start = 0
        for k, v in kv_refs:
            part = _dot(w[:, start:start + k.shape[0]], v[:, hd * 2 * DIFF_V:hd * 2 * DIFF_V + width])
            acc = part if acc is None else acc + part
            start += k.shape[0]
        return acc

    outs = []
    p = p.astype(BF)
    for n, hd in enumerate(heads):
        acc = times_v(p[n * 2 * TILE:(n + 1) * 2 * TILE], hd, 2 * DIFF_V)
        o0 = acc[0:TILE, 0:DIFF_V] * (1.0 / acc[0:TILE, DIFF_V:DIFF_V + 1])
        o1 = acc[TILE:, 0:DIFF_V] * (lam / acc[TILE:, DIFF_V:DIFF_V + 1])
        outs.append(o0 - o1)
    return outs


def _mixer_kernel(*refs, n_tiles, n_total, lam_init, cached):
    n_a = _n_token_mix_refs(cached)
    mix_even, mix_odd, decay_ref = refs[-3:]
    t = pl.program_id(0)
    tile_in_seq = lax.rem(jnp.minimum(t, n_total - 1), n_tiles)

    @pl.when(t == 0)
    def _():
        mix_odd[...] = jnp.zeros_like(mix_odd)
        lg = _log_sigmoid(refs[n_a - 5][...])
        dist = (_iota((TILE, TILE), 0) - _iota((TILE, TILE), 1)).astype(F32)
        diag2 = jnp.where(dist == 0.0, 2.0, 1.0)
        for hd in range(H_RET):
            lgf = lg[0:1, hd * RET_DK:hd * RET_DK + 1]
            lgb = lg[1:2, hd * RET_DK:hd * RET_DK + 1]
            decay_ref[hd] = jnp.exp(jnp.abs(dist) * jnp.where(dist > 0.0, lgf, lgb)) * diag2

    def step(mix_write, mix_read):
        mix_write[...] = _token_mix(*refs[:n_a], decay_ref, tile_in_seq=tile_in_seq, n_tiles=n_tiles,
                                    lam_init=lam_init, cached=cached)
        _project_route(mix_read[...], *refs[n_a:-3])

    @pl.when(lax.rem(t, 2) == 0)
    def _():
        step(mix_even, mix_odd)

    @pl.when(lax.rem(t, 2) == 1)
    def _():
        step(mix_odd, mix_even)


def _n_token_mix_refs(cached):
    return 18 + (2 if cached else 0)


def _token_mix(*refs, tile_in_seq, n_tiles, lam_init, cached):
    rq_ref, rk_ref, rv_ref, rg_ref, sf_ref, sb_ref, dq_ref = refs[:7]
    pos = 7
    kv_refs = []
    if cached:
        kv_refs.append((refs[pos], refs[pos + 1]))
        pos += 2
    kv_refs.append((refs[pos], refs[pos + 1]))
    pos += 2
    (cb_ref, cch_ref, cprev_ref, cnext_ref, lg_ref, lamp_ref, subg_ref, cw_ref, cbias_ref,
     decay_ref) = refs[pos:pos + 10]
    i = tile_in_seq

    lg = _log_sigmoid(lg_ref[...])
    head_of_lane = _iota((1, RET_W), 1) >> 6
    q = rq_ref[...]
    k = rk_ref[...]
    v = rv_ref[...]
    ret_o = jnp.zeros((TILE, RET_W), F32)
    for hd in range(H_RET):
        in_head = head_of_lane == hd
        sc = _dot_nt(jnp.where(in_head, q, 0), k)
        ret_o = ret_o + jnp.where(in_head, _dot((sc * decay_ref[hd]).astype(BF), v), 0.0)
    p = _iota((TILE, 1), 0).astype(F32)
    ret_o = ret_o + _dot(q, sf_ref[...].astype(BF)) * jnp.exp((p + 1.0) * lg[0:1])
    ret_o = ret_o + _dot(q, sb_ref[...].astype(BF)) * jnp.exp((float(TILE) - p) * lg[1:2])
    avg = jnp.where((_iota((RET_W, RET_W), 0) >> 6) == (_iota((RET_W, RET_W), 1) >> 6),
                    1.0 / RET_DK, 0.0).astype(BF)
    rc = ret_o - _dot_hl(ret_o, avg)
    ret = rc * lax.rsqrt(_dot_hl(rc * rc, avg) + EPS) * _silu(rg_ref[...])

    lp = lamp_ref[...]
    lam = (jnp.exp(jnp.sum(lp[0:1] * lp[1:2], axis=-1, keepdims=True))
           - jnp.exp(jnp.sum(lp[2:3] * lp[3:4], axis=-1, keepdims=True)) + lam_init)
    subg = subg_ref[...] * (1.0 - lam_init)
    head_groups = [[hd] for hd in range(H_DIFF)] if cached else [list(range(H_DIFF))]
    heads = []
    for group in head_groups:
        for o in _diff_heads(dq_ref, kv_refs, group, lam):
            o = o * lax.rsqrt(jnp.mean(o * o, axis=-1, keepdims=True) + EPS) * subg
            heads.append(o.astype(BF))
    diff = jnp.concatenate(heads, axis=1)

    cch = cch_ref[...]
    prev = jnp.where(i > 0, cprev_ref[7:8, :], 0.0)
    nxt = jnp.where(i < n_tiles - 1, cnext_ref[0:1, :], 0.0)
    r = _iota((TILE, 1), 0)
    up = jnp.where(r == 0, prev, pltpu.roll(cch, 1, axis=0))
    dn = jnp.where(r == TILE - 1, nxt, pltpu.roll(cch, TILE - 1, axis=0))
    cw = cw_ref[...]
    conv = cb_ref[...] * (up * cw[0:1] + cch * cw[1:2] + dn * cw[2:3] + cbias_ref[...])
    return jnp.concatenate([ret.astype(BF), diff, conv.astype(BF)], axis=1)


def _project_route(mixed, x_ref, mod_ref, wout_ref, lng_ref, lnb_ref, wrh_ref, wrl_ref, br_ref,
                   x1_ref, hs_ref, slots_ref, cnt_ref):
    m = mod_ref[...]
    g1 = m[:, 2 * D_MODEL:3 * D_MODEL]
    sh2 = m[:, 3 * D_MODEL:4 * D_MODEL]
    sc2 = m[:, 4 * D_MODEL:5 * D_MODEL]
    x1 = _layer_norm(ALPHA * x_ref[...] + g1 * _dot(mixed, wout_ref[...]), lng_ref[...], lnb_ref[...])
    x1_ref[...] = x1

    h2 = x1 * (1.0 + sc2) + sh2
    h2h, h2l = _split(h2)
    wrh = wrh_ref[...]
    logits = (_dot_nt(wrh, h2h) + (_dot_nt(wrh, h2l) + _dot_nt(wrl_ref[...], h2h))) + br_ref[...]
    neg = -jnp.inf
    g_id = _iota((EXPERT_ROW0, TILE), 0)
    gl = jnp.where(g_id < N_GROUPS, logits[0:EXPERT_ROW0], neg)
    gmax = jnp.max(gl, axis=0, keepdims=True)
    g_idx = jnp.min(jnp.where(gl == gmax, g_id, N_GROUPS), axis=0, keepdims=True)
    g_w = 1.0 / jnp.sum(jnp.exp(gl - gmax), axis=0, keepdims=True)
    e_id = _iota((N_EXPERTS, TILE), 0)
    el = jnp.where((e_id >> 2) == g_idx, logits[EXPERT_ROW0:EXPERT_ROW0 + N_EXPERTS], neg)
    v1 = jnp.max(el, axis=0, keepdims=True)
    i1 = jnp.min(jnp.where(el == v1, e_id, N_EXPERTS), axis=0, keepdims=True)
    el2 = jnp.where(e_id == i1, neg, el)
    v2 = jnp.max(el2, axis=0, keepdims=True)
    i2 = jnp.min(jnp.where(el2 == v2, e_id, N_EXPERTS), axis=0, keepdims=True)
    t = jnp.exp(v2 - v1)
    w1 = g_w / (1.0 + t)
    w2 = g_w * t / (1.0 + t)

    sel1 = e_id == i1
    sel2 = e_id == i2
    sel = jnp.where(sel1 | sel2, 1.0, 0.0)
    earlier = _ones_where(_iota((TILE, TILE), 0) < _iota((TILE, TILE), 1))
    rank = _dot(sel.astype(BF), earlier)
    cnt = jnp.sum(sel, axis=1, keepdims=True).astype(I32)
    padded = ((cnt + (CHUNK - 1)) >> 4) << 4
    incl = jnp.broadcast_to(padded, (N_EXPERTS, LANES))
    e_row = _iota((N_EXPERTS, LANES), 0)
    for step in (1, 2, 4, 8):
        incl = incl + jnp.where(e_row >= step, pltpu.roll(incl, step, axis=0), 0)
    seg_off = (incl[:, 0:1] - padded).astype(F32)
    spos = seg_off + rank
    slot1 = jnp.sum(jnp.where(sel1, spos, 0.0), axis=0, keepdims=True).astype(I32)
    slot2 = jnp.sum(jnp.where(sel2, spos, 0.0), axis=0, keepdims=True).astype(I32)
    s_id = _iota((CAP, TILE), 0)
    at1 = s_id == slot1
    at2 = s_id == slot2
    hs_ref[:, 0:D_MODEL] = _dot(_ones_where(at1 | at2), h2h).astype(BF)
    w1h = w1.astype(BF).astype(F32)
    w2h = w2.astype(BF).astype(F32)
    ones = jnp.ones((TILE, LANES), BF)
    g_hi = _dot((jnp.where(at1, w1h, 0.0) + jnp.where(at2, w2h, 0.0)).astype(BF), ones)
    g_lo = _dot((jnp.where(at1, w1 - w1h, 0.0) + jnp.where(at2, w2 - w2h, 0.0)).astype(BF), ones)
    lane = _iota((CAP, LANES), 1)
    hs_ref[:, D_MODEL:] = jnp.where(lane == 0, g_hi, jnp.where(lane == 1, g_lo, 0.0)).astype(BF)
    cnt_ref[...] = jnp.broadcast_to(padded, (N_EXPERTS, LANES))

    dr = _iota((LANES, TILE), 0)
    digits = jnp.where(dr == 0, slot1 & (SLOT_RADIX - 1), jnp.where(dr == 1, slot1 >> 5,
             jnp.where(dr == 2, slot2 & (SLOT_RADIX - 1), jnp.where(dr == 3, slot2 >> 5, 0))))
    eye = _ones_where(_iota((TILE, TILE), 0) == _iota((TILE, TILE), 1))
    cols = _dot_nt(eye, digits.astype(F32).astype(BF))
    s1c = (cols[:, 0:1] + SLOT_RADIX * cols[:, 1:2]).astype(I32)
    s2c = (cols[:, 2:3] + SLOT_RADIX * cols[:, 3:4]).astype(I32)
    tl = _iota((TILE, LANES), 1)
    slots_ref[...] = jnp.where(tl == 0, s1c, jnp.where(tl == 1, s2c, 0))


def _mixer(x, mod, layer, row_fn, pr, states, cache_kv, new_kv, w_out_b, lg_lanes, lamp, subg, cw, cbias,
           lng, lnb, wrh, wrl, br, lam_init):
    S, T, _ = x.shape
    n = T // TILE
    rq, rk, rv, rg, dq, cb, cch = pr
    t8 = TILE // 8
    cached = cache_kv is not None
    last = S * n - 1

    def at_mix(t):
        return jnp.minimum(t, last) // n, jnp.minimum(t, last) % n

    def at_out(t):
        return jnp.maximum(t - 1, 0) // n, jnp.maximum(t - 1, 0) % n

    def tok(w, at):
        return pl.BlockSpec((None, TILE, w), lambda t: (*at(t), 0))

    def full(shape):
        return pl.BlockSpec(shape, lambda t: (0,) * len(shape))

    def seq(a):
        mode = pl.Buffered(1) if n > 1 else None
        return pl.BlockSpec((None,) + a.shape[1:], lambda t: (at_mix(t)[0], 0, 0), pipeline_mode=mode)

    def halo(offset):
        def index(t):
            s, i = at_mix(t)
            return (s, jnp.clip(i * t8 + offset, 0, T // 8 - 1), 0)
        return pl.BlockSpec((None, 8, CONV_CH), index)

    if states is None:
        zero_state = jnp.zeros((1, 1, RET_W, RET_W), F32)
        states = (zero_state, zero_state)
        state = pl.BlockSpec((None, None, RET_W, RET_W), lambda t: (0, 0, 0, 0))
    else:
        state = pl.BlockSpec((None, None, RET_W, RET_W), lambda t: (*at_mix(t), 0, 0))

    in_specs = [tok(RET_W, at_mix)] * 4 + [state, state, tok(DIFF_W, at_mix)]
    args = [rq, rk, rv, rg, states[0], states[1], dq]
    for kv in ([cache_kv] if cached else []) + [new_kv]:
        in_specs += [seq(kv[0]), seq(kv[1])]
        args += list(kv)
    in_specs += [tok(CONV_CH, at_mix), tok(CONV_CH, at_mix), halo(-1), halo(t8),
                 full((2, RET_W)), full((4, DIFF_QK)), full((1, DIFF_V)), full((3, CONV_CH)), full((1, CONV_CH))]
    args += [cb, cch, cch, cch, lg_lanes, lamp, subg, cw, cbias]
    assert len(args) == _n_token_mix_refs(cached)
    in_specs += [tok(D_MODEL, at_out),
                 pl.BlockSpec((None, None, 1, 6 * D_MODEL), lambda t: (layer, row_fn(at_out(t)[0]), 0, 0)),
                 pl.BlockSpec((None, D_MODEL, D_MODEL), lambda t: (layer, 0, 0)),
                 full((1, D_MODEL)), full((1, D_MODEL)),
                 full((ROUTER_ROWS, D_MODEL)), full((ROUTER_ROWS, D_MODEL)), full((ROUTER_ROWS, 1))]
    args += [x, mod, w_out_b, lng, lnb, wrh, wrl, br]

    def out_tile(t):
        s, i = at_out(t)
        return s * n + i

    return pl.pallas_call(
        functools.partial(_mixer_kernel, n_tiles=n, n_total=S * n, lam_init=lam_init, cached=cached),
        grid=(S * n + 1,),
        in_specs=in_specs,
        out_specs=[tok(D_MODEL, at_out),
                   pl.BlockSpec((CAP, HS_W), lambda t: (out_tile(t), 0)),
                   tok(LANES, at_out),
                   pl.BlockSpec((None, None, N_EXPERTS, LANES), lambda t: (*at_out(t), 0, 0))],
        out_shape=[jax.ShapeDtypeStruct((S, T, D_MODEL), F32),
                   jax.ShapeDtypeStruct((S * n * CAP, HS_W), BF),
                   jax.ShapeDtypeStruct((S, T, LANES), I32),
                   jax.ShapeDtypeStruct((S, n, N_EXPERTS, LANES), I32)],
        scratch_shapes=[pltpu.VMEM((TILE, D_MODEL), BF), pltpu.VMEM((TILE, D_MODEL), BF),
                        pltpu.VMEM((H_RET, TILE, TILE), F32)],
        compiler_params=_params(("arbitrary",)),
        name="mixer_lat" if cached else "mixer_ctx",
    )(*args)


def _block_schedule(padded, chunks_a):
    NT = padded.shape[0]
    n_chunks = NT * CHUNKS_PER_TILE
    n_blocks = n_chunks // CPB + N_EXPERTS
    c16 = padded // CHUNK
    ends = jnp.cumsum(c16, axis=1)
    before = jnp.cumsum(c16, axis=0) - c16
    per_expert = jnp.sum(c16, axis=0)
    nb = (per_expert + (CPB - 1)) // CPB
    b_end = jnp.cumsum(nb)
    q = jnp.arange(CHUNKS_PER_TILE, dtype=I32)
    e = jnp.arange(N_EXPERTS, dtype=I32)
    key = jnp.sum((ends[:, None, :] <= q[None, :, None]).astype(I32), axis=-1)
    pos_e = ((b_end - nb) * CPB + before)[:, None, :] + (q[None, :, None] - (ends - c16)[:, None, :])
    pos = jnp.sum(jnp.where(key[..., None] == e, pos_e, 0), axis=-1)
    pos = jnp.where(key < N_EXPERTS, pos, -1).reshape(-1)
    match = pos[None, :] == jnp.arange(n_blocks * CPB, dtype=I32)[:, None]
    cid = jnp.sum(jnp.where(match, jnp.arange(n_chunks, dtype=I32)[None, :], 0), axis=-1)
    nv = jnp.sum(match.astype(I32).reshape(n_blocks, -1), axis=-1)
    first = match & (jnp.arange(n_chunks, dtype=I32) < chunks_a)[None, :]
    nv_a = jnp.sum(first.astype(I32).reshape(n_blocks, -1), axis=-1)
    b = jnp.arange(n_blocks, dtype=I32)
    eb = jnp.minimum(jnp.sum((b_end[None, :] <= b[:, None]).astype(I32), axis=-1), N_EXPERTS - 1)
    return eb.astype(I32), jnp.concatenate([nv, nv_a]).astype(I32), cid.astype(I32)


def _moe_kernel(eb_ref, nv_ref, cid_ref, hs_a, hs_b, wg_ref, wu_ref, wd_ref, ys_a, ys_b,
                xbuf, ybuf, wg_b, wu_b, wd_b, in_sem, out_sem, *, n_blocks, chunks_a):
    b = pl.program_id(0)

    def chunk_rows(c):
        return pl.ds(pl.multiple_of(c * CHUNK, CHUNK), CHUNK)

    def gather_copy(which, c, j, slot):
        src = (hs_a, hs_b)[which]
        return pltpu.make_async_copy(src.at[chunk_rows(c)], xbuf.at[slot, chunk_rows(j)], in_sem.at[slot])

    def scatter_copy(which, c, j, slot):
        dst = (ys_a, ys_b)[which]
        return pltpu.make_async_copy(ybuf.at[slot, chunk_rows(j)],
                                     dst.at[chunk_rows(c), pl.ds(0, D_MODEL)], out_sem.at[slot])

    def for_chunks(blk, fn):
        n_a = nv_ref[n_blocks + blk]

        def body_a(j, carry):
            fn(j, 0, cid_ref[blk * CPB + j])
            return carry

        def body_b(j, carry):
            fn(j, 1, cid_ref[blk * CPB + j] - chunks_a)
            return carry
        lax.fori_loop(0, n_a, body_a, 0)
        lax.fori_loop(n_a, nv_ref[blk], body_b, 0)

    def wait_gathers(blk, slot):
        rows = pl.ds(0, nv_ref[blk] * CHUNK)

        @pl.when(nv_ref[blk] > 0)
        def _():
            pltpu.make_async_copy(hs_a.at[rows], xbuf.at[slot, rows], in_sem.at[slot]).wait()

    def wait_scatters(blk, slot):
        rows = pl.ds(0, nv_ref[blk] * CHUNK)

        @pl.when(nv_ref[blk] > 0)
        def _():
            pltpu.make_async_copy(ybuf.at[slot, rows], ys_a.at[rows, pl.ds(0, D_MODEL)], out_sem.at[slot]).wait()

    slot = lax.rem(b, 2)
    other = 1 - slot

    @pl.when(b == 0)
    def _():
        xbuf[...] = jnp.zeros_like(xbuf)
        for_chunks(0, lambda j, w, c: gather_copy(w, c, j, 0).start())

    @pl.when(b + 1 < n_blocks)
    def _():
        for_chunks(b + 1, lambda j, w, c: gather_copy(w, c, j, other).start())

    @pl.when((b == 0) | (eb_ref[b] != eb_ref[jnp.maximum(b - 1, 0)]))
    def _():
        wg_b[...] = wg_ref[...].astype(BF)
        wu_b[...] = wu_ref[...].astype(BF)
        wd_b[...] = wd_ref[...].astype(BF)

    wait_gathers(b, slot)

    @pl.when(b >= 2)
    def _():
        wait_scatters(b - 2, slot)

    @pl.when(nv_ref[b] > 0)
    def _():
        x = xbuf[slot, :, 0:D_MODEL]
        gate = (xbuf[slot, :, D_MODEL:D_MODEL + 1].astype(F32)
                + xbuf[slot, :, D_MODEL + 1:D_MODEL + 2].astype(F32))
        hg = _dot(x, wg_b[...])
        hu = _dot(x, wu_b[...])
        act = _silu(hg) * hu * gate
        ybuf[slot] = _dot(act.astype(BF), wd_b[...]).astype(BF)

    for_chunks(b, lambda j, w, c: scatter_copy(w, c, j, slot).start())

    @pl.when(b == n_blocks - 1)
    def _():
        if n_blocks >= 2:
            wait_scatters(b - 1, other)
        wait_scatters(b, slot)


def _moe(hs_a, hs_b, eb, nv, cid, layer, w_gate, w_up, w_down):
    n_blocks = eb.shape[0]

    def w_spec(a, c):
        return pl.BlockSpec((None, None, a, c), lambda b, eb, nv, cid: (layer, eb[b], 0, 0))

    grid_spec = pltpu.PrefetchScalarGridSpec(
        num_scalar_prefetch=3,
        grid=(n_blocks,),
        in_specs=[_HBM, _HBM, w_spec(D_MODEL, D_EXPERT), w_spec(D_MODEL, D_EXPERT), w_spec(D_EXPERT, D_MODEL)],
        out_specs=[_HBM, _HBM],
        scratch_shapes=[pltpu.VMEM((2, MOE_TMB, HS_W), BF), pltpu.VMEM((2, MOE_TMB, D_MODEL), BF),
                        pltpu.VMEM((D_MODEL, D_EXPERT), BF), pltpu.VMEM((D_MODEL, D_EXPERT), BF),
                        pltpu.VMEM((D_EXPERT, D_MODEL), BF),
                        pltpu.SemaphoreType.DMA((2,)), pltpu.SemaphoreType.DMA((2,))],
    )
    return pl.pallas_call(
        functools.partial(_moe_kernel, n_blocks=n_blocks, chunks_a=hs_a.shape[0] // CHUNK),
        grid_spec=grid_spec,
        out_shape=[jax.ShapeDtypeStruct(hs_a.shape, BF), jax.ShapeDtypeStruct(hs_b.shape, BF)],
        input_output_aliases={3: 0, 4: 1},
        compiler_params=_params(("arbitrary",)),
        name="moe_sorted",
    )(eb, nv, cid, hs_a, hs_b, w_gate, w_up, w_down)


def _combine_kernel(ys_ref, slots_ref, x1_ref, mod_ref, lng_ref, lnb_ref, o_ref):
    g2 = mod_ref[:, 5 * D_MODEL:6 * D_MODEL]
    s_lane = _iota((TILE, CAP), 1)
    for u in range(COMBINE_TILES_PER_STEP):
        sl = slots_ref[u]
        pick = _ones_where((s_lane == sl[:, 0:1]) | (s_lane == sl[:, 1:2]))
        ffn = _dot(pick, ys_ref[u * CAP:(u + 1) * CAP, 0:D_MODEL])
        o_ref[u] = _layer_norm(ALPHA * x1_ref[u] + g2 * ffn, lng_ref[...], lnb_ref[...])


def _combine(ys, slots, x1, mod, layer, row_fn, lng, lnb):
    S, T, _ = x1.shape
    n = T // TILE
    U = COMBINE_TILES_PER_STEP
    assert (S * n) % U == 0 and (n % U == 0 or n == 1)

    def tok(w):
        return pl.BlockSpec((U, TILE, w), lambda t: (t, 0, 0))

    out = pl.pallas_call(
        _combine_kernel,
        grid=(S * n // U,),
        in_specs=[
            pl.BlockSpec((U * CAP, HS_W), lambda t: (t, 0)),
            tok(LANES), tok(D_MODEL),
            pl.BlockSpec((None, None, 1, 6 * D_MODEL), lambda t: (layer, row_fn(t * U // n), 0, 0)),
            pl.BlockSpec((1, D_MODEL), lambda t: (0, 0)),
            pl.BlockSpec((1, D_MODEL), lambda t: (0, 0)),
        ],
        out_specs=tok(D_MODEL),
        out_shape=jax.ShapeDtypeStruct((S * n, TILE, D_MODEL), F32),
        compiler_params=_params(("parallel",)),
        name="moe_combine",
    )(ys, slots.reshape(S * n, TILE, LANES), x1.reshape(S * n, TILE, D_MODEL), mod, lng, lnb)
    return out.reshape(S, T, D_MODEL)


def _rope_tables(n_lat):
    half = DIFF_QK // 2
    pairs = half // 2
    inv = (1.0 / (ROPE_BASE ** (np.arange(pairs, dtype=np.float32) * 2.0 / half))).astype(np.float32)
    t = np.arange(n_lat)
    ang_r = ((t // GRID_W).astype(np.float32)[:, None] * inv[None, :]).astype(np.float64)
    ang_c = ((t % GRID_W).astype(np.float32)[:, None] * inv[None, :]).astype(np.float64)
    cos = np.concatenate([np.cos(ang_r)] * 2 + [np.cos(ang_c)] * 2, axis=1)
    sin = np.concatenate([-np.sin(ang_r), np.sin(ang_r), -np.sin(ang_c), np.sin(ang_c)], axis=1)
    reps = DIFF_W // DIFF_QK
    return (jnp.asarray(np.tile(cos, (1, reps)), F32), jnp.asarray(np.tile(sin, (1, reps)), F32))


def _block_diag(s):
    S = s.shape[0]
    eye = jnp.eye(H_RET, dtype=s.dtype)
    return jnp.einsum('shdv,hg->shdgv', s, eye).reshape(S, RET_W, RET_W)


def _router_rows(w_group, b_group, w_expert, b_expert):
    pad = EXPERT_ROW0 - N_GROUPS
    tail = ROUTER_ROWS - EXPERT_ROW0 - N_EXPERTS
    w = jnp.concatenate([w_group.T, jnp.zeros((pad, D_MODEL), F32),
                         w_expert.reshape(D_MODEL, N_EXPERTS).T, jnp.zeros((tail, D_MODEL), F32)], axis=0)
    bias = jnp.concatenate([b_group, jnp.zeros((pad,), F32), b_expert.reshape(N_EXPERTS), jnp.zeros((tail,), F32)])
    hi = w.astype(BF)
    return hi, (w - hi.astype(F32)).astype(BF), bias.reshape(ROUTER_ROWS, 1)


def kernel(x_prompt, x_sample, cache_diff_k, cache_diff_v, state_ret_fwd, state_ret_bwd, c, c_ctx, w_mod, b_mod, w_in, ret_decay_logit, diff_lambda, diff_subln_g, conv_w, conv_b, w_out, ln_g, ln_b, w_router_group, b_router_group, w_router_expert, b_router_expert, w_gate, w_up, w_down):
    B, T_ctx, _ = x_prompt.shape
    Bd, T_lat, _ = x_sample.shape
    assert T_ctx == TILE and T_lat % TILE == 0 and T_lat % GRID_W == 0
    assert 1 + Bd <= MOD_ROWS

    cond = jnp.concatenate([c_ctx[None, :], c, jnp.zeros((MOD_ROWS - 1 - Bd, D_MODEL), F32)], axis=0)
    mod = _modulation(cond, w_mod, b_mod).reshape(DEPTH, MOD_ROWS, 1, 6 * D_MODEL)
    rope_tabs = _rope_tables(T_lat)
    w_in_b = w_in.astype(BF)
    w_out_b = w_out.astype(BF)

    ctx_row = lambda s: 0
    lat_row = lambda s: s + 1
    ctx_tiles = B * (T_ctx // TILE)
    lat_tiles = Bd * (T_lat // TILE)

    yp, ys = x_prompt, x_sample
    stacked = None
    for l in range(DEPTH):
        lam_init = 0.8 - 0.6 * math.exp(-0.3 * l)
        lg_lanes = jnp.repeat(ret_decay_logit[l], RET_DK, axis=1)
        wrh, wrl, br = _router_rows(w_router_group[l], b_router_group[l], w_router_expert[l], b_router_expert[l])
        shared = (w_out_b, lg_lanes, diff_lambda[l], diff_subln_g[l].reshape(1, DIFF_V), conv_w[l],
                  conv_b[l].reshape(1, CONV_CH), ln_g[l, 0].reshape(1, D_MODEL), ln_b[l, 0].reshape(1, D_MODEL),
                  wrh, wrl, br, lam_init)
        ln2 = (ln_g[l, 1].reshape(1, D_MODEL), ln_b[l, 1].reshape(1, D_MODEL))

        pr = _proj(yp, mod, l, ctx_row, w_in_b, lg_lanes, stacked=stacked)
        rq, rk, rv, rg, dq, dk, dv, cb, cch = pr[:9]
        stacked = pr[9:]
        x1_c, hs_c, slots_c, cnt_c = _mixer(yp, mod, l, ctx_row, (rq, rk, rv, rg, dq, cb, cch), None, None,
                                            (dk, dv), *shared)

        pr = _proj(ys, mod, l, lat_row, w_in_b, lg_lanes, rope_tabs=rope_tabs)
        rq, rk, rv, rg, dq, dk, dv, cb, cch, kvf, kvb = pr
        states = _scan(kvf, kvb, _block_diag(state_ret_fwd[:, l]), _block_diag(state_ret_bwd[:, l]), lg_lanes)
        cache_v = cache_diff_v[:, l].astype(BF)
        cache_kv = (cache_diff_k[:, l].reshape(Bd, -1, DIFF_W).astype(BF),
                    jnp.concatenate([cache_v, jnp.ones_like(cache_v)], axis=-1).reshape(Bd, -1, V_AUG_W))
        x1_l, hs_l, slots_l, cnt_l = _mixer(ys, mod, l, lat_row, (rq, rk, rv, rg, dq, cb, cch), states, cache_kv,
                                            (dk, dv), *shared)

        padded = jnp.concatenate([cnt_c.reshape(ctx_tiles, N_EXPERTS, LANES)[:, :, 0],
                                  cnt_l.reshape(lat_tiles, N_EXPERTS, LANES)[:, :, 0]], axis=0)
        eb, nv, cid = _block_schedule(padded, ctx_tiles * CHUNKS_PER_TILE)
        out_c, out_l = _moe(hs_c, hs_l, eb, nv, cid, l, w_gate, w_up, w_down)
        yp = _combine(out_c, slots_c, x1_c, mod, l, ctx_row, *ln2)
        ys = _combine(out_l, slots_l, x1_l, mod, l, lat_row, *ln2)

    new_k, new_v, st_f, st_b = stacked
    return (yp, ys, new_k.reshape(B, DEPTH, T_ctx, H_DIFF, 2, DIFF_QK),
            new_v.reshape(B, DEPTH, T_ctx, H_DIFF, DIFF_V), st_f, st_b)
```

```python
import functools
import math

import numpy as np
import jax
import jax.numpy as jnp
from jax import lax
from jax.experimental import pallas as pl
from jax.experimental.pallas import tpu as pltpu

D_MODEL = 1024
DEPTH = 2
GRID_W = 64
H_RET = 4
RET_DK = 64
RET_W = H_RET * RET_DK
H_DIFF = 4
DIFF_QK = 64
DIFF_V = 2 * DIFF_QK
DIFF_W = H_DIFF * DIFF_V
CONV_CH = 256
ROPE_BASE = 10000.0
N_GROUPS = 4
EXPERTS_PER_GROUP = 4
N_EXPERTS = N_GROUPS * EXPERTS_PER_GROUP
D_EXPERT = 512
ALPHA = (2 * DEPTH) ** 0.25
EPS = 1e-5
MOD_ROWS = 8
LANES = 128
ROUTER_ROWS = 32
EXPERT_ROW0 = 8

TILE = 256
CHUNK = 16
CAP = 2 * TILE + N_EXPERTS * CHUNK
CHUNKS_PER_TILE = CAP // CHUNK
COMBINE_TILES_PER_STEP = 2
MOE_TMB = 512
CPB = MOE_TMB // CHUNK
HS_W = D_MODEL + LANES
V_AUG_W = 2 * DIFF_W
QK_SCALE_LOG2 = (DIFF_QK ** -0.5) * math.log2(math.e)
SLOT_RADIX = 32

_O_RQ, _O_RK, _O_RV, _O_RG = 0, 256, 512, 768
_O_DQ, _O_DK, _O_DV = 1024, 1536, 2048
_O_CB, _O_CC, _O_CH = 2560, 2816, 3072
IN_WIDTH = 3328

BF = jnp.bfloat16
F32 = jnp.float32
I32 = jnp.int32

_VMEM_LIMIT = 56 * 1024 * 1024


def _dot(a, b):
    return jnp.dot(a, b, preferred_element_type=F32)


def _dot_nt(a, b):
    return lax.dot_general(a, b, (((1,), (1,)), ((), ())), preferred_element_type=F32)


def _dot_tn(a, b):
    return lax.dot_general(a, b, (((0,), (0,)), ((), ())), preferred_element_type=F32)


def _split(x):
    hi = x.astype(BF)
    lo = (x - hi.astype(F32)).astype(BF)
    return hi, lo


def _dot_hl(x, w_bf16):
    hi, lo = _split(x)
    return _dot(hi, w_bf16) + _dot(lo, w_bf16)


def _dot3(x, w_hi, w_lo):
    hi, lo = _split(x)
    return _dot(hi, w_hi) + (_dot(lo, w_hi) + _dot(hi, w_lo))


def _iota(shape, dim):
    return lax.broadcasted_iota(I32, shape, dim)


def _ones_where(cond):
    return jnp.where(cond, 1.0, 0.0).astype(BF)


def _log_sigmoid(x):
    return jnp.minimum(x, 0.0) - jnp.log1p(jnp.exp(-jnp.abs(x)))


def _silu(x):
    return x * jax.nn.sigmoid(x)


def _layer_norm(y, g, b):
    mu = jnp.mean(y, axis=-1, keepdims=True)
    yc = y - mu
    var = jnp.mean(yc * yc, axis=-1, keepdims=True)
    return yc * lax.rsqrt(var + EPS) * g + b


def _params(sem):
    return pltpu.CompilerParams(dimension_semantics=sem, vmem_limit_bytes=_VMEM_LIMIT)


_HBM = pl.BlockSpec(memory_space=pltpu.HBM)


def _mod_kernel(c_ref, w_ref, b_ref, o_ref):
    a = _silu(c_ref[...])
    w_hi, w_lo = _split(w_ref[...])
    o_ref[...] = _dot3(a, w_hi, w_lo) + b_ref[...]


def _modulation(cond, w_mod, b_mod):
    tn = 1536
    n6 = 6 * D_MODEL
    return pl.pallas_call(
        _mod_kernel,
        grid=(DEPTH, n6 // tn),
        in_specs=[
            pl.BlockSpec((MOD_ROWS, D_MODEL), lambda l, j: (0, 0)),
            pl.BlockSpec((None, D_MODEL, tn), lambda l, j: (l, 0, j)),
            pl.BlockSpec((None, 1, tn), lambda l, j: (l, 0, j)),
        ],
        out_specs=pl.BlockSpec((None, MOD_ROWS, tn), lambda l, j: (l, 0, j)),
        out_shape=jax.ShapeDtypeStruct((DEPTH, MOD_ROWS, n6), F32),
        compiler_params=_params(("parallel", "parallel")),
        name="modulation",
    )(cond, w_mod, b_mod.reshape(DEPTH, 1, n6))


def _mod_spec(layer, row_fn):
    return pl.BlockSpec((None, None, 1, 6 * D_MODEL), lambda s, i: (layer, row_fn(s), 0, 0))


def _augment_v(v):
    ones = jnp.ones((v.shape[0], DIFF_V), v.dtype)
    parts = []
    for hd in range(H_DIFF):
        parts += [v[:, hd * DIFF_V:(hd + 1) * DIFF_V], ones]
    return jnp.concatenate(parts, axis=1)


def _swap16(x):
    n = x.shape[-1]
    lane = _iota(x.shape, 1)
    return jnp.where((lane & 16) == 0, pltpu.roll(x, n - 16, axis=1), pltpu.roll(x, 16, axis=1))


def _proj_kernel(*refs, latent, n_aliased, layer):
    x_ref, mod_ref, w_ref, lg_ref = refs[:4]
    pos = 4
    if latent:
        cos_ref, sin_ref = refs[pos:pos + 2]
        pos += 2
    pos += n_aliased
    rq_ref, rk_ref, rv_ref, rg_ref, dq_ref, dk_ref, dv_ref, cb_ref, cch_ref = refs[pos:pos + 9]
    pos += 9
    if latent:
        kvf_ref, kvb_ref = refs[pos:pos + 2]
    else:
        dk32_ref, dv32_ref, stf_ref, stb_ref = refs[pos:pos + 4]

    def put(ref, idx, val):
        if n_aliased:
            ref[idx if idx else ...] = val
        else:
            for other in range(DEPTH):
                ref[(other,) + idx] = val if other == layer else jnp.zeros_like(val)

    m = mod_ref[...]
    sh1 = m[:, 0:D_MODEL]
    sc1 = m[:, D_MODEL:2 * D_MODEL]
    h = (x_ref[...] * (1.0 + sc1) + sh1).astype(BF)

    def col(off, width):
        return _dot(h, w_ref[:, off:off + width])

    rq_ref[...] = col(_O_RQ, RET_W).astype(BF)
    rk = col(_O_RK, RET_W) * (RET_DK ** -0.5)
    rk_ref[...] = rk.astype(BF)
    rv = col(_O_RV, RET_W).astype(BF)
    rv_ref[...] = rv
    rg_ref[...] = col(_O_RG, RET_W)

    lg = _log_sigmoid(lg_ref[...])
    p = _iota((TILE, 1), 0).astype(F32)
    kf = (rk * jnp.exp((TILE - 1.0 - p) * lg[0:1])).astype(BF)
    kb = (rk * jnp.exp(p * lg[1:2])).astype(BF)
    kvf = _dot_tn(kf, rv)
    kvb = _dot_tn(kb, rv)
    if latent:
        kvf_ref[...] = kvf
        kvb_ref[...] = kvb
    else:
        for hd in range(H_RET):
            lo = hd * RET_DK
            put(stf_ref, (hd,), kvf[lo:lo + RET_DK, lo:lo + RET_DK])
            put(stb_ref, (hd,), kvb[lo:lo + RET_DK, lo:lo + RET_DK])

    dq = col(_O_DQ, DIFF_W)
    dk = col(_O_DK, DIFF_W)
    dv = col(_O_DV, DIFF_W)
    if latent:
        cos = cos_ref[...]
        sin = sin_ref[...]
        dq = dq * cos + _swap16(dq) * sin
        dk = dk * cos + _swap16(dk) * sin
    else:
        put(dk32_ref, (), dk)
        put(dv32_ref, (), dv)
    dq_ref[...] = (dq * QK_SCALE_LOG2).astype(BF)
    dk_ref[...] = dk.astype(BF)
    dv_ref[...] = _augment_v(dv.astype(BF))

    cb_ref[...] = col(_O_CB, CONV_CH)
    cch_ref[...] = col(_O_CC, CONV_CH) * col(_O_CH, CONV_CH)


def _proj(x, mod, layer, row_fn, w_in_b, lg_lanes, rope_tabs=None, stacked=None):
    S, T, _ = x.shape
    n = T // TILE
    latent = rope_tabs is not None

    def tok(w):
        return pl.BlockSpec((None, TILE, w), lambda s, i: (s, i, 0))

    def tok_shape(w, dt):
        return jax.ShapeDtypeStruct((S, T, w), dt)

    in_specs = [
        tok(D_MODEL), _mod_spec(layer, row_fn),
        pl.BlockSpec((None, D_MODEL, IN_WIDTH), lambda s, i: (layer, 0, 0)),
        pl.BlockSpec((2, RET_W), lambda s, i: (0, 0)),
    ]
    args = [x, mod, w_in_b, lg_lanes]
    out_specs = [tok(RET_W)] * 4 + [tok(DIFF_W)] * 2 + [tok(V_AUG_W)] + [tok(CONV_CH)] * 2
    out_shape = ([tok_shape(RET_W, BF)] * 3 + [tok_shape(RET_W, F32)] + [tok_shape(DIFF_W, BF)] * 2
                 + [tok_shape(V_AUG_W, BF)]
                 + [tok_shape(CONV_CH, F32)] * 2)
    aliases = {}
    if latent:
        in_specs += [pl.BlockSpec((TILE, DIFF_W), lambda s, i: (i, 0))] * 2
        args += list(rope_tabs)
        out_specs += [pl.BlockSpec((None, None, RET_W, RET_W), lambda s, i: (s, i, 0, 0))] * 2
        out_shape += [jax.ShapeDtypeStruct((S, n, RET_W, RET_W), F32)] * 2
    else:
        assert n == 1
        kv_shape = jax.ShapeDtypeStruct((S, DEPTH, T, DIFF_W), F32)
        st_shape = jax.ShapeDtypeStruct((S, DEPTH, H_RET, RET_DK, RET_DK), F32)
        if stacked is not None:
            in_specs += [_HBM] * 4
            args += list(stacked)
            aliases = {len(args) - 4 + j: len(out_shape) + j for j in range(4)}
            out_specs += [pl.BlockSpec((None, None, TILE, DIFF_W), lambda s, i: (s, layer, i, 0))] * 2
            out_specs += [pl.BlockSpec((None, None, H_RET, RET_DK, RET_DK), lambda s, i: (s, layer, 0, 0, 0))] * 2
        else:
            out_specs += [pl.BlockSpec((None, DEPTH, TILE, DIFF_W), lambda s, i: (s, 0, i, 0))] * 2
            out_specs += [pl.BlockSpec((None, DEPTH, H_RET, RET_DK, RET_DK), lambda s, i: (s, 0, 0, 0, 0))] * 2
        out_shape += [kv_shape, kv_shape, st_shape, st_shape]
    return pl.pallas_call(
        functools.partial(_proj_kernel, latent=latent, n_aliased=len(aliases), layer=layer),
        grid=(S, n),
        in_specs=in_specs,
        out_specs=out_specs,
        out_shape=out_shape,
        input_output_aliases=aliases,
        compiler_params=_params(("parallel", "parallel")),
        name="proj_lat" if latent else "proj_ctx",
    )(*args)


def _scan_kernel(kvf_ref, kvb_ref, s0f_ref, s0b_ref, lg_ref, sf_ref, sb_ref, *, n):
    lg = _log_sigmoid(lg_ref[...])
    dec = jnp.exp(float(TILE) * lg)
    same_head = (_iota((RET_W, RET_W), 0) >> 6) == (_iota((RET_W, RET_W), 1) >> 6)

    s = jnp.where(same_head, s0f_ref[...], 0.0)
    for c in range(n):
        sf_ref[c] = s
        s = s * dec[0:1] + jnp.where(same_head, kvf_ref[c], 0.0)
    s = jnp.where(same_head, s0b_ref[...], 0.0)
    for c in reversed(range(n)):
        sb_ref[c] = s
        s = s * dec[1:2] + jnp.where(same_head, kvb_ref[c], 0.0)


def _scan(kvf, kvb, s0f, s0b, lg_lanes):
    S, n = kvf.shape[:2]
    chunks = pl.BlockSpec((None, n, RET_W, RET_W), lambda s: (s, 0, 0, 0))
    one = pl.BlockSpec((None, RET_W, RET_W), lambda s: (s, 0, 0))
    return pl.pallas_call(
        functools.partial(_scan_kernel, n=n),
        grid=(S,),
        in_specs=[chunks, chunks, one, one, pl.BlockSpec((2, RET_W), lambda s: (0, 0))],
        out_specs=[chunks, chunks],
        out_shape=[jax.ShapeDtypeStruct((S, n, RET_W, RET_W), F32)] * 2,
        compiler_params=_params(("parallel",)),
        name="ret_scan",
    )(kvf, kvb, s0f, s0b, lg_lanes)


def _diff_heads(dq_ref, kv_refs, heads, lam):
    lane = _iota((1, DIFF_V), 1)
    scores = []
    for hd in heads:
        lo = hd * DIFF_V
        q_h = dq_ref[:, lo:lo + DIFF_V]
        qs = jnp.concatenate([jnp.where(lane < DIFF_QK, q_h, 0), jnp.where(lane >= DIFF_QK, q_h, 0)], axis=0)
        scores.append(jnp.concatenate([_dot_nt(qs, k[:, lo:lo + DIFF_V]) for k, _ in kv_refs], axis=1))
    s = jnp.concatenate(scores, axis=0)
    p = jnp.exp2(s - jnp.max(s, axis=-1, keepdims=True))

    def times_v(w, hd, width):
        acc = None
        start = 0
        for k, v in kv_refs:
            part = _dot(w[:, start:start + k.shape[0]], v[:, hd * 2 * DIFF_V:hd * 2 * DIFF_V + width])
            acc = part if acc is None else acc + part
            start += k.shape[0]
        return acc

    outs = []
    p = p.astype(BF)
    for n, hd in enumerate(heads):
        acc = times_v(p[n * 2 * TILE:(n + 1) * 2 * TILE], hd, 2 * DIFF_V)
        o0 = acc[0:TILE, 0:DIFF_V] * (1.0 / acc[0:TILE, DIFF_V:DIFF_V + 1])
        o1 = acc[TILE:, 0:DIFF_V] * (lam / acc[TILE:, DIFF_V:DIFF_V + 1])
        outs.append(o0 - o1)
    return outs


def _mixer_kernel(*refs, n_tiles, n_total, lam_init, cached):
    n_a = _n_token_mix_refs(cached)
    mix_even, mix_odd, decay_ref = refs[-3:]
    t = pl.program_id(0)
    tile_in_seq = lax.rem(jnp.minimum(t, n_total - 1), n_tiles)

    @pl.when(t == 0)
    def _():
        mix_odd[...] = jnp.zeros_like(mix_odd)
        lg = _log_sigmoid(refs[n_a - 5][...])
        dist = (_iota((TILE, TILE), 0) - _iota((TILE, TILE), 1)).astype(F32)
        diag2 = jnp.where(dist == 0.0, 2.0, 1.0)
        for hd in range(H_RET):
            lgf = lg[0:1, hd * RET_DK:hd * RET_DK + 1]
            lgb = lg[1:2, hd * RET_DK:hd * RET_DK + 1]
            decay_ref[hd] = jnp.exp(jnp.abs(dist) * jnp.where(dist > 0.0, lgf, lgb)) * diag2

    def step(mix_write, mix_read):
        mix_write[...] = _token_mix(*refs[:n_a], decay_ref, tile_in_seq=tile_in_seq, n_tiles=n_tiles,
                                    lam_init=lam_init, cached=cached)
        _project_route(mix_read[...], *refs[n_a:-3])

    @pl.when(lax.rem(t, 2) == 0)
    def _():
        step(mix_even, mix_odd)

    @pl.when(lax.rem(t, 2) == 1)
    def _():
        step(mix_odd, mix_even)


def _n_token_mix_refs(cached):
    return 18 + (2 if cached else 0)


def _token_mix(*refs, tile_in_seq, n_tiles, lam_init, cached):
    rq_ref, rk_ref, rv_ref, rg_ref, sf_ref, sb_ref, dq_ref = refs[:7]
    pos = 7
    kv_refs = []
    if cached:
        kv_refs.append((refs[pos], refs[pos + 1]))
        pos += 2
    kv_refs.append((refs[pos], refs[pos + 1]))
    pos += 2
    (cb_ref, cch_ref, cprev_ref, cnext_ref, lg_ref, lamp_ref, subg_ref, cw_ref, cbias_ref,
     decay_ref) = refs[pos:pos + 10]
    i = tile_in_seq

    lg = _log_sigmoid(lg_ref[...])
    head_of_lane = _iota((1, RET_W), 1) >> 6
    q = rq_ref[...]
    k = rk_ref[...]
    v = rv_ref[...]
    ret_o = jnp.zeros((TILE, RET_W), F32)
    for hd in range(H_RET):
        in_head = head_of_lane == hd
        sc = _dot_nt(jnp.where(in_head, q, 0), k)
        ret_o = ret_o + jnp.where(in_head, _dot((sc * decay_ref[hd]).astype(BF), v), 0.0)
    p = _iota((TILE, 1), 0).astype(F32)
    ret_o = ret_o + _dot(q, sf_ref[...].astype(BF)) * jnp.exp((p + 1.0) * lg[0:1])
    ret_o = ret_o + _dot(q, sb_ref[...].astype(BF)) * jnp.exp((float(TILE) - p) * lg[1:2])
    avg = jnp.where((_iota((RET_W, RET_W), 0) >> 6) == (_iota((RET_W, RET_W), 1) >> 6),
                    1.0 / RET_DK, 0.0).astype(BF)
    rc = ret_o - _dot_hl(ret_o, avg)
    ret = rc * lax.rsqrt(_dot_hl(rc * rc, avg) + EPS) * _silu(rg_ref[...])

    lp = lamp_ref[...]
    lam = (jnp.exp(jnp.sum(lp[0:1] * lp[1:2], axis=-1, keepdims=True))
           - jnp.exp(jnp.sum(lp[2:3] * lp[3:4], axis=-1, keepdims=True)) + lam_init)
    subg = subg_ref[...] * (1.0 - lam_init)
    head_groups = [[hd] for hd in range(H_DIFF)] if cached else [list(range(H_DIFF))]
    heads = []
    for group in head_groups:
        for o in _diff_heads(dq_ref, kv_refs, group, lam):
            o = o * lax.rsqrt(jnp.mean(o * o, axis=-1, keepdims=True) + EPS) * subg
            heads.append(o.astype(BF))
    diff = jnp.concatenate(heads, axis=1)

    cch = cch_ref[...]
    prev = jnp.where(i > 0, cprev_ref[7:8, :], 0.0)
    nxt = jnp.where(i < n_tiles - 1, cnext_ref[0:1, :], 0.0)
    r = _iota((TILE, 1), 0)
    up = jnp.where(r == 0, prev, pltpu.roll(cch, 1, axis=0))
    dn = jnp.where(r == TILE - 1, nxt, pltpu.roll(cch, TILE - 1, axis=0))
    cw = cw_ref[...]
    conv = cb_ref[...] * (up * cw[0:1] + cch * cw[1:2] + dn * cw[2:3] + cbias_ref[...])
    return jnp.concatenate([ret.astype(BF), diff, conv.astype(BF)], axis=1)


def _project_route(mixed, x_ref, mod_ref, wout_ref, lng_ref, lnb_ref, wrh_ref, wrl_ref, br_ref,
                   x1_ref, hs_ref, slots_ref, cnt_ref):
    m = mod_ref[...]
    g1 = m[:, 2 * D_MODEL:3 * D_MODEL]
    sh2 = m[:, 3 * D_MODEL:4 * D_MODEL]
    sc2 = m[:, 4 * D_MODEL:5 * D_MODEL]
    x1 = _layer_norm(ALPHA * x_ref[...] + g1 * _dot(mixed, wout_ref[...]), lng_ref[...], lnb_ref[...])
    x1_ref[...] = x1

    h2 = x1 * (1.0 + sc2) + sh2
    h2h, h2l = _split(h2)
    wrh = wrh_ref[...]
    logits = (_dot_nt(wrh, h2h) + (_dot_nt(wrh, h2l) + _dot_nt(wrl_ref[...], h2h))) + br_ref[...]
    neg = -jnp.inf
    g_id = _iota((EXPERT_ROW0, TILE), 0)
    gl = jnp.where(g_id < N_GROUPS, logits[0:EXPERT_ROW0], neg)
    gmax = jnp.max(gl, axis=0, keepdims=True)
    g_idx = jnp.min(jnp.where(gl == gmax, g_id, N_GROUPS), axis=0, keepdims=True)
    g_w = 1.0 / jnp.sum(jnp.exp(gl - gmax), axis=0, keepdims=True)
    e_id = _iota((N_EXPERTS, TILE), 0)
    el = jnp.where((e_id >> 2) == g_idx, logits[EXPERT_ROW0:EXPERT_ROW0 + N_EXPERTS], neg)
    v1 = jnp.max(el, axis=0, keepdims=True)
    i1 = jnp.min(jnp.where(el == v1, e_id, N_EXPERTS), axis=0, keepdims=True)
    el2 = jnp.where(e_id == i1, neg, el)
    v2 = jnp.max(el2, axis=0, keepdims=True)
    i2 = jnp.min(jnp.where(el2 == v2, e_id, N_EXPERTS), axis=0, keepdims=True)
    t = jnp.exp(v2 - v1)
    w1 = g_w / (1.0 + t)
    w2 = g_w * t / (1.0 + t)

    sel1 = e_id == i1
    sel2 = e_id == i2
    sel = jnp.where(sel1 | sel2, 1.0, 0.0)
    earlier = _ones_where(_iota((TILE, TILE), 0) < _iota((TILE, TILE), 1))
    rank = _dot(sel.astype(BF), earlier)
    cnt = jnp.sum(sel, axis=1, keepdims=True).astype(I32)
    padded = ((cnt + (CHUNK - 1)) >> 4) << 4
    incl = jnp.broadcast_to(padded, (N_EXPERTS, LANES))
    e_row = _iota((N_EXPERTS, LANES), 0)
    for step in (1, 2, 4, 8):
        incl = incl + jnp.where(e_row >= step, pltpu.roll(incl, step, axis=0), 0)
    seg_off = (incl[:, 0:1] - padded).astype(F32)
    spos = seg_off + rank
    slot1 = jnp.sum(jnp.where(sel1, spos, 0.0), axis=0, keepdims=True).astype(I32)
    slot2 = jnp.sum(jnp.where(sel2, spos, 0.0), axis=0, keepdims=True).astype(I32)
    s_id = _iota((CAP, TILE), 0)
    at1 = s_id == slot1
    at2 = s_id == slot2
    hs_ref[:, 0:D_MODEL] = _dot(_ones_where(at1 | at2), h2h).astype(BF)
    w1h = w1.astype(BF).astype(F32)
    w2h = w2.astype(BF).astype(F32)
    ones = jnp.ones((TILE, LANES), BF)
    g_hi = _dot((jnp.where(at1, w1h, 0.0) + jnp.where(at2, w2h, 0.0)).astype(BF), ones)
    g_lo = _dot((jnp.where(at1, w1 - w1h, 0.0) + jnp.where(at2, w2 - w2h, 0.0)).astype(BF), ones)
    lane = _iota((CAP, LANES), 1)
    hs_ref[:, D_MODEL:] = jnp.where(lane == 0, g_hi, jnp.where(lane == 1, g_lo, 0.0)).astype(BF)
    cnt_ref[...] = jnp.broadcast_to(padded, (N_EXPERTS, LANES))

    dr = _iota((LANES, TILE), 0)
    digits = jnp.where(dr == 0, slot1 & (SLOT_RADIX - 1), jnp.where(dr == 1, slot1 >> 5,
             jnp.where(dr == 2, slot2 & (SLOT_RADIX - 1), jnp.where(dr == 3, slot2 >> 5, 0))))
    eye = _ones_where(_iota((TILE, TILE), 0) == _iota((TILE, TILE), 1))
    cols = _dot_nt(eye, digits.astype(F32).astype(BF))
    s1c = (cols[:, 0:1] + SLOT_RADIX * cols[:, 1:2]).astype(I32)
    s2c = (cols[:, 2:3] + SLOT_RADIX * cols[:, 3:4]).astype(I32)
    tl = _iota((TILE, LANES), 1)
    slots_ref[...] = jnp.where(tl == 0, s1c, jnp.where(tl == 1, s2c, 0))


def _mixer(x, mod, layer, row_fn, pr, states, cache_kv, new_kv, w_out_b, lg_lanes, lamp, subg, cw, cbias,
           lng, lnb, wrh, wrl, br, lam_init):
    S, T, _ = x.shape
    n = T // TILE
    rq, rk, rv, rg, dq, cb, cch = pr
    t8 = TILE // 8
    cached = cache_kv is not None
    last = S * n - 1

    def at_mix(t):
        return jnp.minimum(t, last) // n, jnp.minimum(t, last) % n

    def at_out(t):
        return jnp.maximum(t - 1, 0) // n, jnp.maximum(t - 1, 0) % n

    def tok(w, at):
        return pl.BlockSpec((None, TILE, w), lambda t: (*at(t), 0))

    def full(shape):
        return pl.BlockSpec(shape, lambda t: (0,) * len(shape))

    def seq(a):
        mode = pl.Buffered(1) if n > 1 else None
        return pl.BlockSpec((None,) + a.shape[1:], lambda t: (at_mix(t)[0], 0, 0), pipeline_mode=mode)

    def halo(offset):
        def index(t):
            s, i = at_mix(t)
            return (s, jnp.clip(i * t8 + offset, 0, T // 8 - 1), 0)
        return pl.BlockSpec((None, 8, CONV_CH), index)

    if states is None:
        zero_state = jnp.zeros((1, 1, RET_W, RET_W), F32)
        states = (zero_state, zero_state)
        state = pl.BlockSpec((None, None, RET_W, RET_W), lambda t: (0, 0, 0, 0))
    else:
        state = pl.BlockSpec((None, None, RET_W, RET_W), lambda t: (*at_mix(t), 0, 0))

    in_specs = [tok(RET_W, at_mix)] * 4 + [state, state, tok(DIFF_W, at_mix)]
    args = [rq, rk, rv, rg, states[0], states[1], dq]
    for kv in ([cache_kv] if cached else []) + [new_kv]:
        in_specs += [seq(kv[0]), seq(kv[1])]
        args += list(kv)
    in_specs += [tok(CONV_CH, at_mix), tok(CONV_CH, at_mix), halo(-1), halo(t8),
                 full((2, RET_W)), full((4, DIFF_QK)), full((1, DIFF_V)), full((3, CONV_CH)), full((1, CONV_CH))]
    args += [cb, cch, cch, cch, lg_lanes, lamp, subg, cw, cbias]
    assert len(args) == _n_token_mix_refs(cached)
    in_specs += [tok(D_MODEL, at_out),
                 pl.BlockSpec((None, None, 1, 6 * D_MODEL), lambda t: (layer, row_fn(at_out(t)[0]), 0, 0)),
                 pl.BlockSpec((None, D_MODEL, D_MODEL), lambda t: (layer, 0, 0)),
                 full((1, D_MODEL)), full((1, D_MODEL)),
                 full((ROUTER_ROWS, D_MODEL)), full((ROUTER_ROWS, D_MODEL)), full((ROUTER_ROWS, 1))]
    args += [x, mod, w_out_b, lng, lnb, wrh, wrl, br]

    def out_tile(t):
        s, i = at_out(t)
        return s * n + i

    return pl.pallas_call(
        functools.partial(_mixer_kernel, n_tiles=n, n_total=S * n, lam_init=lam_init, cached=cached),
        grid=(S * n + 1,),
        in_specs=in_specs,
        out_specs=[tok(D_MODEL, at_out),
                   pl.BlockSpec((CAP, HS_W), lambda t: (out_tile(t), 0)),
                   tok(LANES, at_out),
                   pl.BlockSpec((None, None, N_EXPERTS, LANES), lambda t: (*at_out(t), 0, 0))],
        out_shape=[jax.ShapeDtypeStruct((S, T, D_MODEL), F32),
                   jax.ShapeDtypeStruct((S * n * CAP, HS_W), BF),
                   jax.ShapeDtypeStruct((S, T, LANES), I32),
                   jax.ShapeDtypeStruct((S, n, N_EXPERTS, LANES), I32)],
        scratch_shapes=[pltpu.VMEM((TILE, D_MODEL), BF), pltpu.VMEM((TILE, D_MODEL), BF),
                        pltpu.VMEM((H_RET, TILE, TILE), F32)],
        compiler_params=_params(("arbitrary",)),
        name="mixer_lat" if cached else "mixer_ctx",
    )(*args)


def _block_schedule(padded, chunks_a):
    NT = padded.shape[0]
    n_chunks = NT * CHUNKS_PER_TILE
    n_blocks = n_chunks // CPB + N_EXPERTS
    c16 = padded // CHUNK
    ends = jnp.cumsum(c16, axis=1)
    before = jnp.cumsum(c16, axis=0) - c16
    per_expert = jnp.sum(c16, axis=0)
    nb = (per_expert + (CPB - 1)) // CPB
    b_end = jnp.cumsum(nb)
    q = jnp.arange(CHUNKS_PER_TILE, dtype=I32)
    e = jnp.arange(N_EXPERTS, dtype=I32)
    key = jnp.sum((ends[:, None, :] <= q[None, :, None]).astype(I32), axis=-1)
    pos_e = ((b_end - nb) * CPB + before)[:, None, :] + (q[None, :, None] - (ends - c16)[:, None, :])
    pos = jnp.sum(jnp.where(key[..., None] == e, pos_e, 0), axis=-1)
    pos = jnp.where(key < N_EXPERTS, pos, -1).reshape(-1)
    match = pos[None, :] == jnp.arange(n_blocks * CPB, dtype=I32)[:, None]
    cid = jnp.sum(jnp.where(match, jnp.arange(n_chunks, dtype=I32)[None, :], 0), axis=-1)
    nv = jnp.sum(match.astype(I32).reshape(n_blocks, -1), axis=-1)
    first = match & (jnp.arange(n_chunks, dtype=I32) < chunks_a)[None, :]
    nv_a = jnp.sum(first.astype(I32).reshape(n_blocks, -1), axis=-1)
    b = jnp.arange(n_blocks, dtype=I32)
    eb = jnp.minimum(jnp.sum((b_end[None, :] <= b[:, None]).astype(I32), axis=-1), N_EXPERTS - 1)
    zero = jnp.zeros((1,), I32)
    counts = jnp.concatenate([nv, zero, nv_a, zero]).astype(I32)
    return eb.astype(I32), counts, jnp.concatenate([cid, jnp.zeros((CPB,), I32)]).astype(I32)


def _moe_kernel(eb_ref, nv_ref, cid_ref, hs_a, hs_b, wg_ref, wu_ref, wd_ref, ys_a, ys_b,
                xbuf, ybuf, wg_b, wu_b, wd_b, in_sem, out_sem, *, n_blocks, chunks_a):
    b = pl.program_id(0)

    def chunk_rows(c):
        return pl.ds(pl.multiple_of(c * CHUNK, CHUNK), CHUNK)

    def gather_copy(which, c, j, slot):
        src = (hs_a, hs_b)[which]
        return pltpu.make_async_copy(src.at[chunk_rows(c)], xbuf.at[slot, chunk_rows(j)], in_sem.at[slot])

    def scatter_copy(which, c, j, slot):
        dst = (ys_a, ys_b)[which]
        return pltpu.make_async_copy(ybuf.at[slot, chunk_rows(j)],
                                     dst.at[chunk_rows(c), pl.ds(0, D_MODEL)], out_sem.at[slot])

    def for_chunks(blk, fn):
        n_a = nv_ref[n_blocks + 1 + blk]

        def body_a(j, carry):
            fn(j, 0, cid_ref[blk * CPB + j])
            return carry

        def body_b(j, carry):
            fn(j, 1, cid_ref[blk * CPB + j] - chunks_a)
            return carry
        lax.fori_loop(0, n_a, body_a, 0)
        lax.fori_loop(n_a, nv_ref[blk], body_b, 0)

    def each_chunk(blk, fn):
        n_a = nv_ref[n_blocks + 1 + blk]
        n_v = nv_ref[blk]
        for j in range(CPB):
            c = cid_ref[blk * CPB + j]

            @pl.when(j < n_a)
            def _():
                fn(j, 0, c)

            @pl.when((j >= n_a) & (j < n_v))
            def _():
                fn(j, 1, c - chunks_a)

    def wait_gathers(blk, slot):
        rows = pl.ds(0, nv_ref[blk] * CHUNK)

        @pl.when(nv_ref[blk] > 0)
        def _():
            pltpu.make_async_copy(hs_a.at[rows], xbuf.at[slot, rows], in_sem.at[slot]).wait()

    def wait_scatters(blk, slot):
        rows = pl.ds(0, nv_ref[blk] * CHUNK)

        @pl.when(nv_ref[blk] > 0)
        def _():
            pltpu.make_async_copy(ybuf.at[slot, rows], ys_a.at[rows, pl.ds(0, D_MODEL)], out_sem.at[slot]).wait()

    slot = lax.rem(b, 2)
    other = 1 - slot

    @pl.when(b == 0)
    def _():
        xbuf[...] = jnp.zeros_like(xbuf)
        for_chunks(0, lambda j, w, c: gather_copy(w, c, j, 0).start())

    @pl.when((b == 0) | (eb_ref[b] != eb_ref[jnp.maximum(b - 1, 0)]))
    def _():
        wg_b[...] = wg_ref[...].astype(BF)
        wu_b[...] = wu_ref[...].astype(BF)
        wd_b[...] = wd_ref[...].astype(BF)

    wait_gathers(b, slot)

    @pl.when(b >= 2)
    def _():
        wait_scatters(b - 2, slot)

    @pl.when(nv_ref[b] > 0)
    def _():
        each_chunk(b + 1, lambda j, w, c: gather_copy(w, c, j, other).start())
        x = xbuf[slot, :, 0:D_MODEL]
        gate = (xbuf[slot, :, D_MODEL:D_MODEL + 1].astype(F32)
                + xbuf[slot, :, D_MODEL + 1:D_MODEL + 2].astype(F32))
        hg = _dot(x, wg_b[...])
        hu = _dot(x, wu_b[...])
        act = _silu(hg) * hu * gate
        ybuf[slot] = _dot(act.astype(BF), wd_b[...]).astype(BF)
        each_chunk(b, lambda j, w, c: scatter_copy(w, c, j, slot).start())

    @pl.when(b == n_blocks - 1)
    def _():
        if n_blocks >= 2:
            wait_scatters(b - 1, other)
        wait_scatters(b, slot)


def _moe(hs_a, hs_b, eb, nv, cid, layer, w_gate, w_up, w_down):
    n_blocks = eb.shape[0]

    def w_spec(a, c):
        return pl.BlockSpec((None, None, a, c), lambda b, eb, nv, cid: (layer, eb[b], 0, 0))

    grid_spec = pltpu.PrefetchScalarGridSpec(
        num_scalar_prefetch=3,
        grid=(n_blocks,),
        in_specs=[_HBM, _HBM, w_spec(D_MODEL, D_EXPERT), w_spec(D_MODEL, D_EXPERT), w_spec(D_EXPERT, D_MODEL)],
        out_specs=[_HBM, _HBM],
        scratch_shapes=[pltpu.VMEM((2, MOE_TMB, HS_W), BF), pltpu.VMEM((2, MOE_TMB, D_MODEL), BF),
                        pltpu.VMEM((D_MODEL, D_EXPERT), BF), pltpu.VMEM((D_MODEL, D_EXPERT), BF),
                        pltpu.VMEM((D_EXPERT, D_MODEL), BF),
                        pltpu.SemaphoreType.DMA((2,)), pltpu.SemaphoreType.DMA((2,))],
    )
    return pl.pallas_call(
        functools.partial(_moe_kernel, n_blocks=n_blocks, chunks_a=hs_a.shape[0] // CHUNK),
        grid_spec=grid_spec,
        out_shape=[jax.ShapeDtypeStruct(hs_a.shape, BF), jax.ShapeDtypeStruct(hs_b.shape, BF)],
        input_output_aliases={3: 0, 4: 1},
        compiler_params=_params(("arbitrary",)),
        name="moe_sorted",
    )(eb, nv, cid, hs_a, hs_b, w_gate, w_up, w_down)


def _combine_kernel(ys_ref, slots_ref, x1_ref, mod_ref, lng_ref, lnb_ref, o_ref):
    g2 = mod_ref[:, 5 * D_MODEL:6 * D_MODEL]
    s_lane = _iota((TILE, CAP), 1)
    for u in range(COMBINE_TILES_PER_STEP):
        sl = slots_ref[u]
        pick = _ones_where((s_lane == sl[:, 0:1]) | (s_lane == sl[:, 1:2]))
        ffn = _dot(pick, ys_ref[u * CAP:(u + 1) * CAP, 0:D_MODEL])
        o_ref[u] = _layer_norm(ALPHA * x1_ref[u] + g2 * ffn, lng_ref[...], lnb_ref[...])


def _combine(ys, slots, x1, mod, layer, row_fn, lng, lnb):
    S, T, _ = x1.shape
    n = T // TILE
    U = COMBINE_TILES_PER_STEP
    assert (S * n) % U == 0 and (n % U == 0 or n == 1)

    def tok(w):
        return pl.BlockSpec((U, TILE, w), lambda t: (t, 0, 0))

    out = pl.pallas_call(
        _combine_kernel,
        grid=(S * n // U,),
        in_specs=[
            pl.BlockSpec((U * CAP, HS_W), lambda t: (t, 0)),
            tok(LANES), tok(D_MODEL),
            pl.BlockSpec((None, None, 1, 6 * D_MODEL), lambda t: (layer, row_fn(t * U // n), 0, 0)),
            pl.BlockSpec((1, D_MODEL), lambda t: (0, 0)),
            pl.BlockSpec((1, D_MODEL), lambda t: (0, 0)),
        ],
        out_specs=tok(D_MODEL),
        out_shape=jax.ShapeDtypeStruct((S * n, TILE, D_MODEL), F32),
        compiler_params=_params(("parallel",)),
        name="moe_combine",
    )(ys, slots.reshape(S * n, TILE, LANES), x1.reshape(S * n, TILE, D_MODEL), mod, lng, lnb)
    return out.reshape(S, T, D_MODEL)


def _rope_tables(n_lat):
    half = DIFF_QK // 2
    pairs = half // 2
    inv = (1.0 / (ROPE_BASE ** (np.arange(pairs, dtype=np.float32) * 2.0 / half))).astype(np.float32)
    t = np.arange(n_lat)
    ang_r = ((t // GRID_W).astype(np.float32)[:, None] * inv[None, :]).astype(np.float64)
    ang_c = ((t % GRID_W).astype(np.float32)[:, None] * inv[None, :]).astype(np.float64)
    cos = np.concatenate([np.cos(ang_r)] * 2 + [np.cos(ang_c)] * 2, axis=1)
    sin = np.concatenate([-np.sin(ang_r), np.sin(ang_r), -np.sin(ang_c), np.sin(ang_c)], axis=1)
    reps = DIFF_W // DIFF_QK
    return (jnp.asarray(np.tile(cos, (1, reps)), F32), jnp.asarray(np.tile(sin, (1, reps)), F32))


def _block_diag(s):
    S = s.shape[0]
    eye = jnp.eye(H_RET, dtype=s.dtype)
    return jnp.einsum('shdv,hg->shdgv', s, eye).reshape(S, RET_W, RET_W)


def _router_rows(w_group, b_group, w_expert, b_expert):
    pad = EXPERT_ROW0 - N_GROUPS
    tail = ROUTER_ROWS - EXPERT_ROW0 - N_EXPERTS
    w = jnp.concatenate([w_group.T, jnp.zeros((pad, D_MODEL), F32),
                         w_expert.reshape(D_MODEL, N_EXPERTS).T, jnp.zeros((tail, D_MODEL), F32)], axis=0)
    bias = jnp.concatenate([b_group, jnp.zeros((pad,), F32), b_expert.reshape(N_EXPERTS), jnp.zeros((tail,), F32)])
    hi = w.astype(BF)
    return hi, (w - hi.astype(F32)).astype(BF), bias.reshape(ROUTER_ROWS, 1)


def kernel(x_prompt, x_sample, cache_diff_k, cache_diff_v, state_ret_fwd, state_ret_bwd, c, c_ctx, w_mod, b_mod, w_in, ret_decay_logit, diff_lambda, diff_subln_g, conv_w, conv_b, w_out, ln_g, ln_b, w_router_group, b_router_group, w_router_expert, b_router_expert, w_gate, w_up, w_down):
    B, T_ctx, _ = x_prompt.shape
    Bd, T_lat, _ = x_sample.shape
    assert T_ctx == TILE and T_lat % TILE == 0 and T_lat % GRID_W == 0
    assert 1 + Bd <= MOD_ROWS

    cond = jnp.concatenate([c_ctx[None, :], c, jnp.zeros((MOD_ROWS - 1 - Bd, D_MODEL), F32)], axis=0)
    mod = _modulation(cond, w_mod, b_mod).reshape(DEPTH, MOD_ROWS, 1, 6 * D_MODEL)
    rope_tabs = _rope_tables(T_lat)
    w_in_b = w_in.astype(BF)
    w_out_b = w_out.astype(BF)

    ctx_row = lambda s: 0
    lat_row = lambda s: s + 1
    ctx_tiles = B * (T_ctx // TILE)
    lat_tiles = Bd * (T_lat // TILE)

    yp, ys = x_prompt, x_sample
    stacked = None
    for l in range(DEPTH):
        lam_init = 0.8 - 0.6 * math.exp(-0.3 * l)
        lg_lanes = jnp.repeat(ret_decay_logit[l], RET_DK, axis=1)
        wrh, wrl, br = _router_rows(w_router_group[l], b_router_group[l], w_router_expert[l], b_router_expert[l])
        shared = (w_out_b, lg_lanes, diff_lambda[l], diff_subln_g[l].reshape(1, DIFF_V), conv_w[l],
                  conv_b[l].reshape(1, CONV_CH), ln_g[l, 0].reshape(1, D_MODEL), ln_b[l, 0].reshape(1, D_MODEL),
                  wrh, wrl, br, lam_init)
        ln2 = (ln_g[l, 1].reshape(1, D_MODEL), ln_b[l, 1].reshape(1, D_MODEL))

        pr = _proj(yp, mod, l, ctx_row, w_in_b, lg_lanes, stacked=stacked)
        rq, rk, rv, rg, dq, dk, dv, cb, cch = pr[:9]
        stacked = pr[9:]
        x1_c, hs_c, slots_c, cnt_c = _mixer(yp, mod, l, ctx_row, (rq, rk, rv, rg, dq, cb, cch), None, None,
                                            (dk, dv), *shared)

        pr = _proj(ys, mod, l, lat_row, w_in_b, lg_lanes, rope_tabs=rope_tabs)
        rq, rk, rv, rg, dq, dk, dv, cb, cch, kvf, kvb = pr
        states = _scan(kvf, kvb, _block_diag(state_ret_fwd[:, l]), _block_diag(state_ret_bwd[:, l]), lg_lanes)
        cache_v = cache_diff_v[:, l].astype(BF)
        cache_kv = (cache_diff_k[:, l].reshape(Bd, -1, DIFF_W).astype(BF),
                    jnp.concatenate([cache_v, jnp.ones_like(cache_v)], axis=-1).reshape(Bd, -1, V_AUG_W))
        x1_l, hs_l, slots_l, cnt_l = _mixer(ys, mod, l, lat_row, (rq, rk, rv, rg, dq, cb, cch), states, cache_kv,
                                            (dk, dv), *shared)

        padded = jnp.concatenate([cnt_c.reshape(ctx_tiles, N_EXPERTS, LANES)[:, :, 0],
                                  cnt_l.reshape(lat_tiles, N_EXPERTS, LANES)[:, :, 0]], axis=0)
        eb, nv, cid = _block_schedule(padded, ctx_tiles * CHUNKS_PER_TILE)
        out_c, out_l = _moe(hs_c, hs_l, eb, nv, cid, l, w_gate, w_up, w_down)
        yp = _combine(out_c, slots_c, x1_c, mod, l, ctx_row, *ln2)
        ys = _combine(out_l, slots_l, x1_l, mod, l, lat_row, *ln2)

    new_k, new_v, st_f, st_b = stacked
    return (yp, ys, new_k.reshape(B, DEPTH, T_ctx, H_DIFF, 2, DIFF_QK),
            new_v.reshape(B, DEPTH, T_ctx, H_DIFF, DIFF_V), st_f, st_b)
```

```python
import functools
import math

import numpy as np
import jax
import jax.numpy as jnp
from jax import lax
from jax.experimental import pallas as pl
from jax.experimental.pallas import tpu as pltpu

D_MODEL = 1024
DEPTH = 2
GRID_W = 64
H_RET = 4
RET_DK = 64
RET_W = H_RET * RET_DK
H_DIFF = 4
DIFF_QK = 64
DIFF_V = 2 * DIFF_QK
DIFF_W = H_DIFF * DIFF_V
CONV_CH = 256
ROPE_BASE = 10000.0
N_GROUPS = 4
EXPERTS_PER_GROUP = 4
N_EXPERTS = N_GROUPS * EXPERTS_PER_GROUP
D_EXPERT = 512
ALPHA = (2 * DEPTH) ** 0.25
EPS = 1e-5
MOD_ROWS = 8
LANES = 128
ROUTER_ROWS = 32
EXPERT_ROW0 = 8

TILE = 256
CHUNK = 16
CAP = 2 * TILE + N_EXPERTS * CHUNK
CHUNKS_PER_TILE = CAP // CHUNK
COMBINE_TILES_PER_STEP = 2
PROJ_TILES = 2
MOE_TMB = 512
CPB = MOE_TMB // CHUNK
HS_W = D_MODEL + LANES
V_AUG_W = 2 * DIFF_W
QK_SCALE_LOG2 = (DIFF_QK ** -0.5) * math.log2(math.e)
SLOT_RADIX = 32

_O_RQ, _O_RK, _O_RV, _O_RG = 0, 256, 512, 768
_O_DQ, _O_DK, _O_DV = 1024, 1536, 2048
_O_CB, _O_CC, _O_CH = 2560, 2816, 3072
IN_WIDTH = 3328

BF = jnp.bfloat16
F32 = jnp.float32
I32 = jnp.int32

_VMEM_LIMIT = 56 * 1024 * 1024


def _dot(a, b):
    return jnp.dot(a, b, preferred_element_type=F32)


def _dot_nt(a, b):
    return lax.dot_general(a, b, (((1,), (1,)), ((), ())), preferred_element_type=F32)


def _dot_tn(a, b):
    return lax.dot_general(a, b, (((0,), (0,)), ((), ())), preferred_element_type=F32)


def _split(x):
    hi = x.astype(BF)
    lo = (x - hi.astype(F32)).astype(BF)
    return hi, lo


def _dot_hl(x, w_bf16):
    hi, lo = _split(x)
    return _dot(hi, w_bf16) + _dot(lo, w_bf16)


def _dot3(x, w_hi, w_lo):
    hi, lo = _split(x)
    return _dot(hi, w_hi) + (_dot(lo, w_hi) + _dot(hi, w_lo))


def _iota(shape, dim):
    return lax.broadcasted_iota(I32, shape, dim)


def _ones_where(cond):
    return jnp.where(cond, 1.0, 0.0).astype(BF)


def _log_sigmoid(x):
    return jnp.minimum(x, 0.0) - jnp.log1p(jnp.exp(-jnp.abs(x)))


def _silu(x):
    return x * jax.nn.sigmoid(x)


def _layer_norm(y, g, b):
    mu = jnp.mean(y, axis=-1, keepdims=True)
    yc = y - mu
    var = jnp.mean(yc * yc, axis=-1, keepdims=True)
    return yc * lax.rsqrt(var + EPS) * g + b


def _params(sem):
    return pltpu.CompilerParams(dimension_semantics=sem, vmem_limit_bytes=_VMEM_LIMIT)


_HBM = pl.BlockSpec(memory_space=pltpu.HBM)


def _mod_kernel(c_ref, w_ref, b_ref, o_ref):
    a = _silu(c_ref[...])
    w_hi, w_lo = _split(w_ref[...])
    o_ref[...] = _dot3(a, w_hi, w_lo) + b_ref[...]


def _modulation(cond, w_mod, b_mod):
    tn = 1536
    n6 = 6 * D_MODEL
    return pl.pallas_call(
        _mod_kernel,
        grid=(DEPTH, n6 // tn),
        in_specs=[
            pl.BlockSpec((MOD_ROWS, D_MODEL), lambda l, j: (0, 0)),
            pl.BlockSpec((None, D_MODEL, tn), lambda l, j: (l, 0, j)),
            pl.BlockSpec((None, 1, tn), lambda l, j: (l, 0, j)),
        ],
        out_specs=pl.BlockSpec((None, MOD_ROWS, tn), lambda l, j: (l, 0, j)),
        out_shape=jax.ShapeDtypeStruct((DEPTH, MOD_ROWS, n6), F32),
        compiler_params=_params(("parallel", "parallel")),
        name="modulation",
    )(cond, w_mod, b_mod.reshape(DEPTH, 1, n6))


def _mod_spec(layer, row_fn):
    return pl.BlockSpec((None, None, 1, 6 * D_MODEL), lambda s, i: (layer, row_fn(s), 0, 0))


def _augment_v(v):
    ones = jnp.ones((v.shape[0], DIFF_V), v.dtype)
    parts = []
    for hd in range(H_DIFF):
        parts += [v[:, hd * DIFF_V:(hd + 1) * DIFF_V], ones]
    return jnp.concatenate(parts, axis=1)


def _swap16(x):
    n = x.shape[-1]
    lane = _iota(x.shape, 1)
    return jnp.where((lane & 16) == 0, pltpu.roll(x, n - 16, axis=1), pltpu.roll(x, 16, axis=1))


def _proj_kernel(*refs, latent, n_aliased, layer):
    x_ref, mod_ref, w_ref, lg_ref = refs[:4]
    pos = 4
    if latent:
        cos_ref, sin_ref = refs[pos:pos + 2]
        pos += 2
    pos += n_aliased
    rq_ref, rk_ref, rv_ref, rg_ref, dq_ref, dk_ref, dv_ref, cb_ref, cch_ref = refs[pos:pos + 9]
    pos += 9
    if latent:
        kvf_ref, kvb_ref = refs[pos:pos + 2]
    else:
        dk32_ref, dv32_ref, stf_ref, stb_ref = refs[pos:pos + 4]
    rows = PROJ_TILES * TILE

    def put(ref, u, idx, val):
        if n_aliased:
            ref[(u,) + idx] = val
        else:
            for other in range(DEPTH):
                ref[(u, other) + idx] = val if other == layer else jnp.zeros_like(val)

    def store(ref, val):
        ref[...] = val.reshape(PROJ_TILES, TILE, val.shape[-1])

    m = mod_ref[...]
    sh1 = m[:, 0:D_MODEL]
    sc1 = m[:, D_MODEL:2 * D_MODEL]
    h = (x_ref[...].reshape(rows, D_MODEL) * (1.0 + sc1) + sh1).astype(BF)

    def col(off, width):
        return _dot(h, w_ref[:, off:off + width])

    store(rq_ref, col(_O_RQ, RET_W).astype(BF))
    rk = col(_O_RK, RET_W) * (RET_DK ** -0.5)
    store(rk_ref, rk.astype(BF))
    rv = col(_O_RV, RET_W).astype(BF)
    store(rv_ref, rv)
    store(rg_ref, col(_O_RG, RET_W))

    lg = _log_sigmoid(lg_ref[...])
    p = (_iota((rows, 1), 0) & (TILE - 1)).astype(F32)
    kf = (rk * jnp.exp((TILE - 1.0 - p) * lg[0:1])).astype(BF)
    kb = (rk * jnp.exp(p * lg[1:2])).astype(BF)
    for u in range(PROJ_TILES):
        tile = slice(u * TILE, (u + 1) * TILE)
        kvf = _dot_tn(kf[tile], rv[tile])
        kvb = _dot_tn(kb[tile], rv[tile])
        if latent:
            kvf_ref[u] = kvf
            kvb_ref[u] = kvb
        else:
            for hd in range(H_RET):
                lo = hd * RET_DK
                put(stf_ref, u, (hd,), kvf[lo:lo + RET_DK, lo:lo + RET_DK])
                put(stb_ref, u, (hd,), kvb[lo:lo + RET_DK, lo:lo + RET_DK])

    dq = col(_O_DQ, DIFF_W)
    dk = col(_O_DK, DIFF_W)
    dv = col(_O_DV, DIFF_W)
    if latent:
        cos = cos_ref[...].reshape(rows, DIFF_W)
        sin = sin_ref[...].reshape(rows, DIFF_W)
        dq = dq * cos + _swap16(dq) * sin
        dk = dk * cos + _swap16(dk) * sin
    else:
        for u in range(PROJ_TILES):
            put(dk32_ref, u, (), dk[u * TILE:(u + 1) * TILE])
            put(dv32_ref, u, (), dv[u * TILE:(u + 1) * TILE])
    store(dq_ref, (dq * QK_SCALE_LOG2).astype(BF))
    store(dk_ref, dk.astype(BF))
    store(dv_ref, _augment_v(dv.astype(BF)))

    store(cb_ref, col(_O_CB, CONV_CH))
    store(cch_ref, col(_O_CC, CONV_CH) * col(_O_CH, CONV_CH))


def _proj(x, mod, layer, row_fn, w_in_b, lg_lanes, rope_tabs=None, stacked=None):
    S, T, _ = x.shape
    n = T // TILE
    G = PROJ_TILES
    tiles = S * n
    latent = rope_tabs is not None
    assert tiles % G == 0 and (n % G == 0 or n == 1)

    def tok(w):
        return pl.BlockSpec((G, TILE, w), lambda t: (t, 0, 0))

    def tok_shape(w, dt):
        return jax.ShapeDtypeStruct((tiles, TILE, w), dt)

    in_specs = [
        tok(D_MODEL),
        pl.BlockSpec((None, None, 1, 6 * D_MODEL), lambda t: (layer, row_fn(t * G // n), 0, 0)),
        pl.BlockSpec((None, D_MODEL, IN_WIDTH), lambda t: (layer, 0, 0)),
        pl.BlockSpec((2, RET_W), lambda t: (0, 0)),
    ]
    args = [x.reshape(tiles, TILE, D_MODEL), mod, w_in_b, lg_lanes]
    out_specs = [tok(RET_W)] * 4 + [tok(DIFF_W)] * 2 + [tok(V_AUG_W)] + [tok(CONV_CH)] * 2
    out_shape = ([tok_shape(RET_W, BF)] * 3 + [tok_shape(RET_W, F32)] + [tok_shape(DIFF_W, BF)] * 2
                 + [tok_shape(V_AUG_W, BF)]
                 + [tok_shape(CONV_CH, F32)] * 2)
    aliases = {}
    if latent:
        in_specs += [pl.BlockSpec((G, TILE, DIFF_W), lambda t: (t % (n // G), 0, 0))] * 2
        args += [tab.reshape(n, TILE, DIFF_W) for tab in rope_tabs]
        out_specs += [pl.BlockSpec((G, RET_W, RET_W), lambda t: (t, 0, 0))] * 2
        out_shape += [jax.ShapeDtypeStruct((tiles, RET_W, RET_W), F32)] * 2
    else:
        assert n == 1
        kv_shape = jax.ShapeDtypeStruct((S, DEPTH, T, DIFF_W), F32)
        st_shape = jax.ShapeDtypeStruct((S, DEPTH, H_RET, RET_DK, RET_DK), F32)
        if stacked is not None:
            in_specs += [_HBM] * 4
            args += list(stacked)
            aliases = {len(args) - 4 + j: len(out_shape) + j for j in range(4)}
            out_specs += [pl.BlockSpec((G, None, TILE, DIFF_W), lambda t: (t, layer, 0, 0))] * 2
            out_specs += [pl.BlockSpec((G, None, H_RET, RET_DK, RET_DK), lambda t: (t, layer, 0, 0, 0))] * 2
        else:
            out_specs += [pl.BlockSpec((G, DEPTH, TILE, DIFF_W), lambda t: (t, 0, 0, 0))] * 2
            out_specs += [pl.BlockSpec((G, DEPTH, H_RET, RET_DK, RET_DK), lambda t: (t, 0, 0, 0, 0))] * 2
        out_shape += [kv_shape, kv_shape, st_shape, st_shape]
    outs = pl.pallas_call(
        functools.partial(_proj_kernel, latent=latent, n_aliased=len(aliases), layer=layer),
        grid=(tiles // G,),
        in_specs=in_specs,
        out_specs=out_specs,
        out_shape=out_shape,
        input_output_aliases=aliases,
        compiler_params=_params(("parallel",)),
        name="proj_lat" if latent else "proj_ctx",
    )(*args)
    token_outs = [o.reshape(S, T, o.shape[-1]) for o in outs[:9]]
    if latent:
        return token_outs + [o.reshape(S, n, RET_W, RET_W) for o in outs[9:]]
    return token_outs + list(outs[9:])


def _scan_kernel(kvf_ref, kvb_ref, s0f_ref, s0b_ref, lg_ref, sf_ref, sb_ref, *, n):
    lg = _log_sigmoid(lg_ref[...])
    dec = jnp.exp(float(TILE) * lg)
    same_head = (_iota((RET_W, RET_W), 0) >> 6) == (_iota((RET_W, RET_W), 1) >> 6)

    s = jnp.where(same_head, s0f_ref[...], 0.0)
    for c in range(n):
        sf_ref[c] = s
        s = s * dec[0:1] + jnp.where(same_head, kvf_ref[c], 0.0)
    s = jnp.where(same_head, s0b_ref[...], 0.0)
    for c in reversed(range(n)):
        sb_ref[c] = s
        s = s * dec[1:2] + jnp.where(same_head, kvb_ref[c], 0.0)


def _scan(kvf, kvb, s0f, s0b, lg_lanes):
    S, n = kvf.shape[:2]
    chunks = pl.BlockSpec((None, n, RET_W, RET_W), lambda s: (s, 0, 0, 0))
    one = pl.BlockSpec((None, RET_W, RET_W), lambda s: (s, 0, 0))
    return pl.pallas_call(
        functools.partial(_scan_kernel, n=n),
        grid=(S,),
        in_specs=[chunks, chunks, one, one, pl.BlockSpec((2, RET_W), lambda s: (0, 0))],
        out_specs=[chunks, chunks],
        out_shape=[jax.ShapeDtypeStruct((S, n, RET_W, RET_W), F32)] * 2,
        compiler_params=_params(("parallel",)),
        name="ret_scan",
    )(kvf, kvb, s0f, s0b, lg_lanes)


def _diff_heads(dq_ref, kv_refs, heads, lam):
    lane = _iota((1, DIFF_V), 1)
    scores = []
    for hd in heads:
        lo = hd * DIFF_V
        q_h = dq_ref[:, lo:lo + DIFF_V]
        qs = jnp.concatenate([jnp.where(lane < DIFF_QK, q_h, 0), jnp.where(lane >= DIFF_QK, q_h, 0)], axis=0)
        scores.append(jnp.concatenate([_dot_nt(qs, k[:, lo:lo + DIFF_V]) for k, _ in kv_refs], axis=1))
    s = jnp.concatenate(scores, axis=0)
    p = jnp.exp2(s - jnp.max(s, axis=-1, keepdims=True))

    def times_v(w, hd, width):
        acc = None
        start = 0
        for k, v in kv_refs:
            part = _dot(w[:, start:start + k.shape[0]], v[:, hd * 2 * DIFF_V:hd * 2 * DIFF_V + width])
            acc = part if acc is None else acc + part
            start += k.shape[0]
        return acc

    outs = []
    p = p.astype(BF)
    for n, hd in enumerate(heads):
        acc = times_v(p[n * 2 * TILE:(n + 1) * 2 * TILE], hd, 2 * DIFF_V)
        o0 = acc[0:TILE, 0:DIFF_V] * (1.0 / acc[0:TILE, DIFF_V:DIFF_V + 1])
        o1 = acc[TILE:, 0:DIFF_V] * (lam / acc[TILE:, DIFF_V:DIFF_V + 1])
        outs.append(o0 - o1)
    return outs


def _mixer_kernel(*refs, n_tiles, n_total, lam_init, cached):
    n_a = _n_token_mix_refs(cached)
    mix_even, mix_odd, decay_ref = refs[-3:]
    t = pl.program_id(0)
    tile_in_seq = lax.rem(jnp.minimum(t, n_total - 1), n_tiles)

    @pl.when(t == 0)
    def _():
        mix_odd[...] = jnp.zeros_like(mix_odd)
        lg = _log_sigmoid(refs[n_a - 5][...])
        dist = (_iota((TILE, TILE), 0) - _iota((TILE, TILE), 1)).astype(F32)
        diag2 = jnp.where(dist == 0.0, 2.0, 1.0)
        for hd in range(H_RET):
            lgf = lg[0:1, hd * RET_DK:hd * RET_DK + 1]
            lgb = lg[1:2, hd * RET_DK:hd * RET_DK + 1]
            decay_ref[hd] = jnp.exp(jnp.abs(dist) * jnp.where(dist > 0.0, lgf, lgb)) * diag2

    def step(mix_write, mix_read):
        mix_write[...] = _token_mix(*refs[:n_a], decay_ref, tile_in_seq=tile_in_seq, n_tiles=n_tiles,
                                    lam_init=lam_init, cached=cached)
        _project_route(mix_read[...], *refs[n_a:-3])

    @pl.when(lax.rem(t, 2) == 0)
    def _():
        step(mix_even, mix_odd)

    @pl.when(lax.rem(t, 2) == 1)
    def _():
        step(mix_odd, mix_even)


def _n_token_mix_refs(cached):
    return 18 + (2 if cached else 0)


def _token_mix(*refs, tile_in_seq, n_tiles, lam_init, cached):
    rq_ref, rk_ref, rv_ref, rg_ref, sf_ref, sb_ref, dq_ref = refs[:7]
    pos = 7
    kv_refs = []
    if cached:
        kv_refs.append((refs[pos], refs[pos + 1]))
        pos += 2
    kv_refs.append((refs[pos], refs[pos + 1]))
    pos += 2
    (cb_ref, cch_ref, cprev_ref, cnext_ref, lg_ref, lamp_ref, subg_ref, cw_ref, cbias_ref,
     decay_ref) = refs[pos:pos + 10]
    i = tile_in_seq

    lg = _log_sigmoid(lg_ref[...])
    head_of_lane = _iota((1, RET_W), 1) >> 6
    q = rq_ref[...]
    k = rk_ref[...]
    v = rv_ref[...]
    ret_o = jnp.zeros((TILE, RET_W), F32)
    for hd in range(H_RET):
        in_head = head_of_lane == hd
        sc = _dot_nt(jnp.where(in_head, q, 0), k)
        ret_o = ret_o + jnp.where(in_head, _dot((sc * decay_ref[hd]).astype(BF), v), 0.0)
    p = _iota((TILE, 1), 0).astype(F32)
    ret_o = ret_o + _dot(q, sf_ref[...].astype(BF)) * jnp.exp((p + 1.0) * lg[0:1])
    ret_o = ret_o + _dot(q, sb_ref[...].astype(BF)) * jnp.exp((float(TILE) - p) * lg[1:2])
    avg = jnp.where((_iota((RET_W, RET_W), 0) >> 6) == (_iota((RET_W, RET_W), 1) >> 6),
                    1.0 / RET_DK, 0.0).astype(BF)
    rc = ret_o - _dot_hl(ret_o, avg)
    ret = rc * lax.rsqrt(_dot_hl(rc * rc, avg) + EPS) * _silu(rg_ref[...])

    lp = lamp_ref[...]
    lam = (jnp.exp(jnp.sum(lp[0:1] * lp[1:2], axis=-1, keepdims=True))
           - jnp.exp(jnp.sum(lp[2:3] * lp[3:4], axis=-1, keepdims=True)) + lam_init)
    subg = subg_ref[...] * (1.0 - lam_init)
    head_groups = [[hd] for hd in range(H_DIFF)] if cached else [list(range(H_DIFF))]
    heads = []
    for group in head_groups:
        for o in _diff_heads(dq_ref, kv_refs, group, lam):
            o = o * lax.rsqrt(jnp.mean(o * o, axis=-1, keepdims=True) + EPS) * subg
            heads.append(o.astype(BF))
    diff = jnp.concatenate(heads, axis=1)

    cch = cch_ref[...]
    prev = jnp.where(i > 0, cprev_ref[7:8, :], 0.0)
    nxt = jnp.where(i < n_tiles - 1, cnext_ref[0:1, :], 0.0)
    r = _iota((TILE, 1), 0)
    up = jnp.where(r == 0, prev, pltpu.roll(cch, 1, axis=0))
    dn = jnp.where(r == TILE - 1, nxt, pltpu.roll(cch, TILE - 1, axis=0))
    cw = cw_ref[...]
    conv = cb_ref[...] * (up * cw[0:1] + cch * cw[1:2] + dn * cw[2:3] + cbias_ref[...])
    return jnp.concatenate([ret.astype(BF), diff, conv.astype(BF)], axis=1)


def _project_route(mixed, x_ref, mod_ref, wout_ref, lng_ref, lnb_ref, wrh_ref, wrl_ref, br_ref,
                   x1_ref, hs_ref, slots_ref, cnt_ref):
    m = mod_ref[...]
    g1 = m[:, 2 * D_MODEL:3 * D_MODEL]
    sh2 = m[:, 3 * D_MODEL:4 * D_MODEL]
    sc2 = m[:, 4 * D_MODEL:5 * D_MODEL]
    x1 = _layer_norm(ALPHA * x_ref[...] + g1 * _dot(mixed, wout_ref[...]), lng_ref[...], lnb_ref[...])
    x1_ref[...] = x1

    h2 = x1 * (1.0 + sc2) + sh2
    h2h, h2l = _split(h2)
    wrh = wrh_ref[...]
    logits = (_dot_nt(wrh, h2h) + (_dot_nt(wrh, h2l) + _dot_nt(wrl_ref[...], h2h))) + br_ref[...]
    neg = -jnp.inf
    g_id = _iota((EXPERT_ROW0, TILE), 0)
    gl = jnp.where(g_id < N_GROUPS, logits[0:EXPERT_ROW0], neg)
    gmax = jnp.max(gl, axis=0, keepdims=True)
    g_idx = jnp.min(jnp.where(gl == gmax, g_id, N_GROUPS), axis=0, keepdims=True)
    g_w = 1.0 / jnp.sum(jnp.exp(gl - gmax), axis=0, keepdims=True)
    e_id = _iota((N_EXPERTS, TILE), 0)
    el = jnp.where((e_id >> 2) == g_idx, logits[EXPERT_ROW0:EXPERT_ROW0 + N_EXPERTS], neg)
    v1 = jnp.max(el, axis=0, keepdims=True)
    i1 = jnp.min(jnp.where(el == v1, e_id, N_EXPERTS), axis=0, keepdims=True)
    el2 = jnp.where(e_id == i1, neg, el)
    v2 = jnp.max(el2, axis=0, keepdims=True)
    i2 = jnp.min(jnp.where(el2 == v2, e_id, N_EXPERTS), axis=0, keepdims=True)
    t = jnp.exp(v2 - v1)
    w1 = g_w / (1.0 + t)
    w2 = g_w * t / (1.0 + t)

    sel1 = e_id == i1
    sel2 = e_id == i2
    sel = jnp.where(sel1 | sel2, 1.0, 0.0)
    earlier = _ones_where(_iota((TILE, TILE), 0) < _iota((TILE, TILE), 1))
    rank = _dot(sel.astype(BF), earlier)
    cnt = jnp.sum(sel, axis=1, keepdims=True).astype(I32)
    padded = ((cnt + (CHUNK - 1)) >> 4) << 4
    incl = jnp.broadcast_to(padded, (N_EXPERTS, LANES))
    e_row = _iota((N_EXPERTS, LANES), 0)
    for step in (1, 2, 4, 8):
        incl = incl + jnp.where(e_row >= step, pltpu.roll(incl, step, axis=0), 0)
    seg_off = (incl[:, 0:1] - padded).astype(F32)
    spos = seg_off + rank
    slot1 = jnp.sum(jnp.where(sel1, spos, 0.0), axis=0, keepdims=True).astype(I32)
    slot2 = jnp.sum(jnp.where(sel2, spos, 0.0), axis=0, keepdims=True).astype(I32)
    s_id = _iota((CAP, TILE), 0)
    at1 = s_id == slot1
    at2 = s_id == slot2
    hs_ref[:, 0:D_MODEL] = _dot(_ones_where(at1 | at2), h2h).astype(BF)
    w1h = w1.astype(BF).astype(F32)
    w2h = w2.astype(BF).astype(F32)
    ones = jnp.ones((TILE, LANES), BF)
    g_hi = _dot((jnp.where(at1, w1h, 0.0) + jnp.where(at2, w2h, 0.0)).astype(BF), ones)
    g_lo = _dot((jnp.where(at1, w1 - w1h, 0.0) + jnp.where(at2, w2 - w2h, 0.0)).astype(BF), ones)
    lane = _iota((CAP, LANES), 1)
    hs_ref[:, D_MODEL:] = jnp.where(lane == 0, g_hi, jnp.where(lane == 1, g_lo, 0.0)).astype(BF)
    cnt_ref[...] = jnp.broadcast_to(padded, (N_EXPERTS, LANES))

    dr = _iota((LANES, TILE), 0)
    digits = jnp.where(dr == 0, slot1 & (SLOT_RADIX - 1), jnp.where(dr == 1, slot1 >> 5,
             jnp.where(dr == 2, slot2 & (SLOT_RADIX - 1), jnp.where(dr == 3, slot2 >> 5, 0))))
    eye = _ones_where(_iota((TILE, TILE), 0) == _iota((TILE, TILE), 1))
    cols = _dot_nt(eye, digits.astype(F32).astype(BF))
    s1c = (cols[:, 0:1] + SLOT_RADIX * cols[:, 1:2]).astype(I32)
    s2c = (cols[:, 2:3] + SLOT_RADIX * cols[:, 3:4]).astype(I32)
    tl = _iota((TILE, LANES), 1)
    slots_ref[...] = jnp.where(tl == 0, s1c, jnp.where(tl == 1, s2c, 0))


def _mixer(x, mod, layer, row_fn, pr, states, cache_kv, new_kv, w_out_b, lg_lanes, lamp, subg, cw, cbias,
           lng, lnb, wrh, wrl, br, lam_init):
    S, T, _ = x.shape
    n = T // TILE
    rq, rk, rv, rg, dq, cb, cch = pr
    t8 = TILE // 8
    cached = cache_kv is not None
    last = S * n - 1

    def at_mix(t):
        return jnp.minimum(t, last) // n, jnp.minimum(t, last) % n

    def at_out(t):
        return jnp.maximum(t - 1, 0) // n, jnp.maximum(t - 1, 0) % n

    def tok(w, at):
        return pl.BlockSpec((None, TILE, w), lambda t: (*at(t), 0))

    def full(shape):
        return pl.BlockSpec(shape, lambda t: (0,) * len(shape))

    def seq(a):
        mode = pl.Buffered(1) if n > 1 else None
        return pl.BlockSpec((None,) + a.shape[1:], lambda t: (at_mix(t)[0], 0, 0), pipeline_mode=mode)

    def halo(offset):
        def index(t):
            s, i = at_mix(t)
            return (s, jnp.clip(i * t8 + offset, 0, T // 8 - 1), 0)
        return pl.BlockSpec((None, 8, CONV_CH), index)

    if states is None:
        zero_state = jnp.zeros((1, 1, RET_W, RET_W), F32)
        states = (zero_state, zero_state)
        state = pl.BlockSpec((None, None, RET_W, RET_W), lambda t: (0, 0, 0, 0))
    else:
        state = pl.BlockSpec((None, None, RET_W, RET_W), lambda t: (*at_mix(t), 0, 0))

    in_specs = [tok(RET_W, at_mix)] * 4 + [state, state, tok(DIFF_W, at_mix)]
    args = [rq, rk, rv, rg, states[0], states[1], dq]
    for kv in ([cache_kv] if cached else []) + [new_kv]:
        in_specs += [seq(kv[0]), seq(kv[1])]
        args += list(kv)
    in_specs += [tok(CONV_CH, at_mix), tok(CONV_CH, at_mix), halo(-1), halo(t8),
                 full((2, RET_W)), full((4, DIFF_QK)), full((1, DIFF_V)), full((3, CONV_CH)), full((1, CONV_CH))]
    args += [cb, cch, cch, cch, lg_lanes, lamp, subg, cw, cbias]
    assert len(args) == _n_token_mix_refs(cached)
    in_specs += [tok(D_MODEL, at_out),
                 pl.BlockSpec((None, None, 1, 6 * D_MODEL), lambda t: (layer, row_fn(at_out(t)[0]), 0, 0)),
                 pl.BlockSpec((None, D_MODEL, D_MODEL), lambda t: (layer, 0, 0)),
                 full((1, D_MODEL)), full((1, D_MODEL)),
                 full((ROUTER_ROWS, D_MODEL)), full((ROUTER_ROWS, D_MODEL)), full((ROUTER_ROWS, 1))]
    args += [x, mod, w_out_b, lng, lnb, wrh, wrl, br]

    def out_tile(t):
        s, i = at_out(t)
        return s * n + i

    return pl.pallas_call(
        functools.partial(_mixer_kernel, n_tiles=n, n_total=S * n, lam_init=lam_init, cached=cached),
        grid=(S * n + 1,),
        in_specs=in_specs,
        out_specs=[tok(D_MODEL, at_out),
                   pl.BlockSpec((CAP, HS_W), lambda t: (out_tile(t), 0)),
                   tok(LANES, at_out),
                   pl.BlockSpec((None, None, N_EXPERTS, LANES), lambda t: (*at_out(t), 0, 0))],
        out_shape=[jax.ShapeDtypeStruct((S, T, D_MODEL), F32),
                   jax.ShapeDtypeStruct((S * n * CAP, HS_W), BF),
                   jax.ShapeDtypeStruct((S, T, LANES), I32),
                   jax.ShapeDtypeStruct((S, n, N_EXPERTS, LANES), I32)],
        scratch_shapes=[pltpu.VMEM((TILE, D_MODEL), BF), pltpu.VMEM((TILE, D_MODEL), BF),
                        pltpu.VMEM((H_RET, TILE, TILE), F32)],
        compiler_params=_params(("arbitrary",)),
        name="mixer_lat" if cached else "mixer_ctx",
    )(*args)


def _block_schedule(padded, chunks_a):
    NT = padded.shape[0]
    n_chunks = NT * CHUNKS_PER_TILE
    n_blocks = n_chunks // CPB + N_EXPERTS
    c16 = padded // CHUNK
    ends = jnp.cumsum(c16, axis=1)
    before = jnp.cumsum(c16, axis=0) - c16
    per_expert = jnp.sum(c16, axis=0)
    nb = (per_expert + (CPB - 1)) // CPB
    b_end = jnp.cumsum(nb)
    q = jnp.arange(CHUNKS_PER_TILE, dtype=I32)
    e = jnp.arange(N_EXPERTS, dtype=I32)
    key = jnp.sum((ends[:, None, :] <= q[None, :, None]).astype(I32), axis=-1)
    pos_e = ((b_end - nb) * CPB + before)[:, None, :] + (q[None, :, None] - (ends - c16)[:, None, :])
    pos = jnp.sum(jnp.where(key[..., None] == e, pos_e, 0), axis=-1)
    pos = jnp.where(key < N_EXPERTS, pos, -1).reshape(-1)
    match = pos[None, :] == jnp.arange(n_blocks * CPB, dtype=I32)[:, None]
    cid = jnp.sum(jnp.where(match, jnp.arange(n_chunks, dtype=I32)[None, :], 0), axis=-1)
    nv = jnp.sum(match.astype(I32).reshape(n_blocks, -1), axis=-1)
    first = match & (jnp.arange(n_chunks, dtype=I32) < chunks_a)[None, :]
    nv_a = jnp.sum(first.astype(I32).reshape(n_blocks, -1), axis=-1)
    b = jnp.arange(n_blocks, dtype=I32)
    eb = jnp.minimum(jnp.sum((b_end[None, :] <= b[:, None]).astype(I32), axis=-1), N_EXPERTS - 1)
    zero = jnp.zeros((1,), I32)
    counts = jnp.concatenate([nv, zero, nv_a, zero]).astype(I32)
    return eb.astype(I32), counts, jnp.concatenate([cid, jnp.zeros((CPB,), I32)]).astype(I32)


def _moe_kernel(eb_ref, nv_ref, cid_ref, hs_a, hs_b, wg_ref, wu_ref, wd_ref, ys_a, ys_b,
                xbuf, ybuf, wg_b, wu_b, wd_b, in_sem, out_sem, *, n_blocks, chunks_a):
    b = pl.program_id(0)

    def chunk_rows(c):
        return pl.ds(pl.multiple_of(c * CHUNK, CHUNK), CHUNK)

    def gather_copy(which, c, j, slot):
        src = (hs_a, hs_b)[which]
        return pltpu.make_async_copy(src.at[chunk_rows(c)], xbuf.at[slot, chunk_rows(j)], in_sem.at[slot])

    def scatter_copy(which, c, j, slot):
        dst = (ys_a, ys_b)[which]
        return pltpu.make_async_copy(ybuf.at[slot, chunk_rows(j)],
                                     dst.at[chunk_rows(c), pl.ds(0, D_MODEL)], out_sem.at[slot])

    def for_chunks(blk, fn):
        n_a = nv_ref[n_blocks + 1 + blk]

        def body_a(j, carry):
            fn(j, 0, cid_ref[blk * CPB + j])
            return carry

        def body_b(j, carry):
            fn(j, 1, cid_ref[blk * CPB + j] - chunks_a)
            return carry
        lax.fori_loop(0, n_a, body_a, 0)
        lax.fori_loop(n_a, nv_ref[blk], body_b, 0)

    def each_chunk(blk, fn):
        n_a = nv_ref[n_blocks + 1 + blk]
        n_v = nv_ref[blk]
        for j in range(CPB):
            c = cid_ref[blk * CPB + j]

            @pl.when(j < n_a)
            def _():
                fn(j, 0, c)

            @pl.when((j >= n_a) & (j < n_v))
            def _():
                fn(j, 1, c - chunks_a)

    def wait_gathers(blk, slot):
        rows = pl.ds(0, nv_ref[blk] * CHUNK)

        @pl.when(nv_ref[blk] > 0)
        def _():
            pltpu.make_async_copy(hs_a.at[rows], xbuf.at[slot, rows], in_sem.at[slot]).wait()

    def wait_scatters(blk, slot):
        rows = pl.ds(0, nv_ref[blk] * CHUNK)

        @pl.when(nv_ref[blk] > 0)
        def _():
            pltpu.make_async_copy(ybuf.at[slot, rows], ys_a.at[rows, pl.ds(0, D_MODEL)], out_sem.at[slot]).wait()

    slot = lax.rem(b, 2)
    other = 1 - slot

    @pl.when(b == 0)
    def _():
        xbuf[...] = jnp.zeros_like(xbuf)
        for_chunks(0, lambda j, w, c: gather_copy(w, c, j, 0).start())

    @pl.when((b == 0) | (eb_ref[b] != eb_ref[jnp.maximum(b - 1, 0)]))
    def _():
        wg_b[...] = wg_ref[...].astype(BF)
        wu_b[...] = wu_ref[...].astype(BF)
        wd_b[...] = wd_ref[...].astype(BF)

    wait_gathers(b, slot)

    @pl.when(b >= 2)
    def _():
        wait_scatters(b - 2, slot)

    @pl.when(nv_ref[b] > 0)
    def _():
        each_chunk(b + 1, lambda j, w, c: gather_copy(w, c, j, other).start())
        x = xbuf[slot, :, 0:D_MODEL]
        gate = (xbuf[slot, :, D_MODEL:D_MODEL + 1].astype(F32)
                + xbuf[slot, :, D_MODEL + 1:D_MODEL + 2].astype(F32))
        hg = _dot(x, wg_b[...])
        hu = _dot(x, wu_b[...])
        act = _silu(hg) * hu * gate
        ybuf[slot] = _dot(act.astype(BF), wd_b[...]).astype(BF)
        each_chunk(b, lambda j, w, c: scatter_copy(w, c, j, slot).start())

    @pl.when(b == n_blocks - 1)
    def _():
        if n_blocks >= 2:
            wait_scatters(b - 1, other)
        wait_scatters(b, slot)


def _moe(hs_a, hs_b, eb, nv, cid, layer, w_gate, w_up, w_down):
    n_blocks = eb.shape[0]

    def w_spec(a, c):
        return pl.BlockSpec((None, None, a, c), lambda b, eb, nv, cid: (layer, eb[b], 0, 0))

    grid_spec = pltpu.PrefetchScalarGridSpec(
        num_scalar_prefetch=3,
        grid=(n_blocks,),
        in_specs=[_HBM, _HBM, w_spec(D_MODEL, D_EXPERT), w_spec(D_MODEL, D_EXPERT), w_spec(D_EXPERT, D_MODEL)],
        out_specs=[_HBM, _HBM],
        scratch_shapes=[pltpu.VMEM((2, MOE_TMB, HS_W), BF), pltpu.VMEM((2, MOE_TMB, D_MODEL), BF),
                        pltpu.VMEM((D_MODEL, D_EXPERT), BF), pltpu.VMEM((D_MODEL, D_EXPERT), BF),
                        pltpu.VMEM((D_EXPERT, D_MODEL), BF),
                        pltpu.SemaphoreType.DMA((2,)), pltpu.SemaphoreType.DMA((2,))],
    )
    return pl.pallas_call(
        functools.partial(_moe_kernel, n_blocks=n_blocks, chunks_a=hs_a.shape[0] // CHUNK),
        grid_spec=grid_spec,
        out_shape=[jax.ShapeDtypeStruct(hs_a.shape, BF), jax.ShapeDtypeStruct(hs_b.shape, BF)],
        input_output_aliases={3: 0, 4: 1},
        compiler_params=_params(("arbitrary",)),
        name="moe_sorted",
    )(eb, nv, cid, hs_a, hs_b, w_gate, w_up, w_down)


def _combine_kernel(ys_ref, slots_ref, x1_ref, mod_ref, lng_ref, lnb_ref, o_ref):
    g2 = mod_ref[:, 5 * D_MODEL:6 * D_MODEL]
    s_lane = _iota((TILE, CAP), 1)
    for u in range(COMBINE_TILES_PER_STEP):
        sl = slots_ref[u]
        pick = _ones_where((s_lane == sl[:, 0:1]) | (s_lane == sl[:, 1:2]))
        ffn = _dot(pick, ys_ref[u * CAP:(u + 1) * CAP, 0:D_MODEL])
        o_ref[u] = _layer_norm(ALPHA * x1_ref[u] + g2 * ffn, lng_ref[...], lnb_ref[...])


def _combine(ys, slots, x1, mod, layer, row_fn, lng, lnb):
    S, T, _ = x1.shape
    n = T // TILE
    U = COMBINE_TILES_PER_STEP
    assert (S * n) % U == 0 and (n % U == 0 or n == 1)

    def tok(w):
        return pl.BlockSpec((U, TILE, w), lambda t: (t, 0, 0))

    out = pl.pallas_call(
        _combine_kernel,
        grid=(S * n // U,),
        in_specs=[
            pl.BlockSpec((U * CAP, HS_W), lambda t: (t, 0)),
            tok(LANES), tok(D_MODEL),
            pl.BlockSpec((None, None, 1, 6 * D_MODEL), lambda t: (layer, row_fn(t * U // n), 0, 0)),
            pl.BlockSpec((1, D_MODEL), lambda t: (0, 0)),
            pl.BlockSpec((1, D_MODEL), lambda t: (0, 0)),
        ],
        out_specs=tok(D_MODEL),
        out_shape=jax.ShapeDtypeStruct((S * n, TILE, D_MODEL), F32),
        compiler_params=_params(("parallel",)),
        name="moe_combine",
    )(ys, slots.reshape(S * n, TILE, LANES), x1.reshape(S * n, TILE, D_MODEL), mod, lng, lnb)
    return out.reshape(S, T, D_MODEL)


def _rope_tables(n_lat):
    half = DIFF_QK // 2
    pairs = half // 2
    inv = (1.0 / (ROPE_BASE ** (np.arange(pairs, dtype=np.float32) * 2.0 / half))).astype(np.float32)
    t = np.arange(n_lat)
    ang_r = ((t // GRID_W).astype(np.float32)[:, None] * inv[None, :]).astype(np.float64)
    ang_c = ((t % GRID_W).astype(np.float32)[:, None] * inv[None, :]).astype(np.float64)
    cos = np.concatenate([np.cos(ang_r)] * 2 + [np.cos(ang_c)] * 2, axis=1)
    sin = np.concatenate([-np.sin(ang_r), np.sin(ang_r), -np.sin(ang_c), np.sin(ang_c)], axis=1)
    reps = DIFF_W // DIFF_QK
    return (jnp.asarray(np.tile(cos, (1, reps)), F32), jnp.asarray(np.tile(sin, (1, reps)), F32))


def _block_diag(s):
    S = s.shape[0]
    eye = jnp.eye(H_RET, dtype=s.dtype)
    return jnp.einsum('shdv,hg->shdgv', s, eye).reshape(S, RET_W, RET_W)


def _router_rows(w_group, b_group, w_expert, b_expert):
    pad = EXPERT_ROW0 - N_GROUPS
    tail = ROUTER_ROWS - EXPERT_ROW0 - N_EXPERTS
    w = jnp.concatenate([w_group.T, jnp.zeros((pad, D_MODEL), F32),
                         w_expert.reshape(D_MODEL, N_EXPERTS).T, jnp.zeros((tail, D_MODEL), F32)], axis=0)
    bias = jnp.concatenate([b_group, jnp.zeros((pad,), F32), b_expert.reshape(N_EXPERTS), jnp.zeros((tail,), F32)])
    hi = w.astype(BF)
    return hi, (w - hi.astype(F32)).astype(BF), bias.reshape(ROUTER_ROWS, 1)


def kernel(x_prompt, x_sample, cache_diff_k, cache_diff_v, state_ret_fwd, state_ret_bwd, c, c_ctx, w_mod, b_mod, w_in, ret_decay_logit, diff_lambda, diff_subln_g, conv_w, conv_b, w_out, ln_g, ln_b, w_router_group, b_router_group, w_router_expert, b_router_expert, w_gate, w_up, w_down):
    B, T_ctx, _ = x_prompt.shape
    Bd, T_lat, _ = x_sample.shape
    assert T_ctx == TILE and T_lat % TILE == 0 and T_lat % GRID_W == 0
    assert 1 + Bd <= MOD_ROWS

    cond = jnp.concatenate([c_ctx[None, :], c, jnp.zeros((MOD_ROWS - 1 - Bd, D_MODEL), F32)], axis=0)
    mod = _modulation(cond, w_mod, b_mod).reshape(DEPTH, MOD_ROWS, 1, 6 * D_MODEL)
    rope_tabs = _rope_tables(T_lat)
    w_in_b = w_in.astype(BF)
    w_out_b = w_out.astype(BF)

    ctx_row = lambda s: 0
    lat_row = lambda s: s + 1
    ctx_tiles = B * (T_ctx // TILE)
    lat_tiles = Bd * (T_lat // TILE)

    yp, ys = x_prompt, x_sample
    stacked = None
    for l in range(DEPTH):
        lam_init = 0.8 - 0.6 * math.exp(-0.3 * l)
        lg_lanes = jnp.repeat(ret_decay_logit[l], RET_DK, axis=1)
        wrh, wrl, br = _router_rows(w_router_group[l], b_router_group[l], w_router_expert[l], b_router_expert[l])
        shared = (w_out_b, lg_lanes, diff_lambda[l], diff_subln_g[l].reshape(1, DIFF_V), conv_w[l],
                  conv_b[l].reshape(1, CONV_CH), ln_g[l, 0].reshape(1, D_MODEL), ln_b[l, 0].reshape(1, D_MODEL),
                  wrh, wrl, br, lam_init)
        ln2 = (ln_g[l, 1].reshape(1, D_MODEL), ln_b[l, 1].reshape(1, D_MODEL))

        pr = _proj(yp, mod, l, ctx_row, w_in_b, lg_lanes, stacked=stacked)
        rq, rk, rv, rg, dq, dk, dv, cb, cch = pr[:9]
        stacked = pr[9:]
        x1_c, hs_c, slots_c, cnt_c = _mixer(yp, mod, l, ctx_row, (rq, rk, rv, rg, dq, cb, cch), None, None,
                                            (dk, dv), *shared)

        pr = _proj(ys, mod, l, lat_row, w_in_b, lg_lanes, rope_tabs=rope_tabs)
        rq, rk, rv, rg, dq, dk, dv, cb, cch, kvf, kvb = pr
        states = _scan(kvf, kvb, _block_diag(state_ret_fwd[:, l]), _block_diag(state_ret_bwd[:, l]), lg_lanes)
        cache_v = cache_diff_v[:, l].astype(BF)
        cache_kv = (cache_diff_k[:, l].reshape(Bd, -1, DIFF_W).astype(BF),
                    jnp.concatenate([cache_v, jnp.ones_like(cache_v)], axis=-1).reshape(Bd, -1, V_AUG_W))
        x1_l, hs_l, slots_l, cnt_l = _mixer(ys, mod, l, lat_row, (rq, rk, rv, rg, dq, cb, cch), states, cache_kv,
                                            (dk, dv), *shared)

        padded = jnp.concatenate([cnt_c.reshape(ctx_tiles, N_EXPERTS, LANES)[:, :, 0],
                                  cnt_l.reshape(lat_tiles, N_EXPERTS, LANES)[:, :, 0]], axis=0)
        eb, nv, cid = _block_schedule(padded, ctx_tiles * CHUNKS_PER_TILE)
        out_c, out_l = _moe(hs_c, hs_l, eb, nv, cid, l, w_gate, w_up, w_down)
        yp = _combine(out_c, slots_c, x1_c, mod, l, ctx_row, *ln2)
        ys = _combine(out_l, slots_l, x1_l, mod, l, lat_row, *ln2)

    new_k, new_v, st_f, st_b = stacked
    return (yp, ys, new_k.reshape(B, DEPTH, T_ctx, H_DIFF, 2, DIFF_QK),
            new_v.reshape(B, DEPTH, T_ctx, H_DIFF, DIFF_V), st_f, st_b)
```

```python
import functools
import math

import numpy as np
import jax
import jax.numpy as jnp
from jax import lax
from jax.experimental import pallas as pl
from jax.experimental.pallas import tpu as pltpu

D_MODEL = 1024
DEPTH = 2
GRID_W = 64
H_RET = 4
RET_DK = 64
RET_W = H_RET * RET_DK
H_DIFF = 4
DIFF_QK = 64
DIFF_V = 2 * DIFF_QK
DIFF_W = H_DIFF * DIFF_V
CONV_CH = 256
ROPE_BASE = 10000.0
N_GROUPS = 4
EXPERTS_PER_GROUP = 4
N_EXPERTS = N_GROUPS * EXPERTS_PER_GROUP
D_EXPERT = 512
ALPHA = (2 * DEPTH) ** 0.25
EPS = 1e-5
MOD_ROWS = 8
LANES = 128
ROUTER_ROWS = 32
EXPERT_ROW0 = 8

TILE = 256
CHUNK = 16
CAP = 2 * TILE + N_EXPERTS * CHUNK
CHUNKS_PER_TILE = CAP // CHUNK
COMBINE_TILES_PER_STEP = 2
PROJ_TILES = 2
MOE_TMB = 512
CPB = MOE_TMB // CHUNK
HS_W = D_MODEL + LANES
V_AUG_W = 2 * DIFF_W
QK_SCALE_LOG2 = (DIFF_QK ** -0.5) * math.log2(math.e)
SLOT_RADIX = 32

_O_RQ, _O_RK, _O_RV, _O_RG = 0, 256, 512, 768
_O_DQ, _O_DK, _O_DV = 1024, 1536, 2048
_O_CB, _O_CC, _O_CH = 2560, 2816, 3072
IN_WIDTH = 3328

BF = jnp.bfloat16
F32 = jnp.float32
I32 = jnp.int32

_VMEM_LIMIT = 60 * 1024 * 1024


def _dot(a, b):
    return jnp.dot(a, b, preferred_element_type=F32)


def _dot_nt(a, b):
    return lax.dot_general(a, b, (((1,), (1,)), ((), ())), preferred_element_type=F32)


def _dot_tn(a, b):
    return lax.dot_general(a, b, (((0,), (0,)), ((), ())), preferred_element_type=F32)


def _split(x):
    hi = x.astype(BF)
    lo = (x - hi.astype(F32)).astype(BF)
    return hi, lo


def _dot_hl(x, w_bf16):
    hi, lo = _split(x)
    return _dot(hi, w_bf16) + _dot(lo, w_bf16)


def _dot3(x, w_hi, w_lo):
    hi, lo = _split(x)
    return _dot(hi, w_hi) + (_dot(lo, w_hi) + _dot(hi, w_lo))


def _iota(shape, dim):
    return lax.broadcasted_iota(I32, shape, dim)


def _ones_where(cond):
    return jnp.where(cond, 1.0, 0.0).astype(BF)


def _log_sigmoid(x):
    return jnp.minimum(x, 0.0) - jnp.log1p(jnp.exp(-jnp.abs(x)))


def _silu(x):
    return x * jax.nn.sigmoid(x)


def _layer_norm(y, g, b):
    mu = jnp.mean(y, axis=-1, keepdims=True)
    yc = y - mu
    var = jnp.mean(yc * yc, axis=-1, keepdims=True)
    return yc * lax.rsqrt(var + EPS) * g + b


def _params(sem):
    return pltpu.CompilerParams(dimension_semantics=sem, vmem_limit_bytes=_VMEM_LIMIT)


_HBM = pl.BlockSpec(memory_space=pltpu.HBM)


def _mod_kernel(c_ref, w_ref, b_ref, o_ref):
    a = _silu(c_ref[...])
    w_hi, w_lo = _split(w_ref[...])
    o_ref[...] = _dot3(a, w_hi, w_lo) + b_ref[...]


def _modulation(cond, w_mod, b_mod):
    tn = 1536
    n6 = 6 * D_MODEL
    return pl.pallas_call(
        _mod_kernel,
        grid=(DEPTH, n6 // tn),
        in_specs=[
            pl.BlockSpec((MOD_ROWS, D_MODEL), lambda l, j: (0, 0)),
            pl.BlockSpec((None, D_MODEL, tn), lambda l, j: (l, 0, j)),
            pl.BlockSpec((None, 1, tn), lambda l, j: (l, 0, j)),
        ],
        out_specs=pl.BlockSpec((None, MOD_ROWS, tn), lambda l, j: (l, 0, j)),
        out_shape=jax.ShapeDtypeStruct((DEPTH, MOD_ROWS, n6), F32),
        compiler_params=_params(("parallel", "parallel")),
        name="modulation",
    )(cond, w_mod, b_mod.reshape(DEPTH, 1, n6))


def _mod_spec(layer, row_fn):
    return pl.BlockSpec((None, None, 1, 6 * D_MODEL), lambda s, i: (layer, row_fn(s), 0, 0))


def _augment_v(v):
    ones = jnp.ones((v.shape[0], DIFF_V), v.dtype)
    parts = []
    for hd in range(H_DIFF):
        parts += [v[:, hd * DIFF_V:(hd + 1) * DIFF_V], ones]
    return jnp.concatenate(parts, axis=1)


def _swap16(x):
    n = x.shape[-1]
    lane = _iota(x.shape, 1)
    return jnp.where((lane & 16) == 0, pltpu.roll(x, n - 16, axis=1), pltpu.roll(x, 16, axis=1))


def _proj_kernel(*refs, latent, n_aliased, layer):
    x_ref, mod_ref, w_ref, lg_ref = refs[:4]
    pos = 4
    if latent:
        cos_ref, sin_ref = refs[pos:pos + 2]
        pos += 2
    pos += n_aliased
    rq_ref, rk_ref, rv_ref, rg_ref, dq_ref, dk_ref, dv_ref, cb_ref, cch_ref = refs[pos:pos + 9]
    pos += 9
    if latent:
        kvf_ref, kvb_ref = refs[pos:pos + 2]
    else:
        dk32_ref, dv32_ref, stf_ref, stb_ref = refs[pos:pos + 4]
    rows = PROJ_TILES * TILE

    def put(ref, u, idx, val):
        if n_aliased:
            ref[(u,) + idx] = val
        else:
            for other in range(DEPTH):
                ref[(u, other) + idx] = val if other == layer else jnp.zeros_like(val)

    def store(ref, val):
        ref[...] = val.reshape(PROJ_TILES, TILE, val.shape[-1])

    m = mod_ref[...]
    sh1 = m[:, 0:D_MODEL]
    sc1 = m[:, D_MODEL:2 * D_MODEL]
    h = (x_ref[...].reshape(rows, D_MODEL) * (1.0 + sc1) + sh1).astype(BF)

    def col(off, width):
        return _dot(h, w_ref[:, off:off + width])

    store(rq_ref, col(_O_RQ, RET_W).astype(BF))
    rk = col(_O_RK, RET_W) * (RET_DK ** -0.5)
    store(rk_ref, rk.astype(BF))
    rv = col(_O_RV, RET_W).astype(BF)
    store(rv_ref, rv)
    store(rg_ref, col(_O_RG, RET_W))

    lg = _log_sigmoid(lg_ref[...])
    p = (_iota((rows, 1), 0) & (TILE - 1)).astype(F32)
    kf = (rk * jnp.exp((TILE - 1.0 - p) * lg[0:1])).astype(BF)
    kb = (rk * jnp.exp(p * lg[1:2])).astype(BF)
    for u in range(PROJ_TILES):
        tile = slice(u * TILE, (u + 1) * TILE)
        kvf = _dot_tn(kf[tile], rv[tile])
        kvb = _dot_tn(kb[tile], rv[tile])
        if latent:
            kvf_ref[u] = kvf
            kvb_ref[u] = kvb
        else:
            for hd in range(H_RET):
                lo = hd * RET_DK
                put(stf_ref, u, (hd,), kvf[lo:lo + RET_DK, lo:lo + RET_DK])
                put(stb_ref, u, (hd,), kvb[lo:lo + RET_DK, lo:lo + RET_DK])

    dq = col(_O_DQ, DIFF_W)
    dk = col(_O_DK, DIFF_W)
    dv = col(_O_DV, DIFF_W)
    if latent:
        cos = cos_ref[...].reshape(rows, DIFF_W)
        sin = sin_ref[...].reshape(rows, DIFF_W)
        dq = dq * cos + _swap16(dq) * sin
        dk = dk * cos + _swap16(dk) * sin
    else:
        for u in range(PROJ_TILES):
            put(dk32_ref, u, (), dk[u * TILE:(u + 1) * TILE])
            put(dv32_ref, u, (), dv[u * TILE:(u + 1) * TILE])
    store(dq_ref, (dq * QK_SCALE_LOG2).astype(BF))
    store(dk_ref, dk.astype(BF))
    store(dv_ref, _augment_v(dv.astype(BF)))

    store(cb_ref, col(_O_CB, CONV_CH))
    store(cch_ref, col(_O_CC, CONV_CH) * col(_O_CH, CONV_CH))


def _proj(x, mod, layer, row_fn, w_in_b, lg_lanes, rope_tabs=None, stacked=None):
    S, T, _ = x.shape
    n = T // TILE
    G = PROJ_TILES
    tiles = S * n
    latent = rope_tabs is not None
    assert tiles % G == 0 and (n % G == 0 or n == 1)

    def tok(w):
        return pl.BlockSpec((G, TILE, w), lambda t: (t, 0, 0))

    def tok_shape(w, dt):
        return jax.ShapeDtypeStruct((tiles, TILE, w), dt)

    in_specs = [
        tok(D_MODEL),
        pl.BlockSpec((None, None, 1, 6 * D_MODEL), lambda t: (layer, row_fn(t * G // n), 0, 0)),
        pl.BlockSpec((None, D_MODEL, IN_WIDTH), lambda t: (layer, 0, 0)),
        pl.BlockSpec((2, RET_W), lambda t: (0, 0)),
    ]
    args = [x.reshape(tiles, TILE, D_MODEL), mod, w_in_b, lg_lanes]
    out_specs = [tok(RET_W)] * 4 + [tok(DIFF_W)] * 2 + [tok(V_AUG_W)] + [tok(CONV_CH)] * 2
    out_shape = ([tok_shape(RET_W, BF)] * 3 + [tok_shape(RET_W, F32)] + [tok_shape(DIFF_W, BF)] * 2
                 + [tok_shape(V_AUG_W, BF)]
                 + [tok_shape(CONV_CH, F32)] * 2)
    aliases = {}
    if latent:
        in_specs += [pl.BlockSpec((G, TILE, DIFF_W), lambda t: (t % (n // G), 0, 0))] * 2
        args += [tab.reshape(n, TILE, DIFF_W) for tab in rope_tabs]
        out_specs += [pl.BlockSpec((G, RET_W, RET_W), lambda t: (t, 0, 0))] * 2
        out_shape += [jax.ShapeDtypeStruct((tiles, RET_W, RET_W), F32)] * 2
    else:
        assert n == 1
        kv_shape = jax.ShapeDtypeStruct((S, DEPTH, T, DIFF_W), F32)
        st_shape = jax.ShapeDtypeStruct((S, DEPTH, H_RET, RET_DK, RET_DK), F32)
        if stacked is not None:
            in_specs += [_HBM] * 4
            args += list(stacked)
            aliases = {len(args) - 4 + j: len(out_shape) + j for j in range(4)}
            out_specs += [pl.BlockSpec((G, None, TILE, DIFF_W), lambda t: (t, layer, 0, 0))] * 2
            out_specs += [pl.BlockSpec((G, None, H_RET, RET_DK, RET_DK), lambda t: (t, layer, 0, 0, 0))] * 2
        else:
            out_specs += [pl.BlockSpec((G, DEPTH, TILE, DIFF_W), lambda t: (t, 0, 0, 0))] * 2
            out_specs += [pl.BlockSpec((G, DEPTH, H_RET, RET_DK, RET_DK), lambda t: (t, 0, 0, 0, 0))] * 2
        out_shape += [kv_shape, kv_shape, st_shape, st_shape]
    outs = pl.pallas_call(
        functools.partial(_proj_kernel, latent=latent, n_aliased=len(aliases), layer=layer),
        grid=(tiles // G,),
        in_specs=in_specs,
        out_specs=out_specs,
        out_shape=out_shape,
        input_output_aliases=aliases,
        compiler_params=_params(("parallel",)),
        name="proj_lat" if latent else "proj_ctx",
    )(*args)
    token_outs = [o.reshape(S, T, o.shape[-1]) for o in outs[:9]]
    if latent:
        return token_outs + [o.reshape(S, n, RET_W, RET_W) for o in outs[9:]]
    return token_outs + list(outs[9:])


def _scan_kernel(kvf_ref, kvb_ref, s0f_ref, s0b_ref, lg_ref, sf_ref, sb_ref, *, n):
    lg = _log_sigmoid(lg_ref[...])
    dec = jnp.exp(float(TILE) * lg)
    same_head = (_iota((RET_W, RET_W), 0) >> 6) == (_iota((RET_W, RET_W), 1) >> 6)

    s = jnp.where(same_head, s0f_ref[...], 0.0)
    for c in range(n):
        sf_ref[c] = s
        s = s * dec[0:1] + jnp.where(same_head, kvf_ref[c], 0.0)
    s = jnp.where(same_head, s0b_ref[...], 0.0)
    for c in reversed(range(n)):
        sb_ref[c] = s
        s = s * dec[1:2] + jnp.where(same_head, kvb_ref[c], 0.0)


def _scan(kvf, kvb, s0f, s0b, lg_lanes):
    S, n = kvf.shape[:2]
    chunks = pl.BlockSpec((None, n, RET_W, RET_W), lambda s: (s, 0, 0, 0))
    one = pl.BlockSpec((None, RET_W, RET_W), lambda s: (s, 0, 0))
    return pl.pallas_call(
        functools.partial(_scan_kernel, n=n),
        grid=(S,),
        in_specs=[chunks, chunks, one, one, pl.BlockSpec((2, RET_W), lambda s: (0, 0))],
        out_specs=[chunks, chunks],
        out_shape=[jax.ShapeDtypeStruct((S, n, RET_W, RET_W), F32)] * 2,
        compiler_params=_params(("parallel",)),
        name="ret_scan",
    )(kvf, kvb, s0f, s0b, lg_lanes)


def _diff_heads(dq_ref, kv_refs, heads, lam):
    lane = _iota((1, DIFF_V), 1)
    scores = []
    for hd in heads:
        lo = hd * DIFF_V
        q_h = dq_ref[:, lo:lo + DIFF_V]
        qs = jnp.concatenate([jnp.where(lane < DIFF_QK, q_h, 0), jnp.where(lane >= DIFF_QK, q_h, 0)], axis=0)
        scores.append(jnp.concatenate([_dot_nt(qs, k[:, lo:lo + DIFF_V]) for k, _ in kv_refs], axis=1))
    s = jnp.concatenate(scores, axis=0)
    p = jnp.exp2(s - jnp.max(s, axis=-1, keepdims=True))

    def times_v(w, hd, width):
        acc = None
        start = 0
        for k, v in kv_refs:
            part = _dot(w[:, start:start + k.shape[0]], v[:, hd * 2 * DIFF_V:hd * 2 * DIFF_V + width])
            acc = part if acc is None else acc + part
            start += k.shape[0]
        return acc

    outs = []
    p = p.astype(BF)
    for n, hd in enumerate(heads):
        acc = times_v(p[n * 2 * TILE:(n + 1) * 2 * TILE], hd, 2 * DIFF_V)
        o0 = acc[0:TILE, 0:DIFF_V] * (1.0 / acc[0:TILE, DIFF_V:DIFF_V + 1])
        o1 = acc[TILE:, 0:DIFF_V] * (lam / acc[TILE:, DIFF_V:DIFF_V + 1])
        outs.append(o0 - o1)
    return outs


N_ROUTE_IN = 8
N_ROUTE_OUT = 4


def _mixer_kernel(*refs, paths, lam_init):
    t = pl.program_id(0)
    n_in = [_n_token_mix_refs(cached) + N_ROUTE_IN for _, _, cached in paths]
    in_end = sum(n_in)
    out_end = in_end + N_ROUTE_OUT * len(paths)
    views = []
    for k, (n_tiles, n_total, cached) in enumerate(paths):
        lo = sum(n_in[:k])
        n_a = _n_token_mix_refs(cached)
        views.append(dict(
            mix_in=refs[lo:lo + n_a], route_in=refs[lo + n_a:lo + n_in[k]],
            outs=refs[in_end + N_ROUTE_OUT * k:in_end + N_ROUTE_OUT * (k + 1)],
            mix_even=refs[out_end + 3 * k], mix_odd=refs[out_end + 3 * k + 1], decay=refs[out_end + 3 * k + 2],
            tile_in_seq=lax.rem(jnp.minimum(t, n_total - 1), n_tiles), n_tiles=n_tiles, cached=cached))

    @pl.when(t == 0)
    def _():
        for v in views:
            v["mix_odd"][...] = jnp.zeros_like(v["mix_odd"])
            lg = _log_sigmoid(v["mix_in"][-5][...])
            dist = (_iota((TILE, TILE), 0) - _iota((TILE, TILE), 1)).astype(F32)
            diag2 = jnp.where(dist == 0.0, 2.0, 1.0)
            for hd in range(H_RET):
                lgf = lg[0:1, hd * RET_DK:hd * RET_DK + 1]
                lgb = lg[1:2, hd * RET_DK:hd * RET_DK + 1]
                v["decay"][hd] = jnp.exp(jnp.abs(dist) * jnp.where(dist > 0.0, lgf, lgb)) * diag2

    def step(write, read):
        for v in views:
            v[write][...] = _token_mix(*v["mix_in"], v["decay"], tile_in_seq=v["tile_in_seq"],
                                       n_tiles=v["n_tiles"], lam_init=lam_init, cached=v["cached"])
        for v in views:
            _project_route(v[read][...], *v["route_in"], *v["outs"])

    @pl.when(lax.rem(t, 2) == 0)
    def _():
        step("mix_even", "mix_odd")

    @pl.when(lax.rem(t, 2) == 1)
    def _():
        step("mix_odd", "mix_even")


def _n_token_mix_refs(cached):
    return 18 + (2 if cached else 0)


def _token_mix(*refs, tile_in_seq, n_tiles, lam_init, cached):
    rq_ref, rk_ref, rv_ref, rg_ref, sf_ref, sb_ref, dq_ref = refs[:7]
    pos = 7
    kv_refs = []
    if cached:
        kv_refs.append((refs[pos], refs[pos + 1]))
        pos += 2
    kv_refs.append((refs[pos], refs[pos + 1]))
    pos += 2
    (cb_ref, cch_ref, cprev_ref, cnext_ref, lg_ref, lamp_ref, subg_ref, cw_ref, cbias_ref,
     decay_ref) = refs[pos:pos + 10]
    i = tile_in_seq

    lg = _log_sigmoid(lg_ref[...])
    head_of_lane = _iota((1, RET_W), 1) >> 6
    q = rq_ref[...]
    k = rk_ref[...]
    v = rv_ref[...]
    ret_o = jnp.zeros((TILE, RET_W), F32)
    for hd in range(H_RET):
        in_head = head_of_lane == hd
        sc = _dot_nt(jnp.where(in_head, q, 0), k)
        ret_o = ret_o + jnp.where(in_head, _dot((sc * decay_ref[hd]).astype(BF), v), 0.0)
    p = _iota((TILE, 1), 0).astype(F32)
    ret_o = ret_o + _dot(q, sf_ref[...].astype(BF)) * jnp.exp((p + 1.0) * lg[0:1])
    ret_o = ret_o + _dot(q, sb_ref[...].astype(BF)) * jnp.exp((float(TILE) - p) * lg[1:2])
    avg = jnp.where((_iota((RET_W, RET_W), 0) >> 6) == (_iota((RET_W, RET_W), 1) >> 6),
                    1.0 / RET_DK, 0.0).astype(BF)
    rc = ret_o - _dot_hl(ret_o, avg)
    ret = rc * lax.rsqrt(_dot_hl(rc * rc, avg) + EPS) * _silu(rg_ref[...])

    lp = lamp_ref[...]
    lam = (jnp.exp(jnp.sum(lp[0:1] * lp[1:2], axis=-1, keepdims=True))
           - jnp.exp(jnp.sum(lp[2:3] * lp[3:4], axis=-1, keepdims=True)) + lam_init)
    subg = subg_ref[...] * (1.0 - lam_init)
    head_groups = [[hd] for hd in range(H_DIFF)] if cached else [list(range(H_DIFF))]
    heads = []
    for group in head_groups:
        for o in _diff_heads(dq_ref, kv_refs, group, lam):
            o = o * lax.rsqrt(jnp.mean(o * o, axis=-1, keepdims=True) + EPS) * subg
            heads.append(o.astype(BF))
    diff = jnp.concatenate(heads, axis=1)

    cch = cch_ref[...]
    prev = jnp.where(i > 0, cprev_ref[7:8, :], 0.0)
    nxt = jnp.where(i < n_tiles - 1, cnext_ref[0:1, :], 0.0)
    r = _iota((TILE, 1), 0)
    up = jnp.where(r == 0, prev, pltpu.roll(cch, 1, axis=0))
    dn = jnp.where(r == TILE - 1, nxt, pltpu.roll(cch, TILE - 1, axis=0))
    cw = cw_ref[...]
    conv = cb_ref[...] * (up * cw[0:1] + cch * cw[1:2] + dn * cw[2:3] + cbias_ref[...])
    return jnp.concatenate([ret.astype(BF), diff, conv.astype(BF)], axis=1)


def _project_route(mixed, x_ref, mod_ref, wout_ref, lng_ref, lnb_ref, wrh_ref, wrl_ref, br_ref,
                   x1_ref, hs_ref, slots_ref, cnt_ref):
    m = mod_ref[...]
    g1 = m[:, 2 * D_MODEL:3 * D_MODEL]
    sh2 = m[:, 3 * D_MODEL:4 * D_MODEL]
    sc2 = m[:, 4 * D_MODEL:5 * D_MODEL]
    x1 = _layer_norm(ALPHA * x_ref[...] + g1 * _dot(mixed, wout_ref[...]), lng_ref[...], lnb_ref[...])
    x1_ref[...] = x1

    h2 = x1 * (1.0 + sc2) + sh2
    h2h, h2l = _split(h2)
    wrh = wrh_ref[...]
    logits = (_dot_nt(wrh, h2h) + (_dot_nt(wrh, h2l) + _dot_nt(wrl_ref[...], h2h))) + br_ref[...]
    neg = -jnp.inf
    g_id = _iota((EXPERT_ROW0, TILE), 0)
    gl = jnp.where(g_id < N_GROUPS, logits[0:EXPERT_ROW0], neg)
    gmax = jnp.max(gl, axis=0, keepdims=True)
    g_idx = jnp.min(jnp.where(gl == gmax, g_id, N_GROUPS), axis=0, keepdims=True)
    g_w = 1.0 / jnp.sum(jnp.exp(gl - gmax), axis=0, keepdims=True)
    e_id = _iota((N_EXPERTS, TILE), 0)
    el = jnp.where((e_id >> 2) == g_idx, logits[EXPERT_ROW0:EXPERT_ROW0 + N_EXPERTS], neg)
    v1 = jnp.max(el, axis=0, keepdims=True)
    i1 = jnp.min(jnp.where(el == v1, e_id, N_EXPERTS), axis=0, keepdims=True)
    el2 = jnp.where(e_id == i1, neg, el)
    v2 = jnp.max(el2, axis=0, keepdims=True)
    i2 = jnp.min(jnp.where(el2 == v2, e_id, N_EXPERTS), axis=0, keepdims=True)
    t = jnp.exp(v2 - v1)
    w1 = g_w / (1.0 + t)
    w2 = g_w * t / (1.0 + t)

    sel1 = e_id == i1
    sel2 = e_id == i2
    sel = jnp.where(sel1 | sel2, 1.0, 0.0)
    earlier = _ones_where(_iota((TILE, TILE), 0) < _iota((TILE, TILE), 1))
    rank = _dot(sel.astype(BF), earlier)
    cnt = jnp.sum(sel, axis=1, keepdims=True).astype(I32)
    padded = ((cnt + (CHUNK - 1)) >> 4) << 4
    incl = jnp.broadcast_to(padded, (N_EXPERTS, LANES))
    e_row = _iota((N_EXPERTS, LANES), 0)
    for step in (1, 2, 4, 8):
        incl = incl + jnp.where(e_row >= step, pltpu.roll(incl, step, axis=0), 0)
    seg_off = (incl[:, 0:1] - padded).astype(F32)
    spos = seg_off + rank
    slot1 = jnp.sum(jnp.where(sel1, spos, 0.0), axis=0, keepdims=True).astype(I32)
    slot2 = jnp.sum(jnp.where(sel2, spos, 0.0), axis=0, keepdims=True).astype(I32)
    s_id = _iota((CAP, TILE), 0)
    at1 = s_id == slot1
    at2 = s_id == slot2
    hs_ref[:, 0:D_MODEL] = _dot(_ones_where(at1 | at2), h2h).astype(BF)
    w1h = w1.astype(BF).astype(F32)
    w2h = w2.astype(BF).astype(F32)
    ones = jnp.ones((TILE, LANES), BF)
    g_hi = _dot((jnp.where(at1, w1h, 0.0) + jnp.where(at2, w2h, 0.0)).astype(BF), ones)
    g_lo = _dot((jnp.where(at1, w1 - w1h, 0.0) + jnp.where(at2, w2 - w2h, 0.0)).astype(BF), ones)
    lane = _iota((CAP, LANES), 1)
    hs_ref[:, D_MODEL:] = jnp.where(lane == 0, g_hi, jnp.where(lane == 1, g_lo, 0.0)).astype(BF)
    cnt_ref[...] = jnp.broadcast_to(padded, (N_EXPERTS, LANES))

    dr = _iota((LANES, TILE), 0)
    digits = jnp.where(dr == 0, slot1 & (SLOT_RADIX - 1), jnp.where(dr == 1, slot1 >> 5,
             jnp.where(dr == 2, slot2 & (SLOT_RADIX - 1), jnp.where(dr == 3, slot2 >> 5, 0))))
    eye = _ones_where(_iota((TILE, TILE), 0) == _iota((TILE, TILE), 1))
    cols = _dot_nt(eye, digits.astype(F32).astype(BF))
    s1c = (cols[:, 0:1] + SLOT_RADIX * cols[:, 1:2]).astype(I32)
    s2c = (cols[:, 2:3] + SLOT_RADIX * cols[:, 3:4]).astype(I32)
    tl = _iota((TILE, LANES), 1)
    slots_ref[...] = jnp.where(tl == 0, s1c, jnp.where(tl == 1, s2c, 0))


def _mixer_io(x, mod, layer, row_fn, pr, states, cache_kv, new_kv, w_out_b, lg_lanes, lamp, subg, cw, cbias,
              lng, lnb, wrh, wrl, br):
    S, T, _ = x.shape
    n = T // TILE
    rq, rk, rv, rg, dq, cb, cch = pr
    t8 = TILE // 8
    cached = cache_kv is not None
    last = S * n - 1

    def at_mix(t):
        return jnp.minimum(t, last) // n, jnp.minimum(t, last) % n

    def at_out(t):
        return jnp.clip(t - 1, 0, last) // n, jnp.clip(t - 1, 0, last) % n

    def tok(w, at):
        return pl.BlockSpec((None, TILE, w), lambda t: (*at(t), 0))

    def full(shape):
        return pl.BlockSpec(shape, lambda t: (0,) * len(shape))

    def seq(a):
        mode = pl.Buffered(1) if n > 1 else None
        return pl.BlockSpec((None,) + a.shape[1:], lambda t: (at_mix(t)[0], 0, 0), pipeline_mode=mode)

    def halo(offset):
        def index(t):
            s, i = at_mix(t)
            return (s, jnp.clip(i * t8 + offset, 0, T // 8 - 1), 0)
        return pl.BlockSpec((None, 8, CONV_CH), index)

    if states is None:
        zero_state = jnp.zeros((1, 1, RET_W, RET_W), F32)
        states = (zero_state, zero_state)
        state = pl.BlockSpec((None, None, RET_W, RET_W), lambda t: (0, 0, 0, 0))
    else:
        state = pl.BlockSpec((None, None, RET_W, RET_W), lambda t: (*at_mix(t), 0, 0))

    in_specs = [tok(RET_W, at_mix)] * 4 + [state, state, tok(DIFF_W, at_mix)]
    args = [rq, rk, rv, rg, states[0], states[1], dq]
    for kv in ([cache_kv] if cached else []) + [new_kv]:
        in_specs += [seq(kv[0]), seq(kv[1])]
        args += list(kv)
    in_specs += [tok(CONV_CH, at_mix), tok(CONV_CH, at_mix), halo(-1), halo(t8),
                 full((2, RET_W)), full((4, DIFF_QK)), full((1, DIFF_V)), full((3, CONV_CH)), full((1, CONV_CH))]
    args += [cb, cch, cch, cch, lg_lanes, lamp, subg, cw, cbias]
    assert len(args) == _n_token_mix_refs(cached)
    in_specs += [tok(D_MODEL, at_out),
                 pl.BlockSpec((None, None, 1, 6 * D_MODEL), lambda t: (layer, row_fn(at_out(t)[0]), 0, 0)),
                 pl.BlockSpec((None, D_MODEL, D_MODEL), lambda t: (layer, 0, 0), pipeline_mode=pl.Buffered(1)),
                 full((1, D_MODEL)), full((1, D_MODEL)),
                 full((ROUTER_ROWS, D_MODEL)), full((ROUTER_ROWS, D_MODEL)), full((ROUTER_ROWS, 1))]
    args += [x, mod, w_out_b, lng, lnb, wrh, wrl, br]

    def out_tile(t):
        s, i = at_out(t)
        return s * n + i

    return dict(
        path=(n, S * n, cached), args=args, in_specs=in_specs,
        out_specs=[tok(D_MODEL, at_out),
                   pl.BlockSpec((CAP, HS_W), lambda t: (out_tile(t), 0)),
                   tok(LANES, at_out),
                   pl.BlockSpec((None, None, N_EXPERTS, LANES), lambda t: (*at_out(t), 0, 0))],
        out_shape=[jax.ShapeDtypeStruct((S, T, D_MODEL), F32),
                   jax.ShapeDtypeStruct((S * n * CAP, HS_W), BF),
                   jax.ShapeDtypeStruct((S, T, LANES), I32),
                   jax.ShapeDtypeStruct((S, n, N_EXPERTS, LANES), I32)],
        scratch=[pltpu.VMEM((TILE, D_MODEL), BF), pltpu.VMEM((TILE, D_MODEL), BF),
                 pltpu.VMEM((H_RET, TILE, TILE), F32)])


def _mixers(ios, lam_init):
    steps = max(io["path"][1] for io in ios) + 1
    outs = pl.pallas_call(
        functools.partial(_mixer_kernel, paths=tuple(io["path"] for io in ios), lam_init=lam_init),
        grid=(steps,),
        in_specs=[s for io in ios for s in io["in_specs"]],
        out_specs=[s for io in ios for s in io["out_specs"]],
        out_shape=[s for io in ios for s in io["out_shape"]],
        scratch_shapes=[s for io in ios for s in io["scratch"]],
        compiler_params=_params(("arbitrary",)),
        name="mixer",
    )(*[a for io in ios for a in io["args"]])
    return [outs[N_ROUTE_OUT * k:N_ROUTE_OUT * (k + 1)] for k in range(len(ios))]


def _block_schedule(padded, chunks_a):
    NT = padded.shape[0]
    n_chunks = NT * CHUNKS_PER_TILE
    n_blocks = n_chunks // CPB + N_EXPERTS
    c16 = padded // CHUNK
    ends = jnp.cumsum(c16, axis=1)
    before = jnp.cumsum(c16, axis=0) - c16
    per_expert = jnp.sum(c16, axis=0)
    nb = (per_expert + (CPB - 1)) // CPB
    b_end = jnp.cumsum(nb)
    q = jnp.arange(CHUNKS_PER_TILE, dtype=I32)
    e = jnp.arange(N_EXPERTS, dtype=I32)
    key = jnp.sum((ends[:, None, :] <= q[None, :, None]).astype(I32), axis=-1)
    pos_e = ((b_end - nb) * CPB + before)[:, None, :] + (q[None, :, None] - (ends - c16)[:, None, :])
    pos = jnp.sum(jnp.where(key[..., None] == e, pos_e, 0), axis=-1)
    pos = jnp.where(key < N_EXPERTS, pos, -1).reshape(-1)
    match = pos[None, :] == jnp.arange(n_blocks * CPB, dtype=I32)[:, None]
    cid = jnp.sum(jnp.where(match, jnp.arange(n_chunks, dtype=I32)[None, :], 0), axis=-1)
    nv = jnp.sum(match.astype(I32).reshape(n_blocks, -1), axis=-1)
    first = match & (jnp.arange(n_chunks, dtype=I32) < chunks_a)[None, :]
    nv_a = jnp.sum(first.astype(I32).reshape(n_blocks, -1), axis=-1)
    b = jnp.arange(n_blocks, dtype=I32)
    eb = jnp.minimum(jnp.sum((b_end[None, :] <= b[:, None]).astype(I32), axis=-1), N_EXPERTS - 1)
    zero = jnp.zeros((1,), I32)
    counts = jnp.concatenate([nv, zero, nv_a, zero]).astype(I32)
    return eb.astype(I32), counts, jnp.concatenate([cid, jnp.zeros((CPB,), I32)]).astype(I32)


def _moe_kernel(eb_ref, nv_ref, cid_ref, hs_a, hs_b, wg_ref, wu_ref, wd_ref, ys_a, ys_b,
                xbuf, ybuf, wg_b, wu_b, wd_b, in_sem, out_sem, *, n_blocks, chunks_a):
    b = pl.program_id(0)

    def chunk_rows(c):
        return pl.ds(pl.multiple_of(c * CHUNK, CHUNK), CHUNK)

    def gather_copy(which, c, j, slot):
        src = (hs_a, hs_b)[which]
        return pltpu.make_async_copy(src.at[chunk_rows(c)], xbuf.at[slot, chunk_rows(j)], in_sem.at[slot])

    def scatter_copy(which, c, j, slot):
        dst = (ys_a, ys_b)[which]
        return pltpu.make_async_copy(ybuf.at[slot, chunk_rows(j)],
                                     dst.at[chunk_rows(c), pl.ds(0, D_MODEL)], out_sem.at[slot])

    def for_chunks(blk, fn):
        n_a = nv_ref[n_blocks + 1 + blk]

        def body_a(j, carry):
            fn(j, 0, cid_ref[blk * CPB + j])
            return carry

        def body_b(j, carry):
            fn(j, 1, cid_ref[blk * CPB + j] - chunks_a)
            return carry
        lax.fori_loop(0, n_a, body_a, 0)
        lax.fori_loop(n_a, nv_ref[blk], body_b, 0)

    def each_chunk(blk, fn):
        n_a = nv_ref[n_blocks + 1 + blk]
        n_v = nv_ref[blk]
        for j in range(CPB):
            c = cid_ref[blk * CPB + j]

            @pl.when(j < n_a)
            def _():
                fn(j, 0, c)

            @pl.when((j >= n_a) & (j < n_v))
            def _():
                fn(j, 1, c - chunks_a)

    def wait_gathers(blk, slot):
        rows = pl.ds(0, nv_ref[blk] * CHUNK)

        @pl.when(nv_ref[blk] > 0)
        def _():
            pltpu.make_async_copy(hs_a.at[rows], xbuf.at[slot, rows], in_sem.at[slot]).wait()

    def wait_scatters(blk, slot):
        rows = pl.ds(0, nv_ref[blk] * CHUNK)

        @pl.when(nv_ref[blk] > 0)
        def _():
            pltpu.make_async_copy(ybuf.at[slot, rows], ys_a.at[rows, pl.ds(0, D_MODEL)], out_sem.at[slot]).wait()

    slot = lax.rem(b, 2)
    other = 1 - slot

    @pl.when(b == 0)
    def _():
        xbuf[...] = jnp.zeros_like(xbuf)
        for_chunks(0, lambda j, w, c: gather_copy(w, c, j, 0).start())

    @pl.when((b == 0) | (eb_ref[b] != eb_ref[jnp.maximum(b - 1, 0)]))
    def _():
        wg_b[...] = wg_ref[...].astype(BF)
        wu_b[...] = wu_ref[...].astype(BF)
        wd_b[...] = wd_ref[...].astype(BF)

    wait_gathers(b, slot)

    @pl.when(b >= 2)
    def _():
        wait_scatters(b - 2, slot)

    @pl.when(nv_ref[b] > 0)
    def _():
        each_chunk(b + 1, lambda j, w, c: gather_copy(w, c, j, other).start())
        x = xbuf[slot, :, 0:D_MODEL]
        gate = (xbuf[slot, :, D_MODEL:D_MODEL + 1].astype(F32)
                + xbuf[slot, :, D_MODEL + 1:D_MODEL + 2].astype(F32))
        hg = _dot(x, wg_b[...])
        hu = _dot(x, wu_b[...])
        act = _silu(hg) * hu * gate
        ybuf[slot] = _dot(act.astype(BF), wd_b[...]).astype(BF)
        each_chunk(b, lambda j, w, c: scatter_copy(w, c, j, slot).start())

    @pl.when(b == n_blocks - 1)
    def _():
        if n_blocks >= 2:
            wait_scatters(b - 1, other)
        wait_scatters(b, slot)


def _moe(hs_a, hs_b, eb, nv, cid, layer, w_gate, w_up, w_down):
    n_blocks = eb.shape[0]

    def w_spec(a, c):
        return pl.BlockSpec((None, None, a, c), lambda b, eb, nv, cid: (layer, eb[b], 0, 0))

    grid_spec = pltpu.PrefetchScalarGridSpec(
        num_scalar_prefetch=3,
        grid=(n_blocks,),
        in_specs=[_HBM, _HBM, w_spec(D_MODEL, D_EXPERT), w_spec(D_MODEL, D_EXPERT), w_spec(D_EXPERT, D_MODEL)],
        out_specs=[_HBM, _HBM],
        scratch_shapes=[pltpu.VMEM((2, MOE_TMB, HS_W), BF), pltpu.VMEM((2, MOE_TMB, D_MODEL), BF),
                        pltpu.VMEM((D_MODEL, D_EXPERT), BF), pltpu.VMEM((D_MODEL, D_EXPERT), BF),
                        pltpu.VMEM((D_EXPERT, D_MODEL), BF),
                        pltpu.SemaphoreType.DMA((2,)), pltpu.SemaphoreType.DMA((2,))],
    )
    return pl.pallas_call(
        functools.partial(_moe_kernel, n_blocks=n_blocks, chunks_a=hs_a.shape[0] // CHUNK),
        grid_spec=grid_spec,
        out_shape=[jax.ShapeDtypeStruct(hs_a.shape, BF), jax.ShapeDtypeStruct(hs_b.shape, BF)],
        input_output_aliases={3: 0, 4: 1},
        compiler_params=_params(("arbitrary",)),
        name="moe_sorted",
    )(eb, nv, cid, hs_a, hs_b, w_gate, w_up, w_down)


def _combine_kernel(ys_ref, slots_ref, x1_ref, mod_ref, lng_ref, lnb_ref, o_ref):
    g2 = mod_ref[:, 5 * D_MODEL:6 * D_MODEL]
    s_lane = _iota((TILE, CAP), 1)
    for u in range(COMBINE_TILES_PER_STEP):
        sl = slots_ref[u]
        pick = _ones_where((s_lane == sl[:, 0:1]) | (s_lane == sl[:, 1:2]))
        ffn = _dot(pick, ys_ref[u * CAP:(u + 1) * CAP, 0:D_MODEL])
        o_ref[u] = _layer_norm(ALPHA * x1_ref[u] + g2 * ffn, lng_ref[...], lnb_ref[...])


def _combine(ys, slots, x1, mod, layer, row_fn, lng, lnb):
    S, T, _ = x1.shape
    n = T // TILE
    U = COMBINE_TILES_PER_STEP
    assert (S * n) % U == 0 and (n % U == 0 or n == 1)

    def tok(w):
        return pl.BlockSpec((U, TILE, w), lambda t: (t, 0, 0))

    out = pl.pallas_call(
        _combine_kernel,
        grid=(S * n // U,),
        in_specs=[
            pl.BlockSpec((U * CAP, HS_W), lambda t: (t, 0)),
            tok(LANES), tok(D_MODEL),
            pl.BlockSpec((None, None, 1, 6 * D_MODEL), lambda t: (layer, row_fn(t * U // n), 0, 0)),
            pl.BlockSpec((1, D_MODEL), lambda t: (0, 0)),
            pl.BlockSpec((1, D_MODEL), lambda t: (0, 0)),
        ],
        out_specs=tok(D_MODEL),
        out_shape=jax.ShapeDtypeStruct((S * n, TILE, D_MODEL), F32),
        compiler_params=_params(("parallel",)),
        name="moe_combine",
    )(ys, slots.reshape(S * n, TILE, LANES), x1.reshape(S * n, TILE, D_MODEL), mod, lng, lnb)
    return out.reshape(S, T, D_MODEL)


def _rope_tables(n_lat):
    half = DIFF_QK // 2
    pairs = half // 2
    inv = (1.0 / (ROPE_BASE ** (np.arange(pairs, dtype=np.float32) * 2.0 / half))).astype(np.float32)
    t = np.arange(n_lat)
    ang_r = ((t // GRID_W).astype(np.float32)[:, None] * inv[None, :]).astype(np.float64)
    ang_c = ((t % GRID_W).astype(np.float32)[:, None] * inv[None, :]).astype(np.float64)
    cos = np.concatenate([np.cos(ang_r)] * 2 + [np.cos(ang_c)] * 2, axis=1)
    sin = np.concatenate([-np.sin(ang_r), np.sin(ang_r), -np.sin(ang_c), np.sin(ang_c)], axis=1)
    reps = DIFF_W // DIFF_QK
    return (jnp.asarray(np.tile(cos, (1, reps)), F32), jnp.asarray(np.tile(sin, (1, reps)), F32))


def _block_diag(s):
    S = s.shape[0]
    eye = jnp.eye(H_RET, dtype=s.dtype)
    return jnp.einsum('shdv,hg->shdgv', s, eye).reshape(S, RET_W, RET_W)


def _router_rows(w_group, b_group, w_expert, b_expert):
    pad = EXPERT_ROW0 - N_GROUPS
    tail = ROUTER_ROWS - EXPERT_ROW0 - N_EXPERTS
    w = jnp.concatenate([w_group.T, jnp.zeros((pad, D_MODEL), F32),
                         w_expert.reshape(D_MODEL, N_EXPERTS).T, jnp.zeros((tail, D_MODEL), F32)], axis=0)
    bias = jnp.concatenate([b_group, jnp.zeros((pad,), F32), b_expert.reshape(N_EXPERTS), jnp.zeros((tail,), F32)])
    hi = w.astype(BF)
    return hi, (w - hi.astype(F32)).astype(BF), bias.reshape(ROUTER_ROWS, 1)


def kernel(x_prompt, x_sample, cache_diff_k, cache_diff_v, state_ret_fwd, state_ret_bwd, c, c_ctx, w_mod, b_mod, w_in, ret_decay_logit, diff_lambda, diff_subln_g, conv_w, conv_b, w_out, ln_g, ln_b, w_router_group, b_router_group, w_router_expert, b_router_expert, w_gate, w_up, w_down):
    B, T_ctx, _ = x_prompt.shape
    Bd, T_lat, _ = x_sample.shape
    assert T_ctx == TILE and T_lat % TILE == 0 and T_lat % GRID_W == 0
    assert 1 + Bd <= MOD_ROWS

    cond = jnp.concatenate([c_ctx[None, :], c, jnp.zeros((MOD_ROWS - 1 - Bd, D_MODEL), F32)], axis=0)
    mod = _modulation(cond, w_mod, b_mod).reshape(DEPTH, MOD_ROWS, 1, 6 * D_MODEL)
    rope_tabs = _rope_tables(T_lat)
    w_in_b = w_in.astype(BF)
    w_out_b = w_out.astype(BF)

    ctx_row = lambda s: 0
    lat_row = lambda s: s + 1
    ctx_tiles = B * (T_ctx // TILE)
    lat_tiles = Bd * (T_lat // TILE)

    yp, ys = x_prompt, x_sample
    stacked = None
    for l in range(DEPTH):
        lam_init = 0.8 - 0.6 * math.exp(-0.3 * l)
        lg_lanes = jnp.repeat(ret_decay_logit[l], RET_DK, axis=1)
        wrh, wrl, br = _router_rows(w_router_group[l], b_router_group[l], w_router_expert[l], b_router_expert[l])
        shared = (w_out_b, lg_lanes, diff_lambda[l], diff_subln_g[l].reshape(1, DIFF_V), conv_w[l],
                  conv_b[l].reshape(1, CONV_CH), ln_g[l, 0].reshape(1, D_MODEL), ln_b[l, 0].reshape(1, D_MODEL),
                  wrh, wrl, br)
        ln2 = (ln_g[l, 1].reshape(1, D_MODEL), ln_b[l, 1].reshape(1, D_MODEL))

        pr = _proj(yp, mod, l, ctx_row, w_in_b, lg_lanes, stacked=stacked)
        rq, rk, rv, rg, dq, dk, dv, cb, cch = pr[:9]
        stacked = pr[9:]
        io_c = _mixer_io(yp, mod, l, ctx_row, (rq, rk, rv, rg, dq, cb, cch), None, None, (dk, dv), *shared)

        pr = _proj(ys, mod, l, lat_row, w_in_b, lg_lanes, rope_tabs=rope_tabs)
        rq, rk, rv, rg, dq, dk, dv, cb, cch, kvf, kvb = pr
        states = _scan(kvf, kvb, _block_diag(state_ret_fwd[:, l]), _block_diag(state_ret_bwd[:, l]), lg_lanes)
        cache_v = cache_diff_v[:, l].astype(BF)
        cache_kv = (cache_diff_k[:, l].reshape(Bd, -1, DIFF_W).astype(BF),
                    jnp.concatenate([cache_v, jnp.ones_like(cache_v)], axis=-1).reshape(Bd, -1, V_AUG_W))
        io_l = _mixer_io(ys, mod, l, lat_row, (rq, rk, rv, rg, dq, cb, cch), states, cache_kv, (dk, dv), *shared)
        (x1_c, hs_c, slots_c, cnt_c), (x1_l, hs_l, slots_l, cnt_l) = _mixers([io_c, io_l], lam_init)

        padded = jnp.concatenate([cnt_c.reshape(ctx_tiles, N_EXPERTS, LANES)[:, :, 0],
                                  cnt_l.reshape(lat_tiles, N_EXPERTS, LANES)[:, :, 0]], axis=0)
        eb, nv, cid = _block_schedule(padded, ctx_tiles * CHUNKS_PER_TILE)
        out_c, out_l = _moe(hs_c, hs_l, eb, nv, cid, l, w_gate, w_up, w_down)
        yp = _combine(out_c, slots_c, x1_c, mod, l, ctx_row, *ln2)
        ys = _combine(out_l, slots_l, x1_l, mod, l, lat_row, *ln2)

    new_k, new_v, st_f, st_b = stacked
    return (yp, ys, new_k.reshape(B, DEPTH, T_ctx, H_DIFF, 2, DIFF_QK),
            new_v.reshape(B, DEPTH, T_ctx, H_DIFF, DIFF_V), st_f, st_b)
```

```python
import functools
import math

import numpy as np
import jax
import jax.numpy as jnp
from jax import lax
from jax.experimental import pallas as pl
from jax.experimental.pallas import tpu as pltpu

D_MODEL = 1024
DEPTH = 2
GRID_W = 64
H_RET = 4
RET_DK = 64
RET_W = H_RET * RET_DK
H_DIFF = 4
DIFF_QK = 64
DIFF_V = 2 * DIFF_QK
DIFF_W = H_DIFF * DIFF_V
CONV_CH = 256
ROPE_BASE = 10000.0
N_GROUPS = 4
EXPERTS_PER_GROUP = 4
N_EXPERTS = N_GROUPS * EXPERTS_PER_GROUP
D_EXPERT = 512
ALPHA = (2 * DEPTH) ** 0.25
EPS = 1e-5
MOD_ROWS = 8
LANES = 128
ROUTER_ROWS = 32
EXPERT_ROW0 = 8

TILE = 256
CHUNK = 16
CAP = 2 * TILE + N_EXPERTS * CHUNK
CHUNKS_PER_TILE = CAP // CHUNK
COMBINE_TILES_PER_STEP = 2
PROJ_TILES = 2
MOE_TMB = 512
CPB = MOE_TMB // CHUNK
HS_W = D_MODEL + LANES
V_AUG_W = 2 * DIFF_W
QK_SCALE_LOG2 = (DIFF_QK ** -0.5) * math.log2(math.e)
SLOT_RADIX = 32

_O_RQ, _O_RK, _O_RV, _O_RG = 0, 256, 512, 768
_O_DQ, _O_DK, _O_DV = 1024, 1536, 2048
_O_CB, _O_CC, _O_CH = 2560, 2816, 3072
IN_WIDTH = 3328

BF = jnp.bfloat16
F32 = jnp.float32
I32 = jnp.int32

_VMEM_LIMIT = 60 * 1024 * 1024


def _dot(a, b):
    return jnp.dot(a, b, preferred_element_type=F32)


def _dot_nt(a, b):
    return lax.dot_general(a, b, (((1,), (1,)), ((), ())), preferred_element_type=F32)


def _dot_tn(a, b):
    return lax.dot_general(a, b, (((0,), (0,)), ((), ())), preferred_element_type=F32)


def _split(x):
    hi = x.astype(BF)
    lo = (x - hi.astype(F32)).astype(BF)
    return hi, lo


def _dot_hl(x, w_bf16):
    hi, lo = _split(x)
    return _dot(hi, w_bf16) + _dot(lo, w_bf16)


def _dot3(x, w_hi, w_lo):
    hi, lo = _split(x)
    return _dot(hi, w_hi) + (_dot(lo, w_hi) + _dot(hi, w_lo))


def _iota(shape, dim):
    return lax.broadcasted_iota(I32, shape, dim)


def _ones_where(cond):
    return jnp.where(cond, 1.0, 0.0).astype(BF)


def _log_sigmoid(x):
    return jnp.minimum(x, 0.0) - jnp.log1p(jnp.exp(-jnp.abs(x)))


def _silu(x):
    return x * jax.nn.sigmoid(x)


def _layer_norm(y, g, b):
    mu = jnp.mean(y, axis=-1, keepdims=True)
    yc = y - mu
    var = jnp.mean(yc * yc, axis=-1, keepdims=True)
    return yc * lax.rsqrt(var + EPS) * g + b


def _params(sem):
    return pltpu.CompilerParams(dimension_semantics=sem, vmem_limit_bytes=_VMEM_LIMIT)


_HBM = pl.BlockSpec(memory_space=pltpu.HBM)


def _mod_kernel(c_ref, w_ref, b_ref, o_ref):
    a = _silu(c_ref[...])
    w_hi, w_lo = _split(w_ref[...])
    o_ref[...] = _dot3(a, w_hi, w_lo) + b_ref[...]


def _modulation(cond, w_mod, b_mod):
    tn = 1536
    n6 = 6 * D_MODEL
    return pl.pallas_call(
        _mod_kernel,
        grid=(DEPTH, n6 // tn),
        in_specs=[
            pl.BlockSpec((MOD_ROWS, D_MODEL), lambda l, j: (0, 0)),
            pl.BlockSpec((None, D_MODEL, tn), lambda l, j: (l, 0, j)),
            pl.BlockSpec((None, 1, tn), lambda l, j: (l, 0, j)),
        ],
        out_specs=pl.BlockSpec((None, MOD_ROWS, tn), lambda l, j: (l, 0, j)),
        out_shape=jax.ShapeDtypeStruct((DEPTH, MOD_ROWS, n6), F32),
        compiler_params=_params(("parallel", "parallel")),
        name="modulation",
    )(cond, w_mod, b_mod.reshape(DEPTH, 1, n6))


def _mod_spec(layer, row_fn):
    return pl.BlockSpec((None, None, 1, 6 * D_MODEL), lambda s, i: (layer, row_fn(s), 0, 0))


def _augment_v(v):
    ones = jnp.ones((v.shape[0], DIFF_V), v.dtype)
    parts = []
    for hd in range(H_DIFF):
        parts += [v[:, hd * DIFF_V:(hd + 1) * DIFF_V], ones]
    return jnp.concatenate(parts, axis=1)


def _swap16(x):
    n = x.shape[-1]
    lane = _iota(x.shape, 1)
    return jnp.where((lane & 16) == 0, pltpu.roll(x, n - 16, axis=1), pltpu.roll(x, 16, axis=1))


def _proj_kernel(*refs, latent, n_aliased, layer):
    x_ref, mod_ref, w_ref, lg_ref = refs[:4]
    pos = 4
    if latent:
        cos_ref, sin_ref = refs[pos:pos + 2]
        pos += 2
    pos += n_aliased
    rq_ref, rk_ref, rv_ref, rg_ref, dq_ref, dk_ref, dv_ref, cb_ref, cch_ref = refs[pos:pos + 9]
    pos += 9
    if latent:
        kvf_ref, kvb_ref = refs[pos:pos + 2]
    else:
        dk32_ref, dv32_ref, stf_ref, stb_ref = refs[pos:pos + 4]
    rows = PROJ_TILES * TILE

    def put(ref, u, idx, val):
        if n_aliased:
            ref[(u,) + idx] = val
        else:
            for other in range(DEPTH):
                ref[(u, other) + idx] = val if other == layer else jnp.zeros_like(val)

    def store(ref, val):
        ref[...] = val.reshape(PROJ_TILES, TILE, val.shape[-1])

    m = mod_ref[...]
    sh1 = m[:, 0:D_MODEL]
    sc1 = m[:, D_MODEL:2 * D_MODEL]
    h = (x_ref[...].reshape(rows, D_MODEL) * (1.0 + sc1) + sh1).astype(BF)

    def col(off, width):
        return _dot(h, w_ref[:, off:off + width])

    store(rq_ref, col(_O_RQ, RET_W).astype(BF))
    rk = col(_O_RK, RET_W) * (RET_DK ** -0.5)
    store(rk_ref, rk.astype(BF))
    rv = col(_O_RV, RET_W).astype(BF)
    store(rv_ref, rv)
    store(rg_ref, col(_O_RG, RET_W))

    lg = _log_sigmoid(lg_ref[...])
    p = (_iota((rows, 1), 0) & (TILE - 1)).astype(F32)
    kf = (rk * jnp.exp((TILE - 1.0 - p) * lg[0:1])).astype(BF)
    kb = (rk * jnp.exp(p * lg[1:2])).astype(BF)
    for u in range(PROJ_TILES):
        tile = slice(u * TILE, (u + 1) * TILE)
        kvf = _dot_tn(kf[tile], rv[tile])
        kvb = _dot_tn(kb[tile], rv[tile])
        if latent:
            kvf_ref[u] = kvf
            kvb_ref[u] = kvb
        else:
            for hd in range(H_RET):
                lo = hd * RET_DK
                put(stf_ref, u, (hd,), kvf[lo:lo + RET_DK, lo:lo + RET_DK])
                put(stb_ref, u, (hd,), kvb[lo:lo + RET_DK, lo:lo + RET_DK])

    dq = col(_O_DQ, DIFF_W)
    dk = col(_O_DK, DIFF_W)
    dv = col(_O_DV, DIFF_W)
    if latent:
        cos = cos_ref[...].reshape(rows, DIFF_W)
        sin = sin_ref[...].reshape(rows, DIFF_W)
        dq = dq * cos + _swap16(dq) * sin
        dk = dk * cos + _swap16(dk) * sin
    else:
        for u in range(PROJ_TILES):
            put(dk32_ref, u, (), dk[u * TILE:(u + 1) * TILE])
            put(dv32_ref, u, (), dv[u * TILE:(u + 1) * TILE])
    store(dq_ref, (dq * QK_SCALE_LOG2).astype(BF))
    store(dk_ref, dk.astype(BF))
    store(dv_ref, _augment_v(dv.astype(BF)))

    store(cb_ref, col(_O_CB, CONV_CH))
    store(cch_ref, col(_O_CC, CONV_CH) * col(_O_CH, CONV_CH))


def _proj(x, mod, layer, row_fn, w_in_b, lg_lanes, rope_tabs=None, stacked=None):
    S, T, _ = x.shape
    n = T // TILE
    G = PROJ_TILES
    tiles = S * n
    latent = rope_tabs is not None
    assert tiles % G == 0 and (n % G == 0 or n == 1)

    def tok(w):
        return pl.BlockSpec((G, TILE, w), lambda t: (t, 0, 0))

    def tok_shape(w, dt):
        return jax.ShapeDtypeStruct((tiles, TILE, w), dt)

    in_specs = [
        tok(D_MODEL),
        pl.BlockSpec((None, None, 1, 6 * D_MODEL), lambda t: (layer, row_fn(t * G // n), 0, 0)),
        pl.BlockSpec((None, D_MODEL, IN_WIDTH), lambda t: (layer, 0, 0)),
        pl.BlockSpec((2, RET_W), lambda t: (0, 0)),
    ]
    args = [x.reshape(tiles, TILE, D_MODEL), mod, w_in_b, lg_lanes]
    out_specs = [tok(RET_W)] * 4 + [tok(DIFF_W)] * 2 + [tok(V_AUG_W)] + [tok(CONV_CH)] * 2
    out_shape = ([tok_shape(RET_W, BF)] * 3 + [tok_shape(RET_W, F32)] + [tok_shape(DIFF_W, BF)] * 2
                 + [tok_shape(V_AUG_W, BF)]
                 + [tok_shape(CONV_CH, F32)] * 2)
    aliases = {}
    if latent:
        in_specs += [pl.BlockSpec((G, TILE, DIFF_W), lambda t: (t % (n // G), 0, 0))] * 2
        args += [tab.reshape(n, TILE, DIFF_W) for tab in rope_tabs]
        out_specs += [pl.BlockSpec((G, RET_W, RET_W), lambda t: (t, 0, 0))] * 2
        out_shape += [jax.ShapeDtypeStruct((tiles, RET_W, RET_W), F32)] * 2
    else:
        assert n == 1
        kv_shape = jax.ShapeDtypeStruct((S, DEPTH, T, DIFF_W), F32)
        st_shape = jax.ShapeDtypeStruct((S, DEPTH, H_RET, RET_DK, RET_DK), F32)
        if stacked is not None:
            in_specs += [_HBM] * 4
            args += list(stacked)
            aliases = {len(args) - 4 + j: len(out_shape) + j for j in range(4)}
            out_specs += [pl.BlockSpec((G, None, TILE, DIFF_W), lambda t: (t, layer, 0, 0))] * 2
            out_specs += [pl.BlockSpec((G, None, H_RET, RET_DK, RET_DK), lambda t: (t, layer, 0, 0, 0))] * 2
        else:
            out_specs += [pl.BlockSpec((G, DEPTH, TILE, DIFF_W), lambda t: (t, 0, 0, 0))] * 2
            out_specs += [pl.BlockSpec((G, DEPTH, H_RET, RET_DK, RET_DK), lambda t: (t, 0, 0, 0, 0))] * 2
        out_shape += [kv_shape, kv_shape, st_shape, st_shape]
    outs = pl.pallas_call(
        functools.partial(_proj_kernel, latent=latent, n_aliased=len(aliases), layer=layer),
        grid=(tiles // G,),
        in_specs=in_specs,
        out_specs=out_specs,
        out_shape=out_shape,
        input_output_aliases=aliases,
        compiler_params=_params(("parallel",)),
        name="proj_lat" if latent else "proj_ctx",
    )(*args)
    token_outs = [o.reshape(S, T, o.shape[-1]) for o in outs[:9]]
    if latent:
        return token_outs + [o.reshape(S, n, RET_W, RET_W) for o in outs[9:]]
    return token_outs + list(outs[9:])


def _scan_kernel(kvf_ref, kvb_ref, s0f_ref, s0b_ref, lg_ref, sf_ref, sb_ref, *, n):
    lg = _log_sigmoid(lg_ref[...])
    dec = jnp.exp(float(TILE) * lg)
    same_head = (_iota((RET_W, RET_W), 0) >> 6) == (_iota((RET_W, RET_W), 1) >> 6)

    s = jnp.where(same_head, s0f_ref[...], 0.0)
    for c in range(n):
        sf_ref[c] = s
        s = s * dec[0:1] + jnp.where(same_head, kvf_ref[c], 0.0)
    s = jnp.where(same_head, s0b_ref[...], 0.0)
    for c in reversed(range(n)):
        sb_ref[c] = s
        s = s * dec[1:2] + jnp.where(same_head, kvb_ref[c], 0.0)


def _scan(kvf, kvb, s0f, s0b, lg_lanes):
    S, n = kvf.shape[:2]
    chunks = pl.BlockSpec((None, n, RET_W, RET_W), lambda s: (s, 0, 0, 0))
    one = pl.BlockSpec((None, RET_W, RET_W), lambda s: (s, 0, 0))
    return pl.pallas_call(
        functools.partial(_scan_kernel, n=n),
        grid=(S,),
        in_specs=[chunks, chunks, one, one, pl.BlockSpec((2, RET_W), lambda s: (0, 0))],
        out_specs=[chunks, chunks],
        out_shape=[jax.ShapeDtypeStruct((S, n, RET_W, RET_W), F32)] * 2,
        compiler_params=_params(("parallel",)),
        name="ret_scan",
    )(kvf, kvb, s0f, s0b, lg_lanes)


def _diff_heads(dq_ref, kv_refs, heads, lam):
    lane = _iota((1, DIFF_V), 1)
    scores = []
    for hd in heads:
        lo = hd * DIFF_V
        q_h = dq_ref[:, lo:lo + DIFF_V]
        qs = jnp.concatenate([jnp.where(lane < DIFF_QK, q_h, 0), jnp.where(lane >= DIFF_QK, q_h, 0)], axis=0)
        scores.append(jnp.concatenate([_dot_nt(qs, k[:, lo:lo + DIFF_V]) for k, _ in kv_refs], axis=1))
    s = jnp.concatenate(scores, axis=0)
    p = jnp.exp2(s - jnp.max(s, axis=-1, keepdims=True))

    def times_v(w, hd, width):
        acc = None
        start = 0
        for k, v in kv_refs:
            part = _dot(w[:, start:start + k.shape[0]], v[:, hd * 2 * DIFF_V:hd * 2 * DIFF_V + width])
            acc = part if acc is None else acc + part
            start += k.shape[0]
        return acc

    outs = []
    p = p.astype(BF)
    for n, hd in enumerate(heads):
        acc = times_v(p[n * 2 * TILE:(n + 1) * 2 * TILE], hd, 2 * DIFF_V)
        o0 = acc[0:TILE, 0:DIFF_V] * (1.0 / acc[0:TILE, DIFF_V:DIFF_V + 1])
        o1 = acc[TILE:, 0:DIFF_V] * (lam / acc[TILE:, DIFF_V:DIFF_V + 1])
        outs.append(o0 - o1)
    return outs


N_ROUTE_IN = 8
N_ROUTE_OUT = 4


def _mixer_kernel(*refs, paths, lam_init):
    t = pl.program_id(0)
    n_in = [_n_token_mix_refs(cached) + N_ROUTE_IN for _, _, cached in paths]
    in_end = sum(n_in)
    out_end = in_end + N_ROUTE_OUT * len(paths)
    views = []
    for k, (n_tiles, n_total, cached) in enumerate(paths):
        lo = sum(n_in[:k])
        n_a = _n_token_mix_refs(cached)
        views.append(dict(
            mix_in=refs[lo:lo + n_a], route_in=refs[lo + n_a:lo + n_in[k]],
            outs=refs[in_end + N_ROUTE_OUT * k:in_end + N_ROUTE_OUT * (k + 1)],
            mix_even=refs[out_end + 3 * k], mix_odd=refs[out_end + 3 * k + 1], decay=refs[out_end + 3 * k + 2],
            tile_in_seq=lax.rem(jnp.minimum(t, n_total - 1), n_tiles), n_tiles=n_tiles, cached=cached))

    @pl.when(t == 0)
    def _():
        for v in views:
            v["mix_odd"][...] = jnp.zeros_like(v["mix_odd"])
            lg = _log_sigmoid(v["mix_in"][-5][...])
            dist = (_iota((TILE, TILE), 0) - _iota((TILE, TILE), 1)).astype(F32)
            diag2 = jnp.where(dist == 0.0, 2.0, 1.0)
            for hd in range(H_RET):
                lgf = lg[0:1, hd * RET_DK:hd * RET_DK + 1]
                lgb = lg[1:2, hd * RET_DK:hd * RET_DK + 1]
                v["decay"][hd] = jnp.exp(jnp.abs(dist) * jnp.where(dist > 0.0, lgf, lgb)) * diag2

    def step(write, read):
        order = sorted(views, key=lambda v: not v["cached"])
        stages = [_token_mix(*v["mix_in"], v["decay"], tile_in_seq=v["tile_in_seq"], n_tiles=v["n_tiles"],
                             lam_init=lam_init, cached=v["cached"]) for v in order]
        stages += [_project_route(v[read][...], *v["route_in"], *v["outs"]) for v in order]
        for v, mixed in zip(order, _interleave(stages)):
            v[write][...] = mixed

    @pl.when(lax.rem(t, 2) == 0)
    def _():
        step("mix_even", "mix_odd")

    @pl.when(lax.rem(t, 2) == 1)
    def _():
        step("mix_odd", "mix_even")


def _n_token_mix_refs(cached):
    return 18 + (2 if cached else 0)


def _interleave(stage_generators):
    results = [None] * len(stage_generators)
    live = list(enumerate(stage_generators))
    while live:
        still = []
        for k, gen in live:
            try:
                next(gen)
                still.append((k, gen))
            except StopIteration as done:
                results[k] = done.value
        live = still
    return results


def _token_mix(*refs, tile_in_seq, n_tiles, lam_init, cached):
    rq_ref, rk_ref, rv_ref, rg_ref, sf_ref, sb_ref, dq_ref = refs[:7]
    pos = 7
    kv_refs = []
    if cached:
        kv_refs.append((refs[pos], refs[pos + 1]))
        pos += 2
    kv_refs.append((refs[pos], refs[pos + 1]))
    pos += 2
    (cb_ref, cch_ref, cprev_ref, cnext_ref, lg_ref, lamp_ref, subg_ref, cw_ref, cbias_ref,
     decay_ref) = refs[pos:pos + 10]
    i = tile_in_seq

    lg = _log_sigmoid(lg_ref[...])
    head_of_lane = _iota((1, RET_W), 1) >> 6
    q = rq_ref[...]
    k = rk_ref[...]
    v = rv_ref[...]
    ret_o = jnp.zeros((TILE, RET_W), F32)
    for hd in range(H_RET):
        in_head = head_of_lane == hd
        sc = _dot_nt(jnp.where(in_head, q, 0), k)
        ret_o = ret_o + jnp.where(in_head, _dot((sc * decay_ref[hd]).astype(BF), v), 0.0)
    p = _iota((TILE, 1), 0).astype(F32)
    ret_o = ret_o + _dot(q, sf_ref[...].astype(BF)) * jnp.exp((p + 1.0) * lg[0:1])
    ret_o = ret_o + _dot(q, sb_ref[...].astype(BF)) * jnp.exp((float(TILE) - p) * lg[1:2])
    avg = jnp.where((_iota((RET_W, RET_W), 0) >> 6) == (_iota((RET_W, RET_W), 1) >> 6),
                    1.0 / RET_DK, 0.0).astype(BF)
    rc = ret_o - _dot_hl(ret_o, avg)
    ret = rc * lax.rsqrt(_dot_hl(rc * rc, avg) + EPS) * _silu(rg_ref[...])
    yield

    lp = lamp_ref[...]
    lam = (jnp.exp(jnp.sum(lp[0:1] * lp[1:2], axis=-1, keepdims=True))
           - jnp.exp(jnp.sum(lp[2:3] * lp[3:4], axis=-1, keepdims=True)) + lam_init)
    subg = subg_ref[...] * (1.0 - lam_init)
    head_groups = [[hd] for hd in range(H_DIFF)] if cached else [list(range(H_DIFF))]
    heads = []
    for group in head_groups:
        for o in _diff_heads(dq_ref, kv_refs, group, lam):
            o = o * lax.rsqrt(jnp.mean(o * o, axis=-1, keepdims=True) + EPS) * subg
            heads.append(o.astype(BF))
        yield
    diff = jnp.concatenate(heads, axis=1)

    cch = cch_ref[...]
    prev = jnp.where(i > 0, cprev_ref[7:8, :], 0.0)
    nxt = jnp.where(i < n_tiles - 1, cnext_ref[0:1, :], 0.0)
    r = _iota((TILE, 1), 0)
    up = jnp.where(r == 0, prev, pltpu.roll(cch, 1, axis=0))
    dn = jnp.where(r == TILE - 1, nxt, pltpu.roll(cch, TILE - 1, axis=0))
    cw = cw_ref[...]
    conv = cb_ref[...] * (up * cw[0:1] + cch * cw[1:2] + dn * cw[2:3] + cbias_ref[...])
    return jnp.concatenate([ret.astype(BF), diff, conv.astype(BF)], axis=1)


def _project_route(mixed, x_ref, mod_ref, wout_ref, lng_ref, lnb_ref, wrh_ref, wrl_ref, br_ref,
                   x1_ref, hs_ref, slots_ref, cnt_ref):
    m = mod_ref[...]
    g1 = m[:, 2 * D_MODEL:3 * D_MODEL]
    sh2 = m[:, 3 * D_MODEL:4 * D_MODEL]
    sc2 = m[:, 4 * D_MODEL:5 * D_MODEL]
    x1 = _layer_norm(ALPHA * x_ref[...] + g1 * _dot(mixed, wout_ref[...]), lng_ref[...], lnb_ref[...])
    x1_ref[...] = x1
    yield

    h2 = x1 * (1.0 + sc2) + sh2
    h2h, h2l = _split(h2)
    wrh = wrh_ref[...]
    logits = (_dot_nt(wrh, h2h) + (_dot_nt(wrh, h2l) + _dot_nt(wrl_ref[...], h2h))) + br_ref[...]
    neg = -jnp.inf
    g_id = _iota((EXPERT_ROW0, TILE), 0)
    gl = jnp.where(g_id < N_GROUPS, logits[0:EXPERT_ROW0], neg)
    gmax = jnp.max(gl, axis=0, keepdims=True)
    g_idx = jnp.min(jnp.where(gl == gmax, g_id, N_GROUPS), axis=0, keepdims=True)
    g_w = 1.0 / jnp.sum(jnp.exp(gl - gmax), axis=0, keepdims=True)
    e_id = _iota((N_EXPERTS, TILE), 0)
    el = jnp.where((e_id >> 2) == g_idx, logits[EXPERT_ROW0:EXPERT_ROW0 + N_EXPERTS], neg)
    v1 = jnp.max(el, axis=0, keepdims=True)
    i1 = jnp.min(jnp.where(el == v1, e_id, N_EXPERTS), axis=0, keepdims=True)
    el2 = jnp.where(e_id == i1, neg, el)
    v2 = jnp.max(el2, axis=0, keepdims=True)
    i2 = jnp.min(jnp.where(el2 == v2, e_id, N_EXPERTS), axis=0, keepdims=True)
    t = jnp.exp(v2 - v1)
    w1 = g_w / (1.0 + t)
    w2 = g_w * t / (1.0 + t)
    yield

    sel1 = e_id == i1
    sel2 = e_id == i2
    sel = jnp.where(sel1 | sel2, 1.0, 0.0)
    earlier = _ones_where(_iota((TILE, TILE), 0) < _iota((TILE, TILE), 1))
    rank = _dot(sel.astype(BF), earlier)
    cnt = jnp.sum(sel, axis=1, keepdims=True).astype(I32)
    padded = ((cnt + (CHUNK - 1)) >> 4) << 4
    incl = jnp.broadcast_to(padded, (N_EXPERTS, LANES))
    e_row = _iota((N_EXPERTS, LANES), 0)
    for step in (1, 2, 4, 8):
        incl = incl + jnp.where(e_row >= step, pltpu.roll(incl, step, axis=0), 0)
    seg_off = (incl[:, 0:1] - padded).astype(F32)
    spos = seg_off + rank
    slot1 = jnp.sum(jnp.where(sel1, spos, 0.0), axis=0, keepdims=True).astype(I32)
    slot2 = jnp.sum(jnp.where(sel2, spos, 0.0), axis=0, keepdims=True).astype(I32)
    s_id = _iota((CAP, TILE), 0)
    at1 = s_id == slot1
    at2 = s_id == slot2
    yield
    hs_ref[:, 0:D_MODEL] = _dot(_ones_where(at1 | at2), h2h).astype(BF)
    w1h = w1.astype(BF).astype(F32)
    w2h = w2.astype(BF).astype(F32)
    ones = jnp.ones((TILE, LANES), BF)
    g_hi = _dot((jnp.where(at1, w1h, 0.0) + jnp.where(at2, w2h, 0.0)).astype(BF), ones)
    g_lo = _dot((jnp.where(at1, w1 - w1h, 0.0) + jnp.where(at2, w2 - w2h, 0.0)).astype(BF), ones)
    lane = _iota((CAP, LANES), 1)
    hs_ref[:, D_MODEL:] = jnp.where(lane == 0, g_hi, jnp.where(lane == 1, g_lo, 0.0)).astype(BF)
    cnt_ref[...] = jnp.broadcast_to(padded, (N_EXPERTS, LANES))

    dr = _iota((LANES, TILE), 0)
    digits = jnp.where(dr == 0, slot1 & (SLOT_RADIX - 1), jnp.where(dr == 1, slot1 >> 5,
             jnp.where(dr == 2, slot2 & (SLOT_RADIX - 1), jnp.where(dr == 3, slot2 >> 5, 0))))
    eye = _ones_where(_iota((TILE, TILE), 0) == _iota((TILE, TILE), 1))
    cols = _dot_nt(eye, digits.astype(F32).astype(BF))
    s1c = (cols[:, 0:1] + SLOT_RADIX * cols[:, 1:2]).astype(I32)
    s2c = (cols[:, 2:3] + SLOT_RADIX * cols[:, 3:4]).astype(I32)
    tl = _iota((TILE, LANES), 1)
    slots_ref[...] = jnp.where(tl == 0, s1c, jnp.where(tl == 1, s2c, 0))


def _mixer_io(x, mod, layer, row_fn, pr, states, cache_kv, new_kv, w_out_b, lg_lanes, lamp, subg, cw, cbias,
              lng, lnb, wrh, wrl, br):
    S, T, _ = x.shape
    n = T // TILE
    rq, rk, rv, rg, dq, cb, cch = pr
    t8 = TILE // 8
    cached = cache_kv is not None
    last = S * n - 1

    def at_mix(t):
        return jnp.minimum(t, last) // n, jnp.minimum(t, last) % n

    def at_out(t):
        return jnp.clip(t - 1, 0, last) // n, jnp.clip(t - 1, 0, last) % n

    def tok(w, at):
        return pl.BlockSpec((None, TILE, w), lambda t: (*at(t), 0))

    def full(shape):
        return pl.BlockSpec(shape, lambda t: (0,) * len(shape))

    def seq(a):
        mode = pl.Buffered(1) if n > 1 else None
        return pl.BlockSpec((None,) + a.shape[1:], lambda t: (at_mix(t)[0], 0, 0), pipeline_mode=mode)

    def halo(offset):
        def index(t):
            s, i = at_mix(t)
            return (s, jnp.clip(i * t8 + offset, 0, T // 8 - 1), 0)
        return pl.BlockSpec((None, 8, CONV_CH), index)

    if states is None:
        zero_state = jnp.zeros((1, 1, RET_W, RET_W), F32)
        states = (zero_state, zero_state)
        state = pl.BlockSpec((None, None, RET_W, RET_W), lambda t: (0, 0, 0, 0))
    else:
        state = pl.BlockSpec((None, None, RET_W, RET_W), lambda t: (*at_mix(t), 0, 0))

    in_specs = [tok(RET_W, at_mix)] * 4 + [state, state, tok(DIFF_W, at_mix)]
    args = [rq, rk, rv, rg, states[0], states[1], dq]
    for kv in ([cache_kv] if cached else []) + [new_kv]:
        in_specs += [seq(kv[0]), seq(kv[1])]
        args += list(kv)
    in_specs += [tok(CONV_CH, at_mix), tok(CONV_CH, at_mix), halo(-1), halo(t8),
                 full((2, RET_W)), full((4, DIFF_QK)), full((1, DIFF_V)), full((3, CONV_CH)), full((1, CONV_CH))]
    args += [cb, cch, cch, cch, lg_lanes, lamp, subg, cw, cbias]
    assert len(args) == _n_token_mix_refs(cached)
    in_specs += [tok(D_MODEL, at_out),
                 pl.BlockSpec((None, None, 1, 6 * D_MODEL), lambda t: (layer, row_fn(at_out(t)[0]), 0, 0)),
                 pl.BlockSpec((None, D_MODEL, D_MODEL), lambda t: (layer, 0, 0), pipeline_mode=pl.Buffered(1)),
                 full((1, D_MODEL)), full((1, D_MODEL)),
                 full((ROUTER_ROWS, D_MODEL)), full((ROUTER_ROWS, D_MODEL)), full((ROUTER_ROWS, 1))]
    args += [x, mod, w_out_b, lng, lnb, wrh, wrl, br]

    def out_tile(t):
        s, i = at_out(t)
        return s * n + i

    return dict(
        path=(n, S * n, cached), args=args, in_specs=in_specs,
        out_specs=[tok(D_MODEL, at_out),
                   pl.BlockSpec((CAP, HS_W), lambda t: (out_tile(t), 0)),
                   tok(LANES, at_out),
                   pl.BlockSpec((None, None, N_EXPERTS, LANES), lambda t: (*at_out(t), 0, 0))],
        out_shape=[jax.ShapeDtypeStruct((S, T, D_MODEL), F32),
                   jax.ShapeDtypeStruct((S * n * CAP, HS_W), BF),
                   jax.ShapeDtypeStruct((S, T, LANES), I32),
                   jax.ShapeDtypeStruct((S, n, N_EXPERTS, LANES), I32)],
        scratch=[pltpu.VMEM((TILE, D_MODEL), BF), pltpu.VMEM((TILE, D_MODEL), BF),
                 pltpu.VMEM((H_RET, TILE, TILE), F32)])


def _mixers(ios, lam_init):
    steps = max(io["path"][1] for io in ios) + 1
    outs = pl.pallas_call(
        functools.partial(_mixer_kernel, paths=tuple(io["path"] for io in ios), lam_init=lam_init),
        grid=(steps,),
        in_specs=[s for io in ios for s in io["in_specs"]],
        out_specs=[s for io in ios for s in io["out_specs"]],
        out_shape=[s for io in ios for s in io["out_shape"]],
        scratch_shapes=[s for io in ios for s in io["scratch"]],
        compiler_params=_params(("arbitrary",)),
        name="mixer",
    )(*[a for io in ios for a in io["args"]])
    return [outs[N_ROUTE_OUT * k:N_ROUTE_OUT * (k + 1)] for k in range(len(ios))]


def _block_schedule(padded, chunks_a):
    NT = padded.shape[0]
    n_chunks = NT * CHUNKS_PER_TILE
    n_blocks = n_chunks // CPB + N_EXPERTS
    c16 = padded // CHUNK
    ends = jnp.cumsum(c16, axis=1)
    before = jnp.cumsum(c16, axis=0) - c16
    per_expert = jnp.sum(c16, axis=0)
    nb = (per_expert + (CPB - 1)) // CPB
    b_end = jnp.cumsum(nb)
    q = jnp.arange(CHUNKS_PER_TILE, dtype=I32)
    e = jnp.arange(N_EXPERTS, dtype=I32)
    key = jnp.sum((ends[:, None, :] <= q[None, :, None]).astype(I32), axis=-1)
    pos_e = ((b_end - nb) * CPB + before)[:, None, :] + (q[None, :, None] - (ends - c16)[:, None, :])
    pos = jnp.sum(jnp.where(key[..., None] == e, pos_e, 0), axis=-1)
    pos = jnp.where(key < N_EXPERTS, pos, -1).reshape(-1)
    match = pos[None, :] == jnp.arange(n_blocks * CPB, dtype=I32)[:, None]
    cid = jnp.sum(jnp.where(match, jnp.arange(n_chunks, dtype=I32)[None, :], 0), axis=-1)
    nv = jnp.sum(match.astype(I32).reshape(n_blocks, -1), axis=-1)
    first = match & (jnp.arange(n_chunks, dtype=I32) < chunks_a)[None, :]
    nv_a = jnp.sum(first.astype(I32).reshape(n_blocks, -1), axis=-1)
    b = jnp.arange(n_blocks, dtype=I32)
    eb = jnp.minimum(jnp.sum((b_end[None, :] <= b[:, None]).astype(I32), axis=-1), N_EXPERTS - 1)
    zero = jnp.zeros((1,), I32)
    counts = jnp.concatenate([nv, zero, nv_a, zero]).astype(I32)
    return eb.astype(I32), counts, jnp.concatenate([cid, jnp.zeros((CPB,), I32)]).astype(I32)


def _moe_kernel(eb_ref, nv_ref, cid_ref, hs_a, hs_b, wg_ref, wu_ref, wd_ref, ys_a, ys_b,
                xbuf, ybuf, wg_b, wu_b, wd_b, in_sem, out_sem, *, n_blocks, chunks_a):
    b = pl.program_id(0)

    def chunk_rows(c):
        return pl.ds(pl.multiple_of(c * CHUNK, CHUNK), CHUNK)

    def gather_copy(which, c, j, slot):
        src = (hs_a, hs_b)[which]
        return pltpu.make_async_copy(src.at[chunk_rows(c)], xbuf.at[slot, chunk_rows(j)], in_sem.at[slot])

    def scatter_copy(which, c, j, slot):
        dst = (ys_a, ys_b)[which]
        return pltpu.make_async_copy(ybuf.at[slot, chunk_rows(j)],
                                     dst.at[chunk_rows(c), pl.ds(0, D_MODEL)], out_sem.at[slot])

    def for_chunks(blk, fn):
        n_a = nv_ref[n_blocks + 1 + blk]

        def body_a(j, carry):
            fn(j, 0, cid_ref[blk * CPB + j])
            return carry

        def body_b(j, carry):
            fn(j, 1, cid_ref[blk * CPB + j] - chunks_a)
            return carry
        lax.fori_loop(0, n_a, body_a, 0)
        lax.fori_loop(n_a, nv_ref[blk], body_b, 0)

    def each_chunk(blk, fn):
        n_a = nv_ref[n_blocks + 1 + blk]
        n_v = nv_ref[blk]
        for j in range(CPB):
            c = cid_ref[blk * CPB + j]

            @pl.when(j < n_a)
            def _():
                fn(j, 0, c)

            @pl.when((j >= n_a) & (j < n_v))
            def _():
                fn(j, 1, c - chunks_a)

    def wait_gathers(blk, slot):
        rows = pl.ds(0, nv_ref[blk] * CHUNK)

        @pl.when(nv_ref[blk] > 0)
        def _():
            pltpu.make_async_copy(hs_a.at[rows], xbuf.at[slot, rows], in_sem.at[slot]).wait()

    def wait_scatters(blk, slot):
        rows = pl.ds(0, nv_ref[blk] * CHUNK)

        @pl.when(nv_ref[blk] > 0)
        def _():
            pltpu.make_async_copy(ybuf.at[slot, rows], ys_a.at[rows, pl.ds(0, D_MODEL)], out_sem.at[slot]).wait()

    slot = lax.rem(b, 2)
    other = 1 - slot

    @pl.when(b == 0)
    def _():
        xbuf[...] = jnp.zeros_like(xbuf)
        for_chunks(0, lambda j, w, c: gather_copy(w, c, j, 0).start())

    @pl.when((b == 0) | (eb_ref[b] != eb_ref[jnp.maximum(b - 1, 0)]))
    def _():
        wg_b[...] = wg_ref[...].astype(BF)
        wu_b[...] = wu_ref[...].astype(BF)
        wd_b[...] = wd_ref[...].astype(BF)

    wait_gathers(b, slot)

    @pl.when(b >= 2)
    def _():
        wait_scatters(b - 2, slot)

    @pl.when(nv_ref[b] > 0)
    def _():
        each_chunk(b + 1, lambda j, w, c: gather_copy(w, c, j, other).start())
        x = xbuf[slot, :, 0:D_MODEL]
        gate = (xbuf[slot, :, D_MODEL:D_MODEL + 1].astype(F32)
                + xbuf[slot, :, D_MODEL + 1:D_MODEL + 2].astype(F32))
        hg = _dot(x, wg_b[...])
        hu = _dot(x, wu_b[...])
        act = _silu(hg) * hu * gate
        ybuf[slot] = _dot(act.astype(BF), wd_b[...]).astype(BF)
        each_chunk(b, lambda j, w, c: scatter_copy(w, c, j, slot).start())

    @pl.when(b == n_blocks - 1)
    def _():
        if n_blocks >= 2:
            wait_scatters(b - 1, other)
        wait_scatters(b, slot)


def _moe(hs_a, hs_b, eb, nv, cid, layer, w_gate, w_up, w_down):
    n_blocks = eb.shape[0]

    def w_spec(a, c):
        return pl.BlockSpec((None, None, a, c), lambda b, eb, nv, cid: (layer, eb[b], 0, 0))

    grid_spec = pltpu.PrefetchScalarGridSpec(
        num_scalar_prefetch=3,
        grid=(n_blocks,),
        in_specs=[_HBM, _HBM, w_spec(D_MODEL, D_EXPERT), w_spec(D_MODEL, D_EXPERT), w_spec(D_EXPERT, D_MODEL)],
        out_specs=[_HBM, _HBM],
        scratch_shapes=[pltpu.VMEM((2, MOE_TMB, HS_W), BF), pltpu.VMEM((2, MOE_TMB, D_MODEL), BF),
                        pltpu.VMEM((D_MODEL, D_EXPERT), BF), pltpu.VMEM((D_MODEL, D_EXPERT), BF),
                        pltpu.VMEM((D_EXPERT, D_MODEL), BF),
                        pltpu.SemaphoreType.DMA((2,)), pltpu.SemaphoreType.DMA((2,))],
    )
    return pl.pallas_call(
        functools.partial(_moe_kernel, n_blocks=n_blocks, chunks_a=hs_a.shape[0] // CHUNK),
        grid_spec=grid_spec,
        out_shape=[jax.ShapeDtypeStruct(hs_a.shape, BF), jax.ShapeDtypeStruct(hs_b.shape, BF)],
        input_output_aliases={3: 0, 4: 1},
        compiler_params=_params(("arbitrary",)),
        name="moe_sorted",
    )(eb, nv, cid, hs_a, hs_b, w_gate, w_up, w_down)


def _combine_kernel(ys_ref, slots_ref, x1_ref, mod_ref, lng_ref, lnb_ref, o_ref):
    g2 = mod_ref[:, 5 * D_MODEL:6 * D_MODEL]
    s_lane = _iota((TILE, CAP), 1)
    for u in range(COMBINE_TILES_PER_STEP):
        sl = slots_ref[u]
        pick = _ones_where((s_lane == sl[:, 0:1]) | (s_lane == sl[:, 1:2]))
        ffn = _dot(pick, ys_ref[u * CAP:(u + 1) * CAP, 0:D_MODEL])
        o_ref[u] = _layer_norm(ALPHA * x1_ref[u] + g2 * ffn, lng_ref[...], lnb_ref[...])


def _combine(ys, slots, x1, mod, layer, row_fn, lng, lnb):
    S, T, _ = x1.shape
    n = T // TILE
    U = COMBINE_TILES_PER_STEP
    assert (S * n) % U == 0 and (n % U == 0 or n == 1)

    def tok(w):
        return pl.BlockSpec((U, TILE, w), lambda t: (t, 0, 0))

    out = pl.pallas_call(
        _combine_kernel,
        grid=(S * n // U,),
        in_specs=[
            pl.BlockSpec((U * CAP, HS_W), lambda t: (t, 0)),
            tok(LANES), tok(D_MODEL),
            pl.BlockSpec((None, None, 1, 6 * D_MODEL), lambda t: (layer, row_fn(t * U // n), 0, 0)),
            pl.BlockSpec((1, D_MODEL), lambda t: (0, 0)),
            pl.BlockSpec((1, D_MODEL), lambda t: (0, 0)),
        ],
        out_specs=tok(D_MODEL),
        out_shape=jax.ShapeDtypeStruct((S * n, TILE, D_MODEL), F32),
        compiler_params=_params(("parallel",)),
        name="moe_combine",
    )(ys, slots.reshape(S * n, TILE, LANES), x1.reshape(S * n, TILE, D_MODEL), mod, lng, lnb)
    return out.reshape(S, T, D_MODEL)


def _rope_tables(n_lat):
    half = DIFF_QK // 2
    pairs = half // 2
    inv = (1.0 / (ROPE_BASE ** (np.arange(pairs, dtype=np.float32) * 2.0 / half))).astype(np.float32)
    t = np.arange(n_lat)
    ang_r = ((t // GRID_W).astype(np.float32)[:, None] * inv[None, :]).astype(np.float64)
    ang_c = ((t % GRID_W).astype(np.float32)[:, None] * inv[None, :]).astype(np.float64)
    cos = np.concatenate([np.cos(ang_r)] * 2 + [np.cos(ang_c)] * 2, axis=1)
    sin = np.concatenate([-np.sin(ang_r), np.sin(ang_r), -np.sin(ang_c), np.sin(ang_c)], axis=1)
    reps = DIFF_W // DIFF_QK
    return (jnp.asarray(np.tile(cos, (1, reps)), F32), jnp.asarray(np.tile(sin, (1, reps)), F32))


def _block_diag(s):
    S = s.shape[0]
    eye = jnp.eye(H_RET, dtype=s.dtype)
    return jnp.einsum('shdv,hg->shdgv', s, eye).reshape(S, RET_W, RET_W)


def _router_rows(w_group, b_group, w_expert, b_expert):
    pad = EXPERT_ROW0 - N_GROUPS
    tail = ROUTER_ROWS - EXPERT_ROW0 - N_EXPERTS
    w = jnp.concatenate([w_group.T, jnp.zeros((pad, D_MODEL), F32),
                         w_expert.reshape(D_MODEL, N_EXPERTS).T, jnp.zeros((tail, D_MODEL), F32)], axis=0)
    bias = jnp.concatenate([b_group, jnp.zeros((pad,), F32), b_expert.reshape(N_EXPERTS), jnp.zeros((tail,), F32)])
    hi = w.astype(BF)
    return hi, (w - hi.astype(F32)).astype(BF), bias.reshape(ROUTER_ROWS, 1)


def kernel(x_prompt, x_sample, cache_diff_k, cache_diff_v, state_ret_fwd, state_ret_bwd, c, c_ctx, w_mod, b_mod, w_in, ret_decay_logit, diff_lambda, diff_subln_g, conv_w, conv_b, w_out, ln_g, ln_b, w_router_group, b_router_group, w_router_expert, b_router_expert, w_gate, w_up, w_down):
    B, T_ctx, _ = x_prompt.shape
    Bd, T_lat, _ = x_sample.shape
    assert T_ctx == TILE and T_lat % TILE == 0 and T_lat % GRID_W == 0
    assert 1 + Bd <= MOD_ROWS

    cond = jnp.concatenate([c_ctx[None, :], c, jnp.zeros((MOD_ROWS - 1 - Bd, D_MODEL), F32)], axis=0)
    mod = _modulation(cond, w_mod, b_mod).reshape(DEPTH, MOD_ROWS, 1, 6 * D_MODEL)
    rope_tabs = _rope_tables(T_lat)
    w_in_b = w_in.astype(BF)
    w_out_b = w_out.astype(BF)

    ctx_row = lambda s: 0
    lat_row = lambda s: s + 1
    ctx_tiles = B * (T_ctx // TILE)
    lat_tiles = Bd * (T_lat // TILE)

    yp, ys = x_prompt, x_sample
    stacked = None
    for l in range(DEPTH):
        lam_init = 0.8 - 0.6 * math.exp(-0.3 * l)
        lg_lanes = jnp.repeat(ret_decay_logit[l], RET_DK, axis=1)
        wrh, wrl, br = _router_rows(w_router_group[l], b_router_group[l], w_router_expert[l], b_router_expert[l])
        shared = (w_out_b, lg_lanes, diff_lambda[l], diff_subln_g[l].reshape(1, DIFF_V), conv_w[l],
                  conv_b[l].reshape(1, CONV_CH), ln_g[l, 0].reshape(1, D_MODEL), ln_b[l, 0].reshape(1, D_MODEL),
                  wrh, wrl, br)
        ln2 = (ln_g[l, 1].reshape(1, D_MODEL), ln_b[l, 1].reshape(1, D_MODEL))

        pr = _proj(yp, mod, l, ctx_row, w_in_b, lg_lanes, stacked=stacked)
        rq, rk, rv, rg, dq, dk, dv, cb, cch = pr[:9]
        stacked = pr[9:]
        io_c = _mixer_io(yp, mod, l, ctx_row, (rq, rk, rv, rg, dq, cb, cch), None, None, (dk, dv), *shared)

        pr = _proj(ys, mod, l, lat_row, w_in_b, lg_lanes, rope_tabs=rope_tabs)
        rq, rk, rv, rg, dq, dk, dv, cb, cch, kvf, kvb = pr
        states = _scan(kvf, kvb, _block_diag(state_ret_fwd[:, l]), _block_diag(state_ret_bwd[:, l]), lg_lanes)
        cache_v = cache_diff_v[:, l].astype(BF)
        cache_kv = (cache_diff_k[:, l].reshape(Bd, -1, DIFF_W).astype(BF),
                    jnp.concatenate([cache_v, jnp.ones_like(cache_v)], axis=-1).reshape(Bd, -1, V_AUG_W))
        io_l = _mixer_io(ys, mod, l, lat_row, (rq, rk, rv, rg, dq, cb, cch), states, cache_kv, (dk, dv), *shared)
        (x1_c, hs_c, slots_c, cnt_c), (x1_l, hs_l, slots_l, cnt_l) = _mixers([io_c, io_l], lam_init)

        padded = jnp.concatenate([cnt_c.reshape(ctx_tiles, N_EXPERTS, LANES)[:, :, 0],
                                  cnt_l.reshape(lat_tiles, N_EXPERTS, LANES)[:, :, 0]], axis=0)
        eb, nv, cid = _block_schedule(padded, ctx_tiles * CHUNKS_PER_TILE)
        out_c, out_l = _moe(hs_c, hs_l, eb, nv, cid, l, w_gate, w_up, w_down)
        yp = _combine(out_c, slots_c, x1_c, mod, l, ctx_row, *ln2)
        ys = _combine(out_l, slots_l, x1_l, mod, l, lat_row, *ln2)

    new_k, new_v, st_f, st_b = stacked
    return (yp, ys, new_k.reshape(B, DEPTH, T_ctx, H_DIFF, 2, DIFF_QK),
            new_v.reshape(B, DEPTH, T_ctx, H_DIFF, DIFF_V), st_f, st_b)
```

```python
import functools
import math

import numpy as np
import jax
import jax.numpy as jnp
from jax import lax
from jax.experimental import pallas as pl
from jax.experimental.pallas import tpu as pltpu

D_MODEL = 1024
DEPTH = 2
GRID_W = 64
H_RET = 4
RET_DK = 64
RET_W = H_RET * RET_DK
H_DIFF = 4
DIFF_QK = 64
DIFF_V = 2 * DIFF_QK
DIFF_W = H_DIFF * DIFF_V
CONV_CH = 256
ROPE_BASE = 10000.0
N_GROUPS = 4
EXPERTS_PER_GROUP = 4
N_EXPERTS = N_GROUPS * EXPERTS_PER_GROUP
D_EXPERT = 512
ALPHA = (2 * DEPTH) ** 0.25
EPS = 1e-5
MOD_ROWS = 8
LANES = 128
ROUTER_ROWS = 32
EXPERT_ROW0 = 8

TILE = 256
CHUNK = 16
CAP = 2 * TILE + N_EXPERTS * CHUNK
CHUNKS_PER_TILE = CAP // CHUNK
COMBINE_TILES_PER_STEP = 2
PROJ_TILES = 2
MOE_TMB = 512
CPB = MOE_TMB // CHUNK
HS_W = D_MODEL + LANES
V_AUG_W = 2 * DIFF_W
QK_SCALE_LOG2 = (DIFF_QK ** -0.5) * math.log2(math.e)
SLOT_RADIX = 32

_O_RQ, _O_RK, _O_RV, _O_RG = 0, 256, 512, 768
_O_DQ, _O_DK, _O_DV = 1024, 1536, 2048
_O_CB, _O_CC, _O_CH = 2560, 2816, 3072
IN_WIDTH = 3328

BF = jnp.bfloat16
F32 = jnp.float32
I32 = jnp.int32

_VMEM_LIMIT = 56 * 1024 * 1024


def _dot(a, b):
    return jnp.dot(a, b, preferred_element_type=F32)


def _dot_nt(a, b):
    return lax.dot_general(a, b, (((1,), (1,)), ((), ())), preferred_element_type=F32)


def _dot_tn(a, b):
    return lax.dot_general(a, b, (((0,), (0,)), ((), ())), preferred_element_type=F32)


def _split(x):
    hi = x.astype(BF)
    lo = (x - hi.astype(F32)).astype(BF)
    return hi, lo


def _dot_hl(x, w_bf16):
    hi, lo = _split(x)
    return _dot(hi, w_bf16) + _dot(lo, w_bf16)


def _dot3(x, w_hi, w_lo):
    hi, lo = _split(x)
    return _dot(hi, w_hi) + (_dot(lo, w_hi) + _dot(hi, w_lo))


def _iota(shape, dim):
    return lax.broadcasted_iota(I32, shape, dim)


def _ones_where(cond):
    return jnp.where(cond, 1.0, 0.0).astype(BF)


def _log_sigmoid(x):
    return jnp.minimum(x, 0.0) - jnp.log1p(jnp.exp(-jnp.abs(x)))


def _silu(x):
    return x * jax.nn.sigmoid(x)


def _layer_norm(y, g, b):
    mu = jnp.mean(y, axis=-1, keepdims=True)
    yc = y - mu
    var = jnp.mean(yc * yc, axis=-1, keepdims=True)
    return yc * lax.rsqrt(var + EPS) * g + b


def _params(sem):
    return pltpu.CompilerParams(dimension_semantics=sem, vmem_limit_bytes=_VMEM_LIMIT)


_HBM = pl.BlockSpec(memory_space=pltpu.HBM)


def _mod_kernel(c_ref, w_ref, b_ref, o_ref):
    a = _silu(c_ref[...])
    w_hi, w_lo = _split(w_ref[...])
    o_ref[...] = _dot3(a, w_hi, w_lo) + b_ref[...]


def _modulation(cond, w_mod, b_mod):
    tn = 1536
    n6 = 6 * D_MODEL
    return pl.pallas_call(
        _mod_kernel,
        grid=(DEPTH, n6 // tn),
        in_specs=[
            pl.BlockSpec((MOD_ROWS, D_MODEL), lambda l, j: (0, 0)),
            pl.BlockSpec((None, D_MODEL, tn), lambda l, j: (l, 0, j)),
            pl.BlockSpec((None, 1, tn), lambda l, j: (l, 0, j)),
        ],
        out_specs=pl.BlockSpec((None, MOD_ROWS, tn), lambda l, j: (l, 0, j)),
        out_shape=jax.ShapeDtypeStruct((DEPTH, MOD_ROWS, n6), F32),
        compiler_params=_params(("parallel", "parallel")),
        name="modulation",
    )(cond, w_mod, b_mod.reshape(DEPTH, 1, n6))


def _mod_spec(layer, row_fn):
    return pl.BlockSpec((None, None, 1, 6 * D_MODEL), lambda s, i: (layer, row_fn(s), 0, 0))


def _augment_v(v):
    ones = jnp.ones((v.shape[0], DIFF_V), v.dtype)
    parts = []
    for hd in range(H_DIFF):
        parts += [v[:, hd * DIFF_V:(hd + 1) * DIFF_V], ones]
    return jnp.concatenate(parts, axis=1)


def _swap16(x):
    n = x.shape[-1]
    lane = _iota(x.shape, 1)
    return jnp.where((lane & 16) == 0, pltpu.roll(x, n - 16, axis=1), pltpu.roll(x, 16, axis=1))


def _proj_kernel(*refs, latent, n_aliased, layer):
    x_ref, mod_ref, w_ref, lg_ref = refs[:4]
    pos = 4
    if latent:
        cos_ref, sin_ref = refs[pos:pos + 2]
        pos += 2
    pos += n_aliased
    rq_ref, rk_ref, rv_ref, rg_ref, dq_ref, dk_ref, dv_ref, cb_ref, cch_ref = refs[pos:pos + 9]
    pos += 9
    if latent:
        kvf_ref, kvb_ref = refs[pos:pos + 2]
    else:
        dk32_ref, dv32_ref, stf_ref, stb_ref = refs[pos:pos + 4]
    rows = PROJ_TILES * TILE

    def put(ref, u, idx, val):
        if n_aliased:
            ref[(u,) + idx] = val
        else:
            for other in range(DEPTH):
                ref[(u, other) + idx] = val if other == layer else jnp.zeros_like(val)

    def store(ref, val):
        ref[...] = val.reshape(PROJ_TILES, TILE, val.shape[-1])

    m = mod_ref[...]
    sh1 = m[:, 0:D_MODEL]
    sc1 = m[:, D_MODEL:2 * D_MODEL]
    h = (x_ref[...].reshape(rows, D_MODEL) * (1.0 + sc1) + sh1).astype(BF)

    def col(off, width):
        return _dot(h, w_ref[:, off:off + width])

    store(rq_ref, col(_O_RQ, RET_W).astype(BF))
    rk = col(_O_RK, RET_W) * (RET_DK ** -0.5)
    store(rk_ref, rk.astype(BF))
    rv = col(_O_RV, RET_W).astype(BF)
    store(rv_ref, rv)
    store(rg_ref, col(_O_RG, RET_W))

    lg = _log_sigmoid(lg_ref[...])
    p = (_iota((rows, 1), 0) & (TILE - 1)).astype(F32)
    kf = (rk * jnp.exp((TILE - 1.0 - p) * lg[0:1])).astype(BF)
    kb = (rk * jnp.exp(p * lg[1:2])).astype(BF)
    for u in range(PROJ_TILES):
        tile = slice(u * TILE, (u + 1) * TILE)
        kvf = _dot_tn(kf[tile], rv[tile])
        kvb = _dot_tn(kb[tile], rv[tile])
        if latent:
            kvf_ref[u] = kvf
            kvb_ref[u] = kvb
        else:
            for hd in range(H_RET):
                lo = hd * RET_DK
                put(stf_ref, u, (hd,), kvf[lo:lo + RET_DK, lo:lo + RET_DK])
                put(stb_ref, u, (hd,), kvb[lo:lo + RET_DK, lo:lo + RET_DK])

    dq = col(_O_DQ, DIFF_W)
    dk = col(_O_DK, DIFF_W)
    dv = col(_O_DV, DIFF_W)
    if latent:
        cos = cos_ref[...].reshape(rows, DIFF_W)
        sin = sin_ref[...].reshape(rows, DIFF_W)
        dq = dq * cos + _swap16(dq) * sin
        dk = dk * cos + _swap16(dk) * sin
    else:
        for u in range(PROJ_TILES):
            put(dk32_ref, u, (), dk[u * TILE:(u + 1) * TILE])
            put(dv32_ref, u, (), dv[u * TILE:(u + 1) * TILE])
    store(dq_ref, (dq * QK_SCALE_LOG2).astype(BF))
    store(dk_ref, dk.astype(BF))
    store(dv_ref, _augment_v(dv.astype(BF)))

    store(cb_ref, col(_O_CB, CONV_CH))
    store(cch_ref, col(_O_CC, CONV_CH) * col(_O_CH, CONV_CH))


def _proj(x, mod, layer, row_fn, w_in_b, lg_lanes, rope_tabs=None, stacked=None):
    S, T, _ = x.shape
    n = T // TILE
    G = PROJ_TILES
    tiles = S * n
    latent = rope_tabs is not None
    assert tiles % G == 0 and (n % G == 0 or n == 1)

    def tok(w):
        return pl.BlockSpec((G, TILE, w), lambda t: (t, 0, 0))

    def tok_shape(w, dt):
        return jax.ShapeDtypeStruct((tiles, TILE, w), dt)

    in_specs = [
        tok(D_MODEL),
        pl.BlockSpec((None, None, 1, 6 * D_MODEL), lambda t: (layer, row_fn(t * G // n), 0, 0)),
        pl.BlockSpec((None, D_MODEL, IN_WIDTH), lambda t: (layer, 0, 0)),
        pl.BlockSpec((2, RET_W), lambda t: (0, 0)),
    ]
    args = [x.reshape(tiles, TILE, D_MODEL), mod, w_in_b, lg_lanes]
    out_specs = [tok(RET_W)] * 4 + [tok(DIFF_W)] * 2 + [tok(V_AUG_W)] + [tok(CONV_CH)] * 2
    out_shape = ([tok_shape(RET_W, BF)] * 3 + [tok_shape(RET_W, F32)] + [tok_shape(DIFF_W, BF)] * 2
                 + [tok_shape(V_AUG_W, BF)]
                 + [tok_shape(CONV_CH, F32)] * 2)
    aliases = {}
    if latent:
        in_specs += [pl.BlockSpec((G, TILE, DIFF_W), lambda t: (t % (n // G), 0, 0))] * 2
        args += [tab.reshape(n, TILE, DIFF_W) for tab in rope_tabs]
        out_specs += [pl.BlockSpec((G, RET_W, RET_W), lambda t: (t, 0, 0))] * 2
        out_shape += [jax.ShapeDtypeStruct((tiles, RET_W, RET_W), F32)] * 2
    else:
        assert n == 1
        kv_shape = jax.ShapeDtypeStruct((S, DEPTH, T, DIFF_W), F32)
        st_shape = jax.ShapeDtypeStruct((S, DEPTH, H_RET, RET_DK, RET_DK), F32)
        if stacked is not None:
            in_specs += [_HBM] * 4
            args += list(stacked)
            aliases = {len(args) - 4 + j: len(out_shape) + j for j in range(4)}
            out_specs += [pl.BlockSpec((G, None, TILE, DIFF_W), lambda t: (t, layer, 0, 0))] * 2
            out_specs += [pl.BlockSpec((G, None, H_RET, RET_DK, RET_DK), lambda t: (t, layer, 0, 0, 0))] * 2
        else:
            out_specs += [pl.BlockSpec((G, DEPTH, TILE, DIFF_W), lambda t: (t, 0, 0, 0))] * 2
            out_specs += [pl.BlockSpec((G, DEPTH, H_RET, RET_DK, RET_DK), lambda t: (t, 0, 0, 0, 0))] * 2
        out_shape += [kv_shape, kv_shape, st_shape, st_shape]
    outs = pl.pallas_call(
        functools.partial(_proj_kernel, latent=latent, n_aliased=len(aliases), layer=layer),
        grid=(tiles // G,),
        in_specs=in_specs,
        out_specs=out_specs,
        out_shape=out_shape,
        input_output_aliases=aliases,
        compiler_params=_params(("parallel",)),
        name="proj_lat" if latent else "proj_ctx",
    )(*args)
    token_outs = [o.reshape(S, T, o.shape[-1]) for o in outs[:9]]
    if latent:
        return token_outs + [o.reshape(S, n, RET_W, RET_W) for o in outs[9:]]
    return token_outs + list(outs[9:])


def _scan_kernel(kvf_ref, kvb_ref, s0f_ref, s0b_ref, lg_ref, sf_ref, sb_ref, *, n):
    lg = _log_sigmoid(lg_ref[...])
    dec = jnp.exp(float(TILE) * lg)
    same_head = (_iota((RET_W, RET_W), 0) >> 6) == (_iota((RET_W, RET_W), 1) >> 6)

    s = jnp.where(same_head, s0f_ref[...], 0.0)
    for c in range(n):
        sf_ref[c] = s
        s = s * dec[0:1] + jnp.where(same_head, kvf_ref[c], 0.0)
    s = jnp.where(same_head, s0b_ref[...], 0.0)
    for c in reversed(range(n)):
        sb_ref[c] = s
        s = s * dec[1:2] + jnp.where(same_head, kvb_ref[c], 0.0)


def _scan(kvf, kvb, s0f, s0b, lg_lanes):
    S, n = kvf.shape[:2]
    chunks = pl.BlockSpec((None, n, RET_W, RET_W), lambda s: (s, 0, 0, 0))
    one = pl.BlockSpec((None, RET_W, RET_W), lambda s: (s, 0, 0))
    return pl.pallas_call(
        functools.partial(_scan_kernel, n=n),
        grid=(S,),
        in_specs=[chunks, chunks, one, one, pl.BlockSpec((2, RET_W), lambda s: (0, 0))],
        out_specs=[chunks, chunks],
        out_shape=[jax.ShapeDtypeStruct((S, n, RET_W, RET_W), F32)] * 2,
        compiler_params=_params(("parallel",)),
        name="ret_scan",
    )(kvf, kvb, s0f, s0b, lg_lanes)


def _diff_heads(dq_ref, kv_refs, heads, lam):
    lane = _iota((1, DIFF_V), 1)
    scores = []
    for hd in heads:
        lo = hd * DIFF_V
        q_h = dq_ref[:, lo:lo + DIFF_V]
        qs = jnp.concatenate([jnp.where(lane < DIFF_QK, q_h, 0), jnp.where(lane >= DIFF_QK, q_h, 0)], axis=0)
        scores.append(jnp.concatenate([_dot_nt(qs, k[:, lo:lo + DIFF_V]) for k, _ in kv_refs], axis=1))
    s = jnp.concatenate(scores, axis=0)
    p = jnp.exp2(s - jnp.max(s, axis=-1, keepdims=True))

    def times_v(w, hd, width):
        acc = None
        start = 0
        for k, v in kv_refs:
            part = _dot(w[:, start:start + k.shape[0]], v[:, hd * 2 * DIFF_V:hd * 2 * DIFF_V + width])
            acc = part if acc is None else acc + part
            start += k.shape[0]
        return acc

    outs = []
    p = p.astype(BF)
    for n, hd in enumerate(heads):
        acc = times_v(p[n * 2 * TILE:(n + 1) * 2 * TILE], hd, 2 * DIFF_V)
        o0 = acc[0:TILE, 0:DIFF_V] * (1.0 / acc[0:TILE, DIFF_V:DIFF_V + 1])
        o1 = acc[TILE:, 0:DIFF_V] * (lam / acc[TILE:, DIFF_V:DIFF_V + 1])
        outs.append(o0 - o1)
    return outs


def _mixer_kernel(*refs, n_tiles, n_total, lam_init, cached):
    n_a = _n_token_mix_refs(cached)
    mix_even, mix_odd, decay_ref = refs[-3:]
    t = pl.program_id(0)
    tile_in_seq = lax.rem(jnp.minimum(t, n_total - 1), n_tiles)

    @pl.when(t == 0)
    def _():
        mix_odd[...] = jnp.zeros_like(mix_odd)
        lg = _log_sigmoid(refs[n_a - 5][...])
        dist = (_iota((TILE, TILE), 0) - _iota((TILE, TILE), 1)).astype(F32)
        diag2 = jnp.where(dist == 0.0, 2.0, 1.0)
        for hd in range(H_RET):
            lgf = lg[0:1, hd * RET_DK:hd * RET_DK + 1]
            lgb = lg[1:2, hd * RET_DK:hd * RET_DK + 1]
            decay_ref[hd] = jnp.exp(jnp.abs(dist) * jnp.where(dist > 0.0, lgf, lgb)) * diag2

    def step(mix_write, mix_read):
        _project_route(mix_read[...], *refs[n_a:-3])
        mix_write[...] = _token_mix(*refs[:n_a], decay_ref, tile_in_seq=tile_in_seq, n_tiles=n_tiles,
                                    lam_init=lam_init, cached=cached)

    @pl.when(lax.rem(t, 2) == 0)
    def _():
        step(mix_even, mix_odd)

    @pl.when(lax.rem(t, 2) == 1)
    def _():
        step(mix_odd, mix_even)


def _n_token_mix_refs(cached):
    return 18 + (2 if cached else 0)


def _token_mix(*refs, tile_in_seq, n_tiles, lam_init, cached):
    rq_ref, rk_ref, rv_ref, rg_ref, sf_ref, sb_ref, dq_ref = refs[:7]
    pos = 7
    kv_refs = []
    if cached:
        kv_refs.append((refs[pos], refs[pos + 1]))
        pos += 2
    kv_refs.append((refs[pos], refs[pos + 1]))
    pos += 2
    (cb_ref, cch_ref, cprev_ref, cnext_ref, lg_ref, lamp_ref, subg_ref, cw_ref, cbias_ref,
     decay_ref) = refs[pos:pos + 10]
    i = tile_in_seq

    lg = _log_sigmoid(lg_ref[...])
    head_of_lane = _iota((1, RET_W), 1) >> 6
    q = rq_ref[...]
    k = rk_ref[...]
    v = rv_ref[...]
    ret_o = jnp.zeros((TILE, RET_W), F32)
    for hd in range(H_RET):
        in_head = head_of_lane == hd
        sc = _dot_nt(jnp.where(in_head, q, 0), k)
        ret_o = ret_o + jnp.where(in_head, _dot((sc * decay_ref[hd]).astype(BF), v), 0.0)
    p = _iota((TILE, 1), 0).astype(F32)
    ret_o = ret_o + _dot(q, sf_ref[...].astype(BF)) * jnp.exp((p + 1.0) * lg[0:1])
    ret_o = ret_o + _dot(q, sb_ref[...].astype(BF)) * jnp.exp((float(TILE) - p) * lg[1:2])
    avg = jnp.where((_iota((RET_W, RET_W), 0) >> 6) == (_iota((RET_W, RET_W), 1) >> 6),
                    1.0 / RET_DK, 0.0).astype(BF)
    rc = ret_o - _dot_hl(ret_o, avg)
    ret = rc * lax.rsqrt(_dot_hl(rc * rc, avg) + EPS) * _silu(rg_ref[...])

    lp = lamp_ref[...]
    lam = (jnp.exp(jnp.sum(lp[0:1] * lp[1:2], axis=-1, keepdims=True))
           - jnp.exp(jnp.sum(lp[2:3] * lp[3:4], axis=-1, keepdims=True)) + lam_init)
    subg = subg_ref[...] * (1.0 - lam_init)
    head_groups = [[hd] for hd in range(H_DIFF)] if cached else [list(range(H_DIFF))]
    heads = []
    for group in head_groups:
        for o in _diff_heads(dq_ref, kv_refs, group, lam):
            o = o * lax.rsqrt(jnp.mean(o * o, axis=-1, keepdims=True) + EPS) * subg
            heads.append(o.astype(BF))
    diff = jnp.concatenate(heads, axis=1)

    cch = cch_ref[...]
    prev = jnp.where(i > 0, cprev_ref[7:8, :], 0.0)
    nxt = jnp.where(i < n_tiles - 1, cnext_ref[0:1, :], 0.0)
    r = _iota((TILE, 1), 0)
    up = jnp.where(r == 0, prev, pltpu.roll(cch, 1, axis=0))
    dn = jnp.where(r == TILE - 1, nxt, pltpu.roll(cch, TILE - 1, axis=0))
    cw = cw_ref[...]
    conv = cb_ref[...] * (up * cw[0:1] + cch * cw[1:2] + dn * cw[2:3] + cbias_ref[...])
    return jnp.concatenate([ret.astype(BF), diff, conv.astype(BF)], axis=1)


def _project_route(mixed, x_ref, mod_ref, wout_ref, lng_ref, lnb_ref, wrh_ref, wrl_ref, br_ref,
                   x1_ref, hs_ref, slots_ref, cnt_ref):
    m = mod_ref[...]
    g1 = m[:, 2 * D_MODEL:3 * D_MODEL]
    sh2 = m[:, 3 * D_MODEL:4 * D_MODEL]
    sc2 = m[:, 4 * D_MODEL:5 * D_MODEL]
    x1 = _layer_norm(ALPHA * x_ref[...] + g1 * _dot(mixed, wout_ref[...]), lng_ref[...], lnb_ref[...])
    x1_ref[...] = x1

    h2 = x1 * (1.0 + sc2) + sh2
    h2h, h2l = _split(h2)
    wrh = wrh_ref[...]
    logits = (_dot_nt(wrh, h2h) + (_dot_nt(wrh, h2l) + _dot_nt(wrl_ref[...], h2h))) + br_ref[...]
    neg = -jnp.inf
    g_id = _iota((EXPERT_ROW0, TILE), 0)
    gl = jnp.where(g_id < N_GROUPS, logits[0:EXPERT_ROW0], neg)
    gmax = jnp.max(gl, axis=0, keepdims=True)
    g_idx = jnp.min(jnp.where(gl == gmax, g_id, N_GROUPS), axis=0, keepdims=True)
    g_w = 1.0 / jnp.sum(jnp.exp(gl - gmax), axis=0, keepdims=True)
    e_id = _iota((N_EXPERTS, TILE), 0)
    el = jnp.where((e_id >> 2) == g_idx, logits[EXPERT_ROW0:EXPERT_ROW0 + N_EXPERTS], neg)
    v1 = jnp.max(el, axis=0, keepdims=True)
    i1 = jnp.min(jnp.where(el == v1, e_id, N_EXPERTS), axis=0, keepdims=True)
    el2 = jnp.where(e_id == i1, neg, el)
    v2 = jnp.max(el2, axis=0, keepdims=True)
    i2 = jnp.min(jnp.where(el2 == v2, e_id, N_EXPERTS), axis=0, keepdims=True)
    t = jnp.exp(v2 - v1)
    w1 = g_w / (1.0 + t)
    w2 = g_w * t / (1.0 + t)

    sel1 = e_id == i1
    sel2 = e_id == i2
    sel = jnp.where(sel1 | sel2, 1.0, 0.0)
    earlier = _ones_where(_iota((TILE, TILE), 0) < _iota((TILE, TILE), 1))
    rank = _dot(sel.astype(BF), earlier)
    cnt = jnp.sum(sel, axis=1, keepdims=True).astype(I32)
    padded = ((cnt + (CHUNK - 1)) >> 4) << 4
    incl = jnp.broadcast_to(padded, (N_EXPERTS, LANES))
    e_row = _iota((N_EXPERTS, LANES), 0)
    for step in (1, 2, 4, 8):
        incl = incl + jnp.where(e_row >= step, pltpu.roll(incl, step, axis=0), 0)
    seg_off = (incl[:, 0:1] - padded).astype(F32)
    spos = seg_off + rank
    slot1 = jnp.sum(jnp.where(sel1, spos, 0.0), axis=0, keepdims=True).astype(I32)
    slot2 = jnp.sum(jnp.where(sel2, spos, 0.0), axis=0, keepdims=True).astype(I32)
    s_id = _iota((CAP, TILE), 0)
    at1 = s_id == slot1
    at2 = s_id == slot2
    hs_ref[:, 0:D_MODEL] = _dot(_ones_where(at1 | at2), h2h).astype(BF)
    w1h = w1.astype(BF).astype(F32)
    w2h = w2.astype(BF).astype(F32)
    ones = jnp.ones((TILE, LANES), BF)
    g_hi = _dot((jnp.where(at1, w1h, 0.0) + jnp.where(at2, w2h, 0.0)).astype(BF), ones)
    g_lo = _dot((jnp.where(at1, w1 - w1h, 0.0) + jnp.where(at2, w2 - w2h, 0.0)).astype(BF), ones)
    lane = _iota((CAP, LANES), 1)
    hs_ref[:, D_MODEL:] = jnp.where(lane == 0, g_hi, jnp.where(lane == 1, g_lo, 0.0)).astype(BF)
    cnt_ref[...] = jnp.broadcast_to(padded, (N_EXPERTS, LANES))

    dr = _iota((LANES, TILE), 0)
    digits = jnp.where(dr == 0, slot1 & (SLOT_RADIX - 1), jnp.where(dr == 1, slot1 >> 5,
             jnp.where(dr == 2, slot2 & (SLOT_RADIX - 1), jnp.where(dr == 3, slot2 >> 5, 0))))
    eye = _ones_where(_iota((TILE, TILE), 0) == _iota((TILE, TILE), 1))
    cols = _dot_nt(eye, digits.astype(F32).astype(BF))
    s1c = (cols[:, 0:1] + SLOT_RADIX * cols[:, 1:2]).astype(I32)
    s2c = (cols[:, 2:3] + SLOT_RADIX * cols[:, 3:4]).astype(I32)
    tl = _iota((TILE, LANES), 1)
    slots_ref[...] = jnp.where(tl == 0, s1c, jnp.where(tl == 1, s2c, 0))


def _mixer(x, mod, layer, row_fn, pr, states, cache_kv, new_kv, w_out_b, lg_lanes, lamp, subg, cw, cbias,
           lng, lnb, wrh, wrl, br, lam_init):
    S, T, _ = x.shape
    n = T // TILE
    rq, rk, rv, rg, dq, cb, cch = pr
    t8 = TILE // 8
    cached = cache_kv is not None
    last = S * n - 1

    def at_mix(t):
        return jnp.minimum(t, last) // n, jnp.minimum(t, last) % n

    def at_out(t):
        return jnp.maximum(t - 1, 0) // n, jnp.maximum(t - 1, 0) % n

    def tok(w, at):
        return pl.BlockSpec((None, TILE, w), lambda t: (*at(t), 0))

    def full(shape):
        return pl.BlockSpec(shape, lambda t: (0,) * len(shape))

    def seq(a):
        mode = pl.Buffered(1) if n > 1 else None
        return pl.BlockSpec((None,) + a.shape[1:], lambda t: (at_mix(t)[0], 0, 0), pipeline_mode=mode)

    def halo(offset):
        def index(t):
            s, i = at_mix(t)
            return (s, jnp.clip(i * t8 + offset, 0, T // 8 - 1), 0)
        return pl.BlockSpec((None, 8, CONV_CH), index)

    if states is None:
        zero_state = jnp.zeros((1, 1, RET_W, RET_W), F32)
        states = (zero_state, zero_state)
        state = pl.BlockSpec((None, None, RET_W, RET_W), lambda t: (0, 0, 0, 0))
    else:
        state = pl.BlockSpec((None, None, RET_W, RET_W), lambda t: (*at_mix(t), 0, 0))

    in_specs = [tok(RET_W, at_mix)] * 4 + [state, state, tok(DIFF_W, at_mix)]
    args = [rq, rk, rv, rg, states[0], states[1], dq]
    for kv in ([cache_kv] if cached else []) + [new_kv]:
        in_specs += [seq(kv[0]), seq(kv[1])]
        args += list(kv)
    in_specs += [tok(CONV_CH, at_mix), tok(CONV_CH, at_mix), halo(-1), halo(t8),
                 full((2, RET_W)), full((4, DIFF_QK)), full((1, DIFF_V)), full((3, CONV_CH)), full((1, CONV_CH))]
    args += [cb, cch, cch, cch, lg_lanes, lamp, subg, cw, cbias]
    assert len(args) == _n_token_mix_refs(cached)
    in_specs += [tok(D_MODEL, at_out),
                 pl.BlockSpec((None, None, 1, 6 * D_MODEL), lambda t: (layer, row_fn(at_out(t)[0]), 0, 0)),
                 pl.BlockSpec((None, D_MODEL, D_MODEL), lambda t: (layer, 0, 0)),
                 full((1, D_MODEL)), full((1, D_MODEL)),
                 full((ROUTER_ROWS, D_MODEL)), full((ROUTER_ROWS, D_MODEL)), full((ROUTER_ROWS, 1))]
    args += [x, mod, w_out_b, lng, lnb, wrh, wrl, br]

    def out_tile(t):
        s, i = at_out(t)
        return s * n + i

    return pl.pallas_call(
        functools.partial(_mixer_kernel, n_tiles=n, n_total=S * n, lam_init=lam_init, cached=cached),
        grid=(S * n + 1,),
        in_specs=in_specs,
        out_specs=[tok(D_MODEL, at_out),
                   pl.BlockSpec((CAP, HS_W), lambda t: (out_tile(t), 0)),
                   tok(LANES, at_out),
                   pl.BlockSpec((None, None, N_EXPERTS, LANES), lambda t: (*at_out(t), 0, 0))],
        out_shape=[jax.ShapeDtypeStruct((S, T, D_MODEL), F32),
                   jax.ShapeDtypeStruct((S * n * CAP, HS_W), BF),
                   jax.ShapeDtypeStruct((S, T, LANES), I32),
                   jax.ShapeDtypeStruct((S, n, N_EXPERTS, LANES), I32)],
        scratch_shapes=[pltpu.VMEM((TILE, D_MODEL), BF), pltpu.VMEM((TILE, D_MODEL), BF),
                        pltpu.VMEM((H_RET, TILE, TILE), F32)],
        compiler_params=_params(("arbitrary",)),
        name="mixer_lat" if cached else "mixer_ctx",
    )(*args)


def _block_schedule(padded, chunks_a):
    NT = padded.shape[0]
    n_chunks = NT * CHUNKS_PER_TILE
    n_blocks = n_chunks // CPB + N_EXPERTS
    c16 = padded // CHUNK
    ends = jnp.cumsum(c16, axis=1)
    before = jnp.cumsum(c16, axis=0) - c16
    per_expert = jnp.sum(c16, axis=0)
    nb = (per_expert + (CPB - 1)) // CPB
    b_end = jnp.cumsum(nb)
    q = jnp.arange(CHUNKS_PER_TILE, dtype=I32)
    e = jnp.arange(N_EXPERTS, dtype=I32)
    key = jnp.sum((ends[:, None, :] <= q[None, :, None]).astype(I32), axis=-1)
    pos_e = ((b_end - nb) * CPB + before)[:, None, :] + (q[None, :, None] - (ends - c16)[:, None, :])
    pos = jnp.sum(jnp.where(key[..., None] == e, pos_e, 0), axis=-1)
    pos = jnp.where(key < N_EXPERTS, pos, -1).reshape(-1)
    match = pos[None, :] == jnp.arange(n_blocks * CPB, dtype=I32)[:, None]
    cid = jnp.sum(jnp.where(match, jnp.arange(n_chunks, dtype=I32)[None, :], 0), axis=-1)
    nv = jnp.sum(match.astype(I32).reshape(n_blocks, -1), axis=-1)
    first = match & (jnp.arange(n_chunks, dtype=I32) < chunks_a)[None, :]
    nv_a = jnp.sum(first.astype(I32).reshape(n_blocks, -1), axis=-1)
    b = jnp.arange(n_blocks, dtype=I32)
    eb = jnp.minimum(jnp.sum((b_end[None, :] <= b[:, None]).astype(I32), axis=-1), N_EXPERTS - 1)
    zero = jnp.zeros((1,), I32)
    counts = jnp.concatenate([nv, zero, nv_a, zero]).astype(I32)
    return eb.astype(I32), counts, jnp.concatenate([cid, jnp.zeros((CPB,), I32)]).astype(I32)


def _moe_kernel(eb_ref, nv_ref, cid_ref, hs_a, hs_b, wg_ref, wu_ref, wd_ref, ys_a, ys_b,
                xbuf, ybuf, wg_b, wu_b, wd_b, in_sem, out_sem, *, n_blocks, chunks_a):
    b = pl.program_id(0)

    def chunk_rows(c):
        return pl.ds(pl.multiple_of(c * CHUNK, CHUNK), CHUNK)

    def gather_copy(which, c, j, slot):
        src = (hs_a, hs_b)[which]
        return pltpu.make_async_copy(src.at[chunk_rows(c)], xbuf.at[slot, chunk_rows(j)], in_sem.at[slot])

    def scatter_copy(which, c, j, slot):
        dst = (ys_a, ys_b)[which]
        return pltpu.make_async_copy(ybuf.at[slot, chunk_rows(j)],
                                     dst.at[chunk_rows(c), pl.ds(0, D_MODEL)], out_sem.at[slot])

    def for_chunks(blk, fn):
        n_a = nv_ref[n_blocks + 1 + blk]

        def body_a(j, carry):
            fn(j, 0, cid_ref[blk * CPB + j])
            return carry

        def body_b(j, carry):
            fn(j, 1, cid_ref[blk * CPB + j] - chunks_a)
            return carry
        lax.fori_loop(0, n_a, body_a, 0)
        lax.fori_loop(n_a, nv_ref[blk], body_b, 0)

    def each_chunk(blk, fn):
        n_a = nv_ref[n_blocks + 1 + blk]
        n_v = nv_ref[blk]
        for j in range(CPB):
            c = cid_ref[blk * CPB + j]

            @pl.when(j < n_a)
            def _():
                fn(j, 0, c)

            @pl.when((j >= n_a) & (j < n_v))
            def _():
                fn(j, 1, c - chunks_a)

    def wait_gathers(blk, slot):
        rows = pl.ds(0, nv_ref[blk] * CHUNK)

        @pl.when(nv_ref[blk] > 0)
        def _():
            pltpu.make_async_copy(hs_a.at[rows], xbuf.at[slot, rows], in_sem.at[slot]).wait()

    def wait_scatters(blk, slot):
        rows = pl.ds(0, nv_ref[blk] * CHUNK)

        @pl.when(nv_ref[blk] > 0)
        def _():
            pltpu.make_async_copy(ybuf.at[slot, rows], ys_a.at[rows, pl.ds(0, D_MODEL)], out_sem.at[slot]).wait()

    slot = lax.rem(b, 2)
    other = 1 - slot

    @pl.when(b == 0)
    def _():
        xbuf[...] = jnp.zeros_like(xbuf)
        for_chunks(0, lambda j, w, c: gather_copy(w, c, j, 0).start())

    @pl.when((b == 0) | (eb_ref[b] != eb_ref[jnp.maximum(b - 1, 0)]))
    def _():
        wg_b[...] = wg_ref[...].astype(BF)
        wu_b[...] = wu_ref[...].astype(BF)
        wd_b[...] = wd_ref[...].astype(BF)

    wait_gathers(b, slot)

    @pl.when(b >= 2)
    def _():
        wait_scatters(b - 2, slot)

    @pl.when(nv_ref[b] > 0)
    def _():
        each_chunk(b + 1, lambda j, w, c: gather_copy(w, c, j, other).start())
        x = xbuf[slot, :, 0:D_MODEL]
        gate = (xbuf[slot, :, D_MODEL:D_MODEL + 1].astype(F32)
                + xbuf[slot, :, D_MODEL + 1:D_MODEL + 2].astype(F32))
        hg = _dot(x, wg_b[...])
        hu = _dot(x, wu_b[...])
        act = _silu(hg) * hu * gate
        ybuf[slot] = _dot(act.astype(BF), wd_b[...]).astype(BF)
        each_chunk(b, lambda j, w, c: scatter_copy(w, c, j, slot).start())

    @pl.when(b == n_blocks - 1)
    def _():
        if n_blocks >= 2:
            wait_scatters(b - 1, other)
        wait_scatters(b, slot)


def _moe(hs_a, hs_b, eb, nv, cid, layer, w_gate, w_up, w_down):
    n_blocks = eb.shape[0]

    def w_spec(a, c):
        return pl.BlockSpec((None, None, a, c), lambda b, eb, nv, cid: (layer, eb[b], 0, 0))

    grid_spec = pltpu.PrefetchScalarGridSpec(
        num_scalar_prefetch=3,
        grid=(n_blocks,),
        in_specs=[_HBM, _HBM, w_spec(D_MODEL, D_EXPERT), w_spec(D_MODEL, D_EXPERT), w_spec(D_EXPERT, D_MODEL)],
        out_specs=[_HBM, _HBM],
        scratch_shapes=[pltpu.VMEM((2, MOE_TMB, HS_W), BF), pltpu.VMEM((2, MOE_TMB, D_MODEL), BF),
                        pltpu.VMEM((D_MODEL, D_EXPERT), BF), pltpu.VMEM((D_MODEL, D_EXPERT), BF),
                        pltpu.VMEM((D_EXPERT, D_MODEL), BF),
                        pltpu.SemaphoreType.DMA((2,)), pltpu.SemaphoreType.DMA((2,))],
    )
    return pl.pallas_call(
        functools.partial(_moe_kernel, n_blocks=n_blocks, chunks_a=hs_a.shape[0] // CHUNK),
        grid_spec=grid_spec,
        out_shape=[jax.ShapeDtypeStruct(hs_a.shape, BF), jax.ShapeDtypeStruct(hs_b.shape, BF)],
        input_output_aliases={3: 0, 4: 1},
        compiler_params=_params(("arbitrary",)),
        name="moe_sorted",
    )(eb, nv, cid, hs_a, hs_b, w_gate, w_up, w_down)


def _combine_kernel(ys_ref, slots_ref, x1_ref, mod_ref, lng_ref, lnb_ref, o_ref):
    g2 = mod_ref[:, 5 * D_MODEL:6 * D_MODEL]
    s_lane = _iota((TILE, CAP), 1)
    for u in range(COMBINE_TILES_PER_STEP):
        sl = slots_ref[u]
        pick = _ones_where((s_lane == sl[:, 0:1]) | (s_lane == sl[:, 1:2]))
        ffn = _dot(pick, ys_ref[u * CAP:(u + 1) * CAP, 0:D_MODEL])
        o_ref[u] = _layer_norm(ALPHA * x1_ref[u] + g2 * ffn, lng_ref[...], lnb_ref[...])


def _combine(ys, slots, x1, mod, layer, row_fn, lng, lnb):
    S, T, _ = x1.shape
    n = T // TILE
    U = COMBINE_TILES_PER_STEP
    assert (S * n) % U == 0 and (n % U == 0 or n == 1)

    def tok(w):
        return pl.BlockSpec((U, TILE, w), lambda t: (t, 0, 0))

    out = pl.pallas_call(
        _combine_kernel,
        grid=(S * n // U,),
        in_specs=[
            pl.BlockSpec((U * CAP, HS_W), lambda t: (t, 0)),
            tok(LANES), tok(D_MODEL),
            pl.BlockSpec((None, None, 1, 6 * D_MODEL), lambda t: (layer, row_fn(t * U // n), 0, 0)),
            pl.BlockSpec((1, D_MODEL), lambda t: (0, 0)),
            pl.BlockSpec((1, D_MODEL), lambda t: (0, 0)),
        ],
        out_specs=tok(D_MODEL),
        out_shape=jax.ShapeDtypeStruct((S * n, TILE, D_MODEL), F32),
        compiler_params=_params(("parallel",)),
        name="moe_combine",
    )(ys, slots.reshape(S * n, TILE, LANES), x1.reshape(S * n, TILE, D_MODEL), mod, lng, lnb)
    return out.reshape(S, T, D_MODEL)


def _rope_tables(n_lat):
    half = DIFF_QK // 2
    pairs = half // 2
    inv = (1.0 / (ROPE_BASE ** (np.arange(pairs, dtype=np.float32) * 2.0 / half))).astype(np.float32)
    t = np.arange(n_lat)
    ang_r = ((t // GRID_W).astype(np.float32)[:, None] * inv[None, :]).astype(np.float64)
    ang_c = ((t % GRID_W).astype(np.float32)[:, None] * inv[None, :]).astype(np.float64)
    cos = np.concatenate([np.cos(ang_r)] * 2 + [np.cos(ang_c)] * 2, axis=1)
    sin = np.concatenate([-np.sin(ang_r), np.sin(ang_r), -np.sin(ang_c), np.sin(ang_c)], axis=1)
    reps = DIFF_W // DIFF_QK
    return (jnp.asarray(np.tile(cos, (1, reps)), F32), jnp.asarray(np.tile(sin, (1, reps)), F32))


def _block_diag(s):
    S = s.shape[0]
    eye = jnp.eye(H_RET, dtype=s.dtype)
    return jnp.einsum('shdv,hg->shdgv', s, eye).reshape(S, RET_W, RET_W)


def _router_rows(w_group, b_group, w_expert, b_expert):
    pad = EXPERT_ROW0 - N_GROUPS
    tail = ROUTER_ROWS - EXPERT_ROW0 - N_EXPERTS
    w = jnp.concatenate([w_group.T, jnp.zeros((pad, D_MODEL), F32),
                         w_expert.reshape(D_MODEL, N_EXPERTS).T, jnp.zeros((tail, D_MODEL), F32)], axis=0)
    bias = jnp.concatenate([b_group, jnp.zeros((pad,), F32), b_expert.reshape(N_EXPERTS), jnp.zeros((tail,), F32)])
    hi = w.astype(BF)
    return hi, (w - hi.astype(F32)).astype(BF), bias.reshape(ROUTER_ROWS, 1)


def kernel(x_prompt, x_sample, cache_diff_k, cache_diff_v, state_ret_fwd, state_ret_bwd, c, c_ctx, w_mod, b_mod, w_in, ret_decay_logit, diff_lambda, diff_subln_g, conv_w, conv_b, w_out, ln_g, ln_b, w_router_group, b_router_group, w_router_expert, b_router_expert, w_gate, w_up, w_down):
    B, T_ctx, _ = x_prompt.shape
    Bd, T_lat, _ = x_sample.shape
    assert T_ctx == TILE and T_lat % TILE == 0 and T_lat % GRID_W == 0
    assert 1 + Bd <= MOD_ROWS

    cond = jnp.concatenate([c_ctx[None, :], c, jnp.zeros((MOD_ROWS - 1 - Bd, D_MODEL), F32)], axis=0)
    mod = _modulation(cond, w_mod, b_mod).reshape(DEPTH, MOD_ROWS, 1, 6 * D_MODEL)
    rope_tabs = _rope_tables(T_lat)
    w_in_b = w_in.astype(BF)
    w_out_b = w_out.astype(BF)

    ctx_row = lambda s: 0
    lat_row = lambda s: s + 1
    ctx_tiles = B * (T_ctx // TILE)
    lat_tiles = Bd * (T_lat // TILE)

    yp, ys = x_prompt, x_sample
    stacked = None
    for l in range(DEPTH):
        lam_init = 0.8 - 0.6 * math.exp(-0.3 * l)
        lg_lanes = jnp.repeat(ret_decay_logit[l], RET_DK, axis=1)
        wrh, wrl, br = _router_rows(w_router_group[l], b_router_group[l], w_router_expert[l], b_router_expert[l])
        shared = (w_out_b, lg_lanes, diff_lambda[l], diff_subln_g[l].reshape(1, DIFF_V), conv_w[l],
                  conv_b[l].reshape(1, CONV_CH), ln_g[l, 0].reshape(1, D_MODEL), ln_b[l, 0].reshape(1, D_MODEL),
                  wrh, wrl, br, lam_init)
        ln2 = (ln_g[l, 1].reshape(1, D_MODEL), ln_b[l, 1].reshape(1, D_MODEL))

        pr = _proj(yp, mod, l, ctx_row, w_in_b, lg_lanes, stacked=stacked)
        rq, rk, rv, rg, dq, dk, dv, cb, cch = pr[:9]
        stacked = pr[9:]
        x1_c, hs_c, slots_c, cnt_c = _mixer(yp, mod, l, ctx_row, (rq, rk, rv, rg, dq, cb, cch), None, None,
                                            (dk, dv), *shared)

        pr = _proj(ys, mod, l, lat_row, w_in_b, lg_lanes, rope_tabs=rope_tabs)
        rq, rk, rv, rg, dq, dk, dv, cb, cch, kvf, kvb = pr
        states = _scan(kvf, kvb, _block_diag(state_ret_fwd[:, l]), _block_diag(state_ret_bwd[:, l]), lg_lanes)
        cache_v = cache_diff_v[:, l].astype(BF)
        cache_kv = (cache_diff_k[:, l].reshape(Bd, -1, DIFF_W).astype(BF),
                    jnp.concatenate([cache_v, jnp.ones_like(cache_v)], axis=-1).reshape(Bd, -1, V_AUG_W))
        x1_l, hs_l, slots_l, cnt_l = _mixer(ys, mod, l, lat_row, (rq, rk, rv, rg, dq, cb, cch), states, cache_kv,
                                            (dk, dv), *shared)

        padded = jnp.concatenate([cnt_c.reshape(ctx_tiles, N_EXPERTS, LANES)[:, :, 0],
                                  cnt_l.reshape(lat_tiles, N_EXPERTS, LANES)[:, :, 0]], axis=0)
        eb, nv, cid = _block_schedule(padded, ctx_tiles * CHUNKS_PER_TILE)
        out_c, out_l = _moe(hs_c, hs_l, eb, nv, cid, l, w_gate, w_up, w_down)
        yp = _combine(out_c, slots_c, x1_c, mod, l, ctx_row, *ln2)
        ys = _combine(out_l, slots_l, x1_l, mod, l, lat_row, *ln2)

    new_k, new_v, st_f, st_b = stacked
    return (yp, ys, new_k.reshape(B, DEPTH, T_ctx, H_DIFF, 2, DIFF_QK),
            new_v.reshape(B, DEPTH, T_ctx, H_DIFF, DIFF_V), st_f, st_b)
```

```python
import functools
import math

import numpy as np
import jax
import jax.numpy as jnp
from jax import lax
from jax.experimental import pallas as pl
from jax.experimental.pallas import tpu as pltpu

D_MODEL = 1024
DEPTH = 2
GRID_W = 64
H_RET = 4
RET_DK = 64
RET_W = H_RET * RET_DK
H_DIFF = 4
DIFF_QK = 64
DIFF_V = 2 * DIFF_QK
DIFF_W = H_DIFF * DIFF_V
CONV_CH = 256
ROPE_BASE = 10000.0
N_GROUPS = 4
EXPERTS_PER_GROUP = 4
N_EXPERTS = N_GROUPS * EXPERTS_PER_GROUP
D_EXPERT = 512
ALPHA = (2 * DEPTH) ** 0.25
EPS = 1e-5
MOD_ROWS = 8
LANES = 128
ROUTER_ROWS = 32
EXPERT_ROW0 = 8

TILE = 256
CHUNK = 16
CAP = 2 * TILE + N_EXPERTS * CHUNK
CHUNKS_PER_TILE = CAP // CHUNK
COMBINE_TILES_PER_STEP = 2
PROJ_TILES = 2
MOE_TMB = 512
CPB = MOE_TMB // CHUNK
HS_W = D_MODEL + LANES
V_AUG_W = 2 * DIFF_W
QK_SCALE_LOG2 = (DIFF_QK ** -0.5) * math.log2(math.e)
SLOT_RADIX = 32

_O_RQ, _O_RK, _O_RV, _O_RG = 0, 256, 512, 768
_O_DQ, _O_DK, _O_DV = 1024, 1536, 2048
_O_CB, _O_CC, _O_CH = 2560, 2816, 3072
IN_WIDTH = 3328

BF = jnp.bfloat16
F32 = jnp.float32
I32 = jnp.int32

_VMEM_LIMIT = 56 * 1024 * 1024


def _dot(a, b):
    return jnp.dot(a, b, preferred_element_type=F32)


def _dot_nt(a, b):
    return lax.dot_general(a, b, (((1,), (1,)), ((), ())), preferred_element_type=F32)


def _dot_tn(a, b):
    return lax.dot_general(a, b, (((0,), (0,)), ((), ())), preferred_element_type=F32)


def _split(x):
    hi = x.astype(BF)
    lo = (x - hi.astype(F32)).astype(BF)
    return hi, lo


def _dot_hl(x, w_bf16):
    hi, lo = _split(x)
    return _dot(hi, w_bf16) + _dot(lo, w_bf16)


def _dot3(x, w_hi, w_lo):
    hi, lo = _split(x)
    return _dot(hi, w_hi) + (_dot(lo, w_hi) + _dot(hi, w_lo))


def _iota(shape, dim):
    return lax.broadcasted_iota(I32, shape, dim)


def _ones_where(cond):
    return jnp.where(cond, 1.0, 0.0).astype(BF)


def _log_sigmoid(x):
    return jnp.minimum(x, 0.0) - jnp.log1p(jnp.exp(-jnp.abs(x)))


def _silu(x):
    return x * jax.nn.sigmoid(x)


def _layer_norm(y, g, b):
    mu = jnp.mean(y, axis=-1, keepdims=True)
    yc = y - mu
    var = jnp.mean(yc * yc, axis=-1, keepdims=True)
    return yc * lax.rsqrt(var + EPS) * g + b


def _params(sem):
    return pltpu.CompilerParams(dimension_semantics=sem, vmem_limit_bytes=_VMEM_LIMIT)


_HBM = pl.BlockSpec(memory_space=pltpu.HBM)


def _mod_kernel(c_ref, w_ref, b_ref, o_ref):
    a = _silu(c_ref[...])
    w_hi, w_lo = _split(w_ref[...])
    o_ref[...] = _dot3(a, w_hi, w_lo) + b_ref[...]


def _modulation(cond, w_mod, b_mod):
    tn = 1536
    n6 = 6 * D_MODEL
    return pl.pallas_call(
        _mod_kernel,
        grid=(DEPTH, n6 // tn),
        in_specs=[
            pl.BlockSpec((MOD_ROWS, D_MODEL), lambda l, j: (0, 0)),
            pl.BlockSpec((None, D_MODEL, tn), lambda l, j: (l, 0, j)),
            pl.BlockSpec((None, 1, tn), lambda l, j: (l, 0, j)),
        ],
        out_specs=pl.BlockSpec((None, MOD_ROWS, tn), lambda l, j: (l, 0, j)),
        out_shape=jax.ShapeDtypeStruct((DEPTH, MOD_ROWS, n6), F32),
        compiler_params=_params(("parallel", "parallel")),
        name="modulation",
    )(cond, w_mod, b_mod.reshape(DEPTH, 1, n6))


def _mod_spec(layer, row_fn):
    return pl.BlockSpec((None, None, 1, 6 * D_MODEL), lambda s, i: (layer, row_fn(s), 0, 0))


def _augment_v(v):
    ones = jnp.ones((v.shape[0], DIFF_V), v.dtype)
    parts = []
    for hd in range(H_DIFF):
        parts += [v[:, hd * DIFF_V:(hd + 1) * DIFF_V], ones]
    return jnp.concatenate(parts, axis=1)


def _swap16(x):
    n = x.shape[-1]
    lane = _iota(x.shape, 1)
    return jnp.where((lane & 16) == 0, pltpu.roll(x, n - 16, axis=1), pltpu.roll(x, 16, axis=1))


def _proj_kernel(*refs, latent, n_aliased, layer):
    x_ref, mod_ref, w_ref, lg_ref = refs[:4]
    pos = 4
    if latent:
        cos_ref, sin_ref = refs[pos:pos + 2]
        pos += 2
    pos += n_aliased
    rq_ref, rk_ref, rv_ref, rg_ref, dq_ref, dk_ref, dv_ref, cb_ref, cch_ref = refs[pos:pos + 9]
    pos += 9
    if latent:
        kvf_ref, kvb_ref = refs[pos:pos + 2]
    else:
        dk32_ref, dv32_ref, stf_ref, stb_ref = refs[pos:pos + 4]
    rows = PROJ_TILES * TILE

    def put(ref, u, idx, val):
        if n_aliased:
            ref[(u,) + idx] = val
        else:
            for other in range(DEPTH):
                ref[(u, other) + idx] = val if other == layer else jnp.zeros_like(val)

    def store(ref, val):
        ref[...] = val.reshape(PROJ_TILES, TILE, val.shape[-1])

    m = mod_ref[...]
    sh1 = m[:, 0:D_MODEL]
    sc1 = m[:, D_MODEL:2 * D_MODEL]
    h = (x_ref[...].reshape(rows, D_MODEL) * (1.0 + sc1) + sh1).astype(BF)

    def col(off, width):
        return _dot(h, w_ref[:, off:off + width])

    store(rq_ref, col(_O_RQ, RET_W).astype(BF))
    rk = col(_O_RK, RET_W) * (RET_DK ** -0.5)
    store(rk_ref, rk.astype(BF))
    rv = col(_O_RV, RET_W).astype(BF)
    store(rv_ref, rv)
    store(rg_ref, col(_O_RG, RET_W))

    lg = _log_sigmoid(lg_ref[...])
    p = (_iota((rows, 1), 0) & (TILE - 1)).astype(F32)
    kf = (rk * jnp.exp((TILE - 1.0 - p) * lg[0:1])).astype(BF)
    kb = (rk * jnp.exp(p * lg[1:2])).astype(BF)
    for u in range(PROJ_TILES):
        tile = slice(u * TILE, (u + 1) * TILE)
        kvf = _dot_tn(kf[tile], rv[tile])
        kvb = _dot_tn(kb[tile], rv[tile])
        if latent:
            kvf_ref[u] = kvf
            kvb_ref[u] = kvb
        else:
            for hd in range(H_RET):
                lo = hd * RET_DK
                put(stf_ref, u, (hd,), kvf[lo:lo + RET_DK, lo:lo + RET_DK])
                put(stb_ref, u, (hd,), kvb[lo:lo + RET_DK, lo:lo + RET_DK])

    dq = col(_O_DQ, DIFF_W)
    dk = col(_O_DK, DIFF_W)
    dv = col(_O_DV, DIFF_W)
    if latent:
        cos = cos_ref[...].reshape(rows, DIFF_W)
        sin = sin_ref[...].reshape(rows, DIFF_W)
        dq = dq * cos + _swap16(dq) * sin
        dk = dk * cos + _swap16(dk) * sin
    else:
        for u in range(PROJ_TILES):
            put(dk32_ref, u, (), dk[u * TILE:(u + 1) * TILE])
            put(dv32_ref, u, (), dv[u * TILE:(u + 1) * TILE])
    store(dq_ref, (dq * QK_SCALE_LOG2).astype(BF))
    store(dk_ref, dk.astype(BF))
    store(dv_ref, _augment_v(dv.astype(BF)))

    store(cb_ref, col(_O_CB, CONV_CH))
    store(cch_ref, col(_O_CC, CONV_CH) * col(_O_CH, CONV_CH))


def _proj(x, mod, layer, row_fn, w_in_b, lg_lanes, rope_tabs=None, stacked=None):
    S, T, _ = x.shape
    n = T // TILE
    G = PROJ_TILES
    tiles = S * n
    latent = rope_tabs is not None
    assert tiles % G == 0 and (n % G == 0 or n == 1)

    def tok(w):
        return pl.BlockSpec((G, TILE, w), lambda t: (t, 0, 0))

    def tok_shape(w, dt):
        return jax.ShapeDtypeStruct((tiles, TILE, w), dt)

    in_specs = [
        tok(D_MODEL),
        pl.BlockSpec((None, None, 1, 6 * D_MODEL), lambda t: (layer, row_fn(t * G // n), 0, 0)),
        pl.BlockSpec((None, D_MODEL, IN_WIDTH), lambda t: (layer, 0, 0)),
        pl.BlockSpec((2, RET_W), lambda t: (0, 0)),
    ]
    args = [x.reshape(tiles, TILE, D_MODEL), mod, w_in_b, lg_lanes]
    out_specs = [tok(RET_W)] * 4 + [tok(DIFF_W)] * 2 + [tok(V_AUG_W)] + [tok(CONV_CH)] * 2
    out_shape = ([tok_shape(RET_W, BF)] * 3 + [tok_shape(RET_W, F32)] + [tok_shape(DIFF_W, BF)] * 2
                 + [tok_shape(V_AUG_W, BF)]
                 + [tok_shape(CONV_CH, F32)] * 2)
    aliases = {}
    if latent:
        in_specs += [pl.BlockSpec((G, TILE, DIFF_W), lambda t: (t % (n // G), 0, 0))] * 2
        args += [tab.reshape(n, TILE, DIFF_W) for tab in rope_tabs]
        out_specs += [pl.BlockSpec((G, RET_W, RET_W), lambda t: (t, 0, 0))] * 2
        out_shape += [jax.ShapeDtypeStruct((tiles, RET_W, RET_W), F32)] * 2
    else:
        assert n == 1
        kv_shape = jax.ShapeDtypeStruct((S, DEPTH, T, DIFF_W), F32)
        st_shape = jax.ShapeDtypeStruct((S, DEPTH, H_RET, RET_DK, RET_DK), F32)
        if stacked is not None:
            in_specs += [_HBM] * 4
            args += list(stacked)
            aliases = {len(args) - 4 + j: len(out_shape) + j for j in range(4)}
            out_specs += [pl.BlockSpec((G, None, TILE, DIFF_W), lambda t: (t, layer, 0, 0))] * 2
            out_specs += [pl.BlockSpec((G, None, H_RET, RET_DK, RET_DK), lambda t: (t, layer, 0, 0, 0))] * 2
        else:
            out_specs += [pl.BlockSpec((G, DEPTH, TILE, DIFF_W), lambda t: (t, 0, 0, 0))] * 2
            out_specs += [pl.BlockSpec((G, DEPTH, H_RET, RET_DK, RET_DK), lambda t: (t, 0, 0, 0, 0))] * 2
        out_shape += [kv_shape, kv_shape, st_shape, st_shape]
    outs = pl.pallas_call(
        functools.partial(_proj_kernel, latent=latent, n_aliased=len(aliases), layer=layer),
        grid=(tiles // G,),
        in_specs=in_specs,
        out_specs=out_specs,
        out_shape=out_shape,
        input_output_aliases=aliases,
        compiler_params=_params(("parallel",)),
        name="proj_lat" if latent else "proj_ctx",
    )(*args)
    token_outs = [o.reshape(S, T, o.shape[-1]) for o in outs[:9]]
    if latent:
        return token_outs + [o.reshape(S, n, RET_W, RET_W) for o in outs[9:]]
    return token_outs + list(outs[9:])


def _scan_kernel(kvf_ref, kvb_ref, s0f_ref, s0b_ref, lg_ref, sf_ref, sb_ref, *, n):
    lg = _log_sigmoid(lg_ref[...])
    dec = jnp.exp(float(TILE) * lg)
    same_head = (_iota((RET_W, RET_W), 0) >> 6) == (_iota((RET_W, RET_W), 1) >> 6)

    s = jnp.where(same_head, s0f_ref[...], 0.0)
    for c in range(n):
        sf_ref[c] = s
        s = s * dec[0:1] + jnp.where(same_head, kvf_ref[c], 0.0)
    s = jnp.where(same_head, s0b_ref[...], 0.0)
    for c in reversed(range(n)):
        sb_ref[c] = s
        s = s * dec[1:2] + jnp.where(same_head, kvb_ref[c], 0.0)


def _scan(kvf, kvb, s0f, s0b, lg_lanes):
    S, n = kvf.shape[:2]
    chunks = pl.BlockSpec((None, n, RET_W, RET_W), lambda s: (s, 0, 0, 0))
    one = pl.BlockSpec((None, RET_W, RET_W), lambda s: (s, 0, 0))
    return pl.pallas_call(
        functools.partial(_scan_kernel, n=n),
        grid=(S,),
        in_specs=[chunks, chunks, one, one, pl.BlockSpec((2, RET_W), lambda s: (0, 0))],
        out_specs=[chunks, chunks],
        out_shape=[jax.ShapeDtypeStruct((S, n, RET_W, RET_W), F32)] * 2,
        compiler_params=_params(("parallel",)),
        name="ret_scan",
    )(kvf, kvb, s0f, s0b, lg_lanes)


def _diff_scores(dq_ref, kv_refs, heads):
    lane = _iota((1, DIFF_V), 1)
    scores = []
    for hd in heads:
        lo = hd * DIFF_V
        q_h = dq_ref[:, lo:lo + DIFF_V]
        qs = jnp.concatenate([jnp.where(lane < DIFF_QK, q_h, 0), jnp.where(lane >= DIFF_QK, q_h, 0)], axis=0)
        scores.append(jnp.concatenate([_dot_nt(qs, k[:, lo:lo + DIFF_V]) for k, _ in kv_refs], axis=1))
    return jnp.concatenate(scores, axis=0)


def _diff_outputs(s, kv_refs, heads, lam):
    p = jnp.exp2(s - jnp.max(s, axis=-1, keepdims=True))

    def times_v(w, hd, width):
        acc = None
        start = 0
        for k, v in kv_refs:
            part = _dot(w[:, start:start + k.shape[0]], v[:, hd * 2 * DIFF_V:hd * 2 * DIFF_V + width])
            acc = part if acc is None else acc + part
            start += k.shape[0]
        return acc

    outs = []
    p = p.astype(BF)
    for n, hd in enumerate(heads):
        acc = times_v(p[n * 2 * TILE:(n + 1) * 2 * TILE], hd, 2 * DIFF_V)
        o0 = acc[0:TILE, 0:DIFF_V] * (1.0 / acc[0:TILE, DIFF_V:DIFF_V + 1])
        o1 = acc[TILE:, 0:DIFF_V] * (lam / acc[TILE:, DIFF_V:DIFF_V + 1])
        outs.append(o0 - o1)
    return outs


def _mixer_kernel(*refs, n_tiles, n_total, lam_init, cached):
    n_a = _n_token_mix_refs(cached)
    mix_even, mix_odd, decay_ref = refs[-3:]
    t = pl.program_id(0)
    tile_in_seq = lax.rem(jnp.minimum(t, n_total - 1), n_tiles)

    @pl.when(t == 0)
    def _():
        mix_odd[...] = jnp.zeros_like(mix_odd)
        lg = _log_sigmoid(refs[n_a - 5][...])
        dist = (_iota((TILE, TILE), 0) - _iota((TILE, TILE), 1)).astype(F32)
        diag2 = jnp.where(dist == 0.0, 2.0, 1.0)
        for hd in range(H_RET):
            lgf = lg[0:1, hd * RET_DK:hd * RET_DK + 1]
            lgb = lg[1:2, hd * RET_DK:hd * RET_DK + 1]
            decay_ref[hd] = jnp.exp(jnp.abs(dist) * jnp.where(dist > 0.0, lgf, lgb)) * diag2

    def step(mix_write, mix_read):
        mix_write[...] = _token_mix(*refs[:n_a], decay_ref, tile_in_seq=tile_in_seq, n_tiles=n_tiles,
                                    lam_init=lam_init, cached=cached)
        _project_route(mix_read[...], *refs[n_a:-3])

    @pl.when(lax.rem(t, 2) == 0)
    def _():
        step(mix_even, mix_odd)

    @pl.when(lax.rem(t, 2) == 1)
    def _():
        step(mix_odd, mix_even)


def _n_token_mix_refs(cached):
    return 18 + (2 if cached else 0)


def _token_mix(*refs, tile_in_seq, n_tiles, lam_init, cached):
    rq_ref, rk_ref, rv_ref, rg_ref, sf_ref, sb_ref, dq_ref = refs[:7]
    pos = 7
    kv_refs = []
    if cached:
        kv_refs.append((refs[pos], refs[pos + 1]))
        pos += 2
    kv_refs.append((refs[pos], refs[pos + 1]))
    pos += 2
    (cb_ref, cch_ref, cprev_ref, cnext_ref, lg_ref, lamp_ref, subg_ref, cw_ref, cbias_ref,
     decay_ref) = refs[pos:pos + 10]
    i = tile_in_seq

    lg = _log_sigmoid(lg_ref[...])
    head_of_lane = _iota((1, RET_W), 1) >> 6
    q = rq_ref[...]
    k = rk_ref[...]
    v = rv_ref[...]
    ret_o = jnp.zeros((TILE, RET_W), F32)
    for hd in range(H_RET):
        in_head = head_of_lane == hd
        sc = _dot_nt(jnp.where(in_head, q, 0), k)
        ret_o = ret_o + jnp.where(in_head, _dot((sc * decay_ref[hd]).astype(BF), v), 0.0)
    p = _iota((TILE, 1), 0).astype(F32)
    ret_o = ret_o + _dot(q, sf_ref[...].astype(BF)) * jnp.exp((p + 1.0) * lg[0:1])
    ret_o = ret_o + _dot(q, sb_ref[...].astype(BF)) * jnp.exp((float(TILE) - p) * lg[1:2])
    avg = jnp.where((_iota((RET_W, RET_W), 0) >> 6) == (_iota((RET_W, RET_W), 1) >> 6),
                    1.0 / RET_DK, 0.0).astype(BF)
    rc = ret_o - _dot_hl(ret_o, avg)
    ret = rc * lax.rsqrt(_dot_hl(rc * rc, avg) + EPS) * _silu(rg_ref[...])

    lp = lamp_ref[...]
    lam = (jnp.exp(jnp.sum(lp[0:1] * lp[1:2], axis=-1, keepdims=True))
           - jnp.exp(jnp.sum(lp[2:3] * lp[3:4], axis=-1, keepdims=True)) + lam_init)
    subg = subg_ref[...] * (1.0 - lam_init)
    head_groups = [[hd] for hd in range(H_DIFF)] if cached else [list(range(H_DIFF))]
    heads = []
    scores = _diff_scores(dq_ref, kv_refs, head_groups[0])
    for g, group in enumerate(head_groups):
        ahead = _diff_scores(dq_ref, kv_refs, head_groups[g + 1]) if g + 1 < len(head_groups) else None
        for o in _diff_outputs(scores, kv_refs, group, lam):
            o = o * lax.rsqrt(jnp.mean(o * o, axis=-1, keepdims=True) + EPS) * subg
            heads.append(o.astype(BF))
        scores = ahead
    diff = jnp.concatenate(heads, axis=1)

    cch = cch_ref[...]
    prev = jnp.where(i > 0, cprev_ref[7:8, :], 0.0)
    nxt = jnp.where(i < n_tiles - 1, cnext_ref[0:1, :], 0.0)
    r = _iota((TILE, 1), 0)
    up = jnp.where(r == 0, prev, pltpu.roll(cch, 1, axis=0))
    dn = jnp.where(r == TILE - 1, nxt, pltpu.roll(cch, TILE - 1, axis=0))
    cw = cw_ref[...]
    conv = cb_ref[...] * (up * cw[0:1] + cch * cw[1:2] + dn * cw[2:3] + cbias_ref[...])
    return jnp.concatenate([ret.astype(BF), diff, conv.astype(BF)], axis=1)


def _project_route(mixed, x_ref, mod_ref, wout_ref, lng_ref, lnb_ref, wrh_ref, wrl_ref, br_ref,
                   x1_ref, hs_ref, slots_ref, cnt_ref):
    m = mod_ref[...]
    g1 = m[:, 2 * D_MODEL:3 * D_MODEL]
    sh2 = m[:, 3 * D_MODEL:4 * D_MODEL]
    sc2 = m[:, 4 * D_MODEL:5 * D_MODEL]
    x1 = _layer_norm(ALPHA * x_ref[...] + g1 * _dot(mixed, wout_ref[...]), lng_ref[...], lnb_ref[...])
    x1_ref[...] = x1

    h2 = x1 * (1.0 + sc2) + sh2
    h2h, h2l = _split(h2)
    wrh = wrh_ref[...]
    logits = (_dot_nt(wrh, h2h) + (_dot_nt(wrh, h2l) + _dot_nt(wrl_ref[...], h2h))) + br_ref[...]
    neg = -jnp.inf
    g_id = _iota((EXPERT_ROW0, TILE), 0)
    gl = jnp.where(g_id < N_GROUPS, logits[0:EXPERT_ROW0], neg)
    gmax = jnp.max(gl, axis=0, keepdims=True)
    g_idx = jnp.min(jnp.where(gl == gmax, g_id, N_GROUPS), axis=0, keepdims=True)
    g_w = 1.0 / jnp.sum(jnp.exp(gl - gmax), axis=0, keepdims=True)
    e_id = _iota((N_EXPERTS, TILE), 0)
    el = jnp.where((e_id >> 2) == g_idx, logits[EXPERT_ROW0:EXPERT_ROW0 + N_EXPERTS], neg)
    v1 = jnp.max(el, axis=0, keepdims=True)
    i1 = jnp.min(jnp.where(el == v1, e_id, N_EXPERTS), axis=0, keepdims=True)
    el2 = jnp.where(e_id == i1, neg, el)
    v2 = jnp.max(el2, axis=0, keepdims=True)
    i2 = jnp.min(jnp.where(el2 == v2, e_id, N_EXPERTS), axis=0, keepdims=True)
    t = jnp.exp(v2 - v1)
    w1 = g_w / (1.0 + t)
    w2 = g_w * t / (1.0 + t)

    sel1 = e_id == i1
    sel2 = e_id == i2
    sel = jnp.where(sel1 | sel2, 1.0, 0.0)
    earlier = _ones_where(_iota((TILE, TILE), 0) < _iota((TILE, TILE), 1))
    rank = _dot(sel.astype(BF), earlier)
    cnt = jnp.sum(sel, axis=1, keepdims=True).astype(I32)
    padded = ((cnt + (CHUNK - 1)) >> 4) << 4
    incl = jnp.broadcast_to(padded, (N_EXPERTS, LANES))
    e_row = _iota((N_EXPERTS, LANES), 0)
    for step in (1, 2, 4, 8):
        incl = incl + jnp.where(e_row >= step, pltpu.roll(incl, step, axis=0), 0)
    seg_off = (incl[:, 0:1] - padded).astype(F32)
    spos = seg_off + rank
    slot1 = jnp.sum(jnp.where(sel1, spos, 0.0), axis=0, keepdims=True).astype(I32)
    slot2 = jnp.sum(jnp.where(sel2, spos, 0.0), axis=0, keepdims=True).astype(I32)
    s_id = _iota((CAP, TILE), 0)
    at1 = s_id == slot1
    at2 = s_id == slot2
    hs_ref[:, 0:D_MODEL] = _dot(_ones_where(at1 | at2), h2h).astype(BF)
    w1h = w1.astype(BF).astype(F32)
    w2h = w2.astype(BF).astype(F32)
    ones = jnp.ones((TILE, LANES), BF)
    g_hi = _dot((jnp.where(at1, w1h, 0.0) + jnp.where(at2, w2h, 0.0)).astype(BF), ones)
    g_lo = _dot((jnp.where(at1, w1 - w1h, 0.0) + jnp.where(at2, w2 - w2h, 0.0)).astype(BF), ones)
    lane = _iota((CAP, LANES), 1)
    hs_ref[:, D_MODEL:] = jnp.where(lane == 0, g_hi, jnp.where(lane == 1, g_lo, 0.0)).astype(BF)
    cnt_ref[...] = jnp.broadcast_to(padded, (N_EXPERTS, LANES))

    dr = _iota((LANES, TILE), 0)
    digits = jnp.where(dr == 0, slot1 & (SLOT_RADIX - 1), jnp.where(dr == 1, slot1 >> 5,
             jnp.where(dr == 2, slot2 & (SLOT_RADIX - 1), jnp.where(dr == 3, slot2 >> 5, 0))))
    eye = _ones_where(_iota((TILE, TILE), 0) == _iota((TILE, TILE), 1))
    cols = _dot_nt(eye, digits.astype(F32).astype(BF))
    s1c = (cols[:, 0:1] + SLOT_RADIX * cols[:, 1:2]).astype(I32)
    s2c = (cols[:, 2:3] + SLOT_RADIX * cols[:, 3:4]).astype(I32)
    tl = _iota((TILE, LANES), 1)
    slots_ref[...] = jnp.where(tl == 0, s1c, jnp.where(tl == 1, s2c, 0))


def _mixer(x, mod, layer, row_fn, pr, states, cache_kv, new_kv, w_out_b, lg_lanes, lamp, subg, cw, cbias,
           lng, lnb, wrh, wrl, br, lam_init):
    S, T, _ = x.shape
    n = T // TILE
    rq, rk, rv, rg, dq, cb, cch = pr
    t8 = TILE // 8
    cached = cache_kv is not None
    last = S * n - 1

    def at_mix(t):
        return jnp.minimum(t, last) // n, jnp.minimum(t, last) % n

    def at_out(t):
        return jnp.maximum(t - 1, 0) // n, jnp.maximum(t - 1, 0) % n

    def tok(w, at):
        return pl.BlockSpec((None, TILE, w), lambda t: (*at(t), 0))

    def full(shape):
        return pl.BlockSpec(shape, lambda t: (0,) * len(shape))

    def seq(a):
        mode = pl.Buffered(1) if n > 1 else None
        return pl.BlockSpec((None,) + a.shape[1:], lambda t: (at_mix(t)[0], 0, 0), pipeline_mode=mode)

    def halo(offset):
        def index(t):
            s, i = at_mix(t)
            return (s, jnp.clip(i * t8 + offset, 0, T // 8 - 1), 0)
        return pl.BlockSpec((None, 8, CONV_CH), index)

    if states is None:
        zero_state = jnp.zeros((1, 1, RET_W, RET_W), F32)
        states = (zero_state, zero_state)
        state = pl.BlockSpec((None, None, RET_W, RET_W), lambda t: (0, 0, 0, 0))
    else:
        state = pl.BlockSpec((None, None, RET_W, RET_W), lambda t: (*at_mix(t), 0, 0))

    in_specs = [tok(RET_W, at_mix)] * 4 + [state, state, tok(DIFF_W, at_mix)]
    args = [rq, rk, rv, rg, states[0], states[1], dq]
    for kv in ([cache_kv] if cached else []) + [new_kv]:
        in_specs += [seq(kv[0]), seq(kv[1])]
        args += list(kv)
    in_specs += [tok(CONV_CH, at_mix), tok(CONV_CH, at_mix), halo(-1), halo(t8),
                 full((2, RET_W)), full((4, DIFF_QK)), full((1, DIFF_V)), full((3, CONV_CH)), full((1, CONV_CH))]
    args += [cb, cch, cch, cch, lg_lanes, lamp, subg, cw, cbias]
    assert len(args) == _n_token_mix_refs(cached)
    in_specs += [tok(D_MODEL, at_out),
                 pl.BlockSpec((None, None, 1, 6 * D_MODEL), lambda t: (layer, row_fn(at_out(t)[0]), 0, 0)),
                 pl.BlockSpec((None, D_MODEL, D_MODEL), lambda t: (layer, 0, 0)),
                 full((1, D_MODEL)), full((1, D_MODEL)),
                 full((ROUTER_ROWS, D_MODEL)), full((ROUTER_ROWS, D_MODEL)), full((ROUTER_ROWS, 1))]
    args += [x, mod, w_out_b, lng, lnb, wrh, wrl, br]

    def out_tile(t):
        s, i = at_out(t)
        return s * n + i

    return pl.pallas_call(
        functools.partial(_mixer_kernel, n_tiles=n, n_total=S * n, lam_init=lam_init, cached=cached),
        grid=(S * n + 1,),
        in_specs=in_specs,
        out_specs=[tok(D_MODEL, at_out),
                   pl.BlockSpec((CAP, HS_W), lambda t: (out_tile(t), 0)),
                   tok(LANES, at_out),
                   pl.BlockSpec((None, None, N_EXPERTS, LANES), lambda t: (*at_out(t), 0, 0))],
        out_shape=[jax.ShapeDtypeStruct((S, T, D_MODEL), F32),
                   jax.ShapeDtypeStruct((S * n * CAP, HS_W), BF),
                   jax.ShapeDtypeStruct((S, T, LANES), I32),
                   jax.ShapeDtypeStruct((S, n, N_EXPERTS, LANES), I32)],
        scratch_shapes=[pltpu.VMEM((TILE, D_MODEL), BF), pltpu.VMEM((TILE, D_MODEL), BF),
                        pltpu.VMEM((H_RET, TILE, TILE), F32)],
        compiler_params=_params(("arbitrary",)),
        name="mixer_lat" if cached else "mixer_ctx",
    )(*args)


def _block_schedule(padded, chunks_a):
    NT = padded.shape[0]
    n_chunks = NT * CHUNKS_PER_TILE
    n_blocks = n_chunks // CPB + N_EXPERTS
    c16 = padded // CHUNK
    ends = jnp.cumsum(c16, axis=1)
    before = jnp.cumsum(c16, axis=0) - c16
    per_expert = jnp.sum(c16, axis=0)
    nb = (per_expert + (CPB - 1)) // CPB
    b_end = jnp.cumsum(nb)
    q = jnp.arange(CHUNKS_PER_TILE, dtype=I32)
    e = jnp.arange(N_EXPERTS, dtype=I32)
    key = jnp.sum((ends[:, None, :] <= q[None, :, None]).astype(I32), axis=-1)
    pos_e = ((b_end - nb) * CPB + before)[:, None, :] + (q[None, :, None] - (ends - c16)[:, None, :])
    pos = jnp.sum(jnp.where(key[..., None] == e, pos_e, 0), axis=-1)
    pos = jnp.where(key < N_EXPERTS, pos, -1).reshape(-1)
    match = pos[None, :] == jnp.arange(n_blocks * CPB, dtype=I32)[:, None]
    cid = jnp.sum(jnp.where(match, jnp.arange(n_chunks, dtype=I32)[None, :], 0), axis=-1)
    nv = jnp.sum(match.astype(I32).reshape(n_blocks, -1), axis=-1)
    first = match & (jnp.arange(n_chunks, dtype=I32) < chunks_a)[None, :]
    nv_a = jnp.sum(first.astype(I32).reshape(n_blocks, -1), axis=-1)
    b = jnp.arange(n_blocks, dtype=I32)
    eb = jnp.minimum(jnp.sum((b_end[None, :] <= b[:, None]).astype(I32), axis=-1), N_EXPERTS - 1)
    zero = jnp.zeros((1,), I32)
    counts = jnp.concatenate([nv, zero, nv_a, zero]).astype(I32)
    return eb.astype(I32), counts, jnp.concatenate([cid, jnp.zeros((CPB,), I32)]).astype(I32)


def _moe_kernel(eb_ref, nv_ref, cid_ref, hs_a, hs_b, wg_ref, wu_ref, wd_ref, ys_a, ys_b,
                xbuf, ybuf, wg_b, wu_b, wd_b, in_sem, out_sem, *, n_blocks, chunks_a):
    b = pl.program_id(0)

    def chunk_rows(c):
        return pl.ds(pl.multiple_of(c * CHUNK, CHUNK), CHUNK)

    def gather_copy(which, c, j, slot):
        src = (hs_a, hs_b)[which]
        return pltpu.make_async_copy(src.at[chunk_rows(c)], xbuf.at[slot, chunk_rows(j)], in_sem.at[slot])

    def scatter_copy(which, c, j, slot):
        dst = (ys_a, ys_b)[which]
        return pltpu.make_async_copy(ybuf.at[slot, chunk_rows(j)],
                                     dst.at[chunk_rows(c), pl.ds(0, D_MODEL)], out_sem.at[slot])

    def for_chunks(blk, fn):
        n_a = nv_ref[n_blocks + 1 + blk]

        def body_a(j, carry):
            fn(j, 0, cid_ref[blk * CPB + j])
            return carry

        def body_b(j, carry):
            fn(j, 1, cid_ref[blk * CPB + j] - chunks_a)
            return carry
        lax.fori_loop(0, n_a, body_a, 0)
        lax.fori_loop(n_a, nv_ref[blk], body_b, 0)

    def each_chunk(blk, fn):
        n_a = nv_ref[n_blocks + 1 + blk]
        n_v = nv_ref[blk]
        for j in range(CPB):
            c = cid_ref[blk * CPB + j]

            @pl.when(j < n_a)
            def _():
                fn(j, 0, c)

            @pl.when((j >= n_a) & (j < n_v))
            def _():
                fn(j, 1, c - chunks_a)

    def wait_gathers(blk, slot):
        rows = pl.ds(0, nv_ref[blk] * CHUNK)

        @pl.when(nv_ref[blk] > 0)
        def _():
            pltpu.make_async_copy(hs_a.at[rows], xbuf.at[slot, rows], in_sem.at[slot]).wait()

    def wait_scatters(blk, slot):
        rows = pl.ds(0, nv_ref[blk] * CHUNK)

        @pl.when(nv_ref[blk] > 0)
        def _():
            pltpu.make_async_copy(ybuf.at[slot, rows], ys_a.at[rows, pl.ds(0, D_MODEL)], out_sem.at[slot]).wait()

    slot = lax.rem(b, 2)
    other = 1 - slot

    @pl.when(b == 0)
    def _():
        xbuf[...] = jnp.zeros_like(xbuf)
        for_chunks(0, lambda j, w, c: gather_copy(w, c, j, 0).start())

    @pl.when((b == 0) | (eb_ref[b] != eb_ref[jnp.maximum(b - 1, 0)]))
    def _():
        wg_b[...] = wg_ref[...].astype(BF)
        wu_b[...] = wu_ref[...].astype(BF)
        wd_b[...] = wd_ref[...].astype(BF)

    wait_gathers(b, slot)

    @pl.when(b >= 2)
    def _():
        wait_scatters(b - 2, slot)

    @pl.when(nv_ref[b] > 0)
    def _():
        each_chunk(b + 1, lambda j, w, c: gather_copy(w, c, j, other).start())
        x = xbuf[slot, :, 0:D_MODEL]
        gate = (xbuf[slot, :, D_MODEL:D_MODEL + 1].astype(F32)
                + xbuf[slot, :, D_MODEL + 1:D_MODEL + 2].astype(F32))
        hg = _dot(x, wg_b[...])
        hu = _dot(x, wu_b[...])
        act = _silu(hg) * hu * gate
        ybuf[slot] = _dot(act.astype(BF), wd_b[...]).astype(BF)
        each_chunk(b, lambda j, w, c: scatter_copy(w, c, j, slot).start())

    @pl.when(b == n_blocks - 1)
    def _():
        if n_blocks >= 2:
            wait_scatters(b - 1, other)
        wait_scatters(b, slot)


def _moe(hs_a, hs_b, eb, nv, cid, layer, w_gate, w_up, w_down):
    n_blocks = eb.shape[0]

    def w_spec(a, c):
        return pl.BlockSpec((None, None, a, c), lambda b, eb, nv, cid: (layer, eb[b], 0, 0))

    grid_spec = pltpu.PrefetchScalarGridSpec(
        num_scalar_prefetch=3,
        grid=(n_blocks,),
        in_specs=[_HBM, _HBM, w_spec(D_MODEL, D_EXPERT), w_spec(D_MODEL, D_EXPERT), w_spec(D_EXPERT, D_MODEL)],
        out_specs=[_HBM, _HBM],
        scratch_shapes=[pltpu.VMEM((2, MOE_TMB, HS_W), BF), pltpu.VMEM((2, MOE_TMB, D_MODEL), BF),
                        pltpu.VMEM((D_MODEL, D_EXPERT), BF), pltpu.VMEM((D_MODEL, D_EXPERT), BF),
                        pltpu.VMEM((D_EXPERT, D_MODEL), BF),
                        pltpu.SemaphoreType.DMA((2,)), pltpu.SemaphoreType.DMA((2,))],
    )
    return pl.pallas_call(
        functools.partial(_moe_kernel, n_blocks=n_blocks, chunks_a=hs_a.shape[0] // CHUNK),
        grid_spec=grid_spec,
        out_shape=[jax.ShapeDtypeStruct(hs_a.shape, BF), jax.ShapeDtypeStruct(hs_b.shape, BF)],
        input_output_aliases={3: 0, 4: 1},
        compiler_params=_params(("arbitrary",)),
        name="moe_sorted",
    )(eb, nv, cid, hs_a, hs_b, w_gate, w_up, w_down)


def _combine_kernel(ys_ref, slots_ref, x1_ref, mod_ref, lng_ref, lnb_ref, o_ref):
    g2 = mod_ref[:, 5 * D_MODEL:6 * D_MODEL]
    s_lane = _iota((TILE, CAP), 1)
    for u in range(COMBINE_TILES_PER_STEP):
        sl = slots_ref[u]
        pick = _ones_where((s_lane == sl[:, 0:1]) | (s_lane == sl[:, 1:2]))
        ffn = _dot(pick, ys_ref[u * CAP:(u + 1) * CAP, 0:D_MODEL])
        o_ref[u] = _layer_norm(ALPHA * x1_ref[u] + g2 * ffn, lng_ref[...], lnb_ref[...])


def _combine(ys, slots, x1, mod, layer, row_fn, lng, lnb):
    S, T, _ = x1.shape
    n = T // TILE
    U = COMBINE_TILES_PER_STEP
    assert (S * n) % U == 0 and (n % U == 0 or n == 1)

    def tok(w):
        return pl.BlockSpec((U, TILE, w), lambda t: (t, 0, 0))

    out = pl.pallas_call(
        _combine_kernel,
        grid=(S * n // U,),
        in_specs=[
            pl.BlockSpec((U * CAP, HS_W), lambda t: (t, 0)),
            tok(LANES), tok(D_MODEL),
            pl.BlockSpec((None, None, 1, 6 * D_MODEL), lambda t: (layer, row_fn(t * U // n), 0, 0)),
            pl.BlockSpec((1, D_MODEL), lambda t: (0, 0)),
            pl.BlockSpec((1, D_MODEL), lambda t: (0, 0)),
        ],
        out_specs=tok(D_MODEL),
        out_shape=jax.ShapeDtypeStruct((S * n, TILE, D_MODEL), F32),
        compiler_params=_params(("parallel",)),
        name="moe_combine",
    )(ys, slots.reshape(S * n, TILE, LANES), x1.reshape(S * n, TILE, D_MODEL), mod, lng, lnb)
    return out.reshape(S, T, D_MODEL)


def _rope_tables(n_lat):
    half = DIFF_QK // 2
    pairs = half // 2
    inv = (1.0 / (ROPE_BASE ** (np.arange(pairs, dtype=np.float32) * 2.0 / half))).astype(np.float32)
    t = np.arange(n_lat)
    ang_r = ((t // GRID_W).astype(np.float32)[:, None] * inv[None, :]).astype(np.float64)
    ang_c = ((t % GRID_W).astype(np.float32)[:, None] * inv[None, :]).astype(np.float64)
    cos = np.concatenate([np.cos(ang_r)] * 2 + [np.cos(ang_c)] * 2, axis=1)
    sin = np.concatenate([-np.sin(ang_r), np.sin(ang_r), -np.sin(ang_c), np.sin(ang_c)], axis=1)
    reps = DIFF_W // DIFF_QK
    return (jnp.asarray(np.tile(cos, (1, reps)), F32), jnp.asarray(np.tile(sin, (1, reps)), F32))


def _block_diag(s):
    S = s.shape[0]
    eye = jnp.eye(H_RET, dtype=s.dtype)
    return jnp.einsum('shdv,hg->shdgv', s, eye).reshape(S, RET_W, RET_W)


def _router_rows(w_group, b_group, w_expert, b_expert):
    pad = EXPERT_ROW0 - N_GROUPS
    tail = ROUTER_ROWS - EXPERT_ROW0 - N_EXPERTS
    w = jnp.concatenate([w_group.T, jnp.zeros((pad, D_MODEL), F32),
                         w_expert.reshape(D_MODEL, N_EXPERTS).T, jnp.zeros((tail, D_MODEL), F32)], axis=0)
    bias = jnp.concatenate([b_group, jnp.zeros((pad,), F32), b_expert.reshape(N_EXPERTS), jnp.zeros((tail,), F32)])
    hi = w.astype(BF)
    return hi, (w - hi.astype(F32)).astype(BF), bias.reshape(ROUTER_ROWS, 1)


def kernel(x_prompt, x_sample, cache_diff_k, cache_diff_v, state_ret_fwd, state_ret_bwd, c, c_ctx, w_mod, b_mod, w_in, ret_decay_logit, diff_lambda, diff_subln_g, conv_w, conv_b, w_out, ln_g, ln_b, w_router_group, b_router_group, w_router_expert, b_router_expert, w_gate, w_up, w_down):
    B, T_ctx, _ = x_prompt.shape
    Bd, T_lat, _ = x_sample.shape
    assert T_ctx == TILE and T_lat % TILE == 0 and T_lat % GRID_W == 0
    assert 1 + Bd <= MOD_ROWS

    cond = jnp.concatenate([c_ctx[None, :], c, jnp.zeros((MOD_ROWS - 1 - Bd, D_MODEL), F32)], axis=0)
    mod = _modulation(cond, w_mod, b_mod).reshape(DEPTH, MOD_ROWS, 1, 6 * D_MODEL)
    rope_tabs = _rope_tables(T_lat)
    w_in_b = w_in.astype(BF)
    w_out_b = w_out.astype(BF)

    ctx_row = lambda s: 0
    lat_row = lambda s: s + 1
    ctx_tiles = B * (T_ctx // TILE)
    lat_tiles = Bd * (T_lat // TILE)

    yp, ys = x_prompt, x_sample
    stacked = None
    for l in range(DEPTH):
        lam_init = 0.8 - 0.6 * math.exp(-0.3 * l)
        lg_lanes = jnp.repeat(ret_decay_logit[l], RET_DK, axis=1)
        wrh, wrl, br = _router_rows(w_router_group[l], b_router_group[l], w_router_expert[l], b_router_expert[l])
        shared = (w_out_b, lg_lanes, diff_lambda[l], diff_subln_g[l].reshape(1, DIFF_V), conv_w[l],
                  conv_b[l].reshape(1, CONV_CH), ln_g[l, 0].reshape(1, D_MODEL), ln_b[l, 0].reshape(1, D_MODEL),
                  wrh, wrl, br, lam_init)
        ln2 = (ln_g[l, 1].reshape(1, D_MODEL), ln_b[l, 1].reshape(1, D_MODEL))

        pr = _proj(yp, mod, l, ctx_row, w_in_b, lg_lanes, stacked=stacked)
        rq, rk, rv, rg, dq, dk, dv, cb, cch = pr[:9]
        stacked = pr[9:]
        x1_c, hs_c, slots_c, cnt_c = _mixer(yp, mod, l, ctx_row, (rq, rk, rv, rg, dq, cb, cch), None, None,
                                            (dk, dv), *shared)

        pr = _proj(ys, mod, l, lat_row, w_in_b, lg_lanes, rope_tabs=rope_tabs)
        rq, rk, rv, rg, dq, dk, dv, cb, cch, kvf, kvb = pr
        states = _scan(kvf, kvb, _block_diag(state_ret_fwd[:, l]), _block_diag(state_ret_bwd[:, l]), lg_lanes)
        cache_v = cache_diff_v[:, l].astype(BF)
        cache_kv = (cache_diff_k[:, l].reshape(Bd, -1, DIFF_W).astype(BF),
                    jnp.concatenate([cache_v, jnp.ones_like(cache_v)], axis=-1).reshape(Bd, -1, V_AUG_W))
        x1_l, hs_l, slots_l, cnt_l = _mixer(ys, mod, l, lat_row, (rq, rk, rv, rg, dq, cb, cch), states, cache_kv,
                                            (dk, dv), *shared)

        padded = jnp.concatenate([cnt_c.reshape(ctx_tiles, N_EXPERTS, LANES)[:, :, 0],
                                  cnt_l.reshape(lat_tiles, N_EXPERTS, LANES)[:, :, 0]], axis=0)
        eb, nv, cid = _block_schedule(padded, ctx_tiles * CHUNKS_PER_TILE)
        out_c, out_l = _moe(hs_c, hs_l, eb, nv, cid, l, w_gate, w_up, w_down)
        yp = _combine(out_c, slots_c, x1_c, mod, l, ctx_row, *ln2)
        ys = _combine(out_l, slots_l, x1_l, mod, l, lat_row, *ln2)

    new_k, new_v, st_f, st_b = stacked
    return (yp, ys, new_k.reshape(B, DEPTH, T_ctx, H_DIFF, 2, DIFF_QK),
            new_v.reshape(B, DEPTH, T_ctx, H_DIFF, DIFF_V), st_f, st_b)
```

```python
import functools
import math

import numpy as np
import jax
import jax.numpy as jnp
from jax import lax
from jax.experimental import pallas as pl
from jax.experimental.pallas import tpu as pltpu

D_MODEL = 1024
DEPTH = 2
GRID_W = 64
H_RET = 4
RET_DK = 64
RET_W = H_RET * RET_DK
H_DIFF = 4
DIFF_QK = 64
DIFF_V = 2 * DIFF_QK
DIFF_W = H_DIFF * DIFF_V
CONV_CH = 256
ROPE_BASE = 10000.0
N_GROUPS = 4
EXPERTS_PER_GROUP = 4
N_EXPERTS = N_GROUPS * EXPERTS_PER_GROUP
D_EXPERT = 512
ALPHA = (2 * DEPTH) ** 0.25
EPS = 1e-5
MOD_ROWS = 8
LANES = 128
ROUTER_ROWS = 32
EXPERT_ROW0 = 8

TILE = 256
CHUNK = 16
CAP = 2 * TILE + N_EXPERTS * CHUNK
CHUNKS_PER_TILE = CAP // CHUNK
COMBINE_TILES_PER_STEP = 4
PROJ_TILES = 4
MOE_TMB = 1024
CPB = MOE_TMB // CHUNK
HS_W = D_MODEL + LANES
V_AUG_W = 2 * DIFF_W
QK_SCALE_LOG2 = (DIFF_QK ** -0.5) * math.log2(math.e)
SLOT_RADIX = 32

_O_RQ, _O_RK, _O_RV, _O_RG = 0, 256, 512, 768
_O_DQ, _O_DK, _O_DV = 1024, 1536, 2048
_O_CB, _O_CC, _O_CH = 2560, 2816, 3072
IN_WIDTH = 3328

BF = jnp.bfloat16
F32 = jnp.float32
I32 = jnp.int32

_VMEM_LIMIT = 56 * 1024 * 1024


def _dot(a, b):
    return jnp.dot(a, b, preferred_element_type=F32)


def _dot_nt(a, b):
    return lax.dot_general(a, b, (((1,), (1,)), ((), ())), preferred_element_type=F32)


def _dot_tn(a, b):
    return lax.dot_general(a, b, (((0,), (0,)), ((), ())), preferred_element_type=F32)


def _split(x):
    hi = x.astype(BF)
    lo = (x - hi.astype(F32)).astype(BF)
    return hi, lo


def _dot_hl(x, w_bf16):
    hi, lo = _split(x)
    return _dot(hi, w_bf16) + _dot(lo, w_bf16)


def _dot3(x, w_hi, w_lo):
    hi, lo = _split(x)
    return _dot(hi, w_hi) + (_dot(lo, w_hi) + _dot(hi, w_lo))


def _iota(shape, dim):
    return lax.broadcasted_iota(I32, shape, dim)


def _ones_where(cond):
    return jnp.where(cond, 1.0, 0.0).astype(BF)


def _log_sigmoid(x):
    return jnp.minimum(x, 0.0) - jnp.log1p(jnp.exp(-jnp.abs(x)))


def _silu(x):
    return x * jax.nn.sigmoid(x)


def _layer_norm(y, g, b):
    mu = jnp.mean(y, axis=-1, keepdims=True)
    yc = y - mu
    var = jnp.mean(yc * yc, axis=-1, keepdims=True)
    return yc * lax.rsqrt(var + EPS) * g + b


def _params(sem):
    return pltpu.CompilerParams(dimension_semantics=sem, vmem_limit_bytes=_VMEM_LIMIT)


_HBM = pl.BlockSpec(memory_space=pltpu.HBM)


def _mod_kernel(c_ref, w_ref, b_ref, o_ref):
    a = _silu(c_ref[...])
    w_hi, w_lo = _split(w_ref[...])
    o_ref[...] = _dot3(a, w_hi, w_lo) + b_ref[...]


def _modulation(cond, w_mod, b_mod):
    tn = 1536
    n6 = 6 * D_MODEL
    return pl.pallas_call(
        _mod_kernel,
        grid=(DEPTH, n6 // tn),
        in_specs=[
            pl.BlockSpec((MOD_ROWS, D_MODEL), lambda l, j: (0, 0)),
            pl.BlockSpec((None, D_MODEL, tn), lambda l, j: (l, 0, j)),
            pl.BlockSpec((None, 1, tn), lambda l, j: (l, 0, j)),
        ],
        out_specs=pl.BlockSpec((None, MOD_ROWS, tn), lambda l, j: (l, 0, j)),
        out_shape=jax.ShapeDtypeStruct((DEPTH, MOD_ROWS, n6), F32),
        compiler_params=_params(("parallel", "parallel")),
        name="modulation",
    )(cond, w_mod, b_mod.reshape(DEPTH, 1, n6))


def _mod_spec(layer, row_fn):
    return pl.BlockSpec((None, None, 1, 6 * D_MODEL), lambda s, i: (layer, row_fn(s), 0, 0))


def _augment_v(v):
    ones = jnp.ones((v.shape[0], DIFF_V), v.dtype)
    parts = []
    for hd in range(H_DIFF):
        parts += [v[:, hd * DIFF_V:(hd + 1) * DIFF_V], ones]
    return jnp.concatenate(parts, axis=1)


def _swap16(x):
    n = x.shape[-1]
    lane = _iota(x.shape, 1)
    return jnp.where((lane & 16) == 0, pltpu.roll(x, n - 16, axis=1), pltpu.roll(x, 16, axis=1))


def _proj_kernel(*refs, latent, n_aliased, layer):
    x_ref, mod_ref, w_ref, lg_ref = refs[:4]
    pos = 4
    if latent:
        cos_ref, sin_ref = refs[pos:pos + 2]
        pos += 2
    pos += n_aliased
    rq_ref, rk_ref, rv_ref, rg_ref, dq_ref, dk_ref, dv_ref, cb_ref, cch_ref = refs[pos:pos + 9]
    pos += 9
    if latent:
        kvf_ref, kvb_ref = refs[pos:pos + 2]
    else:
        dk32_ref, dv32_ref, stf_ref, stb_ref = refs[pos:pos + 4]
    rows = PROJ_TILES * TILE

    def put(ref, u, idx, val):
        if n_aliased:
            ref[(u,) + idx] = val
        else:
            for other in range(DEPTH):
                ref[(u, other) + idx] = val if other == layer else jnp.zeros_like(val)

    def store(ref, val):
        ref[...] = val.reshape(PROJ_TILES, TILE, val.shape[-1])

    m = mod_ref[...]
    sh1 = m[:, 0:D_MODEL]
    sc1 = m[:, D_MODEL:2 * D_MODEL]
    h = (x_ref[...].reshape(rows, D_MODEL) * (1.0 + sc1) + sh1).astype(BF)

    def col(off, width):
        return _dot(h, w_ref[:, off:off + width])

    store(rq_ref, col(_O_RQ, RET_W).astype(BF))
    rk = col(_O_RK, RET_W) * (RET_DK ** -0.5)
    store(rk_ref, rk.astype(BF))
    rv = col(_O_RV, RET_W).astype(BF)
    store(rv_ref, rv)
    store(rg_ref, col(_O_RG, RET_W))

    lg = _log_sigmoid(lg_ref[...])
    p = (_iota((rows, 1), 0) & (TILE - 1)).astype(F32)
    kf = (rk * jnp.exp((TILE - 1.0 - p) * lg[0:1])).astype(BF)
    kb = (rk * jnp.exp(p * lg[1:2])).astype(BF)
    for u in range(PROJ_TILES):
        tile = slice(u * TILE, (u + 1) * TILE)
        kvf = _dot_tn(kf[tile], rv[tile])
        kvb = _dot_tn(kb[tile], rv[tile])
        if latent:
            kvf_ref[u] = kvf
            kvb_ref[u] = kvb
        else:
            for hd in range(H_RET):
                lo = hd * RET_DK
                put(stf_ref, u, (hd,), kvf[lo:lo + RET_DK, lo:lo + RET_DK])
                put(stb_ref, u, (hd,), kvb[lo:lo + RET_DK, lo:lo + RET_DK])

    dq = col(_O_DQ, DIFF_W)
    dk = col(_O_DK, DIFF_W)
    dv = col(_O_DV, DIFF_W)
    if latent:
        cos = cos_ref[...].reshape(rows, DIFF_W)
        sin = sin_ref[...].reshape(rows, DIFF_W)
        dq = dq * cos + _swap16(dq) * sin
        dk = dk * cos + _swap16(dk) * sin
    else:
        for u in range(PROJ_TILES):
            put(dk32_ref, u, (), dk[u * TILE:(u + 1) * TILE])
            put(dv32_ref, u, (), dv[u * TILE:(u + 1) * TILE])
    store(dq_ref, (dq * QK_SCALE_LOG2).astype(BF))
    store(dk_ref, dk.astype(BF))
    store(dv_ref, _augment_v(dv.astype(BF)))

    store(cb_ref, col(_O_CB, CONV_CH))
    store(cch_ref, col(_O_CC, CONV_CH) * col(_O_CH, CONV_CH))


def _proj(x, mod, layer, row_fn, w_in_b, lg_lanes, rope_tabs=None, stacked=None):
    S, T, _ = x.shape
    n = T // TILE
    G = PROJ_TILES
    tiles = S * n
    latent = rope_tabs is not None
    assert tiles % G == 0 and (n % G == 0 or n == 1)

    def tok(w):
        return pl.BlockSpec((G, TILE, w), lambda t: (t, 0, 0))

    def tok_shape(w, dt):
        return jax.ShapeDtypeStruct((tiles, TILE, w), dt)

    in_specs = [
        tok(D_MODEL),
        pl.BlockSpec((None, None, 1, 6 * D_MODEL), lambda t: (layer, row_fn(t * G // n), 0, 0)),
        pl.BlockSpec((None, D_MODEL, IN_WIDTH), lambda t: (layer, 0, 0)),
        pl.BlockSpec((2, RET_W), lambda t: (0, 0)),
    ]
    args = [x.reshape(tiles, TILE, D_MODEL), mod, w_in_b, lg_lanes]
    out_specs = [tok(RET_W)] * 4 + [tok(DIFF_W)] * 2 + [tok(V_AUG_W)] + [tok(CONV_CH)] * 2
    out_shape = ([tok_shape(RET_W, BF)] * 3 + [tok_shape(RET_W, F32)] + [tok_shape(DIFF_W, BF)] * 2
                 + [tok_shape(V_AUG_W, BF)]
                 + [tok_shape(CONV_CH, F32)] * 2)
    aliases = {}
    if latent:
        in_specs += [pl.BlockSpec((G, TILE, DIFF_W), lambda t: (t % (n // G), 0, 0))] * 2
        args += [tab.reshape(n, TILE, DIFF_W) for tab in rope_tabs]
        out_specs += [pl.BlockSpec((G, RET_W, RET_W), lambda t: (t, 0, 0))] * 2
        out_shape += [jax.ShapeDtypeStruct((tiles, RET_W, RET_W), F32)] * 2
    else:
        assert n == 1
        kv_shape = jax.ShapeDtypeStruct((S, DEPTH, T, DIFF_W), F32)
        st_shape = jax.ShapeDtypeStruct((S, DEPTH, H_RET, RET_DK, RET_DK), F32)
        if stacked is not None:
            in_specs += [_HBM] * 4
            args += list(stacked)
            aliases = {len(args) - 4 + j: len(out_shape) + j for j in range(4)}
            out_specs += [pl.BlockSpec((G, None, TILE, DIFF_W), lambda t: (t, layer, 0, 0))] * 2
            out_specs += [pl.BlockSpec((G, None, H_RET, RET_DK, RET_DK), lambda t: (t, layer, 0, 0, 0))] * 2
        else:
            out_specs += [pl.BlockSpec((G, DEPTH, TILE, DIFF_W), lambda t: (t, 0, 0, 0))] * 2
            out_specs += [pl.BlockSpec((G, DEPTH, H_RET, RET_DK, RET_DK), lambda t: (t, 0, 0, 0, 0))] * 2
        out_shape += [kv_shape, kv_shape, st_shape, st_shape]
    outs = pl.pallas_call(
        functools.partial(_proj_kernel, latent=latent, n_aliased=len(aliases), layer=layer),
        grid=(tiles // G,),
        in_specs=in_specs,
        out_specs=out_specs,
        out_shape=out_shape,
        input_output_aliases=aliases,
        compiler_params=_params(("parallel",)),
        name="proj_lat" if latent else "proj_ctx",
    )(*args)
    token_outs = [o.reshape(S, T, o.shape[-1]) for o in outs[:9]]
    if latent:
        return token_outs + [o.reshape(S, n, RET_W, RET_W) for o in outs[9:]]
    return token_outs + list(outs[9:])


def _scan_kernel(kvf_ref, kvb_ref, s0f_ref, s0b_ref, lg_ref, sf_ref, sb_ref, *, n):
    lg = _log_sigmoid(lg_ref[...])
    dec = jnp.exp(float(TILE) * lg)
    same_head = (_iota((RET_W, RET_W), 0) >> 6) == (_iota((RET_W, RET_W), 1) >> 6)

    s = jnp.where(same_head, s0f_ref[...], 0.0)
    for c in range(n):
        sf_ref[c] = s
        s = s * dec[0:1] + jnp.where(same_head, kvf_ref[c], 0.0)
    s = jnp.where(same_head, s0b_ref[...], 0.0)
    for c in reversed(range(n)):
        sb_ref[c] = s
        s = s * dec[1:2] + jnp.where(same_head, kvb_ref[c], 0.0)


def _scan(kvf, kvb, s0f, s0b, lg_lanes):
    S, n = kvf.shape[:2]
    chunks = pl.BlockSpec((None, n, RET_W, RET_W), lambda s: (s, 0, 0, 0))
    one = pl.BlockSpec((None, RET_W, RET_W), lambda s: (s, 0, 0))
    return pl.pallas_call(
        functools.partial(_scan_kernel, n=n),
        grid=(S,),
        in_specs=[chunks, chunks, one, one, pl.BlockSpec((2, RET_W), lambda s: (0, 0))],
        out_specs=[chunks, chunks],
        out_shape=[jax.ShapeDtypeStruct((S, n, RET_W, RET_W), F32)] * 2,
        compiler_params=_params(("parallel",)),
        name="ret_scan",
    )(kvf, kvb, s0f, s0b, lg_lanes)


def _diff_scores(dq_ref, kv_refs, heads):
    lane = _iota((1, DIFF_V), 1)
    scores = []
    for hd in heads:
        lo = hd * DIFF_V
        q_h = dq_ref[:, lo:lo + DIFF_V]
        qs = jnp.concatenate([jnp.where(lane < DIFF_QK, q_h, 0), jnp.where(lane >= DIFF_QK, q_h, 0)], axis=0)
        scores.append(jnp.concatenate([_dot_nt(qs, k[:, lo:lo + DIFF_V]) for k, _ in kv_refs], axis=1))
    return jnp.concatenate(scores, axis=0)


def _diff_outputs(s, kv_refs, heads, lam):
    p = jnp.exp2(s - jnp.max(s, axis=-1, keepdims=True))

    def times_v(w, hd, width):
        acc = None
        start = 0
        for k, v in kv_refs:
            part = _dot(w[:, start:start + k.shape[0]], v[:, hd * 2 * DIFF_V:hd * 2 * DIFF_V + width])
            acc = part if acc is None else acc + part
            start += k.shape[0]
        return acc

    outs = []
    p = p.astype(BF)
    for n, hd in enumerate(heads):
        acc = times_v(p[n * 2 * TILE:(n + 1) * 2 * TILE], hd, 2 * DIFF_V)
        o0 = acc[0:TILE, 0:DIFF_V] * (1.0 / acc[0:TILE, DIFF_V:DIFF_V + 1])
        o1 = acc[TILE:, 0:DIFF_V] * (lam / acc[TILE:, DIFF_V:DIFF_V + 1])
        outs.append(o0 - o1)
    return outs


def _mixer_kernel(*refs, n_tiles, n_total, lam_init, cached):
    n_a = _n_token_mix_refs(cached)
    mix_even, mix_odd, decay_ref = refs[-3:]
    t = pl.program_id(0)
    tile_in_seq = lax.rem(jnp.minimum(t, n_total - 1), n_tiles)

    @pl.when(t == 0)
    def _():
        mix_odd[...] = jnp.zeros_like(mix_odd)
        lg = _log_sigmoid(refs[n_a - 5][...])
        dist = (_iota((TILE, TILE), 0) - _iota((TILE, TILE), 1)).astype(F32)
        diag2 = jnp.where(dist == 0.0, 2.0, 1.0)
        for hd in range(H_RET):
            lgf = lg[0:1, hd * RET_DK:hd * RET_DK + 1]
            lgb = lg[1:2, hd * RET_DK:hd * RET_DK + 1]
            decay_ref[hd] = jnp.exp(jnp.abs(dist) * jnp.where(dist > 0.0, lgf, lgb)) * diag2

    def step(mix_write, mix_read):
        mix_write[...] = _token_mix(*refs[:n_a], decay_ref, tile_in_seq=tile_in_seq, n_tiles=n_tiles,
                                    lam_init=lam_init, cached=cached)
        _project_route(mix_read[...], *refs[n_a:-3])

    @pl.when(lax.rem(t, 2) == 0)
    def _():
        step(mix_even, mix_odd)

    @pl.when(lax.rem(t, 2) == 1)
    def _():
        step(mix_odd, mix_even)


def _n_token_mix_refs(cached):
    return 18 + (2 if cached else 0)


def _token_mix(*refs, tile_in_seq, n_tiles, lam_init, cached):
    rq_ref, rk_ref, rv_ref, rg_ref, sf_ref, sb_ref, dq_ref = refs[:7]
    pos = 7
    kv_refs = []
    if cached:
        kv_refs.append((refs[pos], refs[pos + 1]))
        pos += 2
    kv_refs.append((refs[pos], refs[pos + 1]))
    pos += 2
    (cb_ref, cch_ref, cprev_ref, cnext_ref, lg_ref, lamp_ref, subg_ref, cw_ref, cbias_ref,
     decay_ref) = refs[pos:pos + 10]
    i = tile_in_seq

    lg = _log_sigmoid(lg_ref[...])
    head_of_lane = _iota((1, RET_W), 1) >> 6
    q = rq_ref[...]
    k = rk_ref[...]
    v = rv_ref[...]
    ret_o = jnp.zeros((TILE, RET_W), F32)
    for hd in range(H_RET):
        in_head = head_of_lane == hd
        sc = _dot_nt(jnp.where(in_head, q, 0), k)
        ret_o = ret_o + jnp.where(in_head, _dot((sc * decay_ref[hd]).astype(BF), v), 0.0)
    p = _iota((TILE, 1), 0).astype(F32)
    ret_o = ret_o + _dot(q, sf_ref[...].astype(BF)) * jnp.exp((p + 1.0) * lg[0:1])
    ret_o = ret_o + _dot(q, sb_ref[...].astype(BF)) * jnp.exp((float(TILE) - p) * lg[1:2])
    avg = jnp.where((_iota((RET_W, RET_W), 0) >> 6) == (_iota((RET_W, RET_W), 1) >> 6),
                    1.0 / RET_DK, 0.0).astype(BF)
    rc = ret_o - _dot_hl(ret_o, avg)
    ret = rc * lax.rsqrt(_dot_hl(rc * rc, avg) + EPS) * _silu(rg_ref[...])

    lp = lamp_ref[...]
    lam = (jnp.exp(jnp.sum(lp[0:1] * lp[1:2], axis=-1, keepdims=True))
           - jnp.exp(jnp.sum(lp[2:3] * lp[3:4], axis=-1, keepdims=True)) + lam_init)
    subg = subg_ref[...] * (1.0 - lam_init)
    head_groups = [[hd] for hd in range(H_DIFF)] if cached else [list(range(H_DIFF))]
    heads = []
    for group in head_groups:
        scores = _diff_scores(dq_ref, kv_refs, group)
        for o in _diff_outputs(scores, kv_refs, group, lam):
            o = o * lax.rsqrt(jnp.mean(o * o, axis=-1, keepdims=True) + EPS) * subg
            heads.append(o.astype(BF))
    diff = jnp.concatenate(heads, axis=1)

    cch = cch_ref[...]
    prev = jnp.where(i > 0, cprev_ref[7:8, :], 0.0)
    nxt = jnp.where(i < n_tiles - 1, cnext_ref[0:1, :], 0.0)
    r = _iota((TILE, 1), 0)
    up = jnp.where(r == 0, prev, pltpu.roll(cch, 1, axis=0))
    dn = jnp.where(r == TILE - 1, nxt, pltpu.roll(cch, TILE - 1, axis=0))
    cw = cw_ref[...]
    conv = cb_ref[...] * (up * cw[0:1] + cch * cw[1:2] + dn * cw[2:3] + cbias_ref[...])
    return jnp.concatenate([ret.astype(BF), diff, conv.astype(BF)], axis=1)


def _project_route(mixed, x_ref, mod_ref, wout_ref, lng_ref, lnb_ref, wrh_ref, wrl_ref, br_ref,
                   x1_ref, hs_ref, slots_ref, cnt_ref):
    m = mod_ref[...]
    g1 = m[:, 2 * D_MODEL:3 * D_MODEL]
    sh2 = m[:, 3 * D_MODEL:4 * D_MODEL]
    sc2 = m[:, 4 * D_MODEL:5 * D_MODEL]
    x1 = _layer_norm(ALPHA * x_ref[...] + g1 * _dot(mixed, wout_ref[...]), lng_ref[...], lnb_ref[...])
    x1_ref[...] = x1

    h2 = x1 * (1.0 + sc2) + sh2
    h2h, h2l = _split(h2)
    wrh = wrh_ref[...]
    logits = (_dot_nt(wrh, h2h) + (_dot_nt(wrh, h2l) + _dot_nt(wrl_ref[...], h2h))) + br_ref[...]
    neg = -jnp.inf
    g_id = _iota((EXPERT_ROW0, TILE), 0)
    gl = jnp.where(g_id < N_GROUPS, logits[0:EXPERT_ROW0], neg)
    gmax = jnp.max(gl, axis=0, keepdims=True)
    g_idx = jnp.min(jnp.where(gl == gmax, g_id, N_GROUPS), axis=0, keepdims=True)
    g_w = 1.0 / jnp.sum(jnp.exp(gl - gmax), axis=0, keepdims=True)
    e_id = _iota((N_EXPERTS, TILE), 0)
    el = jnp.where((e_id >> 2) == g_idx, logits[EXPERT_ROW0:EXPERT_ROW0 + N_EXPERTS], neg)
    v1 = jnp.max(el, axis=0, keepdims=True)
    i1 = jnp.min(jnp.where(el == v1, e_id, N_EXPERTS), axis=0, keepdims=True)
    el2 = jnp.where(e_id == i1, neg, el)
    v2 = jnp.max(el2, axis=0, keepdims=True)
    i2 = jnp.min(jnp.where(el2 == v2, e_id, N_EXPERTS), axis=0, keepdims=True)
    t = jnp.exp(v2 - v1)
    w1 = g_w / (1.0 + t)
    w2 = g_w * t / (1.0 + t)

    sel1 = e_id == i1
    sel2 = e_id == i2
    sel = jnp.where(sel1 | sel2, 1.0, 0.0)
    earlier = _ones_where(_iota((TILE, TILE), 0) < _iota((TILE, TILE), 1))
    rank = _dot(sel.astype(BF), earlier)
    cnt = jnp.sum(sel, axis=1, keepdims=True).astype(I32)
    padded = ((cnt + (CHUNK - 1)) >> 4) << 4
    incl = jnp.broadcast_to(padded, (N_EXPERTS, LANES))
    e_row = _iota((N_EXPERTS, LANES), 0)
    for step in (1, 2, 4, 8):
        incl = incl + jnp.where(e_row >= step, pltpu.roll(incl, step, axis=0), 0)
    seg_off = (incl[:, 0:1] - padded).astype(F32)
    spos = seg_off + rank
    slot1 = jnp.sum(jnp.where(sel1, spos, 0.0), axis=0, keepdims=True).astype(I32)
    slot2 = jnp.sum(jnp.where(sel2, spos, 0.0), axis=0, keepdims=True).astype(I32)
    s_id = _iota((CAP, TILE), 0)
    at1 = s_id == slot1
    at2 = s_id == slot2
    hs_ref[:, 0:D_MODEL] = _dot(_ones_where(at1 | at2), h2h).astype(BF)
    w1h = w1.astype(BF).astype(F32)
    w2h = w2.astype(BF).astype(F32)
    ones = jnp.ones((TILE, LANES), BF)
    g_hi = _dot((jnp.where(at1, w1h, 0.0) + jnp.where(at2, w2h, 0.0)).astype(BF), ones)
    g_lo = _dot((jnp.where(at1, w1 - w1h, 0.0) + jnp.where(at2, w2 - w2h, 0.0)).astype(BF), ones)
    lane = _iota((CAP, LANES), 1)
    hs_ref[:, D_MODEL:] = jnp.where(lane == 0, g_hi, jnp.where(lane == 1, g_lo, 0.0)).astype(BF)
    cnt_ref[...] = jnp.broadcast_to(padded, (N_EXPERTS, LANES))

    dr = _iota((LANES, TILE), 0)
    digits = jnp.where(dr == 0, slot1 & (SLOT_RADIX - 1), jnp.where(dr == 1, slot1 >> 5,
             jnp.where(dr == 2, slot2 & (SLOT_RADIX - 1), jnp.where(dr == 3, slot2 >> 5, 0))))
    eye = _ones_where(_iota((TILE, TILE), 0) == _iota((TILE, TILE), 1))
    cols = _dot_nt(eye, digits.astype(F32).astype(BF))
    s1c = (cols[:, 0:1] + SLOT_RADIX * cols[:, 1:2]).astype(I32)
    s2c = (cols[:, 2:3] + SLOT_RADIX * cols[:, 3:4]).astype(I32)
    tl = _iota((TILE, LANES), 1)
    slots_ref[...] = jnp.where(tl == 0, s1c, jnp.where(tl == 1, s2c, 0))


def _mixer(x, mod, layer, row_fn, pr, states, cache_kv, new_kv, w_out_b, lg_lanes, lamp, subg, cw, cbias,
           lng, lnb, wrh, wrl, br, lam_init):
    S, T, _ = x.shape
    n = T // TILE
    rq, rk, rv, rg, dq, cb, cch = pr
    t8 = TILE // 8
    cached = cache_kv is not None
    last = S * n - 1

    def at_mix(t):
        return jnp.minimum(t, last) // n, jnp.minimum(t, last) % n

    def at_out(t):
        return jnp.maximum(t - 1, 0) // n, jnp.maximum(t - 1, 0) % n

    def tok(w, at):
        return pl.BlockSpec((None, TILE, w), lambda t: (*at(t), 0))

    def full(shape):
        return pl.BlockSpec(shape, lambda t: (0,) * len(shape))

    def seq(a):
        mode = pl.Buffered(1) if n > 1 else None
        return pl.BlockSpec((None,) + a.shape[1:], lambda t: (at_mix(t)[0], 0, 0), pipeline_mode=mode)

    def halo(offset):
        def index(t):
            s, i = at_mix(t)
            return (s, jnp.clip(i * t8 + offset, 0, T // 8 - 1), 0)
        return pl.BlockSpec((None, 8, CONV_CH), index)

    if states is None:
        zero_state = jnp.zeros((1, 1, RET_W, RET_W), F32)
        states = (zero_state, zero_state)
        state = pl.BlockSpec((None, None, RET_W, RET_W), lambda t: (0, 0, 0, 0))
    else:
        state = pl.BlockSpec((None, None, RET_W, RET_W), lambda t: (*at_mix(t), 0, 0))

    in_specs = [tok(RET_W, at_mix)] * 4 + [state, state, tok(DIFF_W, at_mix)]
    args = [rq, rk, rv, rg, states[0], states[1], dq]
    for kv in ([cache_kv] if cached else []) + [new_kv]:
        in_specs += [seq(kv[0]), seq(kv[1])]
        args += list(kv)
    in_specs += [tok(CONV_CH, at_mix), tok(CONV_CH, at_mix), halo(-1), halo(t8),
                 full((2, RET_W)), full((4, DIFF_QK)), full((1, DIFF_V)), full((3, CONV_CH)), full((1, CONV_CH))]
    args += [cb, cch, cch, cch, lg_lanes, lamp, subg, cw, cbias]
    assert len(args) == _n_token_mix_refs(cached)
    in_specs += [tok(D_MODEL, at_out),
                 pl.BlockSpec((None, None, 1, 6 * D_MODEL), lambda t: (layer, row_fn(at_out(t)[0]), 0, 0)),
                 pl.BlockSpec((None, D_MODEL, D_MODEL), lambda t: (layer, 0, 0)),
                 full((1, D_MODEL)), full((1, D_MODEL)),
                 full((ROUTER_ROWS, D_MODEL)), full((ROUTER_ROWS, D_MODEL)), full((ROUTER_ROWS, 1))]
    args += [x, mod, w_out_b, lng, lnb, wrh, wrl, br]

    def out_tile(t):
        s, i = at_out(t)
        return s * n + i

    return pl.pallas_call(
        functools.partial(_mixer_kernel, n_tiles=n, n_total=S * n, lam_init=lam_init, cached=cached),
        grid=(S * n + 1,),
        in_specs=in_specs,
        out_specs=[tok(D_MODEL, at_out),
                   pl.BlockSpec((CAP, HS_W), lambda t: (out_tile(t), 0)),
                   tok(LANES, at_out),
                   pl.BlockSpec((None, None, N_EXPERTS, LANES), lambda t: (*at_out(t), 0, 0))],
        out_shape=[jax.ShapeDtypeStruct((S, T, D_MODEL), F32),
                   jax.ShapeDtypeStruct((S * n * CAP, HS_W), BF),
                   jax.ShapeDtypeStruct((S, T, LANES), I32),
                   jax.ShapeDtypeStruct((S, n, N_EXPERTS, LANES), I32)],
        scratch_shapes=[pltpu.VMEM((TILE, D_MODEL), BF), pltpu.VMEM((TILE, D_MODEL), BF),
                        pltpu.VMEM((H_RET, TILE, TILE), F32)],
        compiler_params=_params(("arbitrary",)),
        name="mixer_lat" if cached else "mixer_ctx",
    )(*args)


def _block_schedule(padded, chunks_a):
    NT = padded.shape[0]
    n_chunks = NT * CHUNKS_PER_TILE
    n_blocks = n_chunks // CPB + N_EXPERTS
    c16 = padded // CHUNK
    ends = jnp.cumsum(c16, axis=1)
    before = jnp.cumsum(c16, axis=0) - c16
    per_expert = jnp.sum(c16, axis=0)
    nb = (per_expert + (CPB - 1)) // CPB
    b_end = jnp.cumsum(nb)
    q = jnp.arange(CHUNKS_PER_TILE, dtype=I32)
    e = jnp.arange(N_EXPERTS, dtype=I32)
    key = jnp.sum((ends[:, None, :] <= q[None, :, None]).astype(I32), axis=-1)
    pos_e = ((b_end - nb) * CPB + before)[:, None, :] + (q[None, :, None] - (ends - c16)[:, None, :])
    pos = jnp.sum(jnp.where(key[..., None] == e, pos_e, 0), axis=-1)
    pos = jnp.where(key < N_EXPERTS, pos, -1).reshape(-1)
    match = pos[None, :] == jnp.arange(n_blocks * CPB, dtype=I32)[:, None]
    cid = jnp.sum(jnp.where(match, jnp.arange(n_chunks, dtype=I32)[None, :], 0), axis=-1)
    nv = jnp.sum(match.astype(I32).reshape(n_blocks, -1), axis=-1)
    first = match & (jnp.arange(n_chunks, dtype=I32) < chunks_a)[None, :]
    nv_a = jnp.sum(first.astype(I32).reshape(n_blocks, -1), axis=-1)
    b = jnp.arange(n_blocks, dtype=I32)
    eb = jnp.minimum(jnp.sum((b_end[None, :] <= b[:, None]).astype(I32), axis=-1), N_EXPERTS - 1)
    zero = jnp.zeros((1,), I32)
    counts = jnp.concatenate([nv, zero, nv_a, zero]).astype(I32)
    return eb.astype(I32), counts, jnp.concatenate([cid, jnp.zeros((CPB,), I32)]).astype(I32)


def _moe_kernel(eb_ref, nv_ref, cid_ref, hs_a, hs_b, wg_ref, wu_ref, wd_ref, ys_a, ys_b,
                xbuf, ybuf, wg_b, wu_b, wd_b, in_sem, out_sem, *, n_blocks, chunks_a):
    b = pl.program_id(0)

    def chunk_rows(c):
        return pl.ds(pl.multiple_of(c * CHUNK, CHUNK), CHUNK)

    def gather_copy(which, c, j, slot):
        src = (hs_a, hs_b)[which]
        return pltpu.make_async_copy(src.at[chunk_rows(c)], xbuf.at[slot, chunk_rows(j)], in_sem.at[slot])

    def scatter_copy(which, c, j, slot):
        dst = (ys_a, ys_b)[which]
        return pltpu.make_async_copy(ybuf.at[slot, chunk_rows(j)],
                                     dst.at[chunk_rows(c), pl.ds(0, D_MODEL)], out_sem.at[slot])

    def for_chunks(blk, fn):
        n_a = nv_ref[n_blocks + 1 + blk]

        def body_a(j, carry):
            fn(j, 0, cid_ref[blk * CPB + j])
            return carry

        def body_b(j, carry):
            fn(j, 1, cid_ref[blk * CPB + j] - chunks_a)
            return carry
        lax.fori_loop(0, n_a, body_a, 0)
        lax.fori_loop(n_a, nv_ref[blk], body_b, 0)

    def each_chunk(blk, fn):
        n_a = nv_ref[n_blocks + 1 + blk]
        n_v = nv_ref[blk]
        for j in range(CPB):
            c = cid_ref[blk * CPB + j]

            @pl.when(j < n_a)
            def _():
                fn(j, 0, c)

            @pl.when((j >= n_a) & (j < n_v))
            def _():
                fn(j, 1, c - chunks_a)

    def wait_gathers(blk, slot):
        rows = pl.ds(0, nv_ref[blk] * CHUNK)

        @pl.when(nv_ref[blk] > 0)
        def _():
            pltpu.make_async_copy(hs_a.at[rows], xbuf.at[slot, rows], in_sem.at[slot]).wait()

    def wait_scatters(blk, slot):
        rows = pl.ds(0, nv_ref[blk] * CHUNK)

        @pl.when(nv_ref[blk] > 0)
        def _():
            pltpu.make_async_copy(ybuf.at[slot, rows], ys_a.at[rows, pl.ds(0, D_MODEL)], out_sem.at[slot]).wait()

    slot = lax.rem(b, 2)
    other = 1 - slot

    @pl.when(b == 0)
    def _():
        xbuf[...] = jnp.zeros_like(xbuf)
        for_chunks(0, lambda j, w, c: gather_copy(w, c, j, 0).start())

    @pl.when((b == 0) | (eb_ref[b] != eb_ref[jnp.maximum(b - 1, 0)]))
    def _():
        wg_b[...] = wg_ref[...].astype(BF)
        wu_b[...] = wu_ref[...].astype(BF)
        wd_b[...] = wd_ref[...].astype(BF)

    wait_gathers(b, slot)

    @pl.when(b >= 2)
    def _():
        wait_scatters(b - 2, slot)

    @pl.when(nv_ref[b] > 0)
    def _():
        each_chunk(b + 1, lambda j, w, c: gather_copy(w, c, j, other).start())
        x = xbuf[slot, :, 0:D_MODEL]
        gate = (xbuf[slot, :, D_MODEL:D_MODEL + 1].astype(F32)
                + xbuf[slot, :, D_MODEL + 1:D_MODEL + 2].astype(F32))
        hg = _dot(x, wg_b[...])
        hu = _dot(x, wu_b[...])
        act = _silu(hg) * hu * gate
        ybuf[slot] = _dot(act.astype(BF), wd_b[...]).astype(BF)
        each_chunk(b, lambda j, w, c: scatter_copy(w, c, j, slot).start())

    @pl.when(b == n_blocks - 1)
    def _():
        if n_blocks >= 2:
            wait_scatters(b - 1, other)
        wait_scatters(b, slot)


def _moe(hs_a, hs_b, eb, nv, cid, layer, w_gate, w_up, w_down):
    n_blocks = eb.shape[0]

    def w_spec(a, c):
        return pl.BlockSpec((None, None, a, c), lambda b, eb, nv, cid: (layer, eb[b], 0, 0))

    grid_spec = pltpu.PrefetchScalarGridSpec(
        num_scalar_prefetch=3,
        grid=(n_blocks,),
        in_specs=[_HBM, _HBM, w_spec(D_MODEL, D_EXPERT), w_spec(D_MODEL, D_EXPERT), w_spec(D_EXPERT, D_MODEL)],
        out_specs=[_HBM, _HBM],
        scratch_shapes=[pltpu.VMEM((2, MOE_TMB, HS_W), BF), pltpu.VMEM((2, MOE_TMB, D_MODEL), BF),
                        pltpu.VMEM((D_MODEL, D_EXPERT), BF), pltpu.VMEM((D_MODEL, D_EXPERT), BF),
                        pltpu.VMEM((D_EXPERT, D_MODEL), BF),
                        pltpu.SemaphoreType.DMA((2,)), pltpu.SemaphoreType.DMA((2,))],
    )
    return pl.pallas_call(
        functools.partial(_moe_kernel, n_blocks=n_blocks, chunks_a=hs_a.shape[0] // CHUNK),
        grid_spec=grid_spec,
        out_shape=[jax.ShapeDtypeStruct(hs_a.shape, BF), jax.ShapeDtypeStruct(hs_b.shape, BF)],
        input_output_aliases={3: 0, 4: 1},
        compiler_params=_params(("arbitrary",)),
        name="moe_sorted",
    )(eb, nv, cid, hs_a, hs_b, w_gate, w_up, w_down)


def _combine_kernel(ys_ref, slots_ref, x1_ref, mod_ref, lng_ref, lnb_ref, o_ref):
    g2 = mod_ref[:, 5 * D_MODEL:6 * D_MODEL]
    s_lane = _iota((TILE, CAP), 1)
    for u in range(COMBINE_TILES_PER_STEP):
        sl = slots_ref[u]
        pick = _ones_where((s_lane == sl[:, 0:1]) | (s_lane == sl[:, 1:2]))
        ffn = _dot(pick, ys_ref[u * CAP:(u + 1) * CAP, 0:D_MODEL])
        o_ref[u] = _layer_norm(ALPHA * x1_ref[u] + g2 * ffn, lng_ref[...], lnb_ref[...])


def _combine(ys, slots, x1, mod, layer, row_fn, lng, lnb):
    S, T, _ = x1.shape
    n = T // TILE
    U = COMBINE_TILES_PER_STEP
    assert (S * n) % U == 0 and (n % U == 0 or n == 1)

    def tok(w):
        return pl.BlockSpec((U, TILE, w), lambda t: (t, 0, 0))

    out = pl.pallas_call(
        _combine_kernel,
        grid=(S * n // U,),
        in_specs=[
            pl.BlockSpec((U * CAP, HS_W), lambda t: (t, 0)),
            tok(LANES), tok(D_MODEL),
            pl.BlockSpec((None, None, 1, 6 * D_MODEL), lambda t: (layer, row_fn(t * U // n), 0, 0)),
            pl.BlockSpec((1, D_MODEL), lambda t: (0, 0)),
            pl.BlockSpec((1, D_MODEL), lambda t: (0, 0)),
        ],
        out_specs=tok(D_MODEL),
        out_shape=jax.ShapeDtypeStruct((S * n, TILE, D_MODEL), F32),
        compiler_params=_params(("parallel",)),
        name="moe_combine",
    )(ys, slots.reshape(S * n, TILE, LANES), x1.reshape(S * n, TILE, D_MODEL), mod, lng, lnb)
    return out.reshape(S, T, D_MODEL)


def _rope_tables(n_lat):
    half = DIFF_QK // 2
    pairs = half // 2
    inv = (1.0 / (ROPE_BASE ** (np.arange(pairs, dtype=np.float32) * 2.0 / half))).astype(np.float32)
    t = np.arange(n_lat)
    ang_r = ((t // GRID_W).astype(np.float32)[:, None] * inv[None, :]).astype(np.float64)
    ang_c = ((t % GRID_W).astype(np.float32)[:, None] * inv[None, :]).astype(np.float64)
    cos = np.concatenate([np.cos(ang_r)] * 2 + [np.cos(ang_c)] * 2, axis=1)
    sin = np.concatenate([-np.sin(ang_r), np.sin(ang_r), -np.sin(ang_c), np.sin(ang_c)], axis=1)
    reps = DIFF_W // DIFF_QK
    return (jnp.asarray(np.tile(cos, (1, reps)), F32), jnp.asarray(np.tile(sin, (1, reps)), F32))


def _block_diag(s):
    S = s.shape[0]
    eye = jnp.eye(H_RET, dtype=s.dtype)
    return jnp.einsum('shdv,hg->shdgv', s, eye).reshape(S, RET_W, RET_W)


def _router_rows(w_group, b_group, w_expert, b_expert):
    pad = EXPERT_ROW0 - N_GROUPS
    tail = ROUTER_ROWS - EXPERT_ROW0 - N_EXPERTS
    w = jnp.concatenate([w_group.T, jnp.zeros((pad, D_MODEL), F32),
                         w_expert.reshape(D_MODEL, N_EXPERTS).T, jnp.zeros((tail, D_MODEL), F32)], axis=0)
    bias = jnp.concatenate([b_group, jnp.zeros((pad,), F32), b_expert.reshape(N_EXPERTS), jnp.zeros((tail,), F32)])
    hi = w.astype(BF)
    return hi, (w - hi.astype(F32)).astype(BF), bias.reshape(ROUTER_ROWS, 1)


def kernel(x_prompt, x_sample, cache_diff_k, cache_diff_v, state_ret_fwd, state_ret_bwd, c, c_ctx, w_mod, b_mod, w_in, ret_decay_logit, diff_lambda, diff_subln_g, conv_w, conv_b, w_out, ln_g, ln_b, w_router_group, b_router_group, w_router_expert, b_router_expert, w_gate, w_up, w_down):
    B, T_ctx, _ = x_prompt.shape
    Bd, T_lat, _ = x_sample.shape
    assert T_ctx == TILE and T_lat % TILE == 0 and T_lat % GRID_W == 0
    assert 1 + Bd <= MOD_ROWS

    cond = jnp.concatenate([c_ctx[None, :], c, jnp.zeros((MOD_ROWS - 1 - Bd, D_MODEL), F32)], axis=0)
    mod = _modulation(cond, w_mod, b_mod).reshape(DEPTH, MOD_ROWS, 1, 6 * D_MODEL)
    rope_tabs = _rope_tables(T_lat)
    w_in_b = w_in.astype(BF)
    w_out_b = w_out.astype(BF)

    ctx_row = lambda s: 0
    lat_row = lambda s: s + 1
    ctx_tiles = B * (T_ctx // TILE)
    lat_tiles = Bd * (T_lat // TILE)

    yp, ys = x_prompt, x_sample
    stacked = None
    for l in range(DEPTH):
        lam_init = 0.8 - 0.6 * math.exp(-0.3 * l)
        lg_lanes = jnp.repeat(ret_decay_logit[l], RET_DK, axis=1)
        wrh, wrl, br = _router_rows(w_router_group[l], b_router_group[l], w_router_expert[l], b_router_expert[l])
        shared = (w_out_b, lg_lanes, diff_lambda[l], diff_subln_g[l].reshape(1, DIFF_V), conv_w[l],
                  conv_b[l].reshape(1, CONV_CH), ln_g[l, 0].reshape(1, D_MODEL), ln_b[l, 0].reshape(1, D_MODEL),
                  wrh, wrl, br, lam_init)
        ln2 = (ln_g[l, 1].reshape(1, D_MODEL), ln_b[l, 1].reshape(1, D_MODEL))

        pr = _proj(yp, mod, l, ctx_row, w_in_b, lg_lanes, stacked=stacked)
        rq, rk, rv, rg, dq, dk, dv, cb, cch = pr[:9]
        stacked = pr[9:]
        x1_c, hs_c, slots_c, cnt_c = _mixer(yp, mod, l, ctx_row, (rq, rk, rv, rg, dq, cb, cch), None, None,
                                            (dk, dv), *shared)

        pr = _proj(ys, mod, l, lat_row, w_in_b, lg_lanes, rope_tabs=rope_tabs)
        rq, rk, rv, rg, dq, dk, dv, cb, cch, kvf, kvb = pr
        states = _scan(kvf, kvb, _block_diag(state_ret_fwd[:, l]), _block_diag(state_ret_bwd[:, l]), lg_lanes)
        cache_v = cache_diff_v[:, l].astype(BF)
        cache_kv = (cache_diff_k[:, l].reshape(Bd, -1, DIFF_W).astype(BF),
                    jnp.concatenate([cache_v, jnp.ones_like(cache_v)], axis=-1).reshape(Bd, -1, V_AUG_W))
        x1_l, hs_l, slots_l, cnt_l = _mixer(ys, mod, l, lat_row, (rq, rk, rv, rg, dq, cb, cch), states, cache_kv,
                                            (dk, dv), *shared)

        padded = jnp.concatenate([cnt_c.reshape(ctx_tiles, N_EXPERTS, LANES)[:, :, 0],
                                  cnt_l.reshape(lat_tiles, N_EXPERTS, LANES)[:, :, 0]], axis=0)
        eb, nv, cid = _block_schedule(padded, ctx_tiles * CHUNKS_PER_TILE)
        out_c, out_l = _moe(hs_c, hs_l, eb, nv, cid, l, w_gate, w_up, w_down)
        yp = _combine(out_c, slots_c, x1_c, mod, l, ctx_row, *ln2)
        ys = _combine(out_l, slots_l, x1_l, mod, l, lat_row, *ln2)

    new_k, new_v, st_f, st_b = stacked
    return (yp, ys, new_k.reshape(B, DEPTH, T_ctx, H_DIFF, 2, DIFF_QK),
            new_v.reshape(B, DEPTH, T_ctx, H_DIFF, DIFF_V), st_f, st_b)
```

```python
import functools
import math

import numpy as np
import jax
import jax.numpy as jnp
from jax import lax
from jax.experimental import pallas as pl
from jax.experimental.pallas import tpu as pltpu

D_MODEL = 1024
DEPTH = 2
GRID_W = 64
H_RET = 4
RET_DK = 64
RET_W = H_RET * RET_DK
H_DIFF = 4
DIFF_QK = 64
DIFF_V = 2 * DIFF_QK
DIFF_W = H_DIFF * DIFF_V
CONV_CH = 256
ROPE_BASE = 10000.0
N_GROUPS = 4
EXPERTS_PER_GROUP = 4
N_EXPERTS = N_GROUPS * EXPERTS_PER_GROUP
D_EXPERT = 512
ALPHA = (2 * DEPTH) ** 0.25
EPS = 1e-5
MOD_ROWS = 8
LANES = 128
ROUTER_ROWS = 32
EXPERT_ROW0 = 8

TILE = 256
CHUNK = 16
CAP = 2 * TILE + N_EXPERTS * CHUNK
CHUNKS_PER_TILE = CAP // CHUNK
COMBINE_TILES_PER_STEP = 4
PROJ_TILES = 4
MOE_TMB = 1024
CPB = MOE_TMB // CHUNK
HS_W = D_MODEL + LANES
V_AUG_W = 2 * DIFF_W
QK_SCALE_LOG2 = (DIFF_QK ** -0.5) * math.log2(math.e)
SLOT_RADIX = 32

_O_RQ, _O_RK, _O_RV, _O_RG = 0, 256, 512, 768
_O_DQ, _O_DK, _O_DV = 1024, 1536, 2048
_O_CB, _O_CC, _O_CH = 2560, 2816, 3072
IN_WIDTH = 3328

BF = jnp.bfloat16
F32 = jnp.float32
I32 = jnp.int32

_VMEM_LIMIT = 56 * 1024 * 1024


def _dot(a, b):
    return jnp.dot(a, b, preferred_element_type=F32)


def _dot_nt(a, b):
    return lax.dot_general(a, b, (((1,), (1,)), ((), ())), preferred_element_type=F32)


def _dot_tn(a, b):
    return lax.dot_general(a, b, (((0,), (0,)), ((), ())), preferred_element_type=F32)


def _split(x):
    hi = x.astype(BF)
    lo = (x - hi.astype(F32)).astype(BF)
    return hi, lo


def _dot_hl(x, w_bf16):
    hi, lo = _split(x)
    return _dot(hi, w_bf16) + _dot(lo, w_bf16)


def _dot3(x, w_hi, w_lo):
    hi, lo = _split(x)
    return _dot(hi, w_hi) + (_dot(lo, w_hi) + _dot(hi, w_lo))


def _iota(shape, dim):
    return lax.broadcasted_iota(I32, shape, dim)


def _ones_where(cond):
    return jnp.where(cond, 1.0, 0.0).astype(BF)


def _log_sigmoid(x):
    return jnp.minimum(x, 0.0) - jnp.log1p(jnp.exp(-jnp.abs(x)))


def _silu(x):
    return x * jax.nn.sigmoid(x)


def _layer_norm(y, g, b):
    mu = jnp.mean(y, axis=-1, keepdims=True)
    yc = y - mu
    var = jnp.mean(yc * yc, axis=-1, keepdims=True)
    return yc * lax.rsqrt(var + EPS) * g + b


def _params(sem):
    return pltpu.CompilerParams(dimension_semantics=sem, vmem_limit_bytes=_VMEM_LIMIT)


_HBM = pl.BlockSpec(memory_space=pltpu.HBM)


def _mod_kernel(c_ref, w_ref, b_ref, o_ref):
    a = _silu(c_ref[...])
    w_hi, w_lo = _split(w_ref[...])
    o_ref[...] = _dot3(a, w_hi, w_lo) + b_ref[...]


def _modulation(cond, w_mod, b_mod):
    tn = 1536
    n6 = 6 * D_MODEL
    return pl.pallas_call(
        _mod_kernel,
        grid=(DEPTH, n6 // tn),
        in_specs=[
            pl.BlockSpec((MOD_ROWS, D_MODEL), lambda l, j: (0, 0)),
            pl.BlockSpec((None, D_MODEL, tn), lambda l, j: (l, 0, j)),
            pl.BlockSpec((None, 1, tn), lambda l, j: (l, 0, j)),
        ],
        out_specs=pl.BlockSpec((None, MOD_ROWS, tn), lambda l, j: (l, 0, j)),
        out_shape=jax.ShapeDtypeStruct((DEPTH, MOD_ROWS, n6), F32),
        compiler_params=_params(("parallel", "parallel")),
        name="modulation",
    )(cond, w_mod, b_mod.reshape(DEPTH, 1, n6))


def _mod_spec(layer, row_fn):
    return pl.BlockSpec((None, None, 1, 6 * D_MODEL), lambda s, i: (layer, row_fn(s), 0, 0))


def _augment_v(v):
    ones = jnp.ones((v.shape[0], DIFF_V), v.dtype)
    parts = []
    for hd in range(H_DIFF):
        parts += [v[:, hd * DIFF_V:(hd + 1) * DIFF_V], ones]
    return jnp.concatenate(parts, axis=1)


def _swap16(x):
    n = x.shape[-1]
    lane = _iota(x.shape, 1)
    return jnp.where((lane & 16) == 0, pltpu.roll(x, n - 16, axis=1), pltpu.roll(x, 16, axis=1))


def _proj_kernel(*refs, latent, n_aliased, layer):
    x_ref, mod_ref, w_ref, lg_ref = refs[:4]
    pos = 4
    if latent:
        cos_ref, sin_ref = refs[pos:pos + 2]
        pos += 2
    pos += n_aliased
    rq_ref, rk_ref, rv_ref, rg_ref, dq_ref, dk_ref, dv_ref, cb_ref, cch_ref = refs[pos:pos + 9]
    pos += 9
    if latent:
        kvf_ref, kvb_ref = refs[pos:pos + 2]
    else:
        dk32_ref, dv32_ref, stf_ref, stb_ref = refs[pos:pos + 4]
    rows = PROJ_TILES * TILE

    def put(ref, u, idx, val):
        if n_aliased:
            ref[(u,) + idx] = val
        else:
            for other in range(DEPTH):
                ref[(u, other) + idx] = val if other == layer else jnp.zeros_like(val)

    def store(ref, val):
        ref[...] = val.reshape(PROJ_TILES, TILE, val.shape[-1])

    m = mod_ref[...]
    sh1 = m[:, 0:D_MODEL]
    sc1 = m[:, D_MODEL:2 * D_MODEL]
    h = (x_ref[...].reshape(rows, D_MODEL) * (1.0 + sc1) + sh1).astype(BF)

    def col(off, width):
        return _dot(h, w_ref[:, off:off + width])

    store(rq_ref, col(_O_RQ, RET_W).astype(BF))
    rk = col(_O_RK, RET_W) * (RET_DK ** -0.5)
    store(rk_ref, rk.astype(BF))
    rv = col(_O_RV, RET_W).astype(BF)
    store(rv_ref, rv)
    store(rg_ref, col(_O_RG, RET_W))

    lg = _log_sigmoid(lg_ref[...])
    p = (_iota((rows, 1), 0) & (TILE - 1)).astype(F32)
    kf = (rk * jnp.exp((TILE - 1.0 - p) * lg[0:1])).astype(BF)
    kb = (rk * jnp.exp(p * lg[1:2])).astype(BF)
    for u in range(PROJ_TILES):
        tile = slice(u * TILE, (u + 1) * TILE)
        kvf = _dot_tn(kf[tile], rv[tile])
        kvb = _dot_tn(kb[tile], rv[tile])
        if latent:
            kvf_ref[u] = kvf
            kvb_ref[u] = kvb
        else:
            for hd in range(H_RET):
                lo = hd * RET_DK
                put(stf_ref, u, (hd,), kvf[lo:lo + RET_DK, lo:lo + RET_DK])
                put(stb_ref, u, (hd,), kvb[lo:lo + RET_DK, lo:lo + RET_DK])

    dq = col(_O_DQ, DIFF_W)
    dk = col(_O_DK, DIFF_W)
    dv = col(_O_DV, DIFF_W)
    if latent:
        cos = cos_ref[...].reshape(rows, DIFF_W)
        sin = sin_ref[...].reshape(rows, DIFF_W)
        dq = dq * cos + _swap16(dq) * sin
        dk = dk * cos + _swap16(dk) * sin
    else:
        for u in range(PROJ_TILES):
            put(dk32_ref, u, (), dk[u * TILE:(u + 1) * TILE])
            put(dv32_ref, u, (), dv[u * TILE:(u + 1) * TILE])
    store(dq_ref, (dq * QK_SCALE_LOG2).astype(BF))
    store(dk_ref, dk.astype(BF))
    store(dv_ref, _augment_v(dv.astype(BF)))

    store(cb_ref, col(_O_CB, CONV_CH))
    store(cch_ref, col(_O_CC, CONV_CH) * col(_O_CH, CONV_CH))


def _proj(x, mod, layer, row_fn, w_in_b, lg_lanes, rope_tabs=None, stacked=None):
    S, T, _ = x.shape
    n = T // TILE
    G = PROJ_TILES
    tiles = S * n
    latent = rope_tabs is not None
    assert tiles % G == 0 and (n % G == 0 or n == 1)

    def tok(w):
        return pl.BlockSpec((G, TILE, w), lambda t: (t, 0, 0))

    def tok_shape(w, dt):
        return jax.ShapeDtypeStruct((tiles, TILE, w), dt)

    in_specs = [
        tok(D_MODEL),
        pl.BlockSpec((None, None, 1, 6 * D_MODEL), lambda t: (layer, row_fn(t * G // n), 0, 0)),
        pl.BlockSpec((None, D_MODEL, IN_WIDTH), lambda t: (layer, 0, 0)),
        pl.BlockSpec((2, RET_W), lambda t: (0, 0)),
    ]
    args = [x.reshape(tiles, TILE, D_MODEL), mod, w_in_b, lg_lanes]
    out_specs = [tok(RET_W)] * 4 + [tok(DIFF_W)] * 2 + [tok(V_AUG_W)] + [tok(CONV_CH)] * 2
    out_shape = ([tok_shape(RET_W, BF)] * 3 + [tok_shape(RET_W, F32)] + [tok_shape(DIFF_W, BF)] * 2
                 + [tok_shape(V_AUG_W, BF)]
                 + [tok_shape(CONV_CH, F32)] * 2)
    aliases = {}
    if latent:
        in_specs += [pl.BlockSpec((G, TILE, DIFF_W), lambda t: (t % (n // G), 0, 0))] * 2
        args += [tab.reshape(n, TILE, DIFF_W) for tab in rope_tabs]
        out_specs += [pl.BlockSpec((G, RET_W, RET_W), lambda t: (t, 0, 0))] * 2
        out_shape += [jax.ShapeDtypeStruct((tiles, RET_W, RET_W), F32)] * 2
    else:
        assert n == 1
        kv_shape = jax.ShapeDtypeStruct((S, DEPTH, T, DIFF_W), F32)
        st_shape = jax.ShapeDtypeStruct((S, DEPTH, H_RET, RET_DK, RET_DK), F32)
        if stacked is not None:
            in_specs += [_HBM] * 4
            args += list(stacked)
            aliases = {len(args) - 4 + j: len(out_shape) + j for j in range(4)}
            out_specs += [pl.BlockSpec((G, None, TILE, DIFF_W), lambda t: (t, layer, 0, 0))] * 2
            out_specs += [pl.BlockSpec((G, None, H_RET, RET_DK, RET_DK), lambda t: (t, layer, 0, 0, 0))] * 2
        else:
            out_specs += [pl.BlockSpec((G, DEPTH, TILE, DIFF_W), lambda t: (t, 0, 0, 0))] * 2
            out_specs += [pl.BlockSpec((G, DEPTH, H_RET, RET_DK, RET_DK), lambda t: (t, 0, 0, 0, 0))] * 2
        out_shape += [kv_shape, kv_shape, st_shape, st_shape]
    outs = pl.pallas_call(
        functools.partial(_proj_kernel, latent=latent, n_aliased=len(aliases), layer=layer),
        grid=(tiles // G,),
        in_specs=in_specs,
        out_specs=out_specs,
        out_shape=out_shape,
        input_output_aliases=aliases,
        compiler_params=_params(("parallel",)),
        name="proj_lat" if latent else "proj_ctx",
    )(*args)
    token_outs = [o.reshape(S, T, o.shape[-1]) for o in outs[:9]]
    if latent:
        return token_outs + [o.reshape(S, n, RET_W, RET_W) for o in outs[9:]]
    return token_outs + list(outs[9:])


def _scan_kernel(kvf_ref, kvb_ref, s0f_ref, s0b_ref, lg_ref, sf_ref, sb_ref, *, n):
    lg = _log_sigmoid(lg_ref[...])
    dec = jnp.exp(float(TILE) * lg)
    same_head = (_iota((RET_W, RET_W), 0) >> 6) == (_iota((RET_W, RET_W), 1) >> 6)

    s = jnp.where(same_head, s0f_ref[...], 0.0)
    for c in range(n):
        sf_ref[c] = s
        s = s * dec[0:1] + jnp.where(same_head, kvf_ref[c], 0.0)
    s = jnp.where(same_head, s0b_ref[...], 0.0)
    for c in reversed(range(n)):
        sb_ref[c] = s
        s = s * dec[1:2] + jnp.where(same_head, kvb_ref[c], 0.0)


def _scan(kvf, kvb, s0f, s0b, lg_lanes):
    S, n = kvf.shape[:2]
    chunks = pl.BlockSpec((None, n, RET_W, RET_W), lambda s: (s, 0, 0, 0))
    one = pl.BlockSpec((None, RET_W, RET_W), lambda s: (s, 0, 0))
    return pl.pallas_call(
        functools.partial(_scan_kernel, n=n),
        grid=(S,),
        in_specs=[chunks, chunks, one, one, pl.BlockSpec((2, RET_W), lambda s: (0, 0))],
        out_specs=[chunks, chunks],
        out_shape=[jax.ShapeDtypeStruct((S, n, RET_W, RET_W), F32)] * 2,
        compiler_params=_params(("parallel",)),
        name="ret_scan",
    )(kvf, kvb, s0f, s0b, lg_lanes)


def _diff_scores(dq_ref, kv_refs, heads):
    lane = _iota((1, DIFF_V), 1)
    scores = []
    for hd in heads:
        lo = hd * DIFF_V
        q_h = dq_ref[:, lo:lo + DIFF_V]
        qs = jnp.concatenate([jnp.where(lane < DIFF_QK, q_h, 0), jnp.where(lane >= DIFF_QK, q_h, 0)], axis=0)
        scores.append(jnp.concatenate([_dot_nt(qs, k[:, lo:lo + DIFF_V]) for k, _ in kv_refs], axis=1))
    return jnp.concatenate(scores, axis=0)


def _diff_outputs(s, kv_refs, heads, lam):
    p = jnp.exp2(s - jnp.max(s, axis=-1, keepdims=True))

    def times_v(w, hd, width):
        acc = None
        start = 0
        for k, v in kv_refs:
            part = _dot(w[:, start:start + k.shape[0]], v[:, hd * 2 * DIFF_V:hd * 2 * DIFF_V + width])
            acc = part if acc is None else acc + part
            start += k.shape[0]
        return acc

    outs = []
    p = p.astype(BF)
    for n, hd in enumerate(heads):
        acc = times_v(p[n * 2 * TILE:(n + 1) * 2 * TILE], hd, 2 * DIFF_V)
        o0 = acc[0:TILE, 0:DIFF_V] * (1.0 / acc[0:TILE, DIFF_V:DIFF_V + 1])
        o1 = acc[TILE:, 0:DIFF_V] * (lam / acc[TILE:, DIFF_V:DIFF_V + 1])
        outs.append(o0 - o1)
    return outs


def _mixer_kernel(*refs, n_tiles, n_total, lam_init, cached):
    n_a = _n_token_mix_refs(cached)
    mix_even, mix_odd, decay_ref = refs[-3:]
    t = pl.program_id(0)
    tile_in_seq = lax.rem(jnp.minimum(t, n_total - 1), n_tiles)

    @pl.when(t == 0)
    def _():
        mix_odd[...] = jnp.zeros_like(mix_odd)
        lg = _log_sigmoid(refs[n_a - 5][...])
        dist = (_iota((TILE, TILE), 0) - _iota((TILE, TILE), 1)).astype(F32)
        diag2 = jnp.where(dist == 0.0, 2.0, 1.0)
        for hd in range(H_RET):
            lgf = lg[0:1, hd * RET_DK:hd * RET_DK + 1]
            lgb = lg[1:2, hd * RET_DK:hd * RET_DK + 1]
            decay_ref[hd] = jnp.exp(jnp.abs(dist) * jnp.where(dist > 0.0, lgf, lgb)) * diag2

    def step(mix_write, mix_read):
        mix_write[...] = _token_mix(*refs[:n_a], decay_ref, tile_in_seq=tile_in_seq, n_tiles=n_tiles,
                                    lam_init=lam_init, cached=cached)
        _project_route(mix_read[...], *refs[n_a:-3])

    @pl.when(lax.rem(t, 2) == 0)
    def _():
        step(mix_even, mix_odd)

    @pl.when(lax.rem(t, 2) == 1)
    def _():
        step(mix_odd, mix_even)


def _n_token_mix_refs(cached):
    return 18 + (2 if cached else 0)


def _token_mix(*refs, tile_in_seq, n_tiles, lam_init, cached):
    rq_ref, rk_ref, rv_ref, rg_ref, sf_ref, sb_ref, dq_ref = refs[:7]
    pos = 7
    kv_refs = []
    if cached:
        kv_refs.append((refs[pos], refs[pos + 1]))
        pos += 2
    kv_refs.append((refs[pos], refs[pos + 1]))
    pos += 2
    (cb_ref, cch_ref, cprev_ref, cnext_ref, lg_ref, lamp_ref, subg_ref, cw_ref, cbias_ref,
     decay_ref) = refs[pos:pos + 10]
    i = tile_in_seq

    lg = _log_sigmoid(lg_ref[...])
    head_of_lane = _iota((1, RET_W), 1) >> 6
    q = rq_ref[...]
    k = rk_ref[...]
    v = rv_ref[...]
    ret_o = jnp.zeros((TILE, RET_W), F32)
    for hd in range(H_RET):
        in_head = head_of_lane == hd
        sc = _dot_nt(jnp.where(in_head, q, 0), k)
        ret_o = ret_o + jnp.where(in_head, _dot((sc * decay_ref[hd]).astype(BF), v), 0.0)
    p = _iota((TILE, 1), 0).astype(F32)
    ret_o = ret_o + _dot(q, sf_ref[...].astype(BF)) * jnp.exp((p + 1.0) * lg[0:1])
    ret_o = ret_o + _dot(q, sb_ref[...].astype(BF)) * jnp.exp((float(TILE) - p) * lg[1:2])
    avg = jnp.where((_iota((RET_W, RET_W), 0) >> 6) == (_iota((RET_W, RET_W), 1) >> 6),
                    1.0 / RET_DK, 0.0).astype(BF)
    rc = ret_o - _dot_hl(ret_o, avg)
    ret = rc * lax.rsqrt(_dot_hl(rc * rc, avg) + EPS) * _silu(rg_ref[...])

    lp = lamp_ref[...]
    lam = (jnp.exp(jnp.sum(lp[0:1] * lp[1:2], axis=-1, keepdims=True))
           - jnp.exp(jnp.sum(lp[2:3] * lp[3:4], axis=-1, keepdims=True)) + lam_init)
    subg = subg_ref[...] * (1.0 - lam_init)
    head_groups = [[hd] for hd in range(H_DIFF)] if cached else [[0, 1], [2, 3]]
    heads = []
    for group in head_groups:
        scores = _diff_scores(dq_ref, kv_refs, group)
        for o in _diff_outputs(scores, kv_refs, group, lam):
            o = o * lax.rsqrt(jnp.mean(o * o, axis=-1, keepdims=True) + EPS) * subg
            heads.append(o.astype(BF))
    diff = jnp.concatenate(heads, axis=1)

    cch = cch_ref[...]
    prev = jnp.where(i > 0, cprev_ref[7:8, :], 0.0)
    nxt = jnp.where(i < n_tiles - 1, cnext_ref[0:1, :], 0.0)
    r = _iota((TILE, 1), 0)
    up = jnp.where(r == 0, prev, pltpu.roll(cch, 1, axis=0))
    dn = jnp.where(r == TILE - 1, nxt, pltpu.roll(cch, TILE - 1, axis=0))
    cw = cw_ref[...]
    conv = cb_ref[...] * (up * cw[0:1] + cch * cw[1:2] + dn * cw[2:3] + cbias_ref[...])
    return jnp.concatenate([ret.astype(BF), diff, conv.astype(BF)], axis=1)


def _project_route(mixed, x_ref, mod_ref, wout_ref, lng_ref, lnb_ref, wrh_ref, wrl_ref, br_ref,
                   x1_ref, hs_ref, slots_ref, cnt_ref):
    m = mod_ref[...]
    g1 = m[:, 2 * D_MODEL:3 * D_MODEL]
    sh2 = m[:, 3 * D_MODEL:4 * D_MODEL]
    sc2 = m[:, 4 * D_MODEL:5 * D_MODEL]
    x1 = _layer_norm(ALPHA * x_ref[...] + g1 * _dot(mixed, wout_ref[...]), lng_ref[...], lnb_ref[...])
    x1_ref[...] = x1

    h2 = x1 * (1.0 + sc2) + sh2
    h2h, h2l = _split(h2)
    wrh = wrh_ref[...]
    logits = (_dot_nt(wrh, h2h) + (_dot_nt(wrh, h2l) + _dot_nt(wrl_ref[...], h2h))) + br_ref[...]
    neg = -jnp.inf
    g_id = _iota((EXPERT_ROW0, TILE), 0)
    gl = jnp.where(g_id < N_GROUPS, logits[0:EXPERT_ROW0], neg)
    gmax = jnp.max(gl, axis=0, keepdims=True)
    g_idx = jnp.min(jnp.where(gl == gmax, g_id, N_GROUPS), axis=0, keepdims=True)
    g_w = 1.0 / jnp.sum(jnp.exp(gl - gmax), axis=0, keepdims=True)
    e_id = _iota((N_EXPERTS, TILE), 0)
    el = jnp.where((e_id >> 2) == g_idx, logits[EXPERT_ROW0:EXPERT_ROW0 + N_EXPERTS], neg)
    v1 = jnp.max(el, axis=0, keepdims=True)
    i1 = jnp.min(jnp.where(el == v1, e_id, N_EXPERTS), axis=0, keepdims=True)
    el2 = jnp.where(e_id == i1, neg, el)
    v2 = jnp.max(el2, axis=0, keepdims=True)
    i2 = jnp.min(jnp.where(el2 == v2, e_id, N_EXPERTS), axis=0, keepdims=True)
    t = jnp.exp(v2 - v1)
    w1 = g_w / (1.0 + t)
    w2 = g_w * t / (1.0 + t)

    sel1 = e_id == i1
    sel2 = e_id == i2
    sel = jnp.where(sel1 | sel2, 1.0, 0.0)
    earlier = _ones_where(_iota((TILE, TILE), 0) < _iota((TILE, TILE), 1))
    rank = _dot(sel.astype(BF), earlier)
    cnt = jnp.sum(sel, axis=1, keepdims=True).astype(I32)
    padded = ((cnt + (CHUNK - 1)) >> 4) << 4
    incl = jnp.broadcast_to(padded, (N_EXPERTS, LANES))
    e_row = _iota((N_EXPERTS, LANES), 0)
    for step in (1, 2, 4, 8):
        incl = incl + jnp.where(e_row >= step, pltpu.roll(incl, step, axis=0), 0)
    seg_off = (incl[:, 0:1] - padded).astype(F32)
    spos = seg_off + rank
    slot1 = jnp.sum(jnp.where(sel1, spos, 0.0), axis=0, keepdims=True).astype(I32)
    slot2 = jnp.sum(jnp.where(sel2, spos, 0.0), axis=0, keepdims=True).astype(I32)
    s_id = _iota((CAP, TILE), 0)
    at1 = s_id == slot1
    at2 = s_id == slot2
    hs_ref[:, 0:D_MODEL] = _dot(_ones_where(at1 | at2), h2h).astype(BF)
    w1h = w1.astype(BF).astype(F32)
    w2h = w2.astype(BF).astype(F32)
    ones = jnp.ones((TILE, LANES), BF)
    g_hi = _dot((jnp.where(at1, w1h, 0.0) + jnp.where(at2, w2h, 0.0)).astype(BF), ones)
    g_lo = _dot((jnp.where(at1, w1 - w1h, 0.0) + jnp.where(at2, w2 - w2h, 0.0)).astype(BF), ones)
    lane = _iota((CAP, LANES), 1)
    hs_ref[:, D_MODEL:] = jnp.where(lane == 0, g_hi, jnp.where(lane == 1, g_lo, 0.0)).astype(BF)
    cnt_ref[...] = jnp.broadcast_to(padded, (N_EXPERTS, LANES))

    dr = _iota((LANES, TILE), 0)
    digits = jnp.where(dr == 0, slot1 & (SLOT_RADIX - 1), jnp.where(dr == 1, slot1 >> 5,
             jnp.where(dr == 2, slot2 & (SLOT_RADIX - 1), jnp.where(dr == 3, slot2 >> 5, 0))))
    eye = _ones_where(_iota((TILE, TILE), 0) == _iota((TILE, TILE), 1))
    cols = _dot_nt(eye, digits.astype(F32).astype(BF))
    s1c = (cols[:, 0:1] + SLOT_RADIX * cols[:, 1:2]).astype(I32)
    s2c = (cols[:, 2:3] + SLOT_RADIX * cols[:, 3:4]).astype(I32)
    tl = _iota((TILE, LANES), 1)
    slots_ref[...] = jnp.where(tl == 0, s1c, jnp.where(tl == 1, s2c, 0))


def _mixer(x, mod, layer, row_fn, pr, states, cache_kv, new_kv, w_out_b, lg_lanes, lamp, subg, cw, cbias,
           lng, lnb, wrh, wrl, br, lam_init):
    S, T, _ = x.shape
    n = T // TILE
    rq, rk, rv, rg, dq, cb, cch = pr
    t8 = TILE // 8
    cached = cache_kv is not None
    last = S * n - 1

    def at_mix(t):
        return jnp.minimum(t, last) // n, jnp.minimum(t, last) % n

    def at_out(t):
        return jnp.maximum(t - 1, 0) // n, jnp.maximum(t - 1, 0) % n

    def tok(w, at):
        return pl.BlockSpec((None, TILE, w), lambda t: (*at(t), 0))

    def full(shape):
        return pl.BlockSpec(shape, lambda t: (0,) * len(shape))

    def seq(a):
        mode = pl.Buffered(1) if n > 1 else None
        return pl.BlockSpec((None,) + a.shape[1:], lambda t: (at_mix(t)[0], 0, 0), pipeline_mode=mode)

    def halo(offset):
        def index(t):
            s, i = at_mix(t)
            return (s, jnp.clip(i * t8 + offset, 0, T // 8 - 1), 0)
        return pl.BlockSpec((None, 8, CONV_CH), index)

    if states is None:
        zero_state = jnp.zeros((1, 1, RET_W, RET_W), F32)
        states = (zero_state, zero_state)
        state = pl.BlockSpec((None, None, RET_W, RET_W), lambda t: (0, 0, 0, 0))
    else:
        state = pl.BlockSpec((None, None, RET_W, RET_W), lambda t: (*at_mix(t), 0, 0))

    in_specs = [tok(RET_W, at_mix)] * 4 + [state, state, tok(DIFF_W, at_mix)]
    args = [rq, rk, rv, rg, states[0], states[1], dq]
    for kv in ([cache_kv] if cached else []) + [new_kv]:
        in_specs += [seq(kv[0]), seq(kv[1])]
        args += list(kv)
    in_specs += [tok(CONV_CH, at_mix), tok(CONV_CH, at_mix), halo(-1), halo(t8),
                 full((2, RET_W)), full((4, DIFF_QK)), full((1, DIFF_V)), full((3, CONV_CH)), full((1, CONV_CH))]
    args += [cb, cch, cch, cch, lg_lanes, lamp, subg, cw, cbias]
    assert len(args) == _n_token_mix_refs(cached)
    in_specs += [tok(D_MODEL, at_out),
                 pl.BlockSpec((None, None, 1, 6 * D_MODEL), lambda t: (layer, row_fn(at_out(t)[0]), 0, 0)),
                 pl.BlockSpec((None, D_MODEL, D_MODEL), lambda t: (layer, 0, 0)),
                 full((1, D_MODEL)), full((1, D_MODEL)),
                 full((ROUTER_ROWS, D_MODEL)), full((ROUTER_ROWS, D_MODEL)), full((ROUTER_ROWS, 1))]
    args += [x, mod, w_out_b, lng, lnb, wrh, wrl, br]

    def out_tile(t):
        s, i = at_out(t)
        return s * n + i

    return pl.pallas_call(
        functools.partial(_mixer_kernel, n_tiles=n, n_total=S * n, lam_init=lam_init, cached=cached),
        grid=(S * n + 1,),
        in_specs=in_specs,
        out_specs=[tok(D_MODEL, at_out),
                   pl.BlockSpec((CAP, HS_W), lambda t: (out_tile(t), 0)),
                   tok(LANES, at_out),
                   pl.BlockSpec((None, None, N_EXPERTS, LANES), lambda t: (*at_out(t), 0, 0))],
        out_shape=[jax.ShapeDtypeStruct((S, T, D_MODEL), F32),
                   jax.ShapeDtypeStruct((S * n * CAP, HS_W), BF),
                   jax.ShapeDtypeStruct((S, T, LANES), I32),
                   jax.ShapeDtypeStruct((S, n, N_EXPERTS, LANES), I32)],
        scratch_shapes=[pltpu.VMEM((TILE, D_MODEL), BF), pltpu.VMEM((TILE, D_MODEL), BF),
                        pltpu.VMEM((H_RET, TILE, TILE), F32)],
        compiler_params=_params(("arbitrary",)),
        name="mixer_lat" if cached else "mixer_ctx",
    )(*args)


def _block_schedule(padded, chunks_a):
    NT = padded.shape[0]
    n_chunks = NT * CHUNKS_PER_TILE
    n_blocks = n_chunks // CPB + N_EXPERTS
    c16 = padded // CHUNK
    ends = jnp.cumsum(c16, axis=1)
    before = jnp.cumsum(c16, axis=0) - c16
    per_expert = jnp.sum(c16, axis=0)
    nb = (per_expert + (CPB - 1)) // CPB
    b_end = jnp.cumsum(nb)
    q = jnp.arange(CHUNKS_PER_TILE, dtype=I32)
    e = jnp.arange(N_EXPERTS, dtype=I32)
    key = jnp.sum((ends[:, None, :] <= q[None, :, None]).astype(I32), axis=-1)
    pos_e = ((b_end - nb) * CPB + before)[:, None, :] + (q[None, :, None] - (ends - c16)[:, None, :])
    pos = jnp.sum(jnp.where(key[..., None] == e, pos_e, 0), axis=-1)
    pos = jnp.where(key < N_EXPERTS, pos, -1).reshape(-1)
    match = pos[None, :] == jnp.arange(n_blocks * CPB, dtype=I32)[:, None]
    cid = jnp.sum(jnp.where(match, jnp.arange(n_chunks, dtype=I32)[None, :], 0), axis=-1)
    nv = jnp.sum(match.astype(I32).reshape(n_blocks, -1), axis=-1)
    first = match & (jnp.arange(n_chunks, dtype=I32) < chunks_a)[None, :]
    nv_a = jnp.sum(first.astype(I32).reshape(n_blocks, -1), axis=-1)
    b = jnp.arange(n_blocks, dtype=I32)
    eb = jnp.minimum(jnp.sum((b_end[None, :] <= b[:, None]).astype(I32), axis=-1), N_EXPERTS - 1)
    zero = jnp.zeros((1,), I32)
    counts = jnp.concatenate([nv, zero, nv_a, zero]).astype(I32)
    return eb.astype(I32), counts, jnp.concatenate([cid, jnp.zeros((CPB,), I32)]).astype(I32)


def _moe_kernel(eb_ref, nv_ref, cid_ref, hs_a, hs_b, wg_ref, wu_ref, wd_ref, ys_a, ys_b,
                xbuf, ybuf, wg_b, wu_b, wd_b, in_sem, out_sem, *, n_blocks, chunks_a):
    b = pl.program_id(0)

    def chunk_rows(c):
        return pl.ds(pl.multiple_of(c * CHUNK, CHUNK), CHUNK)

    def gather_copy(which, c, j, slot):
        src = (hs_a, hs_b)[which]
        return pltpu.make_async_copy(src.at[chunk_rows(c)], xbuf.at[slot, chunk_rows(j)], in_sem.at[slot])

    def scatter_copy(which, c, j, slot):
        dst = (ys_a, ys_b)[which]
        return pltpu.make_async_copy(ybuf.at[slot, chunk_rows(j)],
                                     dst.at[chunk_rows(c), pl.ds(0, D_MODEL)], out_sem.at[slot])

    def for_chunks(blk, fn):
        n_a = nv_ref[n_blocks + 1 + blk]

        def body_a(j, carry):
            fn(j, 0, cid_ref[blk * CPB + j])
            return carry

        def body_b(j, carry):
            fn(j, 1, cid_ref[blk * CPB + j] - chunks_a)
            return carry
        lax.fori_loop(0, n_a, body_a, 0)
        lax.fori_loop(n_a, nv_ref[blk], body_b, 0)

    def each_chunk(blk, fn):
        n_a = nv_ref[n_blocks + 1 + blk]
        n_v = nv_ref[blk]
        for j in range(CPB):
            c = cid_ref[blk * CPB + j]

            @pl.when(j < n_a)
            def _():
                fn(j, 0, c)

            @pl.when((j >= n_a) & (j < n_v))
            def _():
                fn(j, 1, c - chunks_a)

    def wait_gathers(blk, slot):
        rows = pl.ds(0, nv_ref[blk] * CHUNK)

        @pl.when(nv_ref[blk] > 0)
        def _():
            pltpu.make_async_copy(hs_a.at[rows], xbuf.at[slot, rows], in_sem.at[slot]).wait()

    def wait_scatters(blk, slot):
        rows = pl.ds(0, nv_ref[blk] * CHUNK)

        @pl.when(nv_ref[blk] > 0)
        def _():
            pltpu.make_async_copy(ybuf.at[slot, rows], ys_a.at[rows, pl.ds(0, D_MODEL)], out_sem.at[slot]).wait()

    slot = lax.rem(b, 2)
    other = 1 - slot

    @pl.when(b == 0)
    def _():
        xbuf[...] = jnp.zeros_like(xbuf)
        for_chunks(0, lambda j, w, c: gather_copy(w, c, j, 0).start())

    @pl.when((b == 0) | (eb_ref[b] != eb_ref[jnp.maximum(b - 1, 0)]))
    def _():
        wg_b[...] = wg_ref[...].astype(BF)
        wu_b[...] = wu_ref[...].astype(BF)
        wd_b[...] = wd_ref[...].astype(BF)

    wait_gathers(b, slot)

    @pl.when(b >= 2)
    def _():
        wait_scatters(b - 2, slot)

    @pl.when(nv_ref[b] > 0)
    def _():
        each_chunk(b + 1, lambda j, w, c: gather_copy(w, c, j, other).start())
        x = xbuf[slot, :, 0:D_MODEL]
        gate = (xbuf[slot, :, D_MODEL:D_MODEL + 1].astype(F32)
                + xbuf[slot, :, D_MODEL + 1:D_MODEL + 2].astype(F32))
        hg = _dot(x, wg_b[...])
        hu = _dot(x, wu_b[...])
        act = _silu(hg) * hu * gate
        ybuf[slot] = _dot(act.astype(BF), wd_b[...]).astype(BF)
        each_chunk(b, lambda j, w, c: scatter_copy(w, c, j, slot).start())

    @pl.when(b == n_blocks - 1)
    def _():
        if n_blocks >= 2:
            wait_scatters(b - 1, other)
        wait_scatters(b, slot)


def _moe(hs_a, hs_b, eb, nv, cid, layer, w_gate, w_up, w_down):
    n_blocks = eb.shape[0]

    def w_spec(a, c):
        return pl.BlockSpec((None, None, a, c), lambda b, eb, nv, cid: (layer, eb[b], 0, 0))

    grid_spec = pltpu.PrefetchScalarGridSpec(
        num_scalar_prefetch=3,
        grid=(n_blocks,),
        in_specs=[_HBM, _HBM, w_spec(D_MODEL, D_EXPERT), w_spec(D_MODEL, D_EXPERT), w_spec(D_EXPERT, D_MODEL)],
        out_specs=[_HBM, _HBM],
        scratch_shapes=[pltpu.VMEM((2, MOE_TMB, HS_W), BF), pltpu.VMEM((2, MOE_TMB, D_MODEL), BF),
                        pltpu.VMEM((D_MODEL, D_EXPERT), BF), pltpu.VMEM((D_MODEL, D_EXPERT), BF),
                        pltpu.VMEM((D_EXPERT, D_MODEL), BF),
                        pltpu.SemaphoreType.DMA((2,)), pltpu.SemaphoreType.DMA((2,))],
    )
    return pl.pallas_call(
        functools.partial(_moe_kernel, n_blocks=n_blocks, chunks_a=hs_a.shape[0] // CHUNK),
        grid_spec=grid_spec,
        out_shape=[jax.ShapeDtypeStruct(hs_a.shape, BF), jax.ShapeDtypeStruct(hs_b.shape, BF)],
        input_output_aliases={3: 0, 4: 1},
        compiler_params=_params(("arbitrary",)),
        name="moe_sorted",
    )(eb, nv, cid, hs_a, hs_b, w_gate, w_up, w_down)


def _combine_kernel(ys_ref, slots_ref, x1_ref, mod_ref, lng_ref, lnb_ref, o_ref):
    g2 = mod_ref[:, 5 * D_MODEL:6 * D_MODEL]
    s_lane = _iota((TILE, CAP), 1)
    for u in range(COMBINE_TILES_PER_STEP):
        sl = slots_ref[u]
        pick = _ones_where((s_lane == sl[:, 0:1]) | (s_lane == sl[:, 1:2]))
        ffn = _dot(pick, ys_ref[u * CAP:(u + 1) * CAP, 0:D_MODEL])
        o_ref[u] = _layer_norm(ALPHA * x1_ref[u] + g2 * ffn, lng_ref[...], lnb_ref[...])


def _combine(ys, slots, x1, mod, layer, row_fn, lng, lnb):
    S, T, _ = x1.shape
    n = T // TILE
    U = COMBINE_TILES_PER_STEP
    assert (S * n) % U == 0 and (n % U == 0 or n == 1)

    def tok(w):
        return pl.BlockSpec((U, TILE, w), lambda t: (t, 0, 0))

    out = pl.pallas_call(
        _combine_kernel,
        grid=(S * n // U,),
        in_specs=[
            pl.BlockSpec((U * CAP, HS_W), lambda t: (t, 0)),
            tok(LANES), tok(D_MODEL),
            pl.BlockSpec((None, None, 1, 6 * D_MODEL), lambda t: (layer, row_fn(t * U // n), 0, 0)),
            pl.BlockSpec((1, D_MODEL), lambda t: (0, 0)),
            pl.BlockSpec((1, D_MODEL), lambda t: (0, 0)),
        ],
        out_specs=tok(D_MODEL),
        out_shape=jax.ShapeDtypeStruct((S * n, TILE, D_MODEL), F32),
        compiler_params=_params(("parallel",)),
        name="moe_combine",
    )(ys, slots.reshape(S * n, TILE, LANES), x1.reshape(S * n, TILE, D_MODEL), mod, lng, lnb)
    return out.reshape(S, T, D_MODEL)


def _rope_tables(n_lat):
    half = DIFF_QK // 2
    pairs = half // 2
    inv = (1.0 / (ROPE_BASE ** (np.arange(pairs, dtype=np.float32) * 2.0 / half))).astype(np.float32)
    t = np.arange(n_lat)
    ang_r = ((t // GRID_W).astype(np.float32)[:, None] * inv[None, :]).astype(np.float64)
    ang_c = ((t % GRID_W).astype(np.float32)[:, None] * inv[None, :]).astype(np.float64)
    cos = np.concatenate([np.cos(ang_r)] * 2 + [np.cos(ang_c)] * 2, axis=1)
    sin = np.concatenate([-np.sin(ang_r), np.sin(ang_r), -np.sin(ang_c), np.sin(ang_c)], axis=1)
    reps = DIFF_W // DIFF_QK
    return (jnp.asarray(np.tile(cos, (1, reps)), F32), jnp.asarray(np.tile(sin, (1, reps)), F32))


def _block_diag(s):
    S = s.shape[0]
    eye = jnp.eye(H_RET, dtype=s.dtype)
    return jnp.einsum('shdv,hg->shdgv', s, eye).reshape(S, RET_W, RET_W)


def _router_rows(w_group, b_group, w_expert, b_expert):
    pad = EXPERT_ROW0 - N_GROUPS
    tail = ROUTER_ROWS - EXPERT_ROW0 - N_EXPERTS
    w = jnp.concatenate([w_group.T, jnp.zeros((pad, D_MODEL), F32),
                         w_expert.reshape(D_MODEL, N_EXPERTS).T, jnp.zeros((tail, D_MODEL), F32)], axis=0)
    bias = jnp.concatenate([b_group, jnp.zeros((pad,), F32), b_expert.reshape(N_EXPERTS), jnp.zeros((tail,), F32)])
    hi = w.astype(BF)
    return hi, (w - hi.astype(F32)).astype(BF), bias.reshape(ROUTER_ROWS, 1)


def kernel(x_prompt, x_sample, cache_diff_k, cache_diff_v, state_ret_fwd, state_ret_bwd, c, c_ctx, w_mod, b_mod, w_in, ret_decay_logit, diff_lambda, diff_subln_g, conv_w, conv_b, w_out, ln_g, ln_b, w_router_group, b_router_group, w_router_expert, b_router_expert, w_gate, w_up, w_down):
    B, T_ctx, _ = x_prompt.shape
    Bd, T_lat, _ = x_sample.shape
    assert T_ctx == TILE and T_lat % TILE == 0 and T_lat % GRID_W == 0
    assert 1 + Bd <= MOD_ROWS

    cond = jnp.concatenate([c_ctx[None, :], c, jnp.zeros((MOD_ROWS - 1 - Bd, D_MODEL), F32)], axis=0)
    mod = _modulation(cond, w_mod, b_mod).reshape(DEPTH, MOD_ROWS, 1, 6 * D_MODEL)
    rope_tabs = _rope_tables(T_lat)
    w_in_b = w_in.astype(BF)
    w_out_b = w_out.astype(BF)

    ctx_row = lambda s: 0
    lat_row = lambda s: s + 1
    ctx_tiles = B * (T_ctx // TILE)
    lat_tiles = Bd * (T_lat // TILE)

    yp, ys = x_prompt, x_sample
    stacked = None
    for l in range(DEPTH):
        lam_init = 0.8 - 0.6 * math.exp(-0.3 * l)
        lg_lanes = jnp.repeat(ret_decay_logit[l], RET_DK, axis=1)
        wrh, wrl, br = _router_rows(w_router_group[l], b_router_group[l], w_router_expert[l], b_router_expert[l])
        shared = (w_out_b, lg_lanes, diff_lambda[l], diff_subln_g[l].reshape(1, DIFF_V), conv_w[l],
                  conv_b[l].reshape(1, CONV_CH), ln_g[l, 0].reshape(1, D_MODEL), ln_b[l, 0].reshape(1, D_MODEL),
                  wrh, wrl, br, lam_init)
        ln2 = (ln_g[l, 1].reshape(1, D_MODEL), ln_b[l, 1].reshape(1, D_MODEL))

        pr = _proj(yp, mod, l, ctx_row, w_in_b, lg_lanes, stacked=stacked)
        rq, rk, rv, rg, dq, dk, dv, cb, cch = pr[:9]
        stacked = pr[9:]
        x1_c, hs_c, slots_c, cnt_c = _mixer(yp, mod, l, ctx_row, (rq, rk, rv, rg, dq, cb, cch), None, None,
                                            (dk, dv), *shared)

        pr = _proj(ys, mod, l, lat_row, w_in_b, lg_lanes, rope_tabs=rope_tabs)
        rq, rk, rv, rg, dq, dk, dv, cb, cch, kvf, kvb = pr
        states = _scan(kvf, kvb, _block_diag(state_ret_fwd[:, l]), _block_diag(state_ret_bwd[:, l]), lg_lanes)
        cache_v = cache_diff_v[:, l].astype(BF)
        cache_kv = (cache_diff_k[:, l].reshape(Bd, -1, DIFF_W).astype(BF),
                    jnp.concatenate([cache_v, jnp.ones_like(cache_v)], axis=-1).reshape(Bd, -1, V_AUG_W))
        x1_l, hs_l, slots_l, cnt_l = _mixer(ys, mod, l, lat_row, (rq, rk, rv, rg, dq, cb, cch), states, cache_kv,
                                            (dk, dv), *shared)

        padded = jnp.concatenate([cnt_c.reshape(ctx_tiles, N_EXPERTS, LANES)[:, :, 0],
                                  cnt_l.reshape(lat_tiles, N_EXPERTS, LANES)[:, :, 0]], axis=0)
        eb, nv, cid = _block_schedule(padded, ctx_tiles * CHUNKS_PER_TILE)
        out_c, out_l = _moe(hs_c, hs_l, eb, nv, cid, l, w_gate, w_up, w_down)
        yp = _combine(out_c, slots_c, x1_c, mod, l, ctx_row, *ln2)
        ys = _combine(out_l, slots_l, x1_l, mod, l, lat_row, *ln2)

    new_k, new_v, st_f, st_b = stacked
    return (yp, ys, new_k.reshape(B, DEPTH, T_ctx, H_DIFF, 2, DIFF_QK),
            new_v.reshape(B, DEPTH, T_ctx, H_DIFF, DIFF_V), st_f, st_b)
```

```python
import functools
import math

import numpy as np
import jax
import jax.numpy as jnp
from jax import lax
from jax.experimental import pallas as pl
from jax.experimental.pallas import tpu as pltpu

D_MODEL = 1024
DEPTH = 2
GRID_W = 64
H_RET = 4
RET_DK = 64
RET_W = H_RET * RET_DK
H_DIFF = 4
DIFF_QK = 64
DIFF_V = 2 * DIFF_QK
DIFF_W = H_DIFF * DIFF_V
CONV_CH = 256
ROPE_BASE = 10000.0
N_GROUPS = 4
EXPERTS_PER_GROUP = 4
N_EXPERTS = N_GROUPS * EXPERTS_PER_GROUP
D_EXPERT = 512
ALPHA = (2 * DEPTH) ** 0.25
EPS = 1e-5
MOD_ROWS = 8
LANES = 128
ROUTER_ROWS = 32
EXPERT_ROW0 = 8

TILE = 256
CHUNK = 16
CAP = 2 * TILE + N_EXPERTS * CHUNK
CHUNKS_PER_TILE = CAP // CHUNK
COMBINE_TILES_PER_STEP = 4
PROJ_TILES = 4
MOE_TMB = 1024
CPB = MOE_TMB // CHUNK
HS_W = D_MODEL + LANES
V_AUG_W = 2 * DIFF_W
QK_SCALE_LOG2 = (DIFF_QK ** -0.5) * math.log2(math.e)
SLOT_RADIX = 32

_O_RQ, _O_RK, _O_RV, _O_RG = 0, 256, 512, 768
_O_DQ, _O_DK, _O_DV = 1024, 1536, 2048
_O_CB, _O_CC, _O_CH = 2560, 2816, 3072
IN_WIDTH = 3328

BF = jnp.bfloat16
F32 = jnp.float32
I32 = jnp.int32

_VMEM_LIMIT = 56 * 1024 * 1024


def _dot(a, b):
    return jnp.dot(a, b, preferred_element_type=F32)


def _dot_nt(a, b):
    return lax.dot_general(a, b, (((1,), (1,)), ((), ())), preferred_element_type=F32)


def _dot_tn(a, b):
    return lax.dot_general(a, b, (((0,), (0,)), ((), ())), preferred_element_type=F32)


def _split(x):
    hi = x.astype(BF)
    lo = (x - hi.astype(F32)).astype(BF)
    return hi, lo


def _dot_hl(x, w_bf16):
    hi, lo = _split(x)
    return _dot(hi, w_bf16) + _dot(lo, w_bf16)


def _dot3(x, w_hi, w_lo):
    hi, lo = _split(x)
    return _dot(hi, w_hi) + (_dot(lo, w_hi) + _dot(hi, w_lo))


def _iota(shape, dim):
    return lax.broadcasted_iota(I32, shape, dim)


def _ones_where(cond):
    return jnp.where(cond, 1.0, 0.0).astype(BF)


def _log_sigmoid(x):
    return jnp.minimum(x, 0.0) - jnp.log1p(jnp.exp(-jnp.abs(x)))


def _silu(x):
    return x * jax.nn.sigmoid(x)


def _layer_norm(y, g, b):
    mu = jnp.mean(y, axis=-1, keepdims=True)
    yc = y - mu
    var = jnp.mean(yc * yc, axis=-1, keepdims=True)
    return yc * lax.rsqrt(var + EPS) * g + b


def _params(sem):
    return pltpu.CompilerParams(dimension_semantics=sem, vmem_limit_bytes=_VMEM_LIMIT)


_HBM = pl.BlockSpec(memory_space=pltpu.HBM)


def _mod_kernel(c_ref, w_ref, b_ref, o_ref):
    a = _silu(c_ref[...])
    w_hi, w_lo = _split(w_ref[...])
    o_ref[...] = _dot3(a, w_hi, w_lo) + b_ref[...]


def _modulation(cond, w_mod, b_mod):
    tn = 1536
    n6 = 6 * D_MODEL
    return pl.pallas_call(
        _mod_kernel,
        grid=(DEPTH, n6 // tn),
        in_specs=[
            pl.BlockSpec((MOD_ROWS, D_MODEL), lambda l, j: (0, 0)),
            pl.BlockSpec((None, D_MODEL, tn), lambda l, j: (l, 0, j)),
            pl.BlockSpec((None, 1, tn), lambda l, j: (l, 0, j)),
        ],
        out_specs=pl.BlockSpec((None, MOD_ROWS, tn), lambda l, j: (l, 0, j)),
        out_shape=jax.ShapeDtypeStruct((DEPTH, MOD_ROWS, n6), F32),
        compiler_params=_params(("parallel", "parallel")),
        name="modulation",
    )(cond, w_mod, b_mod.reshape(DEPTH, 1, n6))


def _mod_spec(layer, row_fn):
    return pl.BlockSpec((None, None, 1, 6 * D_MODEL), lambda s, i: (layer, row_fn(s), 0, 0))


def _augment_v(v):
    ones = jnp.ones((v.shape[0], DIFF_V), v.dtype)
    parts = []
    for hd in range(H_DIFF):
        parts += [v[:, hd * DIFF_V:(hd + 1) * DIFF_V], ones]
    return jnp.concatenate(parts, axis=1)


def _swap16(x):
    n = x.shape[-1]
    lane = _iota(x.shape, 1)
    return jnp.where((lane & 16) == 0, pltpu.roll(x, n - 16, axis=1), pltpu.roll(x, 16, axis=1))


def _proj_kernel(*refs, latent, n_aliased, layer):
    x_ref, mod_ref, w_ref, lg_ref = refs[:4]
    pos = 4
    if latent:
        cos_ref, sin_ref = refs[pos:pos + 2]
        pos += 2
    pos += n_aliased
    rq_ref, rk_ref, rv_ref, rg_ref, dq_ref, dk_ref, dv_ref, cb_ref, cch_ref = refs[pos:pos + 9]
    pos += 9
    if latent:
        kvf_ref, kvb_ref = refs[pos:pos + 2]
    else:
        dk32_ref, dv32_ref, stf_ref, stb_ref = refs[pos:pos + 4]
    rows = PROJ_TILES * TILE

    def put(ref, u, idx, val):
        if n_aliased:
            ref[(u,) + idx] = val
        else:
            for other in range(DEPTH):
                ref[(u, other) + idx] = val if other == layer else jnp.zeros_like(val)

    def store(ref, val):
        ref[...] = val.reshape(PROJ_TILES, TILE, val.shape[-1])

    m = mod_ref[...]
    sh1 = m[:, 0:D_MODEL]
    sc1 = m[:, D_MODEL:2 * D_MODEL]
    h = (x_ref[...].reshape(rows, D_MODEL) * (1.0 + sc1) + sh1).astype(BF)

    def col(off, width):
        return _dot(h, w_ref[:, off:off + width])

    store(rq_ref, col(_O_RQ, RET_W).astype(BF))
    rk = col(_O_RK, RET_W) * (RET_DK ** -0.5)
    store(rk_ref, rk.astype(BF))
    rv = col(_O_RV, RET_W).astype(BF)
    store(rv_ref, rv)
    store(rg_ref, col(_O_RG, RET_W))

    lg = _log_sigmoid(lg_ref[...])
    p = (_iota((rows, 1), 0) & (TILE - 1)).astype(F32)
    kf = (rk * jnp.exp((TILE - 1.0 - p) * lg[0:1])).astype(BF)
    kb = (rk * jnp.exp(p * lg[1:2])).astype(BF)
    for u in range(PROJ_TILES):
        tile = slice(u * TILE, (u + 1) * TILE)
        kvf = _dot_tn(kf[tile], rv[tile])
        kvb = _dot_tn(kb[tile], rv[tile])
        if latent:
            kvf_ref[u] = kvf
            kvb_ref[u] = kvb
        else:
            for hd in range(H_RET):
                lo = hd * RET_DK
                put(stf_ref, u, (hd,), kvf[lo:lo + RET_DK, lo:lo + RET_DK])
                put(stb_ref, u, (hd,), kvb[lo:lo + RET_DK, lo:lo + RET_DK])

    dq = col(_O_DQ, DIFF_W)
    dk = col(_O_DK, DIFF_W)
    dv = col(_O_DV, DIFF_W)
    if latent:
        cos = cos_ref[...].reshape(rows, DIFF_W)
        sin = sin_ref[...].reshape(rows, DIFF_W)
        dq = dq * cos + _swap16(dq) * sin
        dk = dk * cos + _swap16(dk) * sin
    else:
        for u in range(PROJ_TILES):
            put(dk32_ref, u, (), dk[u * TILE:(u + 1) * TILE])
            put(dv32_ref, u, (), dv[u * TILE:(u + 1) * TILE])
    store(dq_ref, (dq * QK_SCALE_LOG2).astype(BF))
    store(dk_ref, dk.astype(BF))
    store(dv_ref, _augment_v(dv.astype(BF)))

    store(cb_ref, col(_O_CB, CONV_CH))
    store(cch_ref, col(_O_CC, CONV_CH) * col(_O_CH, CONV_CH))


def _proj(x, mod, layer, row_fn, w_in_b, lg_lanes, rope_tabs=None, stacked=None):
    S, T, _ = x.shape
    n = T // TILE
    G = PROJ_TILES
    tiles = S * n
    latent = rope_tabs is not None
    assert tiles % G == 0 and (n % G == 0 or n == 1)

    def tok(w):
        return pl.BlockSpec((G, TILE, w), lambda t: (t, 0, 0))

    def tok_shape(w, dt):
        return jax.ShapeDtypeStruct((tiles, TILE, w), dt)

    in_specs = [
        tok(D_MODEL),
        pl.BlockSpec((None, None, 1, 6 * D_MODEL), lambda t: (layer, row_fn(t * G // n), 0, 0)),
        pl.BlockSpec((None, D_MODEL, IN_WIDTH), lambda t: (layer, 0, 0)),
        pl.BlockSpec((2, RET_W), lambda t: (0, 0)),
    ]
    args = [x.reshape(tiles, TILE, D_MODEL), mod, w_in_b, lg_lanes]
    out_specs = [tok(RET_W)] * 4 + [tok(DIFF_W)] * 2 + [tok(V_AUG_W)] + [tok(CONV_CH)] * 2
    out_shape = ([tok_shape(RET_W, BF)] * 3 + [tok_shape(RET_W, F32)] + [tok_shape(DIFF_W, BF)] * 2
                 + [tok_shape(V_AUG_W, BF)]
                 + [tok_shape(CONV_CH, F32)] * 2)
    aliases = {}
    if latent:
        in_specs += [pl.BlockSpec((G, TILE, DIFF_W), lambda t: (t % (n // G), 0, 0))] * 2
        args += [tab.reshape(n, TILE, DIFF_W) for tab in rope_tabs]
        out_specs += [pl.BlockSpec((G, RET_W, RET_W), lambda t: (t, 0, 0))] * 2
        out_shape += [jax.ShapeDtypeStruct((tiles, RET_W, RET_W), F32)] * 2
    else:
        assert n == 1
        kv_shape = jax.ShapeDtypeStruct((S, DEPTH, T, DIFF_W), F32)
        st_shape = jax.ShapeDtypeStruct((S, DEPTH, H_RET, RET_DK, RET_DK), F32)
        if stacked is not None:
            in_specs += [_HBM] * 4
            args += list(stacked)
            aliases = {len(args) - 4 + j: len(out_shape) + j for j in range(4)}
            out_specs += [pl.BlockSpec((G, None, TILE, DIFF_W), lambda t: (t, layer, 0, 0))] * 2
            out_specs += [pl.BlockSpec((G, None, H_RET, RET_DK, RET_DK), lambda t: (t, layer, 0, 0, 0))] * 2
        else:
            out_specs += [pl.BlockSpec((G, DEPTH, TILE, DIFF_W), lambda t: (t, 0, 0, 0))] * 2
            out_specs += [pl.BlockSpec((G, DEPTH, H_RET, RET_DK, RET_DK), lambda t: (t, 0, 0, 0, 0))] * 2
        out_shape += [kv_shape, kv_shape, st_shape, st_shape]
    outs = pl.pallas_call(
        functools.partial(_proj_kernel, latent=latent, n_aliased=len(aliases), layer=layer),
        grid=(tiles // G,),
        in_specs=in_specs,
        out_specs=out_specs,
        out_shape=out_shape,
        input_output_aliases=aliases,
        compiler_params=_params(("parallel",)),
        name="proj_lat" if latent else "proj_ctx",
    )(*args)
    token_outs = [o.reshape(S, T, o.shape[-1]) for o in outs[:9]]
    if latent:
        return token_outs + [o.reshape(S, n, RET_W, RET_W) for o in outs[9:]]
    return token_outs + list(outs[9:])


def _scan_kernel(kvf_ref, kvb_ref, s0f_ref, s0b_ref, lg_ref, sf_ref, sb_ref, *, n):
    lg = _log_sigmoid(lg_ref[...])
    dec = jnp.exp(float(TILE) * lg)
    same_head = (_iota((RET_W, RET_W), 0) >> 6) == (_iota((RET_W, RET_W), 1) >> 6)

    s = jnp.where(same_head, s0f_ref[...], 0.0)
    for c in range(n):
        sf_ref[c] = s
        s = s * dec[0:1] + jnp.where(same_head, kvf_ref[c], 0.0)
    s = jnp.where(same_head, s0b_ref[...], 0.0)
    for c in reversed(range(n)):
        sb_ref[c] = s
        s = s * dec[1:2] + jnp.where(same_head, kvb_ref[c], 0.0)


def _scan(kvf, kvb, s0f, s0b, lg_lanes):
    S, n = kvf.shape[:2]
    chunks = pl.BlockSpec((None, n, RET_W, RET_W), lambda s: (s, 0, 0, 0))
    one = pl.BlockSpec((None, RET_W, RET_W), lambda s: (s, 0, 0))
    return pl.pallas_call(
        functools.partial(_scan_kernel, n=n),
        grid=(S,),
        in_specs=[chunks, chunks, one, one, pl.BlockSpec((2, RET_W), lambda s: (0, 0))],
        out_specs=[chunks, chunks],
        out_shape=[jax.ShapeDtypeStruct((S, n, RET_W, RET_W), F32)] * 2,
        compiler_params=_params(("parallel",)),
        name="ret_scan",
    )(kvf, kvb, s0f, s0b, lg_lanes)


def _diff_scores(dq_ref, kv_refs, heads):
    lane = _iota((1, DIFF_V), 1)
    scores = []
    for hd in heads:
        lo = hd * DIFF_V
        q_h = dq_ref[:, lo:lo + DIFF_V]
        qs = jnp.concatenate([jnp.where(lane < DIFF_QK, q_h, 0), jnp.where(lane >= DIFF_QK, q_h, 0)], axis=0)
        scores.append(jnp.concatenate([_dot_nt(qs, k[:, lo:lo + DIFF_V]) for k, _ in kv_refs], axis=1))
    return jnp.concatenate(scores, axis=0)


def _diff_outputs(s, kv_refs, heads, lam):
    p = jnp.exp2(s - jnp.max(s, axis=-1, keepdims=True))

    def times_v(w, hd, width):
        acc = None
        start = 0
        for k, v in kv_refs:
            part = _dot(w[:, start:start + k.shape[0]], v[:, hd * 2 * DIFF_V:hd * 2 * DIFF_V + width])
            acc = part if acc is None else acc + part
            start += k.shape[0]
        return acc

    outs = []
    p = p.astype(BF)
    for n, hd in enumerate(heads):
        acc = times_v(p[n * 2 * TILE:(n + 1) * 2 * TILE], hd, 2 * DIFF_V)
        o0 = acc[0:TILE, 0:DIFF_V] * (1.0 / acc[0:TILE, DIFF_V:DIFF_V + 1])
        o1 = acc[TILE:, 0:DIFF_V] * (lam / acc[TILE:, DIFF_V:DIFF_V + 1])
        outs.append(o0 - o1)
    return outs


def _mixer_kernel(*refs, n_tiles, n_total, lam_init, cached):
    n_a = _n_token_mix_refs(cached)
    mix_even, mix_odd, decay_ref = refs[-3:]
    t = pl.program_id(0)
    tile_in_seq = lax.rem(jnp.minimum(t, n_total - 1), n_tiles)

    @pl.when(t == 0)
    def _():
        mix_odd[...] = jnp.zeros_like(mix_odd)
        lg = _log_sigmoid(refs[n_a - 5][...])
        dist = (_iota((TILE, TILE), 0) - _iota((TILE, TILE), 1)).astype(F32)
        diag2 = jnp.where(dist == 0.0, 2.0, 1.0)
        for hd in range(H_RET):
            lgf = lg[0:1, hd * RET_DK:hd * RET_DK + 1]
            lgb = lg[1:2, hd * RET_DK:hd * RET_DK + 1]
            decay_ref[hd] = jnp.exp(jnp.abs(dist) * jnp.where(dist > 0.0, lgf, lgb)) * diag2

    def step(mix_write, mix_read):
        mix_write[...] = _token_mix(*refs[:n_a], decay_ref, tile_in_seq=tile_in_seq, n_tiles=n_tiles,
                                    lam_init=lam_init, cached=cached)
        _project_route(mix_read[...], *refs[n_a:-3])

    @pl.when(lax.rem(t, 2) == 0)
    def _():
        step(mix_even, mix_odd)

    @pl.when(lax.rem(t, 2) == 1)
    def _():
        step(mix_odd, mix_even)


def _n_token_mix_refs(cached):
    return 18 + (2 if cached else 0)


def _token_mix(*refs, tile_in_seq, n_tiles, lam_init, cached):
    rq_ref, rk_ref, rv_ref, rg_ref, sf_ref, sb_ref, dq_ref = refs[:7]
    pos = 7
    kv_refs = []
    if cached:
        kv_refs.append((refs[pos], refs[pos + 1]))
        pos += 2
    kv_refs.append((refs[pos], refs[pos + 1]))
    pos += 2
    (cb_ref, cch_ref, cprev_ref, cnext_ref, lg_ref, lamp_ref, subg_ref, cw_ref, cbias_ref,
     decay_ref) = refs[pos:pos + 10]
    i = tile_in_seq

    lg = _log_sigmoid(lg_ref[...])
    head_of_lane = _iota((1, RET_W), 1) >> 6
    q = rq_ref[...]
    k = rk_ref[...]
    v = rv_ref[...]
    ret_o = jnp.zeros((TILE, RET_W), F32)
    for hd in range(H_RET):
        in_head = head_of_lane == hd
        sc = _dot_nt(jnp.where(in_head, q, 0), k)
        ret_o = ret_o + jnp.where(in_head, _dot((sc * decay_ref[hd]).astype(BF), v), 0.0)
    p = _iota((TILE, 1), 0).astype(F32)
    ret_o = ret_o + _dot(q, sf_ref[...].astype(BF)) * jnp.exp((p + 1.0) * lg[0:1])
    ret_o = ret_o + _dot(q, sb_ref[...].astype(BF)) * jnp.exp((float(TILE) - p) * lg[1:2])
    avg = jnp.where((_iota((RET_W, RET_W), 0) >> 6) == (_iota((RET_W, RET_W), 1) >> 6),
                    1.0 / RET_DK, 0.0).astype(BF)
    rc = ret_o - _dot_hl(ret_o, avg)
    ret = rc * lax.rsqrt(_dot_hl(rc * rc, avg) + EPS) * _silu(rg_ref[...])

    lp = lamp_ref[...]
    lam = (jnp.exp(jnp.sum(lp[0:1] * lp[1:2], axis=-1, keepdims=True))
           - jnp.exp(jnp.sum(lp[2:3] * lp[3:4], axis=-1, keepdims=True)) + lam_init)
    subg = subg_ref[...] * (1.0 - lam_init)
    head_groups = [[hd] for hd in range(H_DIFF)] if cached else [list(range(H_DIFF))]
    heads = []
    for group in head_groups:
        scores = _diff_scores(dq_ref, kv_refs, group)
        for o in _diff_outputs(scores, kv_refs, group, lam):
            o = o * lax.rsqrt(jnp.mean(o * o, axis=-1, keepdims=True) + EPS) * subg
            heads.append(o.astype(BF))
    diff = jnp.concatenate(heads, axis=1)

    cch = cch_ref[...]
    prev = jnp.where(i > 0, cprev_ref[7:8, :], 0.0)
    nxt = jnp.where(i < n_tiles - 1, cnext_ref[0:1, :], 0.0)
    r = _iota((TILE, 1), 0)
    up = jnp.where(r == 0, prev, pltpu.roll(cch, 1, axis=0))
    dn = jnp.where(r == TILE - 1, nxt, pltpu.roll(cch, TILE - 1, axis=0))
    cw = cw_ref[...]
    conv = cb_ref[...] * (up * cw[0:1] + cch * cw[1:2] + dn * cw[2:3] + cbias_ref[...])
    return jnp.concatenate([ret.astype(BF), diff, conv.astype(BF)], axis=1)


def _project_route(mixed, x_ref, mod_ref, wout_ref, lng_ref, lnb_ref, wrh_ref, wrl_ref, br_ref,
                   x1_ref, hs_ref, slots_ref, cnt_ref):
    m = mod_ref[...]
    g1 = m[:, 2 * D_MODEL:3 * D_MODEL]
    sh2 = m[:, 3 * D_MODEL:4 * D_MODEL]
    sc2 = m[:, 4 * D_MODEL:5 * D_MODEL]
    x1 = _layer_norm(ALPHA * x_ref[...] + g1 * _dot(mixed, wout_ref[...]), lng_ref[...], lnb_ref[...])
    x1_ref[...] = x1

    h2 = x1 * (1.0 + sc2) + sh2
    h2h, h2l = _split(h2)
    wrh = wrh_ref[...]
    logits = (_dot_nt(wrh, h2h) + (_dot_nt(wrh, h2l) + _dot_nt(wrl_ref[...], h2h))) + br_ref[...]
    neg = -jnp.inf
    g_id = _iota((EXPERT_ROW0, TILE), 0)
    gl = jnp.where(g_id < N_GROUPS, logits[0:EXPERT_ROW0], neg)
    gmax = jnp.max(gl, axis=0, keepdims=True)
    g_idx = jnp.min(jnp.where(gl == gmax, g_id, N_GROUPS), axis=0, keepdims=True)
    g_w = 1.0 / jnp.sum(jnp.exp(gl - gmax), axis=0, keepdims=True)
    e_id = _iota((N_EXPERTS, TILE), 0)
    el = jnp.where((e_id >> 2) == g_idx, logits[EXPERT_ROW0:EXPERT_ROW0 + N_EXPERTS], neg)
    v1 = jnp.max(el, axis=0, keepdims=True)
    i1 = jnp.min(jnp.where(el == v1, e_id, N_EXPERTS), axis=0, keepdims=True)
    el2 = jnp.where(e_id == i1, neg, el)
    v2 = jnp.max(el2, axis=0, keepdims=True)
    i2 = jnp.min(jnp.where(el2 == v2, e_id, N_EXPERTS), axis=0, keepdims=True)
    t = jnp.exp(v2 - v1)
    w1 = g_w / (1.0 + t)
    w2 = g_w * t / (1.0 + t)

    sel1 = e_id == i1
    sel2 = e_id == i2
    sel = jnp.where(sel1 | sel2, 1.0, 0.0)
    earlier = _ones_where(_iota((TILE, TILE), 0) < _iota((TILE, TILE), 1))
    rank = _dot(sel.astype(BF), earlier)
    cnt = jnp.sum(sel, axis=1, keepdims=True).astype(I32)
    padded = ((cnt + (CHUNK - 1)) >> 4) << 4
    incl = jnp.broadcast_to(padded, (N_EXPERTS, LANES))
    e_row = _iota((N_EXPERTS, LANES), 0)
    for step in (1, 2, 4, 8):
        incl = incl + jnp.where(e_row >= step, pltpu.roll(incl, step, axis=0), 0)
    seg_off = (incl[:, 0:1] - padded).astype(F32)
    spos = seg_off + rank
    slot1 = jnp.sum(jnp.where(sel1, spos, 0.0), axis=0, keepdims=True).astype(I32)
    slot2 = jnp.sum(jnp.where(sel2, spos, 0.0), axis=0, keepdims=True).astype(I32)
    s_id = _iota((CAP, TILE), 0)
    at1 = s_id == slot1
    at2 = s_id == slot2
    hs_ref[:, 0:D_MODEL] = _dot(_ones_where(at1 | at2), h2h).astype(BF)
    w1h = w1.astype(BF).astype(F32)
    w2h = w2.astype(BF).astype(F32)
    ones = jnp.ones((TILE, LANES), BF)
    g_hi = _dot((jnp.where(at1, w1h, 0.0) + jnp.where(at2, w2h, 0.0)).astype(BF), ones)
    g_lo = _dot((jnp.where(at1, w1 - w1h, 0.0) + jnp.where(at2, w2 - w2h, 0.0)).astype(BF), ones)
    lane = _iota((CAP, LANES), 1)
    hs_ref[:, D_MODEL:] = jnp.where(lane == 0, g_hi, jnp.where(lane == 1, g_lo, 0.0)).astype(BF)
    cnt_ref[...] = jnp.broadcast_to(padded, (N_EXPERTS, LANES))

    dr = _iota((LANES, TILE), 0)
    digits = jnp.where(dr == 0, slot1 & (SLOT_RADIX - 1), jnp.where(dr == 1, slot1 >> 5,
             jnp.where(dr == 2, slot2 & (SLOT_RADIX - 1), jnp.where(dr == 3, slot2 >> 5, 0))))
    eye = _ones_where(_iota((TILE, TILE), 0) == _iota((TILE, TILE), 1))
    cols = _dot_nt(eye, digits.astype(F32).astype(BF))
    s1c = (cols[:, 0:1] + SLOT_RADIX * cols[:, 1:2]).astype(I32)
    s2c = (cols[:, 2:3] + SLOT_RADIX * cols[:, 3:4]).astype(I32)
    tl = _iota((TILE, LANES), 1)
    slots_ref[...] = jnp.where(tl == 0, s1c, jnp.where(tl == 1, s2c, 0))


def _mixer(x, mod, layer, row_fn, pr, states, cache_kv, new_kv, w_out_b, lg_lanes, lamp, subg, cw, cbias,
           lng, lnb, wrh, wrl, br, lam_init):
    S, T, _ = x.shape
    n = T // TILE
    rq, rk, rv, rg, dq, cb, cch = pr
    t8 = TILE // 8
    cached = cache_kv is not None
    last = S * n - 1

    def at_mix(t):
        return jnp.minimum(t, last) // n, jnp.minimum(t, last) % n

    def at_out(t):
        return jnp.maximum(t - 1, 0) // n, jnp.maximum(t - 1, 0) % n

    def tok(w, at):
        return pl.BlockSpec((None, TILE, w), lambda t: (*at(t), 0))

    def full(shape):
        return pl.BlockSpec(shape, lambda t: (0,) * len(shape))

    def seq(a):
        mode = pl.Buffered(1) if n > 1 else None
        return pl.BlockSpec((None,) + a.shape[1:], lambda t: (at_mix(t)[0], 0, 0), pipeline_mode=mode)

    def halo(offset):
        def index(t):
            s, i = at_mix(t)
            return (s, jnp.clip(i * t8 + offset, 0, T // 8 - 1), 0)
        return pl.BlockSpec((None, 8, CONV_CH), index)

    if states is None:
        zero_state = jnp.zeros((1, 1, RET_W, RET_W), F32)
        states = (zero_state, zero_state)
        state = pl.BlockSpec((None, None, RET_W, RET_W), lambda t: (0, 0, 0, 0))
    else:
        state = pl.BlockSpec((None, None, RET_W, RET_W), lambda t: (*at_mix(t), 0, 0))

    in_specs = [tok(RET_W, at_mix)] * 4 + [state, state, tok(DIFF_W, at_mix)]
    args = [rq, rk, rv, rg, states[0], states[1], dq]
    for kv in ([cache_kv] if cached else []) + [new_kv]:
        in_specs += [seq(kv[0]), seq(kv[1])]
        args += list(kv)
    in_specs += [tok(CONV_CH, at_mix), tok(CONV_CH, at_mix), halo(-1), halo(t8),
                 full((2, RET_W)), full((4, DIFF_QK)), full((1, DIFF_V)), full((3, CONV_CH)), full((1, CONV_CH))]
    args += [cb, cch, cch, cch, lg_lanes, lamp, subg, cw, cbias]
    assert len(args) == _n_token_mix_refs(cached)
    in_specs += [tok(D_MODEL, at_out),
                 pl.BlockSpec((None, None, 1, 6 * D_MODEL), lambda t: (layer, row_fn(at_out(t)[0]), 0, 0)),
                 pl.BlockSpec((None, D_MODEL, D_MODEL), lambda t: (layer, 0, 0)),
                 full((1, D_MODEL)), full((1, D_MODEL)),
                 full((ROUTER_ROWS, D_MODEL)), full((ROUTER_ROWS, D_MODEL)), full((ROUTER_ROWS, 1))]
    args += [x, mod, w_out_b, lng, lnb, wrh, wrl, br]

    def out_tile(t):
        s, i = at_out(t)
        return s * n + i

    return pl.pallas_call(
        functools.partial(_mixer_kernel, n_tiles=n, n_total=S * n, lam_init=lam_init, cached=cached),
        grid=(S * n + 1,),
        in_specs=in_specs,
        out_specs=[tok(D_MODEL, at_out),
                   pl.BlockSpec((CAP, HS_W), lambda t: (out_tile(t), 0)),
                   tok(LANES, at_out),
                   pl.BlockSpec((None, None, N_EXPERTS, LANES), lambda t: (*at_out(t), 0, 0))],
        out_shape=[jax.ShapeDtypeStruct((S, T, D_MODEL), F32),
                   jax.ShapeDtypeStruct((S * n * CAP, HS_W), BF),
                   jax.ShapeDtypeStruct((S, T, LANES), I32),
                   jax.ShapeDtypeStruct((S, n, N_EXPERTS, LANES), I32)],
        scratch_shapes=[pltpu.VMEM((TILE, D_MODEL), BF), pltpu.VMEM((TILE, D_MODEL), BF),
                        pltpu.VMEM((H_RET, TILE, TILE), F32)],
        compiler_params=_params(("arbitrary",)),
        name="mixer_lat" if cached else "mixer_ctx",
    )(*args)


def _block_schedule(padded, chunks_a):
    NT = padded.shape[0]
    n_chunks = NT * CHUNKS_PER_TILE
    n_blocks = n_chunks // CPB + N_EXPERTS
    c16 = padded // CHUNK
    ends = jnp.cumsum(c16, axis=1)
    before = jnp.cumsum(c16, axis=0) - c16
    per_expert = jnp.sum(c16, axis=0)
    nb = (per_expert + (CPB - 1)) // CPB
    b_end = jnp.cumsum(nb)
    q = jnp.arange(CHUNKS_PER_TILE, dtype=I32)
    e = jnp.arange(N_EXPERTS, dtype=I32)
    key = jnp.sum((ends[:, None, :] <= q[None, :, None]).astype(I32), axis=-1)
    pos_e = ((b_end - nb) * CPB + before)[:, None, :] + (q[None, :, None] - (ends - c16)[:, None, :])
    pos = jnp.sum(jnp.where(key[..., None] == e, pos_e, 0), axis=-1)
    pos = jnp.where(key < N_EXPERTS, pos, -1).reshape(-1)
    match = pos[None, :] == jnp.arange(n_blocks * CPB, dtype=I32)[:, None]
    cid = jnp.sum(jnp.where(match, jnp.arange(n_chunks, dtype=I32)[None, :], 0), axis=-1)
    nv = jnp.sum(match.astype(I32).reshape(n_blocks, -1), axis=-1)
    first = match & (jnp.arange(n_chunks, dtype=I32) < chunks_a)[None, :]
    nv_a = jnp.sum(first.astype(I32).reshape(n_blocks, -1), axis=-1)
    b = jnp.arange(n_blocks, dtype=I32)
    eb = jnp.minimum(jnp.sum((b_end[None, :] <= b[:, None]).astype(I32), axis=-1), N_EXPERTS - 1)
    zero = jnp.zeros((1,), I32)
    counts = jnp.concatenate([nv, zero, nv_a, zero]).astype(I32)
    return eb.astype(I32), counts, jnp.concatenate([cid, jnp.zeros((CPB,), I32)]).astype(I32)


def _moe_kernel(eb_ref, nv_ref, cid_ref, hs_a, hs_b, wg_ref, wu_ref, wd_ref, ys_a, ys_b,
                xbuf, ybuf, wg_b, wu_b, wd_b, in_sem, out_sem, *, n_blocks, chunks_a):
    b = pl.program_id(0)

    def chunk_rows(c):
        return pl.ds(pl.multiple_of(c * CHUNK, CHUNK), CHUNK)

    def gather_copy(which, c, j, slot):
        src = (hs_a, hs_b)[which]
        return pltpu.make_async_copy(src.at[chunk_rows(c)], xbuf.at[slot, chunk_rows(j)], in_sem.at[slot])

    def scatter_copy(which, c, j, slot):
        dst = (ys_a, ys_b)[which]
        return pltpu.make_async_copy(ybuf.at[slot, chunk_rows(j)],
                                     dst.at[chunk_rows(c), pl.ds(0, D_MODEL)], out_sem.at[slot])

    def for_chunks(blk, fn):
        n_a = nv_ref[n_blocks + 1 + blk]

        def body_a(j, carry):
            fn(j, 0, cid_ref[blk * CPB + j])
            return carry

        def body_b(j, carry):
            fn(j, 1, cid_ref[blk * CPB + j] - chunks_a)
            return carry
        lax.fori_loop(0, n_a, body_a, 0)
        lax.fori_loop(n_a, nv_ref[blk], body_b, 0)

    def each_chunk(blk, fn):
        n_a = nv_ref[n_blocks + 1 + blk]
        n_v = nv_ref[blk]
        for j in range(CPB):
            c = cid_ref[blk * CPB + j]

            @pl.when(j < n_a)
            def _():
                fn(j, 0, c)

            @pl.when((j >= n_a) & (j < n_v))
            def _():
                fn(j, 1, c - chunks_a)

    def wait_gathers(blk, slot):
        rows = pl.ds(0, nv_ref[blk] * CHUNK)

        @pl.when(nv_ref[blk] > 0)
        def _():
            pltpu.make_async_copy(hs_a.at[rows], xbuf.at[slot, rows], in_sem.at[slot]).wait()

    def wait_scatters(blk, slot):
        rows = pl.ds(0, nv_ref[blk] * CHUNK)

        @pl.when(nv_ref[blk] > 0)
        def _():
            pltpu.make_async_copy(ybuf.at[slot, rows], ys_a.at[rows, pl.ds(0, D_MODEL)], out_sem.at[slot]).wait()

    slot = lax.rem(b, 2)
    other = 1 - slot

    @pl.when(b == 0)
    def _():
        xbuf[...] = jnp.zeros_like(xbuf)
        for_chunks(0, lambda j, w, c: gather_copy(w, c, j, 0).start())

    @pl.when((b == 0) | (eb_ref[b] != eb_ref[jnp.maximum(b - 1, 0)]))
    def _():
        wg_b[...] = wg_ref[...].astype(BF)
        wu_b[...] = wu_ref[...].astype(BF)
        wd_b[...] = wd_ref[...].astype(BF)

    wait_gathers(b, slot)

    @pl.when(b >= 2)
    def _():
        wait_scatters(b - 2, slot)

    @pl.when(nv_ref[b] > 0)
    def _():
        each_chunk(b + 1, lambda j, w, c: gather_copy(w, c, j, other).start())
        x = xbuf[slot, :, 0:D_MODEL]
        gate = (xbuf[slot, :, D_MODEL:D_MODEL + 1].astype(F32)
                + xbuf[slot, :, D_MODEL + 1:D_MODEL + 2].astype(F32))
        hg = _dot(x, wg_b[...])
        hu = _dot(x, wu_b[...])
        act = _silu(hg) * hu * gate
        ybuf[slot] = _dot(act.astype(BF), wd_b[...]).astype(BF)
        each_chunk(b, lambda j, w, c: scatter_copy(w, c, j, slot).start())

    @pl.when(b == n_blocks - 1)
    def _():
        if n_blocks >= 2:
            wait_scatters(b - 1, other)
        wait_scatters(b, slot)


def _moe(hs_a, hs_b, eb, nv, cid, layer, w_gate, w_up, w_down):
    n_blocks = eb.shape[0]

    def w_spec(a, c):
        return pl.BlockSpec((None, None, a, c), lambda b, eb, nv, cid: (layer, eb[b], 0, 0))

    grid_spec = pltpu.PrefetchScalarGridSpec(
        num_scalar_prefetch=3,
        grid=(n_blocks,),
        in_specs=[_HBM, _HBM, w_spec(D_MODEL, D_EXPERT), w_spec(D_MODEL, D_EXPERT), w_spec(D_EXPERT, D_MODEL)],
        out_specs=[_HBM, _HBM],
        scratch_shapes=[pltpu.VMEM((2, MOE_TMB, HS_W), BF), pltpu.VMEM((2, MOE_TMB, D_MODEL), BF),
                        pltpu.VMEM((D_MODEL, D_EXPERT), BF), pltpu.VMEM((D_MODEL, D_EXPERT), BF),
                        pltpu.VMEM((D_EXPERT, D_MODEL), BF),
                        pltpu.SemaphoreType.DMA((2,)), pltpu.SemaphoreType.DMA((2,))],
    )
    return pl.pallas_call(
        functools.partial(_moe_kernel, n_blocks=n_blocks, chunks_a=hs_a.shape[0] // CHUNK),
        grid_spec=grid_spec,
        out_shape=[jax.ShapeDtypeStruct(hs_a.shape, BF), jax.ShapeDtypeStruct(hs_b.shape, BF)],
        input_output_aliases={3: 0, 4: 1},
        compiler_params=_params(("arbitrary",)),
        name="moe_sorted",
    )(eb, nv, cid, hs_a, hs_b, w_gate, w_up, w_down)


def _combine_kernel(ys_ref, slots_ref, x1_ref, mod_ref, lng_ref, lnb_ref, o_ref):
    g2 = mod_ref[:, 5 * D_MODEL:6 * D_MODEL]
    s_lane = _iota((TILE, CAP), 1)
    for u in range(COMBINE_TILES_PER_STEP):
        sl = slots_ref[u]
        pick = _ones_where((s_lane == sl[:, 0:1]) | (s_lane == sl[:, 1:2]))
        ffn = _dot(pick, ys_ref[u * CAP:(u + 1) * CAP, 0:D_MODEL])
        o_ref[u] = _layer_norm(ALPHA * x1_ref[u] + g2 * ffn, lng_ref[...], lnb_ref[...])


def _combine(ys, slots, x1, mod, layer, row_fn, lng, lnb):
    S, T, _ = x1.shape
    n = T // TILE
    U = COMBINE_TILES_PER_STEP
    assert (S * n) % U == 0 and (n % U == 0 or n == 1)

    def tok(w):
        return pl.BlockSpec((U, TILE, w), lambda t: (t, 0, 0))

    out = pl.pallas_call(
        _combine_kernel,
        grid=(S * n // U,),
        in_specs=[
            pl.BlockSpec((U * CAP, D_MODEL), lambda t: (t, 0)),
            tok(LANES), tok(D_MODEL),
            pl.BlockSpec((None, None, 1, 6 * D_MODEL), lambda t: (layer, row_fn(t * U // n), 0, 0)),
            pl.BlockSpec((1, D_MODEL), lambda t: (0, 0)),
            pl.BlockSpec((1, D_MODEL), lambda t: (0, 0)),
        ],
        out_specs=tok(D_MODEL),
        out_shape=jax.ShapeDtypeStruct((S * n, TILE, D_MODEL), F32),
        compiler_params=_params(("parallel",)),
        name="moe_combine",
    )(ys, slots.reshape(S * n, TILE, LANES), x1.reshape(S * n, TILE, D_MODEL), mod, lng, lnb)
    return out.reshape(S, T, D_MODEL)


def _rope_tables(n_lat):
    half = DIFF_QK // 2
    pairs = half // 2
    inv = (1.0 / (ROPE_BASE ** (np.arange(pairs, dtype=np.float32) * 2.0 / half))).astype(np.float32)
    t = np.arange(n_lat)
    ang_r = ((t // GRID_W).astype(np.float32)[:, None] * inv[None, :]).astype(np.float64)
    ang_c = ((t % GRID_W).astype(np.float32)[:, None] * inv[None, :]).astype(np.float64)
    cos = np.concatenate([np.cos(ang_r)] * 2 + [np.cos(ang_c)] * 2, axis=1)
    sin = np.concatenate([-np.sin(ang_r), np.sin(ang_r), -np.sin(ang_c), np.sin(ang_c)], axis=1)
    reps = DIFF_W // DIFF_QK
    return (jnp.asarray(np.tile(cos, (1, reps)), F32), jnp.asarray(np.tile(sin, (1, reps)), F32))


def _block_diag(s):
    S = s.shape[0]
    eye = jnp.eye(H_RET, dtype=s.dtype)
    return jnp.einsum('shdv,hg->shdgv', s, eye).reshape(S, RET_W, RET_W)


def _router_rows(w_group, b_group, w_expert, b_expert):
    pad = EXPERT_ROW0 - N_GROUPS
    tail = ROUTER_ROWS - EXPERT_ROW0 - N_EXPERTS
    w = jnp.concatenate([w_group.T, jnp.zeros((pad, D_MODEL), F32),
                         w_expert.reshape(D_MODEL, N_EXPERTS).T, jnp.zeros((tail, D_MODEL), F32)], axis=0)
    bias = jnp.concatenate([b_group, jnp.zeros((pad,), F32), b_expert.reshape(N_EXPERTS), jnp.zeros((tail,), F32)])
    hi = w.astype(BF)
    return hi, (w - hi.astype(F32)).astype(BF), bias.reshape(ROUTER_ROWS, 1)


def kernel(x_prompt, x_sample, cache_diff_k, cache_diff_v, state_ret_fwd, state_ret_bwd, c, c_ctx, w_mod, b_mod, w_in, ret_decay_logit, diff_lambda, diff_subln_g, conv_w, conv_b, w_out, ln_g, ln_b, w_router_group, b_router_group, w_router_expert, b_router_expert, w_gate, w_up, w_down):
    B, T_ctx, _ = x_prompt.shape
    Bd, T_lat, _ = x_sample.shape
    assert T_ctx == TILE and T_lat % TILE == 0 and T_lat % GRID_W == 0
    assert 1 + Bd <= MOD_ROWS

    cond = jnp.concatenate([c_ctx[None, :], c, jnp.zeros((MOD_ROWS - 1 - Bd, D_MODEL), F32)], axis=0)
    mod = _modulation(cond, w_mod, b_mod).reshape(DEPTH, MOD_ROWS, 1, 6 * D_MODEL)
    rope_tabs = _rope_tables(T_lat)
    w_in_b = w_in.astype(BF)
    w_out_b = w_out.astype(BF)

    ctx_row = lambda s: 0
    lat_row = lambda s: s + 1
    ctx_tiles = B * (T_ctx // TILE)
    lat_tiles = Bd * (T_lat // TILE)

    yp, ys = x_prompt, x_sample
    stacked = None
    for l in range(DEPTH):
        lam_init = 0.8 - 0.6 * math.exp(-0.3 * l)
        lg_lanes = jnp.repeat(ret_decay_logit[l], RET_DK, axis=1)
        wrh, wrl, br = _router_rows(w_router_group[l], b_router_group[l], w_router_expert[l], b_router_expert[l])
        shared = (w_out_b, lg_lanes, diff_lambda[l], diff_subln_g[l].reshape(1, DIFF_V), conv_w[l],
                  conv_b[l].reshape(1, CONV_CH), ln_g[l, 0].reshape(1, D_MODEL), ln_b[l, 0].reshape(1, D_MODEL),
                  wrh, wrl, br, lam_init)
        ln2 = (ln_g[l, 1].reshape(1, D_MODEL), ln_b[l, 1].reshape(1, D_MODEL))

        pr = _proj(yp, mod, l, ctx_row, w_in_b, lg_lanes, stacked=stacked)
        rq, rk, rv, rg, dq, dk, dv, cb, cch = pr[:9]
        stacked = pr[9:]
        x1_c, hs_c, slots_c, cnt_c = _mixer(yp, mod, l, ctx_row, (rq, rk, rv, rg, dq, cb, cch), None, None,
                                            (dk, dv), *shared)

        pr = _proj(ys, mod, l, lat_row, w_in_b, lg_lanes, rope_tabs=rope_tabs)
        rq, rk, rv, rg, dq, dk, dv, cb, cch, kvf, kvb = pr
        states = _scan(kvf, kvb, _block_diag(state_ret_fwd[:, l]), _block_diag(state_ret_bwd[:, l]), lg_lanes)
        cache_v = cache_diff_v[:, l].astype(BF)
        cache_kv = (cache_diff_k[:, l].reshape(Bd, -1, DIFF_W).astype(BF),
                    jnp.concatenate([cache_v, jnp.ones_like(cache_v)], axis=-1).reshape(Bd, -1, V_AUG_W))
        x1_l, hs_l, slots_l, cnt_l = _mixer(ys, mod, l, lat_row, (rq, rk, rv, rg, dq, cb, cch), states, cache_kv,
                                            (dk, dv), *shared)

        padded = jnp.concatenate([cnt_c.reshape(ctx_tiles, N_EXPERTS, LANES)[:, :, 0],
                                  cnt_l.reshape(lat_tiles, N_EXPERTS, LANES)[:, :, 0]], axis=0)
        eb, nv, cid = _block_schedule(padded, ctx_tiles * CHUNKS_PER_TILE)
        out_c, out_l = _moe(hs_c, hs_l, eb, nv, cid, l, w_gate, w_up, w_down)
        yp = _combine(out_c, slots_c, x1_c, mod, l, ctx_row, *ln2)
        ys = _combine(out_l, slots_l, x1_l, mod, l, lat_row, *ln2)

    new_k, new_v, st_f, st_b = stacked
    return (yp, ys, new_k.reshape(B, DEPTH, T_ctx, H_DIFF, 2, DIFF_QK),
            new_v.reshape(B, DEPTH, T_ctx, H_DIFF, DIFF_V), st_f, st_b)
```

```python
import functools
import math

import numpy as np
import jax
import jax.numpy as jnp
from jax import lax
from jax.experimental import pallas as pl
from jax.experimental.pallas import tpu as pltpu

D_MODEL = 1024
DEPTH = 2
GRID_W = 64
H_RET = 4
RET_DK = 64
RET_W = H_RET * RET_DK
H_DIFF = 4
DIFF_QK = 64
DIFF_V = 2 * DIFF_QK
DIFF_W = H_DIFF * DIFF_V
CONV_CH = 256
ROPE_BASE = 10000.0
N_GROUPS = 4
EXPERTS_PER_GROUP = 4
N_EXPERTS = N_GROUPS * EXPERTS_PER_GROUP
D_EXPERT = 512
ALPHA = (2 * DEPTH) ** 0.25
EPS = 1e-5
MOD_ROWS = 8
LANES = 128
ROUTER_ROWS = 32
EXPERT_ROW0 = 8

TILE = 256
CHUNK = 16
CAP = 2 * TILE + N_EXPERTS * CHUNK
CHUNKS_PER_TILE = CAP // CHUNK
COMBINE_TILES_PER_STEP = 4
PROJ_TILES = 4
MOE_TMB = 1024
CPB = MOE_TMB // CHUNK
HS_W = D_MODEL + LANES
V_AUG_W = 2 * DIFF_W
QK_SCALE_LOG2 = (DIFF_QK ** -0.5) * math.log2(math.e)
SLOT_RADIX = 32

_O_RQ, _O_RK, _O_RV, _O_RG = 0, 256, 512, 768
_O_DQ, _O_DK, _O_DV = 1024, 1536, 2048
_O_CB, _O_CC, _O_CH = 2560, 2816, 3072
IN_WIDTH = 3328

BF = jnp.bfloat16
F32 = jnp.float32
I32 = jnp.int32

_VMEM_LIMIT = 60 * 1024 * 1024


def _dot(a, b):
    return jnp.dot(a, b, preferred_element_type=F32)


def _dot_nt(a, b):
    return lax.dot_general(a, b, (((1,), (1,)), ((), ())), preferred_element_type=F32)


def _dot_tn(a, b):
    return lax.dot_general(a, b, (((0,), (0,)), ((), ())), preferred_element_type=F32)


def _split(x):
    hi = x.astype(BF)
    lo = (x - hi.astype(F32)).astype(BF)
    return hi, lo


def _dot_hl(x, w_bf16):
    hi, lo = _split(x)
    return _dot(hi, w_bf16) + _dot(lo, w_bf16)


def _dot3(x, w_hi, w_lo):
    hi, lo = _split(x)
    return _dot(hi, w_hi) + (_dot(lo, w_hi) + _dot(hi, w_lo))


def _iota(shape, dim):
    return lax.broadcasted_iota(I32, shape, dim)


def _ones_where(cond):
    return jnp.where(cond, 1.0, 0.0).astype(BF)


def _log_sigmoid(x):
    return jnp.minimum(x, 0.0) - jnp.log1p(jnp.exp(-jnp.abs(x)))


def _silu(x):
    return x * jax.nn.sigmoid(x)


def _layer_norm(y, g, b):
    mu = jnp.mean(y, axis=-1, keepdims=True)
    yc = y - mu
    var = jnp.mean(yc * yc, axis=-1, keepdims=True)
    return yc * lax.rsqrt(var + EPS) * g + b


def _params(sem):
    return pltpu.CompilerParams(dimension_semantics=sem, vmem_limit_bytes=_VMEM_LIMIT)


_HBM = pl.BlockSpec(memory_space=pltpu.HBM)


def _mod_kernel(c_ref, w_ref, b_ref, o_ref):
    a = _silu(c_ref[...])
    w_hi, w_lo = _split(w_ref[...])
    o_ref[...] = _dot3(a, w_hi, w_lo) + b_ref[...]


def _modulation(cond, w_mod, b_mod):
    tn = 1536
    n6 = 6 * D_MODEL
    return pl.pallas_call(
        _mod_kernel,
        grid=(DEPTH, n6 // tn),
        in_specs=[
            pl.BlockSpec((MOD_ROWS, D_MODEL), lambda l, j: (0, 0)),
            pl.BlockSpec((None, D_MODEL, tn), lambda l, j: (l, 0, j)),
            pl.BlockSpec((None, 1, tn), lambda l, j: (l, 0, j)),
        ],
        out_specs=pl.BlockSpec((None, MOD_ROWS, tn), lambda l, j: (l, 0, j)),
        out_shape=jax.ShapeDtypeStruct((DEPTH, MOD_ROWS, n6), F32),
        compiler_params=_params(("parallel", "parallel")),
        name="modulation",
    )(cond, w_mod, b_mod.reshape(DEPTH, 1, n6))


def _mod_spec(layer, row_fn):
    return pl.BlockSpec((None, None, 1, 6 * D_MODEL), lambda s, i: (layer, row_fn(s), 0, 0))


def _augment_v(v):
    ones = jnp.ones((v.shape[0], DIFF_V), v.dtype)
    parts = []
    for hd in range(H_DIFF):
        parts += [v[:, hd * DIFF_V:(hd + 1) * DIFF_V], ones]
    return jnp.concatenate(parts, axis=1)


def _swap16(x):
    n = x.shape[-1]
    lane = _iota(x.shape, 1)
    return jnp.where((lane & 16) == 0, pltpu.roll(x, n - 16, axis=1), pltpu.roll(x, 16, axis=1))


def _proj_kernel(*refs, latent, n_aliased, layer):
    x_ref, mod_ref, w_ref, lg_ref = refs[:4]
    pos = 4
    if latent:
        cos_ref, sin_ref = refs[pos:pos + 2]
        pos += 2
    pos += n_aliased
    rq_ref, rk_ref, rv_ref, rg_ref, dq_ref, dk_ref, dv_ref, cb_ref, cch_ref = refs[pos:pos + 9]
    pos += 9
    if latent:
        kvf_ref, kvb_ref = refs[pos:pos + 2]
    else:
        dk32_ref, dv32_ref, stf_ref, stb_ref = refs[pos:pos + 4]
    rows = PROJ_TILES * TILE

    def put(ref, u, idx, val):
        if n_aliased:
            ref[(u,) + idx] = val
        else:
            for other in range(DEPTH):
                ref[(u, other) + idx] = val if other == layer else jnp.zeros_like(val)

    def store(ref, val):
        ref[...] = val.reshape(PROJ_TILES, TILE, val.shape[-1])

    m = mod_ref[...]
    sh1 = m[:, 0:D_MODEL]
    sc1 = m[:, D_MODEL:2 * D_MODEL]
    h = (x_ref[...].reshape(rows, D_MODEL) * (1.0 + sc1) + sh1).astype(BF)

    def col(off, width):
        return _dot(h, w_ref[:, off:off + width])

    store(rq_ref, col(_O_RQ, RET_W).astype(BF))
    rk = col(_O_RK, RET_W) * (RET_DK ** -0.5)
    store(rk_ref, rk.astype(BF))
    rv = col(_O_RV, RET_W).astype(BF)
    store(rv_ref, rv)
    store(rg_ref, col(_O_RG, RET_W))

    lg = _log_sigmoid(lg_ref[...])
    p = (_iota((rows, 1), 0) & (TILE - 1)).astype(F32)
    kf = (rk * jnp.exp((TILE - 1.0 - p) * lg[0:1])).astype(BF)
    kb = (rk * jnp.exp(p * lg[1:2])).astype(BF)
    for u in range(PROJ_TILES):
        tile = slice(u * TILE, (u + 1) * TILE)
        kvf = _dot_tn(kf[tile], rv[tile])
        kvb = _dot_tn(kb[tile], rv[tile])
        if latent:
            kvf_ref[u] = kvf
            kvb_ref[u] = kvb
        else:
            for hd in range(H_RET):
                lo = hd * RET_DK
                put(stf_ref, u, (hd,), kvf[lo:lo + RET_DK, lo:lo + RET_DK])
                put(stb_ref, u, (hd,), kvb[lo:lo + RET_DK, lo:lo + RET_DK])

    dq = col(_O_DQ, DIFF_W)
    dk = col(_O_DK, DIFF_W)
    dv = col(_O_DV, DIFF_W)
    if latent:
        cos = cos_ref[...].reshape(rows, DIFF_W)
        sin = sin_ref[...].reshape(rows, DIFF_W)
        dq = dq * cos + _swap16(dq) * sin
        dk = dk * cos + _swap16(dk) * sin
    else:
        for u in range(PROJ_TILES):
            put(dk32_ref, u, (), dk[u * TILE:(u + 1) * TILE])
            put(dv32_ref, u, (), dv[u * TILE:(u + 1) * TILE])
    store(dq_ref, (dq * QK_SCALE_LOG2).astype(BF))
    store(dk_ref, dk.astype(BF))
    store(dv_ref, _augment_v(dv.astype(BF)))

    store(cb_ref, col(_O_CB, CONV_CH))
    store(cch_ref, col(_O_CC, CONV_CH) * col(_O_CH, CONV_CH))


def _proj(x, mod, layer, row_fn, w_in_b, lg_lanes, rope_tabs=None, stacked=None):
    S, T, _ = x.shape
    n = T // TILE
    G = PROJ_TILES
    tiles = S * n
    latent = rope_tabs is not None
    assert tiles % G == 0 and (n % G == 0 or n == 1)

    def tok(w):
        return pl.BlockSpec((G, TILE, w), lambda t: (t, 0, 0))

    def tok_shape(w, dt):
        return jax.ShapeDtypeStruct((tiles, TILE, w), dt)

    in_specs = [
        tok(D_MODEL),
        pl.BlockSpec((None, None, 1, 6 * D_MODEL), lambda t: (layer, row_fn(t * G // n), 0, 0)),
        pl.BlockSpec((None, D_MODEL, IN_WIDTH), lambda t: (layer, 0, 0)),
        pl.BlockSpec((2, RET_W), lambda t: (0, 0)),
    ]
    args = [x.reshape(tiles, TILE, D_MODEL), mod, w_in_b, lg_lanes]
    out_specs = [tok(RET_W)] * 4 + [tok(DIFF_W)] * 2 + [tok(V_AUG_W)] + [tok(CONV_CH)] * 2
    out_shape = ([tok_shape(RET_W, BF)] * 3 + [tok_shape(RET_W, F32)] + [tok_shape(DIFF_W, BF)] * 2
                 + [tok_shape(V_AUG_W, BF)]
                 + [tok_shape(CONV_CH, F32)] * 2)
    aliases = {}
    if latent:
        in_specs += [pl.BlockSpec((G, TILE, DIFF_W), lambda t: (t % (n // G), 0, 0))] * 2
        args += [tab.reshape(n, TILE, DIFF_W) for tab in rope_tabs]
        out_specs += [pl.BlockSpec((G, RET_W, RET_W), lambda t: (t, 0, 0))] * 2
        out_shape += [jax.ShapeDtypeStruct((tiles, RET_W, RET_W), F32)] * 2
    else:
        assert n == 1
        kv_shape = jax.ShapeDtypeStruct((S, DEPTH, T, DIFF_W), F32)
        st_shape = jax.ShapeDtypeStruct((S, DEPTH, H_RET, RET_DK, RET_DK), F32)
        if stacked is not None:
            in_specs += [_HBM] * 4
            args += list(stacked)
            aliases = {len(args) - 4 + j: len(out_shape) + j for j in range(4)}
            out_specs += [pl.BlockSpec((G, None, TILE, DIFF_W), lambda t: (t, layer, 0, 0))] * 2
            out_specs += [pl.BlockSpec((G, None, H_RET, RET_DK, RET_DK), lambda t: (t, layer, 0, 0, 0))] * 2
        else:
            out_specs += [pl.BlockSpec((G, DEPTH, TILE, DIFF_W), lambda t: (t, 0, 0, 0))] * 2
            out_specs += [pl.BlockSpec((G, DEPTH, H_RET, RET_DK, RET_DK), lambda t: (t, 0, 0, 0, 0))] * 2
        out_shape += [kv_shape, kv_shape, st_shape, st_shape]
    outs = pl.pallas_call(
        functools.partial(_proj_kernel, latent=latent, n_aliased=len(aliases), layer=layer),
        grid=(tiles // G,),
        in_specs=in_specs,
        out_specs=out_specs,
        out_shape=out_shape,
        input_output_aliases=aliases,
        compiler_params=_params(("parallel",)),
        name="proj_lat" if latent else "proj_ctx",
    )(*args)
    token_outs = [o.reshape(S, T, o.shape[-1]) for o in outs[:9]]
    if latent:
        return token_outs + [o.reshape(S, n, RET_W, RET_W) for o in outs[9:]]
    return token_outs + list(outs[9:])


def _scan_kernel(kvf_ref, kvb_ref, s0f_ref, s0b_ref, lg_ref, sf_ref, sb_ref, *, n):
    lg = _log_sigmoid(lg_ref[...])
    dec = jnp.exp(float(TILE) * lg)
    same_head = (_iota((RET_W, RET_W), 0) >> 6) == (_iota((RET_W, RET_W), 1) >> 6)

    s = jnp.where(same_head, s0f_ref[...], 0.0)
    for c in range(n):
        sf_ref[c] = s
        s = s * dec[0:1] + jnp.where(same_head, kvf_ref[c], 0.0)
    s = jnp.where(same_head, s0b_ref[...], 0.0)
    for c in reversed(range(n)):
        sb_ref[c] = s
        s = s * dec[1:2] + jnp.where(same_head, kvb_ref[c], 0.0)


def _scan(kvf, kvb, s0f, s0b, lg_lanes):
    S, n = kvf.shape[:2]
    chunks = pl.BlockSpec((None, n, RET_W, RET_W), lambda s: (s, 0, 0, 0))
    one = pl.BlockSpec((None, RET_W, RET_W), lambda s: (s, 0, 0))
    return pl.pallas_call(
        functools.partial(_scan_kernel, n=n),
        grid=(S,),
        in_specs=[chunks, chunks, one, one, pl.BlockSpec((2, RET_W), lambda s: (0, 0))],
        out_specs=[chunks, chunks],
        out_shape=[jax.ShapeDtypeStruct((S, n, RET_W, RET_W), F32)] * 2,
        compiler_params=_params(("parallel",)),
        name="ret_scan",
    )(kvf, kvb, s0f, s0b, lg_lanes)


def _diff_scores(dq_ref, kv_refs, heads):
    lane = _iota((1, DIFF_V), 1)
    scores = []
    for hd in heads:
        lo = hd * DIFF_V
        q_h = dq_ref[:, lo:lo + DIFF_V]
        qs = jnp.concatenate([jnp.where(lane < DIFF_QK, q_h, 0), jnp.where(lane >= DIFF_QK, q_h, 0)], axis=0)
        scores.append(jnp.concatenate([_dot_nt(qs, k[:, lo:lo + DIFF_V]) for k, _ in kv_refs], axis=1))
    return jnp.concatenate(scores, axis=0)


def _diff_outputs(s, kv_refs, heads, lam):
    p = jnp.exp2(s - jnp.max(s, axis=-1, keepdims=True))

    def times_v(w, hd, width):
        acc = None
        start = 0
        for k, v in kv_refs:
            part = _dot(w[:, start:start + k.shape[0]], v[:, hd * 2 * DIFF_V:hd * 2 * DIFF_V + width])
            acc = part if acc is None else acc + part
            start += k.shape[0]
        return acc

    outs = []
    p = p.astype(BF)
    for n, hd in enumerate(heads):
        acc = times_v(p[n * 2 * TILE:(n + 1) * 2 * TILE], hd, 2 * DIFF_V)
        o0 = acc[0:TILE, 0:DIFF_V] * (1.0 / acc[0:TILE, DIFF_V:DIFF_V + 1])
        o1 = acc[TILE:, 0:DIFF_V] * (lam / acc[TILE:, DIFF_V:DIFF_V + 1])
        outs.append(o0 - o1)
    return outs


def _mixer_kernel(*refs, n_tiles, n_total, lam_init, cached):
    n_a = _n_token_mix_refs(cached)
    mix_even, mix_odd, decay_ref = refs[-3:]
    t = pl.program_id(0)
    tile_in_seq = lax.rem(jnp.minimum(t, n_total - 1), n_tiles)

    @pl.when(t == 0)
    def _():
        mix_odd[...] = jnp.zeros_like(mix_odd)
        lg = _log_sigmoid(refs[n_a - 5][...])
        dist = (_iota((TILE, TILE), 0) - _iota((TILE, TILE), 1)).astype(F32)
        diag2 = jnp.where(dist == 0.0, 2.0, 1.0)
        for hd in range(H_RET):
            lgf = lg[0:1, hd * RET_DK:hd * RET_DK + 1]
            lgb = lg[1:2, hd * RET_DK:hd * RET_DK + 1]
            decay_ref[hd] = jnp.exp(jnp.abs(dist) * jnp.where(dist > 0.0, lgf, lgb)) * diag2

    def step(mix_write, mix_read):
        mix_write[...] = _token_mix(*refs[:n_a], decay_ref, tile_in_seq=tile_in_seq, n_tiles=n_tiles,
                                    lam_init=lam_init, cached=cached)
        _project_route(mix_read[...], *refs[n_a:-3])

    @pl.when(lax.rem(t, 2) == 0)
    def _():
        step(mix_even, mix_odd)

    @pl.when(lax.rem(t, 2) == 1)
    def _():
        step(mix_odd, mix_even)


def _n_token_mix_refs(cached):
    return 18 + (2 if cached else 0)


def _token_mix(*refs, tile_in_seq, n_tiles, lam_init, cached):
    rq_ref, rk_ref, rv_ref, rg_ref, sf_ref, sb_ref, dq_ref = refs[:7]
    pos = 7
    kv_refs = []
    if cached:
        kv_refs.append((refs[pos], refs[pos + 1]))
        pos += 2
    kv_refs.append((refs[pos], refs[pos + 1]))
    pos += 2
    (cb_ref, cch_ref, cprev_ref, cnext_ref, lg_ref, lamp_ref, subg_ref, cw_ref, cbias_ref,
     decay_ref) = refs[pos:pos + 10]
    i = tile_in_seq

    lg = _log_sigmoid(lg_ref[...])
    head_of_lane = _iota((1, RET_W), 1) >> 6
    q = rq_ref[...]
    k = rk_ref[...]
    v = rv_ref[...]
    ret_o = jnp.zeros((TILE, RET_W), F32)
    for hd in range(H_RET):
        in_head = head_of_lane == hd
        sc = _dot_nt(jnp.where(in_head, q, 0), k)
        ret_o = ret_o + jnp.where(in_head, _dot((sc * decay_ref[hd]).astype(BF), v), 0.0)
    p = _iota((TILE, 1), 0).astype(F32)
    ret_o = ret_o + _dot(q, sf_ref[...].astype(BF)) * jnp.exp((p + 1.0) * lg[0:1])
    ret_o = ret_o + _dot(q, sb_ref[...].astype(BF)) * jnp.exp((float(TILE) - p) * lg[1:2])
    avg = jnp.where((_iota((RET_W, RET_W), 0) >> 6) == (_iota((RET_W, RET_W), 1) >> 6),
                    1.0 / RET_DK, 0.0).astype(BF)
    rc = ret_o - _dot_hl(ret_o, avg)
    ret = rc * lax.rsqrt(_dot_hl(rc * rc, avg) + EPS) * _silu(rg_ref[...])

    lp = lamp_ref[...]
    lam = (jnp.exp(jnp.sum(lp[0:1] * lp[1:2], axis=-1, keepdims=True))
           - jnp.exp(jnp.sum(lp[2:3] * lp[3:4], axis=-1, keepdims=True)) + lam_init)
    subg = subg_ref[...] * (1.0 - lam_init)
    head_groups = [[hd] for hd in range(H_DIFF)] if cached else [list(range(H_DIFF))]
    heads = []
    for group in head_groups:
        scores = _diff_scores(dq_ref, kv_refs, group)
        for o in _diff_outputs(scores, kv_refs, group, lam):
            o = o * lax.rsqrt(jnp.mean(o * o, axis=-1, keepdims=True) + EPS) * subg
            heads.append(o.astype(BF))
    diff = jnp.concatenate(heads, axis=1)

    cch = cch_ref[...]
    prev = jnp.where(i > 0, cprev_ref[7:8, :], 0.0)
    nxt = jnp.where(i < n_tiles - 1, cnext_ref[0:1, :], 0.0)
    r = _iota((TILE, 1), 0)
    up = jnp.where(r == 0, prev, pltpu.roll(cch, 1, axis=0))
    dn = jnp.where(r == TILE - 1, nxt, pltpu.roll(cch, TILE - 1, axis=0))
    cw = cw_ref[...]
    conv = cb_ref[...] * (up * cw[0:1] + cch * cw[1:2] + dn * cw[2:3] + cbias_ref[...])
    return jnp.concatenate([ret.astype(BF), diff, conv.astype(BF)], axis=1)


def _project_route(mixed, x_ref, mod_ref, wout_ref, lng_ref, lnb_ref, wrh_ref, wrl_ref, br_ref,
                   x1_ref, hs_ref, slots_ref, cnt_ref):
    m = mod_ref[...]
    g1 = m[:, 2 * D_MODEL:3 * D_MODEL]
    sh2 = m[:, 3 * D_MODEL:4 * D_MODEL]
    sc2 = m[:, 4 * D_MODEL:5 * D_MODEL]
    x1 = _layer_norm(ALPHA * x_ref[...] + g1 * _dot(mixed, wout_ref[...]), lng_ref[...], lnb_ref[...])
    x1_ref[...] = x1

    h2 = x1 * (1.0 + sc2) + sh2
    h2h, h2l = _split(h2)
    wrh = wrh_ref[...]
    logits = (_dot_nt(wrh, h2h) + (_dot_nt(wrh, h2l) + _dot_nt(wrl_ref[...], h2h))) + br_ref[...]
    neg = -jnp.inf
    g_id = _iota((EXPERT_ROW0, TILE), 0)
    gl = jnp.where(g_id < N_GROUPS, logits[0:EXPERT_ROW0], neg)
    gmax = jnp.max(gl, axis=0, keepdims=True)
    g_idx = jnp.min(jnp.where(gl == gmax, g_id, N_GROUPS), axis=0, keepdims=True)
    g_w = 1.0 / jnp.sum(jnp.exp(gl - gmax), axis=0, keepdims=True)
    e_id = _iota((N_EXPERTS, TILE), 0)
    el = jnp.where((e_id >> 2) == g_idx, logits[EXPERT_ROW0:EXPERT_ROW0 + N_EXPERTS], neg)
    v1 = jnp.max(el, axis=0, keepdims=True)
    i1 = jnp.min(jnp.where(el == v1, e_id, N_EXPERTS), axis=0, keepdims=True)
    el2 = jnp.where(e_id == i1, neg, el)
    v2 = jnp.max(el2, axis=0, keepdims=True)
    i2 = jnp.min(jnp.where(el2 == v2, e_id, N_EXPERTS), axis=0, keepdims=True)
    t = jnp.exp(v2 - v1)
    w1 = g_w / (1.0 + t)
    w2 = g_w * t / (1.0 + t)

    sel1 = e_id == i1
    sel2 = e_id == i2
    sel = jnp.where(sel1 | sel2, 1.0, 0.0)
    earlier = _ones_where(_iota((TILE, TILE), 0) < _iota((TILE, TILE), 1))
    rank = _dot(sel.astype(BF), earlier)
    cnt = jnp.sum(sel, axis=1, keepdims=True).astype(I32)
    padded = ((cnt + (CHUNK - 1)) >> 4) << 4
    incl = jnp.broadcast_to(padded, (N_EXPERTS, LANES))
    e_row = _iota((N_EXPERTS, LANES), 0)
    for step in (1, 2, 4, 8):
        incl = incl + jnp.where(e_row >= step, pltpu.roll(incl, step, axis=0), 0)
    seg_off = (incl[:, 0:1] - padded).astype(F32)
    spos = seg_off + rank
    slot1 = jnp.sum(jnp.where(sel1, spos, 0.0), axis=0, keepdims=True).astype(I32)
    slot2 = jnp.sum(jnp.where(sel2, spos, 0.0), axis=0, keepdims=True).astype(I32)
    s_id = _iota((CAP, TILE), 0)
    at1 = s_id == slot1
    at2 = s_id == slot2
    hs_ref[:, 0:D_MODEL] = _dot(_ones_where(at1 | at2), h2h).astype(BF)
    w1h = w1.astype(BF).astype(F32)
    w2h = w2.astype(BF).astype(F32)
    ones = jnp.ones((TILE, LANES), BF)
    g_hi = _dot((jnp.where(at1, w1h, 0.0) + jnp.where(at2, w2h, 0.0)).astype(BF), ones)
    g_lo = _dot((jnp.where(at1, w1 - w1h, 0.0) + jnp.where(at2, w2 - w2h, 0.0)).astype(BF), ones)
    lane = _iota((CAP, LANES), 1)
    hs_ref[:, D_MODEL:] = jnp.where(lane == 0, g_hi, jnp.where(lane == 1, g_lo, 0.0)).astype(BF)
    cnt_ref[...] = jnp.broadcast_to(padded, (N_EXPERTS, LANES))

    dr = _iota((LANES, TILE), 0)
    digits = jnp.where(dr == 0, slot1 & (SLOT_RADIX - 1), jnp.where(dr == 1, slot1 >> 5,
             jnp.where(dr == 2, slot2 & (SLOT_RADIX - 1), jnp.where(dr == 3, slot2 >> 5, 0))))
    eye = _ones_where(_iota((TILE, TILE), 0) == _iota((TILE, TILE), 1))
    cols = _dot_nt(eye, digits.astype(F32).astype(BF))
    s1c = (cols[:, 0:1] + SLOT_RADIX * cols[:, 1:2]).astype(I32)
    s2c = (cols[:, 2:3] + SLOT_RADIX * cols[:, 3:4]).astype(I32)
    tl = _iota((TILE, LANES), 1)
    slots_ref[...] = jnp.where(tl == 0, s1c, jnp.where(tl == 1, s2c, 0))


def _mixer(x, mod, layer, row_fn, pr, states, cache_kv, new_kv, w_out_b, lg_lanes, lamp, subg, cw, cbias,
           lng, lnb, wrh, wrl, br, lam_init):
    S, T, _ = x.shape
    n = T // TILE
    rq, rk, rv, rg, dq, cb, cch = pr
    t8 = TILE // 8
    cached = cache_kv is not None
    last = S * n - 1

    def at_mix(t):
        return jnp.minimum(t, last) // n, jnp.minimum(t, last) % n

    def at_out(t):
        return jnp.maximum(t - 1, 0) // n, jnp.maximum(t - 1, 0) % n

    def tok(w, at):
        return pl.BlockSpec((None, TILE, w), lambda t: (*at(t), 0))

    def full(shape):
        return pl.BlockSpec(shape, lambda t: (0,) * len(shape))

    def seq(a):
        return pl.BlockSpec((None,) + a.shape[1:], lambda t: (at_mix(t)[0], 0, 0))

    def halo(offset):
        def index(t):
            s, i = at_mix(t)
            return (s, jnp.clip(i * t8 + offset, 0, T // 8 - 1), 0)
        return pl.BlockSpec((None, 8, CONV_CH), index)

    if states is None:
        zero_state = jnp.zeros((1, 1, RET_W, RET_W), F32)
        states = (zero_state, zero_state)
        state = pl.BlockSpec((None, None, RET_W, RET_W), lambda t: (0, 0, 0, 0))
    else:
        state = pl.BlockSpec((None, None, RET_W, RET_W), lambda t: (*at_mix(t), 0, 0))

    in_specs = [tok(RET_W, at_mix)] * 4 + [state, state, tok(DIFF_W, at_mix)]
    args = [rq, rk, rv, rg, states[0], states[1], dq]
    for kv in ([cache_kv] if cached else []) + [new_kv]:
        in_specs += [seq(kv[0]), seq(kv[1])]
        args += list(kv)
    in_specs += [tok(CONV_CH, at_mix), tok(CONV_CH, at_mix), halo(-1), halo(t8),
                 full((2, RET_W)), full((4, DIFF_QK)), full((1, DIFF_V)), full((3, CONV_CH)), full((1, CONV_CH))]
    args += [cb, cch, cch, cch, lg_lanes, lamp, subg, cw, cbias]
    assert len(args) == _n_token_mix_refs(cached)
    in_specs += [tok(D_MODEL, at_out),
                 pl.BlockSpec((None, None, 1, 6 * D_MODEL), lambda t: (layer, row_fn(at_out(t)[0]), 0, 0)),
                 pl.BlockSpec((None, D_MODEL, D_MODEL), lambda t: (layer, 0, 0)),
                 full((1, D_MODEL)), full((1, D_MODEL)),
                 full((ROUTER_ROWS, D_MODEL)), full((ROUTER_ROWS, D_MODEL)), full((ROUTER_ROWS, 1))]
    args += [x, mod, w_out_b, lng, lnb, wrh, wrl, br]

    def out_tile(t):
        s, i = at_out(t)
        return s * n + i

    return pl.pallas_call(
        functools.partial(_mixer_kernel, n_tiles=n, n_total=S * n, lam_init=lam_init, cached=cached),
        grid=(S * n + 1,),
        in_specs=in_specs,
        out_specs=[tok(D_MODEL, at_out),
                   pl.BlockSpec((CAP, HS_W), lambda t: (out_tile(t), 0)),
                   tok(LANES, at_out),
                   pl.BlockSpec((None, None, N_EXPERTS, LANES), lambda t: (*at_out(t), 0, 0))],
        out_shape=[jax.ShapeDtypeStruct((S, T, D_MODEL), F32),
                   jax.ShapeDtypeStruct((S * n * CAP, HS_W), BF),
                   jax.ShapeDtypeStruct((S, T, LANES), I32),
                   jax.ShapeDtypeStruct((S, n, N_EXPERTS, LANES), I32)],
        scratch_shapes=[pltpu.VMEM((TILE, D_MODEL), BF), pltpu.VMEM((TILE, D_MODEL), BF),
                        pltpu.VMEM((H_RET, TILE, TILE), F32)],
        compiler_params=_params(("arbitrary",)),
        name="mixer_lat" if cached else "mixer_ctx",
    )(*args)


def _block_schedule(padded, chunks_a):
    NT = padded.shape[0]
    n_chunks = NT * CHUNKS_PER_TILE
    n_blocks = n_chunks // CPB + N_EXPERTS
    c16 = padded // CHUNK
    ends = jnp.cumsum(c16, axis=1)
    before = jnp.cumsum(c16, axis=0) - c16
    per_expert = jnp.sum(c16, axis=0)
    nb = (per_expert + (CPB - 1)) // CPB
    b_end = jnp.cumsum(nb)
    q = jnp.arange(CHUNKS_PER_TILE, dtype=I32)
    e = jnp.arange(N_EXPERTS, dtype=I32)
    key = jnp.sum((ends[:, None, :] <= q[None, :, None]).astype(I32), axis=-1)
    pos_e = ((b_end - nb) * CPB + before)[:, None, :] + (q[None, :, None] - (ends - c16)[:, None, :])
    pos = jnp.sum(jnp.where(key[..., None] == e, pos_e, 0), axis=-1)
    pos = jnp.where(key < N_EXPERTS, pos, -1).reshape(-1)
    match = pos[None, :] == jnp.arange(n_blocks * CPB, dtype=I32)[:, None]
    cid = jnp.sum(jnp.where(match, jnp.arange(n_chunks, dtype=I32)[None, :], 0), axis=-1)
    nv = jnp.sum(match.astype(I32).reshape(n_blocks, -1), axis=-1)
    first = match & (jnp.arange(n_chunks, dtype=I32) < chunks_a)[None, :]
    nv_a = jnp.sum(first.astype(I32).reshape(n_blocks, -1), axis=-1)
    b = jnp.arange(n_blocks, dtype=I32)
    eb = jnp.minimum(jnp.sum((b_end[None, :] <= b[:, None]).astype(I32), axis=-1), N_EXPERTS - 1)
    zero = jnp.zeros((1,), I32)
    counts = jnp.concatenate([nv, zero, nv_a, zero]).astype(I32)
    return eb.astype(I32), counts, jnp.concatenate([cid, jnp.zeros((CPB,), I32)]).astype(I32)


def _moe_kernel(eb_ref, nv_ref, cid_ref, hs_a, hs_b, wg_ref, wu_ref, wd_ref, ys_a, ys_b,
                xbuf, ybuf, wg_b, wu_b, wd_b, in_sem, out_sem, *, n_blocks, chunks_a):
    b = pl.program_id(0)

    def chunk_rows(c):
        return pl.ds(pl.multiple_of(c * CHUNK, CHUNK), CHUNK)

    def gather_copy(which, c, j, slot):
        src = (hs_a, hs_b)[which]
        return pltpu.make_async_copy(src.at[chunk_rows(c)], xbuf.at[slot, chunk_rows(j)], in_sem.at[slot])

    def scatter_copy(which, c, j, slot):
        dst = (ys_a, ys_b)[which]
        return pltpu.make_async_copy(ybuf.at[slot, chunk_rows(j)],
                                     dst.at[chunk_rows(c), pl.ds(0, D_MODEL)], out_sem.at[slot])

    def for_chunks(blk, fn):
        n_a = nv_ref[n_blocks + 1 + blk]

        def body_a(j, carry):
            fn(j, 0, cid_ref[blk * CPB + j])
            return carry

        def body_b(j, carry):
            fn(j, 1, cid_ref[blk * CPB + j] - chunks_a)
            return carry
        lax.fori_loop(0, n_a, body_a, 0)
        lax.fori_loop(n_a, nv_ref[blk], body_b, 0)

    def each_chunk(blk, fn):
        n_a = nv_ref[n_blocks + 1 + blk]
        n_v = nv_ref[blk]
        for j in range(CPB):
            c = cid_ref[blk * CPB + j]

            @pl.when(j < n_a)
            def _():
                fn(j, 0, c)

            @pl.when((j >= n_a) & (j < n_v))
            def _():
                fn(j, 1, c - chunks_a)

    def wait_gathers(blk, slot):
        rows = pl.ds(0, nv_ref[blk] * CHUNK)

        @pl.when(nv_ref[blk] > 0)
        def _():
            pltpu.make_async_copy(hs_a.at[rows], xbuf.at[slot, rows], in_sem.at[slot]).wait()

    def wait_scatters(blk, slot):
        rows = pl.ds(0, nv_ref[blk] * CHUNK)

        @pl.when(nv_ref[blk] > 0)
        def _():
            pltpu.make_async_copy(ybuf.at[slot, rows], ys_a.at[rows, pl.ds(0, D_MODEL)], out_sem.at[slot]).wait()

    slot = lax.rem(b, 2)
    other = 1 - slot

    @pl.when(b == 0)
    def _():
        xbuf[...] = jnp.zeros_like(xbuf)
        for_chunks(0, lambda j, w, c: gather_copy(w, c, j, 0).start())

    @pl.when((b == 0) | (eb_ref[b] != eb_ref[jnp.maximum(b - 1, 0)]))
    def _():
        wg_b[...] = wg_ref[...].astype(BF)
        wu_b[...] = wu_ref[...].astype(BF)
        wd_b[...] = wd_ref[...].astype(BF)

    wait_gathers(b, slot)

    @pl.when(b >= 2)
    def _():
        wait_scatters(b - 2, slot)

    @pl.when(nv_ref[b] > 0)
    def _():
        each_chunk(b + 1, lambda j, w, c: gather_copy(w, c, j, other).start())
        x = xbuf[slot, :, 0:D_MODEL]
        gate = (xbuf[slot, :, D_MODEL:D_MODEL + 1].astype(F32)
                + xbuf[slot, :, D_MODEL + 1:D_MODEL + 2].astype(F32))
        hg = _dot(x, wg_b[...])
        hu = _dot(x, wu_b[...])
        act = _silu(hg) * hu * gate
        ybuf[slot] = _dot(act.astype(BF), wd_b[...]).astype(BF)
        each_chunk(b, lambda j, w, c: scatter_copy(w, c, j, slot).start())

    @pl.when(b == n_blocks - 1)
    def _():
        if n_blocks >= 2:
            wait_scatters(b - 1, other)
        wait_scatters(b, slot)


def _moe(hs_a, hs_b, eb, nv, cid, layer, w_gate, w_up, w_down):
    n_blocks = eb.shape[0]

    def w_spec(a, c):
        return pl.BlockSpec((None, None, a, c), lambda b, eb, nv, cid: (layer, eb[b], 0, 0))

    grid_spec = pltpu.PrefetchScalarGridSpec(
        num_scalar_prefetch=3,
        grid=(n_blocks,),
        in_specs=[_HBM, _HBM, w_spec(D_MODEL, D_EXPERT), w_spec(D_MODEL, D_EXPERT), w_spec(D_EXPERT, D_MODEL)],
        out_specs=[_HBM, _HBM],
        scratch_shapes=[pltpu.VMEM((2, MOE_TMB, HS_W), BF), pltpu.VMEM((2, MOE_TMB, D_MODEL), BF),
                        pltpu.VMEM((D_MODEL, D_EXPERT), BF), pltpu.VMEM((D_MODEL, D_EXPERT), BF),
                        pltpu.VMEM((D_EXPERT, D_MODEL), BF),
                        pltpu.SemaphoreType.DMA((2,)), pltpu.SemaphoreType.DMA((2,))],
    )
    return pl.pallas_call(
        functools.partial(_moe_kernel, n_blocks=n_blocks, chunks_a=hs_a.shape[0] // CHUNK),
        grid_spec=grid_spec,
        out_shape=[jax.ShapeDtypeStruct(hs_a.shape, BF), jax.ShapeDtypeStruct(hs_b.shape, BF)],
        input_output_aliases={3: 0, 4: 1},
        compiler_params=_params(("arbitrary",)),
        name="moe_sorted",
    )(eb, nv, cid, hs_a, hs_b, w_gate, w_up, w_down)


def _combine_kernel(ys_ref, slots_ref, x1_ref, mod_ref, lng_ref, lnb_ref, o_ref):
    g2 = mod_ref[:, 5 * D_MODEL:6 * D_MODEL]
    s_lane = _iota((TILE, CAP), 1)
    for u in range(COMBINE_TILES_PER_STEP):
        sl = slots_ref[u]
        pick = _ones_where((s_lane == sl[:, 0:1]) | (s_lane == sl[:, 1:2]))
        ffn = _dot(pick, ys_ref[u * CAP:(u + 1) * CAP, 0:D_MODEL])
        o_ref[u] = _layer_norm(ALPHA * x1_ref[u] + g2 * ffn, lng_ref[...], lnb_ref[...])


def _combine(ys, slots, x1, mod, layer, row_fn, lng, lnb):
    S, T, _ = x1.shape
    n = T // TILE
    U = COMBINE_TILES_PER_STEP
    assert (S * n) % U == 0 and (n % U == 0 or n == 1)

    def tok(w):
        return pl.BlockSpec((U, TILE, w), lambda t: (t, 0, 0))

    out = pl.pallas_call(
        _combine_kernel,
        grid=(S * n // U,),
        in_specs=[
            pl.BlockSpec((U * CAP, HS_W), lambda t: (t, 0)),
            tok(LANES), tok(D_MODEL),
            pl.BlockSpec((None, None, 1, 6 * D_MODEL), lambda t: (layer, row_fn(t * U // n), 0, 0)),
            pl.BlockSpec((1, D_MODEL), lambda t: (0, 0)),
            pl.BlockSpec((1, D_MODEL), lambda t: (0, 0)),
        ],
        out_specs=tok(D_MODEL),
        out_shape=jax.ShapeDtypeStruct((S * n, TILE, D_MODEL), F32),
        compiler_params=_params(("parallel",)),
        name="moe_combine",
    )(ys, slots.reshape(S * n, TILE, LANES), x1.reshape(S * n, TILE, D_MODEL), mod, lng, lnb)
    return out.reshape(S, T, D_MODEL)


def _rope_tables(n_lat):
    half = DIFF_QK // 2
    pairs = half // 2
    inv = (1.0 / (ROPE_BASE ** (np.arange(pairs, dtype=np.float32) * 2.0 / half))).astype(np.float32)
    t = np.arange(n_lat)
    ang_r = ((t // GRID_W).astype(np.float32)[:, None] * inv[None, :]).astype(np.float64)
    ang_c = ((t % GRID_W).astype(np.float32)[:, None] * inv[None, :]).astype(np.float64)
    cos = np.concatenate([np.cos(ang_r)] * 2 + [np.cos(ang_c)] * 2, axis=1)
    sin = np.concatenate([-np.sin(ang_r), np.sin(ang_r), -np.sin(ang_c), np.sin(ang_c)], axis=1)
    reps = DIFF_W // DIFF_QK
    return (jnp.asarray(np.tile(cos, (1, reps)), F32), jnp.asarray(np.tile(sin, (1, reps)), F32))


def _block_diag(s):
    S = s.shape[0]
    eye = jnp.eye(H_RET, dtype=s.dtype)
    return jnp.einsum('shdv,hg->shdgv', s, eye).reshape(S, RET_W, RET_W)


def _router_rows(w_group, b_group, w_expert, b_expert):
    pad = EXPERT_ROW0 - N_GROUPS
    tail = ROUTER_ROWS - EXPERT_ROW0 - N_EXPERTS
    w = jnp.concatenate([w_group.T, jnp.zeros((pad, D_MODEL), F32),
                         w_expert.reshape(D_MODEL, N_EXPERTS).T, jnp.zeros((tail, D_MODEL), F32)], axis=0)
    bias = jnp.concatenate([b_group, jnp.zeros((pad,), F32), b_expert.reshape(N_EXPERTS), jnp.zeros((tail,), F32)])
    hi = w.astype(BF)
    return hi, (w - hi.astype(F32)).astype(BF), bias.reshape(ROUTER_ROWS, 1)


def kernel(x_prompt, x_sample, cache_diff_k, cache_diff_v, state_ret_fwd, state_ret_bwd, c, c_ctx, w_mod, b_mod, w_in, ret_decay_logit, diff_lambda, diff_subln_g, conv_w, conv_b, w_out, ln_g, ln_b, w_router_group, b_router_group, w_router_expert, b_router_expert, w_gate, w_up, w_down):
    B, T_ctx, _ = x_prompt.shape
    Bd, T_lat, _ = x_sample.shape
    assert T_ctx == TILE and T_lat % TILE == 0 and T_lat % GRID_W == 0
    assert 1 + Bd <= MOD_ROWS

    cond = jnp.concatenate([c_ctx[None, :], c, jnp.zeros((MOD_ROWS - 1 - Bd, D_MODEL), F32)], axis=0)
    mod = _modulation(cond, w_mod, b_mod).reshape(DEPTH, MOD_ROWS, 1, 6 * D_MODEL)
    rope_tabs = _rope_tables(T_lat)
    w_in_b = w_in.astype(BF)
    w_out_b = w_out.astype(BF)

    ctx_row = lambda s: 0
    lat_row = lambda s: s + 1
    ctx_tiles = B * (T_ctx // TILE)
    lat_tiles = Bd * (T_lat // TILE)

    yp, ys = x_prompt, x_sample
    stacked = None
    for l in range(DEPTH):
        lam_init = 0.8 - 0.6 * math.exp(-0.3 * l)
        lg_lanes = jnp.repeat(ret_decay_logit[l], RET_DK, axis=1)
        wrh, wrl, br = _router_rows(w_router_group[l], b_router_group[l], w_router_expert[l], b_router_expert[l])
        shared = (w_out_b, lg_lanes, diff_lambda[l], diff_subln_g[l].reshape(1, DIFF_V), conv_w[l],
                  conv_b[l].reshape(1, CONV_CH), ln_g[l, 0].reshape(1, D_MODEL), ln_b[l, 0].reshape(1, D_MODEL),
                  wrh, wrl, br, lam_init)
        ln2 = (ln_g[l, 1].reshape(1, D_MODEL), ln_b[l, 1].reshape(1, D_MODEL))

        pr = _proj(yp, mod, l, ctx_row, w_in_b, lg_lanes, stacked=stacked)
        rq, rk, rv, rg, dq, dk, dv, cb, cch = pr[:9]
        stacked = pr[9:]
        x1_c, hs_c, slots_c, cnt_c = _mixer(yp, mod, l, ctx_row, (rq, rk, rv, rg, dq, cb, cch), None, None,
                                            (dk, dv), *shared)

        pr = _proj(ys, mod, l, lat_row, w_in_b, lg_lanes, rope_tabs=rope_tabs)
        rq, rk, rv, rg, dq, dk, dv, cb, cch, kvf, kvb = pr
        states = _scan(kvf, kvb, _block_diag(state_ret_fwd[:, l]), _block_diag(state_ret_bwd[:, l]), lg_lanes)
        cache_v = cache_diff_v[:, l].astype(BF)
        cache_kv = (cache_diff_k[:, l].reshape(Bd, -1, DIFF_W).astype(BF),
                    jnp.concatenate([cache_v, jnp.ones_like(cache_v)], axis=-1).reshape(Bd, -1, V_AUG_W))
        x1_l, hs_l, slots_l, cnt_l = _mixer(ys, mod, l, lat_row, (rq, rk, rv, rg, dq, cb, cch), states, cache_kv,
                                            (dk, dv), *shared)

        padded = jnp.concatenate([cnt_c.reshape(ctx_tiles, N_EXPERTS, LANES)[:, :, 0],
                                  cnt_l.reshape(lat_tiles, N_EXPERTS, LANES)[:, :, 0]], axis=0)
        eb, nv, cid = _block_schedule(padded, ctx_tiles * CHUNKS_PER_TILE)
        out_c, out_l = _moe(hs_c, hs_l, eb, nv, cid, l, w_gate, w_up, w_down)
        yp = _combine(out_c, slots_c, x1_c, mod, l, ctx_row, *ln2)
        ys = _combine(out_l, slots_l, x1_l, mod, l, lat_row, *ln2)

    new_k, new_v, st_f, st_b = stacked
    return (yp, ys, new_k.reshape(B, DEPTH, T_ctx, H_DIFF, 2, DIFF_QK),
            new_v.reshape(B, DEPTH, T_ctx, H_DIFF, DIFF_V), st_f, st_b)
```

```python
import functools
import math

import numpy as np
import jax
import jax.numpy as jnp
from jax import lax
from jax.experimental import pallas as pl
from jax.experimental.pallas import tpu as pltpu

D_MODEL = 1024
DEPTH = 2
GRID_W = 64
H_RET = 4
RET_DK = 64
RET_W = H_RET * RET_DK
H_DIFF = 4
DIFF_QK = 64
DIFF_V = 2 * DIFF_QK
DIFF_W = H_DIFF * DIFF_V
CONV_CH = 256
ROPE_BASE = 10000.0
N_GROUPS = 4
EXPERTS_PER_GROUP = 4
N_EXPERTS = N_GROUPS * EXPERTS_PER_GROUP
D_EXPERT = 512
ALPHA = (2 * DEPTH) ** 0.25
EPS = 1e-5
MOD_ROWS = 8
LANES = 128
ROUTER_ROWS = 32
EXPERT_ROW0 = 8

TILE = 256
CHUNK = 16
CAP = 2 * TILE + N_EXPERTS * CHUNK
CHUNKS_PER_TILE = CAP // CHUNK
COMBINE_TILES_PER_STEP = 4
PROJ_TILES = 4
MOE_TMB = 1024
CPB = MOE_TMB // CHUNK
HS_W = D_MODEL + LANES
V_AUG_W = 2 * DIFF_W
QK_SCALE_LOG2 = (DIFF_QK ** -0.5) * math.log2(math.e)
SLOT_RADIX = 32
RET_DK_BITS = RET_DK.bit_length() - 1
GROUP_BITS = EXPERTS_PER_GROUP.bit_length() - 1
CHUNK_BITS = CHUNK.bit_length() - 1
SLOT_BITS = SLOT_RADIX.bit_length() - 1
assert (1 << RET_DK_BITS, 1 << GROUP_BITS, 1 << CHUNK_BITS, 1 << SLOT_BITS) == (
    RET_DK, EXPERTS_PER_GROUP, CHUNK, SLOT_RADIX)

_O_RQ, _O_RK, _O_RV, _O_RG = 0, 256, 512, 768
_O_DQ, _O_DK, _O_DV = 1024, 1536, 2048
_O_CB, _O_CC, _O_CH = 2560, 2816, 3072
IN_WIDTH = 3328

BF = jnp.bfloat16
F32 = jnp.float32
I32 = jnp.int32

_VMEM_LIMIT = 60 * 1024 * 1024


def _dot(a, b):
    return jnp.dot(a, b, preferred_element_type=F32)


def _dot_nt(a, b):
    return lax.dot_general(a, b, (((1,), (1,)), ((), ())), preferred_element_type=F32)


def _dot_tn(a, b):
    return lax.dot_general(a, b, (((0,), (0,)), ((), ())), preferred_element_type=F32)


def _split(x):
    hi = x.astype(BF)
    lo = (x - hi.astype(F32)).astype(BF)
    return hi, lo


def _dot_hl(x, w_bf16):
    hi, lo = _split(x)
    return _dot(hi, w_bf16) + _dot(lo, w_bf16)


def _dot3(x, w_hi, w_lo):
    hi, lo = _split(x)
    return _dot(hi, w_hi) + (_dot(lo, w_hi) + _dot(hi, w_lo))


def _iota(shape, dim):
    return lax.broadcasted_iota(I32, shape, dim)


def _ones_where(cond):
    return jnp.where(cond, 1.0, 0.0).astype(BF)


def _log_sigmoid(x):
    return jnp.minimum(x, 0.0) - jnp.log1p(jnp.exp(-jnp.abs(x)))


def _silu(x):
    return x * jax.nn.sigmoid(x)


def _layer_norm(y, g, b):
    mu = jnp.mean(y, axis=-1, keepdims=True)
    yc = y - mu
    var = jnp.mean(yc * yc, axis=-1, keepdims=True)
    return yc * lax.rsqrt(var + EPS) * g + b


def _params(sem):
    return pltpu.CompilerParams(dimension_semantics=sem, vmem_limit_bytes=_VMEM_LIMIT)


_HBM = pl.BlockSpec(memory_space=pltpu.HBM)


def _mod_kernel(c_ref, w_ref, b_ref, o_ref):
    a = _silu(c_ref[...])
    w_hi, w_lo = _split(w_ref[...])
    o_ref[...] = _dot3(a, w_hi, w_lo) + b_ref[...]


def _modulation(cond, w_mod, b_mod):
    tn = 1536
    n6 = 6 * D_MODEL
    return pl.pallas_call(
        _mod_kernel,
        grid=(DEPTH, n6 // tn),
        in_specs=[
            pl.BlockSpec((MOD_ROWS, D_MODEL), lambda l, j: (0, 0)),
            pl.BlockSpec((None, D_MODEL, tn), lambda l, j: (l, 0, j)),
            pl.BlockSpec((None, 1, tn), lambda l, j: (l, 0, j)),
        ],
        out_specs=pl.BlockSpec((None, MOD_ROWS, tn), lambda l, j: (l, 0, j)),
        out_shape=jax.ShapeDtypeStruct((DEPTH, MOD_ROWS, n6), F32),
        compiler_params=_params(("parallel", "parallel")),
        name="modulation",
    )(cond, w_mod, b_mod.reshape(DEPTH, 1, n6))


def _augment_v(v):
    ones = jnp.ones((v.shape[0], DIFF_V), v.dtype)
    parts = []
    for hd in range(H_DIFF):
        parts += [v[:, hd * DIFF_V:(hd + 1) * DIFF_V], ones]
    return jnp.concatenate(parts, axis=1)


def _swap16(x):
    n = x.shape[-1]
    lane = _iota(x.shape, 1)
    return jnp.where((lane & 16) == 0, pltpu.roll(x, n - 16, axis=1), pltpu.roll(x, 16, axis=1))


def _proj_kernel(*refs, latent, n_aliased, layer):
    x_ref, mod_ref, w_ref, lg_ref = refs[:4]
    pos = 4
    if latent:
        cos_ref, sin_ref = refs[pos:pos + 2]
        pos += 2
    pos += n_aliased
    rq_ref, rk_ref, rv_ref, rg_ref, dq_ref, dk_ref, dv_ref, cb_ref, cch_ref = refs[pos:pos + 9]
    pos += 9
    if latent:
        kvf_ref, kvb_ref = refs[pos:pos + 2]
    else:
        dk32_ref, dv32_ref, stf_ref, stb_ref = refs[pos:pos + 4]
    rows = PROJ_TILES * TILE

    def put(ref, u, idx, val):
        if n_aliased:
            ref[(u,) + idx] = val
        else:
            for other in range(DEPTH):
                ref[(u, other) + idx] = val if other == layer else jnp.zeros_like(val)

    def store(ref, val):
        ref[...] = val.reshape(PROJ_TILES, TILE, val.shape[-1])

    m = mod_ref[...]
    sh1 = m[:, 0:D_MODEL]
    sc1 = m[:, D_MODEL:2 * D_MODEL]
    h = (x_ref[...].reshape(rows, D_MODEL) * (1.0 + sc1) + sh1).astype(BF)

    def col(off, width):
        return _dot(h, w_ref[:, off:off + width])

    store(rq_ref, col(_O_RQ, RET_W).astype(BF))
    rk = col(_O_RK, RET_W) * (RET_DK ** -0.5)
    store(rk_ref, rk.astype(BF))
    rv = col(_O_RV, RET_W).astype(BF)
    store(rv_ref, rv)
    store(rg_ref, col(_O_RG, RET_W))

    lg = _log_sigmoid(lg_ref[...])
    p = (_iota((rows, 1), 0) & (TILE - 1)).astype(F32)
    kf = (rk * jnp.exp((TILE - 1.0 - p) * lg[0:1])).astype(BF)
    kb = (rk * jnp.exp(p * lg[1:2])).astype(BF)
    for u in range(PROJ_TILES):
        tile = slice(u * TILE, (u + 1) * TILE)
        kvf = _dot_tn(kf[tile], rv[tile])
        kvb = _dot_tn(kb[tile], rv[tile])
        if latent:
            kvf_ref[u] = kvf
            kvb_ref[u] = kvb
        else:
            for hd in range(H_RET):
                lo = hd * RET_DK
                put(stf_ref, u, (hd,), kvf[lo:lo + RET_DK, lo:lo + RET_DK])
                put(stb_ref, u, (hd,), kvb[lo:lo + RET_DK, lo:lo + RET_DK])

    dq = col(_O_DQ, DIFF_W)
    dk = col(_O_DK, DIFF_W)
    dv = col(_O_DV, DIFF_W)
    if latent:
        cos = cos_ref[...].reshape(rows, DIFF_W)
        sin = sin_ref[...].reshape(rows, DIFF_W)
        dq = dq * cos + _swap16(dq) * sin
        dk = dk * cos + _swap16(dk) * sin
    else:
        for u in range(PROJ_TILES):
            put(dk32_ref, u, (), dk[u * TILE:(u + 1) * TILE])
            put(dv32_ref, u, (), dv[u * TILE:(u + 1) * TILE])
    store(dq_ref, (dq * QK_SCALE_LOG2).astype(BF))
    store(dk_ref, dk.astype(BF))
    store(dv_ref, _augment_v(dv.astype(BF)))

    store(cb_ref, col(_O_CB, CONV_CH))
    store(cch_ref, col(_O_CC, CONV_CH) * col(_O_CH, CONV_CH))


def _proj(x, mod, layer, row_fn, w_in_b, lg_lanes, rope_tabs=None, stacked=None):
    S, T, _ = x.shape
    n = T // TILE
    G = PROJ_TILES
    tiles = S * n
    latent = rope_tabs is not None
    assert tiles % G == 0 and (n % G == 0 or n == 1)

    def tok(w):
        return pl.BlockSpec((G, TILE, w), lambda t: (t, 0, 0))

    def tok_shape(w, dt):
        return jax.ShapeDtypeStruct((tiles, TILE, w), dt)

    in_specs = [
        tok(D_MODEL),
        pl.BlockSpec((None, None, 1, 6 * D_MODEL), lambda t: (layer, row_fn(t * G // n), 0, 0)),
        pl.BlockSpec((None, D_MODEL, IN_WIDTH), lambda t: (layer, 0, 0)),
        pl.BlockSpec((2, RET_W), lambda t: (0, 0)),
    ]
    args = [x.reshape(tiles, TILE, D_MODEL), mod, w_in_b, lg_lanes]
    out_specs = [tok(RET_W)] * 4 + [tok(DIFF_W)] * 2 + [tok(V_AUG_W)] + [tok(CONV_CH)] * 2
    out_shape = ([tok_shape(RET_W, BF)] * 3 + [tok_shape(RET_W, F32)] + [tok_shape(DIFF_W, BF)] * 2
                 + [tok_shape(V_AUG_W, BF)]
                 + [tok_shape(CONV_CH, F32)] * 2)
    aliases = {}
    if latent:
        in_specs += [pl.BlockSpec((G, TILE, DIFF_W), lambda t: (t % (n // G), 0, 0))] * 2
        args += [tab.reshape(n, TILE, DIFF_W) for tab in rope_tabs]
        out_specs += [pl.BlockSpec((G, RET_W, RET_W), lambda t: (t, 0, 0))] * 2
        out_shape += [jax.ShapeDtypeStruct((tiles, RET_W, RET_W), F32)] * 2
    else:
        assert n == 1
        kv_shape = jax.ShapeDtypeStruct((S, DEPTH, T, DIFF_W), F32)
        st_shape = jax.ShapeDtypeStruct((S, DEPTH, H_RET, RET_DK, RET_DK), F32)
        if stacked is not None:
            in_specs += [_HBM] * 4
            args += list(stacked)
            aliases = {len(args) - 4 + j: len(out_shape) + j for j in range(4)}
            out_specs += [pl.BlockSpec((G, None, TILE, DIFF_W), lambda t: (t, layer, 0, 0))] * 2
            out_specs += [pl.BlockSpec((G, None, H_RET, RET_DK, RET_DK), lambda t: (t, layer, 0, 0, 0))] * 2
        else:
            out_specs += [pl.BlockSpec((G, DEPTH, TILE, DIFF_W), lambda t: (t, 0, 0, 0))] * 2
            out_specs += [pl.BlockSpec((G, DEPTH, H_RET, RET_DK, RET_DK), lambda t: (t, 0, 0, 0, 0))] * 2
        out_shape += [kv_shape, kv_shape, st_shape, st_shape]
    outs = pl.pallas_call(
        functools.partial(_proj_kernel, latent=latent, n_aliased=len(aliases), layer=layer),
        grid=(tiles // G,),
        in_specs=in_specs,
        out_specs=out_specs,
        out_shape=out_shape,
        input_output_aliases=aliases,
        compiler_params=_params(("parallel",)),
        name="proj_lat" if latent else "proj_ctx",
    )(*args)
    token_outs = [o.reshape(S, T, o.shape[-1]) for o in outs[:9]]
    if latent:
        return token_outs + [o.reshape(S, n, RET_W, RET_W) for o in outs[9:]]
    return token_outs + list(outs[9:])


def _scan_kernel(kvf_ref, kvb_ref, s0f_ref, s0b_ref, lg_ref, sf_ref, sb_ref, *, n):
    lg = _log_sigmoid(lg_ref[...])
    dec = jnp.exp(float(TILE) * lg)
    same_head = (_iota((RET_W, RET_W), 0) >> RET_DK_BITS) == (_iota((RET_W, RET_W), 1) >> RET_DK_BITS)

    s = jnp.where(same_head, s0f_ref[...], 0.0)
    for c in range(n):
        sf_ref[c] = s
        s = s * dec[0:1] + jnp.where(same_head, kvf_ref[c], 0.0)
    s = jnp.where(same_head, s0b_ref[...], 0.0)
    for c in reversed(range(n)):
        sb_ref[c] = s
        s = s * dec[1:2] + jnp.where(same_head, kvb_ref[c], 0.0)


def _scan(kvf, kvb, s0f, s0b, lg_lanes):
    S, n = kvf.shape[:2]
    chunks = pl.BlockSpec((None, n, RET_W, RET_W), lambda s: (s, 0, 0, 0))
    one = pl.BlockSpec((None, RET_W, RET_W), lambda s: (s, 0, 0))
    return pl.pallas_call(
        functools.partial(_scan_kernel, n=n),
        grid=(S,),
        in_specs=[chunks, chunks, one, one, pl.BlockSpec((2, RET_W), lambda s: (0, 0))],
        out_specs=[chunks, chunks],
        out_shape=[jax.ShapeDtypeStruct((S, n, RET_W, RET_W), F32)] * 2,
        compiler_params=_params(("parallel",)),
        name="ret_scan",
    )(kvf, kvb, s0f, s0b, lg_lanes)


def _diff_scores(dq_ref, kv_refs, heads):
    lane = _iota((1, DIFF_V), 1)
    scores = []
    for hd in heads:
        lo = hd * DIFF_V
        q_h = dq_ref[:, lo:lo + DIFF_V]
        qs = jnp.concatenate([jnp.where(lane < DIFF_QK, q_h, 0), jnp.where(lane >= DIFF_QK, q_h, 0)], axis=0)
        scores.append(jnp.concatenate([_dot_nt(qs, k[:, lo:lo + DIFF_V]) for k, _ in kv_refs], axis=1))
    return jnp.concatenate(scores, axis=0)


def _diff_outputs(s, kv_refs, heads, lam):
    p = jnp.exp2(s - jnp.max(s, axis=-1, keepdims=True))

    def times_v(w, hd, width):
        acc = None
        start = 0
        for k, v in kv_refs:
            part = _dot(w[:, start:start + k.shape[0]], v[:, hd * 2 * DIFF_V:hd * 2 * DIFF_V + width])
            acc = part if acc is None else acc + part
            start += k.shape[0]
        return acc

    outs = []
    p = p.astype(BF)
    for n, hd in enumerate(heads):
        acc = times_v(p[n * 2 * TILE:(n + 1) * 2 * TILE], hd, 2 * DIFF_V)
        o0 = acc[0:TILE, 0:DIFF_V] * (1.0 / acc[0:TILE, DIFF_V:DIFF_V + 1])
        o1 = acc[TILE:, 0:DIFF_V] * (lam / acc[TILE:, DIFF_V:DIFF_V + 1])
        outs.append(o0 - o1)
    return outs


def _mixer_kernel(*refs, n_tiles, n_total, lam_init, cached):
    n_a = _n_token_mix_refs(cached)
    mix_even, mix_odd, decay_ref = refs[-3:]
    t = pl.program_id(0)
    tile_in_seq = lax.rem(jnp.minimum(t, n_total - 1), n_tiles)

    @pl.when(t == 0)
    def _():
        mix_odd[...] = jnp.zeros_like(mix_odd)
        lg = _log_sigmoid(refs[n_a - 5][...])
        dist = (_iota((TILE, TILE), 0) - _iota((TILE, TILE), 1)).astype(F32)
        diag2 = jnp.where(dist == 0.0, 2.0, 1.0)
        for hd in range(H_RET):
            lgf = lg[0:1, hd * RET_DK:hd * RET_DK + 1]
            lgb = lg[1:2, hd * RET_DK:hd * RET_DK + 1]
            decay_ref[hd] = jnp.exp(jnp.abs(dist) * jnp.where(dist > 0.0, lgf, lgb)) * diag2

    def step(mix_write, mix_read):
        mix_write[...] = _token_mix(*refs[:n_a], decay_ref, tile_in_seq=tile_in_seq, n_tiles=n_tiles,
                                    lam_init=lam_init, cached=cached)
        _project_route(mix_read[...], *refs[n_a:-3])

    @pl.when(lax.rem(t, 2) == 0)
    def _():
        step(mix_even, mix_odd)

    @pl.when(lax.rem(t, 2) == 1)
    def _():
        step(mix_odd, mix_even)


def _n_token_mix_refs(cached):
    return 18 + (2 if cached else 0)


def _token_mix(*refs, tile_in_seq, n_tiles, lam_init, cached):
    rq_ref, rk_ref, rv_ref, rg_ref, sf_ref, sb_ref, dq_ref = refs[:7]
    pos = 7
    kv_refs = []
    if cached:
        kv_refs.append((refs[pos], refs[pos + 1]))
        pos += 2
    kv_refs.append((refs[pos], refs[pos + 1]))
    pos += 2
    (cb_ref, cch_ref, cprev_ref, cnext_ref, lg_ref, lamp_ref, subg_ref, cw_ref, cbias_ref,
     decay_ref) = refs[pos:pos + 10]
    i = tile_in_seq

    lg = _log_sigmoid(lg_ref[...])
    head_of_lane = _iota((1, RET_W), 1) >> RET_DK_BITS
    q = rq_ref[...]
    k = rk_ref[...]
    v = rv_ref[...]
    ret_o = jnp.zeros((TILE, RET_W), F32)
    for hd in range(H_RET):
        in_head = head_of_lane == hd
        sc = _dot_nt(jnp.where(in_head, q, 0), k)
        ret_o = ret_o + jnp.where(in_head, _dot((sc * decay_ref[hd]).astype(BF), v), 0.0)
    p = _iota((TILE, 1), 0).astype(F32)
    ret_o = ret_o + _dot(q, sf_ref[...].astype(BF)) * jnp.exp((p + 1.0) * lg[0:1])
    ret_o = ret_o + _dot(q, sb_ref[...].astype(BF)) * jnp.exp((float(TILE) - p) * lg[1:2])
    avg = jnp.where((_iota((RET_W, RET_W), 0) >> RET_DK_BITS) == (_iota((RET_W, RET_W), 1) >> RET_DK_BITS),
                    1.0 / RET_DK, 0.0).astype(BF)
    rc = ret_o - _dot_hl(ret_o, avg)
    ret = rc * lax.rsqrt(_dot_hl(rc * rc, avg) + EPS) * _silu(rg_ref[...])

    lp = lamp_ref[...]
    lam = (jnp.exp(jnp.sum(lp[0:1] * lp[1:2], axis=-1, keepdims=True))
           - jnp.exp(jnp.sum(lp[2:3] * lp[3:4], axis=-1, keepdims=True)) + lam_init)
    subg = subg_ref[...] * (1.0 - lam_init)
    head_groups = [[hd] for hd in range(H_DIFF)] if cached else [list(range(H_DIFF))]
    heads = []
    for group in head_groups:
        scores = _diff_scores(dq_ref, kv_refs, group)
        for o in _diff_outputs(scores, kv_refs, group, lam):
            o = o * lax.rsqrt(jnp.mean(o * o, axis=-1, keepdims=True) + EPS) * subg
            heads.append(o.astype(BF))
    diff = jnp.concatenate(heads, axis=1)

    cch = cch_ref[...]
    prev = jnp.where(i > 0, cprev_ref[7:8, :], 0.0)
    nxt = jnp.where(i < n_tiles - 1, cnext_ref[0:1, :], 0.0)
    r = _iota((TILE, 1), 0)
    up = jnp.where(r == 0, prev, pltpu.roll(cch, 1, axis=0))
    dn = jnp.where(r == TILE - 1, nxt, pltpu.roll(cch, TILE - 1, axis=0))
    cw = cw_ref[...]
    conv = cb_ref[...] * (up * cw[0:1] + cch * cw[1:2] + dn * cw[2:3] + cbias_ref[...])
    return jnp.concatenate([ret.astype(BF), diff, conv.astype(BF)], axis=1)


def _project_route(mixed, x_ref, mod_ref, wout_ref, lng_ref, lnb_ref, wrh_ref, wrl_ref, br_ref,
                   x1_ref, hs_ref, slots_ref, cnt_ref):
    m = mod_ref[...]
    g1 = m[:, 2 * D_MODEL:3 * D_MODEL]
    sh2 = m[:, 3 * D_MODEL:4 * D_MODEL]
    sc2 = m[:, 4 * D_MODEL:5 * D_MODEL]
    x1 = _layer_norm(ALPHA * x_ref[...] + g1 * _dot(mixed, wout_ref[...]), lng_ref[...], lnb_ref[...])
    x1_ref[...] = x1

    h2 = x1 * (1.0 + sc2) + sh2
    h2h, h2l = _split(h2)
    wrh = wrh_ref[...]
    logits = (_dot_nt(wrh, h2h) + (_dot_nt(wrh, h2l) + _dot_nt(wrl_ref[...], h2h))) + br_ref[...]
    neg = -jnp.inf
    g_id = _iota((EXPERT_ROW0, TILE), 0)
    gl = jnp.where(g_id < N_GROUPS, logits[0:EXPERT_ROW0], neg)
    gmax = jnp.max(gl, axis=0, keepdims=True)
    g_idx = jnp.min(jnp.where(gl == gmax, g_id, N_GROUPS), axis=0, keepdims=True)
    g_w = 1.0 / jnp.sum(jnp.exp(gl - gmax), axis=0, keepdims=True)
    e_id = _iota((N_EXPERTS, TILE), 0)
    el = jnp.where((e_id >> GROUP_BITS) == g_idx, logits[EXPERT_ROW0:EXPERT_ROW0 + N_EXPERTS], neg)
    v1 = jnp.max(el, axis=0, keepdims=True)
    i1 = jnp.min(jnp.where(el == v1, e_id, N_EXPERTS), axis=0, keepdims=True)
    el2 = jnp.where(e_id == i1, neg, el)
    v2 = jnp.max(el2, axis=0, keepdims=True)
    i2 = jnp.min(jnp.where(el2 == v2, e_id, N_EXPERTS), axis=0, keepdims=True)
    t = jnp.exp(v2 - v1)
    w1 = g_w / (1.0 + t)
    w2 = g_w * t / (1.0 + t)

    sel1 = e_id == i1
    sel2 = e_id == i2
    sel = jnp.where(sel1 | sel2, 1.0, 0.0)
    earlier = _ones_where(_iota((TILE, TILE), 0) < _iota((TILE, TILE), 1))
    rank = _dot(sel.astype(BF), earlier)
    cnt = jnp.sum(sel, axis=1, keepdims=True).astype(I32)
    padded = ((cnt + (CHUNK - 1)) >> CHUNK_BITS) << CHUNK_BITS
    incl = jnp.broadcast_to(padded, (N_EXPERTS, LANES))
    e_row = _iota((N_EXPERTS, LANES), 0)
    for step in (1, 2, 4, 8):
        incl = incl + jnp.where(e_row >= step, pltpu.roll(incl, step, axis=0), 0)
    seg_off = (incl[:, 0:1] - padded).astype(F32)
    spos = seg_off + rank
    slot1 = jnp.sum(jnp.where(sel1, spos, 0.0), axis=0, keepdims=True).astype(I32)
    slot2 = jnp.sum(jnp.where(sel2, spos, 0.0), axis=0, keepdims=True).astype(I32)
    s_id = _iota((CAP, TILE), 0)
    at1 = s_id == slot1
    at2 = s_id == slot2
    hs_ref[:, 0:D_MODEL] = _dot(_ones_where(at1 | at2), h2h).astype(BF)
    w1h = w1.astype(BF).astype(F32)
    w2h = w2.astype(BF).astype(F32)
    ones = jnp.ones((TILE, LANES), BF)
    g_hi = _dot((jnp.where(at1, w1h, 0.0) + jnp.where(at2, w2h, 0.0)).astype(BF), ones)
    g_lo = _dot((jnp.where(at1, w1 - w1h, 0.0) + jnp.where(at2, w2 - w2h, 0.0)).astype(BF), ones)
    lane = _iota((CAP, LANES), 1)
    hs_ref[:, D_MODEL:] = jnp.where(lane == 0, g_hi, jnp.where(lane == 1, g_lo, 0.0)).astype(BF)
    cnt_ref[...] = jnp.broadcast_to(padded, (N_EXPERTS, LANES))

    dr = _iota((LANES, TILE), 0)
    digits = jnp.where(dr == 0, slot1 & (SLOT_RADIX - 1), jnp.where(dr == 1, slot1 >> SLOT_BITS,
             jnp.where(dr == 2, slot2 & (SLOT_RADIX - 1), jnp.where(dr == 3, slot2 >> SLOT_BITS, 0))))
    eye = _ones_where(_iota((TILE, TILE), 0) == _iota((TILE, TILE), 1))
    cols = _dot_nt(eye, digits.astype(F32).astype(BF))
    s1c = (cols[:, 0:1] + SLOT_RADIX * cols[:, 1:2]).astype(I32)
    s2c = (cols[:, 2:3] + SLOT_RADIX * cols[:, 3:4]).astype(I32)
    tl = _iota((TILE, LANES), 1)
    slots_ref[...] = jnp.where(tl == 0, s1c, jnp.where(tl == 1, s2c, 0))


def _mixer(x, mod, layer, row_fn, pr, states, cache_kv, new_kv, w_out_b, lg_lanes, lamp, subg, cw, cbias,
           lng, lnb, wrh, wrl, br, lam_init):
    S, T, _ = x.shape
    n = T // TILE
    rq, rk, rv, rg, dq, cb, cch = pr
    t8 = TILE // 8
    cached = cache_kv is not None
    last = S * n - 1

    def at_mix(t):
        return jnp.minimum(t, last) // n, jnp.minimum(t, last) % n

    def at_out(t):
        return jnp.maximum(t - 1, 0) // n, jnp.maximum(t - 1, 0) % n

    def tok(w, at):
        return pl.BlockSpec((None, TILE, w), lambda t: (*at(t), 0))

    def full(shape):
        return pl.BlockSpec(shape, lambda t: (0,) * len(shape))

    def seq(a):
        return pl.BlockSpec((None,) + a.shape[1:], lambda t: (at_mix(t)[0], 0, 0))

    def halo(offset):
        def index(t):
            s, i = at_mix(t)
            return (s, jnp.clip(i * t8 + offset, 0, T // 8 - 1), 0)
        return pl.BlockSpec((None, 8, CONV_CH), index)

    if states is None:
        zero_state = jnp.zeros((1, 1, RET_W, RET_W), F32)
        states = (zero_state, zero_state)
        state = pl.BlockSpec((None, None, RET_W, RET_W), lambda t: (0, 0, 0, 0))
    else:
        state = pl.BlockSpec((None, None, RET_W, RET_W), lambda t: (*at_mix(t), 0, 0))

    in_specs = [tok(RET_W, at_mix)] * 4 + [state, state, tok(DIFF_W, at_mix)]
    args = [rq, rk, rv, rg, states[0], states[1], dq]
    for kv in ([cache_kv] if cached else []) + [new_kv]:
        in_specs += [seq(kv[0]), seq(kv[1])]
        args += list(kv)
    in_specs += [tok(CONV_CH, at_mix), tok(CONV_CH, at_mix), halo(-1), halo(t8),
                 full((2, RET_W)), full((4, DIFF_QK)), full((1, DIFF_V)), full((3, CONV_CH)), full((1, CONV_CH))]
    args += [cb, cch, cch, cch, lg_lanes, lamp, subg, cw, cbias]
    assert len(args) == _n_token_mix_refs(cached)
    in_specs += [tok(D_MODEL, at_out),
                 pl.BlockSpec((None, None, 1, 6 * D_MODEL), lambda t: (layer, row_fn(at_out(t)[0]), 0, 0)),
                 pl.BlockSpec((None, D_MODEL, D_MODEL), lambda t: (layer, 0, 0)),
                 full((1, D_MODEL)), full((1, D_MODEL)),
                 full((ROUTER_ROWS, D_MODEL)), full((ROUTER_ROWS, D_MODEL)), full((ROUTER_ROWS, 1))]
    args += [x, mod, w_out_b, lng, lnb, wrh, wrl, br]

    def out_tile(t):
        s, i = at_out(t)
        return s * n + i

    return pl.pallas_call(
        functools.partial(_mixer_kernel, n_tiles=n, n_total=S * n, lam_init=lam_init, cached=cached),
        grid=(S * n + 1,),
        in_specs=in_specs,
        out_specs=[tok(D_MODEL, at_out),
                   pl.BlockSpec((CAP, HS_W), lambda t: (out_tile(t), 0)),
                   tok(LANES, at_out),
                   pl.BlockSpec((None, None, N_EXPERTS, LANES), lambda t: (*at_out(t), 0, 0))],
        out_shape=[jax.ShapeDtypeStruct((S, T, D_MODEL), F32),
                   jax.ShapeDtypeStruct((S * n * CAP, HS_W), BF),
                   jax.ShapeDtypeStruct((S, T, LANES), I32),
                   jax.ShapeDtypeStruct((S, n, N_EXPERTS, LANES), I32)],
        scratch_shapes=[pltpu.VMEM((TILE, D_MODEL), BF), pltpu.VMEM((TILE, D_MODEL), BF),
                        pltpu.VMEM((H_RET, TILE, TILE), F32)],
        compiler_params=_params(("arbitrary",)),
        name="mixer_lat" if cached else "mixer_ctx",
    )(*args)


def _block_schedule(padded, chunks_a):
    NT = padded.shape[0]
    n_chunks = NT * CHUNKS_PER_TILE
    n_blocks = n_chunks // CPB + N_EXPERTS
    c16 = padded // CHUNK
    ends = jnp.cumsum(c16, axis=1)
    before = jnp.cumsum(c16, axis=0) - c16
    per_expert = jnp.sum(c16, axis=0)
    nb = (per_expert + (CPB - 1)) // CPB
    b_end = jnp.cumsum(nb)
    q = jnp.arange(CHUNKS_PER_TILE, dtype=I32)
    e = jnp.arange(N_EXPERTS, dtype=I32)
    key = jnp.sum((ends[:, None, :] <= q[None, :, None]).astype(I32), axis=-1)
    pos_e = ((b_end - nb) * CPB + before)[:, None, :] + (q[None, :, None] - (ends - c16)[:, None, :])
    pos = jnp.sum(jnp.where(key[..., None] == e, pos_e, 0), axis=-1)
    pos = jnp.where(key < N_EXPERTS, pos, -1).reshape(-1)
    match = pos[None, :] == jnp.arange(n_blocks * CPB, dtype=I32)[:, None]
    cid = jnp.sum(jnp.where(match, jnp.arange(n_chunks, dtype=I32)[None, :], 0), axis=-1)
    nv = jnp.sum(match.astype(I32).reshape(n_blocks, -1), axis=-1)
    first = match & (jnp.arange(n_chunks, dtype=I32) < chunks_a)[None, :]
    nv_a = jnp.sum(first.astype(I32).reshape(n_blocks, -1), axis=-1)
    b = jnp.arange(n_blocks, dtype=I32)
    eb = jnp.minimum(jnp.sum((b_end[None, :] <= b[:, None]).astype(I32), axis=-1), N_EXPERTS - 1)
    zero = jnp.zeros((1,), I32)
    counts = jnp.concatenate([nv, zero, nv_a, zero]).astype(I32)
    return eb.astype(I32), counts, jnp.concatenate([cid, jnp.zeros((CPB,), I32)]).astype(I32)


def _moe_kernel(eb_ref, nv_ref, cid_ref, hs_a, hs_b, wg_ref, wu_ref, wd_ref, ys_a, ys_b,
                xbuf, ybuf, wg_b, wu_b, wd_b, in_sem, out_sem, *, n_blocks, chunks_a):
    b = pl.program_id(0)

    def chunk_rows(c):
        return pl.ds(pl.multiple_of(c * CHUNK, CHUNK), CHUNK)

    def gather_copy(which, c, j, slot):
        src = (hs_a, hs_b)[which]
        return pltpu.make_async_copy(src.at[chunk_rows(c)], xbuf.at[slot, chunk_rows(j)], in_sem.at[slot])

    def scatter_copy(which, c, j, slot):
        dst = (ys_a, ys_b)[which]
        return pltpu.make_async_copy(ybuf.at[slot, chunk_rows(j)],
                                     dst.at[chunk_rows(c), pl.ds(0, D_MODEL)], out_sem.at[slot])

    def for_chunks(blk, fn):
        n_a = nv_ref[n_blocks + 1 + blk]

        def body_a(j, carry):
            fn(j, 0, cid_ref[blk * CPB + j])
            return carry

        def body_b(j, carry):
            fn(j, 1, cid_ref[blk * CPB + j] - chunks_a)
            return carry
        lax.fori_loop(0, n_a, body_a, 0)
        lax.fori_loop(n_a, nv_ref[blk], body_b, 0)

    def each_chunk(blk, fn):
        n_a = nv_ref[n_blocks + 1 + blk]
        n_v = nv_ref[blk]
        for j in range(CPB):
            c = cid_ref[blk * CPB + j]

            @pl.when(j < n_a)
            def _():
                fn(j, 0, c)

            @pl.when((j >= n_a) & (j < n_v))
            def _():
                fn(j, 1, c - chunks_a)

    def wait_gathers(blk, slot):
        rows = pl.ds(0, nv_ref[blk] * CHUNK)

        @pl.when(nv_ref[blk] > 0)
        def _():
            pltpu.make_async_copy(hs_a.at[rows], xbuf.at[slot, rows], in_sem.at[slot]).wait()

    def wait_scatters(blk, slot):
        rows = pl.ds(0, nv_ref[blk] * CHUNK)

        @pl.when(nv_ref[blk] > 0)
        def _():
            pltpu.make_async_copy(ybuf.at[slot, rows], ys_a.at[rows, pl.ds(0, D_MODEL)], out_sem.at[slot]).wait()

    slot = lax.rem(b, 2)
    other = 1 - slot

    @pl.when(b == 0)
    def _():
        xbuf[...] = jnp.zeros_like(xbuf)
        for_chunks(0, lambda j, w, c: gather_copy(w, c, j, 0).start())

    @pl.when((b == 0) | (eb_ref[b] != eb_ref[jnp.maximum(b - 1, 0)]))
    def _():
        wg_b[...] = wg_ref[...].astype(BF)
        wu_b[...] = wu_ref[...].astype(BF)
        wd_b[...] = wd_ref[...].astype(BF)

    wait_gathers(b, slot)

    @pl.when(b >= 2)
    def _():
        wait_scatters(b - 2, slot)

    @pl.when(nv_ref[b] > 0)
    def _():
        each_chunk(b + 1, lambda j, w, c: gather_copy(w, c, j, other).start())
        x = xbuf[slot, :, 0:D_MODEL]
        gate = (xbuf[slot, :, D_MODEL:D_MODEL + 1].astype(F32)
                + xbuf[slot, :, D_MODEL + 1:D_MODEL + 2].astype(F32))
        hg = _dot(x, wg_b[...])
        hu = _dot(x, wu_b[...])
        act = _silu(hg) * hu * gate
        ybuf[slot] = _dot(act.astype(BF), wd_b[...]).astype(BF)
        each_chunk(b, lambda j, w, c: scatter_copy(w, c, j, slot).start())

    @pl.when(b == n_blocks - 1)
    def _():
        if n_blocks >= 2:
            wait_scatters(b - 1, other)
        wait_scatters(b, slot)


def _moe(hs_a, hs_b, eb, nv, cid, layer, w_gate, w_up, w_down):
    n_blocks = eb.shape[0]

    def w_spec(a, c):
        return pl.BlockSpec((None, None, a, c), lambda b, eb, nv, cid: (layer, eb[b], 0, 0))

    grid_spec = pltpu.PrefetchScalarGridSpec(
        num_scalar_prefetch=3,
        grid=(n_blocks,),
        in_specs=[_HBM, _HBM, w_spec(D_MODEL, D_EXPERT), w_spec(D_MODEL, D_EXPERT), w_spec(D_EXPERT, D_MODEL)],
        out_specs=[_HBM, _HBM],
        scratch_shapes=[pltpu.VMEM((2, MOE_TMB, HS_W), BF), pltpu.VMEM((2, MOE_TMB, D_MODEL), BF),
                        pltpu.VMEM((D_MODEL, D_EXPERT), BF), pltpu.VMEM((D_MODEL, D_EXPERT), BF),
                        pltpu.VMEM((D_EXPERT, D_MODEL), BF),
                        pltpu.SemaphoreType.DMA((2,)), pltpu.SemaphoreType.DMA((2,))],
    )
    return pl.pallas_call(
        functools.partial(_moe_kernel, n_blocks=n_blocks, chunks_a=hs_a.shape[0] // CHUNK),
        grid_spec=grid_spec,
        out_shape=[jax.ShapeDtypeStruct(hs_a.shape, BF), jax.ShapeDtypeStruct(hs_b.shape, BF)],
        input_output_aliases={3: 0, 4: 1},
        compiler_params=_params(("arbitrary",)),
        name="moe_sorted",
    )(eb, nv, cid, hs_a, hs_b, w_gate, w_up, w_down)


def _combine_kernel(ys_ref, slots_ref, x1_ref, mod_ref, lng_ref, lnb_ref, o_ref):
    g2 = mod_ref[:, 5 * D_MODEL:6 * D_MODEL]
    s_lane = _iota((TILE, CAP), 1)
    for u in range(COMBINE_TILES_PER_STEP):
        sl = slots_ref[u]
        pick = _ones_where((s_lane == sl[:, 0:1]) | (s_lane == sl[:, 1:2]))
        ffn = _dot(pick, ys_ref[u * CAP:(u + 1) * CAP, 0:D_MODEL])
        o_ref[u] = _layer_norm(ALPHA * x1_ref[u] + g2 * ffn, lng_ref[...], lnb_ref[...])


def _combine(ys, slots, x1, mod, layer, row_fn, lng, lnb):
    S, T, _ = x1.shape
    n = T // TILE
    U = COMBINE_TILES_PER_STEP
    assert (S * n) % U == 0 and (n % U == 0 or n == 1)

    def tok(w):
        return pl.BlockSpec((U, TILE, w), lambda t: (t, 0, 0))

    out = pl.pallas_call(
        _combine_kernel,
        grid=(S * n // U,),
        in_specs=[
            pl.BlockSpec((U * CAP, HS_W), lambda t: (t, 0)),
            tok(LANES), tok(D_MODEL),
            pl.BlockSpec((None, None, 1, 6 * D_MODEL), lambda t: (layer, row_fn(t * U // n), 0, 0)),
            pl.BlockSpec((1, D_MODEL), lambda t: (0, 0)),
            pl.BlockSpec((1, D_MODEL), lambda t: (0, 0)),
        ],
        out_specs=tok(D_MODEL),
        out_shape=jax.ShapeDtypeStruct((S * n, TILE, D_MODEL), F32),
        compiler_params=_params(("parallel",)),
        name="moe_combine",
    )(ys, slots.reshape(S * n, TILE, LANES), x1.reshape(S * n, TILE, D_MODEL), mod, lng, lnb)
    return out.reshape(S, T, D_MODEL)


def _rope_tables(n_lat):
    half = DIFF_QK // 2
    pairs = half // 2
    inv = (1.0 / (ROPE_BASE ** (np.arange(pairs, dtype=np.float32) * 2.0 / half))).astype(np.float32)
    t = np.arange(n_lat)
    ang_r = ((t // GRID_W).astype(np.float32)[:, None] * inv[None, :]).astype(np.float64)
    ang_c = ((t % GRID_W).astype(np.float32)[:, None] * inv[None, :]).astype(np.float64)
    cos = np.concatenate([np.cos(ang_r)] * 2 + [np.cos(ang_c)] * 2, axis=1)
    sin = np.concatenate([-np.sin(ang_r), np.sin(ang_r), -np.sin(ang_c), np.sin(ang_c)], axis=1)
    reps = DIFF_W // DIFF_QK
    return (jnp.asarray(np.tile(cos, (1, reps)), F32), jnp.asarray(np.tile(sin, (1, reps)), F32))


def _block_diag(s):
    S = s.shape[0]
    eye = jnp.eye(H_RET, dtype=s.dtype)
    return jnp.einsum('shdv,hg->shdgv', s, eye).reshape(S, RET_W, RET_W)


def _router_rows(w_group, b_group, w_expert, b_expert):
    pad = EXPERT_ROW0 - N_GROUPS
    tail = ROUTER_ROWS - EXPERT_ROW0 - N_EXPERTS
    w = jnp.concatenate([w_group.T, jnp.zeros((pad, D_MODEL), F32),
                         w_expert.reshape(D_MODEL, N_EXPERTS).T, jnp.zeros((tail, D_MODEL), F32)], axis=0)
    bias = jnp.concatenate([b_group, jnp.zeros((pad,), F32), b_expert.reshape(N_EXPERTS), jnp.zeros((tail,), F32)])
    hi = w.astype(BF)
    return hi, (w - hi.astype(F32)).astype(BF), bias.reshape(ROUTER_ROWS, 1)


def kernel(x_prompt, x_sample, cache_diff_k, cache_diff_v, state_ret_fwd, state_ret_bwd, c, c_ctx, w_mod, b_mod, w_in, ret_decay_logit, diff_lambda, diff_subln_g, conv_w, conv_b, w_out, ln_g, ln_b, w_router_group, b_router_group, w_router_expert, b_router_expert, w_gate, w_up, w_down):
    B, T_ctx, _ = x_prompt.shape
    Bd, T_lat, _ = x_sample.shape
    assert T_ctx == TILE and T_lat % TILE == 0 and T_lat % GRID_W == 0
    assert 1 + Bd <= MOD_ROWS

    cond = jnp.concatenate([c_ctx[None, :], c, jnp.zeros((MOD_ROWS - 1 - Bd, D_MODEL), F32)], axis=0)
    mod = _modulation(cond, w_mod, b_mod).reshape(DEPTH, MOD_ROWS, 1, 6 * D_MODEL)
    rope_tabs = _rope_tables(T_lat)
    w_in_b = w_in.astype(BF)
    w_out_b = w_out.astype(BF)

    ctx_row = lambda s: 0
    lat_row = lambda s: s + 1
    ctx_tiles = B * (T_ctx // TILE)
    lat_tiles = Bd * (T_lat // TILE)

    yp, ys = x_prompt, x_sample
    stacked = None
    for l in range(DEPTH):
        lam_init = 0.8 - 0.6 * math.exp(-0.3 * l)
        lg_lanes = jnp.repeat(ret_decay_logit[l], RET_DK, axis=1)
        wrh, wrl, br = _router_rows(w_router_group[l], b_router_group[l], w_router_expert[l], b_router_expert[l])
        shared = (w_out_b, lg_lanes, diff_lambda[l], diff_subln_g[l].reshape(1, DIFF_V), conv_w[l],
                  conv_b[l].reshape(1, CONV_CH), ln_g[l, 0].reshape(1, D_MODEL), ln_b[l, 0].reshape(1, D_MODEL),
                  wrh, wrl, br, lam_init)
        ln2 = (ln_g[l, 1].reshape(1, D_MODEL), ln_b[l, 1].reshape(1, D_MODEL))

        pr = _proj(yp, mod, l, ctx_row, w_in_b, lg_lanes, stacked=stacked)
        rq, rk, rv, rg, dq, dk, dv, cb, cch = pr[:9]
        stacked = pr[9:]
        x1_c, hs_c, slots_c, cnt_c = _mixer(yp, mod, l, ctx_row, (rq, rk, rv, rg, dq, cb, cch), None, None,
                                            (dk, dv), *shared)

        pr = _proj(ys, mod, l, lat_row, w_in_b, lg_lanes, rope_tabs=rope_tabs)
        rq, rk, rv, rg, dq, dk, dv, cb, cch, kvf, kvb = pr
        states = _scan(kvf, kvb, _block_diag(state_ret_fwd[:, l]), _block_diag(state_ret_bwd[:, l]), lg_lanes)
        cache_v = cache_diff_v[:, l].astype(BF)
        cache_kv = (cache_diff_k[:, l].reshape(Bd, -1, DIFF_W).astype(BF),
                    jnp.concatenate([cache_v, jnp.ones_like(cache_v)], axis=-1).reshape(Bd, -1, V_AUG_W))
        x1_l, hs_l, slots_l, cnt_l = _mixer(ys, mod, l, lat_row, (rq, rk, rv, rg, dq, cb, cch), states, cache_kv,
                                            (dk, dv), *shared)

        padded = jnp.concatenate([cnt_c.reshape(ctx_tiles, N_EXPERTS, LANES)[:, :, 0],
                                  cnt_l.reshape(lat_tiles, N_EXPERTS, LANES)[:, :, 0]], axis=0)
        eb, nv, cid = _block_schedule(padded, ctx_tiles * CHUNKS_PER_TILE)
        out_c, out_l = _moe(hs_c, hs_l, eb, nv, cid, l, w_gate, w_up, w_down)
        yp = _combine(out_c, slots_c, x1_c, mod, l, ctx_row, *ln2)
        ys = _combine(out_l, slots_l, x1_l, mod, l, lat_row, *ln2)

    new_k, new_v, st_f, st_b = stacked
    return (yp, ys, new_k.reshape(B, DEPTH, T_ctx, H_DIFF, 2, DIFF_QK),
            new_v.reshape(B, DEPTH, T_ctx, H_DIFF, DIFF_V), st_f, st_b)
```

```python
import functools
import math

import numpy as np
import jax
import jax.numpy as jnp
from jax import lax
from jax.experimental import pallas as pl
from jax.experimental.pallas import tpu as pltpu

D_MODEL = 1024
DEPTH = 2
GRID_W = 64
H_RET = 4
RET_DK = 64
RET_W = H_RET * RET_DK
H_DIFF = 4
DIFF_QK = 64
DIFF_V = 2 * DIFF_QK
DIFF_W = H_DIFF * DIFF_V
CONV_CH = 256
ROPE_BASE = 10000.0
N_GROUPS = 4
EXPERTS_PER_GROUP = 4
N_EXPERTS = N_GROUPS * EXPERTS_PER_GROUP
D_EXPERT = 512
ALPHA = (2 * DEPTH) ** 0.25
EPS = 1e-5
MOD_ROWS = 8
LANES = 128
ROUTER_ROWS = 32
EXPERT_ROW0 = 8

TILE = 256
CHUNK = 16
CAP = 2 * TILE + N_EXPERTS * CHUNK
CHUNKS_PER_TILE = CAP // CHUNK
COMBINE_TILES_PER_STEP = 4
PROJ_TILES = 4
MOE_TMB = 1024
CPB = MOE_TMB // CHUNK
HS_W = D_MODEL + LANES
V_AUG_W = 2 * DIFF_W
QK_SCALE_LOG2 = (DIFF_QK ** -0.5) * math.log2(math.e)
SLOT_RADIX = 32
RET_DK_BITS = RET_DK.bit_length() - 1
GROUP_BITS = EXPERTS_PER_GROUP.bit_length() - 1
CHUNK_BITS = CHUNK.bit_length() - 1
SLOT_BITS = SLOT_RADIX.bit_length() - 1
assert (1 << RET_DK_BITS, 1 << GROUP_BITS, 1 << CHUNK_BITS, 1 << SLOT_BITS) == (
    RET_DK, EXPERTS_PER_GROUP, CHUNK, SLOT_RADIX)

_O_RQ, _O_RK, _O_RV, _O_RG = 0, 256, 512, 768
_O_DQ, _O_DK, _O_DV = 1024, 1536, 2048
_O_CB, _O_CC, _O_CH = 2560, 2816, 3072
IN_WIDTH = 3328

BF = jnp.bfloat16
F32 = jnp.float32
I32 = jnp.int32

_VMEM_LIMIT = 60 * 1024 * 1024


def _dot(a, b):
    return jnp.dot(a, b, preferred_element_type=F32)


def _dot_nt(a, b):
    return lax.dot_general(a, b, (((1,), (1,)), ((), ())), preferred_element_type=F32)


def _dot_tn(a, b):
    return lax.dot_general(a, b, (((0,), (0,)), ((), ())), preferred_element_type=F32)


def _split(x):
    hi = x.astype(BF)
    lo = (x - hi.astype(F32)).astype(BF)
    return hi, lo


def _dot_hl(x, w_bf16):
    hi, lo = _split(x)
    return _dot(hi, w_bf16) + _dot(lo, w_bf16)


def _dot3(x, w_hi, w_lo):
    hi, lo = _split(x)
    return _dot(hi, w_hi) + (_dot(lo, w_hi) + _dot(hi, w_lo))


def _iota(shape, dim):
    return lax.broadcasted_iota(I32, shape, dim)


def _ones_where(cond):
    return jnp.where(cond, 1.0, 0.0).astype(BF)


def _log_sigmoid(x):
    return jnp.minimum(x, 0.0) - jnp.log1p(jnp.exp(-jnp.abs(x)))


def _silu(x):
    return x * jax.nn.sigmoid(x)


def _layer_norm(y, g, b):
    mu = jnp.mean(y, axis=-1, keepdims=True)
    yc = y - mu
    var = jnp.mean(yc * yc, axis=-1, keepdims=True)
    return yc * lax.rsqrt(var + EPS) * g + b


def _params(sem):
    return pltpu.CompilerParams(dimension_semantics=sem, vmem_limit_bytes=_VMEM_LIMIT)


_HBM = pl.BlockSpec(memory_space=pltpu.HBM)


def _mod_kernel(c_ref, w_ref, b_ref, o_ref):
    a = _silu(c_ref[...])
    w_hi, w_lo = _split(w_ref[...])
    o_ref[...] = _dot3(a, w_hi, w_lo) + b_ref[...]


def _modulation(cond, w_mod, b_mod):
    tn = 1536
    n6 = 6 * D_MODEL
    return pl.pallas_call(
        _mod_kernel,
        grid=(DEPTH, n6 // tn),
        in_specs=[
            pl.BlockSpec((MOD_ROWS, D_MODEL), lambda l, j: (0, 0)),
            pl.BlockSpec((None, D_MODEL, tn), lambda l, j: (l, 0, j)),
            pl.BlockSpec((None, 1, tn), lambda l, j: (l, 0, j)),
        ],
        out_specs=pl.BlockSpec((None, MOD_ROWS, tn), lambda l, j: (l, 0, j)),
        out_shape=jax.ShapeDtypeStruct((DEPTH, MOD_ROWS, n6), F32),
        compiler_params=_params(("parallel", "parallel")),
        name="modulation",
    )(cond, w_mod, b_mod.reshape(DEPTH, 1, n6))


def _augment_v(v):
    ones = jnp.ones((v.shape[0], DIFF_V), v.dtype)
    parts = []
    for hd in range(H_DIFF):
        parts += [v[:, hd * DIFF_V:(hd + 1) * DIFF_V], ones]
    return jnp.concatenate(parts, axis=1)


def _swap16(x):
    n = x.shape[-1]
    lane = _iota(x.shape, 1)
    return jnp.where((lane & 16) == 0, pltpu.roll(x, n - 16, axis=1), pltpu.roll(x, 16, axis=1))


def _proj_kernel(*refs, latent, n_aliased, layer):
    x_ref, mod_ref, w_ref, lg_ref = refs[:4]
    pos = 4
    if latent:
        cos_ref, sin_ref = refs[pos:pos + 2]
        pos += 2
    pos += n_aliased
    rq_ref, rk_ref, rv_ref, rg_ref, dq_ref, dk_ref, dv_ref, cb_ref, cch_ref = refs[pos:pos + 9]
    pos += 9
    if latent:
        kvf_ref, kvb_ref = refs[pos:pos + 2]
    else:
        dk32_ref, dv32_ref, stf_ref, stb_ref = refs[pos:pos + 4]
    rows = PROJ_TILES * TILE

    def put(ref, u, idx, val):
        if n_aliased:
            ref[(u,) + idx] = val
        else:
            for other in range(DEPTH):
                ref[(u, other) + idx] = val if other == layer else jnp.zeros_like(val)

    def store(ref, val):
        ref[...] = val.reshape(PROJ_TILES, TILE, val.shape[-1])

    m = mod_ref[...]
    sh1 = m[:, 0:D_MODEL]
    sc1 = m[:, D_MODEL:2 * D_MODEL]
    h = (x_ref[...].reshape(rows, D_MODEL) * (1.0 + sc1) + sh1).astype(BF)

    def col(off, width):
        return _dot(h, w_ref[:, off:off + width])

    store(rq_ref, col(_O_RQ, RET_W).astype(BF))
    rk = col(_O_RK, RET_W) * (RET_DK ** -0.5)
    store(rk_ref, rk.astype(BF))
    rv = col(_O_RV, RET_W).astype(BF)
    store(rv_ref, rv)
    store(rg_ref, col(_O_RG, RET_W))

    lg = _log_sigmoid(lg_ref[...])
    p = (_iota((rows, 1), 0) & (TILE - 1)).astype(F32)
    kf = (rk * jnp.exp((TILE - 1.0 - p) * lg[0:1])).astype(BF)
    kb = (rk * jnp.exp(p * lg[1:2])).astype(BF)
    for u in range(PROJ_TILES):
        tile = slice(u * TILE, (u + 1) * TILE)
        kvf = _dot_tn(kf[tile], rv[tile])
        kvb = _dot_tn(kb[tile], rv[tile])
        if latent:
            kvf_ref[u] = kvf
            kvb_ref[u] = kvb
        else:
            for hd in range(H_RET):
                lo = hd * RET_DK
                put(stf_ref, u, (hd,), kvf[lo:lo + RET_DK, lo:lo + RET_DK])
                put(stb_ref, u, (hd,), kvb[lo:lo + RET_DK, lo:lo + RET_DK])

    dq = col(_O_DQ, DIFF_W)
    dk = col(_O_DK, DIFF_W)
    dv = col(_O_DV, DIFF_W)
    if latent:
        cos = cos_ref[...].reshape(rows, DIFF_W)
        sin = sin_ref[...].reshape(rows, DIFF_W)
        dq = dq * cos + _swap16(dq) * sin
        dk = dk * cos + _swap16(dk) * sin
    else:
        for u in range(PROJ_TILES):
            put(dk32_ref, u, (), dk[u * TILE:(u + 1) * TILE])
            put(dv32_ref, u, (), dv[u * TILE:(u + 1) * TILE])
    store(dq_ref, (dq * QK_SCALE_LOG2).astype(BF))
    store(dk_ref, dk.astype(BF))
    store(dv_ref, _augment_v(dv.astype(BF)))

    store(cb_ref, col(_O_CB, CONV_CH))
    store(cch_ref, col(_O_CC, CONV_CH) * col(_O_CH, CONV_CH))


def _proj(x, mod, layer, row_fn, w_in_b, lg_lanes, rope_tabs=None, stacked=None):
    S, T, _ = x.shape
    n = T // TILE
    G = PROJ_TILES
    tiles = S * n
    latent = rope_tabs is not None
    assert tiles % G == 0 and (n % G == 0 or n == 1)

    def tok(w):
        return pl.BlockSpec((G, TILE, w), lambda t: (t, 0, 0))

    def tok_shape(w, dt):
        return jax.ShapeDtypeStruct((tiles, TILE, w), dt)

    in_specs = [
        tok(D_MODEL),
        pl.BlockSpec((None, None, 1, 6 * D_MODEL), lambda t: (layer, row_fn(t * G // n), 0, 0)),
        pl.BlockSpec((None, D_MODEL, IN_WIDTH), lambda t: (layer, 0, 0)),
        pl.BlockSpec((2, RET_W), lambda t: (0, 0)),
    ]
    args = [x.reshape(tiles, TILE, D_MODEL), mod, w_in_b, lg_lanes]
    out_specs = [tok(RET_W)] * 4 + [tok(DIFF_W)] * 2 + [tok(V_AUG_W)] + [tok(CONV_CH)] * 2
    out_shape = ([tok_shape(RET_W, BF)] * 3 + [tok_shape(RET_W, F32)] + [tok_shape(DIFF_W, BF)] * 2
                 + [tok_shape(V_AUG_W, BF)]
                 + [tok_shape(CONV_CH, F32)] * 2)
    aliases = {}
    if latent:
        in_specs += [pl.BlockSpec((G, TILE, DIFF_W), lambda t: (t % (n // G), 0, 0))] * 2
        args += [tab.reshape(n, TILE, DIFF_W) for tab in rope_tabs]
        out_specs += [pl.BlockSpec((G, RET_W, RET_W), lambda t: (t, 0, 0))] * 2
        out_shape += [jax.ShapeDtypeStruct((tiles, RET_W, RET_W), F32)] * 2
    else:
        assert n == 1
        kv_shape = jax.ShapeDtypeStruct((S, DEPTH, T, DIFF_W), F32)
        st_shape = jax.ShapeDtypeStruct((S, DEPTH, H_RET, RET_DK, RET_DK), F32)
        if stacked is not None:
            in_specs += [_HBM] * 4
            args += list(stacked)
            aliases = {len(args) - 4 + j: len(out_shape) + j for j in range(4)}
            out_specs += [pl.BlockSpec((G, None, TILE, DIFF_W), lambda t: (t, layer, 0, 0))] * 2
            out_specs += [pl.BlockSpec((G, None, H_RET, RET_DK, RET_DK), lambda t: (t, layer, 0, 0, 0))] * 2
        else:
            out_specs += [pl.BlockSpec((G, DEPTH, TILE, DIFF_W), lambda t: (t, 0, 0, 0))] * 2
            out_specs += [pl.BlockSpec((G, DEPTH, H_RET, RET_DK, RET_DK), lambda t: (t, 0, 0, 0, 0))] * 2
        out_shape += [kv_shape, kv_shape, st_shape, st_shape]
    outs = pl.pallas_call(
        functools.partial(_proj_kernel, latent=latent, n_aliased=len(aliases), layer=layer),
        grid=(tiles // G,),
        in_specs=in_specs,
        out_specs=out_specs,
        out_shape=out_shape,
        input_output_aliases=aliases,
        compiler_params=_params(("parallel",)),
        name="proj_lat" if latent else "proj_ctx",
    )(*args)
    token_outs = [o.reshape(S, T, o.shape[-1]) for o in outs[:9]]
    if latent:
        return token_outs + [o.reshape(S, n, RET_W, RET_W) for o in outs[9:]]
    return token_outs + list(outs[9:])


def _scan_kernel(kvf_ref, kvb_ref, s0f_ref, s0b_ref, lg_ref, sf_ref, sb_ref, *, n):
    lg = _log_sigmoid(lg_ref[...])
    dec = jnp.exp(float(TILE) * lg)
    same_head = (_iota((RET_W, RET_W), 0) >> RET_DK_BITS) == (_iota((RET_W, RET_W), 1) >> RET_DK_BITS)

    s = jnp.where(same_head, s0f_ref[...], 0.0)
    for c in range(n):
        sf_ref[c] = s
        s = s * dec[0:1] + jnp.where(same_head, kvf_ref[c], 0.0)
    s = jnp.where(same_head, s0b_ref[...], 0.0)
    for c in reversed(range(n)):
        sb_ref[c] = s
        s = s * dec[1:2] + jnp.where(same_head, kvb_ref[c], 0.0)


def _scan(kvf, kvb, s0f, s0b, lg_lanes):
    S, n = kvf.shape[:2]
    chunks = pl.BlockSpec((None, n, RET_W, RET_W), lambda s: (s, 0, 0, 0))
    one = pl.BlockSpec((None, RET_W, RET_W), lambda s: (s, 0, 0))
    return pl.pallas_call(
        functools.partial(_scan_kernel, n=n),
        grid=(S,),
        in_specs=[chunks, chunks, one, one, pl.BlockSpec((2, RET_W), lambda s: (0, 0))],
        out_specs=[chunks, chunks],
        out_shape=[jax.ShapeDtypeStruct((S, n, RET_W, RET_W), F32)] * 2,
        compiler_params=_params(("parallel",)),
        name="ret_scan",
    )(kvf, kvb, s0f, s0b, lg_lanes)


def _diff_scores(dq_ref, kv_refs, heads):
    lane = _iota((1, DIFF_V), 1)
    scores = []
    for hd in heads:
        lo = hd * DIFF_V
        q_h = dq_ref[:, lo:lo + DIFF_V]
        qs = jnp.concatenate([jnp.where(lane < DIFF_QK, q_h, 0), jnp.where(lane >= DIFF_QK, q_h, 0)], axis=0)
        scores.append(jnp.concatenate([_dot_nt(qs, k[:, lo:lo + DIFF_V]) for k, _ in kv_refs], axis=1))
    return jnp.concatenate(scores, axis=0)


def _diff_outputs(s, kv_refs, heads, lam):
    p = jnp.exp2(s - jnp.max(s, axis=-1, keepdims=True))

    def times_v(w, hd, width):
        acc = None
        start = 0
        for k, v in kv_refs:
            part = _dot(w[:, start:start + k.shape[0]], v[:, hd * 2 * DIFF_V:hd * 2 * DIFF_V + width])
            acc = part if acc is None else acc + part
            start += k.shape[0]
        return acc

    outs = []
    p = p.astype(BF)
    for n, hd in enumerate(heads):
        acc = times_v(p[n * 2 * TILE:(n + 1) * 2 * TILE], hd, 2 * DIFF_V)
        o0 = acc[0:TILE, 0:DIFF_V] * (1.0 / acc[0:TILE, DIFF_V:DIFF_V + 1])
        o1 = acc[TILE:, 0:DIFF_V] * (lam / acc[TILE:, DIFF_V:DIFF_V + 1])
        outs.append(o0 - o1)
    return outs


def _diff_head_by_map(dq_ref, kv_refs, hd, lam):
    lane = _iota((1, DIFF_V), 1)
    lo = hd * DIFF_V
    q_h = dq_ref[:, lo:lo + DIFF_V]
    outs = []
    for first, scale in ((True, 1.0), (False, lam)):
        qm = jnp.where((lane < DIFF_QK) == first, q_h, 0)
        s = jnp.concatenate([_dot_nt(qm, k[:, lo:lo + DIFF_V]) for k, _ in kv_refs], axis=1)
        p = jnp.exp2(s - jnp.max(s, axis=-1, keepdims=True)).astype(BF)
        acc = None
        start = 0
        for k, v in kv_refs:
            part = _dot(p[:, start:start + k.shape[0]], v[:, hd * 2 * DIFF_V:(hd + 1) * 2 * DIFF_V])
            acc = part if acc is None else acc + part
            start += k.shape[0]
        outs.append(acc[:, 0:DIFF_V] * (scale / acc[:, DIFF_V:DIFF_V + 1]))
    return outs[0] - outs[1]


def _mixer_kernel(*refs, n_tiles, n_total, lam_init, cached):
    n_a = _n_token_mix_refs(cached)
    mix_even, mix_odd, decay_ref = refs[-3:]
    t = pl.program_id(0)
    tile_in_seq = lax.rem(jnp.minimum(t, n_total - 1), n_tiles)

    @pl.when(t == 0)
    def _():
        mix_odd[...] = jnp.zeros_like(mix_odd)
        lg = _log_sigmoid(refs[n_a - 5][...])
        dist = (_iota((TILE, TILE), 0) - _iota((TILE, TILE), 1)).astype(F32)
        diag2 = jnp.where(dist == 0.0, 2.0, 1.0)
        for hd in range(H_RET):
            lgf = lg[0:1, hd * RET_DK:hd * RET_DK + 1]
            lgb = lg[1:2, hd * RET_DK:hd * RET_DK + 1]
            decay_ref[hd] = jnp.exp(jnp.abs(dist) * jnp.where(dist > 0.0, lgf, lgb)) * diag2

    def step(mix_write, mix_read):
        mix_write[...] = _token_mix(*refs[:n_a], decay_ref, tile_in_seq=tile_in_seq, n_tiles=n_tiles,
                                    lam_init=lam_init, cached=cached)
        _project_route(mix_read[...], *refs[n_a:-3])

    @pl.when(lax.rem(t, 2) == 0)
    def _():
        step(mix_even, mix_odd)

    @pl.when(lax.rem(t, 2) == 1)
    def _():
        step(mix_odd, mix_even)


def _n_token_mix_refs(cached):
    return 18 + (2 if cached else 0)


def _token_mix(*refs, tile_in_seq, n_tiles, lam_init, cached):
    rq_ref, rk_ref, rv_ref, rg_ref, sf_ref, sb_ref, dq_ref = refs[:7]
    pos = 7
    kv_refs = []
    if cached:
        kv_refs.append((refs[pos], refs[pos + 1]))
        pos += 2
    kv_refs.append((refs[pos], refs[pos + 1]))
    pos += 2
    (cb_ref, cch_ref, cprev_ref, cnext_ref, lg_ref, lamp_ref, subg_ref, cw_ref, cbias_ref,
     decay_ref) = refs[pos:pos + 10]
    i = tile_in_seq

    lg = _log_sigmoid(lg_ref[...])
    head_of_lane = _iota((1, RET_W), 1) >> RET_DK_BITS
    q = rq_ref[...]
    k = rk_ref[...]
    v = rv_ref[...]
    ret_o = jnp.zeros((TILE, RET_W), F32)
    for hd in range(H_RET):
        in_head = head_of_lane == hd
        sc = _dot_nt(jnp.where(in_head, q, 0), k)
        ret_o = ret_o + jnp.where(in_head, _dot((sc * decay_ref[hd]).astype(BF), v), 0.0)
    p = _iota((TILE, 1), 0).astype(F32)
    ret_o = ret_o + _dot(q, sf_ref[...].astype(BF)) * jnp.exp((p + 1.0) * lg[0:1])
    ret_o = ret_o + _dot(q, sb_ref[...].astype(BF)) * jnp.exp((float(TILE) - p) * lg[1:2])
    avg = jnp.where((_iota((RET_W, RET_W), 0) >> RET_DK_BITS) == (_iota((RET_W, RET_W), 1) >> RET_DK_BITS),
                    1.0 / RET_DK, 0.0).astype(BF)
    rc = ret_o - _dot_hl(ret_o, avg)
    ret = rc * lax.rsqrt(_dot_hl(rc * rc, avg) + EPS) * _silu(rg_ref[...])

    lp = lamp_ref[...]
    lam = (jnp.exp(jnp.sum(lp[0:1] * lp[1:2], axis=-1, keepdims=True))
           - jnp.exp(jnp.sum(lp[2:3] * lp[3:4], axis=-1, keepdims=True)) + lam_init)
    subg = subg_ref[...] * (1.0 - lam_init)
    head_groups = [[hd] for hd in range(H_DIFF)] if cached else [list(range(H_DIFF))]
    heads = []
    for group in head_groups:
        if cached:
            outs = [_diff_head_by_map(dq_ref, kv_refs, hd, lam) for hd in group]
        else:
            outs = _diff_outputs(_diff_scores(dq_ref, kv_refs, group), kv_refs, group, lam)
        for o in outs:
            o = o * lax.rsqrt(jnp.mean(o * o, axis=-1, keepdims=True) + EPS) * subg
            heads.append(o.astype(BF))
    diff = jnp.concatenate(heads, axis=1)

    cch = cch_ref[...]
    prev = jnp.where(i > 0, cprev_ref[7:8, :], 0.0)
    nxt = jnp.where(i < n_tiles - 1, cnext_ref[0:1, :], 0.0)
    r = _iota((TILE, 1), 0)
    up = jnp.where(r == 0, prev, pltpu.roll(cch, 1, axis=0))
    dn = jnp.where(r == TILE - 1, nxt, pltpu.roll(cch, TILE - 1, axis=0))
    cw = cw_ref[...]
    conv = cb_ref[...] * (up * cw[0:1] + cch * cw[1:2] + dn * cw[2:3] + cbias_ref[...])
    return jnp.concatenate([ret.astype(BF), diff, conv.astype(BF)], axis=1)


def _project_route(mixed, x_ref, mod_ref, wout_ref, lng_ref, lnb_ref, wrh_ref, wrl_ref, br_ref,
                   x1_ref, hs_ref, slots_ref, cnt_ref):
    m = mod_ref[...]
    g1 = m[:, 2 * D_MODEL:3 * D_MODEL]
    sh2 = m[:, 3 * D_MODEL:4 * D_MODEL]
    sc2 = m[:, 4 * D_MODEL:5 * D_MODEL]
    x1 = _layer_norm(ALPHA * x_ref[...] + g1 * _dot(mixed, wout_ref[...]), lng_ref[...], lnb_ref[...])
    x1_ref[...] = x1

    h2 = x1 * (1.0 + sc2) + sh2
    h2h, h2l = _split(h2)
    wrh = wrh_ref[...]
    logits = (_dot_nt(wrh, h2h) + (_dot_nt(wrh, h2l) + _dot_nt(wrl_ref[...], h2h))) + br_ref[...]
    neg = -jnp.inf
    g_id = _iota((EXPERT_ROW0, TILE), 0)
    gl = jnp.where(g_id < N_GROUPS, logits[0:EXPERT_ROW0], neg)
    gmax = jnp.max(gl, axis=0, keepdims=True)
    g_idx = jnp.min(jnp.where(gl == gmax, g_id, N_GROUPS), axis=0, keepdims=True)
    g_w = 1.0 / jnp.sum(jnp.exp(gl - gmax), axis=0, keepdims=True)
    e_id = _iota((N_EXPERTS, TILE), 0)
    el = jnp.where((e_id >> GROUP_BITS) == g_idx, logits[EXPERT_ROW0:EXPERT_ROW0 + N_EXPERTS], neg)
    v1 = jnp.max(el, axis=0, keepdims=True)
    i1 = jnp.min(jnp.where(el == v1, e_id, N_EXPERTS), axis=0, keepdims=True)
    el2 = jnp.where(e_id == i1, neg, el)
    v2 = jnp.max(el2, axis=0, keepdims=True)
    i2 = jnp.min(jnp.where(el2 == v2, e_id, N_EXPERTS), axis=0, keepdims=True)
    t = jnp.exp(v2 - v1)
    w1 = g_w / (1.0 + t)
    w2 = g_w * t / (1.0 + t)

    sel1 = e_id == i1
    sel2 = e_id == i2
    sel = jnp.where(sel1 | sel2, 1.0, 0.0)
    earlier = _ones_where(_iota((TILE, TILE), 0) < _iota((TILE, TILE), 1))
    rank = _dot(sel.astype(BF), earlier)
    cnt = jnp.sum(sel, axis=1, keepdims=True).astype(I32)
    padded = ((cnt + (CHUNK - 1)) >> CHUNK_BITS) << CHUNK_BITS
    incl = jnp.broadcast_to(padded, (N_EXPERTS, LANES))
    e_row = _iota((N_EXPERTS, LANES), 0)
    for step in (1, 2, 4, 8):
        incl = incl + jnp.where(e_row >= step, pltpu.roll(incl, step, axis=0), 0)
    seg_off = (incl[:, 0:1] - padded).astype(F32)
    spos = seg_off + rank
    slot1 = jnp.sum(jnp.where(sel1, spos, 0.0), axis=0, keepdims=True).astype(I32)
    slot2 = jnp.sum(jnp.where(sel2, spos, 0.0), axis=0, keepdims=True).astype(I32)
    s_id = _iota((CAP, TILE), 0)
    at1 = s_id == slot1
    at2 = s_id == slot2
    hs_ref[:, 0:D_MODEL] = _dot(_ones_where(at1 | at2), h2h).astype(BF)
    w1h = w1.astype(BF).astype(F32)
    w2h = w2.astype(BF).astype(F32)
    ones = jnp.ones((TILE, LANES), BF)
    g_hi = _dot((jnp.where(at1, w1h, 0.0) + jnp.where(at2, w2h, 0.0)).astype(BF), ones)
    g_lo = _dot((jnp.where(at1, w1 - w1h, 0.0) + jnp.where(at2, w2 - w2h, 0.0)).astype(BF), ones)
    lane = _iota((CAP, LANES), 1)
    hs_ref[:, D_MODEL:] = jnp.where(lane == 0, g_hi, jnp.where(lane == 1, g_lo, 0.0)).astype(BF)
    cnt_ref[...] = jnp.broadcast_to(padded, (N_EXPERTS, LANES))

    dr = _iota((LANES, TILE), 0)
    digits = jnp.where(dr == 0, slot1 & (SLOT_RADIX - 1), jnp.where(dr == 1, slot1 >> SLOT_BITS,
             jnp.where(dr == 2, slot2 & (SLOT_RADIX - 1), jnp.where(dr == 3, slot2 >> SLOT_BITS, 0))))
    eye = _ones_where(_iota((TILE, TILE), 0) == _iota((TILE, TILE), 1))
    cols = _dot_nt(eye, digits.astype(F32).astype(BF))
    s1c = (cols[:, 0:1] + SLOT_RADIX * cols[:, 1:2]).astype(I32)
    s2c = (cols[:, 2:3] + SLOT_RADIX * cols[:, 3:4]).astype(I32)
    tl = _iota((TILE, LANES), 1)
    slots_ref[...] = jnp.where(tl == 0, s1c, jnp.where(tl == 1, s2c, 0))


def _mixer(x, mod, layer, row_fn, pr, states, cache_kv, new_kv, w_out_b, lg_lanes, lamp, subg, cw, cbias,
           lng, lnb, wrh, wrl, br, lam_init):
    S, T, _ = x.shape
    n = T // TILE
    rq, rk, rv, rg, dq, cb, cch = pr
    t8 = TILE // 8
    cached = cache_kv is not None
    last = S * n - 1

    def at_mix(t):
        return jnp.minimum(t, last) // n, jnp.minimum(t, last) % n

    def at_out(t):
        return jnp.maximum(t - 1, 0) // n, jnp.maximum(t - 1, 0) % n

    def tok(w, at):
        return pl.BlockSpec((None, TILE, w), lambda t: (*at(t), 0))

    def full(shape):
        return pl.BlockSpec(shape, lambda t: (0,) * len(shape))

    def seq(a):
        return pl.BlockSpec((None,) + a.shape[1:], lambda t: (at_mix(t)[0], 0, 0))

    def halo(offset):
        def index(t):
            s, i = at_mix(t)
            return (s, jnp.clip(i * t8 + offset, 0, T // 8 - 1), 0)
        return pl.BlockSpec((None, 8, CONV_CH), index)

    if states is None:
        zero_state = jnp.zeros((1, 1, RET_W, RET_W), F32)
        states = (zero_state, zero_state)
        state = pl.BlockSpec((None, None, RET_W, RET_W), lambda t: (0, 0, 0, 0))
    else:
        state = pl.BlockSpec((None, None, RET_W, RET_W), lambda t: (*at_mix(t), 0, 0))

    in_specs = [tok(RET_W, at_mix)] * 4 + [state, state, tok(DIFF_W, at_mix)]
    args = [rq, rk, rv, rg, states[0], states[1], dq]
    for kv in ([cache_kv] if cached else []) + [new_kv]:
        in_specs += [seq(kv[0]), seq(kv[1])]
        args += list(kv)
    in_specs += [tok(CONV_CH, at_mix), tok(CONV_CH, at_mix), halo(-1), halo(t8),
                 full((2, RET_W)), full((4, DIFF_QK)), full((1, DIFF_V)), full((3, CONV_CH)), full((1, CONV_CH))]
    args += [cb, cch, cch, cch, lg_lanes, lamp, subg, cw, cbias]
    assert len(args) == _n_token_mix_refs(cached)
    in_specs += [tok(D_MODEL, at_out),
                 pl.BlockSpec((None, None, 1, 6 * D_MODEL), lambda t: (layer, row_fn(at_out(t)[0]), 0, 0)),
                 pl.BlockSpec((None, D_MODEL, D_MODEL), lambda t: (layer, 0, 0)),
                 full((1, D_MODEL)), full((1, D_MODEL)),
                 full((ROUTER_ROWS, D_MODEL)), full((ROUTER_ROWS, D_MODEL)), full((ROUTER_ROWS, 1))]
    args += [x, mod, w_out_b, lng, lnb, wrh, wrl, br]

    def out_tile(t):
        s, i = at_out(t)
        return s * n + i

    return pl.pallas_call(
        functools.partial(_mixer_kernel, n_tiles=n, n_total=S * n, lam_init=lam_init, cached=cached),
        grid=(S * n + 1,),
        in_specs=in_specs,
        out_specs=[tok(D_MODEL, at_out),
                   pl.BlockSpec((CAP, HS_W), lambda t: (out_tile(t), 0)),
                   tok(LANES, at_out),
                   pl.BlockSpec((None, None, N_EXPERTS, LANES), lambda t: (*at_out(t), 0, 0))],
        out_shape=[jax.ShapeDtypeStruct((S, T, D_MODEL), F32),
                   jax.ShapeDtypeStruct((S * n * CAP, HS_W), BF),
                   jax.ShapeDtypeStruct((S, T, LANES), I32),
                   jax.ShapeDtypeStruct((S, n, N_EXPERTS, LANES), I32)],
        scratch_shapes=[pltpu.VMEM((TILE, D_MODEL), BF), pltpu.VMEM((TILE, D_MODEL), BF),
                        pltpu.VMEM((H_RET, TILE, TILE), F32)],
        compiler_params=_params(("arbitrary",)),
        name="mixer_lat" if cached else "mixer_ctx",
    )(*args)


def _block_schedule(padded, chunks_a):
    NT = padded.shape[0]
    n_chunks = NT * CHUNKS_PER_TILE
    n_blocks = n_chunks // CPB + N_EXPERTS
    c16 = padded // CHUNK
    ends = jnp.cumsum(c16, axis=1)
    before = jnp.cumsum(c16, axis=0) - c16
    per_expert = jnp.sum(c16, axis=0)
    nb = (per_expert + (CPB - 1)) // CPB
    b_end = jnp.cumsum(nb)
    q = jnp.arange(CHUNKS_PER_TILE, dtype=I32)
    e = jnp.arange(N_EXPERTS, dtype=I32)
    key = jnp.sum((ends[:, None, :] <= q[None, :, None]).astype(I32), axis=-1)
    pos_e = ((b_end - nb) * CPB + before)[:, None, :] + (q[None, :, None] - (ends - c16)[:, None, :])
    pos = jnp.sum(jnp.where(key[..., None] == e, pos_e, 0), axis=-1)
    pos = jnp.where(key < N_EXPERTS, pos, -1).reshape(-1)
    match = pos[None, :] == jnp.arange(n_blocks * CPB, dtype=I32)[:, None]
    cid = jnp.sum(jnp.where(match, jnp.arange(n_chunks, dtype=I32)[None, :], 0), axis=-1)
    nv = jnp.sum(match.astype(I32).reshape(n_blocks, -1), axis=-1)
    first = match & (jnp.arange(n_chunks, dtype=I32) < chunks_a)[None, :]
    nv_a = jnp.sum(first.astype(I32).reshape(n_blocks, -1), axis=-1)
    b = jnp.arange(n_blocks, dtype=I32)
    eb = jnp.minimum(jnp.sum((b_end[None, :] <= b[:, None]).astype(I32), axis=-1), N_EXPERTS - 1)
    zero = jnp.zeros((1,), I32)
    counts = jnp.concatenate([nv, zero, nv_a, zero]).astype(I32)
    return eb.astype(I32), counts, jnp.concatenate([cid, jnp.zeros((CPB,), I32)]).astype(I32)


def _moe_kernel(eb_ref, nv_ref, cid_ref, hs_a, hs_b, wg_ref, wu_ref, wd_ref, ys_a, ys_b,
                xbuf, ybuf, wg_b, wu_b, wd_b, in_sem, out_sem, *, n_blocks, chunks_a):
    b = pl.program_id(0)

    def chunk_rows(c):
        return pl.ds(pl.multiple_of(c * CHUNK, CHUNK), CHUNK)

    def gather_copy(which, c, j, slot):
        src = (hs_a, hs_b)[which]
        return pltpu.make_async_copy(src.at[chunk_rows(c)], xbuf.at[slot, chunk_rows(j)], in_sem.at[slot])

    def scatter_copy(which, c, j, slot):
        dst = (ys_a, ys_b)[which]
        return pltpu.make_async_copy(ybuf.at[slot, chunk_rows(j)],
                                     dst.at[chunk_rows(c), pl.ds(0, D_MODEL)], out_sem.at[slot])

    def for_chunks(blk, fn):
        n_a = nv_ref[n_blocks + 1 + blk]

        def body_a(j, carry):
            fn(j, 0, cid_ref[blk * CPB + j])
            return carry

        def body_b(j, carry):
            fn(j, 1, cid_ref[blk * CPB + j] - chunks_a)
            return carry
        lax.fori_loop(0, n_a, body_a, 0)
        lax.fori_loop(n_a, nv_ref[blk], body_b, 0)

    def each_chunk(blk, fn):
        n_a = nv_ref[n_blocks + 1 + blk]
        n_v = nv_ref[blk]
        for j in range(CPB):
            c = cid_ref[blk * CPB + j]

            @pl.when(j < n_a)
            def _():
                fn(j, 0, c)

            @pl.when((j >= n_a) & (j < n_v))
            def _():
                fn(j, 1, c - chunks_a)

    def wait_gathers(blk, slot):
        rows = pl.ds(0, nv_ref[blk] * CHUNK)

        @pl.when(nv_ref[blk] > 0)
        def _():
            pltpu.make_async_copy(hs_a.at[rows], xbuf.at[slot, rows], in_sem.at[slot]).wait()

    def wait_scatters(blk, slot):
        rows = pl.ds(0, nv_ref[blk] * CHUNK)

        @pl.when(nv_ref[blk] > 0)
        def _():
            pltpu.make_async_copy(ybuf.at[slot, rows], ys_a.at[rows, pl.ds(0, D_MODEL)], out_sem.at[slot]).wait()

    slot = lax.rem(b, 2)
    other = 1 - slot

    @pl.when(b == 0)
    def _():
        xbuf[...] = jnp.zeros_like(xbuf)
        for_chunks(0, lambda j, w, c: gather_copy(w, c, j, 0).start())

    @pl.when((b == 0) | (eb_ref[b] != eb_ref[jnp.maximum(b - 1, 0)]))
    def _():
        wg_b[...] = wg_ref[...].astype(BF)
        wu_b[...] = wu_ref[...].astype(BF)
        wd_b[...] = wd_ref[...].astype(BF)

    wait_gathers(b, slot)

    @pl.when(b >= 2)
    def _():
        wait_scatters(b - 2, slot)

    @pl.when(nv_ref[b] > 0)
    def _():
        each_chunk(b + 1, lambda j, w, c: gather_copy(w, c, j, other).start())
        x = xbuf[slot, :, 0:D_MODEL]
        gate = (xbuf[slot, :, D_MODEL:D_MODEL + 1].astype(F32)
                + xbuf[slot, :, D_MODEL + 1:D_MODEL + 2].astype(F32))
        hg = _dot(x, wg_b[...])
        hu = _dot(x, wu_b[...])
        act = _silu(hg) * hu * gate
        ybuf[slot] = _dot(act.astype(BF), wd_b[...]).astype(BF)
        each_chunk(b, lambda j, w, c: scatter_copy(w, c, j, slot).start())

    @pl.when(b == n_blocks - 1)
    def _():
        if n_blocks >= 2:
            wait_scatters(b - 1, other)
        wait_scatters(b, slot)


def _moe(hs_a, hs_b, eb, nv, cid, layer, w_gate, w_up, w_down):
    n_blocks = eb.shape[0]

    def w_spec(a, c):
        return pl.BlockSpec((None, None, a, c), lambda b, eb, nv, cid: (layer, eb[b], 0, 0))

    grid_spec = pltpu.PrefetchScalarGridSpec(
        num_scalar_prefetch=3,
        grid=(n_blocks,),
        in_specs=[_HBM, _HBM, w_spec(D_MODEL, D_EXPERT), w_spec(D_MODEL, D_EXPERT), w_spec(D_EXPERT, D_MODEL)],
        out_specs=[_HBM, _HBM],
        scratch_shapes=[pltpu.VMEM((2, MOE_TMB, HS_W), BF), pltpu.VMEM((2, MOE_TMB, D_MODEL), BF),
                        pltpu.VMEM((D_MODEL, D_EXPERT), BF), pltpu.VMEM((D_MODEL, D_EXPERT), BF),
                        pltpu.VMEM((D_EXPERT, D_MODEL), BF),
                        pltpu.SemaphoreType.DMA((2,)), pltpu.SemaphoreType.DMA((2,))],
    )
    return pl.pallas_call(
        functools.partial(_moe_kernel, n_blocks=n_blocks, chunks_a=hs_a.shape[0] // CHUNK),
        grid_spec=grid_spec,
        out_shape=[jax.ShapeDtypeStruct(hs_a.shape, BF), jax.ShapeDtypeStruct(hs_b.shape, BF)],
        input_output_aliases={3: 0, 4: 1},
        compiler_params=_params(("arbitrary",)),
        name="moe_sorted",
    )(eb, nv, cid, hs_a, hs_b, w_gate, w_up, w_down)


def _combine_kernel(ys_ref, slots_ref, x1_ref, mod_ref, lng_ref, lnb_ref, o_ref):
    g2 = mod_ref[:, 5 * D_MODEL:6 * D_MODEL]
    s_lane = _iota((TILE, CAP), 1)
    for u in range(COMBINE_TILES_PER_STEP):
        sl = slots_ref[u]
        pick = _ones_where((s_lane == sl[:, 0:1]) | (s_lane == sl[:, 1:2]))
        ffn = _dot(pick, ys_ref[u * CAP:(u + 1) * CAP, 0:D_MODEL])
        o_ref[u] = _layer_norm(ALPHA * x1_ref[u] + g2 * ffn, lng_ref[...], lnb_ref[...])


def _combine(ys, slots, x1, mod, layer, row_fn, lng, lnb):
    S, T, _ = x1.shape
    n = T // TILE
    U = COMBINE_TILES_PER_STEP
    assert (S * n) % U == 0 and (n % U == 0 or n == 1)

    def tok(w):
        return pl.BlockSpec((U, TILE, w), lambda t: (t, 0, 0))

    out = pl.pallas_call(
        _combine_kernel,
        grid=(S * n // U,),
        in_specs=[
            pl.BlockSpec((U * CAP, HS_W), lambda t: (t, 0)),
            tok(LANES), tok(D_MODEL),
            pl.BlockSpec((None, None, 1, 6 * D_MODEL), lambda t: (layer, row_fn(t * U // n), 0, 0)),
            pl.BlockSpec((1, D_MODEL), lambda t: (0, 0)),
            pl.BlockSpec((1, D_MODEL), lambda t: (0, 0)),
        ],
        out_specs=tok(D_MODEL),
        out_shape=jax.ShapeDtypeStruct((S * n, TILE, D_MODEL), F32),
        compiler_params=_params(("parallel",)),
        name="moe_combine",
    )(ys, slots.reshape(S * n, TILE, LANES), x1.reshape(S * n, TILE, D_MODEL), mod, lng, lnb)
    return out.reshape(S, T, D_MODEL)


def _rope_tables(n_lat):
    half = DIFF_QK // 2
    pairs = half // 2
    inv = (1.0 / (ROPE_BASE ** (np.arange(pairs, dtype=np.float32) * 2.0 / half))).astype(np.float32)
    t = np.arange(n_lat)
    ang_r = ((t // GRID_W).astype(np.float32)[:, None] * inv[None, :]).astype(np.float64)
    ang_c = ((t % GRID_W).astype(np.float32)[:, None] * inv[None, :]).astype(np.float64)
    cos = np.concatenate([np.cos(ang_r)] * 2 + [np.cos(ang_c)] * 2, axis=1)
    sin = np.concatenate([-np.sin(ang_r), np.sin(ang_r), -np.sin(ang_c), np.sin(ang_c)], axis=1)
    reps = DIFF_W // DIFF_QK
    return (jnp.asarray(np.tile(cos, (1, reps)), F32), jnp.asarray(np.tile(sin, (1, reps)), F32))


def _block_diag(s):
    S = s.shape[0]
    eye = jnp.eye(H_RET, dtype=s.dtype)
    return jnp.einsum('shdv,hg->shdgv', s, eye).reshape(S, RET_W, RET_W)


def _router_rows(w_group, b_group, w_expert, b_expert):
    pad = EXPERT_ROW0 - N_GROUPS
    tail = ROUTER_ROWS - EXPERT_ROW0 - N_EXPERTS
    w = jnp.concatenate([w_group.T, jnp.zeros((pad, D_MODEL), F32),
                         w_expert.reshape(D_MODEL, N_EXPERTS).T, jnp.zeros((tail, D_MODEL), F32)], axis=0)
    bias = jnp.concatenate([b_group, jnp.zeros((pad,), F32), b_expert.reshape(N_EXPERTS), jnp.zeros((tail,), F32)])
    hi = w.astype(BF)
    return hi, (w - hi.astype(F32)).astype(BF), bias.reshape(ROUTER_ROWS, 1)


def kernel(x_prompt, x_sample, cache_diff_k, cache_diff_v, state_ret_fwd, state_ret_bwd, c, c_ctx, w_mod, b_mod, w_in, ret_decay_logit, diff_lambda, diff_subln_g, conv_w, conv_b, w_out, ln_g, ln_b, w_router_group, b_router_group, w_router_expert, b_router_expert, w_gate, w_up, w_down):
    B, T_ctx, _ = x_prompt.shape
    Bd, T_lat, _ = x_sample.shape
    assert T_ctx == TILE and T_lat % TILE == 0 and T_lat % GRID_W == 0
    assert 1 + Bd <= MOD_ROWS

    cond = jnp.concatenate([c_ctx[None, :], c, jnp.zeros((MOD_ROWS - 1 - Bd, D_MODEL), F32)], axis=0)
    mod = _modulation(cond, w_mod, b_mod).reshape(DEPTH, MOD_ROWS, 1, 6 * D_MODEL)
    rope_tabs = _rope_tables(T_lat)
    w_in_b = w_in.astype(BF)
    w_out_b = w_out.astype(BF)

    ctx_row = lambda s: 0
    lat_row = lambda s: s + 1
    ctx_tiles = B * (T_ctx // TILE)
    lat_tiles = Bd * (T_lat // TILE)

    yp, ys = x_prompt, x_sample
    stacked = None
    for l in range(DEPTH):
        lam_init = 0.8 - 0.6 * math.exp(-0.3 * l)
        lg_lanes = jnp.repeat(ret_decay_logit[l], RET_DK, axis=1)
        wrh, wrl, br = _router_rows(w_router_group[l], b_router_group[l], w_router_expert[l], b_router_expert[l])
        shared = (w_out_b, lg_lanes, diff_lambda[l], diff_subln_g[l].reshape(1, DIFF_V), conv_w[l],
                  conv_b[l].reshape(1, CONV_CH), ln_g[l, 0].reshape(1, D_MODEL), ln_b[l, 0].reshape(1, D_MODEL),
                  wrh, wrl, br, lam_init)
        ln2 = (ln_g[l, 1].reshape(1, D_MODEL), ln_b[l, 1].reshape(1, D_MODEL))

        pr = _proj(yp, mod, l, ctx_row, w_in_b, lg_lanes, stacked=stacked)
        rq, rk, rv, rg, dq, dk, dv, cb, cch = pr[:9]
        stacked = pr[9:]
        x1_c, hs_c, slots_c, cnt_c = _mixer(yp, mod, l, ctx_row, (rq, rk, rv, rg, dq, cb, cch), None, None,
                                            (dk, dv), *shared)

        pr = _proj(ys, mod, l, lat_row, w_in_b, lg_lanes, rope_tabs=rope_tabs)
        rq, rk, rv, rg, dq, dk, dv, cb, cch, kvf, kvb = pr
        states = _scan(kvf, kvb, _block_diag(state_ret_fwd[:, l]), _block_diag(state_ret_bwd[:, l]), lg_lanes)
        cache_v = cache_diff_v[:, l].astype(BF)
        cache_kv = (cache_diff_k[:, l].reshape(Bd, -1, DIFF_W).astype(BF),
                    jnp.concatenate([cache_v, jnp.ones_like(cache_v)], axis=-1).reshape(Bd, -1, V_AUG_W))
        x1_l, hs_l, slots_l, cnt_l = _mixer(ys, mod, l, lat_row, (rq, rk, rv, rg, dq, cb, cch), states, cache_kv,
                                            (dk, dv), *shared)

        padded = jnp.concatenate([cnt_c.reshape(ctx_tiles, N_EXPERTS, LANES)[:, :, 0],
                                  cnt_l.reshape(lat_tiles, N_EXPERTS, LANES)[:, :, 0]], axis=0)
        eb, nv, cid = _block_schedule(padded, ctx_tiles * CHUNKS_PER_TILE)
        out_c, out_l = _moe(hs_c, hs_l, eb, nv, cid, l, w_gate, w_up, w_down)
        yp = _combine(out_c, slots_c, x1_c, mod, l, ctx_row, *ln2)
        ys = _combine(out_l, slots_l, x1_l, mod, l, lat_row, *ln2)

    new_k, new_v, st_f, st_b = stacked
    return (yp, ys, new_k.reshape(B, DEPTH, T_ctx, H_DIFF, 2, DIFF_QK),
            new_v.reshape(B, DEPTH, T_ctx, H_DIFF, DIFF_V), st_f, st_b)
```
